```python
import jax, jax.numpy as jnp
from jax import lax
import numpy as np

D_MODEL = 1024
BATCH = 32
SEQ = 2048
DEPTH = 1

FOX_HEADS = 8
FOX_HEAD_DIM = 64
FOX_WIDTH = FOX_HEADS * FOX_HEAD_DIM
MLA_HEADS = 8
MLA_NOPE_DIM = 64
MLA_ROPE_DIM = 32
MLA_QK_DIM = MLA_NOPE_DIM + MLA_ROPE_DIM
MLA_V_DIM = 64
MLA_Q_RANK = 256
MLA_KV_RANK = 128
MLA_WIDTH = MLA_HEADS * MLA_V_DIM
MIX_WIDTH = FOX_WIDTH + MLA_WIDTH
IN_SPLITS = (FOX_WIDTH, FOX_WIDTH, FOX_WIDTH, FOX_HEADS, MLA_Q_RANK, MLA_KV_RANK, MLA_ROPE_DIM)
IN_WIDTH = sum(IN_SPLITS)
D_FF = -(-8 * D_MODEL // (3 * 256)) * 256

Q_BLOCK = 128
ROPE_THETA = 10000.0
NORM_EPS = 1e-6

kernel_name = "hymba_fox_mla_hybrid_layer"


def rmsnorm(x, g):
    xf = x.astype(jnp.float32)
    y = xf * lax.rsqrt(jnp.mean(xf * xf, axis=-1, keepdims=True) + NORM_EPS)
    return (y * g.astype(jnp.float32)).astype(x.dtype)


def rope_tables(positions, dim):
    inv_freq = ROPE_THETA ** (-jnp.arange(0, dim, 2, dtype=jnp.float32) / dim)
    ang = positions.astype(jnp.float32)[:, :, None] * inv_freq[None, None, :]
    return jnp.cos(ang)[:, None], jnp.sin(ang)[:, None]


def apply_rope(x, cos, sin):
    xf = x.astype(jnp.float32)
    x1, x2 = jnp.split(xf, 2, axis=-1)
    return jnp.concatenate([x1 * cos - x2 * sin, x2 * cos + x1 * sin], axis=-1).astype(x.dtype)


def causal_block_attention(q, k, v, scale, log_decay=None):
    seq = q.shape[2]
    outs = []
    for blk in range(seq // Q_BLOCK):
        q0 = blk * Q_BLOCK
        q1 = q0 + Q_BLOCK
        qb = q[:, :, q0:q1]
        kb = k[:, :, :q1]
        vb = v[:, :, :q1]
        logits = jnp.einsum('bhqd,bhkd->bhqk', qb, kb,
                            preferred_element_type=jnp.float32) * scale
        if log_decay is not None:
            ld = log_decay.astype(jnp.float32)
            logits = logits + (ld[:, :, q0:q1, None] - ld[:, :, None, :q1])
        q_pos = q0 + jnp.arange(Q_BLOCK)
        k_pos = jnp.arange(q1)
        mask = k_pos[None, :] <= q_pos[:, None]
        logits = jnp.where(mask, logits, -jnp.inf)
        p = jax.nn.softmax(logits, axis=-1)
        outs.append(jnp.einsum('bhqk,bhkd->bhqd', p.astype(vb.dtype), vb))
    return jnp.concatenate(outs, axis=2)


def to_heads(t, n_heads):
    b, s, _ = t.shape
    return t.reshape(b, s, n_heads, -1).transpose(0, 2, 1, 3)


def from_heads(t):
    b, h, s, d = t.shape
    return t.transpose(0, 2, 1, 3).reshape(b, s, h * d)


def hybrid_mixer(h, cos, sin, w_in, b_fgate, q_norm_g, w_uq, kv_norm_g, w_ukv,
                 fox_out_g, mla_out_g, w_o):
    b, s, _ = h.shape
    proj = jnp.einsum('bsd,de->bse', h, w_in)
    cuts = list(np.cumsum(IN_SPLITS)[:-1])
    fq, fk, fv, f_logit, q_lat, kv_lat, k_rope = jnp.split(proj, cuts, axis=-1)

    log_f = jax.nn.log_sigmoid((f_logit + b_fgate).astype(jnp.float32))
    c = jnp.cumsum(log_f, axis=1).transpose(0, 2, 1)
    fox_o = causal_block_attention(to_heads(fq, FOX_HEADS), to_heads(fk, FOX_HEADS),
                                   to_heads(fv, FOX_HEADS), FOX_HEAD_DIM ** -0.5, c)
    fox_o = rmsnorm(from_heads(fox_o), fox_out_g)

    q = jnp.einsum('bsr,re->bse', rmsnorm(q_lat, q_norm_g), w_uq)
    q = to_heads(q, MLA_HEADS)
    q_nope, q_pe = q[..., :MLA_NOPE_DIM], q[..., MLA_NOPE_DIM:]
    q = jnp.concatenate([q_nope, apply_rope(q_pe, cos, sin)], axis=-1)
    kv = jnp.einsum('bsr,re->bse', rmsnorm(kv_lat, kv_norm_g), w_ukv)
    kv = to_heads(kv, MLA_HEADS)
    k_nope, v = kv[..., :MLA_NOPE_DIM], kv[..., MLA_NOPE_DIM:]
    k_pe = apply_rope(k_rope[:, None], cos, sin)
    k = jnp.concatenate([k_nope, jnp.broadcast_to(k_pe, (b, MLA_HEADS, s, MLA_ROPE_DIM))], axis=-1)
    mla_o = causal_block_attention(q, k, v, MLA_QK_DIM ** -0.5)
    mla_o = rmsnorm(from_heads(mla_o), mla_out_g)

    return jnp.einsum('bse,ed->bsd', jnp.concatenate([fox_o, mla_o], axis=-1), w_o)


def swiglu(h, w_gate, w_up, w_down):
    g = jnp.einsum('bsd,df->bsf', h, w_gate)
    u = jnp.einsum('bsd,df->bsf', h, w_up)
    return jnp.einsum('bsf,fd->bsd', jax.nn.silu(g) * u, w_down)


def _fwd_setup_inputs(seed: int = 0) -> dict:
    key = jax.random.key(seed)
    ks = jax.random.split(key, 20)

    def w(k, shape, fan_in):
        return jax.random.normal(k, shape, jnp.float32) * fan_in ** -0.5

    def gain(k, shape):
        return 1.0 + 0.05 * jax.random.normal(k, shape, jnp.float32)

    x = jax.random.normal(ks[0], (BATCH, SEQ, D_MODEL), jnp.float32)
    offsets = jax.random.randint(ks[1], (BATCH, 1), 0, 4096, dtype=jnp.int32)
    positions = offsets + jnp.arange(SEQ, dtype=jnp.int32)[None, :]
    return {
        "x": x,
        "positions": positions,
        "norm_mix_g": gain(ks[2], (DEPTH, D_MODEL)),
        "w_in": w(ks[3], (DEPTH, D_MODEL, IN_WIDTH), D_MODEL),
        "b_fgate": 1.0 + 3.0 * jax.random.uniform(ks[4], (DEPTH, FOX_HEADS), jnp.float32),
        "q_norm_g": gain(ks[5], (DEPTH, MLA_Q_RANK)),
        "w_uq": w(ks[6], (DEPTH, MLA_Q_RANK, MLA_HEADS * MLA_QK_DIM), MLA_Q_RANK),
        "kv_norm_g": gain(ks[7], (DEPTH, MLA_KV_RANK)),
        "w_ukv": w(ks[8], (DEPTH, MLA_KV_RANK, MLA_HEADS * (MLA_NOPE_DIM + MLA_V_DIM)), MLA_KV_RANK),
        "fox_out_g": gain(ks[9], (DEPTH, FOX_WIDTH)),
        "mla_out_g": gain(ks[10], (DEPTH, MLA_WIDTH)),
        "w_o": w(ks[11], (DEPTH, MIX_WIDTH, D_MODEL), MIX_WIDTH),
        "norm_ffn_g": gain(ks[12], (DEPTH, D_MODEL)),
        "w_gate": w(ks[13], (DEPTH, D_MODEL, D_FF), D_MODEL),
        "w_up": w(ks[14], (DEPTH, D_MODEL, D_FF), D_MODEL),
        "w_down": w(ks[15], (DEPTH, D_FF, D_MODEL), D_FF),
        "final_norm_g": gain(ks[16], (D_MODEL,)),
    }


def _fwd_reference(x, positions, norm_mix_g, w_in, b_fgate, q_norm_g, w_uq, kv_norm_g, w_ukv,
              fox_out_g, mla_out_g, w_o, norm_ffn_g, w_gate, w_up, w_down, final_norm_g):
    cos, sin = rope_tables(positions, MLA_ROPE_DIM)
    for l in range(DEPTH):
        h = rmsnorm(x, norm_mix_g[l])
        x = x + hybrid_mixer(h, cos, sin, w_in[l], b_fgate[l], q_norm_g[l], w_uq[l],
                             kv_norm_g[l], w_ukv[l], fox_out_g[l], mla_out_g[l], w_o[l])
        h = rmsnorm(x, norm_ffn_g[l])
        x = x + swiglu(h, w_gate[l], w_up[l], w_down[l])
    return rmsnorm(x, final_norm_g)


import jax as _jax
import jax.numpy as _jnp

TWIN_FORMAT = 'train_step'
FWD_PARAMS = ['x', 'positions', 'norm_mix_g', 'w_in', 'b_fgate', 'q_norm_g', 'w_uq', 'kv_norm_g', 'w_ukv', 'fox_out_g', 'mla_out_g', 'w_o', 'norm_ffn_g', 'w_gate', 'w_up', 'w_down', 'final_norm_g']
TWIN_WEIGHTS = ['norm_mix_g', 'w_in', 'b_fgate', 'q_norm_g', 'w_uq', 'kv_norm_g', 'w_ukv', 'fox_out_g', 'mla_out_g', 'w_o', 'norm_ffn_g', 'w_gate', 'w_up', 'w_down', 'final_norm_g']
TWIN_DIFF_INPUT = 'x'
TWIN_INPUTS = ['x', 'positions', 'norm_mix_g', 'w_in', 'b_fgate', 'q_norm_g', 'w_uq', 'kv_norm_g', 'w_ukv', 'fox_out_g', 'mla_out_g', 'w_o', 'norm_ffn_g', 'w_gate', 'w_up', 'w_down', 'final_norm_g', 'loss_target', 'm_norm_mix_g', 'm_w_in', 'm_b_fgate', 'm_q_norm_g', 'm_w_uq', 'm_kv_norm_g', 'm_w_ukv', 'm_fox_out_g', 'm_mla_out_g', 'm_w_o', 'm_norm_ffn_g', 'm_w_gate', 'm_w_up', 'm_w_down', 'm_final_norm_g', 'v_norm_mix_g', 'v_w_in', 'v_b_fgate', 'v_q_norm_g', 'v_w_uq', 'v_kv_norm_g', 'v_w_ukv', 'v_fox_out_g', 'v_mla_out_g', 'v_w_o', 'v_norm_ffn_g', 'v_w_gate', 'v_w_up', 'v_w_down', 'v_final_norm_g']
TWIN_OUTPUTS = ['loss', 'grad_x', 'grad_norm_mix_g', 'grad_w_in', 'grad_b_fgate', 'grad_q_norm_g', 'grad_w_uq', 'grad_kv_norm_g', 'grad_w_ukv', 'grad_fox_out_g', 'grad_mla_out_g', 'grad_w_o', 'grad_norm_ffn_g', 'grad_w_gate', 'grad_w_up', 'grad_w_down', 'grad_final_norm_g', 'delta_norm_mix_g', 'delta_w_in', 'delta_b_fgate', 'delta_q_norm_g', 'delta_w_uq', 'delta_kv_norm_g', 'delta_w_ukv', 'delta_fox_out_g', 'delta_mla_out_g', 'delta_w_o', 'delta_norm_ffn_g', 'delta_w_gate', 'delta_w_up', 'delta_w_down', 'delta_final_norm_g', 'new_m_norm_mix_g', 'new_m_w_in', 'new_m_b_fgate', 'new_m_q_norm_g', 'new_m_w_uq', 'new_m_kv_norm_g', 'new_m_w_ukv', 'new_m_fox_out_g', 'new_m_mla_out_g', 'new_m_w_o', 'new_m_norm_ffn_g', 'new_m_w_gate', 'new_m_w_up', 'new_m_w_down', 'new_m_final_norm_g', 'new_v_norm_mix_g', 'new_v_w_in', 'new_v_b_fgate', 'new_v_q_norm_g', 'new_v_w_uq', 'new_v_kv_norm_g', 'new_v_w_ukv', 'new_v_fox_out_g', 'new_v_mla_out_g', 'new_v_w_o', 'new_v_norm_ffn_g', 'new_v_w_gate', 'new_v_w_up', 'new_v_w_down', 'new_v_final_norm_g']
TWIN_LEAF_KINDS = {'loss': 'loss', 'grad_x': 'grad_x', 'grad_norm_mix_g': 'grad_w', 'grad_w_in': 'grad_w', 'grad_b_fgate': 'grad_w', 'grad_q_norm_g': 'grad_w', 'grad_w_uq': 'grad_w', 'grad_kv_norm_g': 'grad_w', 'grad_w_ukv': 'grad_w', 'grad_fox_out_g': 'grad_w', 'grad_mla_out_g': 'grad_w', 'grad_w_o': 'grad_w', 'grad_norm_ffn_g': 'grad_w', 'grad_w_gate': 'grad_w', 'grad_w_up': 'grad_w', 'grad_w_down': 'grad_w', 'grad_final_norm_g': 'grad_w', 'delta_norm_mix_g': 'delta_w', 'delta_w_in': 'delta_w', 'delta_b_fgate': 'delta_w', 'delta_q_norm_g': 'delta_w', 'delta_w_uq': 'delta_w', 'delta_kv_norm_g': 'delta_w', 'delta_w_ukv': 'delta_w', 'delta_fox_out_g': 'delta_w', 'delta_mla_out_g': 'delta_w', 'delta_w_o': 'delta_w', 'delta_norm_ffn_g': 'delta_w', 'delta_w_gate': 'delta_w', 'delta_w_up': 'delta_w', 'delta_w_down': 'delta_w', 'delta_final_norm_g': 'delta_w', 'new_m_norm_mix_g': 'new_m', 'new_m_w_in': 'new_m', 'new_m_b_fgate': 'new_m', 'new_m_q_norm_g': 'new_m', 'new_m_w_uq': 'new_m', 'new_m_kv_norm_g': 'new_m', 'new_m_w_ukv': 'new_m', 'new_m_fox_out_g': 'new_m', 'new_m_mla_out_g': 'new_m', 'new_m_w_o': 'new_m', 'new_m_norm_ffn_g': 'new_m', 'new_m_w_gate': 'new_m', 'new_m_w_up': 'new_m', 'new_m_w_down': 'new_m', 'new_m_final_norm_g': 'new_m', 'new_v_norm_mix_g': 'new_v', 'new_v_w_in': 'new_v', 'new_v_b_fgate': 'new_v', 'new_v_q_norm_g': 'new_v', 'new_v_w_uq': 'new_v', 'new_v_kv_norm_g': 'new_v', 'new_v_w_ukv': 'new_v', 'new_v_fox_out_g': 'new_v', 'new_v_mla_out_g': 'new_v', 'new_v_w_o': 'new_v', 'new_v_norm_ffn_g': 'new_v', 'new_v_w_gate': 'new_v', 'new_v_w_up': 'new_v', 'new_v_w_down': 'new_v', 'new_v_final_norm_g': 'new_v'}


def _forward(args):
    return _fwd_reference(*[args[k] for k in FWD_PARAMS])


def _output_shape():
    out = _jax.eval_shape(lambda: _forward(_fwd_setup_inputs(0)))
    return out.shape, out.dtype

N_MICROBATCH = 1
ADAM_LR = 0.001
ADAM_B1 = 0.9
ADAM_B2 = 0.999
ADAM_EPS = 1e-08
ADAM_WD = 0.01
ADAM_STEP = 10
PER_EXAMPLE_BATCH_AXIS = {'x': 0, 'positions': 0, 'loss_target': 0}
SHARED_INPUTS = []
_WEIGHT_DTYPES = {'norm_mix_g': _jnp.float32, 'w_in': _jnp.float32, 'b_fgate': _jnp.float32, 'q_norm_g': _jnp.float32, 'w_uq': _jnp.float32, 'kv_norm_g': _jnp.float32, 'w_ukv': _jnp.float32, 'fox_out_g': _jnp.float32, 'mla_out_g': _jnp.float32, 'w_o': _jnp.float32, 'norm_ffn_g': _jnp.float32, 'w_gate': _jnp.float32, 'w_up': _jnp.float32, 'w_down': _jnp.float32, 'final_norm_g': _jnp.float32}
MOMENT_SCALE = {'norm_mix_g': 2.969978e-01, 'w_in': 2.120553e-01, 'b_fgate': 2.226263e+00, 'q_norm_g': 3.032844e-01, 'w_uq': 1.722382e-01, 'kv_norm_g': 9.581051e-01, 'w_ukv': 2.030980e-01, 'fox_out_g': 2.141829e-01, 'mla_out_g': 2.048575e-01, 'w_o': 2.035940e-01, 'norm_ffn_g': 1.403725e-01, 'w_gate': 6.151079e-02, 'w_up': 6.029311e-02, 'w_down': 9.911374e-02, 'final_norm_g': 6.396185e+01}


def _to_microbatches(a, axis):
    t = _jnp.moveaxis(a, axis, 0)
    t = t.reshape((N_MICROBATCH, t.shape[0] // N_MICROBATCH) + t.shape[1:])
    return _jnp.moveaxis(t, 1, axis + 1)


def setup_inputs(seed: int = 0) -> dict:
    inp = _fwd_setup_inputs(seed)
    key = _jax.random.fold_in(_jax.random.key(seed), 7919)
    shape, _ = _output_shape()
    out = dict(inp)
    out["loss_target"] = _jax.random.normal(_jax.random.fold_in(key, 0), shape, _jnp.float32)
    for i, name in enumerate(TWIN_WEIGHTS):
        w = inp[name].astype(_jnp.float32)
        if MOMENT_SCALE is None:
            s = _jnp.sqrt(_jnp.mean(_jnp.square(w)) + 1e-30)
        else:
            s = MOMENT_SCALE[name]
        km, kv = _jax.random.split(_jax.random.fold_in(key, i + 1))
        out[name] = w
        out["m_" + name] = s * _jax.random.normal(km, w.shape, _jnp.float32)
        out["v_" + name] = (s * s) * _jax.random.uniform(kv, w.shape, _jnp.float32, 0.5, 1.5)
    if N_MICROBATCH > 1:
        for name, axis in PER_EXAMPLE_BATCH_AXIS.items():
            out[name] = _to_microbatches(out[name], axis)
    return {'x': out['x'], 'positions': out['positions'], 'norm_mix_g': out['norm_mix_g'], 'w_in': out['w_in'], 'b_fgate': out['b_fgate'], 'q_norm_g': out['q_norm_g'], 'w_uq': out['w_uq'], 'kv_norm_g': out['kv_norm_g'], 'w_ukv': out['w_ukv'], 'fox_out_g': out['fox_out_g'], 'mla_out_g': out['mla_out_g'], 'w_o': out['w_o'], 'norm_ffn_g': out['norm_ffn_g'], 'w_gate': out['w_gate'], 'w_up': out['w_up'], 'w_down': out['w_down'], 'final_norm_g': out['final_norm_g'], 'loss_target': out['loss_target'], 'm_norm_mix_g': out['m_norm_mix_g'], 'm_w_in': out['m_w_in'], 'm_b_fgate': out['m_b_fgate'], 'm_q_norm_g': out['m_q_norm_g'], 'm_w_uq': out['m_w_uq'], 'm_kv_norm_g': out['m_kv_norm_g'], 'm_w_ukv': out['m_w_ukv'], 'm_fox_out_g': out['m_fox_out_g'], 'm_mla_out_g': out['m_mla_out_g'], 'm_w_o': out['m_w_o'], 'm_norm_ffn_g': out['m_norm_ffn_g'], 'm_w_gate': out['m_w_gate'], 'm_w_up': out['m_w_up'], 'm_w_down': out['m_w_down'], 'm_final_norm_g': out['m_final_norm_g'], 'v_norm_mix_g': out['v_norm_mix_g'], 'v_w_in': out['v_w_in'], 'v_b_fgate': out['v_b_fgate'], 'v_q_norm_g': out['v_q_norm_g'], 'v_w_uq': out['v_w_uq'], 'v_kv_norm_g': out['v_kv_norm_g'], 'v_w_ukv': out['v_w_ukv'], 'v_fox_out_g': out['v_fox_out_g'], 'v_mla_out_g': out['v_mla_out_g'], 'v_w_o': out['v_w_o'], 'v_norm_ffn_g': out['v_norm_ffn_g'], 'v_w_gate': out['v_w_gate'], 'v_w_up': out['v_w_up'], 'v_w_down': out['v_w_down'], 'v_final_norm_g': out['v_final_norm_g']}


def _loss(weights, diff, rest, loss_target):
    with _jax.named_scope("forward"):
        args = {**rest, TWIN_DIFF_INPUT: diff, **{k: w.astype(_WEIGHT_DTYPES[k]) for k, w in weights.items()}}
        y = _forward(args)
    with _jax.named_scope("loss_head"):
        err = _jnp.square(y.astype(_jnp.float32) - loss_target)
        return 0.5 * _jnp.sum(_jnp.mean(err, axis=-1)) if err.ndim else 0.5 * err


def _adamw(w, g, m, v):
    m = ADAM_B1 * m + (1.0 - ADAM_B1) * g
    v = ADAM_B2 * v + (1.0 - ADAM_B2) * _jnp.square(g)
    m_hat = m / (1.0 - ADAM_B1 ** ADAM_STEP)
    v_hat = v / (1.0 - ADAM_B2 ** ADAM_STEP)
    delta = -ADAM_LR * (m_hat / (_jnp.sqrt(v_hat) + ADAM_EPS) + ADAM_WD * w)
    return delta, m, v


def reference(x, positions, norm_mix_g, w_in, b_fgate, q_norm_g, w_uq, kv_norm_g, w_ukv, fox_out_g, mla_out_g, w_o, norm_ffn_g, w_gate, w_up, w_down, final_norm_g, loss_target, m_norm_mix_g, m_w_in, m_b_fgate, m_q_norm_g, m_w_uq, m_kv_norm_g, m_w_ukv, m_fox_out_g, m_mla_out_g, m_w_o, m_norm_ffn_g, m_w_gate, m_w_up, m_w_down, m_final_norm_g, v_norm_mix_g, v_w_in, v_b_fgate, v_q_norm_g, v_w_uq, v_kv_norm_g, v_w_ukv, v_fox_out_g, v_mla_out_g, v_w_o, v_norm_ffn_g, v_w_gate, v_w_up, v_w_down, v_final_norm_g):
    given = dict(x=x, positions=positions, norm_mix_g=norm_mix_g, w_in=w_in, b_fgate=b_fgate, q_norm_g=q_norm_g, w_uq=w_uq, kv_norm_g=kv_norm_g, w_ukv=w_ukv, fox_out_g=fox_out_g, mla_out_g=mla_out_g, w_o=w_o, norm_ffn_g=norm_ffn_g, w_gate=w_gate, w_up=w_up, w_down=w_down, final_norm_g=final_norm_g, loss_target=loss_target, m_norm_mix_g=m_norm_mix_g, m_w_in=m_w_in, m_b_fgate=m_b_fgate, m_q_norm_g=m_q_norm_g, m_w_uq=m_w_uq, m_kv_norm_g=m_kv_norm_g, m_w_ukv=m_w_ukv, m_fox_out_g=m_fox_out_g, m_mla_out_g=m_mla_out_g, m_w_o=m_w_o, m_norm_ffn_g=m_norm_ffn_g, m_w_gate=m_w_gate, m_w_up=m_w_up, m_w_down=m_w_down, m_final_norm_g=m_final_norm_g, v_norm_mix_g=v_norm_mix_g, v_w_in=v_w_in, v_b_fgate=v_b_fgate, v_q_norm_g=v_q_norm_g, v_w_uq=v_w_uq, v_kv_norm_g=v_kv_norm_g, v_w_ukv=v_w_ukv, v_fox_out_g=v_fox_out_g, v_mla_out_g=v_mla_out_g, v_w_o=v_w_o, v_norm_ffn_g=v_norm_ffn_g, v_w_gate=v_w_gate, v_w_up=v_w_up, v_w_down=v_w_down, v_final_norm_g=v_final_norm_g)
    weights = {n: given[n] for n in TWIN_WEIGHTS}
    shared = {n: given[n] for n in SHARED_INPUTS}
    per_example = {n: given[n] for n in ['x', 'positions']}
    grad_fn = _jax.value_and_grad(_loss, argnums=(0, 1))

    def one_microbatch(ex, loss_target):
        ex = dict(ex)
        diff = ex.pop(TWIN_DIFF_INPUT)
        return grad_fn(weights, diff, {**shared, **ex}, loss_target)

    if N_MICROBATCH == 1:
        loss, (grad_w, grad_x) = one_microbatch(per_example, given["loss_target"])
    else:
        def body(carry, xs):
            loss_sum, grad_sum = carry
            l_k, (gw_k, gx_k) = one_microbatch(xs[0], xs[1])
            with _jax.named_scope("update"):
                return (loss_sum + l_k, _jax.tree.map(_jnp.add, grad_sum, gw_k)), gx_k

        init = (_jnp.zeros((), _jnp.float32), _jax.tree.map(_jnp.zeros_like, weights))
        (loss, grad_w), grad_x = _jax.lax.scan(body, init, (per_example, given["loss_target"]))
    with _jax.named_scope("update"):
        delta_w, new_m, new_v = {}, {}, {}
        for n in TWIN_WEIGHTS:
            delta_w[n], new_m[n], new_v[n] = _adamw(weights[n], grad_w[n], given["m_" + n], given["v_" + n])
    return (loss, grad_x, *[grad_w[n] for n in TWIN_WEIGHTS], *[delta_w[n] for n in TWIN_WEIGHTS],
            *[new_m[n] for n in TWIN_WEIGHTS], *[new_v[n] for n in TWIN_WEIGHTS])
```

```python
import functools

import numpy as np
import jax
import jax.numpy as jnp
from jax import lax
from jax.experimental import pallas as pl
from jax.experimental.pallas import tpu as pltpu

F32 = jnp.float32
BF16 = jnp.bfloat16
MESH = pl.DeviceIdType.MESH

EPS = 1e-6
D_MODEL = 1024
HEADS = 8
PAIRS = HEADS // 2
FOX_W = 512
Q_RANK = 256
KV_RANK = 128
ROPE = 32
D_FF = 2816
N_CHIPS = 4
FOX_SCALE = 64 ** -0.5
MLA_SCALE = 96 ** -0.5
LANES = 128
NEG = -1e30

ADAM_LR, ADAM_B1, ADAM_B2, ADAM_EPS, ADAM_WD, ADAM_STEP = 0.001, 0.9, 0.999, 1e-08, 0.01, 10

C_FQ, C_FK, C_FV, C_QL, C_KVL, C_MA, C_MB, C_END = 0, 512, 1024, 1536, 1792, 1920, 2048, 2176

VMEM_LIMIT = 60 * 1024 * 1024

BIG = ("w_in", "w_uq", "w_ukv", "w_o", "w_gate", "w_up", "w_down")
COL_SHARDED = ("w_in", "w_uq", "w_ukv", "w_gate", "w_up")
SMALL = ("norm_mix_g", "b_fgate", "q_norm_g", "kv_norm_g", "fox_out_g", "mla_out_g", "norm_ffn_g", "final_norm_g")
FULL_SHAPES = {"w_in": (1024, 1960), "w_uq": (256, 768), "w_ukv": (128, 1024), "w_o": (1024, 1024),
               "w_gate": (1024, 2816), "w_up": (1024, 2816), "w_down": (2816, 1024)}
PACK_ELEMS = sum(a * b for a, b in FULL_SHAPES.values()) // N_CHIPS
PACK_ROWS = -(-PACK_ELEMS // (1024 * 32)) * 32
HALF_ROWS = PACK_ROWS // 2


def _params(sem=None):
    return pltpu.CompilerParams(dimension_semantics=sem, vmem_limit_bytes=VMEM_LIMIT)


def _full(shape):
    n = len(shape)
    return pl.BlockSpec(shape, lambda *_: (0,) * n, pipeline_mode=pl.Buffered(1))


def _dot(a, b):
    return jnp.dot(a, b, preferred_element_type=F32)


def _dot_nt(a, b):
    return lax.dot_general(a, b, (((1,), (1,)), ((), ())), preferred_element_type=F32)


def _dot_tn(a, b):
    return lax.dot_general(a, b, (((0,), (0,)), ((), ())), preferred_element_type=F32)


def _split3(v):
    hi = v.astype(BF16)
    r1 = v - hi.astype(F32)
    mid = r1.astype(BF16)
    lo = (r1 - mid.astype(F32)).astype(BF16)
    return hi, mid, lo


def _rms(v, width):
    return lax.rsqrt(jnp.sum(v * v, axis=1, keepdims=True) * (1.0 / width) + EPS)


def _rms_bwd(dy, xhat, r, g, width):
    u = dy * g
    return r * (u - xhat * (jnp.sum(u * xhat, axis=1, keepdims=True) * (1.0 / width)))


def _in_proj(x, g1, w_in, w_q12, w_k, w_v, gq, gkv, bfg, ct, st, sel, seq, tm):
    T = x.shape[0]
    nsb = seq // tm

    def body(x_ref, g1_ref, win_ref, wq_ref, wk_ref, wv_ref, gq_ref, gkv_ref, b_ref, ct_ref, st_ref, sel_ref,
             h1_ref, qf_ref, kf_ref, vf_ref, qm_ref, km_ref, vm_ref, lat_ref, qn_ref, kvn_ref, carry):
        i = pl.program_id(0)

        @pl.when(i % nsb == 0)
        def _():
            carry[...] = jnp.zeros_like(carry)

        xv = x_ref[...]
        h = (xv * _rms(xv, D_MODEL) * g1_ref[...]).astype(BF16)
        h1_ref[...] = h
        proj = _dot(h, win_ref[...])
        lane = lax.broadcasted_iota(jnp.int32, (tm, LANES), 1)
        low = lane < 64
        misc_a = proj[:, C_MA:C_MB]
        misc_b = proj[:, C_MB:C_END]

        z = misc_a + b_ref[...]
        lf = jnp.where(lane < HEADS, jnp.minimum(z, 0.0) - jnp.log1p(jnp.exp(-jnp.abs(z))), 0.0)
        rr = lax.broadcasted_iota(jnp.int32, (tm, tm), 0)
        cc = lax.broadcasted_iota(jnp.int32, (tm, tm), 1)
        tri = (rr >= cc).astype(BF16)
        a0, a1, a2 = _split3(lf)
        c = _dot(tri, a0) + _dot(tri, a1) + _dot(tri, a2) + carry[0:1, :]
        carry[0:1, :] = c[tm - 1:tm, :]
        c0, c1, c2 = _split3(c)
        cpl = _dot(jnp.concatenate([c0, c1, c2], axis=1), sel_ref[...])
        qpad = jnp.where((lane >= 64) & (lane < 67), -1.0, 0.0)
        for j in range(PAIRS):
            qc = proj[:, C_FQ + LANES * j:C_FQ + LANES * (j + 1)] * FOX_SCALE
            kc = proj[:, C_FK + LANES * j:C_FK + LANES * (j + 1)]
            e, o = 2 * LANES * j, 2 * LANES * j + LANES
            qf_ref[:, e:e + LANES] = jnp.where(low, qc, qpad).astype(BF16)
            qf_ref[:, o:o + LANES] = jnp.where(low, pltpu.roll(qc, 64, 1), qpad).astype(BF16)
            kf_ref[:, e:e + LANES] = jnp.where(low, kc, cpl[:, e:e + LANES]).astype(BF16)
            kf_ref[:, o:o + LANES] = jnp.where(low, pltpu.roll(kc, 64, 1), cpl[:, o:o + LANES]).astype(BF16)
        vf_ref[...] = proj[:, C_FV:C_QL].astype(BF16)

        ql = proj[:, C_QL:C_KVL]
        kvl = proj[:, C_KVL:C_MA]
        qn = (ql * _rms(ql, Q_RANK) * gq_ref[...]).astype(BF16)
        kvn = (kvl * _rms(kvl, KV_RANK) * gkv_ref[...]).astype(BF16)
        lat_ref[...] = proj[:, C_QL:C_MB]
        qn_ref[...] = qn
        kvn_ref[...] = kvn
        q12 = _dot(qn, wq_ref[...])
        kn = _dot(kvn, wk_ref[...])
        ctv = ct_ref[...]
        stv = st_ref[...]
        cq = (jnp.where(low, 1.0, 0.0) + ctv) * MLA_SCALE
        sq = stv * MLA_SCALE
        kpe = misc_a * ctv + misc_b * stv
        for hd in range(HEADS):
            s0 = LANES * hd
            qm_ref[:, s0:s0 + LANES] = (q12[:, s0:s0 + LANES] * cq + q12[:, 1024 + s0:1024 + s0 + LANES] * sq).astype(BF16)
            km_ref[:, s0:s0 + LANES] = (kn[:, s0:s0 + LANES] + kpe).astype(BF16)
        vm_ref[...] = _dot(kvn, wv_ref[...]).astype(BF16)

    row = lambda w: pl.BlockSpec((tm, w), lambda i: (i, 0))
    out_shape = (
        jax.ShapeDtypeStruct((T, D_MODEL), BF16),
        jax.ShapeDtypeStruct((T, 1024), BF16), jax.ShapeDtypeStruct((T, 1024), BF16), jax.ShapeDtypeStruct((T, 512), BF16),
        jax.ShapeDtypeStruct((T, 1024), BF16), jax.ShapeDtypeStruct((T, 1024), BF16), jax.ShapeDtypeStruct((T, 512), BF16),
        jax.ShapeDtypeStruct((T, 512), F32),
        jax.ShapeDtypeStruct((T, Q_RANK), BF16), jax.ShapeDtypeStruct((T, KV_RANK), BF16),
    )
    return pl.pallas_call(
        body, name="in_proj", grid=(T // tm,), out_shape=out_shape,
        in_specs=[row(D_MODEL), _full(g1.shape), _full(w_in.shape), _full(w_q12.shape), _full(w_k.shape), _full(w_v.shape),
                  _full(gq.shape), _full(gkv.shape), _full(bfg.shape), row(LANES), row(LANES), _full(sel.shape)],
        out_specs=[row(D_MODEL), row(1024), row(1024), row(512), row(1024), row(1024), row(512), row(512), row(Q_RANK), row(KV_RANK)],
        scratch_shapes=[pltpu.VMEM((8, LANES), F32)],
        compiler_params=_params(("arbitrary",)),
    )(x, g1, w_in, w_q12, w_k, w_v, gq, gkv, bfg, ct, st, sel)


def _attn_fwd(q, k, v, nb, seq, tq, name):
    T = q.shape[0]
    nq = seq // tq

    def body(q_ref, k_ref, v_ref, o_ref, lse_ref):
        qi = pl.program_id(2)
        lane = lax.broadcasted_iota(jnp.int32, (tq, LANES), 1)
        rr = lax.broadcasted_iota(jnp.int32, (tq, tq), 0)
        cc = lax.broadcasted_iota(jnp.int32, (tq, tq), 1)
        outs = []
        for hh in range(2):
            qv = q_ref[:, LANES * hh:LANES * (hh + 1)]

            def step(kj, carry, masked, hh=hh, qv=qv):
                m, l, acc = carry
                rows = pl.ds(pl.multiple_of(kj * tq, tq), tq)
                s = _dot_nt(qv, k_ref[rows, LANES * hh:LANES * (hh + 1)])
                if masked:
                    s = jnp.where(cc <= rr, s, NEG)
                m_new = jnp.maximum(m, jnp.max(s, axis=1, keepdims=True))
                alpha = jnp.exp(m - m_new)
                p = jnp.exp(s - m_new)
                l = alpha * l + jnp.sum(p, axis=1, keepdims=True)
                acc = alpha * acc + _dot(p.astype(BF16), v_ref[rows, :])
                return m_new, l, acc

            init = (jnp.full((tq, 1), NEG, F32), jnp.zeros((tq, 1), F32), jnp.zeros((tq, LANES), F32))
            carry = lax.fori_loop(0, qi, functools.partial(step, masked=False), init)
            m, l, acc = step(qi, carry, True)
            outs.append(acc / l)
            lse_ref[:, LANES * hh:LANES * (hh + 1)] = jnp.broadcast_to(m + jnp.log(l), (tq, LANES))
        o_ref[...] = jnp.where(lane < 64, outs[0], outs[1])

    return pl.pallas_call(
        body, name=name, grid=(nb, PAIRS, nq),
        out_shape=(jax.ShapeDtypeStruct((T, 512), F32), jax.ShapeDtypeStruct((T, 1024), F32)),
        in_specs=[pl.BlockSpec((tq, 2 * LANES), lambda b, p, i: (b * nq + i, p)),
                  pl.BlockSpec((seq, 2 * LANES), lambda b, p, i: (b, p)),
                  pl.BlockSpec((seq, LANES), lambda b, p, i: (b, p))],
        out_specs=[pl.BlockSpec((tq, LANES), lambda b, p, i: (b * nq + i, p)),
                   pl.BlockSpec((tq, 2 * LANES), lambda b, p, i: (b * nq + i, p))],
        compiler_params=_params(("arbitrary", "arbitrary", "arbitrary")),
    )(q, k, v)


def _attn_bwd(q, k, v, o, do, lse, nb, seq, tq, name, key_bias):
    T = q.shape[0]
    nq = seq // tq

    def body(q_ref, k_ref, v_ref, o_ref, do_ref, lse_ref, dq_ref, dk_ref, dv_ref, dsc, rsum):
        kj = pl.program_id(2)
        lane_s = lax.broadcasted_iota(jnp.int32, (seq, LANES), 1)
        lane = lax.broadcasted_iota(jnp.int32, (tq, LANES), 1)
        rr = lax.broadcasted_iota(jnp.int32, (tq, tq), 0)
        cc = lax.broadcasted_iota(jnp.int32, (tq, tq), 1)

        @pl.when(kj == 0)
        def _():
            dq_ref[...] = jnp.zeros_like(dq_ref)
            prod = do_ref[...].astype(F32) * o_ref[...]
            d0 = jnp.sum(jnp.where(lane_s < 64, prod, 0.0), axis=1, keepdims=True)
            d1 = jnp.sum(jnp.where(lane_s < 64, 0.0, prod), axis=1, keepdims=True)
            dsc[0] = jnp.broadcast_to(d0, (seq, LANES))
            dsc[1] = jnp.broadcast_to(d1, (seq, LANES))
            if key_bias:
                rsum[...] = jnp.zeros_like(rsum)

        vv = v_ref[...]
        dv_acc = jnp.zeros((tq, LANES), F32)
        for hh in range(2):
            kv = k_ref[:, LANES * hh:LANES * (hh + 1)]
            keep = (lane < 64) if hh == 0 else (lane >= 64)

            def step(qi, carry, masked, hh=hh, kv=kv, keep=keep):
                dk_acc, dv_acc, col = carry
                rows = pl.ds(pl.multiple_of(qi * tq, tq), tq)
                qv = q_ref[rows, LANES * hh:LANES * (hh + 1)]
                dom = jnp.where(keep, do_ref[rows, :], jnp.zeros((), BF16))
                s = _dot_nt(qv, kv)
                if masked:
                    s = jnp.where(cc <= rr, s, NEG)
                p = jnp.exp(s - lse_ref[rows, LANES * hh:LANES * hh + 1])
                dp = _dot_nt(dom, vv)
                ds32 = p * (dp - dsc[hh, rows, 0:1])
                if key_bias:
                    col = col + jnp.sum(ds32, axis=0, keepdims=True)
                    rsum[hh, rows, :] += jnp.broadcast_to(jnp.sum(ds32, axis=1, keepdims=True), (tq, LANES))
                ds = ds32.astype(BF16)
                dv_acc = dv_acc + _dot_tn(p.astype(BF16), dom)
                dk_acc = dk_acc + _dot_tn(ds, qv)
                dq_ref[rows, LANES * hh:LANES * (hh + 1)] += _dot(ds, kv)
                return dk_acc, dv_acc, col

            carry = step(kj, (jnp.zeros((tq, LANES), F32), dv_acc, jnp.zeros((1, tq), F32)), True)
            dk_acc, dv_acc, col = lax.fori_loop(kj + 1, nq, functools.partial(step, masked=False), carry)
            if key_bias:
                first = (lax.broadcasted_iota(jnp.int32, (8, LANES), 0) == 0).astype(BF16)
                colsum = sum(_dot_tn(jnp.broadcast_to(piece, (8, tq)), first) for piece in _split3(col))
                dk_acc = jnp.where(lane == 64, -colsum, dk_acc)
            dk_ref[:, LANES * hh:LANES * (hh + 1)] = dk_acc
        dv_ref[...] = dv_acc

        if key_bias:
            @pl.when(kj == nq - 1)
            def _():
                for hh in range(2):
                    blk = dq_ref[:, LANES * hh:LANES * (hh + 1)]
                    dq_ref[:, LANES * hh:LANES * (hh + 1)] = jnp.where(lane_s == 64, rsum[hh], blk)

    per_seq = lambda w: pl.BlockSpec((seq, w), lambda b, p, j: (b, p))
    per_blk = lambda w: pl.BlockSpec((tq, w), lambda b, p, j: (b * nq + j, p))
    return pl.pallas_call(
        body, name=name, grid=(nb, PAIRS, nq),
        out_shape=(jax.ShapeDtypeStruct((T, 1024), F32), jax.ShapeDtypeStruct((T, 1024), F32), jax.ShapeDtypeStruct((T, 512), F32)),
        in_specs=[per_seq(2 * LANES), per_blk(2 * LANES), per_blk(LANES), per_seq(LANES), per_seq(LANES), per_seq(2 * LANES)],
        out_specs=[per_seq(2 * LANES), per_blk(2 * LANES), per_blk(LANES)],
        scratch_shapes=[pltpu.VMEM((2, seq, LANES), F32), pltpu.VMEM((2, seq, LANES) if key_bias else (2, 8, LANES), F32)],
        compiler_params=_params(("arbitrary", "arbitrary", "arbitrary")),
    )(q, k, v, o, do, lse)


def _mid(of, om, x, tgt, g_fo, g_mo, g2, g3, w_o, w_g, w_u, w_d, tm):
    T = x.shape[0]

    def body(of_ref, om_ref, x_ref, t_ref, gfo_ref, gmo_ref, g2_ref, g3_ref, wo_ref, wg_ref, wu_ref, wd_ref,
             a_ref, h2_ref, hh_ref, dgu_ref, dx3_ref, dx2_ref, dof_ref, dom_ref, st_ref):
        i = pl.program_id(0)

        @pl.when(i == 0)
        def _():
            st_ref[...] = jnp.zeros_like(st_ref)

        ofv, omv = of_ref[...], om_ref[...]
        rf, rm = _rms(ofv, FOX_W), _rms(omv, FOX_W)
        fhat, mhat = ofv * rf, omv * rm
        a = jnp.concatenate([fhat * gfo_ref[...], mhat * gmo_ref[...]], axis=1).astype(BF16)
        a_ref[...] = a
        x2 = x_ref[...] + _dot(a, wo_ref[...])
        r2 = _rms(x2, D_MODEL)
        xh2 = x2 * r2
        h2 = (xh2 * g2_ref[...]).astype(BF16)
        h2_ref[...] = h2
        gt = _dot(h2, wg_ref[...])
        up = _dot(h2, wu_ref[...])
        sg = jax.nn.sigmoid(gt)
        sl = gt * sg
        hid = (sl * up).astype(BF16)
        hh_ref[...] = hid
        x3 = x2 + _dot(hid, wd_ref[...])
        r3 = _rms(x3, D_MODEL)
        xh3 = x3 * r3
        diff = xh3 * g3_ref[...] - t_ref[...]
        dy = diff * (1.0 / D_MODEL)
        st_ref[3:4, :] += jnp.sum(diff * diff, axis=0, keepdims=True) * (0.5 / D_MODEL)
        st_ref[0:1, :] += jnp.sum(dy * xh3, axis=0, keepdims=True)
        dx3 = _rms_bwd(dy, xh3, r3, g3_ref[...], D_MODEL)
        dx3b = dx3.astype(BF16)
        dx3_ref[...] = dx3b
        dhid = _dot_nt(dx3b, wd_ref[...])
        dg = (dhid * up * (sg * (1.0 + gt * (1.0 - sg)))).astype(BF16)
        du = (dhid * sl).astype(BF16)
        dgu_ref[:, 0:D_FF] = dg
        dgu_ref[:, D_FF:2 * D_FF] = du
        dh2 = _dot_nt(dg, wg_ref[...]) + _dot_nt(du, wu_ref[...])
        st_ref[1:2, :] += jnp.sum(dh2 * xh2, axis=0, keepdims=True)
        dx2 = dx3 + _rms_bwd(dh2, xh2, r2, g2_ref[...], D_MODEL)
        dx2_ref[...] = dx2
        da = _dot_nt(dx2.astype(BF16), wo_ref[...])
        daf, dam = da[:, 0:FOX_W], da[:, FOX_W:2 * FOX_W]
        st_ref[2:3, 0:FOX_W] += jnp.sum(daf * fhat, axis=0, keepdims=True)
        st_ref[2:3, FOX_W:2 * FOX_W] += jnp.sum(dam * mhat, axis=0, keepdims=True)
        dof_ref[...] = _rms_bwd(daf, fhat, rf, gfo_ref[...], FOX_W).astype(BF16)
        dom_ref[...] = _rms_bwd(dam, mhat, rm, gmo_ref[...], FOX_W).astype(BF16)

    row = lambda w: pl.BlockSpec((tm, w), lambda i: (i, 0))
    out_shape = (
        jax.ShapeDtypeStruct((T, 1024), BF16), jax.ShapeDtypeStruct((T, 1024), BF16), jax.ShapeDtypeStruct((T, D_FF), BF16),
        jax.ShapeDtypeStruct((T, 2 * D_FF), BF16), jax.ShapeDtypeStruct((T, 1024), BF16), jax.ShapeDtypeStruct((T, 1024), F32),
        jax.ShapeDtypeStruct((T, 512), BF16), jax.ShapeDtypeStruct((T, 512), BF16), jax.ShapeDtypeStruct((8, 1024), F32),
    )
    return pl.pallas_call(
        body, name="mid", grid=(T // tm,), out_shape=out_shape,
        in_specs=[row(512), row(512), row(1024), row(1024), _full(g_fo.shape), _full(g_mo.shape), _full(g2.shape), _full(g3.shape),
                  _full(w_o.shape), _full(w_g.shape), _full(w_u.shape), _full(w_d.shape)],
        out_specs=[row(1024), row(1024), row(D_FF), row(2 * D_FF), row(1024), row(1024), row(512), row(512),
                   pl.BlockSpec((8, 1024), lambda i: (0, 0))],
        compiler_params=_params(("arbitrary",)),
    )(of, om, x, tgt, g_fo, g_mo, g2, g3, w_o, w_g, w_u, w_d)


def _in_bwd(dqf, dkf, dvf, dqm, dkm, dvm, lat, x, dx2, g1, gq, gkv, bfg, ct, st, sel_t, w_in, w_q12, w_kv, seq, tm):
    T = x.shape[0]
    nblk = T // tm
    nsb = seq // tm

    def body(dqf_ref, dkf_ref, dvf_ref, dqm_ref, dkm_ref, dvm_ref, lat_ref, x_ref, dx2_ref, g1_ref, gq_ref, gkv_ref, b_ref,
             ct_ref, st_ref, selt_ref, win_ref, wq_ref, wkv_ref,
             dx_ref, dproj_ref, dq12_ref, dkv_ref, stat_ref, carry):
        i = pl.program_id(0)

        @pl.when(i == 0)
        def _():
            stat_ref[...] = jnp.zeros_like(stat_ref)

        @pl.when(i % nsb == 0)
        def _():
            carry[...] = jnp.zeros_like(carry)

        lane = lax.broadcasted_iota(jnp.int32, (tm, LANES), 1)
        low = lane < 64
        ctv, stv = ct_ref[...], st_ref[...]

        for j in range(PAIRS):
            e, o = 2 * LANES * j, 2 * LANES * j + LANES
            dq = jnp.where(low, dqf_ref[:, e:e + LANES], 0.0) + pltpu.roll(jnp.where(low, dqf_ref[:, o:o + LANES], 0.0), 64, 1)
            dk = jnp.where(low, dkf_ref[:, e:e + LANES], 0.0) + pltpu.roll(jnp.where(low, dkf_ref[:, o:o + LANES], 0.0), 64, 1)
            dproj_ref[:, C_FQ + LANES * j:C_FQ + LANES * (j + 1)] = (dq * FOX_SCALE).astype(BF16)
            dproj_ref[:, C_FK + LANES * j:C_FK + LANES * (j + 1)] = dk.astype(BF16)
        dproj_ref[:, C_FV:C_QL] = dvf_ref[...].astype(BF16)
        dcv = dkf_ref[...] + dqf_ref[...]
        k_hi = dcv.astype(BF16)
        k_lo = (dcv - k_hi.astype(F32)).astype(BF16)
        dc = _dot(k_hi, selt_ref[...]) + _dot(k_lo, selt_ref[...])
        rr = lax.broadcasted_iota(jnp.int32, (tm, tm), 0)
        cc = lax.broadcasted_iota(jnp.int32, (tm, tm), 1)
        triu = (cc >= rr).astype(BF16)
        a0, a1, a2 = _split3(dc)
        dlf = _dot(triu, a0) + _dot(triu, a1) + _dot(triu, a2) + carry[0:1, :]
        carry[0:1, :] = dlf[0:1, :]
        misc_a = lat_ref[:, Q_RANK + KV_RANK:Q_RANK + KV_RANK + LANES]
        z = misc_a + b_ref[...]
        dz = jnp.where(lane < HEADS, dlf * jax.nn.sigmoid(-z), 0.0)
        stat_ref[3:4, 0:LANES] += jnp.sum(dz, axis=0, keepdims=True)

        cq = (jnp.where(low, 1.0, 0.0) + ctv) * MLA_SCALE
        sq = stv * MLA_SCALE
        dkpe = jnp.zeros((tm, LANES), F32)
        for hd in range(HEADS):
            s0 = LANES * hd
            dqh = dqm_ref[:, s0:s0 + LANES]
            dq12_ref[:, s0:s0 + LANES] = (dqh * cq).astype(BF16)
            dq12_ref[:, 1024 + s0:1024 + s0 + LANES] = (dqh * sq).astype(BF16)
            dkpe = dkpe + dkm_ref[:, s0:s0 + LANES]
        dkv_ref[:, 0:1024] = dkm_ref[...].astype(BF16)
        dkv_ref[:, 1024:1536] = dvm_ref[...].astype(BF16)
        dproj_ref[:, C_MA:C_MB] = (dz + dkpe * ctv).astype(BF16)
        dproj_ref[:, C_MB:C_END] = (dkpe * stv).astype(BF16)
        dqn = _dot_nt(dq12_ref[...], wq_ref[...])
        dkvn = _dot_nt(dkv_ref[...], wkv_ref[...])
        ql = lat_ref[:, 0:Q_RANK]
        kvl = lat_ref[:, Q_RANK:Q_RANK + KV_RANK]
        rq, rkv = _rms(ql, Q_RANK), _rms(kvl, KV_RANK)
        qhat, kvhat = ql * rq, kvl * rkv
        stat_ref[1:2, 0:Q_RANK] += jnp.sum(dqn * qhat, axis=0, keepdims=True)
        stat_ref[2:3, 0:KV_RANK] += jnp.sum(dkvn * kvhat, axis=0, keepdims=True)
        dproj_ref[:, C_QL:C_KVL] = _rms_bwd(dqn, qhat, rq, gq_ref[...], Q_RANK).astype(BF16)
        dproj_ref[:, C_KVL:C_MA] = _rms_bwd(dkvn, kvhat, rkv, gkv_ref[...], KV_RANK).astype(BF16)

        dh1 = _dot_nt(dproj_ref[...], win_ref[...])
        xv = x_ref[...]
        r1 = _rms(xv, D_MODEL)
        xh = xv * r1
        stat_ref[0:1, :] += jnp.sum(dh1 * xh, axis=0, keepdims=True)
        dx_ref[...] = dx2_ref[...] + _rms_bwd(dh1, xh, r1, g1_ref[...], D_MODEL)

    rev = lambda w: pl.BlockSpec((tm, w), lambda i: (nblk - 1 - i, 0))
    out_shape = (
        jax.ShapeDtypeStruct((T, 1024), F32), jax.ShapeDtypeStruct((T, C_END), BF16), jax.ShapeDtypeStruct((T, 2048), BF16),
        jax.ShapeDtypeStruct((T, 1536), BF16), jax.ShapeDtypeStruct((8, 1024), F32),
    )
    return pl.pallas_call(
        body, name="in_bwd", grid=(nblk,), out_shape=out_shape,
        in_specs=[rev(1024), rev(1024), rev(512), rev(1024), rev(1024), rev(512), rev(512), rev(1024), rev(1024),
                  _full(g1.shape), _full(gq.shape), _full(gkv.shape), _full(bfg.shape), rev(LANES), rev(LANES), _full(sel_t.shape),
                  _full(w_in.shape), _full(w_q12.shape), _full(w_kv.shape)],
        out_specs=[rev(1024), rev(C_END), rev(2048), rev(1536), pl.BlockSpec((8, 1024), lambda i: (0, 0))],
        scratch_shapes=[pltpu.VMEM((8, LANES), F32)],
        compiler_params=_params(("arbitrary",)),
    )(dqf, dkf, dvf, dqm, dkm, dvm, lat, x, dx2, g1, gq, gkv, bfg, ct, st, sel_t, w_in, w_q12, w_kv)


def _wgrad(a, b, tn, tt, name):
    T, K = a.shape
    N = b.shape[1]

    def body(a_ref, b_ref, o_ref):
        @pl.when(pl.program_id(1) == 0)
        def _():
            o_ref[...] = jnp.zeros_like(o_ref)

        o_ref[...] += _dot_tn(a_ref[...].astype(BF16), b_ref[...].astype(BF16))

    return pl.pallas_call(
        body, name=name, grid=(N // tn, T // tt), out_shape=jax.ShapeDtypeStruct((K, N), F32),
        in_specs=[pl.BlockSpec((tt, K), lambda n, t: (t, 0)), pl.BlockSpec((tt, tn), lambda n, t: (t, n))],
        out_specs=pl.BlockSpec((K, tn), lambda n, t: (0, n)),
        compiler_params=_params(("arbitrary", "arbitrary")),
    )(a, b)


ANY = pl.BlockSpec(memory_space=pl.ANY)


def _place():
    return lax.axis_index("x"), lax.axis_index("y"), lax.axis_index("c")


def _other_chips(x, y):
    return [(1 - x, y), (x, 1 - y), (1 - x, 1 - y)]


def _gather_weights(pack):
    R = pack.shape[0]

    def body(p_ref, o_ref, send, recv, lsem):
        x, y, c = _place()
        mine = pltpu.make_async_copy(p_ref, o_ref.at[2 * x + y], lsem)
        mine.start()
        out = [pltpu.make_async_remote_copy(src_ref=p_ref, dst_ref=o_ref.at[2 * x + y], send_sem=send.at[j], recv_sem=recv.at[j],
                                            device_id=(cx, cy, c), device_id_type=MESH)
               for j, (cx, cy) in enumerate(_other_chips(x, y))]
        for cp in out:
            cp.start()
        for j, (cx, cy) in enumerate(_other_chips(x, y)):
            pltpu.make_async_remote_copy(src_ref=p_ref, dst_ref=o_ref.at[2 * cx + cy], send_sem=send.at[j], recv_sem=recv.at[j],
                                         device_id=(cx, cy, c), device_id_type=MESH).wait_recv()
        for cp in out:
            cp.wait_send()
        mine.wait()

    return pl.pallas_call(
        body, name="gather_weights", out_shape=jax.ShapeDtypeStruct((N_CHIPS, R, 1024), pack.dtype),
        in_specs=[ANY], out_specs=ANY,
        scratch_shapes=[pltpu.SemaphoreType.DMA((3,)), pltpu.SemaphoreType.DMA((3,)), pltpu.SemaphoreType.DMA],
        compiler_params=pltpu.CompilerParams(has_side_effects=True),
    )(pack)


def _swap_halves(g):
    n, R, W = g.shape
    hr = R // 2

    def body(g_ref, o_ref, send, recv):
        x, y, c = _place()
        cp = pltpu.make_async_remote_copy(src_ref=g_ref.at[:, pl.ds((1 - c) * hr, hr), :], dst_ref=o_ref, send_sem=send, recv_sem=recv,
                                          device_id=(x, y, 1 - c), device_id_type=MESH)
        cp.start()
        cp.wait()

    return pl.pallas_call(
        body, name="swap_halves", out_shape=jax.ShapeDtypeStruct((n, hr, W), g.dtype),
        in_specs=[ANY], out_specs=ANY,
        scratch_shapes=[pltpu.SemaphoreType.DMA, pltpu.SemaphoreType.DMA],
        compiler_params=pltpu.CompilerParams(has_side_effects=True),
    )(g)


def _add_half(g, got, tr):
    n, R, W = g.shape
    hr = R // 2
    nb = hr // tr

    def body(c_ref, g_ref, r_ref, o_ref):
        o_ref[...] = g_ref[...] + r_ref[...]

    c = lax.axis_index("c")
    return pl.pallas_call(
        body, name="add_half",
        grid_spec=pltpu.PrefetchScalarGridSpec(
            num_scalar_prefetch=1, grid=(n, nb),
            in_specs=[pl.BlockSpec((1, tr, W), lambda k, i, c_ref: (k, c_ref[0] * nb + i, 0)),
                      pl.BlockSpec((1, tr, W), lambda k, i, c_ref: (k, i, 0))],
            out_specs=pl.BlockSpec((1, tr, W), lambda k, i, c_ref: (k, i, 0))),
        out_shape=jax.ShapeDtypeStruct((n, hr, W), g.dtype),
        compiler_params=_params(("arbitrary", "arbitrary")),
    )(jnp.reshape(c, (1,)).astype(jnp.int32), g, got)


def _scatter_slabs(s):
    n, hr, W = s.shape

    def body(s_ref, o_ref, send, recv):
        x, y, c = _place()
        out = [pltpu.make_async_remote_copy(src_ref=s_ref.at[2 * cx + cy], dst_ref=o_ref.at[j], send_sem=send.at[j], recv_sem=recv.at[j],
                                            device_id=(cx, cy, c), device_id_type=MESH)
               for j, (cx, cy) in enumerate(_other_chips(x, y))]
        for cp in out:
            cp.start()
        for cp in out:
            cp.wait()

    return pl.pallas_call(
        body, name="scatter_slabs", out_shape=jax.ShapeDtypeStruct((3, hr, W), s.dtype),
        in_specs=[ANY], out_specs=ANY,
        scratch_shapes=[pltpu.SemaphoreType.DMA((3,)), pltpu.SemaphoreType.DMA((3,))],
        compiler_params=pltpu.CompilerParams(has_side_effects=True),
    )(s)


def _sum_slabs(s, got, tr):
    n, hr, W = s.shape
    nb = hr // tr

    def body(k_ref, s_ref, r_ref, o_ref):
        o_ref[...] = ((s_ref[0] + r_ref[0]) + r_ref[1]) + r_ref[2]

    k = 2 * lax.axis_index("x") + lax.axis_index("y")
    return pl.pallas_call(
        body, name="sum_slabs",
        grid_spec=pltpu.PrefetchScalarGridSpec(
            num_scalar_prefetch=1, grid=(nb,),
            in_specs=[pl.BlockSpec((1, tr, W), lambda i, k_ref: (k_ref[0], i, 0)),
                      pl.BlockSpec((3, tr, W), lambda i, k_ref: (0, i, 0))],
            out_specs=pl.BlockSpec((tr, W), lambda i, k_ref: (i, 0))),
        out_shape=jax.ShapeDtypeStruct((hr, W), s.dtype),
        compiler_params=_params(("arbitrary",)),
    )(jnp.reshape(k, (1,)).astype(jnp.int32), s, got)


def _join_halves(t):
    hr, W = t.shape

    def body(t_ref, o_ref, send, recv, lsem):
        x, y, c = _place()
        mine = pltpu.make_async_copy(t_ref, o_ref.at[c], lsem)
        mine.start()
        cp = pltpu.make_async_remote_copy(src_ref=t_ref, dst_ref=o_ref.at[c], send_sem=send, recv_sem=recv,
                                          device_id=(x, y, 1 - c), device_id_type=MESH)
        cp.start()
        pltpu.make_async_remote_copy(src_ref=t_ref, dst_ref=o_ref.at[1 - c], send_sem=send, recv_sem=recv,
                                     device_id=(x, y, 1 - c), device_id_type=MESH).wait_recv()
        cp.wait_send()
        mine.wait()

    return pl.pallas_call(
        body, name="join_halves", out_shape=jax.ShapeDtypeStruct((2, hr, W), t.dtype),
        in_specs=[ANY], out_specs=ANY,
        scratch_shapes=[pltpu.SemaphoreType.DMA, pltpu.SemaphoreType.DMA, pltpu.SemaphoreType.DMA],
        compiler_params=pltpu.CompilerParams(has_side_effects=True),
    )(t)


def _allreduce_small(v):
    def body(v_ref, o_ref, buf, send, recv):
        x, y, c = _place()
        me = 4 * x + 2 * y + c
        buf[me] = v_ref[...]
        out = []
        for j in range(7):
            fx, fy, fc = (j + 1) >> 2 & 1, (j + 1) >> 1 & 1, (j + 1) & 1
            peer = (x ^ fx, y ^ fy, c ^ fc)
            out.append(pltpu.make_async_remote_copy(src_ref=v_ref, dst_ref=buf.at[me], send_sem=send.at[j], recv_sem=recv.at[j],
                                                    device_id=peer, device_id_type=MESH))
        for cp in out:
            cp.start()
        for j in range(7):
            fx, fy, fc = (j + 1) >> 2 & 1, (j + 1) >> 1 & 1, (j + 1) & 1
            src = 4 * (x ^ fx) + 2 * (y ^ fy) + (c ^ fc)
            pltpu.make_async_remote_copy(src_ref=v_ref, dst_ref=buf.at[src], send_sem=send.at[j], recv_sem=recv.at[j],
                                         device_id=(x ^ fx, y ^ fy, c ^ fc), device_id_type=MESH).wait_recv()
        for cp in out:
            cp.wait_send()
        acc = buf[0]
        for d in range(1, 8):
            acc = acc + buf[d]
        o_ref[...] = acc

    vm = pl.BlockSpec(memory_space=pltpu.VMEM)
    return pl.pallas_call(
        body, name="allreduce_small", out_shape=jax.ShapeDtypeStruct(v.shape, v.dtype),
        in_specs=[vm], out_specs=vm,
        scratch_shapes=[pltpu.VMEM((8,) + v.shape, v.dtype), pltpu.SemaphoreType.DMA((7,)), pltpu.SemaphoreType.DMA((7,))],
        compiler_params=pltpu.CompilerParams(has_side_effects=True),
    )(v)


def _adamw(w, g, m, v, name):
    R, C = w.shape
    tr = R
    for cand in (256, 176, 128, 64, 32, 16, 8):
        if R % cand == 0:
            tr = cand
            break

    def body(w_ref, g_ref, m_ref, v_ref, d_ref, nm_ref, nv_ref):
        gv = g_ref[...]
        nm = ADAM_B1 * m_ref[...] + (1.0 - ADAM_B1) * gv
        nv = ADAM_B2 * v_ref[...] + (1.0 - ADAM_B2) * (gv * gv)
        m_hat = nm / (1.0 - ADAM_B1 ** ADAM_STEP)
        v_hat = nv / (1.0 - ADAM_B2 ** ADAM_STEP)
        d_ref[...] = -ADAM_LR * (m_hat / (jnp.sqrt(v_hat) + ADAM_EPS) + ADAM_WD * w_ref[...])
        nm_ref[...] = nm
        nv_ref[...] = nv

    blk = pl.BlockSpec((tr, C), lambda i: (i, 0))
    sh = jax.ShapeDtypeStruct((R, C), F32)
    return pl.pallas_call(
        body, name=name, grid=(R // tr,), out_shape=(sh, sh, sh),
        in_specs=[blk, blk, blk, blk], out_specs=[blk, blk, blk],
        compiler_params=_params(("arbitrary",)),
    )(w, g, m, v)


def _arrange(w):
    win = w["w_in"]
    dt = win.dtype
    z = lambda r, c: jnp.zeros((r, c), dt)
    zh = lambda c: jnp.zeros((w["w_uq"].shape[0], HEADS, c), dt)
    kr1, kr2 = win[:, 1928:1944], win[:, 1944:1960]
    misc_a = jnp.concatenate([win[:, 1536:1544], z(1024, 56), kr1, kr2, z(1024, 32)], axis=1)
    misc_b = jnp.concatenate([z(1024, 64), kr2, kr1, z(1024, 32)], axis=1)
    w_in = jnp.concatenate([win[:, 0:1536], win[:, 1544:1928], misc_a, misc_b], axis=1)
    wq = w["w_uq"].reshape(Q_RANK, HEADS, 96)
    q1 = jnp.concatenate([wq, zh(32)], axis=2).reshape(Q_RANK, 1024)
    q2 = jnp.concatenate([zh(64), wq[:, :, 80:96], wq[:, :, 64:80], zh(32)], axis=2).reshape(Q_RANK, 1024)
    wkv = w["w_ukv"].reshape(KV_RANK, HEADS, 128)
    wk = jnp.concatenate([wkv[:, :, 0:64], jnp.zeros((KV_RANK, HEADS, 64), dt)], axis=2).reshape(KV_RANK, 1024)
    wv = wkv[:, :, 64:128].reshape(KV_RANK, 512)
    b = lambda a: a.astype(BF16)
    return dict(w_in=b(w_in), w_q12=b(jnp.concatenate([q1, q2], axis=1)), w_k=b(wk), w_v=b(wv),
                w_kv=b(jnp.concatenate([wk, wv], axis=1)), w_o=b(w["w_o"]), w_g=b(w["w_gate"]), w_u=b(w["w_up"]), w_d=b(w["w_down"]))


def _unarrange(g_in, g_q12, g_kv):
    kr1 = g_in[:, C_MA + 64:C_MA + 80] + g_in[:, C_MB + 80:C_MB + 96]
    kr2 = g_in[:, C_MA + 80:C_MA + 96] + g_in[:, C_MB + 64:C_MB + 80]
    w_in = jnp.concatenate([g_in[:, 0:1536], g_in[:, C_MA:C_MA + 8], g_in[:, 1536:1920], kr1, kr2], axis=1)
    g1 = g_q12[:, 0:1024].reshape(Q_RANK, HEADS, 128)
    g2 = g_q12[:, 1024:2048].reshape(Q_RANK, HEADS, 128)
    w_uq = jnp.concatenate([g1[:, :, 0:64], g1[:, :, 64:80] + g2[:, :, 80:96], g1[:, :, 80:96] + g2[:, :, 64:80]], axis=2).reshape(Q_RANK, 768)
    gk = g_kv[:, 0:1024].reshape(KV_RANK, HEADS, 128)
    gv = g_kv[:, 1024:1536].reshape(KV_RANK, HEADS, 64)
    w_ukv = jnp.concatenate([gk[:, :, 0:64], gv], axis=2).reshape(KV_RANK, 1024)
    return w_in, w_uq, w_ukv


def _selectors():
    sel = np.zeros((384, 1024), np.float32)
    sel_t = np.zeros((1024, LANES), np.float32)
    for h in range(HEADS):
        for piece in range(3):
            sel[LANES * piece + h, LANES * h + 64 + piece] = 1.0
        sel_t[LANES * h + 64, h] = 1.0
    return jnp.asarray(sel, BF16), jnp.asarray(sel_t, BF16)


def _rope_tables(positions):
    inv_freq = 10000.0 ** (-jnp.arange(0, ROPE, 2, dtype=F32) / ROPE)
    ang = positions.reshape(-1).astype(F32)[:, None] * inv_freq[None, :]
    cos, sin = jnp.cos(ang), jnp.sin(ang)
    z64, z32 = jnp.zeros((ang.shape[0], 64), F32), jnp.zeros((ang.shape[0], 32), F32)
    return jnp.concatenate([z64, cos, cos, z32], axis=1), jnp.concatenate([z64, -sin, sin, z32], axis=1)


def _local_step(x, positions, w, small, loss_target, tm=256, tq=512):
    nb, seq, _ = x.shape
    T = nb * seq
    tm, tq = min(tm, seq), min(tq, seq)
    xf = x.reshape(T, D_MODEL)
    tgt = loss_target.reshape(T, D_MODEL)
    a = _arrange(w)
    sel, sel_t = _selectors()
    ct, st = _rope_tables(positions)
    bfg = jnp.concatenate([small["b_fgate"], jnp.zeros((1, LANES - HEADS), F32)], axis=1)
    g1, gq, gkv = small["norm_mix_g"], small["q_norm_g"], small["kv_norm_g"]

    h1, qf, kf, vf, qm, km, vm, lat, qn, kvn = _in_proj(xf, g1, a["w_in"], a["w_q12"], a["w_k"], a["w_v"], gq, gkv, bfg, ct, st, sel, seq, tm)
    of, lse_f = _attn_fwd(qf, kf, vf, nb, seq, tq, "fox_fwd")
    om, lse_m = _attn_fwd(qm, km, vm, nb, seq, tq, "mla_fwd")
    a_cat, h2, hid, dgu, dx3, dx2, dof, dom, st_mid = _mid(
        of, om, xf, tgt, small["fox_out_g"], small["mla_out_g"], small["norm_ffn_g"], small["final_norm_g"].reshape(1, D_MODEL),
        a["w_o"], a["w_g"], a["w_u"], a["w_d"], tm)
    dqf, dkf, dvf = _attn_bwd(qf, kf, vf, of, dof, lse_f, nb, seq, tq, "fox_bwd", True)
    dqm, dkm, dvm = _attn_bwd(qm, km, vm, om, dom, lse_m, nb, seq, tq, "mla_bwd", False)
    dx, dproj, dq12, dkv, st_in = _in_bwd(dqf, dkf, dvf, dqm, dkm, dvm, lat, xf, dx2, g1, gq, gkv, bfg, ct, st, sel_t,
                                          a["w_in"], a["w_q12"], a["w_kv"], seq, tm)
    tt = min(512, T)
    g_in = _wgrad(h1, dproj, C_END, tt, "wgrad_in")
    g_q12 = _wgrad(qn, dq12, 2048, tt, "wgrad_uq")
    g_kv = _wgrad(kvn, dkv, 1536, tt, "wgrad_ukv")
    g_o = _wgrad(a_cat, dx2, 512, tt, "wgrad_o")
    g_gu = _wgrad(h2, dgu, D_FF // 2, tt, "wgrad_gate_up")
    g_d = _wgrad(hid, dx3, 512, tt, "wgrad_down")
    gw_in, gw_uq, gw_ukv = _unarrange(g_in, g_q12, g_kv)
    grads = dict(w_in=gw_in, w_uq=gw_uq, w_ukv=gw_ukv, w_o=g_o, w_gate=g_gu[:, 0:D_FF], w_up=g_gu[:, D_FF:2 * D_FF], w_down=g_d)
    loss_row = jnp.concatenate([jnp.sum(st_mid[3:4, :], axis=1, keepdims=True), jnp.zeros((1, D_MODEL - 1), F32)], axis=1)
    stats = jnp.concatenate([st_in[0:1], st_mid[1:2], st_mid[0:1], st_mid[2:3], st_in[1:2], st_in[2:3], st_in[3:4], loss_row], axis=0)
    return stats, dx.reshape(x.shape), grads


def _pack_shards(t):
    flat = jnp.concatenate([t[n].reshape(-1) for n in BIG])
    return jnp.pad(flat, (0, PACK_ROWS * 1024 - flat.shape[0])).reshape(PACK_ROWS, 1024)


def _unpack_full(slabs):
    flat = slabs.reshape(N_CHIPS, -1)
    out, off = {}, 0
    for n in BIG:
        r, c = FULL_SHAPES[n]
        cnt = r * c // N_CHIPS
        part = flat[:, off:off + cnt]
        off += cnt
        if n in COL_SHARDED:
            out[n] = part.reshape(N_CHIPS, r, c // N_CHIPS).transpose(1, 0, 2).reshape(r, c)
        else:
            out[n] = part.reshape(r, c)
    return out


def _pack_full(g):
    parts = []
    for n in BIG:
        r, c = FULL_SHAPES[n]
        if n in COL_SHARDED:
            parts.append(g[n].reshape(r, N_CHIPS, c // N_CHIPS).transpose(1, 0, 2).reshape(N_CHIPS, -1))
        else:
            parts.append(g[n].reshape(N_CHIPS, -1))
    flat = jnp.concatenate(parts, axis=1)
    return jnp.pad(flat, ((0, 0), (0, PACK_ROWS * 1024 - flat.shape[1]))).reshape(N_CHIPS, PACK_ROWS, 1024)


def _unpack_shard(slab, like):
    flat = slab.reshape(-1)
    out, off = {}, 0
    for n in BIG:
        cnt = like[n].size
        out[n] = flat[off:off + cnt].reshape(like[n].shape)
        off += cnt
    return out


SMALL_ROWS = {"norm_mix_g": (0, 1024), "norm_ffn_g": (1, 1024), "final_norm_g": (2, 1024), "q_norm_g": (4, 256),
              "kv_norm_g": (5, 128), "b_fgate": (6, 8)}


def kernel(x, positions, norm_mix_g, w_in, b_fgate, q_norm_g, w_uq, kv_norm_g, w_ukv, fox_out_g, mla_out_g, w_o, norm_ffn_g, w_gate, w_up, w_down, final_norm_g, loss_target, m_norm_mix_g, m_w_in, m_b_fgate, m_q_norm_g, m_w_uq, m_kv_norm_g, m_w_ukv, m_fox_out_g, m_mla_out_g, m_w_o, m_norm_ffn_g, m_w_gate, m_w_up, m_w_down, m_final_norm_g, v_norm_mix_g, v_w_in, v_b_fgate, v_q_norm_g, v_w_uq, v_kv_norm_g, v_w_ukv, v_fox_out_g, v_mla_out_g, v_w_o, v_norm_ffn_g, v_w_gate, v_w_up, v_w_down, v_final_norm_g):
    names = ["norm_mix_g", "w_in", "b_fgate", "q_norm_g", "w_uq", "kv_norm_g", "w_ukv", "fox_out_g", "mla_out_g", "w_o",
             "norm_ffn_g", "w_gate", "w_up", "w_down", "final_norm_g"]
    wts = dict(zip(names, [norm_mix_g, w_in, b_fgate, q_norm_g, w_uq, kv_norm_g, w_ukv, fox_out_g, mla_out_g, w_o, norm_ffn_g,
                           w_gate, w_up, w_down, final_norm_g]))
    mom = dict(zip(names, [m_norm_mix_g, m_w_in, m_b_fgate, m_q_norm_g, m_w_uq, m_kv_norm_g, m_w_ukv, m_fox_out_g, m_mla_out_g,
                           m_w_o, m_norm_ffn_g, m_w_gate, m_w_up, m_w_down, m_final_norm_g]))
    var = dict(zip(names, [v_norm_mix_g, v_w_in, v_b_fgate, v_q_norm_g, v_w_uq, v_kv_norm_g, v_w_ukv, v_fox_out_g, v_mla_out_g,
                           v_w_o, v_norm_ffn_g, v_w_gate, v_w_up, v_w_down, v_final_norm_g]))
    shard = {n: wts[n][0] for n in BIG}

    slabs = _gather_weights(_pack_shards(shard).astype(BF16))
    full = _unpack_full(slabs)

    small = {n: wts[n] for n in SMALL if n != "final_norm_g"}
    small["final_norm_g"] = final_norm_g
    stats, grad_x, gfull = _local_step(x, positions, full, small, loss_target)

    stats = _allreduce_small(stats)
    gpack = _pack_full(gfull)
    tr = 184 if HALF_ROWS % 184 == 0 else 8
    mine = _add_half(gpack, _swap_halves(gpack), tr)
    reduced = _sum_slabs(mine, _scatter_slabs(mine), tr)
    gshard = _unpack_shard(_join_halves(reduced).reshape(PACK_ROWS, 1024), shard)

    grads, delta, new_m, new_v = {}, {}, {}, {}
    for n in BIG:
        grads[n] = gshard[n][None]
        d, nm, nv = _adamw(shard[n], gshard[n], mom[n][0], var[n][0], "adamw_" + n)
        delta[n], new_m[n], new_v[n] = d[None], nm[None], nv[None]
    sm_g = {}
    for n, (row, width) in SMALL_ROWS.items():
        sm_g[n] = stats[row:row + 1, 0:width]
    sm_g["fox_out_g"] = stats[3:4, 0:512]
    sm_g["mla_out_g"] = stats[3:4, 512:1024]
    pad = lambda a: jnp.pad(a.reshape(1, -1), ((0, 0), (0, 1024 - a.size)))
    order = list(SMALL)
    stack = lambda d: jnp.concatenate([pad(d[n]) for n in order], axis=0)
    sd, sm, sv = _adamw(stack(wts), stack(sm_g), stack(mom), stack(var), "adamw_small")
    for i, n in enumerate(order):
        shp = wts[n].shape
        grads[n] = sm_g[n].reshape(shp)
        delta[n] = sd[i, 0:wts[n].size].reshape(shp)
        new_m[n] = sm[i, 0:wts[n].size].reshape(shp)
        new_v[n] = sv[i, 0:wts[n].size].reshape(shp)
    loss = stats[7, 0]
    return (loss, grad_x, *[grads[n] for n in names], *[delta[n] for n in names], *[new_m[n] for n in names], *[new_v[n] for n in names])
```

```python
import functools

import numpy as np
import jax
import jax.numpy as jnp
from jax import lax
from jax.experimental import pallas as pl
from jax.experimental.pallas import tpu as pltpu

F32 = jnp.float32
BF16 = jnp.bfloat16
MESH = pl.DeviceIdType.MESH

EPS = 1e-6
D_MODEL = 1024
HEADS = 8
PAIRS = HEADS // 2
FOX_W = 512
Q_RANK = 256
KV_RANK = 128
ROPE = 32
D_FF = 2816
N_CHIPS = 4
FF_CHUNK = D_FF // N_CHIPS
FOX_SCALE = 64 ** -0.5
MLA_SCALE = 96 ** -0.5
LANES = 128
NEG = -1e30

ADAM_LR, ADAM_B1, ADAM_B2, ADAM_EPS, ADAM_WD, ADAM_STEP = 0.001, 0.9, 0.999, 1e-08, 0.01, 10

C_FQ, C_FK, C_FV, C_QL, C_KVL, C_MA, C_MB, C_END = 0, 512, 1024, 1536, 1792, 1920, 2048, 2176

VMEM_LIMIT = 60 * 1024 * 1024
ROW_TILE = 256
ATTN_TILE = 512

HEAD3 = ("w_in", "w_uq", "w_ukv")
FFN4 = ("w_o", "w_gate", "w_up", "w_down")
COL_SHARDED = ("w_in", "w_uq", "w_ukv", "w_gate", "w_up")
SMALL = ("norm_mix_g", "b_fgate", "q_norm_g", "kv_norm_g", "fox_out_g", "mla_out_g", "norm_ffn_g", "final_norm_g")
FULL_SHAPES = {"w_in": (1024, 1960), "w_uq": (256, 768), "w_ukv": (128, 1024)}
PACK_ELEMS = sum(a * b for a, b in FULL_SHAPES.values()) // N_CHIPS
PACK_ROWS = -(-PACK_ELEMS // (1024 * 32)) * 32


def _params(sem=None):
    return pltpu.CompilerParams(dimension_semantics=sem, vmem_limit_bytes=VMEM_LIMIT)


def _full(shape):
    n = len(shape)
    return pl.BlockSpec(shape, lambda *_: (0,) * n, pipeline_mode=pl.Buffered(1))


def _dot(a, b):
    return jnp.dot(a, b, preferred_element_type=F32)


def _dot_nt(a, b):
    return lax.dot_general(a, b, (((1,), (1,)), ((), ())), preferred_element_type=F32)


def _dot_tn(a, b):
    return lax.dot_general(a, b, (((0,), (0,)), ((), ())), preferred_element_type=F32)


def _split3(v):
    hi = v.astype(BF16)
    r1 = v - hi.astype(F32)
    mid = r1.astype(BF16)
    lo = (r1 - mid.astype(F32)).astype(BF16)
    return hi, mid, lo


def _rms(v, width):
    return lax.rsqrt(jnp.sum(v * v, axis=1, keepdims=True) * (1.0 / width) + EPS)


def _rms_bwd(dy, xhat, r, g, width):
    u = dy * g
    return r * (u - xhat * (jnp.sum(u * xhat, axis=1, keepdims=True) * (1.0 / width)))


ANY = pl.BlockSpec(memory_space=pl.ANY)


def _place():
    return lax.axis_index("x"), lax.axis_index("y"), lax.axis_index("c")


def _other_chips(x, y):
    return [(1 - x, y), (x, 1 - y), (1 - x, 1 - y)]


def _remote(src, dst, send, recv, j, dev):
    return pltpu.make_async_remote_copy(src_ref=src, dst_ref=dst, send_sem=send.at[j], recv_sem=recv.at[j], device_id=dev, device_id_type=MESH)


class _Exchange:
    def __init__(self, ins, outs, n_remote, n_local, build):
        self.ins, self.outs, self.n_remote, self.n_local, self.build = list(ins), list(outs), n_remote, max(n_local, 1), build

    def sems(self):
        return [pltpu.SemaphoreType.DMA((self.n_remote,)), pltpu.SemaphoreType.DMA((self.n_remote,)), pltpu.SemaphoreType.DMA((self.n_local,))]

    def start(self, in_refs, out_refs, sems):
        for cp in self.build(in_refs, out_refs, *sems)[0]:
            cp.start()

    def wait(self, in_refs, out_refs, sems):
        for w in self.build(in_refs, out_refs, *sems)[1]:
            w()


def _gather_exchange(shards):
    def build(ins, outs, send, recv, lsem):
        x, y, c = _place()
        starts, waits = [], []
        for i, (s, o) in enumerate(zip(ins, outs)):
            mine = pltpu.make_async_copy(s, o.at[2 * x + y], lsem.at[i])
            starts.append(mine)
            waits.append(mine.wait)
            for j, (cx, cy) in enumerate(_other_chips(x, y)):
                out = _remote(s, o.at[2 * x + y], send, recv, 3 * i + j, (cx, cy, c))
                starts.append(out)
                waits.append(_remote(s, o.at[2 * cx + cy], send, recv, 3 * i + j, (cx, cy, c)).wait_recv)
                waits.append(out.wait_send)
        return starts, waits

    outs = [jax.ShapeDtypeStruct((N_CHIPS,) + s.shape, s.dtype) for s in shards]
    return _Exchange(shards, outs, 3 * len(shards), len(shards), build)


def _swap_exchange(grads):
    def build(ins, outs, send, recv, lsem):
        x, y, c = _place()
        cps = []
        for i, (g, o) in enumerate(zip(ins, outs)):
            hr = g.shape[1] // 2
            cps.append(_remote(g.at[:, pl.ds((1 - c) * hr, hr), :], o, send, recv, i, (x, y, 1 - c)))
        return cps, [cp.wait for cp in cps]

    outs = [jax.ShapeDtypeStruct((g.shape[0], g.shape[1] // 2, g.shape[2]), g.dtype) for g in grads]
    return _Exchange(grads, outs, len(grads), 0, build)


def _scatter_exchange(sums):
    def build(ins, outs, send, recv, lsem):
        x, y, c = _place()
        cps = []
        for i, (s, o) in enumerate(zip(ins, outs)):
            for j, (cx, cy) in enumerate(_other_chips(x, y)):
                cps.append(_remote(s.at[2 * cx + cy], o.at[j], send, recv, 3 * i + j, (cx, cy, c)))
        return cps, [cp.wait for cp in cps]

    outs = [jax.ShapeDtypeStruct((3,) + s.shape[1:], s.dtype) for s in sums]
    return _Exchange(sums, outs, 3 * len(sums), 0, build)


def _join_exchange(halves):
    def build(ins, outs, send, recv, lsem):
        x, y, c = _place()
        starts, waits = [], []
        for i, (t, o) in enumerate(zip(ins, outs)):
            hr = t.shape[0]
            mine = pltpu.make_async_copy(t, o.at[pl.ds(c * hr, hr), :], lsem.at[i])
            out = _remote(t, o.at[pl.ds(c * hr, hr), :], send, recv, i, (x, y, 1 - c))
            starts += [mine, out]
            waits += [_remote(t, o.at[pl.ds((1 - c) * hr, hr), :], send, recv, i, (x, y, 1 - c)).wait_recv, out.wait_send, mine.wait]
        return starts, waits

    outs = [jax.ShapeDtypeStruct((2 * t.shape[0], t.shape[1]), t.dtype) for t in halves]
    return _Exchange(halves, outs, len(halves), len(halves), build)


def _run_exchange(ex, name):
    n_in, n_out = len(ex.ins), len(ex.outs)

    def body(*refs):
        ins, outs, sems = refs[:n_in], refs[n_in:n_in + n_out], refs[n_in + n_out:]
        ex.start(ins, outs, sems)
        ex.wait(ins, outs, sems)

    return pl.pallas_call(
        body, name=name, out_shape=tuple(ex.outs), in_specs=[ANY] * n_in, out_specs=tuple([ANY] * n_out),
        scratch_shapes=ex.sems(), compiler_params=pltpu.CompilerParams(has_side_effects=True),
    )(*ex.ins)


def _allreduce_small(v):
    def body(v_ref, o_ref, buf, send, recv):
        x, y, c = _place()
        me = 4 * x + 2 * y + c
        buf[me] = v_ref[...]
        out = []
        for j in range(7):
            fx, fy, fc = (j + 1) >> 2 & 1, (j + 1) >> 1 & 1, (j + 1) & 1
            out.append(_remote(v_ref, buf.at[me], send, recv, j, (x ^ fx, y ^ fy, c ^ fc)))
        for cp in out:
            cp.start()
        for j in range(7):
            fx, fy, fc = (j + 1) >> 2 & 1, (j + 1) >> 1 & 1, (j + 1) & 1
            src = 4 * (x ^ fx) + 2 * (y ^ fy) + (c ^ fc)
            _remote(v_ref, buf.at[src], send, recv, j, (x ^ fx, y ^ fy, c ^ fc)).wait_recv()
        for cp in out:
            cp.wait_send()
        acc = buf[0]
        for d in range(1, 8):
            acc = acc + buf[d]
        o_ref[...] = acc

    vm = pl.BlockSpec(memory_space=pltpu.VMEM)
    return pl.pallas_call(
        body, name="allreduce_small", out_shape=jax.ShapeDtypeStruct(v.shape, v.dtype),
        in_specs=[vm], out_specs=vm,
        scratch_shapes=[pltpu.VMEM((8,) + v.shape, v.dtype), pltpu.SemaphoreType.DMA((7,)), pltpu.SemaphoreType.DMA((7,))],
        compiler_params=pltpu.CompilerParams(has_side_effects=True),
    )(v)


def _add_half(g, got, tr):
    n, R, C = g.shape
    hr = R // 2
    nb = hr // tr

    def body(c_ref, g_ref, r_ref, o_ref):
        o_ref[...] = (g_ref[...] + r_ref[...]).astype(BF16)

    c = lax.axis_index("c")
    return pl.pallas_call(
        body, name="add_half",
        grid_spec=pltpu.PrefetchScalarGridSpec(
            num_scalar_prefetch=1, grid=(n, nb),
            in_specs=[pl.BlockSpec((1, tr, C), lambda k, i, c_ref: (k, c_ref[0] * nb + i, 0)),
                      pl.BlockSpec((1, tr, C), lambda k, i, c_ref: (k, i, 0))],
            out_specs=pl.BlockSpec((1, tr, C), lambda k, i, c_ref: (k, i, 0))),
        out_shape=jax.ShapeDtypeStruct((n, hr, C), BF16),
        compiler_params=_params(("arbitrary", "arbitrary")),
    )(jnp.reshape(c, (1,)).astype(jnp.int32), g, got)


def _sum_slabs(g, got, recv, tr):
    n, R, C = g.shape
    hr = R // 2
    nb = hr // tr

    def body(kc_ref, g_ref, s_ref, r_ref, o_ref):
        o_ref[...] = (((g_ref[0] + s_ref[0]) + r_ref[0].astype(F32)) + r_ref[1].astype(F32)) + r_ref[2].astype(F32)

    kc = jnp.stack([2 * lax.axis_index("x") + lax.axis_index("y"), lax.axis_index("c")]).astype(jnp.int32)
    return pl.pallas_call(
        body, name="sum_slabs",
        grid_spec=pltpu.PrefetchScalarGridSpec(
            num_scalar_prefetch=1, grid=(nb,),
            in_specs=[pl.BlockSpec((1, tr, C), lambda i, kc_ref: (kc_ref[0], kc_ref[1] * nb + i, 0)),
                      pl.BlockSpec((1, tr, C), lambda i, kc_ref: (kc_ref[0], i, 0)),
                      pl.BlockSpec((3, tr, C), lambda i, kc_ref: (0, i, 0))],
            out_specs=pl.BlockSpec((tr, C), lambda i, kc_ref: (i, 0))),
        out_shape=jax.ShapeDtypeStruct((hr, C), F32),
        compiler_params=_params(("arbitrary",)),
    )(kc, g, got, recv)


def _row_tile(rows):
    for cand in (256, 184, 176, 144, 128, 64, 32, 16, 8):
        if rows % cand == 0:
            return cand
    return rows


def _in_proj(x, g1, w_in, w_q12, w_k, w_v, gq, gkv, bfg, ct, st, sel, seq, tm):
    T = x.shape[0]
    nsb = seq // tm

    def body(x_ref, g1_ref, win_ref, wq_ref, wk_ref, wv_ref, gq_ref, gkv_ref, b_ref, ct_ref, st_ref, sel_ref,
             h1_ref, qf_ref, kf_ref, vf_ref, qm_ref, km_ref, vm_ref, lat_ref, qn_ref, kvn_ref, carry):
        i = pl.program_id(0)

        @pl.when(i % nsb == 0)
        def _():
            carry[...] = jnp.zeros_like(carry)

        xv = x_ref[...]
        h = (xv * _rms(xv, D_MODEL) * g1_ref[...]).astype(BF16)
        h1_ref[...] = h
        proj = _dot(h, win_ref[...])
        lane = lax.broadcasted_iota(jnp.int32, (tm, LANES), 1)
        low = lane < 64
        misc_a = proj[:, C_MA:C_MB]
        misc_b = proj[:, C_MB:C_END]

        z = misc_a + b_ref[...]
        lf = jnp.where(lane < HEADS, jnp.minimum(z, 0.0) - jnp.log1p(jnp.exp(-jnp.abs(z))), 0.0)
        rr = lax.broadcasted_iota(jnp.int32, (tm, tm), 0)
        cc = lax.broadcasted_iota(jnp.int32, (tm, tm), 1)
        tri = (rr >= cc).astype(BF16)
        a0, a1, a2 = _split3(lf)
        c = _dot(tri, a0) + _dot(tri, a1) + _dot(tri, a2) + carry[0:1, :]
        carry[0:1, :] = c[tm - 1:tm, :]
        c0, c1, c2 = _split3(c)
        cpl = _dot(jnp.concatenate([c0, c1, c2], axis=1), sel_ref[...])
        qpad = jnp.where((lane >= 64) & (lane < 67), -1.0, 0.0)
        for j in range(PAIRS):
            qc = proj[:, C_FQ + LANES * j:C_FQ + LANES * (j + 1)] * FOX_SCALE
            kc = proj[:, C_FK + LANES * j:C_FK + LANES * (j + 1)]
            e, o = 2 * LANES * j, 2 * LANES * j + LANES
            qf_ref[:, e:e + LANES] = jnp.where(low, qc, qpad).astype(BF16)
            qf_ref[:, o:o + LANES] = jnp.where(low, pltpu.roll(qc, 64, 1), qpad).astype(BF16)
            kf_ref[:, e:e + LANES] = jnp.where(low, kc, cpl[:, e:e + LANES]).astype(BF16)
            kf_ref[:, o:o + LANES] = jnp.where(low, pltpu.roll(kc, 64, 1), cpl[:, o:o + LANES]).astype(BF16)
        vf_ref[...] = proj[:, C_FV:C_QL].astype(BF16)

        ql = proj[:, C_QL:C_KVL]
        kvl = proj[:, C_KVL:C_MA]
        qn = (ql * _rms(ql, Q_RANK) * gq_ref[...]).astype(BF16)
        kvn = (kvl * _rms(kvl, KV_RANK) * gkv_ref[...]).astype(BF16)
        lat_ref[...] = proj[:, C_QL:C_MB]
        qn_ref[...] = qn
        kvn_ref[...] = kvn
        q12 = _dot(qn, wq_ref[...])
        kn = _dot(kvn, wk_ref[...])
        ctv = ct_ref[...]
        stv = st_ref[...]
        cq = (jnp.where(low, 1.0, 0.0) + ctv) * MLA_SCALE
        sq = stv * MLA_SCALE
        kpe = misc_a * ctv + misc_b * stv
        for hd in range(HEADS):
            s0 = LANES * hd
            qm_ref[:, s0:s0 + LANES] = (q12[:, s0:s0 + LANES] * cq + q12[:, 1024 + s0:1024 + s0 + LANES] * sq).astype(BF16)
            km_ref[:, s0:s0 + LANES] = (kn[:, s0:s0 + LANES] + kpe).astype(BF16)
        vm_ref[...] = _dot(kvn, wv_ref[...]).astype(BF16)

    row = lambda w: pl.BlockSpec((tm, w), lambda i: (i, 0))
    out_shape = (
        jax.ShapeDtypeStruct((T, D_MODEL), BF16),
        jax.ShapeDtypeStruct((T, 1024), BF16), jax.ShapeDtypeStruct((T, 1024), BF16), jax.ShapeDtypeStruct((T, 512), BF16),
        jax.ShapeDtypeStruct((T, 1024), BF16), jax.ShapeDtypeStruct((T, 1024), BF16), jax.ShapeDtypeStruct((T, 512), BF16),
        jax.ShapeDtypeStruct((T, 512), F32),
        jax.ShapeDtypeStruct((T, Q_RANK), BF16), jax.ShapeDtypeStruct((T, KV_RANK), BF16),
    )
    return pl.pallas_call(
        body, name="in_proj", grid=(T // tm,), out_shape=out_shape,
        in_specs=[row(D_MODEL), _full(g1.shape), _full(w_in.shape), _full(w_q12.shape), _full(w_k.shape), _full(w_v.shape),
                  _full(gq.shape), _full(gkv.shape), _full(bfg.shape), row(LANES), row(LANES), _full(sel.shape)],
        out_specs=[row(D_MODEL), row(1024), row(1024), row(512), row(1024), row(1024), row(512), row(512), row(Q_RANK), row(KV_RANK)],
        scratch_shapes=[pltpu.VMEM((8, LANES), F32)],
        compiler_params=_params(("arbitrary",)),
    )(x, g1, w_in, w_q12, w_k, w_v, gq, gkv, bfg, ct, st, sel)


def _attn_fwd(q, k, v, nb, seq, tq, name, ex=None):
    T = q.shape[0]
    nq = seq // tq
    n_in, n_out = (len(ex.ins), len(ex.outs)) if ex else (0, 0)

    def body(*refs):
        q_ref, k_ref, v_ref = refs[0:3]
        o_ref, lse_ref = refs[3 + n_in:5 + n_in]
        b, pr, qi = pl.program_id(0), pl.program_id(1), pl.program_id(2)
        if ex:
            ex_refs = (refs[3:3 + n_in], refs[5 + n_in:5 + n_in + n_out], refs[5 + n_in + n_out:])

            @pl.when((b == 0) & (pr == 0) & (qi == 0))
            def _():
                ex.start(*ex_refs)

        lane = lax.broadcasted_iota(jnp.int32, (tq, LANES), 1)
        rr = lax.broadcasted_iota(jnp.int32, (tq, tq), 0)
        cc = lax.broadcasted_iota(jnp.int32, (tq, tq), 1)
        outs = []
        for hh in range(2):
            qv = q_ref[:, LANES * hh:LANES * (hh + 1)]

            def step(kj, carry, masked, hh=hh, qv=qv):
                m, l, acc = carry
                rows = pl.ds(pl.multiple_of(kj * tq, tq), tq)
                s = _dot_nt(qv, k_ref[rows, LANES * hh:LANES * (hh + 1)])
                if masked:
                    s = jnp.where(cc <= rr, s, NEG)
                m_new = jnp.maximum(m, jnp.max(s, axis=1, keepdims=True))
                alpha = jnp.exp(m - m_new)
                p = jnp.exp(s - m_new)
                l = alpha * l + jnp.sum(p, axis=1, keepdims=True)
                acc = alpha * acc + _dot(p.astype(BF16), v_ref[rows, :])
                return m_new, l, acc

            init = (jnp.full((tq, 1), NEG, F32), jnp.zeros((tq, 1), F32), jnp.zeros((tq, LANES), F32))
            carry = lax.fori_loop(0, qi, functools.partial(step, masked=False), init)
            m, l, acc = step(qi, carry, True)
            outs.append(acc / l)
            lse_ref[:, LANES * hh:LANES * (hh + 1)] = jnp.broadcast_to(m + jnp.log(l), (tq, LANES))
        o_ref[...] = jnp.where(lane < 64, outs[0], outs[1])

        if ex:
            @pl.when((b == nb - 1) & (pr == PAIRS - 1) & (qi == nq - 1))
            def _():
                ex.wait(*ex_refs)

    res = pl.pallas_call(
        body, name=name, grid=(nb, PAIRS, nq),
        out_shape=(jax.ShapeDtypeStruct((T, 512), F32), jax.ShapeDtypeStruct((T, 1024), F32)) + tuple(ex.outs if ex else ()),
        in_specs=[pl.BlockSpec((tq, 2 * LANES), lambda b, p, i: (b * nq + i, p)),
                  pl.BlockSpec((seq, 2 * LANES), lambda b, p, i: (b, p)),
                  pl.BlockSpec((seq, LANES), lambda b, p, i: (b, p))] + [ANY] * n_in,
        out_specs=[pl.BlockSpec((tq, LANES), lambda b, p, i: (b * nq + i, p)),
                   pl.BlockSpec((tq, 2 * LANES), lambda b, p, i: (b * nq + i, p))] + [ANY] * n_out,
        scratch_shapes=ex.sems() if ex else [],
        compiler_params=_params(("arbitrary", "arbitrary", "arbitrary")),
    )(q, k, v, *(ex.ins if ex else ()))
    return res[0], res[1], list(res[2:])


def _attn_bwd(q, k, v, o, do, lse, nb, seq, tq, name, key_bias, ex=None):
    T = q.shape[0]
    nq = seq // tq
    n_in, n_out = (len(ex.ins), len(ex.outs)) if ex else (0, 0)

    def body(*refs):
        q_ref, k_ref, v_ref, o_ref, do_ref, lse_ref = refs[0:6]
        dq_ref, dk_ref, dv_ref = refs[6 + n_in:9 + n_in]
        dsc, rsum = refs[9 + n_in + n_out:11 + n_in + n_out]
        b, pr, kj = pl.program_id(0), pl.program_id(1), pl.program_id(2)
        if ex:
            ex_refs = (refs[6:6 + n_in], refs[9 + n_in:9 + n_in + n_out], refs[11 + n_in + n_out:])

            @pl.when((b == 0) & (pr == 0) & (kj == 0))
            def _():
                ex.start(*ex_refs)

        lane_s = lax.broadcasted_iota(jnp.int32, (seq, LANES), 1)
        lane = lax.broadcasted_iota(jnp.int32, (tq, LANES), 1)
        rr = lax.broadcasted_iota(jnp.int32, (tq, tq), 0)
        cc = lax.broadcasted_iota(jnp.int32, (tq, tq), 1)

        @pl.when(kj == 0)
        def _():
            dq_ref[...] = jnp.zeros_like(dq_ref)
            prod = do_ref[...].astype(F32) * o_ref[...]
            d0 = jnp.sum(jnp.where(lane_s < 64, prod, 0.0), axis=1, keepdims=True)
            d1 = jnp.sum(jnp.where(lane_s < 64, 0.0, prod), axis=1, keepdims=True)
            dsc[0] = jnp.broadcast_to(d0, (seq, LANES))
            dsc[1] = jnp.broadcast_to(d1, (seq, LANES))
            if key_bias:
                rsum[...] = jnp.zeros_like(rsum)

        vv = v_ref[...]
        dv_acc = jnp.zeros((tq, LANES), F32)
        for hh in range(2):
            kv = k_ref[:, LANES * hh:LANES * (hh + 1)]
            keep = (lane < 64) if hh == 0 else (lane >= 64)

            def step(qi, carry, masked, hh=hh, kv=kv, keep=keep):
                dk_acc, dv_acc, col = carry
                rows = pl.ds(pl.multiple_of(qi * tq, tq), tq)
                qv = q_ref[rows, LANES * hh:LANES * (hh + 1)]
                dom = jnp.where(keep, do_ref[rows, :], jnp.zeros((), BF16))
                s = _dot_nt(qv, kv)
                if masked:
                    s = jnp.where(cc <= rr, s, NEG)
                p = jnp.exp(s - lse_ref[rows, LANES * hh:LANES * hh + 1])
                dp = _dot_nt(dom, vv)
                ds32 = p * (dp - dsc[hh, rows, 0:1])
                if key_bias:
                    col = col + jnp.sum(ds32, axis=0, keepdims=True)
                    rsum[hh, rows, :] += jnp.broadcast_to(jnp.sum(ds32, axis=1, keepdims=True), (tq, LANES))
                ds = ds32.astype(BF16)
                dv_acc = dv_acc + _dot_tn(p.astype(BF16), dom)
                dk_acc = dk_acc + _dot_tn(ds, qv)
                dq_ref[rows, LANES * hh:LANES * (hh + 1)] += _dot(ds, kv)
                return dk_acc, dv_acc, col

            carry = step(kj, (jnp.zeros((tq, LANES), F32), dv_acc, jnp.zeros((1, tq), F32)), True)
            dk_acc, dv_acc, col = lax.fori_loop(kj + 1, nq, functools.partial(step, masked=False), carry)
            if key_bias:
                first = (lax.broadcasted_iota(jnp.int32, (8, LANES), 0) == 0).astype(BF16)
                colsum = sum(_dot_tn(jnp.broadcast_to(piece, (8, tq)), first) for piece in _split3(col))
                dk_acc = jnp.where(lane == 64, -colsum, dk_acc)
            dk_ref[:, LANES * hh:LANES * (hh + 1)] = dk_acc
        dv_ref[...] = dv_acc

        if key_bias:
            @pl.when(kj == nq - 1)
            def _():
                for hh in range(2):
                    blk = dq_ref[:, LANES * hh:LANES * (hh + 1)]
                    dq_ref[:, LANES * hh:LANES * (hh + 1)] = jnp.where(lane_s == 64, rsum[hh], blk)

        if ex:
            @pl.when((b == nb - 1) & (pr == PAIRS - 1) & (kj == nq - 1))
            def _():
                ex.wait(*ex_refs)

    per_seq = lambda w: pl.BlockSpec((seq, w), lambda b, p, j: (b, p))
    per_blk = lambda w: pl.BlockSpec((tq, w), lambda b, p, j: (b * nq + j, p))
    res = pl.pallas_call(
        body, name=name, grid=(nb, PAIRS, nq),
        out_shape=(jax.ShapeDtypeStruct((T, 1024), F32), jax.ShapeDtypeStruct((T, 1024), F32), jax.ShapeDtypeStruct((T, 512), F32))
        + tuple(ex.outs if ex else ()),
        in_specs=[per_seq(2 * LANES), per_blk(2 * LANES), per_blk(LANES), per_seq(LANES), per_seq(LANES), per_seq(2 * LANES)] + [ANY] * n_in,
        out_specs=[per_seq(2 * LANES), per_blk(2 * LANES), per_blk(LANES)] + [ANY] * n_out,
        scratch_shapes=[pltpu.VMEM((2, seq, LANES), F32), pltpu.VMEM((2, seq, LANES) if key_bias else (2, 8, LANES), F32)]
        + (ex.sems() if ex else []),
        compiler_params=_params(("arbitrary", "arbitrary", "arbitrary")),
    )(q, k, v, o, do, lse, *(ex.ins if ex else ()))
    return res[0], res[1], res[2], list(res[3:])


def _mid(of, om, x, tgt, g_fo, g_mo, g2, g3, w_o, w_g, w_u, w_d, tm):
    T = x.shape[0]

    def body(of_ref, om_ref, x_ref, t_ref, gfo_ref, gmo_ref, g2_ref, g3_ref, wo_ref, wg_ref, wu_ref, wd_ref,
             a_ref, h2_ref, hid_ref, dg_ref, du_ref, dx3_ref, dx2_ref, dof_ref, dom_ref, st_ref):
        i = pl.program_id(0)

        @pl.when(i == 0)
        def _():
            st_ref[...] = jnp.zeros_like(st_ref)

        ofv, omv = of_ref[...], om_ref[...]
        rf, rm = _rms(ofv, FOX_W), _rms(omv, FOX_W)
        fhat, mhat = ofv * rf, omv * rm
        a = jnp.concatenate([fhat * gfo_ref[...], mhat * gmo_ref[...]], axis=1).astype(BF16)
        a_ref[...] = a
        x2 = x_ref[...] + _dot(a, wo_ref[...])
        r2 = _rms(x2, D_MODEL)
        xh2 = x2 * r2
        h2 = (xh2 * g2_ref[...]).astype(BF16)
        h2_ref[...] = h2
        x3 = x2
        kept = []
        for k in range(N_CHIPS):
            gt = _dot(h2, wg_ref[k])
            up = _dot(h2, wu_ref[k])
            sg = jax.nn.sigmoid(gt)
            sl = gt * sg
            hid = (sl * up).astype(BF16)
            hid_ref[k] = hid
            x3 = x3 + _dot(hid, wd_ref[k])
            kept.append((up * (sg * (1.0 + gt * (1.0 - sg))), sl))
        r3 = _rms(x3, D_MODEL)
        xh3 = x3 * r3
        diff = xh3 * g3_ref[...] - t_ref[...]
        dy = diff * (1.0 / D_MODEL)
        st_ref[3:4, :] += jnp.sum(diff * diff, axis=0, keepdims=True) * (0.5 / D_MODEL)
        st_ref[0:1, :] += jnp.sum(dy * xh3, axis=0, keepdims=True)
        dx3 = _rms_bwd(dy, xh3, r3, g3_ref[...], D_MODEL)
        dx3b = dx3.astype(BF16)
        dx3_ref[...] = dx3b
        dh2 = jnp.zeros((tm, D_MODEL), F32)
        for k in range(N_CHIPS):
            dhid = _dot_nt(dx3b, wd_ref[k])
            dg = (dhid * kept[k][0]).astype(BF16)
            du = (dhid * kept[k][1]).astype(BF16)
            dg_ref[k] = dg
            du_ref[k] = du
            dh2 = dh2 + _dot_nt(dg, wg_ref[k]) + _dot_nt(du, wu_ref[k])
        st_ref[1:2, :] += jnp.sum(dh2 * xh2, axis=0, keepdims=True)
        dx2 = dx3 + _rms_bwd(dh2, xh2, r2, g2_ref[...], D_MODEL)
        dx2_ref[...] = dx2
        da = _dot_nt(dx2.astype(BF16), wo_ref[...])
        daf, dam = da[:, 0:FOX_W], da[:, FOX_W:2 * FOX_W]
        st_ref[2:3, 0:FOX_W] += jnp.sum(daf * fhat, axis=0, keepdims=True)
        st_ref[2:3, FOX_W:2 * FOX_W] += jnp.sum(dam * mhat, axis=0, keepdims=True)
        dof_ref[...] = _rms_bwd(daf, fhat, rf, gfo_ref[...], FOX_W).astype(BF16)
        dom_ref[...] = _rms_bwd(dam, mhat, rm, gmo_ref[...], FOX_W).astype(BF16)

    row = lambda w: pl.BlockSpec((tm, w), lambda i: (i, 0))
    chunked = pl.BlockSpec((N_CHIPS, tm, FF_CHUNK), lambda i: (0, i, 0))
    ff = jax.ShapeDtypeStruct((N_CHIPS, T, FF_CHUNK), BF16)
    out_shape = (
        jax.ShapeDtypeStruct((T, 1024), BF16), jax.ShapeDtypeStruct((T, 1024), BF16), ff, ff, ff,
        jax.ShapeDtypeStruct((T, 1024), BF16), jax.ShapeDtypeStruct((T, 1024), F32),
        jax.ShapeDtypeStruct((T, 512), BF16), jax.ShapeDtypeStruct((T, 512), BF16), jax.ShapeDtypeStruct((8, 1024), F32),
    )
    return pl.pallas_call(
        body, name="mid", grid=(T // tm,), out_shape=out_shape,
        in_specs=[row(512), row(512), row(1024), row(1024), _full(g_fo.shape), _full(g_mo.shape), _full(g2.shape), _full(g3.shape),
                  _full(w_o.shape), _full(w_g.shape), _full(w_u.shape), _full(w_d.shape)],
        out_specs=[row(1024), row(1024), chunked, chunked, chunked, row(1024), row(1024), row(512), row(512),
                   pl.BlockSpec((8, 1024), lambda i: (0, 0))],
        compiler_params=_params(("arbitrary",)),
    )(of, om, x, tgt, g_fo, g_mo, g2, g3, w_o, w_g, w_u, w_d)


def _in_bwd(dqf, dkf, dvf, dqm, dkm, dvm, lat, x, dx2, g1, gq, gkv, bfg, ct, st, sel_t, w_in, w_q12, w_kv, seq, tm, ex=None):
    T = x.shape[0]
    nblk = T // tm
    nsb = seq // tm
    n_in, n_out = (len(ex.ins), len(ex.outs)) if ex else (0, 0)

    def body(*refs):
        (dqf_ref, dkf_ref, dvf_ref, dqm_ref, dkm_ref, dvm_ref, lat_ref, x_ref, dx2_ref, g1_ref, gq_ref, gkv_ref, b_ref,
         ct_ref, st_ref, selt_ref, win_ref, wq_ref, wkv_ref) = refs[0:19]
        dx_ref, dproj_ref, dq12_ref, dkv_ref, stat_ref = refs[19 + n_in:24 + n_in]
        carry = refs[24 + n_in + n_out]
        i = pl.program_id(0)
        if ex:
            ex_refs = (refs[19:19 + n_in], refs[24 + n_in:24 + n_in + n_out], refs[25 + n_in + n_out:])

            @pl.when(i == 0)
            def _():
                ex.start(*ex_refs)

        @pl.when(i == 0)
        def _():
            stat_ref[...] = jnp.zeros_like(stat_ref)

        @pl.when(i % nsb == 0)
        def _():
            carry[...] = jnp.zeros_like(carry)

        lane = lax.broadcasted_iota(jnp.int32, (tm, LANES), 1)
        low = lane < 64
        ctv, stv = ct_ref[...], st_ref[...]

        for j in range(PAIRS):
            e, o = 2 * LANES * j, 2 * LANES * j + LANES
            dq = jnp.where(low, dqf_ref[:, e:e + LANES], 0.0) + pltpu.roll(jnp.where(low, dqf_ref[:, o:o + LANES], 0.0), 64, 1)
            dk = jnp.where(low, dkf_ref[:, e:e + LANES], 0.0) + pltpu.roll(jnp.where(low, dkf_ref[:, o:o + LANES], 0.0), 64, 1)
            dproj_ref[:, C_FQ + LANES * j:C_FQ + LANES * (j + 1)] = (dq * FOX_SCALE).astype(BF16)
            dproj_ref[:, C_FK + LANES * j:C_FK + LANES * (j + 1)] = dk.astype(BF16)
        dproj_ref[:, C_FV:C_QL] = dvf_ref[...].astype(BF16)
        dcv = dkf_ref[...] + dqf_ref[...]
        k_hi = dcv.astype(BF16)
        k_lo = (dcv - k_hi.astype(F32)).astype(BF16)
        dc = _dot(k_hi, selt_ref[...]) + _dot(k_lo, selt_ref[...])
        rr = lax.broadcasted_iota(jnp.int32, (tm, tm), 0)
        cc = lax.broadcasted_iota(jnp.int32, (tm, tm), 1)
        triu = (cc >= rr).astype(BF16)
        a0, a1, a2 = _split3(dc)
        dlf = _dot(triu, a0) + _dot(triu, a1) + _dot(triu, a2) + carry[0:1, :]
        carry[0:1, :] = dlf[0:1, :]
        misc_a = lat_ref[:, Q_RANK + KV_RANK:Q_RANK + KV_RANK + LANES]
        z = misc_a + b_ref[...]
        dz = jnp.where(lane < HEADS, dlf * jax.nn.sigmoid(-z), 0.0)
        stat_ref[3:4, 0:LANES] += jnp.sum(dz, axis=0, keepdims=True)

        cq = (jnp.where(low, 1.0, 0.0) + ctv) * MLA_SCALE
        sq = stv * MLA_SCALE
        dkpe = jnp.zeros((tm, LANES), F32)
        for hd in range(HEADS):
            s0 = LANES * hd
            dqh = dqm_ref[:, s0:s0 + LANES]
            dq12_ref[:, s0:s0 + LANES] = (dqh * cq).astype(BF16)
            dq12_ref[:, 1024 + s0:1024 + s0 + LANES] = (dqh * sq).astype(BF16)
            dkpe = dkpe + dkm_ref[:, s0:s0 + LANES]
        dkv_ref[:, 0:1024] = dkm_ref[...].astype(BF16)
        dkv_ref[:, 1024:1536] = dvm_ref[...].astype(BF16)
        dproj_ref[:, C_MA:C_MB] = (dz + dkpe * ctv).astype(BF16)
        dproj_ref[:, C_MB:C_END] = (dkpe * stv).astype(BF16)
        dqn = _dot_nt(dq12_ref[...], wq_ref[...])
        dkvn = _dot_nt(dkv_ref[...], wkv_ref[...])
        ql = lat_ref[:, 0:Q_RANK]
        kvl = lat_ref[:, Q_RANK:Q_RANK + KV_RANK]
        rq, rkv = _rms(ql, Q_RANK), _rms(kvl, KV_RANK)
        qhat, kvhat = ql * rq, kvl * rkv
        stat_ref[1:2, 0:Q_RANK] += jnp.sum(dqn * qhat, axis=0, keepdims=True)
        stat_ref[2:3, 0:KV_RANK] += jnp.sum(dkvn * kvhat, axis=0, keepdims=True)
        dproj_ref[:, C_QL:C_KVL] = _rms_bwd(dqn, qhat, rq, gq_ref[...], Q_RANK).astype(BF16)
        dproj_ref[:, C_KVL:C_MA] = _rms_bwd(dkvn, kvhat, rkv, gkv_ref[...], KV_RANK).astype(BF16)

        dh1 = _dot_nt(dproj_ref[...], win_ref[...])
        xv = x_ref[...]
        r1 = _rms(xv, D_MODEL)
        xh = xv * r1
        stat_ref[0:1, :] += jnp.sum(dh1 * xh, axis=0, keepdims=True)
        dx_ref[...] = dx2_ref[...] + _rms_bwd(dh1, xh, r1, g1_ref[...], D_MODEL)

        if ex:
            @pl.when(i == nblk - 1)
            def _():
                ex.wait(*ex_refs)

    rev = lambda w: pl.BlockSpec((tm, w), lambda i: (nblk - 1 - i, 0))
    out_shape = (
        jax.ShapeDtypeStruct((T, 1024), F32), jax.ShapeDtypeStruct((T, C_END), BF16), jax.ShapeDtypeStruct((T, 2048), BF16),
        jax.ShapeDtypeStruct((T, 1536), BF16), jax.ShapeDtypeStruct((8, 1024), F32),
    ) + tuple(ex.outs if ex else ())
    res = pl.pallas_call(
        body, name="in_bwd", grid=(nblk,), out_shape=out_shape,
        in_specs=[rev(1024), rev(1024), rev(512), rev(1024), rev(1024), rev(512), rev(512), rev(1024), rev(1024),
                  _full(g1.shape), _full(gq.shape), _full(gkv.shape), _full(bfg.shape), rev(LANES), rev(LANES), _full(sel_t.shape),
                  _full(w_in.shape), _full(w_q12.shape), _full(w_kv.shape)] + [ANY] * n_in,
        out_specs=[rev(1024), rev(C_END), rev(2048), rev(1536), pl.BlockSpec((8, 1024), lambda i: (0, 0))] + [ANY] * n_out,
        scratch_shapes=[pltpu.VMEM((8, LANES), F32)] + (ex.sems() if ex else []),
        compiler_params=_params(("arbitrary",)),
    )(dqf, dkf, dvf, dqm, dkm, dvm, lat, x, dx2, g1, gq, gkv, bfg, ct, st, sel_t, w_in, w_q12, w_kv, *(ex.ins if ex else ()))
    return res[0], res[1], res[2], res[3], res[4], list(res[5:])


def _wgrad(a, b, tn, tt, name, ex=None):
    T, K = a.shape
    N = b.shape[1]
    n_in, n_out = (len(ex.ins), len(ex.outs)) if ex else (0, 0)
    gn, gt = N // tn, T // tt

    def body(*refs):
        a_ref, b_ref, o_ref = refs[0], refs[1], refs[2 + n_in]
        n, t = pl.program_id(0), pl.program_id(1)
        if ex:
            ex_refs = (refs[2:2 + n_in], refs[3 + n_in:3 + n_in + n_out], refs[3 + n_in + n_out:])

            @pl.when((n == 0) & (t == 0))
            def _():
                ex.start(*ex_refs)

        @pl.when(t == 0)
        def _():
            o_ref[...] = jnp.zeros_like(o_ref)

        o_ref[...] += _dot_tn(a_ref[...].astype(BF16), b_ref[...].astype(BF16))

        if ex:
            @pl.when((n == gn - 1) & (t == gt - 1))
            def _():
                ex.wait(*ex_refs)

    res = pl.pallas_call(
        body, name=name, grid=(gn, gt), out_shape=(jax.ShapeDtypeStruct((K, N), F32),) + tuple(ex.outs if ex else ()),
        in_specs=[pl.BlockSpec((tt, K), lambda n, t: (t, 0)), pl.BlockSpec((tt, tn), lambda n, t: (t, n))] + [ANY] * n_in,
        out_specs=[pl.BlockSpec((K, tn), lambda n, t: (0, n))] + [ANY] * n_out,
        scratch_shapes=ex.sems() if ex else [],
        compiler_params=_params(("arbitrary", "arbitrary")),
    )(a, b, *(ex.ins if ex else ()))
    return (res[0], list(res[1:])) if ex else res[0]


def _wgrad_chunked(a, b, tt, name):
    a_chunked = a.ndim == 3
    T, K = a.shape[-2:]
    N = b.shape[-1]

    def body(a_ref, b_ref, o_ref):
        @pl.when(pl.program_id(1) == 0)
        def _():
            o_ref[...] = jnp.zeros_like(o_ref)

        o_ref[...] += _dot_tn(a_ref[...], b_ref[...])

    plain = lambda w: pl.BlockSpec((tt, w), lambda k, t: (t, 0))
    chunk = lambda w: pl.BlockSpec((None, tt, w), lambda k, t: (k, t, 0))
    return pl.pallas_call(
        body, name=name, grid=(N_CHIPS, T // tt), out_shape=jax.ShapeDtypeStruct((N_CHIPS, K, N), F32),
        in_specs=[chunk(K) if a_chunked else plain(K), plain(N) if a_chunked else chunk(N)],
        out_specs=pl.BlockSpec((None, K, N), lambda k, t: (k, 0, 0)),
        compiler_params=_params(("arbitrary", "arbitrary")),
    )(a, b)


def _adamw(w, g, m, v, name):
    R, C = w.shape
    tr = _row_tile(R)

    def body(w_ref, g_ref, m_ref, v_ref, d_ref, nm_ref, nv_ref):
        gv = g_ref[...]
        nm = ADAM_B1 * m_ref[...] + (1.0 - ADAM_B1) * gv
        nv = ADAM_B2 * v_ref[...] + (1.0 - ADAM_B2) * (gv * gv)
        m_hat = nm / (1.0 - ADAM_B1 ** ADAM_STEP)
        v_hat = nv / (1.0 - ADAM_B2 ** ADAM_STEP)
        d_ref[...] = -ADAM_LR * (m_hat / (jnp.sqrt(v_hat) + ADAM_EPS) + ADAM_WD * w_ref[...])
        nm_ref[...] = nm
        nv_ref[...] = nv

    blk = pl.BlockSpec((tr, C), lambda i: (i, 0))
    sh = jax.ShapeDtypeStruct((R, C), F32)
    return pl.pallas_call(
        body, name=name, grid=(R // tr,), out_shape=(sh, sh, sh),
        in_specs=[blk, blk, blk, blk], out_specs=[blk, blk, blk],
        compiler_params=_params(("arbitrary",)),
    )(w, g, m, v)


def _arrange(w):
    win = w["w_in"]
    dt = win.dtype
    z = lambda r, c: jnp.zeros((r, c), dt)
    zh = lambda c: jnp.zeros((Q_RANK, HEADS, c), dt)
    kr1, kr2 = win[:, 1928:1944], win[:, 1944:1960]
    misc_a = jnp.concatenate([win[:, 1536:1544], z(1024, 56), kr1, kr2, z(1024, 32)], axis=1)
    misc_b = jnp.concatenate([z(1024, 64), kr2, kr1, z(1024, 32)], axis=1)
    w_in = jnp.concatenate([win[:, 0:1536], win[:, 1544:1928], misc_a, misc_b], axis=1)
    wq = w["w_uq"].reshape(Q_RANK, HEADS, 96)
    q1 = jnp.concatenate([wq, zh(32)], axis=2).reshape(Q_RANK, 1024)
    q2 = jnp.concatenate([zh(64), wq[:, :, 80:96], wq[:, :, 64:80], zh(32)], axis=2).reshape(Q_RANK, 1024)
    wkv = w["w_ukv"].reshape(KV_RANK, HEADS, 128)
    wk = jnp.concatenate([wkv[:, :, 0:64], jnp.zeros((KV_RANK, HEADS, 64), dt)], axis=2).reshape(KV_RANK, 1024)
    wv = wkv[:, :, 64:128].reshape(KV_RANK, 512)
    return dict(w_in=w_in, w_q12=jnp.concatenate([q1, q2], axis=1), w_k=wk, w_v=wv, w_kv=jnp.concatenate([wk, wv], axis=1))


def _unarrange(g_in, g_q12, g_kv):
    kr1 = g_in[:, C_MA + 64:C_MA + 80] + g_in[:, C_MB + 80:C_MB + 96]
    kr2 = g_in[:, C_MA + 80:C_MA + 96] + g_in[:, C_MB + 64:C_MB + 80]
    w_in = jnp.concatenate([g_in[:, 0:1536], g_in[:, C_MA:C_MA + 8], g_in[:, 1536:1920], kr1, kr2], axis=1)
    g1 = g_q12[:, 0:1024].reshape(Q_RANK, HEADS, 128)
    g2 = g_q12[:, 1024:2048].reshape(Q_RANK, HEADS, 128)
    w_uq = jnp.concatenate([g1[:, :, 0:64], g1[:, :, 64:80] + g2[:, :, 80:96], g1[:, :, 80:96] + g2[:, :, 64:80]], axis=2).reshape(Q_RANK, 768)
    gk = g_kv[:, 0:1024].reshape(KV_RANK, HEADS, 128)
    gv = g_kv[:, 1024:1536].reshape(KV_RANK, HEADS, 64)
    w_ukv = jnp.concatenate([gk[:, :, 0:64], gv], axis=2).reshape(KV_RANK, 1024)
    return dict(w_in=w_in, w_uq=w_uq, w_ukv=w_ukv)


def _selectors():
    sel = np.zeros((384, 1024), np.float32)
    sel_t = np.zeros((1024, LANES), np.float32)
    for h in range(HEADS):
        for piece in range(3):
            sel[LANES * piece + h, LANES * h + 64 + piece] = 1.0
        sel_t[LANES * h + 64, h] = 1.0
    return jnp.asarray(sel, BF16), jnp.asarray(sel_t, BF16)


def _rope_tables(positions):
    inv_freq = 10000.0 ** (-jnp.arange(0, ROPE, 2, dtype=F32) / ROPE)
    ang = positions.reshape(-1).astype(F32)[:, None] * inv_freq[None, :]
    cos, sin = jnp.cos(ang), jnp.sin(ang)
    z64, z32 = jnp.zeros((ang.shape[0], 64), F32), jnp.zeros((ang.shape[0], 32), F32)
    return jnp.concatenate([z64, cos, cos, z32], axis=1), jnp.concatenate([z64, -sin, sin, z32], axis=1)


def _pack_shards(t):
    flat = jnp.concatenate([t[n].reshape(-1) for n in HEAD3])
    return jnp.pad(flat, (0, PACK_ROWS * 1024 - flat.shape[0])).reshape(PACK_ROWS, 1024)


def _unpack_full(slabs):
    flat = slabs.reshape(N_CHIPS, -1)
    out, off = {}, 0
    for n in HEAD3:
        r, c = FULL_SHAPES[n]
        cnt = r * c // N_CHIPS
        out[n] = flat[:, off:off + cnt].reshape(N_CHIPS, r, c // N_CHIPS).transpose(1, 0, 2).reshape(r, c)
        off += cnt
    return out


def _pack_full(g):
    parts = []
    for n in HEAD3:
        r, c = FULL_SHAPES[n]
        parts.append(g[n].reshape(r, N_CHIPS, c // N_CHIPS).transpose(1, 0, 2).reshape(N_CHIPS, -1))
    flat = jnp.concatenate(parts, axis=1)
    return jnp.pad(flat, ((0, 0), (0, PACK_ROWS * 1024 - flat.shape[1]))).reshape(N_CHIPS, PACK_ROWS, 1024)


def _unpack_shard(slab, like):
    flat = slab.reshape(-1)
    out, off = {}, 0
    for n in HEAD3:
        cnt = like[n].size
        out[n] = flat[off:off + cnt].reshape(like[n].shape)
        off += cnt
    return out


def _reduce_tail(gpack):
    tr = _row_tile(gpack.shape[1] // 2)
    got, = _run_exchange(_swap_exchange([gpack]), "tail_swap")
    sums = _add_half(gpack, got, tr)
    recv, = _run_exchange(_scatter_exchange([sums]), "tail_scatter")
    joined, = _run_exchange(_join_exchange([_sum_slabs(gpack, got, recv, tr)]), "tail_join")
    return joined


SMALL_ROWS = {"norm_mix_g": (0, 1024), "norm_ffn_g": (1, 1024), "final_norm_g": (2, 1024), "q_norm_g": (4, 256),
              "kv_norm_g": (5, 128), "b_fgate": (6, 8)}


def kernel(x, positions, norm_mix_g, w_in, b_fgate, q_norm_g, w_uq, kv_norm_g, w_ukv, fox_out_g, mla_out_g, w_o, norm_ffn_g, w_gate, w_up, w_down, final_norm_g, loss_target, m_norm_mix_g, m_w_in, m_b_fgate, m_q_norm_g, m_w_uq, m_kv_norm_g, m_w_ukv, m_fox_out_g, m_mla_out_g, m_w_o, m_norm_ffn_g, m_w_gate, m_w_up, m_w_down, m_final_norm_g, v_norm_mix_g, v_w_in, v_b_fgate, v_q_norm_g, v_w_uq, v_kv_norm_g, v_w_ukv, v_fox_out_g, v_mla_out_g, v_w_o, v_norm_ffn_g, v_w_gate, v_w_up, v_w_down, v_final_norm_g):
    names = ["norm_mix_g", "w_in", "b_fgate", "q_norm_g", "w_uq", "kv_norm_g", "w_ukv", "fox_out_g", "mla_out_g", "w_o",
             "norm_ffn_g", "w_gate", "w_up", "w_down", "final_norm_g"]
    wts = dict(zip(names, [norm_mix_g, w_in, b_fgate, q_norm_g, w_uq, kv_norm_g, w_ukv, fox_out_g, mla_out_g, w_o, norm_ffn_g,
                           w_gate, w_up, w_down, final_norm_g]))
    mom = dict(zip(names, [m_norm_mix_g, m_w_in, m_b_fgate, m_q_norm_g, m_w_uq, m_kv_norm_g, m_w_ukv, m_fox_out_g, m_mla_out_g,
                           m_w_o, m_norm_ffn_g, m_w_gate, m_w_up, m_w_down, m_final_norm_g]))
    var = dict(zip(names, [v_norm_mix_g, v_w_in, v_b_fgate, v_q_norm_g, v_w_uq, v_kv_norm_g, v_w_ukv, v_fox_out_g, v_mla_out_g,
                           v_w_o, v_norm_ffn_g, v_w_gate, v_w_up, v_w_down, v_final_norm_g]))
    shard = {n: wts[n][0] for n in HEAD3 + FFN4}
    nb, seq, _ = x.shape
    T = nb * seq
    tm, tq = min(ROW_TILE, seq), min(ATTN_TILE, seq)
    tt = min(512, T)
    xf = x.reshape(T, D_MODEL)
    tgt = loss_target.reshape(T, D_MODEL)

    head, = _run_exchange(_gather_exchange([_pack_shards(shard).astype(BF16)]), "gather_head")
    a = _arrange(_unpack_full(head))
    sel, sel_t = _selectors()
    ct, st = _rope_tables(positions)
    bfg = jnp.concatenate([b_fgate, jnp.zeros((1, LANES - HEADS), F32)], axis=1)
    g1, gq, gkv = norm_mix_g, q_norm_g, kv_norm_g

    h1, qf, kf, vf, qm, km, vm, lat, qn, kvn = _in_proj(xf, g1, a["w_in"], a["w_q12"], a["w_k"], a["w_v"], gq, gkv, bfg, ct, st, sel, seq, tm)
    of, lse_f, gathered = _attn_fwd(qf, kf, vf, nb, seq, tq, "fox_fwd", _gather_exchange([shard[n].astype(BF16) for n in FFN4]))
    om, lse_m, _ = _attn_fwd(qm, km, vm, nb, seq, tq, "mla_fwd")
    wo4, wg4, wu4, wd4 = gathered
    a_cat, h2, hid, dg, du, dx3, dx2, dof, dom, st_mid = _mid(
        of, om, xf, tgt, fox_out_g, mla_out_g, norm_ffn_g, final_norm_g.reshape(1, D_MODEL),
        wo4.reshape(D_MODEL, D_MODEL), wg4, wu4, wd4, tm)

    big = [_wgrad(a_cat, dx2, 512, tt, "wgrad_o").reshape(N_CHIPS, D_MODEL // N_CHIPS, D_MODEL),
           _wgrad_chunked(h2, dg, tt, "wgrad_gate"), _wgrad_chunked(h2, du, tt, "wgrad_up"), _wgrad_chunked(hid, dx3, tt, "wgrad_down")]
    dqf, dkf, dvf, got = _attn_bwd(qf, kf, vf, of, dof, lse_f, nb, seq, tq, "fox_bwd", True, _swap_exchange(big))
    sums = [_add_half(g, s, _row_tile(s.shape[1])) for g, s in zip(big, got)]
    dqm, dkm, dvm, recv = _attn_bwd(qm, km, vm, om, dom, lse_m, nb, seq, tq, "mla_bwd", False, _scatter_exchange(sums))
    halves = [_sum_slabs(g, s, r, _row_tile(s.shape[1])) for g, s, r in zip(big, got, recv)]
    dx, dproj, dq12, dkv, st_in, _ = _in_bwd(dqf, dkf, dvf, dqm, dkm, dvm, lat, xf, dx2, g1, gq, gkv, bfg, ct, st, sel_t,
                                             a["w_in"], a["w_q12"], a["w_kv"], seq, tm)
    g_in, joined = _wgrad(h1, dproj, C_END, tt, "wgrad_in", _join_exchange(halves))
    gshard = dict(zip(FFN4, joined))

    loss_row = jnp.concatenate([jnp.sum(st_mid[3:4, :], axis=1, keepdims=True), jnp.zeros((1, D_MODEL - 1), F32)], axis=1)
    stats = jnp.concatenate([st_in[0:1], st_mid[1:2], st_mid[0:1], st_mid[2:3], st_in[1:2], st_in[2:3], st_in[3:4], loss_row], axis=0)
    stats = _allreduce_small(stats)
    tail = _unarrange(g_in, _wgrad(qn, dq12, 2048, tt, "wgrad_uq"), _wgrad(kvn, dkv, 1536, tt, "wgrad_ukv"))
    gshard.update(_unpack_shard(_reduce_tail(_pack_full(tail)), shard))

    grads, delta, new_m, new_v = {}, {}, {}, {}
    for n in HEAD3 + FFN4:
        grads[n] = gshard[n][None]
        d, nm, nv = _adamw(shard[n], gshard[n], mom[n][0], var[n][0], "adamw_" + n)
        delta[n], new_m[n], new_v[n] = d[None], nm[None], nv[None]
    sm_g = {n: stats[row:row + 1, 0:width] for n, (row, width) in SMALL_ROWS.items()}
    sm_g["fox_out_g"] = stats[3:4, 0:512]
    sm_g["mla_out_g"] = stats[3:4, 512:1024]
    pad = lambda t: jnp.pad(t.reshape(1, -1), ((0, 0), (0, 1024 - t.size)))
    stack = lambda d: jnp.concatenate([pad(d[n]) for n in SMALL], axis=0)
    sd, sm, sv = _adamw(stack(wts), stack(sm_g), stack(mom), stack(var), "adamw_small")
    for i, n in enumerate(SMALL):
        shp = wts[n].shape
        grads[n] = sm_g[n].reshape(shp)
        delta[n] = sd[i, 0:wts[n].size].reshape(shp)
        new_m[n] = sm[i, 0:wts[n].size].reshape(shp)
        new_v[n] = sv[i, 0:wts[n].size].reshape(shp)
    loss = stats[7, 0]
    return (loss, dx.reshape(x.shape), *[grads[n] for n in names], *[delta[n] for n in names],
            *[new_m[n] for n in names], *[new_v[n] for n in names])
```

```python
import functools

import numpy as np
import jax
import jax.numpy as jnp
from jax import lax
from jax.experimental import pallas as pl
from jax.experimental.pallas import tpu as pltpu

F32 = jnp.float32
BF16 = jnp.bfloat16
MESH = pl.DeviceIdType.MESH

EPS = 1e-6
D_MODEL = 1024
HEADS = 8
PAIRS = HEADS // 2
FOX_W = 512
Q_RANK = 256
KV_RANK = 128
ROPE = 32
D_FF = 2816
N_CHIPS = 4
FF_CHUNK = D_FF // N_CHIPS
FOX_SCALE = 64 ** -0.5
MLA_SCALE = 96 ** -0.5
LANES = 128
NEG = -1e30

ADAM_LR, ADAM_B1, ADAM_B2, ADAM_EPS, ADAM_WD, ADAM_STEP = 0.001, 0.9, 0.999, 1e-08, 0.01, 10

C_FQ, C_FK, C_FV, C_QL, C_KVL, C_MA, C_MB, C_END = 0, 512, 1024, 1536, 1792, 1920, 2048, 2176

VMEM_LIMIT = 60 * 1024 * 1024
ROW_TILE = 256
ATTN_TILE = 512

HEAD3 = ("w_in", "w_uq", "w_ukv")
FFN4 = ("w_o", "w_gate", "w_up", "w_down")
COL_SHARDED = ("w_in", "w_uq", "w_ukv", "w_gate", "w_up")
SMALL = ("norm_mix_g", "b_fgate", "q_norm_g", "kv_norm_g", "fox_out_g", "mla_out_g", "norm_ffn_g", "final_norm_g")
FULL_SHAPES = {"w_in": (1024, 1960), "w_uq": (256, 768), "w_ukv": (128, 1024)}
PACK_ELEMS = sum(a * b for a, b in FULL_SHAPES.values()) // N_CHIPS
PACK_ROWS = -(-PACK_ELEMS // (1024 * 32)) * 32


def _params(sem=None):
    return pltpu.CompilerParams(dimension_semantics=sem, vmem_limit_bytes=VMEM_LIMIT)


def _full(shape):
    n = len(shape)
    return pl.BlockSpec(shape, lambda *_: (0,) * n, pipeline_mode=pl.Buffered(1))


def _dot(a, b):
    return jnp.dot(a, b, preferred_element_type=F32)


def _dot_nt(a, b):
    return lax.dot_general(a, b, (((1,), (1,)), ((), ())), preferred_element_type=F32)


def _dot_tn(a, b):
    return lax.dot_general(a, b, (((0,), (0,)), ((), ())), preferred_element_type=F32)


def _split3(v):
    hi = v.astype(BF16)
    r1 = v - hi.astype(F32)
    mid = r1.astype(BF16)
    lo = (r1 - mid.astype(F32)).astype(BF16)
    return hi, mid, lo


def _rms(v, width):
    return lax.rsqrt(jnp.sum(v * v, axis=1, keepdims=True) * (1.0 / width) + EPS)


def _rms_bwd(dy, xhat, r, g, width):
    u = dy * g
    return r * (u - xhat * (jnp.sum(u * xhat, axis=1, keepdims=True) * (1.0 / width)))


ANY = pl.BlockSpec(memory_space=pl.ANY)


def _place():
    return lax.axis_index("x"), lax.axis_index("y"), lax.axis_index("c")


def _other_chips(x, y):
    return [(1 - x, y), (x, 1 - y), (1 - x, 1 - y)]


def _remote(src, dst, send, recv, j, dev):
    return pltpu.make_async_remote_copy(src_ref=src, dst_ref=dst, send_sem=send.at[j], recv_sem=recv.at[j], device_id=dev, device_id_type=MESH)


class _Exchange:
    def __init__(self, ins, outs, n_remote, n_local, build, in_place=False):
        self.ins, self.outs, self.n_remote, self.n_local, self.build = list(ins), list(outs), n_remote, max(n_local, 1), build
        self.in_place = in_place

    def aliases(self, first_in, first_out):
        return {first_in + i: first_out + i for i in range(len(self.ins))} if self.in_place else {}

    def sems(self):
        return [pltpu.SemaphoreType.DMA((self.n_remote,)), pltpu.SemaphoreType.DMA((self.n_remote,)), pltpu.SemaphoreType.DMA((self.n_local,))]

    def start(self, in_refs, out_refs, sems):
        for cp in self.build(in_refs, out_refs, *sems)[0]:
            cp.start()

    def wait(self, in_refs, out_refs, sems):
        for w in self.build(in_refs, out_refs, *sems)[1]:
            w()


def _gather_exchange(shards, own=True):
    def build(ins, outs, send, recv, lsem):
        x, y, c = _place()
        starts, waits = [], []
        for i, (s, o) in enumerate(zip(ins, outs)):
            if own:
                mine = pltpu.make_async_copy(s, o.at[2 * x + y], lsem.at[i])
                starts.append(mine)
                waits.append(mine.wait)
            for j, (cx, cy) in enumerate(_other_chips(x, y)):
                out = _remote(s, o.at[2 * x + y], send, recv, 3 * i + j, (cx, cy, c))
                starts.append(out)
                waits.append(_remote(s, o.at[2 * cx + cy], send, recv, 3 * i + j, (cx, cy, c)).wait_recv)
                waits.append(out.wait_send)
        return starts, waits

    outs = [jax.ShapeDtypeStruct((N_CHIPS,) + s.shape, s.dtype) for s in shards]
    return _Exchange(shards, outs, 3 * len(shards), len(shards), build)


def _swap_exchange(grads):
    def build(ins, outs, send, recv, lsem):
        x, y, c = _place()
        cps = []
        for i, (g, o) in enumerate(zip(ins, outs)):
            hr = g.shape[1] // 2
            cps.append(_remote(g.at[:, pl.ds((1 - c) * hr, hr), :], o, send, recv, i, (x, y, 1 - c)))
        return cps, [cp.wait for cp in cps]

    outs = [jax.ShapeDtypeStruct((g.shape[0], g.shape[1] // 2, g.shape[2]), g.dtype) for g in grads]
    return _Exchange(grads, outs, len(grads), 0, build)


def _scatter_exchange(sums):
    def build(ins, outs, send, recv, lsem):
        x, y, c = _place()
        cps = []
        for i, (s, o) in enumerate(zip(ins, outs)):
            for j, (cx, cy) in enumerate(_other_chips(x, y)):
                cps.append(_remote(s.at[2 * cx + cy], o.at[j], send, recv, 3 * i + j, (cx, cy, c)))
        return cps, [cp.wait for cp in cps]

    outs = [jax.ShapeDtypeStruct((3,) + s.shape[1:], s.dtype) for s in sums]
    return _Exchange(sums, outs, 3 * len(sums), 0, build)


def _join_exchange(bufs):
    def build(ins, outs, send, recv, lsem):
        x, y, c = _place()
        starts, waits = [], []
        for i, (t, o) in enumerate(zip(ins, outs)):
            hr = t.shape[0] // 2
            out = _remote(t.at[pl.ds(c * hr, hr), :], o.at[pl.ds(c * hr, hr), :], send, recv, i, (x, y, 1 - c))
            starts.append(out)
            waits += [_remote(t.at[pl.ds(c * hr, hr), :], o.at[pl.ds((1 - c) * hr, hr), :], send, recv, i, (x, y, 1 - c)).wait_recv,
                      out.wait_send]
        return starts, waits

    outs = [jax.ShapeDtypeStruct(t.shape, t.dtype) for t in bufs]
    return _Exchange(bufs, outs, len(bufs), 0, build, in_place=True)


def _run_exchange(ex, name):
    n_in, n_out = len(ex.ins), len(ex.outs)

    def body(*refs):
        ins, outs, sems = refs[:n_in], refs[n_in:n_in + n_out], refs[n_in + n_out:]
        ex.start(ins, outs, sems)
        ex.wait(ins, outs, sems)

    return pl.pallas_call(
        body, name=name, out_shape=tuple(ex.outs), in_specs=[ANY] * n_in, out_specs=tuple([ANY] * n_out),
        scratch_shapes=ex.sems(), input_output_aliases=ex.aliases(0, 0),
        compiler_params=pltpu.CompilerParams(has_side_effects=True),
    )(*ex.ins)


def _allreduce_small(v):
    def body(v_ref, o_ref, buf, send, recv):
        x, y, c = _place()
        me = 4 * x + 2 * y + c
        buf[me] = v_ref[...]
        out = []
        for j in range(7):
            fx, fy, fc = (j + 1) >> 2 & 1, (j + 1) >> 1 & 1, (j + 1) & 1
            out.append(_remote(v_ref, buf.at[me], send, recv, j, (x ^ fx, y ^ fy, c ^ fc)))
        for cp in out:
            cp.start()
        for j in range(7):
            fx, fy, fc = (j + 1) >> 2 & 1, (j + 1) >> 1 & 1, (j + 1) & 1
            src = 4 * (x ^ fx) + 2 * (y ^ fy) + (c ^ fc)
            _remote(v_ref, buf.at[src], send, recv, j, (x ^ fx, y ^ fy, c ^ fc)).wait_recv()
        for cp in out:
            cp.wait_send()
        acc = buf[0]
        for d in range(1, 8):
            acc = acc + buf[d]
        o_ref[...] = acc

    vm = pl.BlockSpec(memory_space=pltpu.VMEM)
    return pl.pallas_call(
        body, name="allreduce_small", out_shape=jax.ShapeDtypeStruct(v.shape, v.dtype),
        in_specs=[vm], out_specs=vm,
        scratch_shapes=[pltpu.VMEM((8,) + v.shape, v.dtype), pltpu.SemaphoreType.DMA((7,)), pltpu.SemaphoreType.DMA((7,))],
        compiler_params=pltpu.CompilerParams(has_side_effects=True),
    )(v)


def _add_half(g, got, tr):
    n, R, C = g.shape
    hr = R // 2
    nb = hr // tr

    def body(c_ref, g_ref, r_ref, o_ref):
        o_ref[...] = (g_ref[...] + r_ref[...]).astype(BF16)

    c = lax.axis_index("c")
    return pl.pallas_call(
        body, name="add_half",
        grid_spec=pltpu.PrefetchScalarGridSpec(
            num_scalar_prefetch=1, grid=(n, nb),
            in_specs=[pl.BlockSpec((1, tr, C), lambda k, i, c_ref: (k, c_ref[0] * nb + i, 0)),
                      pl.BlockSpec((1, tr, C), lambda k, i, c_ref: (k, i, 0))],
            out_specs=pl.BlockSpec((1, tr, C), lambda k, i, c_ref: (k, i, 0))),
        out_shape=jax.ShapeDtypeStruct((n, hr, C), BF16),
        compiler_params=_params(("arbitrary", "arbitrary")),
    )(jnp.reshape(c, (1,)).astype(jnp.int32), g, got)


def _sum_slabs(g, got, recv, tr):
    n, R, C = g.shape
    hr = R // 2
    nb = hr // tr

    def body(kc_ref, g_ref, s_ref, r_ref, o_ref):
        o_ref[...] = (((g_ref[0] + s_ref[0]) + r_ref[0].astype(F32)) + r_ref[1].astype(F32)) + r_ref[2].astype(F32)

    kc = jnp.stack([2 * lax.axis_index("x") + lax.axis_index("y"), lax.axis_index("c")]).astype(jnp.int32)
    return pl.pallas_call(
        body, name="sum_slabs",
        grid_spec=pltpu.PrefetchScalarGridSpec(
            num_scalar_prefetch=1, grid=(nb,),
            in_specs=[pl.BlockSpec((1, tr, C), lambda i, kc_ref: (kc_ref[0], kc_ref[1] * nb + i, 0)),
                      pl.BlockSpec((1, tr, C), lambda i, kc_ref: (kc_ref[0], i, 0)),
                      pl.BlockSpec((3, tr, C), lambda i, kc_ref: (0, i, 0))],
            out_specs=pl.BlockSpec((tr, C), lambda i, kc_ref: (kc_ref[1] * nb + i, 0))),
        out_shape=jax.ShapeDtypeStruct((R, C), F32),
        compiler_params=_params(("arbitrary",)),
    )(kc, g, got, recv)


def _row_tile(rows):
    for cand in (256, 184, 176, 144, 128, 64, 32, 16, 8):
        if rows % cand == 0:
            return cand
    return rows


def _in_proj(x, g1, w_in, w_q12, w_k, w_v, gq, gkv, bfg, ct, st, sel, seq, tm):
    T = x.shape[0]
    nsb = seq // tm

    def body(x_ref, g1_ref, win_ref, wq_ref, wk_ref, wv_ref, gq_ref, gkv_ref, b_ref, ct_ref, st_ref, sel_ref,
             h1_ref, qf_ref, kf_ref, vf_ref, qm_ref, km_ref, vm_ref, lat_ref, qn_ref, kvn_ref, carry):
        i = pl.program_id(0)

        @pl.when(i % nsb == 0)
        def _():
            carry[...] = jnp.zeros_like(carry)

        xv = x_ref[...]
        h = (xv * _rms(xv, D_MODEL) * g1_ref[...]).astype(BF16)
        h1_ref[...] = h
        proj = _dot(h, win_ref[...])
        lane = lax.broadcasted_iota(jnp.int32, (tm, LANES), 1)
        low = lane < 64
        misc_a = proj[:, C_MA:C_MB]
        misc_b = proj[:, C_MB:C_END]

        z = misc_a + b_ref[...]
        lf = jnp.where(lane < HEADS, jnp.minimum(z, 0.0) - jnp.log1p(jnp.exp(-jnp.abs(z))), 0.0)
        rr = lax.broadcasted_iota(jnp.int32, (tm, tm), 0)
        cc = lax.broadcasted_iota(jnp.int32, (tm, tm), 1)
        tri = (rr >= cc).astype(BF16)
        a0, a1, a2 = _split3(lf)
        c = _dot(tri, a0) + _dot(tri, a1) + _dot(tri, a2) + carry[0:1, :]
        carry[0:1, :] = c[tm - 1:tm, :]
        c0, c1, c2 = _split3(c)
        cpl = _dot(jnp.concatenate([c0, c1, c2], axis=1), sel_ref[...])
        qpad = jnp.where((lane >= 64) & (lane < 67), -1.0, 0.0)
        for j in range(PAIRS):
            qc = proj[:, C_FQ + LANES * j:C_FQ + LANES * (j + 1)] * FOX_SCALE
            kc = proj[:, C_FK + LANES * j:C_FK + LANES * (j + 1)]
            e, o = 2 * LANES * j, 2 * LANES * j + LANES
            qf_ref[:, e:e + LANES] = jnp.where(low, qc, qpad).astype(BF16)
            qf_ref[:, o:o + LANES] = jnp.where(low, pltpu.roll(qc, 64, 1), qpad).astype(BF16)
            kf_ref[:, e:e + LANES] = jnp.where(low, kc, cpl[:, e:e + LANES]).astype(BF16)
            kf_ref[:, o:o + LANES] = jnp.where(low, pltpu.roll(kc, 64, 1), cpl[:, o:o + LANES]).astype(BF16)
        vf_ref[...] = proj[:, C_FV:C_QL].astype(BF16)

        ql = proj[:, C_QL:C_KVL]
        kvl = proj[:, C_KVL:C_MA]
        qn = (ql * _rms(ql, Q_RANK) * gq_ref[...]).astype(BF16)
        kvn = (kvl * _rms(kvl, KV_RANK) * gkv_ref[...]).astype(BF16)
        lat_ref[...] = proj[:, C_QL:C_MB]
        qn_ref[...] = qn
        kvn_ref[...] = kvn
        q12 = _dot(qn, wq_ref[...])
        kn = _dot(kvn, wk_ref[...])
        ctv = ct_ref[...]
        stv = st_ref[...]
        cq = (jnp.where(low, 1.0, 0.0) + ctv) * MLA_SCALE
        sq = stv * MLA_SCALE
        kpe = misc_a * ctv + misc_b * stv
        for hd in range(HEADS):
            s0 = LANES * hd
            qm_ref[:, s0:s0 + LANES] = (q12[:, s0:s0 + LANES] * cq + q12[:, 1024 + s0:1024 + s0 + LANES] * sq).astype(BF16)
            km_ref[:, s0:s0 + LANES] = (kn[:, s0:s0 + LANES] + kpe).astype(BF16)
        vm_ref[...] = _dot(kvn, wv_ref[...]).astype(BF16)

    row = lambda w: pl.BlockSpec((tm, w), lambda i: (i, 0))
    out_shape = (
        jax.ShapeDtypeStruct((T, D_MODEL), BF16),
        jax.ShapeDtypeStruct((T, 1024), BF16), jax.ShapeDtypeStruct((T, 1024), BF16), jax.ShapeDtypeStruct((T, 512), BF16),
        jax.ShapeDtypeStruct((T, 1024), BF16), jax.ShapeDtypeStruct((T, 1024), BF16), jax.ShapeDtypeStruct((T, 512), BF16),
        jax.ShapeDtypeStruct((T, 512), F32),
        jax.ShapeDtypeStruct((T, Q_RANK), BF16), jax.ShapeDtypeStruct((T, KV_RANK), BF16),
    )
    return pl.pallas_call(
        body, name="in_proj", grid=(T // tm,), out_shape=out_shape,
        in_specs=[row(D_MODEL), _full(g1.shape), _full(w_in.shape), _full(w_q12.shape), _full(w_k.shape), _full(w_v.shape),
                  _full(gq.shape), _full(gkv.shape), _full(bfg.shape), row(LANES), row(LANES), _full(sel.shape)],
        out_specs=[row(D_MODEL), row(1024), row(1024), row(512), row(1024), row(1024), row(512), row(512), row(Q_RANK), row(KV_RANK)],
        scratch_shapes=[pltpu.VMEM((8, LANES), F32)],
        compiler_params=_params(("arbitrary",)),
    )(x, g1, w_in, w_q12, w_k, w_v, gq, gkv, bfg, ct, st, sel)


def _attn_fwd(q, k, v, nb, seq, tq, name, ex=None):
    T = q.shape[0]
    nq = seq // tq
    n_in, n_out = (len(ex.ins), len(ex.outs)) if ex else (0, 0)

    def body(*refs):
        q_ref, k_ref, v_ref = refs[0:3]
        o_ref, lse_ref = refs[3 + n_in:5 + n_in]
        b, pr, qi = pl.program_id(0), pl.program_id(1), pl.program_id(2)
        if ex:
            ex_refs = (refs[3:3 + n_in], refs[5 + n_in:5 + n_in + n_out], refs[5 + n_in + n_out:])

            @pl.when((b == 0) & (pr == 0) & (qi == 0))
            def _():
                ex.start(*ex_refs)

        lane = lax.broadcasted_iota(jnp.int32, (tq, LANES), 1)
        rr = lax.broadcasted_iota(jnp.int32, (tq, tq), 0)
        cc = lax.broadcasted_iota(jnp.int32, (tq, tq), 1)
        def step(kj, carry, masked):
            rows = pl.ds(pl.multiple_of(kj * tq, tq), tq)
            vv = v_ref[rows, :]
            new = []
            for hh in range(2):
                m, l, acc = carry[hh]
                s = _dot_nt(q_ref[:, LANES * hh:LANES * (hh + 1)], k_ref[rows, LANES * hh:LANES * (hh + 1)])
                if masked:
                    s = jnp.where(cc <= rr, s, NEG)
                m_new = jnp.maximum(m, jnp.max(s, axis=1, keepdims=True))
                alpha = jnp.exp(m - m_new)
                p = jnp.exp(s - m_new)
                l = alpha * l + jnp.sum(p, axis=1, keepdims=True)
                acc = alpha * acc + _dot(p.astype(BF16), vv)
                new.append((m_new, l, acc))
            return tuple(new)

        one = (jnp.full((tq, 1), NEG, F32), jnp.zeros((tq, 1), F32), jnp.zeros((tq, LANES), F32))
        carry = lax.fori_loop(0, qi, functools.partial(step, masked=False), (one, one))
        (m0, l0, acc0), (m1, l1, acc1) = step(qi, carry, True)
        lse_ref[:, 0:LANES] = jnp.broadcast_to(m0 + jnp.log(l0), (tq, LANES))
        lse_ref[:, LANES:2 * LANES] = jnp.broadcast_to(m1 + jnp.log(l1), (tq, LANES))
        o_ref[...] = jnp.where(lane < 64, acc0 / l0, acc1 / l1)

        if ex:
            @pl.when((b == nb - 1) & (pr == PAIRS - 1) & (qi == nq - 1))
            def _():
                ex.wait(*ex_refs)

    res = pl.pallas_call(
        body, name=name, grid=(nb, PAIRS, nq),
        out_shape=(jax.ShapeDtypeStruct((T, 512), F32), jax.ShapeDtypeStruct((T, 1024), F32)) + tuple(ex.outs if ex else ()),
        in_specs=[pl.BlockSpec((tq, 2 * LANES), lambda b, p, i: (b * nq + i, p)),
                  pl.BlockSpec((seq, 2 * LANES), lambda b, p, i: (b, p)),
                  pl.BlockSpec((seq, LANES), lambda b, p, i: (b, p))] + [ANY] * n_in,
        out_specs=[pl.BlockSpec((tq, LANES), lambda b, p, i: (b * nq + i, p)),
                   pl.BlockSpec((tq, 2 * LANES), lambda b, p, i: (b * nq + i, p))] + [ANY] * n_out,
        scratch_shapes=ex.sems() if ex else [],
        compiler_params=_params(("arbitrary", "arbitrary", "arbitrary")),
    )(q, k, v, *(ex.ins if ex else ()))
    return res[0], res[1], list(res[2:])


def _attn_bwd(q, k, v, o, do, lse, nb, seq, tq, name, key_bias, ex=None):
    T = q.shape[0]
    nq = seq // tq
    n_in, n_out = (len(ex.ins), len(ex.outs)) if ex else (0, 0)

    def body(*refs):
        q_ref, k_ref, v_ref, o_ref, do_ref, lse_ref = refs[0:6]
        dq_ref, dk_ref, dv_ref = refs[6 + n_in:9 + n_in]
        dsc, rsum = refs[9 + n_in + n_out:11 + n_in + n_out]
        b, pr, kj = pl.program_id(0), pl.program_id(1), pl.program_id(2)
        if ex:
            ex_refs = (refs[6:6 + n_in], refs[9 + n_in:9 + n_in + n_out], refs[11 + n_in + n_out:])

            @pl.when((b == 0) & (pr == 0) & (kj == 0))
            def _():
                ex.start(*ex_refs)

        lane_s = lax.broadcasted_iota(jnp.int32, (seq, LANES), 1)
        lane = lax.broadcasted_iota(jnp.int32, (tq, LANES), 1)
        rr = lax.broadcasted_iota(jnp.int32, (tq, tq), 0)
        cc = lax.broadcasted_iota(jnp.int32, (tq, tq), 1)

        @pl.when(kj == 0)
        def _():
            dq_ref[...] = jnp.zeros_like(dq_ref)
            prod = do_ref[...].astype(F32) * o_ref[...]
            d0 = jnp.sum(jnp.where(lane_s < 64, prod, 0.0), axis=1, keepdims=True)
            d1 = jnp.sum(jnp.where(lane_s < 64, 0.0, prod), axis=1, keepdims=True)
            dsc[0] = jnp.broadcast_to(d0, (seq, LANES))
            dsc[1] = jnp.broadcast_to(d1, (seq, LANES))
            if key_bias:
                rsum[...] = jnp.zeros_like(rsum)

        vv = v_ref[...]
        dv_acc = jnp.zeros((tq, LANES), F32)
        for hh in range(2):
            kv = k_ref[:, LANES * hh:LANES * (hh + 1)]
            keep = (lane < 64) if hh == 0 else (lane >= 64)

            def step(qi, carry, masked, hh=hh, kv=kv, keep=keep):
                dk_acc, dv_acc, col = carry
                rows = pl.ds(pl.multiple_of(qi * tq, tq), tq)
                qv = q_ref[rows, LANES * hh:LANES * (hh + 1)]
                dom = jnp.where(keep, do_ref[rows, :], jnp.zeros((), BF16))
                s = _dot_nt(qv, kv)
                if masked:
                    s = jnp.where(cc <= rr, s, NEG)
                p = jnp.exp(s - lse_ref[rows, LANES * hh:LANES * hh + 1])
                dp = _dot_nt(dom, vv)
                ds32 = p * (dp - dsc[hh, rows, 0:1])
                if key_bias:
                    col = col + jnp.sum(ds32, axis=0, keepdims=True)
                    rsum[hh, rows, :] += jnp.broadcast_to(jnp.sum(ds32, axis=1, keepdims=True), (tq, LANES))
                ds = ds32.astype(BF16)
                dv_acc = dv_acc + _dot_tn(p.astype(BF16), dom)
                dk_acc = dk_acc + _dot_tn(ds, qv)
                dq_ref[rows, LANES * hh:LANES * (hh + 1)] += _dot(ds, kv)
                return dk_acc, dv_acc, col

            carry = step(kj, (jnp.zeros((tq, LANES), F32), dv_acc, jnp.zeros((1, tq), F32)), True)
            dk_acc, dv_acc, col = lax.fori_loop(kj + 1, nq, functools.partial(step, masked=False), carry)
            if key_bias:
                first = (lax.broadcasted_iota(jnp.int32, (8, LANES), 0) == 0).astype(BF16)
                colsum = sum(_dot_tn(jnp.broadcast_to(piece, (8, tq)), first) for piece in _split3(col))
                dk_acc = jnp.where(lane == 64, -colsum, dk_acc)
            dk_ref[:, LANES * hh:LANES * (hh + 1)] = dk_acc
        dv_ref[...] = dv_acc

        if key_bias:
            @pl.when(kj == nq - 1)
            def _():
                for hh in range(2):
                    blk = dq_ref[:, LANES * hh:LANES * (hh + 1)]
                    dq_ref[:, LANES * hh:LANES * (hh + 1)] = jnp.where(lane_s == 64, rsum[hh], blk)

        if ex:
            @pl.when((b == nb - 1) & (pr == PAIRS - 1) & (kj == nq - 1))
            def _():
                ex.wait(*ex_refs)

    per_seq = lambda w: pl.BlockSpec((seq, w), lambda b, p, j: (b, p))
    per_blk = lambda w: pl.BlockSpec((tq, w), lambda b, p, j: (b * nq + j, p))
    res = pl.pallas_call(
        body, name=name, grid=(nb, PAIRS, nq),
        out_shape=(jax.ShapeDtypeStruct((T, 1024), F32), jax.ShapeDtypeStruct((T, 1024), F32), jax.ShapeDtypeStruct((T, 512), F32))
        + tuple(ex.outs if ex else ()),
        in_specs=[per_seq(2 * LANES), per_blk(2 * LANES), per_blk(LANES), per_seq(LANES), per_seq(LANES), per_seq(2 * LANES)] + [ANY] * n_in,
        out_specs=[per_seq(2 * LANES), per_blk(2 * LANES), per_blk(LANES)] + [ANY] * n_out,
        scratch_shapes=[pltpu.VMEM((2, seq, LANES), F32), pltpu.VMEM((2, seq, LANES) if key_bias else (2, 8, LANES), F32)]
        + (ex.sems() if ex else []),
        compiler_params=_params(("arbitrary", "arbitrary", "arbitrary")),
    )(q, k, v, o, do, lse, *(ex.ins if ex else ()))
    return res[0], res[1], res[2], list(res[3:])


def _mid(of, om, x, tgt, g_fo, g_mo, g2, g3, w_o, w_g, w_u, w_d, tm):
    T = x.shape[0]

    def body(of_ref, om_ref, x_ref, t_ref, gfo_ref, gmo_ref, g2_ref, g3_ref, wo_ref, wg_ref, wu_ref, wd_ref,
             a_ref, h2_ref, hid_ref, dg_ref, du_ref, dx3_ref, dx2_ref, dof_ref, dom_ref, st_ref):
        i = pl.program_id(0)

        @pl.when(i == 0)
        def _():
            st_ref[...] = jnp.zeros_like(st_ref)

        ofv, omv = of_ref[...], om_ref[...]
        rf, rm = _rms(ofv, FOX_W), _rms(omv, FOX_W)
        fhat, mhat = ofv * rf, omv * rm
        a = jnp.concatenate([fhat * gfo_ref[...], mhat * gmo_ref[...]], axis=1).astype(BF16)
        a_ref[...] = a
        x2 = x_ref[...] + _dot(a, wo_ref[...])
        r2 = _rms(x2, D_MODEL)
        xh2 = x2 * r2
        h2 = (xh2 * g2_ref[...]).astype(BF16)
        h2_ref[...] = h2
        x3 = x2
        kept = []
        for k in range(N_CHIPS):
            gt = _dot(h2, wg_ref[k])
            up = _dot(h2, wu_ref[k])
            sg = jax.nn.sigmoid(gt)
            sl = gt * sg
            hid = (sl * up).astype(BF16)
            hid_ref[k] = hid
            x3 = x3 + _dot(hid, wd_ref[k])
            kept.append((up * (sg * (1.0 + gt * (1.0 - sg))), sl))
        r3 = _rms(x3, D_MODEL)
        xh3 = x3 * r3
        diff = xh3 * g3_ref[...] - t_ref[...]
        dy = diff * (1.0 / D_MODEL)
        st_ref[3:4, :] += jnp.sum(diff * diff, axis=0, keepdims=True) * (0.5 / D_MODEL)
        st_ref[0:1, :] += jnp.sum(dy * xh3, axis=0, keepdims=True)
        dx3 = _rms_bwd(dy, xh3, r3, g3_ref[...], D_MODEL)
        dx3b = dx3.astype(BF16)
        dx3_ref[...] = dx3b
        dh2 = jnp.zeros((tm, D_MODEL), F32)
        for k in range(N_CHIPS):
            dhid = _dot_nt(dx3b, wd_ref[k])
            dg = (dhid * kept[k][0]).astype(BF16)
            du = (dhid * kept[k][1]).astype(BF16)
            dg_ref[k] = dg
            du_ref[k] = du
            dh2 = dh2 + _dot_nt(dg, wg_ref[k]) + _dot_nt(du, wu_ref[k])
        st_ref[1:2, :] += jnp.sum(dh2 * xh2, axis=0, keepdims=True)
        dx2 = dx3 + _rms_bwd(dh2, xh2, r2, g2_ref[...], D_MODEL)
        dx2_ref[...] = dx2
        da = _dot_nt(dx2.astype(BF16), wo_ref[...])
        daf, dam = da[:, 0:FOX_W], da[:, FOX_W:2 * FOX_W]
        st_ref[2:3, 0:FOX_W] += jnp.sum(daf * fhat, axis=0, keepdims=True)
        st_ref[2:3, FOX_W:2 * FOX_W] += jnp.sum(dam * mhat, axis=0, keepdims=True)
        dof_ref[...] = _rms_bwd(daf, fhat, rf, gfo_ref[...], FOX_W).astype(BF16)
        dom_ref[...] = _rms_bwd(dam, mhat, rm, gmo_ref[...], FOX_W).astype(BF16)

    row = lambda w: pl.BlockSpec((tm, w), lambda i: (i, 0))
    chunked = pl.BlockSpec((N_CHIPS, tm, FF_CHUNK), lambda i: (0, i, 0))
    ff = jax.ShapeDtypeStruct((N_CHIPS, T, FF_CHUNK), BF16)
    out_shape = (
        jax.ShapeDtypeStruct((T, 1024), BF16), jax.ShapeDtypeStruct((T, 1024), BF16), ff, ff, ff,
        jax.ShapeDtypeStruct((T, 1024), BF16), jax.ShapeDtypeStruct((T, 1024), F32),
        jax.ShapeDtypeStruct((T, 512), BF16), jax.ShapeDtypeStruct((T, 512), BF16), jax.ShapeDtypeStruct((8, 1024), F32),
    )
    return pl.pallas_call(
        body, name="mid", grid=(T // tm,), out_shape=out_shape,
        in_specs=[row(512), row(512), row(1024), row(1024), _full(g_fo.shape), _full(g_mo.shape), _full(g2.shape), _full(g3.shape),
                  _full(w_o.shape), _full(w_g.shape), _full(w_u.shape), _full(w_d.shape)],
        out_specs=[row(1024), row(1024), chunked, chunked, chunked, row(1024), row(1024), row(512), row(512),
                   pl.BlockSpec((8, 1024), lambda i: (0, 0))],
        compiler_params=_params(("arbitrary",)),
    )(of, om, x, tgt, g_fo, g_mo, g2, g3, w_o, w_g, w_u, w_d)


def _in_bwd(dqf, dkf, dvf, dqm, dkm, dvm, lat, x, dx2, g1, gq, gkv, bfg, ct, st, sel_t, w_in, w_q12, w_kv, seq, tm, ex=None):
    T = x.shape[0]
    nblk = T // tm
    nsb = seq // tm
    n_in, n_out = (len(ex.ins), len(ex.outs)) if ex else (0, 0)

    def body(*refs):
        (dqf_ref, dkf_ref, dvf_ref, dqm_ref, dkm_ref, dvm_ref, lat_ref, x_ref, dx2_ref, g1_ref, gq_ref, gkv_ref, b_ref,
         ct_ref, st_ref, selt_ref, win_ref, wq_ref, wkv_ref) = refs[0:19]
        dx_ref, dproj_ref, dq12_ref, dkv_ref, stat_ref = refs[19 + n_in:24 + n_in]
        carry = refs[24 + n_in + n_out]
        i = pl.program_id(0)
        if ex:
            ex_refs = (refs[19:19 + n_in], refs[24 + n_in:24 + n_in + n_out], refs[25 + n_in + n_out:])

            @pl.when(i == 0)
            def _():
                ex.start(*ex_refs)

        @pl.when(i == 0)
        def _():
            stat_ref[...] = jnp.zeros_like(stat_ref)

        @pl.when(i % nsb == 0)
        def _():
            carry[...] = jnp.zeros_like(carry)

        lane = lax.broadcasted_iota(jnp.int32, (tm, LANES), 1)
        low = lane < 64
        ctv, stv = ct_ref[...], st_ref[...]

        for j in range(PAIRS):
            e, o = 2 * LANES * j, 2 * LANES * j + LANES
            dq = jnp.where(low, dqf_ref[:, e:e + LANES], 0.0) + pltpu.roll(jnp.where(low, dqf_ref[:, o:o + LANES], 0.0), 64, 1)
            dk = jnp.where(low, dkf_ref[:, e:e + LANES], 0.0) + pltpu.roll(jnp.where(low, dkf_ref[:, o:o + LANES], 0.0), 64, 1)
            dproj_ref[:, C_FQ + LANES * j:C_FQ + LANES * (j + 1)] = (dq * FOX_SCALE).astype(BF16)
            dproj_ref[:, C_FK + LANES * j:C_FK + LANES * (j + 1)] = dk.astype(BF16)
        dproj_ref[:, C_FV:C_QL] = dvf_ref[...].astype(BF16)
        dcv = dkf_ref[...] + dqf_ref[...]
        k_hi = dcv.astype(BF16)
        k_lo = (dcv - k_hi.astype(F32)).astype(BF16)
        dc = _dot(k_hi, selt_ref[...]) + _dot(k_lo, selt_ref[...])
        rr = lax.broadcasted_iota(jnp.int32, (tm, tm), 0)
        cc = lax.broadcasted_iota(jnp.int32, (tm, tm), 1)
        triu = (cc >= rr).astype(BF16)
        a0, a1, a2 = _split3(dc)
        dlf = _dot(triu, a0) + _dot(triu, a1) + _dot(triu, a2) + carry[0:1, :]
        carry[0:1, :] = dlf[0:1, :]
        misc_a = lat_ref[:, Q_RANK + KV_RANK:Q_RANK + KV_RANK + LANES]
        z = misc_a + b_ref[...]
        dz = jnp.where(lane < HEADS, dlf * jax.nn.sigmoid(-z), 0.0)
        stat_ref[3:4, 0:LANES] += jnp.sum(dz, axis=0, keepdims=True)

        cq = (jnp.where(low, 1.0, 0.0) + ctv) * MLA_SCALE
        sq = stv * MLA_SCALE
        dkpe = jnp.zeros((tm, LANES), F32)
        for hd in range(HEADS):
            s0 = LANES * hd
            dqh = dqm_ref[:, s0:s0 + LANES]
            dq12_ref[:, s0:s0 + LANES] = (dqh * cq).astype(BF16)
            dq12_ref[:, 1024 + s0:1024 + s0 + LANES] = (dqh * sq).astype(BF16)
            dkpe = dkpe + dkm_ref[:, s0:s0 + LANES]
        dkv_ref[:, 0:1024] = dkm_ref[...].astype(BF16)
        dkv_ref[:, 1024:1536] = dvm_ref[...].astype(BF16)
        dproj_ref[:, C_MA:C_MB] = (dz + dkpe * ctv).astype(BF16)
        dproj_ref[:, C_MB:C_END] = (dkpe * stv).astype(BF16)
        dqn = _dot_nt(dq12_ref[...], wq_ref[...])
        dkvn = _dot_nt(dkv_ref[...], wkv_ref[...])
        ql = lat_ref[:, 0:Q_RANK]
        kvl = lat_ref[:, Q_RANK:Q_RANK + KV_RANK]
        rq, rkv = _rms(ql, Q_RANK), _rms(kvl, KV_RANK)
        qhat, kvhat = ql * rq, kvl * rkv
        stat_ref[1:2, 0:Q_RANK] += jnp.sum(dqn * qhat, axis=0, keepdims=True)
        stat_ref[2:3, 0:KV_RANK] += jnp.sum(dkvn * kvhat, axis=0, keepdims=True)
        dproj_ref[:, C_QL:C_KVL] = _rms_bwd(dqn, qhat, rq, gq_ref[...], Q_RANK).astype(BF16)
        dproj_ref[:, C_KVL:C_MA] = _rms_bwd(dkvn, kvhat, rkv, gkv_ref[...], KV_RANK).astype(BF16)

        dh1 = _dot_nt(dproj_ref[...], win_ref[...])
        xv = x_ref[...]
        r1 = _rms(xv, D_MODEL)
        xh = xv * r1
        stat_ref[0:1, :] += jnp.sum(dh1 * xh, axis=0, keepdims=True)
        dx_ref[...] = dx2_ref[...] + _rms_bwd(dh1, xh, r1, g1_ref[...], D_MODEL)

        if ex:
            @pl.when(i == nblk - 1)
            def _():
                ex.wait(*ex_refs)

    rev = lambda w: pl.BlockSpec((tm, w), lambda i: (nblk - 1 - i, 0))
    out_shape = (
        jax.ShapeDtypeStruct((T, 1024), F32), jax.ShapeDtypeStruct((T, C_END), BF16), jax.ShapeDtypeStruct((T, 2048), BF16),
        jax.ShapeDtypeStruct((T, 1536), BF16), jax.ShapeDtypeStruct((8, 1024), F32),
    ) + tuple(ex.outs if ex else ())
    res = pl.pallas_call(
        body, name="in_bwd", grid=(nblk,), out_shape=out_shape,
        in_specs=[rev(1024), rev(1024), rev(512), rev(1024), rev(1024), rev(512), rev(512), rev(1024), rev(1024),
                  _full(g1.shape), _full(gq.shape), _full(gkv.shape), _full(bfg.shape), rev(LANES), rev(LANES), _full(sel_t.shape),
                  _full(w_in.shape), _full(w_q12.shape), _full(w_kv.shape)] + [ANY] * n_in,
        out_specs=[rev(1024), rev(C_END), rev(2048), rev(1536), pl.BlockSpec((8, 1024), lambda i: (0, 0))] + [ANY] * n_out,
        scratch_shapes=[pltpu.VMEM((8, LANES), F32)] + (ex.sems() if ex else []),
        compiler_params=_params(("arbitrary",)),
    )(dqf, dkf, dvf, dqm, dkm, dvm, lat, x, dx2, g1, gq, gkv, bfg, ct, st, sel_t, w_in, w_q12, w_kv, *(ex.ins if ex else ()))
    return res[0], res[1], res[2], res[3], res[4], list(res[5:])


def _wgrad(a, b, tn, tt, name, ex=None):
    T, K = a.shape
    N = b.shape[1]
    n_in, n_out = (len(ex.ins), len(ex.outs)) if ex else (0, 0)
    gn, gt = N // tn, T // tt

    def body(*refs):
        a_ref, b_ref, o_ref = refs[0], refs[1], refs[2 + n_in]
        n, t = pl.program_id(0), pl.program_id(1)
        if ex:
            ex_refs = (refs[2:2 + n_in], refs[3 + n_in:3 + n_in + n_out], refs[3 + n_in + n_out:])

            @pl.when((n == 0) & (t == 0))
            def _():
                ex.start(*ex_refs)

        @pl.when(t == 0)
        def _():
            o_ref[...] = jnp.zeros_like(o_ref)

        o_ref[...] += _dot_tn(a_ref[...].astype(BF16), b_ref[...].astype(BF16))

        if ex:
            @pl.when((n == gn - 1) & (t == gt - 1))
            def _():
                ex.wait(*ex_refs)

    res = pl.pallas_call(
        body, name=name, grid=(gn, gt), out_shape=(jax.ShapeDtypeStruct((K, N), F32),) + tuple(ex.outs if ex else ()),
        in_specs=[pl.BlockSpec((tt, K), lambda n, t: (t, 0)), pl.BlockSpec((tt, tn), lambda n, t: (t, n))] + [ANY] * n_in,
        out_specs=[pl.BlockSpec((K, tn), lambda n, t: (0, n))] + [ANY] * n_out,
        scratch_shapes=ex.sems() if ex else [], input_output_aliases=ex.aliases(2, 1) if ex else {},
        compiler_params=_params(("arbitrary", "arbitrary")),
    )(a, b, *(ex.ins if ex else ()))
    return (res[0], list(res[1:])) if ex else res[0]


def _wgrad_chunked(a, b, tt, name):
    a_chunked = a.ndim == 3
    T, K = a.shape[-2:]
    N = b.shape[-1]

    def body(a_ref, b_ref, o_ref):
        @pl.when(pl.program_id(1) == 0)
        def _():
            o_ref[...] = jnp.zeros_like(o_ref)

        o_ref[...] += _dot_tn(a_ref[...], b_ref[...])

    plain = lambda w: pl.BlockSpec((tt, w), lambda k, t: (t, 0))
    chunk = lambda w: pl.BlockSpec((None, tt, w), lambda k, t: (k, t, 0))
    return pl.pallas_call(
        body, name=name, grid=(N_CHIPS, T // tt), out_shape=jax.ShapeDtypeStruct((N_CHIPS, K, N), F32),
        in_specs=[chunk(K) if a_chunked else plain(K), plain(N) if a_chunked else chunk(N)],
        out_specs=pl.BlockSpec((None, K, N), lambda k, t: (k, 0, 0)),
        compiler_params=_params(("arbitrary", "arbitrary")),
    )(a, b)


def _adamw(w, g, m, v, name):
    R, C = w.shape
    tr = _row_tile(R)

    def body(w_ref, g_ref, m_ref, v_ref, d_ref, nm_ref, nv_ref):
        gv = g_ref[...]
        nm = ADAM_B1 * m_ref[...] + (1.0 - ADAM_B1) * gv
        nv = ADAM_B2 * v_ref[...] + (1.0 - ADAM_B2) * (gv * gv)
        m_hat = nm / (1.0 - ADAM_B1 ** ADAM_STEP)
        v_hat = nv / (1.0 - ADAM_B2 ** ADAM_STEP)
        d_ref[...] = -ADAM_LR * (m_hat / (jnp.sqrt(v_hat) + ADAM_EPS) + ADAM_WD * w_ref[...])
        nm_ref[...] = nm
        nv_ref[...] = nv

    blk = pl.BlockSpec((tr, C), lambda i: (i, 0))
    sh = jax.ShapeDtypeStruct((R, C), F32)
    return pl.pallas_call(
        body, name=name, grid=(R // tr,), out_shape=(sh, sh, sh),
        in_specs=[blk, blk, blk, blk], out_specs=[blk, blk, blk],
        compiler_params=_params(("arbitrary",)),
    )(w, g, m, v)


def _arrange(w):
    win = w["w_in"]
    dt = win.dtype
    z = lambda r, c: jnp.zeros((r, c), dt)
    zh = lambda c: jnp.zeros((Q_RANK, HEADS, c), dt)
    kr1, kr2 = win[:, 1928:1944], win[:, 1944:1960]
    misc_a = jnp.concatenate([win[:, 1536:1544], z(1024, 56), kr1, kr2, z(1024, 32)], axis=1)
    misc_b = jnp.concatenate([z(1024, 64), kr2, kr1, z(1024, 32)], axis=1)
    w_in = jnp.concatenate([win[:, 0:1536], win[:, 1544:1928], misc_a, misc_b], axis=1)
    wq = w["w_uq"].reshape(Q_RANK, HEADS, 96)
    q1 = jnp.concatenate([wq, zh(32)], axis=2).reshape(Q_RANK, 1024)
    q2 = jnp.concatenate([zh(64), wq[:, :, 80:96], wq[:, :, 64:80], zh(32)], axis=2).reshape(Q_RANK, 1024)
    wkv = w["w_ukv"].reshape(KV_RANK, HEADS, 128)
    wk = jnp.concatenate([wkv[:, :, 0:64], jnp.zeros((KV_RANK, HEADS, 64), dt)], axis=2).reshape(KV_RANK, 1024)
    wv = wkv[:, :, 64:128].reshape(KV_RANK, 512)
    return dict(w_in=w_in, w_q12=jnp.concatenate([q1, q2], axis=1), w_k=wk, w_v=wv, w_kv=jnp.concatenate([wk, wv], axis=1))


def _unarrange(g_in, g_q12, g_kv):
    kr1 = g_in[:, C_MA + 64:C_MA + 80] + g_in[:, C_MB + 80:C_MB + 96]
    kr2 = g_in[:, C_MA + 80:C_MA + 96] + g_in[:, C_MB + 64:C_MB + 80]
    w_in = jnp.concatenate([g_in[:, 0:1536], g_in[:, C_MA:C_MA + 8], g_in[:, 1536:1920], kr1, kr2], axis=1)
    g1 = g_q12[:, 0:1024].reshape(Q_RANK, HEADS, 128)
    g2 = g_q12[:, 1024:2048].reshape(Q_RANK, HEADS, 128)
    w_uq = jnp.concatenate([g1[:, :, 0:64], g1[:, :, 64:80] + g2[:, :, 80:96], g1[:, :, 80:96] + g2[:, :, 64:80]], axis=2).reshape(Q_RANK, 768)
    gk = g_kv[:, 0:1024].reshape(KV_RANK, HEADS, 128)
    gv = g_kv[:, 1024:1536].reshape(KV_RANK, HEADS, 64)
    w_ukv = jnp.concatenate([gk[:, :, 0:64], gv], axis=2).reshape(KV_RANK, 1024)
    return dict(w_in=w_in, w_uq=w_uq, w_ukv=w_ukv)


def _selectors():
    sel = np.zeros((384, 1024), np.float32)
    sel_t = np.zeros((1024, LANES), np.float32)
    for h in range(HEADS):
        for piece in range(3):
            sel[LANES * piece + h, LANES * h + 64 + piece] = 1.0
        sel_t[LANES * h + 64, h] = 1.0
    return jnp.asarray(sel, BF16), jnp.asarray(sel_t, BF16)


def _rope_tables(positions):
    inv_freq = 10000.0 ** (-jnp.arange(0, ROPE, 2, dtype=F32) / ROPE)
    ang = positions.reshape(-1).astype(F32)[:, None] * inv_freq[None, :]
    cos, sin = jnp.cos(ang), jnp.sin(ang)
    z64, z32 = jnp.zeros((ang.shape[0], 64), F32), jnp.zeros((ang.shape[0], 32), F32)
    return jnp.concatenate([z64, cos, cos, z32], axis=1), jnp.concatenate([z64, -sin, sin, z32], axis=1)


def _pack_shards(t):
    flat = jnp.concatenate([t[n].reshape(-1) for n in HEAD3])
    return jnp.pad(flat, (0, PACK_ROWS * 1024 - flat.shape[0])).reshape(PACK_ROWS, 1024)


def _unpack_full(slabs):
    flat = slabs.reshape(N_CHIPS, -1)
    out, off = {}, 0
    for n in HEAD3:
        r, c = FULL_SHAPES[n]
        cnt = r * c // N_CHIPS
        out[n] = flat[:, off:off + cnt].reshape(N_CHIPS, r, c // N_CHIPS).transpose(1, 0, 2).reshape(r, c)
        off += cnt
    return out


def _pack_full(g):
    parts = []
    for n in HEAD3:
        r, c = FULL_SHAPES[n]
        parts.append(g[n].reshape(r, N_CHIPS, c // N_CHIPS).transpose(1, 0, 2).reshape(N_CHIPS, -1))
    flat = jnp.concatenate(parts, axis=1)
    return jnp.pad(flat, ((0, 0), (0, PACK_ROWS * 1024 - flat.shape[1]))).reshape(N_CHIPS, PACK_ROWS, 1024)


def _unpack_shard(slab, like):
    flat = slab.reshape(-1)
    out, off = {}, 0
    for n in HEAD3:
        cnt = like[n].size
        out[n] = flat[off:off + cnt].reshape(like[n].shape)
        off += cnt
    return out


def _reduce_tail(gpack):
    tr = _row_tile(gpack.shape[1] // 2)
    got, = _run_exchange(_swap_exchange([gpack]), "tail_swap")
    sums = _add_half(gpack, got, tr)
    recv, = _run_exchange(_scatter_exchange([sums]), "tail_scatter")
    joined, = _run_exchange(_join_exchange([_sum_slabs(gpack, got, recv, tr)]), "tail_join")
    return joined


SMALL_ROWS = {"norm_mix_g": (0, 1024), "norm_ffn_g": (1, 1024), "final_norm_g": (2, 1024), "q_norm_g": (4, 256),
              "kv_norm_g": (5, 128), "b_fgate": (6, 8)}


def kernel(x, positions, norm_mix_g, w_in, b_fgate, q_norm_g, w_uq, kv_norm_g, w_ukv, fox_out_g, mla_out_g, w_o, norm_ffn_g, w_gate, w_up, w_down, final_norm_g, loss_target, m_norm_mix_g, m_w_in, m_b_fgate, m_q_norm_g, m_w_uq, m_kv_norm_g, m_w_ukv, m_fox_out_g, m_mla_out_g, m_w_o, m_norm_ffn_g, m_w_gate, m_w_up, m_w_down, m_final_norm_g, v_norm_mix_g, v_w_in, v_b_fgate, v_q_norm_g, v_w_uq, v_kv_norm_g, v_w_ukv, v_fox_out_g, v_mla_out_g, v_w_o, v_norm_ffn_g, v_w_gate, v_w_up, v_w_down, v_final_norm_g):
    names = ["norm_mix_g", "w_in", "b_fgate", "q_norm_g", "w_uq", "kv_norm_g", "w_ukv", "fox_out_g", "mla_out_g", "w_o",
             "norm_ffn_g", "w_gate", "w_up", "w_down", "final_norm_g"]
    wts = dict(zip(names, [norm_mix_g, w_in, b_fgate, q_norm_g, w_uq, kv_norm_g, w_ukv, fox_out_g, mla_out_g, w_o, norm_ffn_g,
                           w_gate, w_up, w_down, final_norm_g]))
    mom = dict(zip(names, [m_norm_mix_g, m_w_in, m_b_fgate, m_q_norm_g, m_w_uq, m_kv_norm_g, m_w_ukv, m_fox_out_g, m_mla_out_g,
                           m_w_o, m_norm_ffn_g, m_w_gate, m_w_up, m_w_down, m_final_norm_g]))
    var = dict(zip(names, [v_norm_mix_g, v_w_in, v_b_fgate, v_q_norm_g, v_w_uq, v_kv_norm_g, v_w_ukv, v_fox_out_g, v_mla_out_g,
                           v_w_o, v_norm_ffn_g, v_w_gate, v_w_up, v_w_down, v_final_norm_g]))
    shard = {n: wts[n][0] for n in HEAD3 + FFN4}
    nb, seq, _ = x.shape
    T = nb * seq
    tm, tq = min(ROW_TILE, seq), min(ATTN_TILE, seq)
    tt = min(512, T)
    xf = x.reshape(T, D_MODEL)
    tgt = loss_target.reshape(T, D_MODEL)

    pack = _pack_shards(shard).astype(BF16)
    head, = _run_exchange(_gather_exchange([pack], own=False), "gather_head")
    head = lax.dynamic_update_slice(head, pack[None], (2 * lax.axis_index("x") + lax.axis_index("y"), 0, 0))
    a = _arrange(_unpack_full(head))
    sel, sel_t = _selectors()
    ct, st = _rope_tables(positions)
    bfg = jnp.concatenate([b_fgate, jnp.zeros((1, LANES - HEADS), F32)], axis=1)
    g1, gq, gkv = norm_mix_g, q_norm_g, kv_norm_g

    h1, qf, kf, vf, qm, km, vm, lat, qn, kvn = _in_proj(xf, g1, a["w_in"], a["w_q12"], a["w_k"], a["w_v"], gq, gkv, bfg, ct, st, sel, seq, tm)
    of, lse_f, gathered = _attn_fwd(qf, kf, vf, nb, seq, tq, "fox_fwd", _gather_exchange([shard[n].astype(BF16) for n in FFN4]))
    om, lse_m, _ = _attn_fwd(qm, km, vm, nb, seq, tq, "mla_fwd")
    wo4, wg4, wu4, wd4 = gathered
    a_cat, h2, hid, dg, du, dx3, dx2, dof, dom, st_mid = _mid(
        of, om, xf, tgt, fox_out_g, mla_out_g, norm_ffn_g, final_norm_g.reshape(1, D_MODEL),
        wo4.reshape(D_MODEL, D_MODEL), wg4, wu4, wd4, tm)

    big = [_wgrad(a_cat, dx2, 512, tt, "wgrad_o").reshape(N_CHIPS, D_MODEL // N_CHIPS, D_MODEL),
           _wgrad_chunked(h2, dg, tt, "wgrad_gate"), _wgrad_chunked(h2, du, tt, "wgrad_up"), _wgrad_chunked(hid, dx3, tt, "wgrad_down")]
    dqf, dkf, dvf, got = _attn_bwd(qf, kf, vf, of, dof, lse_f, nb, seq, tq, "fox_bwd", True, _swap_exchange(big))
    sums = [_add_half(g, s, _row_tile(s.shape[1])) for g, s in zip(big, got)]
    dqm, dkm, dvm, recv = _attn_bwd(qm, km, vm, om, dom, lse_m, nb, seq, tq, "mla_bwd", False, _scatter_exchange(sums))
    halves = [_sum_slabs(g, s, r, _row_tile(s.shape[1])) for g, s, r in zip(big, got, recv)]
    dx, dproj, dq12, dkv, st_in, _ = _in_bwd(dqf, dkf, dvf, dqm, dkm, dvm, lat, xf, dx2, g1, gq, gkv, bfg, ct, st, sel_t,
                                             a["w_in"], a["w_q12"], a["w_kv"], seq, tm)
    g_in, joined = _wgrad(h1, dproj, C_END, tt, "wgrad_in", _join_exchange(halves))
    gshard = dict(zip(FFN4, joined))

    loss_row = jnp.concatenate([jnp.sum(st_mid[3:4, :], axis=1, keepdims=True), jnp.zeros((1, D_MODEL - 1), F32)], axis=1)
    stats = jnp.concatenate([st_in[0:1], st_mid[1:2], st_mid[0:1], st_mid[2:3], st_in[1:2], st_in[2:3], st_in[3:4], loss_row], axis=0)
    stats = _allreduce_small(stats)
    tail = _unarrange(g_in, _wgrad(qn, dq12, 2048, tt, "wgrad_uq"), _wgrad(kvn, dkv, 1536, tt, "wgrad_ukv"))
    gshard.update(_unpack_shard(_reduce_tail(_pack_full(tail)), shard))

    grads, delta, new_m, new_v = {}, {}, {}, {}
    for n in HEAD3 + FFN4:
        grads[n] = gshard[n][None]
        d, nm, nv = _adamw(shard[n], gshard[n], mom[n][0], var[n][0], "adamw_" + n)
        delta[n], new_m[n], new_v[n] = d[None], nm[None], nv[None]
    sm_g = {n: stats[row:row + 1, 0:width] for n, (row, width) in SMALL_ROWS.items()}
    sm_g["fox_out_g"] = stats[3:4, 0:512]
    sm_g["mla_out_g"] = stats[3:4, 512:1024]
    pad = lambda t: jnp.pad(t.reshape(1, -1), ((0, 0), (0, 1024 - t.size)))
    stack = lambda d: jnp.concatenate([pad(d[n]) for n in SMALL], axis=0)
    sd, sm, sv = _adamw(stack(wts), stack(sm_g), stack(mom), stack(var), "adamw_small")
    for i, n in enumerate(SMALL):
        shp = wts[n].shape
        grads[n] = sm_g[n].reshape(shp)
        delta[n] = sd[i, 0:wts[n].size].reshape(shp)
        new_m[n] = sm[i, 0:wts[n].size].reshape(shp)
        new_v[n] = sv[i, 0:wts[n].size].reshape(shp)
    loss = stats[7, 0]
    return (loss, dx.reshape(x.shape), *[grads[n] for n in names], *[delta[n] for n in names],
            *[new_m[n] for n in names], *[new_v[n] for n in names])
```

```python
import functools

import numpy as np
import jax
import jax.numpy as jnp
from jax import lax
from jax.experimental import pallas as pl
from jax.experimental.pallas import tpu as pltpu

F32 = jnp.float32
BF16 = jnp.bfloat16
MESH = pl.DeviceIdType.MESH

EPS = 1e-6
D_MODEL = 1024
HEADS = 8
PAIRS = HEADS // 2
FOX_W = 512
Q_RANK = 256
KV_RANK = 128
ROPE = 32
D_FF = 2816
N_CHIPS = 4
FF_CHUNK = D_FF // N_CHIPS
FOX_SCALE = 64 ** -0.5
MLA_SCALE = 96 ** -0.5
LANES = 128
NEG = -1e30

ADAM_LR, ADAM_B1, ADAM_B2, ADAM_EPS, ADAM_WD, ADAM_STEP = 0.001, 0.9, 0.999, 1e-08, 0.01, 10

C_FQ, C_FK, C_FV, C_QL, C_KVL, C_MA, C_MB, C_END = 0, 512, 1024, 1536, 1792, 1920, 2048, 2176

VMEM_LIMIT = 60 * 1024 * 1024
ROW_TILE = 256
ATTN_TILE = 512
WGRAD_TILE = 2048

HEAD3 = ("w_in", "w_uq", "w_ukv")
FFN4 = ("w_o", "w_gate", "w_up", "w_down")
COL_SHARDED = ("w_in", "w_uq", "w_ukv", "w_gate", "w_up")
SMALL = ("norm_mix_g", "b_fgate", "q_norm_g", "kv_norm_g", "fox_out_g", "mla_out_g", "norm_ffn_g", "final_norm_g")
FULL_SHAPES = {"w_in": (1024, 1960), "w_uq": (256, 768), "w_ukv": (128, 1024)}
PACK_ELEMS = sum(a * b for a, b in FULL_SHAPES.values()) // N_CHIPS
PACK_ROWS = -(-PACK_ELEMS // (1024 * 32)) * 32


def _params(sem=None):
    return pltpu.CompilerParams(dimension_semantics=sem, vmem_limit_bytes=VMEM_LIMIT)


def _full(shape):
    n = len(shape)
    return pl.BlockSpec(shape, lambda *_: (0,) * n, pipeline_mode=pl.Buffered(1))


def _dot(a, b):
    return jnp.dot(a, b, preferred_element_type=F32)


def _dot_nt(a, b):
    return lax.dot_general(a, b, (((1,), (1,)), ((), ())), preferred_element_type=F32)


def _dot_tn(a, b):
    return lax.dot_general(a, b, (((0,), (0,)), ((), ())), preferred_element_type=F32)


def _split3(v):
    hi = v.astype(BF16)
    r1 = v - hi.astype(F32)
    mid = r1.astype(BF16)
    lo = (r1 - mid.astype(F32)).astype(BF16)
    return hi, mid, lo


def _rms(v, width):
    return lax.rsqrt(jnp.sum(v * v, axis=1, keepdims=True) * (1.0 / width) + EPS)


def _rms_bwd(dy, xhat, r, g, width):
    u = dy * g
    return r * (u - xhat * (jnp.sum(u * xhat, axis=1, keepdims=True) * (1.0 / width)))


ANY = pl.BlockSpec(memory_space=pl.ANY)


def _place():
    return lax.axis_index("x"), lax.axis_index("y"), lax.axis_index("c")


def _other_chips(x, y):
    return [(1 - x, y), (x, 1 - y), (1 - x, 1 - y)]


def _remote(src, dst, send, recv, j, dev):
    return pltpu.make_async_remote_copy(src_ref=src, dst_ref=dst, send_sem=send.at[j], recv_sem=recv.at[j], device_id=dev, device_id_type=MESH)


class _Exchange:
    def __init__(self, ins, outs, n_remote, n_local, build, in_place=False):
        self.ins, self.outs, self.n_remote, self.n_local, self.build = list(ins), list(outs), n_remote, max(n_local, 1), build
        self.in_place = in_place

    def aliases(self, first_in, first_out):
        return {first_in + i: first_out + i for i in range(len(self.ins))} if self.in_place else {}

    def sems(self):
        return [pltpu.SemaphoreType.DMA((self.n_remote,)), pltpu.SemaphoreType.DMA((self.n_remote,)), pltpu.SemaphoreType.DMA((self.n_local,))]

    def start(self, in_refs, out_refs, sems):
        for cp in self.build(in_refs, out_refs, *sems)[0]:
            cp.start()

    def wait(self, in_refs, out_refs, sems):
        for w in self.build(in_refs, out_refs, *sems)[1]:
            w()


def _gather_exchange(shards, own=True):
    def build(ins, outs, send, recv, lsem):
        x, y, c = _place()
        starts, waits = [], []
        for i, (s, o) in enumerate(zip(ins, outs)):
            if own:
                mine = pltpu.make_async_copy(s, o.at[2 * x + y], lsem.at[i])
                starts.append(mine)
                waits.append(mine.wait)
            for j, (cx, cy) in enumerate(_other_chips(x, y)):
                out = _remote(s, o.at[2 * x + y], send, recv, 3 * i + j, (cx, cy, c))
                starts.append(out)
                waits.append(_remote(s, o.at[2 * cx + cy], send, recv, 3 * i + j, (cx, cy, c)).wait_recv)
                waits.append(out.wait_send)
        return starts, waits

    outs = [jax.ShapeDtypeStruct((N_CHIPS,) + s.shape, s.dtype) for s in shards]
    return _Exchange(shards, outs, 3 * len(shards), len(shards), build)


def _swap_exchange(grads):
    def build(ins, outs, send, recv, lsem):
        x, y, c = _place()
        cps = []
        for i, (g, o) in enumerate(zip(ins, outs)):
            hr = g.shape[1] // 2
            cps.append(_remote(g.at[:, pl.ds((1 - c) * hr, hr), :], o, send, recv, i, (x, y, 1 - c)))
        return cps, [cp.wait for cp in cps]

    outs = [jax.ShapeDtypeStruct((g.shape[0], g.shape[1] // 2, g.shape[2]), g.dtype) for g in grads]
    return _Exchange(grads, outs, len(grads), 0, build)


def _scatter_exchange(sums):
    def build(ins, outs, send, recv, lsem):
        x, y, c = _place()
        cps = []
        for i, (s, o) in enumerate(zip(ins, outs)):
            for j, (cx, cy) in enumerate(_other_chips(x, y)):
                cps.append(_remote(s.at[2 * cx + cy], o.at[j], send, recv, 3 * i + j, (cx, cy, c)))
        return cps, [cp.wait for cp in cps]

    outs = [jax.ShapeDtypeStruct((3,) + s.shape[1:], s.dtype) for s in sums]
    return _Exchange(sums, outs, 3 * len(sums), 0, build)


def _join_exchange(bufs):
    def build(ins, outs, send, recv, lsem):
        x, y, c = _place()
        starts, waits = [], []
        for i, (t, o) in enumerate(zip(ins, outs)):
            hr = t.shape[0] // 2
            out = _remote(t.at[pl.ds(c * hr, hr), :], o.at[pl.ds(c * hr, hr), :], send, recv, i, (x, y, 1 - c))
            starts.append(out)
            waits += [_remote(t.at[pl.ds(c * hr, hr), :], o.at[pl.ds((1 - c) * hr, hr), :], send, recv, i, (x, y, 1 - c)).wait_recv,
                      out.wait_send]
        return starts, waits

    outs = [jax.ShapeDtypeStruct(t.shape, t.dtype) for t in bufs]
    return _Exchange(bufs, outs, len(bufs), 0, build, in_place=True)


def _run_exchange(ex, name):
    n_in, n_out = len(ex.ins), len(ex.outs)

    def body(*refs):
        ins, outs, sems = refs[:n_in], refs[n_in:n_in + n_out], refs[n_in + n_out:]
        ex.start(ins, outs, sems)
        ex.wait(ins, outs, sems)

    return pl.pallas_call(
        body, name=name, out_shape=tuple(ex.outs), in_specs=[ANY] * n_in, out_specs=tuple([ANY] * n_out),
        scratch_shapes=ex.sems(), input_output_aliases=ex.aliases(0, 0),
        compiler_params=pltpu.CompilerParams(has_side_effects=True),
    )(*ex.ins)


def _allreduce_small(v):
    def body(v_ref, o_ref, buf, send, recv):
        x, y, c = _place()
        me = 4 * x + 2 * y + c
        buf[me] = v_ref[...]
        out = []
        for j in range(7):
            fx, fy, fc = (j + 1) >> 2 & 1, (j + 1) >> 1 & 1, (j + 1) & 1
            out.append(_remote(v_ref, buf.at[me], send, recv, j, (x ^ fx, y ^ fy, c ^ fc)))
        for cp in out:
            cp.start()
        for j in range(7):
            fx, fy, fc = (j + 1) >> 2 & 1, (j + 1) >> 1 & 1, (j + 1) & 1
            src = 4 * (x ^ fx) + 2 * (y ^ fy) + (c ^ fc)
            _remote(v_ref, buf.at[src], send, recv, j, (x ^ fx, y ^ fy, c ^ fc)).wait_recv()
        for cp in out:
            cp.wait_send()
        acc = buf[0]
        for d in range(1, 8):
            acc = acc + buf[d]
        o_ref[...] = acc

    vm = pl.BlockSpec(memory_space=pltpu.VMEM)
    return pl.pallas_call(
        body, name="allreduce_small", out_shape=jax.ShapeDtypeStruct(v.shape, v.dtype),
        in_specs=[vm], out_specs=vm,
        scratch_shapes=[pltpu.VMEM((8,) + v.shape, v.dtype), pltpu.SemaphoreType.DMA((7,)), pltpu.SemaphoreType.DMA((7,))],
        compiler_params=pltpu.CompilerParams(has_side_effects=True),
    )(v)


def _add_half(g, got, tr):
    n, R, C = g.shape
    hr = R // 2
    nb = hr // tr

    def body(c_ref, g_ref, r_ref, o_ref):
        o_ref[...] = (g_ref[...] + r_ref[...]).astype(BF16)

    c = lax.axis_index("c")
    return pl.pallas_call(
        body, name="add_half",
        grid_spec=pltpu.PrefetchScalarGridSpec(
            num_scalar_prefetch=1, grid=(n, nb),
            in_specs=[pl.BlockSpec((1, tr, C), lambda k, i, c_ref: (k, c_ref[0] * nb + i, 0)),
                      pl.BlockSpec((1, tr, C), lambda k, i, c_ref: (k, i, 0))],
            out_specs=pl.BlockSpec((1, tr, C), lambda k, i, c_ref: (k, i, 0))),
        out_shape=jax.ShapeDtypeStruct((n, hr, C), BF16),
        compiler_params=_params(("arbitrary", "arbitrary")),
    )(jnp.reshape(c, (1,)).astype(jnp.int32), g, got)


def _sum_slabs(g, got, recv, tr):
    n, R, C = g.shape
    hr = R // 2
    nb = hr // tr

    def body(kc_ref, g_ref, s_ref, r_ref, o_ref):
        o_ref[...] = (((g_ref[0] + s_ref[0]) + r_ref[0].astype(F32)) + r_ref[1].astype(F32)) + r_ref[2].astype(F32)

    kc = jnp.stack([2 * lax.axis_index("x") + lax.axis_index("y"), lax.axis_index("c")]).astype(jnp.int32)
    return pl.pallas_call(
        body, name="sum_slabs",
        grid_spec=pltpu.PrefetchScalarGridSpec(
            num_scalar_prefetch=1, grid=(nb,),
            in_specs=[pl.BlockSpec((1, tr, C), lambda i, kc_ref: (kc_ref[0], kc_ref[1] * nb + i, 0)),
                      pl.BlockSpec((1, tr, C), lambda i, kc_ref: (kc_ref[0], i, 0)),
                      pl.BlockSpec((3, tr, C), lambda i, kc_ref: (0, i, 0))],
            out_specs=pl.BlockSpec((tr, C), lambda i, kc_ref: (kc_ref[1] * nb + i, 0))),
        out_shape=jax.ShapeDtypeStruct((R, C), F32),
        compiler_params=_params(("arbitrary",)),
    )(kc, g, got, recv)


def _row_tile(rows):
    for cand in (256, 184, 176, 144, 128, 64, 32, 16, 8):
        if rows % cand == 0:
            return cand
    return rows


def _in_proj(x, g1, w_in, w_q12, w_k, w_v, gq, gkv, bfg, ct, st, sel, seq, tm):
    T = x.shape[0]
    nsb = seq // tm

    def body(x_ref, g1_ref, win_ref, wq_ref, wk_ref, wv_ref, gq_ref, gkv_ref, b_ref, ct_ref, st_ref, sel_ref,
             h1_ref, qf_ref, kf_ref, vf_ref, qm_ref, km_ref, vm_ref, lat_ref, qn_ref, kvn_ref, carry):
        i = pl.program_id(0)

        @pl.when(i % nsb == 0)
        def _():
            carry[...] = jnp.zeros_like(carry)

        xv = x_ref[...]
        h = (xv * _rms(xv, D_MODEL) * g1_ref[...]).astype(BF16)
        h1_ref[...] = h
        proj = _dot(h, win_ref[...])
        lane = lax.broadcasted_iota(jnp.int32, (tm, LANES), 1)
        low = lane < 64
        misc_a = proj[:, C_MA:C_MB]
        misc_b = proj[:, C_MB:C_END]

        z = misc_a + b_ref[...]
        lf = jnp.where(lane < HEADS, jnp.minimum(z, 0.0) - jnp.log1p(jnp.exp(-jnp.abs(z))), 0.0)
        rr = lax.broadcasted_iota(jnp.int32, (tm, tm), 0)
        cc = lax.broadcasted_iota(jnp.int32, (tm, tm), 1)
        tri = (rr >= cc).astype(BF16)
        a0, a1, a2 = _split3(lf)
        c = _dot(tri, a0) + _dot(tri, a1) + _dot(tri, a2) + carry[0:1, :]
        carry[0:1, :] = c[tm - 1:tm, :]
        c0, c1, c2 = _split3(c)
        cpl = _dot(jnp.concatenate([c0, c1, c2], axis=1), sel_ref[...])
        qpad = jnp.where((lane >= 64) & (lane < 67), -1.0, 0.0)
        for j in range(PAIRS):
            qc = proj[:, C_FQ + LANES * j:C_FQ + LANES * (j + 1)] * FOX_SCALE
            kc = proj[:, C_FK + LANES * j:C_FK + LANES * (j + 1)]
            e, o = 2 * LANES * j, 2 * LANES * j + LANES
            qf_ref[:, e:e + LANES] = jnp.where(low, qc, qpad).astype(BF16)
            qf_ref[:, o:o + LANES] = jnp.where(low, pltpu.roll(qc, 64, 1), qpad).astype(BF16)
            kf_ref[:, e:e + LANES] = jnp.where(low, kc, cpl[:, e:e + LANES]).astype(BF16)
            kf_ref[:, o:o + LANES] = jnp.where(low, pltpu.roll(kc, 64, 1), cpl[:, o:o + LANES]).astype(BF16)
        vf_ref[...] = proj[:, C_FV:C_QL].astype(BF16)

        ql = proj[:, C_QL:C_KVL]
        kvl = proj[:, C_KVL:C_MA]
        qn = (ql * _rms(ql, Q_RANK) * gq_ref[...]).astype(BF16)
        kvn = (kvl * _rms(kvl, KV_RANK) * gkv_ref[...]).astype(BF16)
        lat_ref[...] = proj[:, C_QL:C_MB]
        qn_ref[...] = qn
        kvn_ref[...] = kvn
        q12 = _dot(qn, wq_ref[...])
        kn = _dot(kvn, wk_ref[...])
        ctv = ct_ref[...]
        stv = st_ref[...]
        cq = (jnp.where(low, 1.0, 0.0) + ctv) * MLA_SCALE
        sq = stv * MLA_SCALE
        kpe = misc_a * ctv + misc_b * stv
        for hd in range(HEADS):
            s0 = LANES * hd
            qm_ref[:, s0:s0 + LANES] = (q12[:, s0:s0 + LANES] * cq + q12[:, 1024 + s0:1024 + s0 + LANES] * sq).astype(BF16)
            km_ref[:, s0:s0 + LANES] = (kn[:, s0:s0 + LANES] + kpe).astype(BF16)
        vm_ref[...] = _dot(kvn, wv_ref[...]).astype(BF16)

    row = lambda w: pl.BlockSpec((tm, w), lambda i: (i, 0))
    out_shape = (
        jax.ShapeDtypeStruct((T, D_MODEL), BF16),
        jax.ShapeDtypeStruct((T, 1024), BF16), jax.ShapeDtypeStruct((T, 1024), BF16), jax.ShapeDtypeStruct((T, 512), BF16),
        jax.ShapeDtypeStruct((T, 1024), BF16), jax.ShapeDtypeStruct((T, 1024), BF16), jax.ShapeDtypeStruct((T, 512), BF16),
        jax.ShapeDtypeStruct((T, 512), F32),
        jax.ShapeDtypeStruct((T, Q_RANK), BF16), jax.ShapeDtypeStruct((T, KV_RANK), BF16),
    )
    return pl.pallas_call(
        body, name="in_proj", grid=(T // tm,), out_shape=out_shape,
        in_specs=[row(D_MODEL), _full(g1.shape), _full(w_in.shape), _full(w_q12.shape), _full(w_k.shape), _full(w_v.shape),
                  _full(gq.shape), _full(gkv.shape), _full(bfg.shape), row(LANES), row(LANES), _full(sel.shape)],
        out_specs=[row(D_MODEL), row(1024), row(1024), row(512), row(1024), row(1024), row(512), row(512), row(Q_RANK), row(KV_RANK)],
        scratch_shapes=[pltpu.VMEM((8, LANES), F32)],
        compiler_params=_params(("arbitrary",)),
    )(x, g1, w_in, w_q12, w_k, w_v, gq, gkv, bfg, ct, st, sel)


def _attn_fwd(q, k, v, nb, seq, tq, name, ex=None):
    T = q.shape[0]
    nq = seq // tq
    n_in, n_out = (len(ex.ins), len(ex.outs)) if ex else (0, 0)

    def body(*refs):
        q_ref, k_ref, v_ref = refs[0:3]
        o_ref, lse_ref = refs[3 + n_in:5 + n_in]
        b, pr, qi = pl.program_id(0), pl.program_id(1), pl.program_id(2)
        if ex:
            ex_refs = (refs[3:3 + n_in], refs[5 + n_in:5 + n_in + n_out], refs[5 + n_in + n_out:])

            @pl.when((b == 0) & (pr == 0) & (qi == 0))
            def _():
                ex.start(*ex_refs)

        lane = lax.broadcasted_iota(jnp.int32, (tq, LANES), 1)
        rr = lax.broadcasted_iota(jnp.int32, (tq, tq), 0)
        cc = lax.broadcasted_iota(jnp.int32, (tq, tq), 1)
        def step(kj, carry, masked):
            rows = pl.ds(pl.multiple_of(kj * tq, tq), tq)
            vv = v_ref[rows, :]
            new = []
            for hh in range(2):
                m, l, acc = carry[hh]
                s = _dot_nt(q_ref[:, LANES * hh:LANES * (hh + 1)], k_ref[rows, LANES * hh:LANES * (hh + 1)])
                if masked:
                    s = jnp.where(cc <= rr, s, NEG)
                m_new = jnp.maximum(m, jnp.max(s, axis=1, keepdims=True))
                alpha = jnp.exp(m - m_new)
                p = jnp.exp(s - m_new)
                l = alpha * l + jnp.sum(p, axis=1, keepdims=True)
                acc = alpha * acc + _dot(p.astype(BF16), vv)
                new.append((m_new, l, acc))
            return tuple(new)

        one = (jnp.full((tq, 1), NEG, F32), jnp.zeros((tq, 1), F32), jnp.zeros((tq, LANES), F32))
        carry = lax.fori_loop(0, qi, functools.partial(step, masked=False), (one, one))
        (m0, l0, acc0), (m1, l1, acc1) = step(qi, carry, True)
        lse_ref[:, 0:LANES] = jnp.broadcast_to(m0 + jnp.log(l0), (tq, LANES))
        lse_ref[:, LANES:2 * LANES] = jnp.broadcast_to(m1 + jnp.log(l1), (tq, LANES))
        o_ref[...] = jnp.where(lane < 64, acc0 / l0, acc1 / l1)

        if ex:
            @pl.when((b == nb - 1) & (pr == PAIRS - 1) & (qi == nq - 1))
            def _():
                ex.wait(*ex_refs)

    res = pl.pallas_call(
        body, name=name, grid=(nb, PAIRS, nq),
        out_shape=(jax.ShapeDtypeStruct((T, 512), F32), jax.ShapeDtypeStruct((T, 1024), F32)) + tuple(ex.outs if ex else ()),
        in_specs=[pl.BlockSpec((tq, 2 * LANES), lambda b, p, i: (b * nq + i, p)),
                  pl.BlockSpec((seq, 2 * LANES), lambda b, p, i: (b, p)),
                  pl.BlockSpec((seq, LANES), lambda b, p, i: (b, p))] + [ANY] * n_in,
        out_specs=[pl.BlockSpec((tq, LANES), lambda b, p, i: (b * nq + i, p)),
                   pl.BlockSpec((tq, 2 * LANES), lambda b, p, i: (b * nq + i, p))] + [ANY] * n_out,
        scratch_shapes=ex.sems() if ex else [],
        compiler_params=_params(("arbitrary", "arbitrary", "arbitrary")),
    )(q, k, v, *(ex.ins if ex else ()))
    return res[0], res[1], list(res[2:])


def _attn_bwd(q, k, v, o, do, lse, nb, seq, tq, name, key_bias, ex=None):
    T = q.shape[0]
    nq = seq // tq
    n_in, n_out = (len(ex.ins), len(ex.outs)) if ex else (0, 0)

    def body(*refs):
        q_ref, k_ref, v_ref, o_ref, do_ref, lse_ref = refs[0:6]
        dq_ref, dk_ref, dv_ref = refs[6 + n_in:9 + n_in]
        dsc, rsum = refs[9 + n_in + n_out:11 + n_in + n_out]
        b, pr, kj = pl.program_id(0), pl.program_id(1), pl.program_id(2)
        if ex:
            ex_refs = (refs[6:6 + n_in], refs[9 + n_in:9 + n_in + n_out], refs[11 + n_in + n_out:])

            @pl.when((b == 0) & (pr == 0) & (kj == 0))
            def _():
                ex.start(*ex_refs)

        lane_s = lax.broadcasted_iota(jnp.int32, (seq, LANES), 1)
        lane = lax.broadcasted_iota(jnp.int32, (tq, LANES), 1)
        rr = lax.broadcasted_iota(jnp.int32, (tq, tq), 0)
        cc = lax.broadcasted_iota(jnp.int32, (tq, tq), 1)

        @pl.when(kj == 0)
        def _():
            dq_ref[...] = jnp.zeros_like(dq_ref)
            prod = do_ref[...].astype(F32) * o_ref[...]
            d0 = jnp.sum(jnp.where(lane_s < 64, prod, 0.0), axis=1, keepdims=True)
            d1 = jnp.sum(jnp.where(lane_s < 64, 0.0, prod), axis=1, keepdims=True)
            dsc[0] = jnp.broadcast_to(d0, (seq, LANES))
            dsc[1] = jnp.broadcast_to(d1, (seq, LANES))
            if key_bias:
                rsum[...] = jnp.zeros_like(rsum)

        vv = v_ref[...]

        def step(qi, carry, masked):
            dkt, dvt, cols = carry
            rows = pl.ds(pl.multiple_of(qi * tq, tq), tq)
            dov = do_ref[rows, :]
            new_dkt, new_cols = [], []
            for hh in range(2):
                qv = q_ref[rows, LANES * hh:LANES * (hh + 1)]
                kv = k_ref[:, LANES * hh:LANES * (hh + 1)]
                dom = jnp.where((lane < 64) if hh == 0 else (lane >= 64), dov, jnp.zeros((), BF16))
                s = _dot_nt(qv, kv)
                if masked:
                    s = jnp.where(cc <= rr, s, NEG)
                p = jnp.exp(s - lse_ref[rows, LANES * hh:LANES * hh + 1])
                dp = _dot_nt(dom, vv)
                ds32 = p * (dp - dsc[hh, rows, 0:1])
                col = cols[hh]
                if key_bias:
                    col = col + jnp.sum(ds32, axis=0, keepdims=True)
                    rsum[hh, rows, :] += jnp.broadcast_to(jnp.sum(ds32, axis=1, keepdims=True), (tq, LANES))
                ds = ds32.astype(BF16)
                dvt = dvt + _dot_tn(dom, p.astype(BF16))
                new_dkt.append(dkt[hh] + _dot_tn(qv, ds))
                new_cols.append(col)
                dq_ref[rows, LANES * hh:LANES * (hh + 1)] += _dot(ds, kv)
            return tuple(new_dkt), dvt, tuple(new_cols)

        zt = jnp.zeros((LANES, tq), F32)
        zc = jnp.zeros((1, tq), F32)
        carry = step(kj, ((zt, zt), zt, (zc, zc)), True)
        dkt, dvt, cols = lax.fori_loop(kj + 1, nq, functools.partial(step, masked=False), carry)
        row_t = lax.broadcasted_iota(jnp.int32, (LANES, tq), 0)
        for hh in range(2):
            dk_h = jnp.where(row_t == 64, -cols[hh], dkt[hh]) if key_bias else dkt[hh]
            dk_ref[:, LANES * hh:LANES * (hh + 1)] = dk_h.T
        dv_ref[...] = dvt.T

        if key_bias:
            @pl.when(kj == nq - 1)
            def _():
                for hh in range(2):
                    blk = dq_ref[:, LANES * hh:LANES * (hh + 1)]
                    dq_ref[:, LANES * hh:LANES * (hh + 1)] = jnp.where(lane_s == 64, rsum[hh], blk)

        if ex:
            @pl.when((b == nb - 1) & (pr == PAIRS - 1) & (kj == nq - 1))
            def _():
                ex.wait(*ex_refs)

    per_seq = lambda w: pl.BlockSpec((seq, w), lambda b, p, j: (b, p))
    per_blk = lambda w: pl.BlockSpec((tq, w), lambda b, p, j: (b * nq + j, p))
    res = pl.pallas_call(
        body, name=name, grid=(nb, PAIRS, nq),
        out_shape=(jax.ShapeDtypeStruct((T, 1024), F32), jax.ShapeDtypeStruct((T, 1024), F32), jax.ShapeDtypeStruct((T, 512), F32))
        + tuple(ex.outs if ex else ()),
        in_specs=[per_seq(2 * LANES), per_blk(2 * LANES), per_blk(LANES), per_seq(LANES), per_seq(LANES), per_seq(2 * LANES)] + [ANY] * n_in,
        out_specs=[per_seq(2 * LANES), per_blk(2 * LANES), per_blk(LANES)] + [ANY] * n_out,
        scratch_shapes=[pltpu.VMEM((2, seq, LANES), F32), pltpu.VMEM((2, seq, LANES) if key_bias else (2, 8, LANES), F32)]
        + (ex.sems() if ex else []),
        compiler_params=_params(("arbitrary", "arbitrary", "arbitrary")),
    )(q, k, v, o, do, lse, *(ex.ins if ex else ()))
    return res[0], res[1], res[2], list(res[3:])


def _mid(of, om, x, tgt, g_fo, g_mo, g2, g3, w_o, w_g, w_u, w_d, tm):
    T = x.shape[0]

    def body(of_ref, om_ref, x_ref, t_ref, gfo_ref, gmo_ref, g2_ref, g3_ref, wo_ref, wg_ref, wu_ref, wd_ref,
             a_ref, h2_ref, hid_ref, dg_ref, du_ref, dx3_ref, dx2_ref, dof_ref, dom_ref, st_ref):
        i = pl.program_id(0)

        @pl.when(i == 0)
        def _():
            st_ref[...] = jnp.zeros_like(st_ref)

        ofv, omv = of_ref[...], om_ref[...]
        rf, rm = _rms(ofv, FOX_W), _rms(omv, FOX_W)
        fhat, mhat = ofv * rf, omv * rm
        a = jnp.concatenate([fhat * gfo_ref[...], mhat * gmo_ref[...]], axis=1).astype(BF16)
        a_ref[...] = a
        x2 = x_ref[...] + _dot(a, wo_ref[...])
        r2 = _rms(x2, D_MODEL)
        xh2 = x2 * r2
        h2 = (xh2 * g2_ref[...]).astype(BF16)
        h2_ref[...] = h2
        x3 = x2
        kept = []
        for k in range(N_CHIPS):
            gt = _dot(h2, wg_ref[k])
            up = _dot(h2, wu_ref[k])
            sg = jax.nn.sigmoid(gt)
            sl = gt * sg
            hid = (sl * up).astype(BF16)
            hid_ref[k] = hid
            x3 = x3 + _dot(hid, wd_ref[k])
            kept.append((up * (sg * (1.0 + gt * (1.0 - sg))), sl))
        r3 = _rms(x3, D_MODEL)
        xh3 = x3 * r3
        diff = xh3 * g3_ref[...] - t_ref[...]
        dy = diff * (1.0 / D_MODEL)
        st_ref[3:4, :] += jnp.sum(diff * diff, axis=0, keepdims=True) * (0.5 / D_MODEL)
        st_ref[0:1, :] += jnp.sum(dy * xh3, axis=0, keepdims=True)
        dx3 = _rms_bwd(dy, xh3, r3, g3_ref[...], D_MODEL)
        dx3b = dx3.astype(BF16)
        dx3_ref[...] = dx3b
        dh2 = jnp.zeros((tm, D_MODEL), F32)
        for k in range(N_CHIPS):
            dhid = _dot_nt(dx3b, wd_ref[k])
            dg = (dhid * kept[k][0]).astype(BF16)
            du = (dhid * kept[k][1]).astype(BF16)
            dg_ref[k] = dg
            du_ref[k] = du
            dh2 = dh2 + _dot_nt(dg, wg_ref[k]) + _dot_nt(du, wu_ref[k])
        st_ref[1:2, :] += jnp.sum(dh2 * xh2, axis=0, keepdims=True)
        dx2 = dx3 + _rms_bwd(dh2, xh2, r2, g2_ref[...], D_MODEL)
        dx2_ref[...] = dx2
        da = _dot_nt(dx2.astype(BF16), wo_ref[...])
        daf, dam = da[:, 0:FOX_W], da[:, FOX_W:2 * FOX_W]
        st_ref[2:3, 0:FOX_W] += jnp.sum(daf * fhat, axis=0, keepdims=True)
        st_ref[2:3, FOX_W:2 * FOX_W] += jnp.sum(dam * mhat, axis=0, keepdims=True)
        dof_ref[...] = _rms_bwd(daf, fhat, rf, gfo_ref[...], FOX_W).astype(BF16)
        dom_ref[...] = _rms_bwd(dam, mhat, rm, gmo_ref[...], FOX_W).astype(BF16)

    row = lambda w: pl.BlockSpec((tm, w), lambda i: (i, 0))
    chunked = pl.BlockSpec((N_CHIPS, tm, FF_CHUNK), lambda i: (0, i, 0))
    ff = jax.ShapeDtypeStruct((N_CHIPS, T, FF_CHUNK), BF16)
    out_shape = (
        jax.ShapeDtypeStruct((T, 1024), BF16), jax.ShapeDtypeStruct((T, 1024), BF16), ff, ff, ff,
        jax.ShapeDtypeStruct((T, 1024), BF16), jax.ShapeDtypeStruct((T, 1024), F32),
        jax.ShapeDtypeStruct((T, 512), BF16), jax.ShapeDtypeStruct((T, 512), BF16), jax.ShapeDtypeStruct((8, 1024), F32),
    )
    return pl.pallas_call(
        body, name="mid", grid=(T // tm,), out_shape=out_shape,
        in_specs=[row(512), row(512), row(1024), row(1024), _full(g_fo.shape), _full(g_mo.shape), _full(g2.shape), _full(g3.shape),
                  _full(w_o.shape), _full(w_g.shape), _full(w_u.shape), _full(w_d.shape)],
        out_specs=[row(1024), row(1024), chunked, chunked, chunked, row(1024), row(1024), row(512), row(512),
                   pl.BlockSpec((8, 1024), lambda i: (0, 0))],
        compiler_params=_params(("arbitrary",)),
    )(of, om, x, tgt, g_fo, g_mo, g2, g3, w_o, w_g, w_u, w_d)


def _in_bwd(dqf, dkf, dvf, dqm, dkm, dvm, lat, x, dx2, g1, gq, gkv, bfg, ct, st, sel_t, w_in, w_q12, w_kv, seq, tm, ex=None):
    T = x.shape[0]
    nblk = T // tm
    nsb = seq // tm
    n_in, n_out = (len(ex.ins), len(ex.outs)) if ex else (0, 0)

    def body(*refs):
        (dqf_ref, dkf_ref, dvf_ref, dqm_ref, dkm_ref, dvm_ref, lat_ref, x_ref, dx2_ref, g1_ref, gq_ref, gkv_ref, b_ref,
         ct_ref, st_ref, selt_ref, win_ref, wq_ref, wkv_ref) = refs[0:19]
        dx_ref, dproj_ref, dq12_ref, dkv_ref, stat_ref = refs[19 + n_in:24 + n_in]
        carry = refs[24 + n_in + n_out]
        i = pl.program_id(0)
        if ex:
            ex_refs = (refs[19:19 + n_in], refs[24 + n_in:24 + n_in + n_out], refs[25 + n_in + n_out:])

            @pl.when(i == 0)
            def _():
                ex.start(*ex_refs)

        @pl.when(i == 0)
        def _():
            stat_ref[...] = jnp.zeros_like(stat_ref)

        @pl.when(i % nsb == 0)
        def _():
            carry[...] = jnp.zeros_like(carry)

        lane = lax.broadcasted_iota(jnp.int32, (tm, LANES), 1)
        low = lane < 64
        ctv, stv = ct_ref[...], st_ref[...]

        for j in range(PAIRS):
            e, o = 2 * LANES * j, 2 * LANES * j + LANES
            dq = jnp.where(low, dqf_ref[:, e:e + LANES], 0.0) + pltpu.roll(jnp.where(low, dqf_ref[:, o:o + LANES], 0.0), 64, 1)
            dk = jnp.where(low, dkf_ref[:, e:e + LANES], 0.0) + pltpu.roll(jnp.where(low, dkf_ref[:, o:o + LANES], 0.0), 64, 1)
            dproj_ref[:, C_FQ + LANES * j:C_FQ + LANES * (j + 1)] = (dq * FOX_SCALE).astype(BF16)
            dproj_ref[:, C_FK + LANES * j:C_FK + LANES * (j + 1)] = dk.astype(BF16)
        dproj_ref[:, C_FV:C_QL] = dvf_ref[...].astype(BF16)
        dcv = dkf_ref[...] + dqf_ref[...]
        k_hi = dcv.astype(BF16)
        k_lo = (dcv - k_hi.astype(F32)).astype(BF16)
        dc = _dot(k_hi, selt_ref[...]) + _dot(k_lo, selt_ref[...])
        rr = lax.broadcasted_iota(jnp.int32, (tm, tm), 0)
        cc = lax.broadcasted_iota(jnp.int32, (tm, tm), 1)
        triu = (cc >= rr).astype(BF16)
        a0, a1, a2 = _split3(dc)
        dlf = _dot(triu, a0) + _dot(triu, a1) + _dot(triu, a2) + carry[0:1, :]
        carry[0:1, :] = dlf[0:1, :]
        misc_a = lat_ref[:, Q_RANK + KV_RANK:Q_RANK + KV_RANK + LANES]
        z = misc_a + b_ref[...]
        dz = jnp.where(lane < HEADS, dlf * jax.nn.sigmoid(-z), 0.0)
        stat_ref[3:4, 0:LANES] += jnp.sum(dz, axis=0, keepdims=True)

        cq = (jnp.where(low, 1.0, 0.0) + ctv) * MLA_SCALE
        sq = stv * MLA_SCALE
        dkpe = jnp.zeros((tm, LANES), F32)
        for hd in range(HEADS):
            s0 = LANES * hd
            dqh = dqm_ref[:, s0:s0 + LANES]
            dq12_ref[:, s0:s0 + LANES] = (dqh * cq).astype(BF16)
            dq12_ref[:, 1024 + s0:1024 + s0 + LANES] = (dqh * sq).astype(BF16)
            dkpe = dkpe + dkm_ref[:, s0:s0 + LANES]
        dkv_ref[:, 0:1024] = dkm_ref[...].astype(BF16)
        dkv_ref[:, 1024:1536] = dvm_ref[...].astype(BF16)
        dproj_ref[:, C_MA:C_MB] = (dz + dkpe * ctv).astype(BF16)
        dproj_ref[:, C_MB:C_END] = (dkpe * stv).astype(BF16)
        dqn = _dot_nt(dq12_ref[...], wq_ref[...])
        dkvn = _dot_nt(dkv_ref[...], wkv_ref[...])
        ql = lat_ref[:, 0:Q_RANK]
        kvl = lat_ref[:, Q_RANK:Q_RANK + KV_RANK]
        rq, rkv = _rms(ql, Q_RANK), _rms(kvl, KV_RANK)
        qhat, kvhat = ql * rq, kvl * rkv
        stat_ref[1:2, 0:Q_RANK] += jnp.sum(dqn * qhat, axis=0, keepdims=True)
        stat_ref[2:3, 0:KV_RANK] += jnp.sum(dkvn * kvhat, axis=0, keepdims=True)
        dproj_ref[:, C_QL:C_KVL] = _rms_bwd(dqn, qhat, rq, gq_ref[...], Q_RANK).astype(BF16)
        dproj_ref[:, C_KVL:C_MA] = _rms_bwd(dkvn, kvhat, rkv, gkv_ref[...], KV_RANK).astype(BF16)

        dh1 = _dot_nt(dproj_ref[...], win_ref[...])
        xv = x_ref[...]
        r1 = _rms(xv, D_MODEL)
        xh = xv * r1
        stat_ref[0:1, :] += jnp.sum(dh1 * xh, axis=0, keepdims=True)
        dx_ref[...] = dx2_ref[...] + _rms_bwd(dh1, xh, r1, g1_ref[...], D_MODEL)

        if ex:
            @pl.when(i == nblk - 1)
            def _():
                ex.wait(*ex_refs)

    rev = lambda w: pl.BlockSpec((tm, w), lambda i: (nblk - 1 - i, 0))
    out_shape = (
        jax.ShapeDtypeStruct((T, 1024), F32), jax.ShapeDtypeStruct((T, C_END), BF16), jax.ShapeDtypeStruct((T, 2048), BF16),
        jax.ShapeDtypeStruct((T, 1536), BF16), jax.ShapeDtypeStruct((8, 1024), F32),
    ) + tuple(ex.outs if ex else ())
    res = pl.pallas_call(
        body, name="in_bwd", grid=(nblk,), out_shape=out_shape,
        in_specs=[rev(1024), rev(1024), rev(512), rev(1024), rev(1024), rev(512), rev(512), rev(1024), rev(1024),
                  _full(g1.shape), _full(gq.shape), _full(gkv.shape), _full(bfg.shape), rev(LANES), rev(LANES), _full(sel_t.shape),
                  _full(w_in.shape), _full(w_q12.shape), _full(w_kv.shape)] + [ANY] * n_in,
        out_specs=[rev(1024), rev(C_END), rev(2048), rev(1536), pl.BlockSpec((8, 1024), lambda i: (0, 0))] + [ANY] * n_out,
        scratch_shapes=[pltpu.VMEM((8, LANES), F32)] + (ex.sems() if ex else []),
        compiler_params=_params(("arbitrary",)),
    )(dqf, dkf, dvf, dqm, dkm, dvm, lat, x, dx2, g1, gq, gkv, bfg, ct, st, sel_t, w_in, w_q12, w_kv, *(ex.ins if ex else ()))
    return res[0], res[1], res[2], res[3], res[4], list(res[5:])


def _wgrad(a, b, tn, tt, name, ex=None):
    T, K = a.shape
    N = b.shape[1]
    n_in, n_out = (len(ex.ins), len(ex.outs)) if ex else (0, 0)
    gn, gt = N // tn, T // tt

    def body(*refs):
        a_ref, b_ref, o_ref = refs[0], refs[1], refs[2 + n_in]
        n, t = pl.program_id(0), pl.program_id(1)
        if ex:
            ex_refs = (refs[2:2 + n_in], refs[3 + n_in:3 + n_in + n_out], refs[3 + n_in + n_out:])

            @pl.when((n == 0) & (t == 0))
            def _():
                ex.start(*ex_refs)

        @pl.when(t == 0)
        def _():
            o_ref[...] = jnp.zeros_like(o_ref)

        o_ref[...] += _dot_tn(a_ref[...].astype(BF16), b_ref[...].astype(BF16))

        if ex:
            @pl.when((n == gn - 1) & (t == gt - 1))
            def _():
                ex.wait(*ex_refs)

    res = pl.pallas_call(
        body, name=name, grid=(gn, gt), out_shape=(jax.ShapeDtypeStruct((K, N), F32),) + tuple(ex.outs if ex else ()),
        in_specs=[pl.BlockSpec((tt, K), lambda n, t: (t, 0)), pl.BlockSpec((tt, tn), lambda n, t: (t, n))] + [ANY] * n_in,
        out_specs=[pl.BlockSpec((K, tn), lambda n, t: (0, n))] + [ANY] * n_out,
        scratch_shapes=ex.sems() if ex else [], input_output_aliases=ex.aliases(2, 1) if ex else {},
        compiler_params=_params(("arbitrary", "arbitrary")),
    )(a, b, *(ex.ins if ex else ()))
    return (res[0], list(res[1:])) if ex else res[0]


def _wgrad_chunked(a, b, tt, name):
    a_chunked = a.ndim == 3
    T, K = a.shape[-2:]
    N = b.shape[-1]

    def body(a_ref, b_ref, o_ref):
        @pl.when(pl.program_id(1) == 0)
        def _():
            o_ref[...] = jnp.zeros_like(o_ref)

        o_ref[...] += _dot_tn(a_ref[...], b_ref[...])

    plain = lambda w: pl.BlockSpec((tt, w), lambda k, t: (t, 0))
    chunk = lambda w: pl.BlockSpec((None, tt, w), lambda k, t: (k, t, 0))
    return pl.pallas_call(
        body, name=name, grid=(N_CHIPS, T // tt), out_shape=jax.ShapeDtypeStruct((N_CHIPS, K, N), F32),
        in_specs=[chunk(K) if a_chunked else plain(K), plain(N) if a_chunked else chunk(N)],
        out_specs=pl.BlockSpec((None, K, N), lambda k, t: (k, 0, 0)),
        compiler_params=_params(("arbitrary", "arbitrary")),
    )(a, b)


def _adamw(w, g, m, v, name):
    R, C = w.shape
    tr = _row_tile(R)

    def body(w_ref, g_ref, m_ref, v_ref, d_ref, nm_ref, nv_ref):
        gv = g_ref[...]
        nm = ADAM_B1 * m_ref[...] + (1.0 - ADAM_B1) * gv
        nv = ADAM_B2 * v_ref[...] + (1.0 - ADAM_B2) * (gv * gv)
        m_hat = nm / (1.0 - ADAM_B1 ** ADAM_STEP)
        v_hat = nv / (1.0 - ADAM_B2 ** ADAM_STEP)
        d_ref[...] = -ADAM_LR * (m_hat / (jnp.sqrt(v_hat) + ADAM_EPS) + ADAM_WD * w_ref[...])
        nm_ref[...] = nm
        nv_ref[...] = nv

    blk = pl.BlockSpec((tr, C), lambda i: (i, 0))
    sh = jax.ShapeDtypeStruct((R, C), F32)
    return pl.pallas_call(
        body, name=name, grid=(R // tr,), out_shape=(sh, sh, sh),
        in_specs=[blk, blk, blk, blk], out_specs=[blk, blk, blk],
        compiler_params=_params(("arbitrary",)),
    )(w, g, m, v)


def _arrange(w):
    win = w["w_in"]
    dt = win.dtype
    z = lambda r, c: jnp.zeros((r, c), dt)
    zh = lambda c: jnp.zeros((Q_RANK, HEADS, c), dt)
    kr1, kr2 = win[:, 1928:1944], win[:, 1944:1960]
    misc_a = jnp.concatenate([win[:, 1536:1544], z(1024, 56), kr1, kr2, z(1024, 32)], axis=1)
    misc_b = jnp.concatenate([z(1024, 64), kr2, kr1, z(1024, 32)], axis=1)
    w_in = jnp.concatenate([win[:, 0:1536], win[:, 1544:1928], misc_a, misc_b], axis=1)
    wq = w["w_uq"].reshape(Q_RANK, HEADS, 96)
    q1 = jnp.concatenate([wq, zh(32)], axis=2).reshape(Q_RANK, 1024)
    q2 = jnp.concatenate([zh(64), wq[:, :, 80:96], wq[:, :, 64:80], zh(32)], axis=2).reshape(Q_RANK, 1024)
    wkv = w["w_ukv"].reshape(KV_RANK, HEADS, 128)
    wk = jnp.concatenate([wkv[:, :, 0:64], jnp.zeros((KV_RANK, HEADS, 64), dt)], axis=2).reshape(KV_RANK, 1024)
    wv = wkv[:, :, 64:128].reshape(KV_RANK, 512)
    return dict(w_in=w_in, w_q12=jnp.concatenate([q1, q2], axis=1), w_k=wk, w_v=wv, w_kv=jnp.concatenate([wk, wv], axis=1))


def _unarrange(g_in, g_q12, g_kv):
    kr1 = g_in[:, C_MA + 64:C_MA + 80] + g_in[:, C_MB + 80:C_MB + 96]
    kr2 = g_in[:, C_MA + 80:C_MA + 96] + g_in[:, C_MB + 64:C_MB + 80]
    w_in = jnp.concatenate([g_in[:, 0:1536], g_in[:, C_MA:C_MA + 8], g_in[:, 1536:1920], kr1, kr2], axis=1)
    g1 = g_q12[:, 0:1024].reshape(Q_RANK, HEADS, 128)
    g2 = g_q12[:, 1024:2048].reshape(Q_RANK, HEADS, 128)
    w_uq = jnp.concatenate([g1[:, :, 0:64], g1[:, :, 64:80] + g2[:, :, 80:96], g1[:, :, 80:96] + g2[:, :, 64:80]], axis=2).reshape(Q_RANK, 768)
    gk = g_kv[:, 0:1024].reshape(KV_RANK, HEADS, 128)
    gv = g_kv[:, 1024:1536].reshape(KV_RANK, HEADS, 64)
    w_ukv = jnp.concatenate([gk[:, :, 0:64], gv], axis=2).reshape(KV_RANK, 1024)
    return dict(w_in=w_in, w_uq=w_uq, w_ukv=w_ukv)


def _selectors():
    sel = np.zeros((384, 1024), np.float32)
    sel_t = np.zeros((1024, LANES), np.float32)
    for h in range(HEADS):
        for piece in range(3):
            sel[LANES * piece + h, LANES * h + 64 + piece] = 1.0
        sel_t[LANES * h + 64, h] = 1.0
    return jnp.asarray(sel, BF16), jnp.asarray(sel_t, BF16)


def _rope_tables(positions):
    inv_freq = 10000.0 ** (-jnp.arange(0, ROPE, 2, dtype=F32) / ROPE)
    ang = positions.reshape(-1).astype(F32)[:, None] * inv_freq[None, :]
    cos, sin = jnp.cos(ang), jnp.sin(ang)
    z64, z32 = jnp.zeros((ang.shape[0], 64), F32), jnp.zeros((ang.shape[0], 32), F32)
    return jnp.concatenate([z64, cos, cos, z32], axis=1), jnp.concatenate([z64, -sin, sin, z32], axis=1)


def _pack_shards(t):
    flat = jnp.concatenate([t[n].reshape(-1) for n in HEAD3])
    return jnp.pad(flat, (0, PACK_ROWS * 1024 - flat.shape[0])).reshape(PACK_ROWS, 1024)


def _unpack_full(slabs):
    flat = slabs.reshape(N_CHIPS, -1)
    out, off = {}, 0
    for n in HEAD3:
        r, c = FULL_SHAPES[n]
        cnt = r * c // N_CHIPS
        out[n] = flat[:, off:off + cnt].reshape(N_CHIPS, r, c // N_CHIPS).transpose(1, 0, 2).reshape(r, c)
        off += cnt
    return out


def _pack_full(g):
    parts = []
    for n in HEAD3:
        r, c = FULL_SHAPES[n]
        parts.append(g[n].reshape(r, N_CHIPS, c // N_CHIPS).transpose(1, 0, 2).reshape(N_CHIPS, -1))
    flat = jnp.concatenate(parts, axis=1)
    return jnp.pad(flat, ((0, 0), (0, PACK_ROWS * 1024 - flat.shape[1]))).reshape(N_CHIPS, PACK_ROWS, 1024)


def _unpack_shard(slab, like):
    flat = slab.reshape(-1)
    out, off = {}, 0
    for n in HEAD3:
        cnt = like[n].size
        out[n] = flat[off:off + cnt].reshape(like[n].shape)
        off += cnt
    return out


def _reduce_tail(gpack):
    tr = _row_tile(gpack.shape[1] // 2)
    got, = _run_exchange(_swap_exchange([gpack]), "tail_swap")
    sums = _add_half(gpack, got, tr)
    recv, = _run_exchange(_scatter_exchange([sums]), "tail_scatter")
    joined, = _run_exchange(_join_exchange([_sum_slabs(gpack, got, recv, tr)]), "tail_join")
    return joined


SMALL_ROWS = {"norm_mix_g": (0, 1024), "norm_ffn_g": (1, 1024), "final_norm_g": (2, 1024), "q_norm_g": (4, 256),
              "kv_norm_g": (5, 128), "b_fgate": (6, 8)}


def kernel(x, positions, norm_mix_g, w_in, b_fgate, q_norm_g, w_uq, kv_norm_g, w_ukv, fox_out_g, mla_out_g, w_o, norm_ffn_g, w_gate, w_up, w_down, final_norm_g, loss_target, m_norm_mix_g, m_w_in, m_b_fgate, m_q_norm_g, m_w_uq, m_kv_norm_g, m_w_ukv, m_fox_out_g, m_mla_out_g, m_w_o, m_norm_ffn_g, m_w_gate, m_w_up, m_w_down, m_final_norm_g, v_norm_mix_g, v_w_in, v_b_fgate, v_q_norm_g, v_w_uq, v_kv_norm_g, v_w_ukv, v_fox_out_g, v_mla_out_g, v_w_o, v_norm_ffn_g, v_w_gate, v_w_up, v_w_down, v_final_norm_g):
    names = ["norm_mix_g", "w_in", "b_fgate", "q_norm_g", "w_uq", "kv_norm_g", "w_ukv", "fox_out_g", "mla_out_g", "w_o",
             "norm_ffn_g", "w_gate", "w_up", "w_down", "final_norm_g"]
    wts = dict(zip(names, [norm_mix_g, w_in, b_fgate, q_norm_g, w_uq, kv_norm_g, w_ukv, fox_out_g, mla_out_g, w_o, norm_ffn_g,
                           w_gate, w_up, w_down, final_norm_g]))
    mom = dict(zip(names, [m_norm_mix_g, m_w_in, m_b_fgate, m_q_norm_g, m_w_uq, m_kv_norm_g, m_w_ukv, m_fox_out_g, m_mla_out_g,
                           m_w_o, m_norm_ffn_g, m_w_gate, m_w_up, m_w_down, m_final_norm_g]))
    var = dict(zip(names, [v_norm_mix_g, v_w_in, v_b_fgate, v_q_norm_g, v_w_uq, v_kv_norm_g, v_w_ukv, v_fox_out_g, v_mla_out_g,
                           v_w_o, v_norm_ffn_g, v_w_gate, v_w_up, v_w_down, v_final_norm_g]))
    shard = {n: wts[n][0] for n in HEAD3 + FFN4}
    nb, seq, _ = x.shape
    T = nb * seq
    tm, tq = min(ROW_TILE, seq), min(ATTN_TILE, seq)
    tt = min(WGRAD_TILE, T)
    xf = x.reshape(T, D_MODEL)
    tgt = loss_target.reshape(T, D_MODEL)

    pack = _pack_shards(shard).astype(BF16)
    head, = _run_exchange(_gather_exchange([pack], own=False), "gather_head")
    head = lax.dynamic_update_slice(head, pack[None], (2 * lax.axis_index("x") + lax.axis_index("y"), 0, 0))
    a = _arrange(_unpack_full(head))
    sel, sel_t = _selectors()
    ct, st = _rope_tables(positions)
    bfg = jnp.concatenate([b_fgate, jnp.zeros((1, LANES - HEADS), F32)], axis=1)
    g1, gq, gkv = norm_mix_g, q_norm_g, kv_norm_g

    h1, qf, kf, vf, qm, km, vm, lat, qn, kvn = _in_proj(xf, g1, a["w_in"], a["w_q12"], a["w_k"], a["w_v"], gq, gkv, bfg, ct, st, sel, seq, tm)
    of, lse_f, gathered = _attn_fwd(qf, kf, vf, nb, seq, tq, "fox_fwd", _gather_exchange([shard[n].astype(BF16) for n in FFN4]))
    om, lse_m, _ = _attn_fwd(qm, km, vm, nb, seq, tq, "mla_fwd")
    wo4, wg4, wu4, wd4 = gathered
    a_cat, h2, hid, dg, du, dx3, dx2, dof, dom, st_mid = _mid(
        of, om, xf, tgt, fox_out_g, mla_out_g, norm_ffn_g, final_norm_g.reshape(1, D_MODEL),
        wo4.reshape(D_MODEL, D_MODEL), wg4, wu4, wd4, tm)

    big = [_wgrad(a_cat, dx2, 512, tt, "wgrad_o").reshape(N_CHIPS, D_MODEL // N_CHIPS, D_MODEL),
           _wgrad_chunked(h2, dg, tt, "wgrad_gate"), _wgrad_chunked(h2, du, tt, "wgrad_up"), _wgrad_chunked(hid, dx3, tt, "wgrad_down")]
    dqf, dkf, dvf, got = _attn_bwd(qf, kf, vf, of, dof, lse_f, nb, seq, tq, "fox_bwd", True, _swap_exchange(big))
    sums = [_add_half(g, s, _row_tile(s.shape[1])) for g, s in zip(big, got)]
    dqm, dkm, dvm, recv = _attn_bwd(qm, km, vm, om, dom, lse_m, nb, seq, tq, "mla_bwd", False, _scatter_exchange(sums))
    halves = [_sum_slabs(g, s, r, _row_tile(s.shape[1])) for g, s, r in zip(big, got, recv)]
    dx, dproj, dq12, dkv, st_in, _ = _in_bwd(dqf, dkf, dvf, dqm, dkm, dvm, lat, xf, dx2, g1, gq, gkv, bfg, ct, st, sel_t,
                                             a["w_in"], a["w_q12"], a["w_kv"], seq, tm)
    g_in, joined = _wgrad(h1, dproj, C_END, tt, "wgrad_in", _join_exchange(halves))
    gshard = dict(zip(FFN4, joined))

    loss_row = jnp.concatenate([jnp.sum(st_mid[3:4, :], axis=1, keepdims=True), jnp.zeros((1, D_MODEL - 1), F32)], axis=1)
    stats = jnp.concatenate([st_in[0:1], st_mid[1:2], st_mid[0:1], st_mid[2:3], st_in[1:2], st_in[2:3], st_in[3:4], loss_row], axis=0)
    stats = _allreduce_small(stats)
    tail = _unarrange(g_in, _wgrad(qn, dq12, 2048, tt, "wgrad_uq"), _wgrad(kvn, dkv, 1536, tt, "wgrad_ukv"))
    gshard.update(_unpack_shard(_reduce_tail(_pack_full(tail)), shard))

    grads, delta, new_m, new_v = {}, {}, {}, {}
    for n in HEAD3 + FFN4:
        grads[n] = gshard[n][None]
        d, nm, nv = _adamw(shard[n], gshard[n], mom[n][0], var[n][0], "adamw_" + n)
        delta[n], new_m[n], new_v[n] = d[None], nm[None], nv[None]
    sm_g = {n: stats[row:row + 1, 0:width] for n, (row, width) in SMALL_ROWS.items()}
    sm_g["fox_out_g"] = stats[3:4, 0:512]
    sm_g["mla_out_g"] = stats[3:4, 512:1024]
    pad = lambda t: jnp.pad(t.reshape(1, -1), ((0, 0), (0, 1024 - t.size)))
    stack = lambda d: jnp.concatenate([pad(d[n]) for n in SMALL], axis=0)
    sd, sm, sv = _adamw(stack(wts), stack(sm_g), stack(mom), stack(var), "adamw_small")
    for i, n in enumerate(SMALL):
        shp = wts[n].shape
        grads[n] = sm_g[n].reshape(shp)
        delta[n] = sd[i, 0:wts[n].size].reshape(shp)
        new_m[n] = sm[i, 0:wts[n].size].reshape(shp)
        new_v[n] = sv[i, 0:wts[n].size].reshape(shp)
    loss = stats[7, 0]
    return (loss, dx.reshape(x.shape), *[grads[n] for n in names], *[delta[n] for n in names],
            *[new_m[n] for n in names], *[new_v[n] for n in names])
```

```python
import functools

import numpy as np
import jax
import jax.numpy as jnp
from jax import lax
from jax.experimental import pallas as pl
from jax.experimental.pallas import tpu as pltpu

F32 = jnp.float32
BF16 = jnp.bfloat16
MESH = pl.DeviceIdType.MESH

EPS = 1e-6
D_MODEL = 1024
HEADS = 8
PAIRS = HEADS // 2
FOX_W = 512
Q_RANK = 256
KV_RANK = 128
ROPE = 32
D_FF = 2816
N_CHIPS = 4
FOX_SCALE = 64 ** -0.5
MLA_SCALE = 96 ** -0.5
LANES = 128
NEG = -1e30

ADAM_LR, ADAM_B1, ADAM_B2, ADAM_EPS, ADAM_WD, ADAM_STEP = 0.001, 0.9, 0.999, 1e-08, 0.01, 10

C_FQ, C_FK, C_FV, C_QL, C_KVL, C_MA, C_MB, C_END = 0, 512, 1024, 1536, 1792, 1920, 2048, 2176

VMEM_LIMIT = 60 * 1024 * 1024
ROW_TILE = 256
ATTN_TILE = 512
WGRAD_TILE = 2048

HEAD3 = ("w_in", "w_uq", "w_ukv")
FFN4 = ("w_o", "w_gate", "w_up", "w_down")
TRANSPOSED = ("w_in", "w_uq", "w_gate", "w_up")
SMALL = ("norm_mix_g", "b_fgate", "q_norm_g", "kv_norm_g", "fox_out_g", "mla_out_g", "norm_ffn_g", "final_norm_g")


def _params(sem=None):
    return pltpu.CompilerParams(dimension_semantics=sem, vmem_limit_bytes=VMEM_LIMIT)


def _full(shape):
    n = len(shape)
    return pl.BlockSpec(shape, lambda *_: (0,) * n, pipeline_mode=pl.Buffered(1))


def _dot(a, b):
    return jnp.dot(a, b, preferred_element_type=F32)


def _dot_nt(a, b):
    return lax.dot_general(a, b, (((1,), (1,)), ((), ())), preferred_element_type=F32)


def _dot_tn(a, b):
    return lax.dot_general(a, b, (((0,), (0,)), ((), ())), preferred_element_type=F32)


def _split3(v):
    hi = v.astype(BF16)
    r1 = v - hi.astype(F32)
    mid = r1.astype(BF16)
    lo = (r1 - mid.astype(F32)).astype(BF16)
    return hi, mid, lo


def _rms(v, width):
    return lax.rsqrt(jnp.sum(v * v, axis=1, keepdims=True) * (1.0 / width) + EPS)


def _rms_bwd(dy, xhat, r, g, width):
    u = dy * g
    return r * (u - xhat * (jnp.sum(u * xhat, axis=1, keepdims=True) * (1.0 / width)))


ANY = pl.BlockSpec(memory_space=pl.ANY)


def _place():
    return lax.axis_index("x"), lax.axis_index("y"), lax.axis_index("c")


def _other_chips(x, y):
    return [(1 - x, y), (x, 1 - y), (1 - x, 1 - y)]


def _remote(src, dst, send, recv, j, dev):
    return pltpu.make_async_remote_copy(src_ref=src, dst_ref=dst, send_sem=send.at[j], recv_sem=recv.at[j], device_id=dev, device_id_type=MESH)


class _Exchange:
    def __init__(self, ins, outs, n_remote, n_local, build, in_place=False):
        self.ins, self.outs, self.n_remote, self.n_local, self.build = list(ins), list(outs), n_remote, max(n_local, 1), build
        self.in_place = in_place

    def aliases(self, first_in, first_out):
        return {first_in + i: first_out + i for i in range(len(self.ins))} if self.in_place else {}

    def sems(self):
        return [pltpu.SemaphoreType.DMA((self.n_remote,)), pltpu.SemaphoreType.DMA((self.n_remote,)), pltpu.SemaphoreType.DMA((self.n_local,))]

    def start(self, in_refs, out_refs, sems):
        for cp in self.build(in_refs, out_refs, *sems)[0]:
            cp.start()

    def wait(self, in_refs, out_refs, sems):
        for w in self.build(in_refs, out_refs, *sems)[1]:
            w()


def _gather_exchange(shards, own=True):
    def build(ins, outs, send, recv, lsem):
        x, y, c = _place()
        starts, waits = [], []
        for i, (s, o) in enumerate(zip(ins, outs)):
            if own:
                mine = pltpu.make_async_copy(s, o.at[2 * x + y], lsem.at[i])
                starts.append(mine)
                waits.append(mine.wait)
            for j, (cx, cy) in enumerate(_other_chips(x, y)):
                out = _remote(s, o.at[2 * x + y], send, recv, 3 * i + j, (cx, cy, c))
                starts.append(out)
                waits.append(_remote(s, o.at[2 * cx + cy], send, recv, 3 * i + j, (cx, cy, c)).wait_recv)
                waits.append(out.wait_send)
        return starts, waits

    outs = [jax.ShapeDtypeStruct((N_CHIPS,) + s.shape, s.dtype) for s in shards]
    return _Exchange(shards, outs, 3 * len(shards), len(shards), build)


def _swap_exchange(grads):
    def build(ins, outs, send, recv, lsem):
        x, y, c = _place()
        cps = []
        for i, (g, o) in enumerate(zip(ins, outs)):
            hc = g.shape[2] // 2
            cps.append(_remote(g.at[:, :, pl.ds((1 - c) * hc, hc)], o, send, recv, i, (x, y, 1 - c)))
        return cps, [cp.wait for cp in cps]

    outs = [jax.ShapeDtypeStruct((g.shape[0], g.shape[1], g.shape[2] // 2), g.dtype) for g in grads]
    return _Exchange(grads, outs, len(grads), 0, build)


def _scatter_exchange(sums):
    def build(ins, outs, send, recv, lsem):
        x, y, c = _place()
        cps = []
        for i, (s, o) in enumerate(zip(ins, outs)):
            for j, (cx, cy) in enumerate(_other_chips(x, y)):
                cps.append(_remote(s.at[2 * cx + cy], o.at[j], send, recv, 3 * i + j, (cx, cy, c)))
        return cps, [cp.wait for cp in cps]

    outs = [jax.ShapeDtypeStruct((3,) + s.shape[1:], s.dtype) for s in sums]
    return _Exchange(sums, outs, 3 * len(sums), 0, build)


def _join_exchange(bufs):
    def build(ins, outs, send, recv, lsem):
        x, y, c = _place()
        starts, waits = [], []
        for i, (t, o) in enumerate(zip(ins, outs)):
            hc = t.shape[1] // 2
            out = _remote(t.at[:, pl.ds(c * hc, hc)], o.at[:, pl.ds(c * hc, hc)], send, recv, i, (x, y, 1 - c))
            starts.append(out)
            waits += [_remote(t.at[:, pl.ds(c * hc, hc)], o.at[:, pl.ds((1 - c) * hc, hc)], send, recv, i, (x, y, 1 - c)).wait_recv,
                      out.wait_send]
        return starts, waits

    outs = [jax.ShapeDtypeStruct(t.shape, t.dtype) for t in bufs]
    return _Exchange(bufs, outs, len(bufs), 0, build, in_place=True)


def _run_exchange(ex, name):
    n_in, n_out = len(ex.ins), len(ex.outs)

    def body(*refs):
        ins, outs, sems = refs[:n_in], refs[n_in:n_in + n_out], refs[n_in + n_out:]
        ex.start(ins, outs, sems)
        ex.wait(ins, outs, sems)

    return pl.pallas_call(
        body, name=name, out_shape=tuple(ex.outs), in_specs=[ANY] * n_in, out_specs=tuple([ANY] * n_out),
        scratch_shapes=ex.sems(), input_output_aliases=ex.aliases(0, 0),
        compiler_params=pltpu.CompilerParams(has_side_effects=True),
    )(*ex.ins)


def _allreduce_small(v):
    def body(v_ref, o_ref, buf, send, recv):
        x, y, c = _place()
        me = 4 * x + 2 * y + c
        buf[me] = v_ref[...]
        out = []
        for j in range(7):
            fx, fy, fc = (j + 1) >> 2 & 1, (j + 1) >> 1 & 1, (j + 1) & 1
            out.append(_remote(v_ref, buf.at[me], send, recv, j, (x ^ fx, y ^ fy, c ^ fc)))
        for cp in out:
            cp.start()
        for j in range(7):
            fx, fy, fc = (j + 1) >> 2 & 1, (j + 1) >> 1 & 1, (j + 1) & 1
            src = 4 * (x ^ fx) + 2 * (y ^ fy) + (c ^ fc)
            _remote(v_ref, buf.at[src], send, recv, j, (x ^ fx, y ^ fy, c ^ fc)).wait_recv()
        for cp in out:
            cp.wait_send()
        acc = buf[0]
        for d in range(1, 8):
            acc = acc + buf[d]
        o_ref[...] = acc

    vm = pl.BlockSpec(memory_space=pltpu.VMEM)
    return pl.pallas_call(
        body, name="allreduce_small", out_shape=jax.ShapeDtypeStruct(v.shape, v.dtype),
        in_specs=[vm], out_specs=vm,
        scratch_shapes=[pltpu.VMEM((8,) + v.shape, v.dtype), pltpu.SemaphoreType.DMA((7,)), pltpu.SemaphoreType.DMA((7,))],
        compiler_params=pltpu.CompilerParams(has_side_effects=True),
    )(v)


def _add_half(g, got):
    n, R, C = g.shape
    hc = C // 2

    def body(c_ref, g_ref, r_ref, o_ref):
        o_ref[...] = (g_ref[...] + r_ref[...]).astype(BF16)

    c = lax.axis_index("c")
    return pl.pallas_call(
        body, name="add_half",
        grid_spec=pltpu.PrefetchScalarGridSpec(
            num_scalar_prefetch=1, grid=(n,),
            in_specs=[pl.BlockSpec((1, R, hc), lambda k, c_ref: (k, 0, c_ref[0])),
                      pl.BlockSpec((1, R, hc), lambda k, c_ref: (k, 0, 0))],
            out_specs=pl.BlockSpec((1, R, hc), lambda k, c_ref: (k, 0, 0))),
        out_shape=jax.ShapeDtypeStruct((n, R, hc), BF16),
        compiler_params=_params(("arbitrary",)),
    )(jnp.reshape(c, (1,)).astype(jnp.int32), g, got)


def _sum_slabs(g, got, recv):
    n, R, C = g.shape
    hc = C // 2

    def body(kc_ref, g_ref, s_ref, r_ref, o_ref):
        o_ref[...] = (((g_ref[0] + s_ref[0]) + r_ref[0].astype(F32)) + r_ref[1].astype(F32)) + r_ref[2].astype(F32)

    kc = jnp.stack([2 * lax.axis_index("x") + lax.axis_index("y"), lax.axis_index("c")]).astype(jnp.int32)
    return pl.pallas_call(
        body, name="sum_slabs",
        grid_spec=pltpu.PrefetchScalarGridSpec(
            num_scalar_prefetch=1, grid=(1,),
            in_specs=[pl.BlockSpec((1, R, hc), lambda i, kc_ref: (kc_ref[0], 0, kc_ref[1])),
                      pl.BlockSpec((1, R, hc), lambda i, kc_ref: (kc_ref[0], 0, 0)),
                      pl.BlockSpec((3, R, hc), lambda i, kc_ref: (0, 0, 0))],
            out_specs=pl.BlockSpec((R, hc), lambda i, kc_ref: (0, kc_ref[1]))),
        out_shape=jax.ShapeDtypeStruct((R, C), F32),
        compiler_params=_params(("arbitrary",)),
    )(kc, g, got, recv)


def _row_tile(rows):
    for cand in (256, 184, 176, 144, 128, 64, 32, 16, 8):
        if rows % cand == 0:
            return cand
    return rows


def _in_proj(x, g1, w_in, w_q12, w_k, w_v, gq, gkv, bfg, ct, st, sel, seq, tm):
    T = x.shape[0]
    nsb = seq // tm

    def body(x_ref, g1_ref, win_ref, wq_ref, wk_ref, wv_ref, gq_ref, gkv_ref, b_ref, ct_ref, st_ref, sel_ref,
             h1_ref, qf_ref, kf_ref, vf_ref, qm_ref, km_ref, vm_ref, lat_ref, qn_ref, kvn_ref, carry):
        i = pl.program_id(0)

        @pl.when(i % nsb == 0)
        def _():
            carry[...] = jnp.zeros_like(carry)

        xv = x_ref[...]
        h = (xv * _rms(xv, D_MODEL) * g1_ref[...]).astype(BF16)
        h1_ref[...] = h
        proj = _dot_nt(h, win_ref[...])
        lane = lax.broadcasted_iota(jnp.int32, (tm, LANES), 1)
        low = lane < 64
        misc_a = proj[:, C_MA:C_MB]
        misc_b = proj[:, C_MB:C_END]

        z = misc_a + b_ref[...]
        lf = jnp.where(lane < HEADS, jnp.minimum(z, 0.0) - jnp.log1p(jnp.exp(-jnp.abs(z))), 0.0)
        rr = lax.broadcasted_iota(jnp.int32, (tm, tm), 0)
        cc = lax.broadcasted_iota(jnp.int32, (tm, tm), 1)
        tri = (rr >= cc).astype(BF16)
        a0, a1, a2 = _split3(lf)
        c = _dot(tri, a0) + _dot(tri, a1) + _dot(tri, a2) + carry[0:1, :]
        carry[0:1, :] = c[tm - 1:tm, :]
        c0, c1, c2 = _split3(c)
        cpl = _dot(jnp.concatenate([c0, c1, c2], axis=1), sel_ref[...])
        qpad = jnp.where((lane >= 64) & (lane < 67), -1.0, 0.0)
        for j in range(PAIRS):
            qc = proj[:, C_FQ + LANES * j:C_FQ + LANES * (j + 1)] * FOX_SCALE
            kc = proj[:, C_FK + LANES * j:C_FK + LANES * (j + 1)]
            e, o = 2 * LANES * j, 2 * LANES * j + LANES
            qf_ref[:, e:e + LANES] = jnp.where(low, qc, qpad).astype(BF16)
            qf_ref[:, o:o + LANES] = jnp.where(low, pltpu.roll(qc, 64, 1), qpad).astype(BF16)
            kf_ref[:, e:e + LANES] = jnp.where(low, kc, cpl[:, e:e + LANES]).astype(BF16)
            kf_ref[:, o:o + LANES] = jnp.where(low, pltpu.roll(kc, 64, 1), cpl[:, o:o + LANES]).astype(BF16)
        vf_ref[...] = proj[:, C_FV:C_QL].astype(BF16)

        ql = proj[:, C_QL:C_KVL]
        kvl = proj[:, C_KVL:C_MA]
        qn = (ql * _rms(ql, Q_RANK) * gq_ref[...]).astype(BF16)
        kvn = (kvl * _rms(kvl, KV_RANK) * gkv_ref[...]).astype(BF16)
        lat_ref[...] = proj[:, C_QL:C_MB]
        qn_ref[...] = qn
        kvn_ref[...] = kvn
        q12 = _dot_nt(qn, wq_ref[...])
        kn = _dot(kvn, wk_ref[...])
        ctv = ct_ref[...]
        stv = st_ref[...]
        cq = (jnp.where(low, 1.0, 0.0) + ctv) * MLA_SCALE
        sq = stv * MLA_SCALE
        kpe = misc_a * ctv + misc_b * stv
        for hd in range(HEADS):
            s0 = LANES * hd
            qm_ref[:, s0:s0 + LANES] = (q12[:, s0:s0 + LANES] * cq + q12[:, 1024 + s0:1024 + s0 + LANES] * sq).astype(BF16)
            km_ref[:, s0:s0 + LANES] = (kn[:, s0:s0 + LANES] + kpe).astype(BF16)
        vm_ref[...] = _dot(kvn, wv_ref[...]).astype(BF16)

    row = lambda w: pl.BlockSpec((tm, w), lambda i: (i, 0))
    out_shape = (
        jax.ShapeDtypeStruct((T, D_MODEL), BF16),
        jax.ShapeDtypeStruct((T, 1024), BF16), jax.ShapeDtypeStruct((T, 1024), BF16), jax.ShapeDtypeStruct((T, 512), BF16),
        jax.ShapeDtypeStruct((T, 1024), BF16), jax.ShapeDtypeStruct((T, 1024), BF16), jax.ShapeDtypeStruct((T, 512), BF16),
        jax.ShapeDtypeStruct((T, 512), F32),
        jax.ShapeDtypeStruct((T, Q_RANK), BF16), jax.ShapeDtypeStruct((T, KV_RANK), BF16),
    )
    return pl.pallas_call(
        body, name="in_proj", grid=(T // tm,), out_shape=out_shape,
        in_specs=[row(D_MODEL), _full(g1.shape), _full(w_in.shape), _full(w_q12.shape), _full(w_k.shape), _full(w_v.shape),
                  _full(gq.shape), _full(gkv.shape), _full(bfg.shape), row(LANES), row(LANES), _full(sel.shape)],
        out_specs=[row(D_MODEL), row(1024), row(1024), row(512), row(1024), row(1024), row(512), row(512), row(Q_RANK), row(KV_RANK)],
        scratch_shapes=[pltpu.VMEM((8, LANES), F32)],
        compiler_params=_params(("arbitrary",)),
    )(x, g1, w_in, w_q12, w_k, w_v, gq, gkv, bfg, ct, st, sel)


def _attn_fwd(q, k, v, nb, seq, tq, name, ex=None):
    T = q.shape[0]
    nq = seq // tq
    n_in, n_out = (len(ex.ins), len(ex.outs)) if ex else (0, 0)

    def body(*refs):
        q_ref, k_ref, v_ref = refs[0:3]
        o_ref, lse_ref = refs[3 + n_in:5 + n_in]
        b, pr, qi = pl.program_id(0), pl.program_id(1), pl.program_id(2)
        if ex:
            ex_refs = (refs[3:3 + n_in], refs[5 + n_in:5 + n_in + n_out], refs[5 + n_in + n_out:])

            @pl.when((b == 0) & (pr == 0) & (qi == 0))
            def _():
                ex.start(*ex_refs)

        lane = lax.broadcasted_iota(jnp.int32, (tq, LANES), 1)
        rr = lax.broadcasted_iota(jnp.int32, (tq, tq), 0)
        cc = lax.broadcasted_iota(jnp.int32, (tq, tq), 1)
        def step(kj, carry, masked):
            rows = pl.ds(pl.multiple_of(kj * tq, tq), tq)
            vv = v_ref[rows, :]
            new = []
            for hh in range(2):
                m, l, acc = carry[hh]
                s = _dot_nt(q_ref[:, LANES * hh:LANES * (hh + 1)], k_ref[rows, LANES * hh:LANES * (hh + 1)])
                if masked:
                    s = jnp.where(cc <= rr, s, NEG)
                m_new = jnp.maximum(m, jnp.max(s, axis=1, keepdims=True))
                alpha = jnp.exp(m - m_new)
                p = jnp.exp(s - m_new)
                l = alpha * l + jnp.sum(p, axis=1, keepdims=True)
                acc = alpha * acc + _dot(p.astype(BF16), vv)
                new.append((m_new, l, acc))
            return tuple(new)

        one = (jnp.full((tq, 1), NEG, F32), jnp.zeros((tq, 1), F32), jnp.zeros((tq, LANES), F32))
        carry = lax.fori_loop(0, qi, functools.partial(step, masked=False), (one, one))
        (m0, l0, acc0), (m1, l1, acc1) = step(qi, carry, True)
        lse_ref[:, 0:LANES] = jnp.broadcast_to(m0 + jnp.log(l0), (tq, LANES))
        lse_ref[:, LANES:2 * LANES] = jnp.broadcast_to(m1 + jnp.log(l1), (tq, LANES))
        o_ref[...] = jnp.where(lane < 64, acc0 / l0, acc1 / l1)

        if ex:
            @pl.when((b == nb - 1) & (pr == PAIRS - 1) & (qi == nq - 1))
            def _():
                ex.wait(*ex_refs)

    res = pl.pallas_call(
        body, name=name, grid=(nb, PAIRS, nq),
        out_shape=(jax.ShapeDtypeStruct((T, 512), F32), jax.ShapeDtypeStruct((T, 1024), F32)) + tuple(ex.outs if ex else ()),
        in_specs=[pl.BlockSpec((tq, 2 * LANES), lambda b, p, i: (b * nq + i, p)),
                  pl.BlockSpec((seq, 2 * LANES), lambda b, p, i: (b, p)),
                  pl.BlockSpec((seq, LANES), lambda b, p, i: (b, p))] + [ANY] * n_in,
        out_specs=[pl.BlockSpec((tq, LANES), lambda b, p, i: (b * nq + i, p)),
                   pl.BlockSpec((tq, 2 * LANES), lambda b, p, i: (b * nq + i, p))] + [ANY] * n_out,
        scratch_shapes=ex.sems() if ex else [],
        compiler_params=_params(("arbitrary", "arbitrary", "arbitrary")),
    )(q, k, v, *(ex.ins if ex else ()))
    return res[0], res[1], list(res[2:])


def _attn_bwd(q, k, v, o, do, lse, nb, seq, tq, name, key_bias, ex=None):
    T = q.shape[0]
    nq = seq // tq
    n_in, n_out = (len(ex.ins), len(ex.outs)) if ex else (0, 0)

    def body(*refs):
        q_ref, k_ref, v_ref, o_ref, do_ref, lse_ref = refs[0:6]
        dq_ref, dk_ref, dv_ref = refs[6 + n_in:9 + n_in]
        dsc, rsum = refs[9 + n_in + n_out:11 + n_in + n_out]
        b, pr, kj = pl.program_id(0), pl.program_id(1), pl.program_id(2)
        if ex:
            ex_refs = (refs[6:6 + n_in], refs[9 + n_in:9 + n_in + n_out], refs[11 + n_in + n_out:])

            @pl.when((b == 0) & (pr == 0) & (kj == 0))
            def _():
                ex.start(*ex_refs)

        lane_s = lax.broadcasted_iota(jnp.int32, (seq, LANES), 1)
        lane = lax.broadcasted_iota(jnp.int32, (tq, LANES), 1)
        rr = lax.broadcasted_iota(jnp.int32, (tq, tq), 0)
        cc = lax.broadcasted_iota(jnp.int32, (tq, tq), 1)

        @pl.when(kj == 0)
        def _():
            dq_ref[...] = jnp.zeros_like(dq_ref)
            prod = do_ref[...].astype(F32) * o_ref[...]
            d0 = jnp.sum(jnp.where(lane_s < 64, prod, 0.0), axis=1, keepdims=True)
            d1 = jnp.sum(jnp.where(lane_s < 64, 0.0, prod), axis=1, keepdims=True)
            dsc[0] = jnp.broadcast_to(d0, (seq, LANES))
            dsc[1] = jnp.broadcast_to(d1, (seq, LANES))
            if key_bias:
                rsum[...] = jnp.zeros_like(rsum)

        vv = v_ref[...]

        def step(qi, carry, masked):
            dkt, dvt, cols = carry
            rows = pl.ds(pl.multiple_of(qi * tq, tq), tq)
            dov = do_ref[rows, :]
            new_dkt, new_cols = [], []
            for hh in range(2):
                qv = q_ref[rows, LANES * hh:LANES * (hh + 1)]
                kv = k_ref[:, LANES * hh:LANES * (hh + 1)]
                dom = jnp.where((lane < 64) if hh == 0 else (lane >= 64), dov, jnp.zeros((), BF16))
                s = _dot_nt(qv, kv)
                if masked:
                    s = jnp.where(cc <= rr, s, NEG)
                p = jnp.exp(s - lse_ref[rows, LANES * hh:LANES * hh + 1])
                dp = _dot_nt(dom, vv)
                ds32 = p * (dp - dsc[hh, rows, 0:1])
                col = cols[hh]
                if key_bias:
                    col = col + jnp.sum(ds32, axis=0, keepdims=True)
                    rsum[hh, rows, :] += jnp.broadcast_to(jnp.sum(ds32, axis=1, keepdims=True), (tq, LANES))
                ds = ds32.astype(BF16)
                dvt = dvt + _dot_tn(dom, p.astype(BF16))
                new_dkt.append(dkt[hh] + _dot_tn(qv, ds))
                new_cols.append(col)
                dq_ref[rows, LANES * hh:LANES * (hh + 1)] += _dot(ds, kv)
            return tuple(new_dkt), dvt, tuple(new_cols)

        zt = jnp.zeros((LANES, tq), F32)
        zc = jnp.zeros((1, tq), F32)
        carry = step(kj, ((zt, zt), zt, (zc, zc)), True)
        dkt, dvt, cols = lax.fori_loop(kj + 1, nq, functools.partial(step, masked=False), carry)
        row_t = lax.broadcasted_iota(jnp.int32, (LANES, tq), 0)
        for hh in range(2):
            dk_h = jnp.where(row_t == 64, -cols[hh], dkt[hh]) if key_bias else dkt[hh]
            dk_ref[:, LANES * hh:LANES * (hh + 1)] = dk_h.T
        dv_ref[...] = dvt.T

        if key_bias:
            @pl.when(kj == nq - 1)
            def _():
                for hh in range(2):
                    blk = dq_ref[:, LANES * hh:LANES * (hh + 1)]
                    dq_ref[:, LANES * hh:LANES * (hh + 1)] = jnp.where(lane_s == 64, rsum[hh], blk)

        if ex:
            @pl.when((b == nb - 1) & (pr == PAIRS - 1) & (kj == nq - 1))
            def _():
                ex.wait(*ex_refs)

    per_seq = lambda w: pl.BlockSpec((seq, w), lambda b, p, j: (b, p))
    per_blk = lambda w: pl.BlockSpec((tq, w), lambda b, p, j: (b * nq + j, p))
    res = pl.pallas_call(
        body, name=name, grid=(nb, PAIRS, nq),
        out_shape=(jax.ShapeDtypeStruct((T, 1024), F32), jax.ShapeDtypeStruct((T, 1024), F32), jax.ShapeDtypeStruct((T, 512), F32))
        + tuple(ex.outs if ex else ()),
        in_specs=[per_seq(2 * LANES), per_blk(2 * LANES), per_blk(LANES), per_seq(LANES), per_seq(LANES), per_seq(2 * LANES)] + [ANY] * n_in,
        out_specs=[per_seq(2 * LANES), per_blk(2 * LANES), per_blk(LANES)] + [ANY] * n_out,
        scratch_shapes=[pltpu.VMEM((2, seq, LANES), F32), pltpu.VMEM((2, seq, LANES) if key_bias else (2, 8, LANES), F32)]
        + (ex.sems() if ex else []),
        compiler_params=_params(("arbitrary", "arbitrary", "arbitrary")),
    )(q, k, v, o, do, lse, *(ex.ins if ex else ()))
    return res[0], res[1], res[2], list(res[3:])


def _mid(of, om, x, tgt, g_fo, g_mo, g2, g3, w_o, w_g, w_u, w_d, tm):
    T = x.shape[0]

    def body(of_ref, om_ref, x_ref, t_ref, gfo_ref, gmo_ref, g2_ref, g3_ref, wo_ref, wg_ref, wu_ref, wd_ref,
             a_ref, h2_ref, hid_ref, dg_ref, du_ref, dx3_ref, dx2_ref, dof_ref, dom_ref, st_ref):
        i = pl.program_id(0)

        @pl.when(i == 0)
        def _():
            st_ref[...] = jnp.zeros_like(st_ref)

        ofv, omv = of_ref[...], om_ref[...]
        rf, rm = _rms(ofv, FOX_W), _rms(omv, FOX_W)
        fhat, mhat = ofv * rf, omv * rm
        a = jnp.concatenate([fhat * gfo_ref[...], mhat * gmo_ref[...]], axis=1).astype(BF16)
        a_ref[...] = a
        x2 = x_ref[...] + _dot(a, wo_ref[...])
        r2 = _rms(x2, D_MODEL)
        xh2 = x2 * r2
        h2 = (xh2 * g2_ref[...]).astype(BF16)
        h2_ref[...] = h2
        gt = _dot_nt(h2, wg_ref[...])
        up = _dot_nt(h2, wu_ref[...])
        sg = jax.nn.sigmoid(gt)
        sl = gt * sg
        hid = (sl * up).astype(BF16)
        hid_ref[...] = hid
        x3 = x2 + _dot(hid, wd_ref[...])
        r3 = _rms(x3, D_MODEL)
        xh3 = x3 * r3
        diff = xh3 * g3_ref[...] - t_ref[...]
        dy = diff * (1.0 / D_MODEL)
        st_ref[3:4, :] += jnp.sum(diff * diff, axis=0, keepdims=True) * (0.5 / D_MODEL)
        st_ref[0:1, :] += jnp.sum(dy * xh3, axis=0, keepdims=True)
        dx3 = _rms_bwd(dy, xh3, r3, g3_ref[...], D_MODEL)
        dx3b = dx3.astype(BF16)
        dx3_ref[...] = dx3b
        dhid = _dot_nt(dx3b, wd_ref[...])
        dg = (dhid * up * (sg * (1.0 + gt * (1.0 - sg)))).astype(BF16)
        du = (dhid * sl).astype(BF16)
        dg_ref[...] = dg
        du_ref[...] = du
        dh2 = _dot(dg, wg_ref[...]) + _dot(du, wu_ref[...])
        st_ref[1:2, :] += jnp.sum(dh2 * xh2, axis=0, keepdims=True)
        dx2 = dx3 + _rms_bwd(dh2, xh2, r2, g2_ref[...], D_MODEL)
        dx2_ref[...] = dx2
        da = _dot_nt(dx2.astype(BF16), wo_ref[...])
        daf, dam = da[:, 0:FOX_W], da[:, FOX_W:2 * FOX_W]
        st_ref[2:3, 0:FOX_W] += jnp.sum(daf * fhat, axis=0, keepdims=True)
        st_ref[2:3, FOX_W:2 * FOX_W] += jnp.sum(dam * mhat, axis=0, keepdims=True)
        dof_ref[...] = _rms_bwd(daf, fhat, rf, gfo_ref[...], FOX_W).astype(BF16)
        dom_ref[...] = _rms_bwd(dam, mhat, rm, gmo_ref[...], FOX_W).astype(BF16)

    row = lambda w: pl.BlockSpec((tm, w), lambda i: (i, 0))
    ff = jax.ShapeDtypeStruct((T, D_FF), BF16)
    out_shape = (
        jax.ShapeDtypeStruct((T, 1024), BF16), jax.ShapeDtypeStruct((T, 1024), BF16), ff, ff, ff,
        jax.ShapeDtypeStruct((T, 1024), BF16), jax.ShapeDtypeStruct((T, 1024), F32),
        jax.ShapeDtypeStruct((T, 512), BF16), jax.ShapeDtypeStruct((T, 512), BF16), jax.ShapeDtypeStruct((8, 1024), F32),
    )
    return pl.pallas_call(
        body, name="mid", grid=(T // tm,), out_shape=out_shape,
        in_specs=[row(512), row(512), row(1024), row(1024), _full(g_fo.shape), _full(g_mo.shape), _full(g2.shape), _full(g3.shape),
                  _full(w_o.shape), _full(w_g.shape), _full(w_u.shape), _full(w_d.shape)],
        out_specs=[row(1024), row(1024), row(D_FF), row(D_FF), row(D_FF), row(1024), row(1024), row(512), row(512),
                   pl.BlockSpec((8, 1024), lambda i: (0, 0))],
        compiler_params=_params(("arbitrary",)),
    )(of, om, x, tgt, g_fo, g_mo, g2, g3, w_o, w_g, w_u, w_d)


def _in_bwd(dqf, dkf, dvf, dqm, dkm, dvm, lat, x, dx2, g1, gq, gkv, bfg, ct, st, sel_t, w_in, w_q12, w_kv, seq, tm, ex=None):
    T = x.shape[0]
    nblk = T // tm
    nsb = seq // tm
    n_in, n_out = (len(ex.ins), len(ex.outs)) if ex else (0, 0)

    def body(*refs):
        (dqf_ref, dkf_ref, dvf_ref, dqm_ref, dkm_ref, dvm_ref, lat_ref, x_ref, dx2_ref, g1_ref, gq_ref, gkv_ref, b_ref,
         ct_ref, st_ref, selt_ref, win_ref, wq_ref, wkv_ref) = refs[0:19]
        dx_ref, dproj_ref, dq12_ref, dkv_ref, stat_ref = refs[19 + n_in:24 + n_in]
        carry = refs[24 + n_in + n_out]
        i = pl.program_id(0)
        if ex:
            ex_refs = (refs[19:19 + n_in], refs[24 + n_in:24 + n_in + n_out], refs[25 + n_in + n_out:])

            @pl.when(i == 0)
            def _():
                ex.start(*ex_refs)

        @pl.when(i == 0)
        def _():
            stat_ref[...] = jnp.zeros_like(stat_ref)

        @pl.when(i % nsb == 0)
        def _():
            carry[...] = jnp.zeros_like(carry)

        lane = lax.broadcasted_iota(jnp.int32, (tm, LANES), 1)
        low = lane < 64
        ctv, stv = ct_ref[...], st_ref[...]

        for j in range(PAIRS):
            e, o = 2 * LANES * j, 2 * LANES * j + LANES
            dq = jnp.where(low, dqf_ref[:, e:e + LANES], 0.0) + pltpu.roll(jnp.where(low, dqf_ref[:, o:o + LANES], 0.0), 64, 1)
            dk = jnp.where(low, dkf_ref[:, e:e + LANES], 0.0) + pltpu.roll(jnp.where(low, dkf_ref[:, o:o + LANES], 0.0), 64, 1)
            dproj_ref[:, C_FQ + LANES * j:C_FQ + LANES * (j + 1)] = (dq * FOX_SCALE).astype(BF16)
            dproj_ref[:, C_FK + LANES * j:C_FK + LANES * (j + 1)] = dk.astype(BF16)
        dproj_ref[:, C_FV:C_QL] = dvf_ref[...].astype(BF16)
        dcv = dkf_ref[...] + dqf_ref[...]
        k_hi = dcv.astype(BF16)
        k_lo = (dcv - k_hi.astype(F32)).astype(BF16)
        dc = _dot(k_hi, selt_ref[...]) + _dot(k_lo, selt_ref[...])
        rr = lax.broadcasted_iota(jnp.int32, (tm, tm), 0)
        cc = lax.broadcasted_iota(jnp.int32, (tm, tm), 1)
        triu = (cc >= rr).astype(BF16)
        a0, a1, a2 = _split3(dc)
        dlf = _dot(triu, a0) + _dot(triu, a1) + _dot(triu, a2) + carry[0:1, :]
        carry[0:1, :] = dlf[0:1, :]
        misc_a = lat_ref[:, Q_RANK + KV_RANK:Q_RANK + KV_RANK + LANES]
        z = misc_a + b_ref[...]
        dz = jnp.where(lane < HEADS, dlf * jax.nn.sigmoid(-z), 0.0)
        stat_ref[3:4, 0:LANES] += jnp.sum(dz, axis=0, keepdims=True)

        cq = (jnp.where(low, 1.0, 0.0) + ctv) * MLA_SCALE
        sq = stv * MLA_SCALE
        dkpe = jnp.zeros((tm, LANES), F32)
        for hd in range(HEADS):
            s0 = LANES * hd
            dqh = dqm_ref[:, s0:s0 + LANES]
            dq12_ref[:, s0:s0 + LANES] = (dqh * cq).astype(BF16)
            dq12_ref[:, 1024 + s0:1024 + s0 + LANES] = (dqh * sq).astype(BF16)
            dkpe = dkpe + dkm_ref[:, s0:s0 + LANES]
        dkv_ref[:, 0:1024] = dkm_ref[...].astype(BF16)
        dkv_ref[:, 1024:1536] = dvm_ref[...].astype(BF16)
        dproj_ref[:, C_MA:C_MB] = (dz + dkpe * ctv).astype(BF16)
        dproj_ref[:, C_MB:C_END] = (dkpe * stv).astype(BF16)
        dqn = _dot(dq12_ref[...], wq_ref[...])
        dkvn = _dot_nt(dkv_ref[...], wkv_ref[...])
        ql = lat_ref[:, 0:Q_RANK]
        kvl = lat_ref[:, Q_RANK:Q_RANK + KV_RANK]
        rq, rkv = _rms(ql, Q_RANK), _rms(kvl, KV_RANK)
        qhat, kvhat = ql * rq, kvl * rkv
        stat_ref[1:2, 0:Q_RANK] += jnp.sum(dqn * qhat, axis=0, keepdims=True)
        stat_ref[2:3, 0:KV_RANK] += jnp.sum(dkvn * kvhat, axis=0, keepdims=True)
        dproj_ref[:, C_QL:C_KVL] = _rms_bwd(dqn, qhat, rq, gq_ref[...], Q_RANK).astype(BF16)
        dproj_ref[:, C_KVL:C_MA] = _rms_bwd(dkvn, kvhat, rkv, gkv_ref[...], KV_RANK).astype(BF16)

        dh1 = _dot(dproj_ref[...], win_ref[...])
        xv = x_ref[...]
        r1 = _rms(xv, D_MODEL)
        xh = xv * r1
        stat_ref[0:1, :] += jnp.sum(dh1 * xh, axis=0, keepdims=True)
        dx_ref[...] = dx2_ref[...] + _rms_bwd(dh1, xh, r1, g1_ref[...], D_MODEL)

        if ex:
            @pl.when(i == nblk - 1)
            def _():
                ex.wait(*ex_refs)

    rev = lambda w: pl.BlockSpec((tm, w), lambda i: (nblk - 1 - i, 0))
    out_shape = (
        jax.ShapeDtypeStruct((T, 1024), F32), jax.ShapeDtypeStruct((T, C_END), BF16), jax.ShapeDtypeStruct((T, 2048), BF16),
        jax.ShapeDtypeStruct((T, 1536), BF16), jax.ShapeDtypeStruct((8, 1024), F32),
    ) + tuple(ex.outs if ex else ())
    res = pl.pallas_call(
        body, name="in_bwd", grid=(nblk,), out_shape=out_shape,
        in_specs=[rev(1024), rev(1024), rev(512), rev(1024), rev(1024), rev(512), rev(512), rev(1024), rev(1024),
                  _full(g1.shape), _full(gq.shape), _full(gkv.shape), _full(bfg.shape), rev(LANES), rev(LANES), _full(sel_t.shape),
                  _full(w_in.shape), _full(w_q12.shape), _full(w_kv.shape)] + [ANY] * n_in,
        out_specs=[rev(1024), rev(C_END), rev(2048), rev(1536), pl.BlockSpec((8, 1024), lambda i: (0, 0))] + [ANY] * n_out,
        scratch_shapes=[pltpu.VMEM((8, LANES), F32)] + (ex.sems() if ex else []),
        compiler_params=_params(("arbitrary",)),
    )(dqf, dkf, dvf, dqm, dkm, dvm, lat, x, dx2, g1, gq, gkv, bfg, ct, st, sel_t, w_in, w_q12, w_kv, *(ex.ins if ex else ()))
    return res[0], res[1], res[2], res[3], res[4], list(res[5:])


def _wgrad(a, b, tk, tt, name, ex=None):
    T, K = a.shape
    N = b.shape[1]
    n_in, n_out = (len(ex.ins), len(ex.outs)) if ex else (0, 0)
    gk, gt = K // tk, T // tt

    def body(*refs):
        a_ref, b_ref, o_ref = refs[0], refs[1], refs[2 + n_in]
        kb, t = pl.program_id(0), pl.program_id(1)
        if ex:
            ex_refs = (refs[2:2 + n_in], refs[3 + n_in:3 + n_in + n_out], refs[3 + n_in + n_out:])

            @pl.when((kb == 0) & (t == 0))
            def _():
                ex.start(*ex_refs)

        @pl.when(t == 0)
        def _():
            o_ref[...] = jnp.zeros_like(o_ref)

        o_ref[...] += _dot_tn(a_ref[...].astype(BF16), b_ref[...].astype(BF16))

        if ex:
            @pl.when((kb == gk - 1) & (t == gt - 1))
            def _():
                ex.wait(*ex_refs)

    res = pl.pallas_call(
        body, name=name, grid=(gk, gt), out_shape=(jax.ShapeDtypeStruct((K, N), F32),) + tuple(ex.outs if ex else ()),
        in_specs=[pl.BlockSpec((tt, tk), lambda kb, t: (t, kb)), pl.BlockSpec((tt, N), lambda kb, t: (t, 0))] + [ANY] * n_in,
        out_specs=[pl.BlockSpec((tk, N), lambda kb, t: (kb, 0))] + [ANY] * n_out,
        scratch_shapes=ex.sems() if ex else [], input_output_aliases=ex.aliases(2, 1) if ex else {},
        compiler_params=_params(("arbitrary", "arbitrary")),
    )(a, b, *(ex.ins if ex else ()))
    return (res[0], list(res[1:])) if ex else res[0]


def _adamw(w, g, m, v, name):
    R, C = w.shape
    tr = _row_tile(R)

    def body(w_ref, g_ref, m_ref, v_ref, d_ref, nm_ref, nv_ref):
        gv = g_ref[...]
        nm = ADAM_B1 * m_ref[...] + (1.0 - ADAM_B1) * gv
        nv = ADAM_B2 * v_ref[...] + (1.0 - ADAM_B2) * (gv * gv)
        m_hat = nm / (1.0 - ADAM_B1 ** ADAM_STEP)
        v_hat = nv / (1.0 - ADAM_B2 ** ADAM_STEP)
        d_ref[...] = -ADAM_LR * (m_hat / (jnp.sqrt(v_hat) + ADAM_EPS) + ADAM_WD * w_ref[...])
        nm_ref[...] = nm
        nv_ref[...] = nv

    blk = pl.BlockSpec((tr, C), lambda i: (i, 0))
    sh = jax.ShapeDtypeStruct((R, C), F32)
    return pl.pallas_call(
        body, name=name, grid=(R // tr,), out_shape=(sh, sh, sh),
        in_specs=[blk, blk, blk, blk], out_specs=[blk, blk, blk],
        compiler_params=_params(("arbitrary",)),
    )(w, g, m, v)


def _arrange(win_t, wuq_t, wukv):
    dt = win_t.dtype
    z = lambda r: jnp.zeros((r, D_MODEL), dt)
    zh = lambda r: jnp.zeros((HEADS, r, Q_RANK), dt)
    kr1, kr2 = win_t[1928:1944], win_t[1944:1960]
    misc_a = jnp.concatenate([win_t[1536:1544], z(56), kr1, kr2, z(32)], axis=0)
    misc_b = jnp.concatenate([z(64), kr2, kr1, z(32)], axis=0)
    w_in = jnp.concatenate([win_t[0:1536], win_t[1544:1928], misc_a, misc_b], axis=0)
    wq = wuq_t.reshape(HEADS, 96, Q_RANK)
    q1 = jnp.concatenate([wq, zh(32)], axis=1).reshape(1024, Q_RANK)
    q2 = jnp.concatenate([zh(64), wq[:, 80:96], wq[:, 64:80], zh(32)], axis=1).reshape(1024, Q_RANK)
    wkv = wukv.reshape(KV_RANK, HEADS, 128)
    wk = jnp.concatenate([wkv[:, :, 0:64], jnp.zeros((KV_RANK, HEADS, 64), dt)], axis=2).reshape(KV_RANK, 1024)
    wv = wkv[:, :, 64:128].reshape(KV_RANK, 512)
    return dict(w_in=w_in, w_q12=jnp.concatenate([q1, q2], axis=0), w_k=wk, w_v=wv, w_kv=jnp.concatenate([wk, wv], axis=1))


def _unarrange(g_in, g_q12, g_kv):
    kr1 = g_in[C_MA + 64:C_MA + 80] + g_in[C_MB + 80:C_MB + 96]
    kr2 = g_in[C_MA + 80:C_MA + 96] + g_in[C_MB + 64:C_MB + 80]
    win_t = jnp.concatenate([g_in[0:1536], g_in[C_MA:C_MA + 8], g_in[1536:1920], kr1, kr2], axis=0)
    g1 = g_q12[0:1024].reshape(HEADS, 128, Q_RANK)
    g2 = g_q12[1024:2048].reshape(HEADS, 128, Q_RANK)
    wuq_t = jnp.concatenate([g1[:, 0:64], g1[:, 64:80] + g2[:, 80:96], g1[:, 80:96] + g2[:, 64:80]], axis=1).reshape(768, Q_RANK)
    gk = g_kv[:, 0:1024].reshape(KV_RANK, HEADS, 128)
    gv = g_kv[:, 1024:1536].reshape(KV_RANK, HEADS, 64)
    wukv = jnp.concatenate([gk[:, :, 0:64], gv], axis=2).reshape(KV_RANK, 1024)
    return win_t, wuq_t, wukv


def _selectors():
    sel = np.zeros((384, 1024), np.float32)
    sel_t = np.zeros((1024, LANES), np.float32)
    for h in range(HEADS):
        for piece in range(3):
            sel[LANES * piece + h, LANES * h + 64 + piece] = 1.0
        sel_t[LANES * h + 64, h] = 1.0
    return jnp.asarray(sel, BF16), jnp.asarray(sel_t, BF16)


def _rope_tables(positions):
    inv_freq = 10000.0 ** (-jnp.arange(0, ROPE, 2, dtype=F32) / ROPE)
    ang = positions.reshape(-1).astype(F32)[:, None] * inv_freq[None, :]
    cos, sin = jnp.cos(ang), jnp.sin(ang)
    z64, z32 = jnp.zeros((ang.shape[0], 64), F32), jnp.zeros((ang.shape[0], 32), F32)
    return jnp.concatenate([z64, cos, cos, z32], axis=1), jnp.concatenate([z64, -sin, sin, z32], axis=1)


def _reduce_tail(slabs):
    got = _run_exchange(_swap_exchange(slabs), "tail_swap")
    sums = [_add_half(g, s) for g, s in zip(slabs, got)]
    recv = _run_exchange(_scatter_exchange(sums), "tail_scatter")
    return _run_exchange(_join_exchange([_sum_slabs(g, s, r) for g, s, r in zip(slabs, got, recv)]), "tail_join")


def _work(name, t):
    return jnp.swapaxes(t[0], 0, 1) if name in TRANSPOSED else t[0]


def _back(name, t):
    return (jnp.swapaxes(t, 0, 1) if name in TRANSPOSED else t)[None]


SMALL_ROWS = {"norm_mix_g": (0, 1024), "norm_ffn_g": (1, 1024), "final_norm_g": (2, 1024), "q_norm_g": (4, 256),
              "kv_norm_g": (5, 128), "b_fgate": (6, 8)}


def kernel(x, positions, norm_mix_g, w_in, b_fgate, q_norm_g, w_uq, kv_norm_g, w_ukv, fox_out_g, mla_out_g, w_o, norm_ffn_g, w_gate, w_up, w_down, final_norm_g, loss_target, m_norm_mix_g, m_w_in, m_b_fgate, m_q_norm_g, m_w_uq, m_kv_norm_g, m_w_ukv, m_fox_out_g, m_mla_out_g, m_w_o, m_norm_ffn_g, m_w_gate, m_w_up, m_w_down, m_final_norm_g, v_norm_mix_g, v_w_in, v_b_fgate, v_q_norm_g, v_w_uq, v_kv_norm_g, v_w_ukv, v_fox_out_g, v_mla_out_g, v_w_o, v_norm_ffn_g, v_w_gate, v_w_up, v_w_down, v_final_norm_g):
    names = ["norm_mix_g", "w_in", "b_fgate", "q_norm_g", "w_uq", "kv_norm_g", "w_ukv", "fox_out_g", "mla_out_g", "w_o",
             "norm_ffn_g", "w_gate", "w_up", "w_down", "final_norm_g"]
    wts = dict(zip(names, [norm_mix_g, w_in, b_fgate, q_norm_g, w_uq, kv_norm_g, w_ukv, fox_out_g, mla_out_g, w_o, norm_ffn_g,
                           w_gate, w_up, w_down, final_norm_g]))
    mom = dict(zip(names, [m_norm_mix_g, m_w_in, m_b_fgate, m_q_norm_g, m_w_uq, m_kv_norm_g, m_w_ukv, m_fox_out_g, m_mla_out_g,
                           m_w_o, m_norm_ffn_g, m_w_gate, m_w_up, m_w_down, m_final_norm_g]))
    var = dict(zip(names, [v_norm_mix_g, v_w_in, v_b_fgate, v_q_norm_g, v_w_uq, v_kv_norm_g, v_w_ukv, v_fox_out_g, v_mla_out_g,
                           v_w_o, v_norm_ffn_g, v_w_gate, v_w_up, v_w_down, v_final_norm_g]))
    shard = {n: _work(n, wts[n]) for n in HEAD3 + FFN4}
    nb, seq, _ = x.shape
    T = nb * seq
    tm, tq = min(ROW_TILE, seq), min(ATTN_TILE, seq)
    tt = min(WGRAD_TILE, T)
    xf = x.reshape(T, D_MODEL)
    tgt = loss_target.reshape(T, D_MODEL)
    chip = 2 * lax.axis_index("x") + lax.axis_index("y")

    mine = [shard[n].astype(BF16) for n in HEAD3]
    head = _run_exchange(_gather_exchange(mine, own=False), "gather_head")
    win4, wuq4, wukv4 = [lax.dynamic_update_slice(h, s[None], (chip, 0, 0)) for h, s in zip(head, mine)]
    a = _arrange(win4.reshape(-1, D_MODEL), wuq4.reshape(-1, Q_RANK), wukv4.transpose(1, 0, 2).reshape(KV_RANK, -1))
    sel, sel_t = _selectors()
    ct, st = _rope_tables(positions)
    bfg = jnp.concatenate([b_fgate, jnp.zeros((1, LANES - HEADS), F32)], axis=1)
    g1, gq, gkv = norm_mix_g, q_norm_g, kv_norm_g

    h1, qf, kf, vf, qm, km, vm, lat, qn, kvn = _in_proj(xf, g1, a["w_in"], a["w_q12"], a["w_k"], a["w_v"], gq, gkv, bfg, ct, st, sel, seq, tm)
    of, lse_f, gathered = _attn_fwd(qf, kf, vf, nb, seq, tq, "fox_fwd", _gather_exchange([shard[n].astype(BF16) for n in FFN4]))
    om, lse_m, _ = _attn_fwd(qm, km, vm, nb, seq, tq, "mla_fwd")
    wo4, wg4, wu4, wd4 = gathered
    a_cat, h2, hid, dg, du, dx3, dx2, dof, dom, st_mid = _mid(
        of, om, xf, tgt, fox_out_g, mla_out_g, norm_ffn_g, final_norm_g.reshape(1, D_MODEL),
        wo4.reshape(D_MODEL, D_MODEL), wg4.reshape(D_FF, D_MODEL), wu4.reshape(D_FF, D_MODEL), wd4.reshape(D_FF, D_MODEL), tm)

    slab = lambda g: g.reshape(N_CHIPS, g.shape[0] // N_CHIPS, g.shape[1])
    big = [slab(_wgrad(a_cat, dx2, D_MODEL, tt, "wgrad_o")), slab(_wgrad(dg, h2, D_FF // 2, tt, "wgrad_gate")),
           slab(_wgrad(du, h2, D_FF // 2, tt, "wgrad_up")), slab(_wgrad(hid, dx3, D_FF // 2, tt, "wgrad_down"))]
    dqf, dkf, dvf, got = _attn_bwd(qf, kf, vf, of, dof, lse_f, nb, seq, tq, "fox_bwd", True, _swap_exchange(big))
    sums = [_add_half(g, s) for g, s in zip(big, got)]
    dqm, dkm, dvm, recv = _attn_bwd(qm, km, vm, om, dom, lse_m, nb, seq, tq, "mla_bwd", False, _scatter_exchange(sums))
    halves = [_sum_slabs(g, s, r) for g, s, r in zip(big, got, recv)]
    dx, dproj, dq12, dkv, st_in, _ = _in_bwd(dqf, dkf, dvf, dqm, dkm, dvm, lat, xf, dx2, g1, gq, gkv, bfg, ct, st, sel_t,
                                             a["w_in"], a["w_q12"], a["w_kv"], seq, tm)
    g_in, joined = _wgrad(dproj, h1, C_END, tt, "wgrad_in", _join_exchange(halves))
    gshard = dict(zip(FFN4, joined))

    loss_row = jnp.concatenate([jnp.sum(st_mid[3:4, :], axis=1, keepdims=True), jnp.zeros((1, D_MODEL - 1), F32)], axis=1)
    stats = jnp.concatenate([st_in[0:1], st_mid[1:2], st_mid[0:1], st_mid[2:3], st_in[1:2], st_in[2:3], st_in[3:4], loss_row], axis=0)
    stats = _allreduce_small(stats)
    gwin_t, gwuq_t, gwukv = _unarrange(g_in, _wgrad(dq12, qn, 2048, tt, "wgrad_uq"), _wgrad(kvn, dkv, KV_RANK, tt, "wgrad_ukv"))
    tail = [slab(gwin_t), slab(gwuq_t), gwukv.reshape(KV_RANK, N_CHIPS, -1).transpose(1, 0, 2)]
    gshard.update(zip(HEAD3, _reduce_tail(tail)))

    grads, delta, new_m, new_v = {}, {}, {}, {}
    for n in HEAD3 + FFN4:
        grads[n] = _back(n, gshard[n])
        d, nm, nv = _adamw(shard[n], gshard[n], _work(n, mom[n]), _work(n, var[n]), "adamw_" + n)
        delta[n], new_m[n], new_v[n] = _back(n, d), _back(n, nm), _back(n, nv)
    sm_g = {n: stats[row:row + 1, 0:width] for n, (row, width) in SMALL_ROWS.items()}
    sm_g["fox_out_g"] = stats[3:4, 0:512]
    sm_g["mla_out_g"] = stats[3:4, 512:1024]
    pad = lambda t: jnp.pad(t.reshape(1, -1), ((0, 0), (0, 1024 - t.size)))
    stack = lambda d: jnp.concatenate([pad(d[n]) for n in SMALL], axis=0)
    sd, sm, sv = _adamw(stack(wts), stack(sm_g), stack(mom), stack(var), "adamw_small")
    for i, n in enumerate(SMALL):
        shp = wts[n].shape
        grads[n] = sm_g[n].reshape(shp)
        delta[n] = sd[i, 0:wts[n].size].reshape(shp)
        new_m[n] = sm[i, 0:wts[n].size].reshape(shp)
        new_v[n] = sv[i, 0:wts[n].size].reshape(shp)
    loss = stats[7, 0]
    return (loss, dx.reshape(x.shape), *[grads[n] for n in names], *[delta[n] for n in names],
            *[new_m[n] for n in names], *[new_v[n] for n in names])
```

```python
import functools

import numpy as np
import jax
import jax.numpy as jnp
from jax import lax
from jax.experimental import pallas as pl
from jax.experimental.pallas import tpu as pltpu

F32 = jnp.float32
BF16 = jnp.bfloat16
MESH = pl.DeviceIdType.MESH

EPS = 1e-6
D_MODEL = 1024
HEADS = 8
PAIRS = HEADS // 2
FOX_W = 512
Q_RANK = 256
KV_RANK = 128
ROPE = 32
D_FF = 2816
N_CHIPS = 4
FOX_SCALE = 64 ** -0.5
MLA_SCALE = 96 ** -0.5
LANES = 128
NEG = -1e30

ADAM_LR, ADAM_B1, ADAM_B2, ADAM_EPS, ADAM_WD, ADAM_STEP = 0.001, 0.9, 0.999, 1e-08, 0.01, 10

C_FQ, C_FK, C_FV, C_QL, C_KVL, C_MA, C_MB, C_END = 0, 512, 1024, 1536, 1792, 1920, 2048, 2176

VMEM_LIMIT = 60 * 1024 * 1024
ROW_TILE = 256
ATTN_TILE = 512
WGRAD_TILE = 2048

HEAD3 = ("w_in", "w_uq", "w_ukv")
FFN4 = ("w_o", "w_gate", "w_up", "w_down")
TRANSPOSED = ("w_in", "w_uq", "w_gate", "w_up")
SMALL = ("norm_mix_g", "b_fgate", "q_norm_g", "kv_norm_g", "fox_out_g", "mla_out_g", "norm_ffn_g", "final_norm_g")


def _params(sem=None):
    return pltpu.CompilerParams(dimension_semantics=sem, vmem_limit_bytes=VMEM_LIMIT)


def _full(shape):
    n = len(shape)
    return pl.BlockSpec(shape, lambda *_: (0,) * n, pipeline_mode=pl.Buffered(1))


def _dot(a, b):
    return jnp.dot(a, b, preferred_element_type=F32)


def _dot_nt(a, b):
    return lax.dot_general(a, b, (((1,), (1,)), ((), ())), preferred_element_type=F32)


def _dot_tn(a, b):
    return lax.dot_general(a, b, (((0,), (0,)), ((), ())), preferred_element_type=F32)


def _split3(v):
    hi = v.astype(BF16)
    r1 = v - hi.astype(F32)
    mid = r1.astype(BF16)
    lo = (r1 - mid.astype(F32)).astype(BF16)
    return hi, mid, lo


def _rms(v, width):
    return lax.rsqrt(jnp.sum(v * v, axis=1, keepdims=True) * (1.0 / width) + EPS)


def _rms_bwd(dy, xhat, r, g, width):
    u = dy * g
    return r * (u - xhat * (jnp.sum(u * xhat, axis=1, keepdims=True) * (1.0 / width)))


ANY = pl.BlockSpec(memory_space=pl.ANY)


def _place():
    return lax.axis_index("x"), lax.axis_index("y"), lax.axis_index("c")


def _other_chips(x, y):
    return [(1 - x, y), (x, 1 - y), (1 - x, 1 - y)]


def _remote(src, dst, send, recv, j, dev):
    return pltpu.make_async_remote_copy(src_ref=src, dst_ref=dst, send_sem=send.at[j], recv_sem=recv.at[j], device_id=dev, device_id_type=MESH)


class _Exchange:
    def __init__(self, ins, outs, n_remote, n_local, build, in_place=False):
        self.ins, self.outs, self.n_remote, self.n_local, self.build = list(ins), list(outs), n_remote, max(n_local, 1), build
        self.in_place = in_place

    def aliases(self, first_in, first_out):
        return {first_in + i: first_out + i for i in range(len(self.ins))} if self.in_place else {}

    def sems(self):
        return [pltpu.SemaphoreType.DMA((self.n_remote,)), pltpu.SemaphoreType.DMA((self.n_remote,)), pltpu.SemaphoreType.DMA((self.n_local,))]

    def start(self, in_refs, out_refs, sems):
        for cp in self.build(in_refs, out_refs, *sems)[0]:
            cp.start()

    def wait(self, in_refs, out_refs, sems):
        for w in self.build(in_refs, out_refs, *sems)[1]:
            w()


def _gather_exchange(shards, own=True):
    def build(ins, outs, send, recv, lsem):
        x, y, c = _place()
        starts, waits = [], []
        for i, (s, o) in enumerate(zip(ins, outs)):
            if own:
                mine = pltpu.make_async_copy(s, o.at[2 * x + y], lsem.at[i])
                starts.append(mine)
                waits.append(mine.wait)
            for j, (cx, cy) in enumerate(_other_chips(x, y)):
                out = _remote(s, o.at[2 * x + y], send, recv, 3 * i + j, (cx, cy, c))
                starts.append(out)
                waits.append(_remote(s, o.at[2 * cx + cy], send, recv, 3 * i + j, (cx, cy, c)).wait_recv)
                waits.append(out.wait_send)
        return starts, waits

    outs = [jax.ShapeDtypeStruct((N_CHIPS,) + s.shape, s.dtype) for s in shards]
    return _Exchange(shards, outs, 3 * len(shards), len(shards), build)


def _gather_split_exchange(shards):
    n = len(shards)

    def build(ins, outs, send, recv, lsem):
        x, y, c = _place()
        starts, waits, last = [], [], []
        for i, (s, o) in enumerate(zip(ins, outs)):
            hc = s.shape[1] // 2
            mine, other = pl.ds(c * hc, hc), pl.ds((1 - c) * hc, hc)
            for j, (cx, cy) in enumerate(_other_chips(x, y)):
                out = _remote(s.at[:, mine], o.at[2 * x + y, :, mine], send, recv, 3 * i + j, (cx, cy, c))
                landed = o.at[2 * cx + cy, :, mine]
                arrive = _remote(s.at[:, mine], landed, send, recv, 3 * i + j, (cx, cy, c))
                onward = _remote(landed, landed, send, recv, 3 * n + 3 * i + j, (x, y, 1 - c))
                from_sibling = _remote(landed, o.at[2 * cx + cy, :, other], send, recv, 3 * n + 3 * i + j, (x, y, 1 - c))
                starts.append(out)
                waits.append(lambda arrive=arrive, onward=onward: (arrive.wait_recv(), onward.start()))
                last += [from_sibling.wait_recv, onward.wait_send, out.wait_send]
        return starts, waits + last

    outs = [jax.ShapeDtypeStruct((N_CHIPS,) + s.shape, s.dtype) for s in shards]
    return _Exchange(shards, outs, 6 * n, 0, build)


def _swap_exchange(grads):
    def build(ins, outs, send, recv, lsem):
        x, y, c = _place()
        cps = []
        for i, (g, o) in enumerate(zip(ins, outs)):
            hc = g.shape[2] // 2
            cps.append(_remote(g.at[:, :, pl.ds((1 - c) * hc, hc)], o, send, recv, i, (x, y, 1 - c)))
        return cps, [cp.wait for cp in cps]

    outs = [jax.ShapeDtypeStruct((g.shape[0], g.shape[1], g.shape[2] // 2), g.dtype) for g in grads]
    return _Exchange(grads, outs, len(grads), 0, build)


def _scatter_exchange(sums):
    def build(ins, outs, send, recv, lsem):
        x, y, c = _place()
        cps = []
        for i, (s, o) in enumerate(zip(ins, outs)):
            for j, (cx, cy) in enumerate(_other_chips(x, y)):
                cps.append(_remote(s.at[2 * cx + cy], o.at[j], send, recv, 3 * i + j, (cx, cy, c)))
        return cps, [cp.wait for cp in cps]

    outs = [jax.ShapeDtypeStruct((3,) + s.shape[1:], s.dtype) for s in sums]
    return _Exchange(sums, outs, 3 * len(sums), 0, build)


def _join_exchange(bufs):
    def build(ins, outs, send, recv, lsem):
        x, y, c = _place()
        starts, waits = [], []
        for i, (t, o) in enumerate(zip(ins, outs)):
            hc = t.shape[1] // 2
            out = _remote(t.at[:, pl.ds(c * hc, hc)], o.at[:, pl.ds(c * hc, hc)], send, recv, i, (x, y, 1 - c))
            starts.append(out)
            waits += [_remote(t.at[:, pl.ds(c * hc, hc)], o.at[:, pl.ds((1 - c) * hc, hc)], send, recv, i, (x, y, 1 - c)).wait_recv,
                      out.wait_send]
        return starts, waits

    outs = [jax.ShapeDtypeStruct(t.shape, t.dtype) for t in bufs]
    return _Exchange(bufs, outs, len(bufs), 0, build, in_place=True)


def _run_exchange(ex, name):
    n_in, n_out = len(ex.ins), len(ex.outs)

    def body(*refs):
        ins, outs, sems = refs[:n_in], refs[n_in:n_in + n_out], refs[n_in + n_out:]
        ex.start(ins, outs, sems)
        ex.wait(ins, outs, sems)

    return pl.pallas_call(
        body, name=name, out_shape=tuple(ex.outs), in_specs=[ANY] * n_in, out_specs=tuple([ANY] * n_out),
        scratch_shapes=ex.sems(), input_output_aliases=ex.aliases(0, 0),
        compiler_params=pltpu.CompilerParams(has_side_effects=True),
    )(*ex.ins)


def _allreduce_small(v):
    def body(v_ref, o_ref, buf, send, recv):
        x, y, c = _place()
        me = 4 * x + 2 * y + c
        buf[me] = v_ref[...]
        out = []
        for j in range(7):
            fx, fy, fc = (j + 1) >> 2 & 1, (j + 1) >> 1 & 1, (j + 1) & 1
            out.append(_remote(v_ref, buf.at[me], send, recv, j, (x ^ fx, y ^ fy, c ^ fc)))
        for cp in out:
            cp.start()
        for j in range(7):
            fx, fy, fc = (j + 1) >> 2 & 1, (j + 1) >> 1 & 1, (j + 1) & 1
            src = 4 * (x ^ fx) + 2 * (y ^ fy) + (c ^ fc)
            _remote(v_ref, buf.at[src], send, recv, j, (x ^ fx, y ^ fy, c ^ fc)).wait_recv()
        for cp in out:
            cp.wait_send()
        acc = buf[0]
        for d in range(1, 8):
            acc = acc + buf[d]
        o_ref[...] = acc

    vm = pl.BlockSpec(memory_space=pltpu.VMEM)
    return pl.pallas_call(
        body, name="allreduce_small", out_shape=jax.ShapeDtypeStruct(v.shape, v.dtype),
        in_specs=[vm], out_specs=vm,
        scratch_shapes=[pltpu.VMEM((8,) + v.shape, v.dtype), pltpu.SemaphoreType.DMA((7,)), pltpu.SemaphoreType.DMA((7,))],
        compiler_params=pltpu.CompilerParams(has_side_effects=True),
    )(v)


def _add_half(g, got):
    n, R, C = g.shape
    hc = C // 2

    def body(c_ref, g_ref, r_ref, o_ref):
        o_ref[...] = (g_ref[...] + r_ref[...]).astype(BF16)

    c = lax.axis_index("c")
    return pl.pallas_call(
        body, name="add_half",
        grid_spec=pltpu.PrefetchScalarGridSpec(
            num_scalar_prefetch=1, grid=(n,),
            in_specs=[pl.BlockSpec((1, R, hc), lambda k, c_ref: (k, 0, c_ref[0])),
                      pl.BlockSpec((1, R, hc), lambda k, c_ref: (k, 0, 0))],
            out_specs=pl.BlockSpec((1, R, hc), lambda k, c_ref: (k, 0, 0))),
        out_shape=jax.ShapeDtypeStruct((n, R, hc), BF16),
        compiler_params=_params(("arbitrary",)),
    )(jnp.reshape(c, (1,)).astype(jnp.int32), g, got)


def _sum_slabs(g, got, recv):
    n, R, C = g.shape
    hc = C // 2

    def body(kc_ref, g_ref, s_ref, r_ref, o_ref):
        o_ref[...] = (((g_ref[0] + s_ref[0]) + r_ref[0].astype(F32)) + r_ref[1].astype(F32)) + r_ref[2].astype(F32)

    kc = jnp.stack([2 * lax.axis_index("x") + lax.axis_index("y"), lax.axis_index("c")]).astype(jnp.int32)
    return pl.pallas_call(
        body, name="sum_slabs",
        grid_spec=pltpu.PrefetchScalarGridSpec(
            num_scalar_prefetch=1, grid=(1,),
            in_specs=[pl.BlockSpec((1, R, hc), lambda i, kc_ref: (kc_ref[0], 0, kc_ref[1])),
                      pl.BlockSpec((1, R, hc), lambda i, kc_ref: (kc_ref[0], 0, 0)),
                      pl.BlockSpec((3, R, hc), lambda i, kc_ref: (0, 0, 0))],
            out_specs=pl.BlockSpec((R, hc), lambda i, kc_ref: (0, kc_ref[1]))),
        out_shape=jax.ShapeDtypeStruct((R, C), F32),
        compiler_params=_params(("arbitrary",)),
    )(kc, g, got, recv)


def _row_tile(rows):
    for cand in (256, 184, 176, 144, 128, 64, 32, 16, 8):
        if rows % cand == 0:
            return cand
    return rows


def _in_proj(x, g1, w_in, w_q12, w_k, w_v, gq, gkv, bfg, ct, st, sel, seq, tm):
    T = x.shape[0]
    nsb = seq // tm

    def body(x_ref, g1_ref, win_ref, wq_ref, wk_ref, wv_ref, gq_ref, gkv_ref, b_ref, ct_ref, st_ref, sel_ref,
             h1_ref, qf_ref, kf_ref, vf_ref, qm_ref, km_ref, vm_ref, lat_ref, qn_ref, kvn_ref, carry):
        i = pl.program_id(0)

        @pl.when(i % nsb == 0)
        def _():
            carry[...] = jnp.zeros_like(carry)

        xv = x_ref[...]
        h = (xv * _rms(xv, D_MODEL) * g1_ref[...]).astype(BF16)
        h1_ref[...] = h
        proj = _dot_nt(h, win_ref[...])
        lane = lax.broadcasted_iota(jnp.int32, (tm, LANES), 1)
        low = lane < 64
        misc_a = proj[:, C_MA:C_MB]
        misc_b = proj[:, C_MB:C_END]

        z = misc_a + b_ref[...]
        lf = jnp.where(lane < HEADS, jnp.minimum(z, 0.0) - jnp.log1p(jnp.exp(-jnp.abs(z))), 0.0)
        rr = lax.broadcasted_iota(jnp.int32, (tm, tm), 0)
        cc = lax.broadcasted_iota(jnp.int32, (tm, tm), 1)
        tri = (rr >= cc).astype(BF16)
        a0, a1, a2 = _split3(lf)
        c = _dot(tri, a0) + _dot(tri, a1) + _dot(tri, a2) + carry[0:1, :]
        carry[0:1, :] = c[tm - 1:tm, :]
        c0, c1, c2 = _split3(c)
        cpl = _dot(jnp.concatenate([c0, c1, c2], axis=1), sel_ref[...])
        qpad = jnp.where((lane >= 64) & (lane < 67), -1.0, 0.0)
        for j in range(PAIRS):
            qc = proj[:, C_FQ + LANES * j:C_FQ + LANES * (j + 1)] * FOX_SCALE
            kc = proj[:, C_FK + LANES * j:C_FK + LANES * (j + 1)]
            e, o = 2 * LANES * j, 2 * LANES * j + LANES
            qf_ref[:, e:e + LANES] = jnp.where(low, qc, qpad).astype(BF16)
            qf_ref[:, o:o + LANES] = jnp.where(low, pltpu.roll(qc, 64, 1), qpad).astype(BF16)
            kf_ref[:, e:e + LANES] = jnp.where(low, kc, cpl[:, e:e + LANES]).astype(BF16)
            kf_ref[:, o:o + LANES] = jnp.where(low, pltpu.roll(kc, 64, 1), cpl[:, o:o + LANES]).astype(BF16)
        vf_ref[...] = proj[:, C_FV:C_QL].astype(BF16)

        ql = proj[:, C_QL:C_KVL]
        kvl = proj[:, C_KVL:C_MA]
        qn = (ql * _rms(ql, Q_RANK) * gq_ref[...]).astype(BF16)
        kvn = (kvl * _rms(kvl, KV_RANK) * gkv_ref[...]).astype(BF16)
        lat_ref[...] = proj[:, C_QL:C_MB]
        qn_ref[...] = qn
        kvn_ref[...] = kvn
        q12 = _dot_nt(qn, wq_ref[...])
        kn = _dot(kvn, wk_ref[...])
        ctv = ct_ref[...]
        stv = st_ref[...]
        cq = (jnp.where(low, 1.0, 0.0) + ctv) * MLA_SCALE
        sq = stv * MLA_SCALE
        kpe = misc_a * ctv + misc_b * stv
        for hd in range(HEADS):
            s0 = LANES * hd
            qm_ref[:, s0:s0 + LANES] = (q12[:, s0:s0 + LANES] * cq + q12[:, 1024 + s0:1024 + s0 + LANES] * sq).astype(BF16)
            km_ref[:, s0:s0 + LANES] = (kn[:, s0:s0 + LANES] + kpe).astype(BF16)
        vm_ref[...] = _dot(kvn, wv_ref[...]).astype(BF16)

    row = lambda w: pl.BlockSpec((tm, w), lambda i: (i, 0))
    out_shape = (
        jax.ShapeDtypeStruct((T, D_MODEL), BF16),
        jax.ShapeDtypeStruct((T, 1024), BF16), jax.ShapeDtypeStruct((T, 1024), BF16), jax.ShapeDtypeStruct((T, 512), BF16),
        jax.ShapeDtypeStruct((T, 1024), BF16), jax.ShapeDtypeStruct((T, 1024), BF16), jax.ShapeDtypeStruct((T, 512), BF16),
        jax.ShapeDtypeStruct((T, 512), F32),
        jax.ShapeDtypeStruct((T, Q_RANK), BF16), jax.ShapeDtypeStruct((T, KV_RANK), BF16),
    )
    return pl.pallas_call(
        body, name="in_proj", grid=(T // tm,), out_shape=out_shape,
        in_specs=[row(D_MODEL), _full(g1.shape), _full(w_in.shape), _full(w_q12.shape), _full(w_k.shape), _full(w_v.shape),
                  _full(gq.shape), _full(gkv.shape), _full(bfg.shape), row(LANES), row(LANES), _full(sel.shape)],
        out_specs=[row(D_MODEL), row(1024), row(1024), row(512), row(1024), row(1024), row(512), row(512), row(Q_RANK), row(KV_RANK)],
        scratch_shapes=[pltpu.VMEM((8, LANES), F32)],
        compiler_params=_params(("arbitrary",)),
    )(x, g1, w_in, w_q12, w_k, w_v, gq, gkv, bfg, ct, st, sel)


def _attn_fwd(q, k, v, nb, seq, tq, name, ex=None):
    T = q.shape[0]
    nq = seq // tq
    n_in, n_out = (len(ex.ins), len(ex.outs)) if ex else (0, 0)

    def body(*refs):
        q_ref, k_ref, v_ref = refs[0:3]
        o_ref, lse_ref = refs[3 + n_in:5 + n_in]
        b, pr, qi = pl.program_id(0), pl.program_id(1), pl.program_id(2)
        if ex:
            ex_refs = (refs[3:3 + n_in], refs[5 + n_in:5 + n_in + n_out], refs[8 + n_in + n_out:])

            @pl.when((b == 0) & (pr == 0) & (qi == 0))
            def _():
                ex.start(*ex_refs)

        s_sc, p_sc, acc_sc = refs[5 + n_in + n_out:8 + n_in + n_out]
        strip = 64
        key_s = lax.broadcasted_iota(jnp.int32, (strip, tq), 0)
        qry_s = lax.broadcasted_iota(jnp.int32, (strip, tq), 1)
        row_t = lax.broadcasted_iota(jnp.int32, (LANES, tq), 0)
        acc_sc[...] = jnp.zeros(acc_sc.shape, F32)

        def fold(x, op):
            out = x[0:8]
            for r in range(8, strip, 8):
                out = op(out, x[r:r + 8])
            return out

        def step(kj, state, masked):
            rows = pl.ds(pl.multiple_of(kj * tq, tq), tq)
            for hh in range(2):
                s_sc[hh] = _dot_nt(k_ref[rows, LANES * hh:LANES * (hh + 1)], q_ref[:, LANES * hh:LANES * (hh + 1)])
            vv = v_ref[rows, :]
            new = []
            for hh in range(2):
                m, l = state[hh]

                def strip_of(r0, hh=hh):
                    s = s_sc[hh, r0:r0 + strip, :]
                    return jnp.where(key_s + r0 <= qry_s, s, NEG) if masked else s

                mx = fold(strip_of(0), jnp.maximum)
                for r0 in range(strip, tq, strip):
                    mx = jnp.maximum(mx, fold(strip_of(r0), jnp.maximum))
                m_new = jnp.maximum(m, jnp.max(mx, axis=0, keepdims=True))
                alpha = jnp.exp(m - m_new)
                sm = jnp.zeros((8, tq), F32)
                for r0 in range(0, tq, strip):
                    p = jnp.exp(strip_of(r0) - m_new)
                    sm = sm + fold(p, jnp.add)
                    p_sc[hh, r0:r0 + strip, :] = p.astype(BF16)
                l = alpha * l + jnp.sum(sm, axis=0, keepdims=True)
                acc_sc[hh] = alpha * acc_sc[hh] + _dot_tn(vv, p_sc[hh])
                new.append((m_new, l))
            return tuple(new)

        one = (jnp.full((1, tq), NEG, F32), jnp.zeros((1, tq), F32))
        state = lax.fori_loop(0, qi, functools.partial(step, masked=False), (one, one))
        (m0, l0), (m1, l1) = step(qi, state, True)
        o_ref[...] = jnp.where(row_t < 64, acc_sc[0] / l0, acc_sc[1] / l1).T
        lse_ref[:, 0:LANES] = jnp.broadcast_to(m0 + jnp.log(l0), (LANES, tq)).T
        lse_ref[:, LANES:2 * LANES] = jnp.broadcast_to(m1 + jnp.log(l1), (LANES, tq)).T

        if ex:
            @pl.when((b == nb - 1) & (pr == PAIRS - 1) & (qi == nq - 1))
            def _():
                ex.wait(*ex_refs)

    res = pl.pallas_call(
        body, name=name, grid=(nb, PAIRS, nq),
        out_shape=(jax.ShapeDtypeStruct((T, 512), F32), jax.ShapeDtypeStruct((T, 1024), F32)) + tuple(ex.outs if ex else ()),
        in_specs=[pl.BlockSpec((tq, 2 * LANES), lambda b, p, i: (b * nq + i, p)),
                  pl.BlockSpec((seq, 2 * LANES), lambda b, p, i: (b, p)),
                  pl.BlockSpec((seq, LANES), lambda b, p, i: (b, p))] + [ANY] * n_in,
        out_specs=[pl.BlockSpec((tq, LANES), lambda b, p, i: (b * nq + i, p)),
                   pl.BlockSpec((tq, 2 * LANES), lambda b, p, i: (b * nq + i, p))] + [ANY] * n_out,
        scratch_shapes=[pltpu.VMEM((2, tq, tq), F32), pltpu.VMEM((2, tq, tq), BF16), pltpu.VMEM((2, LANES, tq), F32)]
        + (ex.sems() if ex else []),
        compiler_params=_params(("arbitrary", "arbitrary", "arbitrary")),
    )(q, k, v, *(ex.ins if ex else ()))
    return res[0], res[1], list(res[2:])


def _attn_bwd(q, k, v, o, do, lse, nb, seq, tq, name, key_bias, ex=None):
    T = q.shape[0]
    nq = seq // tq
    n_in, n_out = (len(ex.ins), len(ex.outs)) if ex else (0, 0)

    def body(*refs):
        q_ref, k_ref, v_ref, o_ref, do_ref, lse_ref = refs[0:6]
        dq_ref, dk_ref, dv_ref = refs[6 + n_in:9 + n_in]
        dsc, rsum = refs[9 + n_in + n_out:11 + n_in + n_out]
        b, pr, kj = pl.program_id(0), pl.program_id(1), pl.program_id(2)
        if ex:
            ex_refs = (refs[6:6 + n_in], refs[9 + n_in:9 + n_in + n_out], refs[11 + n_in + n_out:])

            @pl.when((b == 0) & (pr == 0) & (kj == 0))
            def _():
                ex.start(*ex_refs)

        lane_s = lax.broadcasted_iota(jnp.int32, (seq, LANES), 1)
        lane = lax.broadcasted_iota(jnp.int32, (tq, LANES), 1)
        rr = lax.broadcasted_iota(jnp.int32, (tq, tq), 0)
        cc = lax.broadcasted_iota(jnp.int32, (tq, tq), 1)

        @pl.when(kj == 0)
        def _():
            dq_ref[...] = jnp.zeros_like(dq_ref)
            prod = do_ref[...].astype(F32) * o_ref[...]
            d0 = jnp.sum(jnp.where(lane_s < 64, prod, 0.0), axis=1, keepdims=True)
            d1 = jnp.sum(jnp.where(lane_s < 64, 0.0, prod), axis=1, keepdims=True)
            dsc[0] = jnp.broadcast_to(d0, (seq, LANES))
            dsc[1] = jnp.broadcast_to(d1, (seq, LANES))
            if key_bias:
                rsum[...] = jnp.zeros_like(rsum)

        vv = v_ref[...]

        def step(qi, carry, masked):
            dkt, dvt, cols = carry
            rows = pl.ds(pl.multiple_of(qi * tq, tq), tq)
            dov = do_ref[rows, :]
            new_dkt, new_cols = [], []
            for hh in range(2):
                qv = q_ref[rows, LANES * hh:LANES * (hh + 1)]
                kv = k_ref[:, LANES * hh:LANES * (hh + 1)]
                dom = jnp.where((lane < 64) if hh == 0 else (lane >= 64), dov, jnp.zeros((), BF16))
                s = _dot_nt(qv, kv)
                if masked:
                    s = jnp.where(cc <= rr, s, NEG)
                p = jnp.exp(s - lse_ref[rows, LANES * hh:LANES * hh + 1])
                dp = _dot_nt(dom, vv)
                ds32 = p * (dp - dsc[hh, rows, 0:1])
                col = cols[hh]
                if key_bias:
                    col = col + jnp.sum(ds32, axis=0, keepdims=True)
                    rsum[hh, rows, :] += jnp.broadcast_to(jnp.sum(ds32, axis=1, keepdims=True), (tq, LANES))
                ds = ds32.astype(BF16)
                dvt = dvt + _dot_tn(dom, p.astype(BF16))
                new_dkt.append(dkt[hh] + _dot_tn(qv, ds))
                new_cols.append(col)
                dq_ref[rows, LANES * hh:LANES * (hh + 1)] += _dot(ds, kv)
            return tuple(new_dkt), dvt, tuple(new_cols)

        zt = jnp.zeros((LANES, tq), F32)
        zc = jnp.zeros((1, tq), F32)
        carry = step(kj, ((zt, zt), zt, (zc, zc)), True)
        dkt, dvt, cols = lax.fori_loop(kj + 1, nq, functools.partial(step, masked=False), carry)
        row_t = lax.broadcasted_iota(jnp.int32, (LANES, tq), 0)
        for hh in range(2):
            dk_h = jnp.where(row_t == 64, -cols[hh], dkt[hh]) if key_bias else dkt[hh]
            dk_ref[:, LANES * hh:LANES * (hh + 1)] = dk_h.T
        dv_ref[...] = dvt.T

        if key_bias:
            @pl.when(kj == nq - 1)
            def _():
                for hh in range(2):
                    blk = dq_ref[:, LANES * hh:LANES * (hh + 1)]
                    dq_ref[:, LANES * hh:LANES * (hh + 1)] = jnp.where(lane_s == 64, rsum[hh], blk)

        if ex:
            @pl.when((b == nb - 1) & (pr == PAIRS - 1) & (kj == nq - 1))
            def _():
                ex.wait(*ex_refs)

    per_seq = lambda w: pl.BlockSpec((seq, w), lambda b, p, j: (b, p))
    per_blk = lambda w: pl.BlockSpec((tq, w), lambda b, p, j: (b * nq + j, p))
    res = pl.pallas_call(
        body, name=name, grid=(nb, PAIRS, nq),
        out_shape=(jax.ShapeDtypeStruct((T, 1024), F32), jax.ShapeDtypeStruct((T, 1024), F32), jax.ShapeDtypeStruct((T, 512), F32))
        + tuple(ex.outs if ex else ()),
        in_specs=[per_seq(2 * LANES), per_blk(2 * LANES), per_blk(LANES), per_seq(LANES), per_seq(LANES), per_seq(2 * LANES)] + [ANY] * n_in,
        out_specs=[per_seq(2 * LANES), per_blk(2 * LANES), per_blk(LANES)] + [ANY] * n_out,
        scratch_shapes=[pltpu.VMEM((2, seq, LANES), F32), pltpu.VMEM((2, seq, LANES) if key_bias else (2, 8, LANES), F32)]
        + (ex.sems() if ex else []),
        compiler_params=_params(("arbitrary", "arbitrary", "arbitrary")),
    )(q, k, v, o, do, lse, *(ex.ins if ex else ()))
    return res[0], res[1], res[2], list(res[3:])


def _mid(of, om, x, tgt, g_fo, g_mo, g2, g3, w_o, w_g, w_u, w_d, tm):
    T = x.shape[0]

    def body(of_ref, om_ref, x_ref, t_ref, gfo_ref, gmo_ref, g2_ref, g3_ref, wo_ref, wg_ref, wu_ref, wd_ref,
             a_ref, h2_ref, hid_ref, dg_ref, du_ref, dx3_ref, dx2_ref, dof_ref, dom_ref, st_ref):
        i = pl.program_id(0)

        @pl.when(i == 0)
        def _():
            st_ref[...] = jnp.zeros_like(st_ref)

        ofv, omv = of_ref[...], om_ref[...]
        rf, rm = _rms(ofv, FOX_W), _rms(omv, FOX_W)
        fhat, mhat = ofv * rf, omv * rm
        a = jnp.concatenate([fhat * gfo_ref[...], mhat * gmo_ref[...]], axis=1).astype(BF16)
        a_ref[...] = a
        x2 = x_ref[...] + _dot(a, wo_ref[...])
        r2 = _rms(x2, D_MODEL)
        xh2 = x2 * r2
        h2 = (xh2 * g2_ref[...]).astype(BF16)
        h2_ref[...] = h2
        gt = _dot_nt(h2, wg_ref[...])
        up = _dot_nt(h2, wu_ref[...])
        sg = jax.nn.sigmoid(gt)
        sl = gt * sg
        hid = (sl * up).astype(BF16)
        hid_ref[...] = hid
        x3 = x2 + _dot(hid, wd_ref[...])
        r3 = _rms(x3, D_MODEL)
        xh3 = x3 * r3
        diff = xh3 * g3_ref[...] - t_ref[...]
        dy = diff * (1.0 / D_MODEL)
        st_ref[3:4, :] += jnp.sum(diff * diff, axis=0, keepdims=True) * (0.5 / D_MODEL)
        st_ref[0:1, :] += jnp.sum(dy * xh3, axis=0, keepdims=True)
        dx3 = _rms_bwd(dy, xh3, r3, g3_ref[...], D_MODEL)
        dx3b = dx3.astype(BF16)
        dx3_ref[...] = dx3b
        dhid = _dot_nt(dx3b, wd_ref[...])
        dg = (dhid * up * (sg * (1.0 + gt * (1.0 - sg)))).astype(BF16)
        du = (dhid * sl).astype(BF16)
        dg_ref[...] = dg
        du_ref[...] = du
        dh2 = _dot(dg, wg_ref[...]) + _dot(du, wu_ref[...])
        st_ref[1:2, :] += jnp.sum(dh2 * xh2, axis=0, keepdims=True)
        dx2 = dx3 + _rms_bwd(dh2, xh2, r2, g2_ref[...], D_MODEL)
        dx2_ref[...] = dx2
        da = _dot_nt(dx2.astype(BF16), wo_ref[...])
        daf, dam = da[:, 0:FOX_W], da[:, FOX_W:2 * FOX_W]
        st_ref[2:3, 0:FOX_W] += jnp.sum(daf * fhat, axis=0, keepdims=True)
        st_ref[2:3, FOX_W:2 * FOX_W] += jnp.sum(dam * mhat, axis=0, keepdims=True)
        dof_ref[...] = _rms_bwd(daf, fhat, rf, gfo_ref[...], FOX_W).astype(BF16)
        dom_ref[...] = _rms_bwd(dam, mhat, rm, gmo_ref[...], FOX_W).astype(BF16)

    row = lambda w: pl.BlockSpec((tm, w), lambda i: (i, 0))
    ff = jax.ShapeDtypeStruct((T, D_FF), BF16)
    out_shape = (
        jax.ShapeDtypeStruct((T, 1024), BF16), jax.ShapeDtypeStruct((T, 1024), BF16), ff, ff, ff,
        jax.ShapeDtypeStruct((T, 1024), BF16), jax.ShapeDtypeStruct((T, 1024), F32),
        jax.ShapeDtypeStruct((T, 512), BF16), jax.ShapeDtypeStruct((T, 512), BF16), jax.ShapeDtypeStruct((8, 1024), F32),
    )
    return pl.pallas_call(
        body, name="mid", grid=(T // tm,), out_shape=out_shape,
        in_specs=[row(512), row(512), row(1024), row(1024), _full(g_fo.shape), _full(g_mo.shape), _full(g2.shape), _full(g3.shape),
                  _full(w_o.shape), _full(w_g.shape), _full(w_u.shape), _full(w_d.shape)],
        out_specs=[row(1024), row(1024), row(D_FF), row(D_FF), row(D_FF), row(1024), row(1024), row(512), row(512),
                   pl.BlockSpec((8, 1024), lambda i: (0, 0))],
        compiler_params=_params(("arbitrary",)),
    )(of, om, x, tgt, g_fo, g_mo, g2, g3, w_o, w_g, w_u, w_d)


def _in_bwd(dqf, dkf, dvf, dqm, dkm, dvm, lat, x, dx2, g1, gq, gkv, bfg, ct, st, sel_t, w_in, w_q12, w_kv, seq, tm, ex=None):
    T = x.shape[0]
    nblk = T // tm
    nsb = seq // tm
    n_in, n_out = (len(ex.ins), len(ex.outs)) if ex else (0, 0)

    def body(*refs):
        (dqf_ref, dkf_ref, dvf_ref, dqm_ref, dkm_ref, dvm_ref, lat_ref, x_ref, dx2_ref, g1_ref, gq_ref, gkv_ref, b_ref,
         ct_ref, st_ref, selt_ref, win_ref, wq_ref, wkv_ref) = refs[0:19]
        dx_ref, dproj_ref, dq12_ref, dkv_ref, stat_ref = refs[19 + n_in:24 + n_in]
        carry = refs[24 + n_in + n_out]
        i = pl.program_id(0)
        if ex:
            ex_refs = (refs[19:19 + n_in], refs[24 + n_in:24 + n_in + n_out], refs[25 + n_in + n_out:])

            @pl.when(i == 0)
            def _():
                ex.start(*ex_refs)

        @pl.when(i == 0)
        def _():
            stat_ref[...] = jnp.zeros_like(stat_ref)

        @pl.when(i % nsb == 0)
        def _():
            carry[...] = jnp.zeros_like(carry)

        lane = lax.broadcasted_iota(jnp.int32, (tm, LANES), 1)
        low = lane < 64
        ctv, stv = ct_ref[...], st_ref[...]

        for j in range(PAIRS):
            e, o = 2 * LANES * j, 2 * LANES * j + LANES
            dq = jnp.where(low, dqf_ref[:, e:e + LANES], 0.0) + pltpu.roll(jnp.where(low, dqf_ref[:, o:o + LANES], 0.0), 64, 1)
            dk = jnp.where(low, dkf_ref[:, e:e + LANES], 0.0) + pltpu.roll(jnp.where(low, dkf_ref[:, o:o + LANES], 0.0), 64, 1)
            dproj_ref[:, C_FQ + LANES * j:C_FQ + LANES * (j + 1)] = (dq * FOX_SCALE).astype(BF16)
            dproj_ref[:, C_FK + LANES * j:C_FK + LANES * (j + 1)] = dk.astype(BF16)
        dproj_ref[:, C_FV:C_QL] = dvf_ref[...].astype(BF16)
        dcv = dkf_ref[...] + dqf_ref[...]
        k_hi = dcv.astype(BF16)
        k_lo = (dcv - k_hi.astype(F32)).astype(BF16)
        dc = _dot(k_hi, selt_ref[...]) + _dot(k_lo, selt_ref[...])
        rr = lax.broadcasted_iota(jnp.int32, (tm, tm), 0)
        cc = lax.broadcasted_iota(jnp.int32, (tm, tm), 1)
        triu = (cc >= rr).astype(BF16)
        a0, a1, a2 = _split3(dc)
        dlf = _dot(triu, a0) + _dot(triu, a1) + _dot(triu, a2) + carry[0:1, :]
        carry[0:1, :] = dlf[0:1, :]
        misc_a = lat_ref[:, Q_RANK + KV_RANK:Q_RANK + KV_RANK + LANES]
        z = misc_a + b_ref[...]
        dz = jnp.where(lane < HEADS, dlf * jax.nn.sigmoid(-z), 0.0)
        stat_ref[3:4, 0:LANES] += jnp.sum(dz, axis=0, keepdims=True)

        cq = (jnp.where(low, 1.0, 0.0) + ctv) * MLA_SCALE
        sq = stv * MLA_SCALE
        dkpe = jnp.zeros((tm, LANES), F32)
        for hd in range(HEADS):
            s0 = LANES * hd
            dqh = dqm_ref[:, s0:s0 + LANES]
            dq12_ref[:, s0:s0 + LANES] = (dqh * cq).astype(BF16)
            dq12_ref[:, 1024 + s0:1024 + s0 + LANES] = (dqh * sq).astype(BF16)
            dkpe = dkpe + dkm_ref[:, s0:s0 + LANES]
        dkv_ref[:, 0:1024] = dkm_ref[...].astype(BF16)
        dkv_ref[:, 1024:1536] = dvm_ref[...].astype(BF16)
        dproj_ref[:, C_MA:C_MB] = (dz + dkpe * ctv).astype(BF16)
        dproj_ref[:, C_MB:C_END] = (dkpe * stv).astype(BF16)
        dqn = _dot(dq12_ref[...], wq_ref[...])
        dkvn = _dot_nt(dkv_ref[...], wkv_ref[...])
        ql = lat_ref[:, 0:Q_RANK]
        kvl = lat_ref[:, Q_RANK:Q_RANK + KV_RANK]
        rq, rkv = _rms(ql, Q_RANK), _rms(kvl, KV_RANK)
        qhat, kvhat = ql * rq, kvl * rkv
        stat_ref[1:2, 0:Q_RANK] += jnp.sum(dqn * qhat, axis=0, keepdims=True)
        stat_ref[2:3, 0:KV_RANK] += jnp.sum(dkvn * kvhat, axis=0, keepdims=True)
        dproj_ref[:, C_QL:C_KVL] = _rms_bwd(dqn, qhat, rq, gq_ref[...], Q_RANK).astype(BF16)
        dproj_ref[:, C_KVL:C_MA] = _rms_bwd(dkvn, kvhat, rkv, gkv_ref[...], KV_RANK).astype(BF16)

        dh1 = _dot(dproj_ref[...], win_ref[...])
        xv = x_ref[...]
        r1 = _rms(xv, D_MODEL)
        xh = xv * r1
        stat_ref[0:1, :] += jnp.sum(dh1 * xh, axis=0, keepdims=True)
        dx_ref[...] = dx2_ref[...] + _rms_bwd(dh1, xh, r1, g1_ref[...], D_MODEL)

        if ex:
            @pl.when(i == nblk - 1)
            def _():
                ex.wait(*ex_refs)

    rev = lambda w: pl.BlockSpec((tm, w), lambda i: (nblk - 1 - i, 0))
    out_shape = (
        jax.ShapeDtypeStruct((T, 1024), F32), jax.ShapeDtypeStruct((T, C_END), BF16), jax.ShapeDtypeStruct((T, 2048), BF16),
        jax.ShapeDtypeStruct((T, 1536), BF16), jax.ShapeDtypeStruct((8, 1024), F32),
    ) + tuple(ex.outs if ex else ())
    res = pl.pallas_call(
        body, name="in_bwd", grid=(nblk,), out_shape=out_shape,
        in_specs=[rev(1024), rev(1024), rev(512), rev(1024), rev(1024), rev(512), rev(512), rev(1024), rev(1024),
                  _full(g1.shape), _full(gq.shape), _full(gkv.shape), _full(bfg.shape), rev(LANES), rev(LANES), _full(sel_t.shape),
                  _full(w_in.shape), _full(w_q12.shape), _full(w_kv.shape)] + [ANY] * n_in,
        out_specs=[rev(1024), rev(C_END), rev(2048), rev(1536), pl.BlockSpec((8, 1024), lambda i: (0, 0))] + [ANY] * n_out,
        scratch_shapes=[pltpu.VMEM((8, LANES), F32)] + (ex.sems() if ex else []),
        compiler_params=_params(("arbitrary",)),
    )(dqf, dkf, dvf, dqm, dkm, dvm, lat, x, dx2, g1, gq, gkv, bfg, ct, st, sel_t, w_in, w_q12, w_kv, *(ex.ins if ex else ()))
    return res[0], res[1], res[2], res[3], res[4], list(res[5:])


def _wgrad(a, b, tk, tt, name, ex=None):
    T, K = a.shape
    N = b.shape[1]
    n_in, n_out = (len(ex.ins), len(ex.outs)) if ex else (0, 0)
    gk, gt = K // tk, T // tt

    def body(*refs):
        a_ref, b_ref, o_ref = refs[0], refs[1], refs[2 + n_in]
        kb, t = pl.program_id(0), pl.program_id(1)
        if ex:
            ex_refs = (refs[2:2 + n_in], refs[3 + n_in:3 + n_in + n_out], refs[3 + n_in + n_out:])

            @pl.when((kb == 0) & (t == 0))
            def _():
                ex.start(*ex_refs)

        @pl.when(t == 0)
        def _():
            o_ref[...] = jnp.zeros_like(o_ref)

        o_ref[...] += _dot_tn(a_ref[...].astype(BF16), b_ref[...].astype(BF16))

        if ex:
            @pl.when((kb == gk - 1) & (t == gt - 1))
            def _():
                ex.wait(*ex_refs)

    res = pl.pallas_call(
        body, name=name, grid=(gk, gt), out_shape=(jax.ShapeDtypeStruct((K, N), F32),) + tuple(ex.outs if ex else ()),
        in_specs=[pl.BlockSpec((tt, tk), lambda kb, t: (t, kb)), pl.BlockSpec((tt, N), lambda kb, t: (t, 0))] + [ANY] * n_in,
        out_specs=[pl.BlockSpec((tk, N), lambda kb, t: (kb, 0))] + [ANY] * n_out,
        scratch_shapes=ex.sems() if ex else [], input_output_aliases=ex.aliases(2, 1) if ex else {},
        compiler_params=_params(("arbitrary", "arbitrary")),
    )(a, b, *(ex.ins if ex else ()))
    return (res[0], list(res[1:])) if ex else res[0]


def _adamw(w, g, m, v, name):
    R, C = w.shape
    tr = _row_tile(R)

    def body(w_ref, g_ref, m_ref, v_ref, d_ref, nm_ref, nv_ref):
        gv = g_ref[...]
        nm = ADAM_B1 * m_ref[...] + (1.0 - ADAM_B1) * gv
        nv = ADAM_B2 * v_ref[...] + (1.0 - ADAM_B2) * (gv * gv)
        m_hat = nm / (1.0 - ADAM_B1 ** ADAM_STEP)
        v_hat = nv / (1.0 - ADAM_B2 ** ADAM_STEP)
        d_ref[...] = -ADAM_LR * (m_hat / (jnp.sqrt(v_hat) + ADAM_EPS) + ADAM_WD * w_ref[...])
        nm_ref[...] = nm
        nv_ref[...] = nv

    blk = pl.BlockSpec((tr, C), lambda i: (i, 0))
    sh = jax.ShapeDtypeStruct((R, C), F32)
    return pl.pallas_call(
        body, name=name, grid=(R // tr,), out_shape=(sh, sh, sh),
        in_specs=[blk, blk, blk, blk], out_specs=[blk, blk, blk],
        compiler_params=_params(("arbitrary",)),
    )(w, g, m, v)


def _arrange(win_t, wuq_t, wukv):
    dt = win_t.dtype
    z = lambda r: jnp.zeros((r, D_MODEL), dt)
    zh = lambda r: jnp.zeros((HEADS, r, Q_RANK), dt)
    kr1, kr2 = win_t[1928:1944], win_t[1944:1960]
    misc_a = jnp.concatenate([win_t[1536:1544], z(56), kr1, kr2, z(32)], axis=0)
    misc_b = jnp.concatenate([z(64), kr2, kr1, z(32)], axis=0)
    w_in = jnp.concatenate([win_t[0:1536], win_t[1544:1928], misc_a, misc_b], axis=0)
    wq = wuq_t.reshape(HEADS, 96, Q_RANK)
    q1 = jnp.concatenate([wq, zh(32)], axis=1).reshape(1024, Q_RANK)
    q2 = jnp.concatenate([zh(64), wq[:, 80:96], wq[:, 64:80], zh(32)], axis=1).reshape(1024, Q_RANK)
    wkv = wukv.reshape(KV_RANK, HEADS, 128)
    wk = jnp.concatenate([wkv[:, :, 0:64], jnp.zeros((KV_RANK, HEADS, 64), dt)], axis=2).reshape(KV_RANK, 1024)
    wv = wkv[:, :, 64:128].reshape(KV_RANK, 512)
    return dict(w_in=w_in, w_q12=jnp.concatenate([q1, q2], axis=0), w_k=wk, w_v=wv, w_kv=jnp.concatenate([wk, wv], axis=1))


def _unarrange(g_in, g_q12, g_kv):
    kr1 = g_in[C_MA + 64:C_MA + 80] + g_in[C_MB + 80:C_MB + 96]
    kr2 = g_in[C_MA + 80:C_MA + 96] + g_in[C_MB + 64:C_MB + 80]
    win_t = jnp.concatenate([g_in[0:1536], g_in[C_MA:C_MA + 8], g_in[1536:1920], kr1, kr2], axis=0)
    g1 = g_q12[0:1024].reshape(HEADS, 128, Q_RANK)
    g2 = g_q12[1024:2048].reshape(HEADS, 128, Q_RANK)
    wuq_t = jnp.concatenate([g1[:, 0:64], g1[:, 64:80] + g2[:, 80:96], g1[:, 80:96] + g2[:, 64:80]], axis=1).reshape(768, Q_RANK)
    gk = g_kv[:, 0:1024].reshape(KV_RANK, HEADS, 128)
    gv = g_kv[:, 1024:1536].reshape(KV_RANK, HEADS, 64)
    wukv = jnp.concatenate([gk[:, :, 0:64], gv], axis=2).reshape(KV_RANK, 1024)
    return win_t, wuq_t, wukv


def _selectors():
    sel = np.zeros((384, 1024), np.float32)
    sel_t = np.zeros((1024, LANES), np.float32)
    for h in range(HEADS):
        for piece in range(3):
            sel[LANES * piece + h, LANES * h + 64 + piece] = 1.0
        sel_t[LANES * h + 64, h] = 1.0
    return jnp.asarray(sel, BF16), jnp.asarray(sel_t, BF16)


def _rope_tables(positions):
    inv_freq = 10000.0 ** (-jnp.arange(0, ROPE, 2, dtype=F32) / ROPE)
    ang = positions.reshape(-1).astype(F32)[:, None] * inv_freq[None, :]
    cos, sin = jnp.cos(ang), jnp.sin(ang)
    z64, z32 = jnp.zeros((ang.shape[0], 64), F32), jnp.zeros((ang.shape[0], 32), F32)
    return jnp.concatenate([z64, cos, cos, z32], axis=1), jnp.concatenate([z64, -sin, sin, z32], axis=1)


def _reduce_tail(slabs):
    got = _run_exchange(_swap_exchange(slabs), "tail_swap")
    sums = [_add_half(g, s) for g, s in zip(slabs, got)]
    recv = _run_exchange(_scatter_exchange(sums), "tail_scatter")
    return _run_exchange(_join_exchange([_sum_slabs(g, s, r) for g, s, r in zip(slabs, got, recv)]), "tail_join")


def _work(name, t):
    return jnp.swapaxes(t[0], 0, 1) if name in TRANSPOSED else t[0]


def _back(name, t):
    return (jnp.swapaxes(t, 0, 1) if name in TRANSPOSED else t)[None]


SMALL_ROWS = {"norm_mix_g": (0, 1024), "norm_ffn_g": (1, 1024), "final_norm_g": (2, 1024), "q_norm_g": (4, 256),
              "kv_norm_g": (5, 128), "b_fgate": (6, 8)}


def kernel(x, positions, norm_mix_g, w_in, b_fgate, q_norm_g, w_uq, kv_norm_g, w_ukv, fox_out_g, mla_out_g, w_o, norm_ffn_g, w_gate, w_up, w_down, final_norm_g, loss_target, m_norm_mix_g, m_w_in, m_b_fgate, m_q_norm_g, m_w_uq, m_kv_norm_g, m_w_ukv, m_fox_out_g, m_mla_out_g, m_w_o, m_norm_ffn_g, m_w_gate, m_w_up, m_w_down, m_final_norm_g, v_norm_mix_g, v_w_in, v_b_fgate, v_q_norm_g, v_w_uq, v_kv_norm_g, v_w_ukv, v_fox_out_g, v_mla_out_g, v_w_o, v_norm_ffn_g, v_w_gate, v_w_up, v_w_down, v_final_norm_g):
    names = ["norm_mix_g", "w_in", "b_fgate", "q_norm_g", "w_uq", "kv_norm_g", "w_ukv", "fox_out_g", "mla_out_g", "w_o",
             "norm_ffn_g", "w_gate", "w_up", "w_down", "final_norm_g"]
    wts = dict(zip(names, [norm_mix_g, w_in, b_fgate, q_norm_g, w_uq, kv_norm_g, w_ukv, fox_out_g, mla_out_g, w_o, norm_ffn_g,
                           w_gate, w_up, w_down, final_norm_g]))
    mom = dict(zip(names, [m_norm_mix_g, m_w_in, m_b_fgate, m_q_norm_g, m_w_uq, m_kv_norm_g, m_w_ukv, m_fox_out_g, m_mla_out_g,
                           m_w_o, m_norm_ffn_g, m_w_gate, m_w_up, m_w_down, m_final_norm_g]))
    var = dict(zip(names, [v_norm_mix_g, v_w_in, v_b_fgate, v_q_norm_g, v_w_uq, v_kv_norm_g, v_w_ukv, v_fox_out_g, v_mla_out_g,
                           v_w_o, v_norm_ffn_g, v_w_gate, v_w_up, v_w_down, v_final_norm_g]))
    shard = {n: _work(n, wts[n]) for n in HEAD3 + FFN4}
    nb, seq, _ = x.shape
    T = nb * seq
    tm, tq = min(ROW_TILE, seq), min(ATTN_TILE, seq)
    tt = min(WGRAD_TILE, T)
    xf = x.reshape(T, D_MODEL)
    tgt = loss_target.reshape(T, D_MODEL)
    chip = 2 * lax.axis_index("x") + lax.axis_index("y")

    mine = [shard[n].astype(BF16) for n in HEAD3]
    head = _run_exchange(_gather_split_exchange(mine), "gather_head")
    win4, wuq4, wukv4 = [lax.dynamic_update_slice(h, s[None], (chip, 0, 0)) for h, s in zip(head, mine)]
    a = _arrange(win4.reshape(-1, D_MODEL), wuq4.reshape(-1, Q_RANK), wukv4.transpose(1, 0, 2).reshape(KV_RANK, -1))
    sel, sel_t = _selectors()
    ct, st = _rope_tables(positions)
    bfg = jnp.concatenate([b_fgate, jnp.zeros((1, LANES - HEADS), F32)], axis=1)
    g1, gq, gkv = norm_mix_g, q_norm_g, kv_norm_g

    h1, qf, kf, vf, qm, km, vm, lat, qn, kvn = _in_proj(xf, g1, a["w_in"], a["w_q12"], a["w_k"], a["w_v"], gq, gkv, bfg, ct, st, sel, seq, tm)
    of, lse_f, gathered = _attn_fwd(qf, kf, vf, nb, seq, tq, "fox_fwd", _gather_exchange([shard[n].astype(BF16) for n in FFN4]))
    om, lse_m, _ = _attn_fwd(qm, km, vm, nb, seq, tq, "mla_fwd")
    wo4, wg4, wu4, wd4 = gathered
    a_cat, h2, hid, dg, du, dx3, dx2, dof, dom, st_mid = _mid(
        of, om, xf, tgt, fox_out_g, mla_out_g, norm_ffn_g, final_norm_g.reshape(1, D_MODEL),
        wo4.reshape(D_MODEL, D_MODEL), wg4.reshape(D_FF, D_MODEL), wu4.reshape(D_FF, D_MODEL), wd4.reshape(D_FF, D_MODEL), tm)

    slab = lambda g: g.reshape(N_CHIPS, g.shape[0] // N_CHIPS, g.shape[1])
    big = [slab(_wgrad(a_cat, dx2, D_MODEL, tt, "wgrad_o")), slab(_wgrad(dg, h2, D_FF // 2, tt, "wgrad_gate")),
           slab(_wgrad(du, h2, D_FF // 2, tt, "wgrad_up")), slab(_wgrad(hid, dx3, D_FF // 2, tt, "wgrad_down"))]
    dqf, dkf, dvf, got = _attn_bwd(qf, kf, vf, of, dof, lse_f, nb, seq, tq, "fox_bwd", True, _swap_exchange(big))
    sums = [_add_half(g, s) for g, s in zip(big, got)]
    dqm, dkm, dvm, recv = _attn_bwd(qm, km, vm, om, dom, lse_m, nb, seq, tq, "mla_bwd", False, _scatter_exchange(sums))
    halves = [_sum_slabs(g, s, r) for g, s, r in zip(big, got, recv)]
    dx, dproj, dq12, dkv, st_in, _ = _in_bwd(dqf, dkf, dvf, dqm, dkm, dvm, lat, xf, dx2, g1, gq, gkv, bfg, ct, st, sel_t,
                                             a["w_in"], a["w_q12"], a["w_kv"], seq, tm)
    g_in, joined = _wgrad(dproj, h1, C_END, tt, "wgrad_in", _join_exchange(halves))
    gshard = dict(zip(FFN4, joined))

    loss_row = jnp.concatenate([jnp.sum(st_mid[3:4, :], axis=1, keepdims=True), jnp.zeros((1, D_MODEL - 1), F32)], axis=1)
    stats = jnp.concatenate([st_in[0:1], st_mid[1:2], st_mid[0:1], st_mid[2:3], st_in[1:2], st_in[2:3], st_in[3:4], loss_row], axis=0)
    stats = _allreduce_small(stats)
    gwin_t, gwuq_t, gwukv = _unarrange(g_in, _wgrad(dq12, qn, 2048, tt, "wgrad_uq"), _wgrad(kvn, dkv, KV_RANK, tt, "wgrad_ukv"))
    tail = [slab(gwin_t), slab(gwuq_t), gwukv.reshape(KV_RANK, N_CHIPS, -1).transpose(1, 0, 2)]
    gshard.update(zip(HEAD3, _reduce_tail(tail)))

    grads, delta, new_m, new_v = {}, {}, {}, {}
    for n in HEAD3 + FFN4:
        grads[n] = _back(n, gshard[n])
        d, nm, nv = _adamw(shard[n], gshard[n], _work(n, mom[n]), _work(n, var[n]), "adamw_" + n)
        delta[n], new_m[n], new_v[n] = _back(n, d), _back(n, nm), _back(n, nv)
    sm_g = {n: stats[row:row + 1, 0:width] for n, (row, width) in SMALL_ROWS.items()}
    sm_g["fox_out_g"] = stats[3:4, 0:512]
    sm_g["mla_out_g"] = stats[3:4, 512:1024]
    pad = lambda t: jnp.pad(t.reshape(1, -1), ((0, 0), (0, 1024 - t.size)))
    stack = lambda d: jnp.concatenate([pad(d[n]) for n in SMALL], axis=0)
    sd, sm, sv = _adamw(stack(wts), stack(sm_g), stack(mom), stack(var), "adamw_small")
    for i, n in enumerate(SMALL):
        shp = wts[n].shape
        grads[n] = sm_g[n].reshape(shp)
        delta[n] = sd[i, 0:wts[n].size].reshape(shp)
        new_m[n] = sm[i, 0:wts[n].size].reshape(shp)
        new_v[n] = sv[i, 0:wts[n].size].reshape(shp)
    loss = stats[7, 0]
    return (loss, dx.reshape(x.shape), *[grads[n] for n in names], *[delta[n] for n in names],
            *[new_m[n] for n in names], *[new_v[n] for n in names])
```

```python
import functools

import numpy as np
import jax
import jax.numpy as jnp
from jax import lax
from jax.experimental import pallas as pl
from jax.experimental.pallas import tpu as pltpu

F32 = jnp.float32
BF16 = jnp.bfloat16
MESH = pl.DeviceIdType.MESH

EPS = 1e-6
D_MODEL = 1024
HEADS = 8
PAIRS = HEADS // 2
FOX_W = 512
Q_RANK = 256
KV_RANK = 128
ROPE = 32
D_FF = 2816
N_CHIPS = 4
FOX_SCALE = 64 ** -0.5
MLA_SCALE = 96 ** -0.5
LANES = 128
NEG = -1e30

ADAM_LR, ADAM_B1, ADAM_B2, ADAM_EPS, ADAM_WD, ADAM_STEP = 0.001, 0.9, 0.999, 1e-08, 0.01, 10

C_FQ, C_FK, C_FV, C_QL, C_KVL, C_MA, C_MB, C_END = 0, 512, 1024, 1536, 1792, 1920, 2048, 2176

VMEM_LIMIT = 60 * 1024 * 1024
ROW_TILE = 256
ATTN_TILE = 512
WGRAD_TILE = 2048

HEAD3 = ("w_in", "w_uq", "w_ukv")
FFN4 = ("w_o", "w_gate", "w_up", "w_down")
TRANSPOSED = ("w_in", "w_uq", "w_gate", "w_up")
SMALL = ("norm_mix_g", "b_fgate", "q_norm_g", "kv_norm_g", "fox_out_g", "mla_out_g", "norm_ffn_g", "final_norm_g")


def _params(sem=None):
    return pltpu.CompilerParams(dimension_semantics=sem, vmem_limit_bytes=VMEM_LIMIT)


def _full(shape):
    n = len(shape)
    return pl.BlockSpec(shape, lambda *_: (0,) * n, pipeline_mode=pl.Buffered(1))


def _dot(a, b):
    return jnp.dot(a, b, preferred_element_type=F32)


def _dot_nt(a, b):
    return lax.dot_general(a, b, (((1,), (1,)), ((), ())), preferred_element_type=F32)


def _dot_tn(a, b):
    return lax.dot_general(a, b, (((0,), (0,)), ((), ())), preferred_element_type=F32)


def _split3(v):
    hi = v.astype(BF16)
    r1 = v - hi.astype(F32)
    mid = r1.astype(BF16)
    lo = (r1 - mid.astype(F32)).astype(BF16)
    return hi, mid, lo


def _rms(v, width):
    return lax.rsqrt(jnp.sum(v * v, axis=1, keepdims=True) * (1.0 / width) + EPS)


def _rms_bwd(dy, xhat, r, g, width):
    u = dy * g
    return r * (u - xhat * (jnp.sum(u * xhat, axis=1, keepdims=True) * (1.0 / width)))


ANY = pl.BlockSpec(memory_space=pl.ANY)


def _place():
    return lax.axis_index("x"), lax.axis_index("y"), lax.axis_index("c")


def _other_chips(x, y):
    return [(1 - x, y), (x, 1 - y), (1 - x, 1 - y)]


def _remote(src, dst, send, recv, j, dev):
    return pltpu.make_async_remote_copy(src_ref=src, dst_ref=dst, send_sem=send.at[j], recv_sem=recv.at[j], device_id=dev, device_id_type=MESH)


class _Exchange:
    def __init__(self, ins, outs, n_remote, n_local, build, in_place=False):
        self.ins, self.outs, self.n_remote, self.n_local, self.build = list(ins), list(outs), n_remote, max(n_local, 1), build
        self.in_place = in_place

    def aliases(self, first_in, first_out):
        return {first_in + i: first_out + i for i in range(len(self.ins))} if self.in_place else {}

    def sems(self):
        return [pltpu.SemaphoreType.DMA((self.n_remote,)), pltpu.SemaphoreType.DMA((self.n_remote,)), pltpu.SemaphoreType.DMA((self.n_local,))]

    def start(self, in_refs, out_refs, sems):
        for cp in self.build(in_refs, out_refs, *sems)[0]:
            cp.start()

    def wait(self, in_refs, out_refs, sems):
        for w in self.build(in_refs, out_refs, *sems)[1]:
            w()


def _gather_exchange(shards, own=True):
    def build(ins, outs, send, recv, lsem):
        x, y, c = _place()
        starts, waits = [], []
        for i, (s, o) in enumerate(zip(ins, outs)):
            if own:
                mine = pltpu.make_async_copy(s, o.at[2 * x + y], lsem.at[i])
                starts.append(mine)
                waits.append(mine.wait)
            for j, (cx, cy) in enumerate(_other_chips(x, y)):
                out = _remote(s, o.at[2 * x + y], send, recv, 3 * i + j, (cx, cy, c))
                starts.append(out)
                waits.append(_remote(s, o.at[2 * cx + cy], send, recv, 3 * i + j, (cx, cy, c)).wait_recv)
                waits.append(out.wait_send)
        return starts, waits

    outs = [jax.ShapeDtypeStruct((N_CHIPS,) + s.shape, s.dtype) for s in shards]
    return _Exchange(shards, outs, 3 * len(shards), len(shards), build)


def _gather_split_exchange(shards):
    n = len(shards)

    def build(ins, outs, send, recv, lsem):
        x, y, c = _place()
        starts, waits, last = [], [], []
        for i, (s, o) in enumerate(zip(ins, outs)):
            hc = s.shape[1] // 2
            mine, other = pl.ds(c * hc, hc), pl.ds((1 - c) * hc, hc)
            for j, (cx, cy) in enumerate(_other_chips(x, y)):
                out = _remote(s.at[:, mine], o.at[2 * x + y, :, mine], send, recv, 3 * i + j, (cx, cy, c))
                landed = o.at[2 * cx + cy, :, mine]
                arrive = _remote(s.at[:, mine], landed, send, recv, 3 * i + j, (cx, cy, c))
                onward = _remote(landed, landed, send, recv, 3 * n + 3 * i + j, (x, y, 1 - c))
                from_sibling = _remote(landed, o.at[2 * cx + cy, :, other], send, recv, 3 * n + 3 * i + j, (x, y, 1 - c))
                starts.append(out)
                waits.append(lambda arrive=arrive, onward=onward: (arrive.wait_recv(), onward.start()))
                last += [from_sibling.wait_recv, onward.wait_send, out.wait_send]
        return starts, waits + last

    outs = [jax.ShapeDtypeStruct((N_CHIPS,) + s.shape, s.dtype) for s in shards]
    return _Exchange(shards, outs, 6 * n, 0, build)


def _swap_exchange(grads):
    def build(ins, outs, send, recv, lsem):
        x, y, c = _place()
        cps = []
        for i, (g, o) in enumerate(zip(ins, outs)):
            hc = g.shape[2] // 2
            cps.append(_remote(g.at[:, :, pl.ds((1 - c) * hc, hc)], o, send, recv, i, (x, y, 1 - c)))
        return cps, [cp.wait for cp in cps]

    outs = [jax.ShapeDtypeStruct((g.shape[0], g.shape[1], g.shape[2] // 2), g.dtype) for g in grads]
    return _Exchange(grads, outs, len(grads), 0, build)


def _scatter_exchange(sums):
    def build(ins, outs, send, recv, lsem):
        x, y, c = _place()
        cps = []
        for i, (s, o) in enumerate(zip(ins, outs)):
            for j, (cx, cy) in enumerate(_other_chips(x, y)):
                cps.append(_remote(s.at[2 * cx + cy], o.at[j], send, recv, 3 * i + j, (cx, cy, c)))
        return cps, [cp.wait for cp in cps]

    outs = [jax.ShapeDtypeStruct((3,) + s.shape[1:], s.dtype) for s in sums]
    return _Exchange(sums, outs, 3 * len(sums), 0, build)


def _join_exchange(bufs):
    def build(ins, outs, send, recv, lsem):
        x, y, c = _place()
        starts, waits = [], []
        for i, (t, o) in enumerate(zip(ins, outs)):
            hc = t.shape[1] // 2
            out = _remote(t.at[:, pl.ds(c * hc, hc)], o.at[:, pl.ds(c * hc, hc)], send, recv, i, (x, y, 1 - c))
            starts.append(out)
            waits += [_remote(t.at[:, pl.ds(c * hc, hc)], o.at[:, pl.ds((1 - c) * hc, hc)], send, recv, i, (x, y, 1 - c)).wait_recv,
                      out.wait_send]
        return starts, waits

    outs = [jax.ShapeDtypeStruct(t.shape, t.dtype) for t in bufs]
    return _Exchange(bufs, outs, len(bufs), 0, build, in_place=True)


def _run_exchange(ex, name):
    n_in, n_out = len(ex.ins), len(ex.outs)

    def body(*refs):
        ins, outs, sems = refs[:n_in], refs[n_in:n_in + n_out], refs[n_in + n_out:]
        ex.start(ins, outs, sems)
        ex.wait(ins, outs, sems)

    return pl.pallas_call(
        body, name=name, out_shape=tuple(ex.outs), in_specs=[ANY] * n_in, out_specs=tuple([ANY] * n_out),
        scratch_shapes=ex.sems(), input_output_aliases=ex.aliases(0, 0),
        compiler_params=pltpu.CompilerParams(has_side_effects=True),
    )(*ex.ins)


def _allreduce_small(v):
    def body(v_ref, o_ref, buf, send, recv):
        x, y, c = _place()
        me = 4 * x + 2 * y + c
        buf[me] = v_ref[...]
        out = []
        for j in range(7):
            fx, fy, fc = (j + 1) >> 2 & 1, (j + 1) >> 1 & 1, (j + 1) & 1
            out.append(_remote(v_ref, buf.at[me], send, recv, j, (x ^ fx, y ^ fy, c ^ fc)))
        for cp in out:
            cp.start()
        for j in range(7):
            fx, fy, fc = (j + 1) >> 2 & 1, (j + 1) >> 1 & 1, (j + 1) & 1
            src = 4 * (x ^ fx) + 2 * (y ^ fy) + (c ^ fc)
            _remote(v_ref, buf.at[src], send, recv, j, (x ^ fx, y ^ fy, c ^ fc)).wait_recv()
        for cp in out:
            cp.wait_send()
        acc = buf[0]
        for d in range(1, 8):
            acc = acc + buf[d]
        o_ref[...] = acc

    vm = pl.BlockSpec(memory_space=pltpu.VMEM)
    return pl.pallas_call(
        body, name="allreduce_small", out_shape=jax.ShapeDtypeStruct(v.shape, v.dtype),
        in_specs=[vm], out_specs=vm,
        scratch_shapes=[pltpu.VMEM((8,) + v.shape, v.dtype), pltpu.SemaphoreType.DMA((7,)), pltpu.SemaphoreType.DMA((7,))],
        compiler_params=pltpu.CompilerParams(has_side_effects=True),
    )(v)


def _add_half(g, got):
    n, R, C = g.shape
    hc = C // 2

    def body(c_ref, g_ref, r_ref, o_ref):
        o_ref[...] = (g_ref[...] + r_ref[...]).astype(BF16)

    c = lax.axis_index("c")
    return pl.pallas_call(
        body, name="add_half",
        grid_spec=pltpu.PrefetchScalarGridSpec(
            num_scalar_prefetch=1, grid=(n,),
            in_specs=[pl.BlockSpec((1, R, hc), lambda k, c_ref: (k, 0, c_ref[0])),
                      pl.BlockSpec((1, R, hc), lambda k, c_ref: (k, 0, 0))],
            out_specs=pl.BlockSpec((1, R, hc), lambda k, c_ref: (k, 0, 0))),
        out_shape=jax.ShapeDtypeStruct((n, R, hc), BF16),
        compiler_params=_params(("arbitrary",)),
    )(jnp.reshape(c, (1,)).astype(jnp.int32), g, got)


def _sum_slabs(g, got, recv):
    n, R, C = g.shape
    hc = C // 2

    def body(kc_ref, g_ref, s_ref, r_ref, o_ref):
        o_ref[...] = (((g_ref[0] + s_ref[0]) + r_ref[0].astype(F32)) + r_ref[1].astype(F32)) + r_ref[2].astype(F32)

    kc = jnp.stack([2 * lax.axis_index("x") + lax.axis_index("y"), lax.axis_index("c")]).astype(jnp.int32)
    return pl.pallas_call(
        body, name="sum_slabs",
        grid_spec=pltpu.PrefetchScalarGridSpec(
            num_scalar_prefetch=1, grid=(1,),
            in_specs=[pl.BlockSpec((1, R, hc), lambda i, kc_ref: (kc_ref[0], 0, kc_ref[1])),
                      pl.BlockSpec((1, R, hc), lambda i, kc_ref: (kc_ref[0], 0, 0)),
                      pl.BlockSpec((3, R, hc), lambda i, kc_ref: (0, 0, 0))],
            out_specs=pl.BlockSpec((R, hc), lambda i, kc_ref: (0, kc_ref[1]))),
        out_shape=jax.ShapeDtypeStruct((R, C), F32),
        compiler_params=_params(("arbitrary",)),
    )(kc, g, got, recv)


def _row_tile(rows):
    for cand in (256, 184, 176, 144, 128, 64, 32, 16, 8):
        if rows % cand == 0:
            return cand
    return rows


def _in_proj(x, g1, w_in, w_q12, w_k, w_v, gq, gkv, bfg, ct, st, sel, seq, tm):
    T = x.shape[0]
    nsb = seq // tm

    def body(x_ref, g1_ref, win_ref, wq_ref, wk_ref, wv_ref, gq_ref, gkv_ref, b_ref, ct_ref, st_ref, sel_ref,
             h1_ref, qf_ref, kf_ref, vf_ref, qm_ref, km_ref, vm_ref, lat_ref, qn_ref, kvn_ref, carry):
        i = pl.program_id(0)

        @pl.when(i % nsb == 0)
        def _():
            carry[...] = jnp.zeros_like(carry)

        xv = x_ref[...]
        h = (xv * _rms(xv, D_MODEL) * g1_ref[...]).astype(BF16)
        h1_ref[...] = h
        proj = _dot_nt(h, win_ref[...])
        lane = lax.broadcasted_iota(jnp.int32, (tm, LANES), 1)
        low = lane < 64
        misc_a = proj[:, C_MA:C_MB]
        misc_b = proj[:, C_MB:C_END]

        z = misc_a + b_ref[...]
        lf = jnp.where(lane < HEADS, jnp.minimum(z, 0.0) - jnp.log1p(jnp.exp(-jnp.abs(z))), 0.0)
        rr = lax.broadcasted_iota(jnp.int32, (tm, tm), 0)
        cc = lax.broadcasted_iota(jnp.int32, (tm, tm), 1)
        tri = (rr >= cc).astype(BF16)
        a0, a1, a2 = _split3(lf)
        c = _dot(tri, a0) + _dot(tri, a1) + _dot(tri, a2) + carry[0:1, :]
        carry[0:1, :] = c[tm - 1:tm, :]
        c0, c1, c2 = _split3(c)
        cpl = _dot(jnp.concatenate([c0, c1, c2], axis=1), sel_ref[...])
        qpad = jnp.where((lane >= 64) & (lane < 67), -1.0, 0.0)
        for j in range(PAIRS):
            qc = proj[:, C_FQ + LANES * j:C_FQ + LANES * (j + 1)] * FOX_SCALE
            kc = proj[:, C_FK + LANES * j:C_FK + LANES * (j + 1)]
            e, o = 2 * LANES * j, 2 * LANES * j + LANES
            qf_ref[:, e:e + LANES] = jnp.where(low, qc, qpad).astype(BF16)
            qf_ref[:, o:o + LANES] = jnp.where(low, pltpu.roll(qc, 64, 1), qpad).astype(BF16)
            kf_ref[:, e:e + LANES] = jnp.where(low, kc, cpl[:, e:e + LANES]).astype(BF16)
            kf_ref[:, o:o + LANES] = jnp.where(low, pltpu.roll(kc, 64, 1), cpl[:, o:o + LANES]).astype(BF16)
        vf_ref[...] = proj[:, C_FV:C_QL].astype(BF16)

        ql = proj[:, C_QL:C_KVL]
        kvl = proj[:, C_KVL:C_MA]
        qn = (ql * _rms(ql, Q_RANK) * gq_ref[...]).astype(BF16)
        kvn = (kvl * _rms(kvl, KV_RANK) * gkv_ref[...]).astype(BF16)
        lat_ref[...] = proj[:, C_QL:C_MB]
        qn_ref[...] = qn
        kvn_ref[...] = kvn
        q12 = _dot_nt(qn, wq_ref[...])
        kn = _dot(kvn, wk_ref[...])
        ctv = ct_ref[...]
        stv = st_ref[...]
        cq = (jnp.where(low, 1.0, 0.0) + ctv) * MLA_SCALE
        sq = stv * MLA_SCALE
        kpe = misc_a * ctv + misc_b * stv
        for hd in range(HEADS):
            s0 = LANES * hd
            qm_ref[:, s0:s0 + LANES] = (q12[:, s0:s0 + LANES] * cq + q12[:, 1024 + s0:1024 + s0 + LANES] * sq).astype(BF16)
            km_ref[:, s0:s0 + LANES] = (kn[:, s0:s0 + LANES] + kpe).astype(BF16)
        vm_ref[...] = _dot(kvn, wv_ref[...]).astype(BF16)

    row = lambda w: pl.BlockSpec((tm, w), lambda i: (i, 0))
    out_shape = (
        jax.ShapeDtypeStruct((T, D_MODEL), BF16),
        jax.ShapeDtypeStruct((T, 1024), BF16), jax.ShapeDtypeStruct((T, 1024), BF16), jax.ShapeDtypeStruct((T, 512), BF16),
        jax.ShapeDtypeStruct((T, 1024), BF16), jax.ShapeDtypeStruct((T, 1024), BF16), jax.ShapeDtypeStruct((T, 512), BF16),
        jax.ShapeDtypeStruct((T, 512), F32),
        jax.ShapeDtypeStruct((T, Q_RANK), BF16), jax.ShapeDtypeStruct((T, KV_RANK), BF16),
    )
    return pl.pallas_call(
        body, name="in_proj", grid=(T // tm,), out_shape=out_shape,
        in_specs=[row(D_MODEL), _full(g1.shape), _full(w_in.shape), _full(w_q12.shape), _full(w_k.shape), _full(w_v.shape),
                  _full(gq.shape), _full(gkv.shape), _full(bfg.shape), row(LANES), row(LANES), _full(sel.shape)],
        out_specs=[row(D_MODEL), row(1024), row(1024), row(512), row(1024), row(1024), row(512), row(512), row(Q_RANK), row(KV_RANK)],
        scratch_shapes=[pltpu.VMEM((8, LANES), F32)],
        compiler_params=_params(("arbitrary",)),
    )(x, g1, w_in, w_q12, w_k, w_v, gq, gkv, bfg, ct, st, sel)


def _attn_fwd(q, k, v, nb, seq, tq, name, ex=None):
    T = q.shape[0]
    nq = seq // tq
    n_in, n_out = (len(ex.ins), len(ex.outs)) if ex else (0, 0)

    def body(*refs):
        q_ref, k_ref, v_ref = refs[0:3]
        o_ref, lse_ref = refs[3 + n_in:5 + n_in]
        b, pr, qi = pl.program_id(0), pl.program_id(1), pl.program_id(2)
        if ex:
            ex_refs = (refs[3:3 + n_in], refs[5 + n_in:5 + n_in + n_out], refs[8 + n_in + n_out:])

            @pl.when((b == 0) & (pr == 0) & (qi == 0))
            def _():
                ex.start(*ex_refs)

        s_sc, p_sc, acc_sc = refs[5 + n_in + n_out:8 + n_in + n_out]
        strip = 64
        key_s = lax.broadcasted_iota(jnp.int32, (strip, tq), 0)
        qry_s = lax.broadcasted_iota(jnp.int32, (strip, tq), 1)
        row_t = lax.broadcasted_iota(jnp.int32, (LANES, tq), 0)
        acc_sc[...] = jnp.zeros(acc_sc.shape, F32)

        def fold(x, op):
            out = x[0:8]
            for r in range(8, strip, 8):
                out = op(out, x[r:r + 8])
            return out

        def step(kj, state, masked):
            rows = pl.ds(pl.multiple_of(kj * tq, tq), tq)
            for hh in range(2):
                s_sc[hh] = _dot_nt(k_ref[rows, LANES * hh:LANES * (hh + 1)], q_ref[:, LANES * hh:LANES * (hh + 1)])
            vv = v_ref[rows, :]
            new = []
            for hh in range(2):
                m, l = state[hh]

                def strip_of(r0, hh=hh):
                    s = s_sc[hh, r0:r0 + strip, :]
                    return jnp.where(key_s + r0 <= qry_s, s, NEG) if masked else s

                mx = fold(strip_of(0), jnp.maximum)
                for r0 in range(strip, tq, strip):
                    mx = jnp.maximum(mx, fold(strip_of(r0), jnp.maximum))
                m_new = jnp.maximum(m, jnp.max(mx, axis=0, keepdims=True))
                alpha = jnp.exp(m - m_new)
                sm = jnp.zeros((8, tq), F32)
                for r0 in range(0, tq, strip):
                    p = jnp.exp(strip_of(r0) - m_new)
                    sm = sm + fold(p, jnp.add)
                    p_sc[hh, r0:r0 + strip, :] = p.astype(BF16)
                l = alpha * l + jnp.sum(sm, axis=0, keepdims=True)
                acc_sc[hh] = alpha * acc_sc[hh] + _dot_tn(vv, p_sc[hh])
                new.append((m_new, l))
            return tuple(new)

        one = (jnp.full((1, tq), NEG, F32), jnp.zeros((1, tq), F32))
        state = lax.fori_loop(0, qi, functools.partial(step, masked=False), (one, one))
        (m0, l0), (m1, l1) = step(qi, state, True)
        o_ref[...] = jnp.where(row_t < 64, acc_sc[0] / l0, acc_sc[1] / l1).T
        lse_ref[:, 0:LANES] = jnp.broadcast_to(m0 + jnp.log(l0), (LANES, tq)).T
        lse_ref[:, LANES:2 * LANES] = jnp.broadcast_to(m1 + jnp.log(l1), (LANES, tq)).T

        if ex:
            @pl.when((b == nb - 1) & (pr == PAIRS - 1) & (qi == nq - 1))
            def _():
                ex.wait(*ex_refs)

    res = pl.pallas_call(
        body, name=name, grid=(nb, PAIRS, nq),
        out_shape=(jax.ShapeDtypeStruct((T, 512), F32), jax.ShapeDtypeStruct((T, 1024), F32)) + tuple(ex.outs if ex else ()),
        in_specs=[pl.BlockSpec((tq, 2 * LANES), lambda b, p, i: (b * nq + i, p)),
                  pl.BlockSpec((seq, 2 * LANES), lambda b, p, i: (b, p)),
                  pl.BlockSpec((seq, LANES), lambda b, p, i: (b, p))] + [ANY] * n_in,
        out_specs=[pl.BlockSpec((tq, LANES), lambda b, p, i: (b * nq + i, p)),
                   pl.BlockSpec((tq, 2 * LANES), lambda b, p, i: (b * nq + i, p))] + [ANY] * n_out,
        scratch_shapes=[pltpu.VMEM((2, tq, tq), F32), pltpu.VMEM((2, tq, tq), BF16), pltpu.VMEM((2, LANES, tq), F32)]
        + (ex.sems() if ex else []),
        compiler_params=_params(("arbitrary", "arbitrary", "arbitrary")),
    )(q, k, v, *(ex.ins if ex else ()))
    return res[0], res[1], list(res[2:])


def _attn_bwd(q, k, v, o, do, lse, nb, seq, tq, name, key_bias, ex=None):
    T = q.shape[0]
    nq = seq // tq
    n_in, n_out = (len(ex.ins), len(ex.outs)) if ex else (0, 0)

    def body(*refs):
        q_ref, k_ref, v_ref, o_ref, do_ref, lse_ref = refs[0:6]
        dq_ref, dk_ref, dv_ref = refs[6 + n_in:9 + n_in]
        dsc, rsum = refs[9 + n_in + n_out:11 + n_in + n_out]
        b, pr, kj = pl.program_id(0), pl.program_id(1), pl.program_id(2)
        if ex:
            ex_refs = (refs[6:6 + n_in], refs[9 + n_in:9 + n_in + n_out], refs[11 + n_in + n_out:])

            @pl.when((b == 0) & (pr == 0) & (kj == 0))
            def _():
                ex.start(*ex_refs)

        lane_s = lax.broadcasted_iota(jnp.int32, (seq, LANES), 1)
        lane = lax.broadcasted_iota(jnp.int32, (tq, LANES), 1)
        rr = lax.broadcasted_iota(jnp.int32, (tq, tq), 0)
        cc = lax.broadcasted_iota(jnp.int32, (tq, tq), 1)

        @pl.when(kj == 0)
        def _():
            dq_ref[...] = jnp.zeros_like(dq_ref)
            prod = do_ref[...].astype(F32) * o_ref[...]
            d0 = jnp.sum(jnp.where(lane_s < 64, prod, 0.0), axis=1, keepdims=True)
            d1 = jnp.sum(jnp.where(lane_s < 64, 0.0, prod), axis=1, keepdims=True)
            dsc[0] = jnp.broadcast_to(d0, (seq, LANES))
            dsc[1] = jnp.broadcast_to(d1, (seq, LANES))
            if key_bias:
                rsum[...] = jnp.zeros_like(rsum)

        vv = v_ref[...]

        def step(qi, carry, masked):
            dkt, dvt, cols = carry
            rows = pl.ds(pl.multiple_of(qi * tq, tq), tq)
            dov = do_ref[rows, :]
            new_dkt, new_cols = [], []
            for hh in range(2):
                qv = q_ref[rows, LANES * hh:LANES * (hh + 1)]
                kv = k_ref[:, LANES * hh:LANES * (hh + 1)]
                dom = jnp.where((lane < 64) if hh == 0 else (lane >= 64), dov, jnp.zeros((), BF16))
                s = _dot_nt(qv, kv)
                if masked:
                    s = jnp.where(cc <= rr, s, NEG)
                p = jnp.exp(s - lse_ref[rows, LANES * hh:LANES * hh + 1])
                dp = _dot_nt(dom, vv)
                ds32 = p * (dp - dsc[hh, rows, 0:1])
                col = cols[hh]
                if key_bias:
                    col = col + jnp.sum(ds32, axis=0, keepdims=True)
                    rsum[hh, rows, :] += jnp.broadcast_to(jnp.sum(ds32, axis=1, keepdims=True), (tq, LANES))
                ds = ds32.astype(BF16)
                dvt = dvt + _dot_tn(dom, p.astype(BF16))
                new_dkt.append(dkt[hh] + _dot_tn(qv, ds))
                new_cols.append(col)
                dq_ref[rows, LANES * hh:LANES * (hh + 1)] += _dot(ds, kv)
            return tuple(new_dkt), dvt, tuple(new_cols)

        zt = jnp.zeros((LANES, tq), F32)
        zc = jnp.zeros((1, tq), F32)
        carry = step(kj, ((zt, zt), zt, (zc, zc)), True)
        dkt, dvt, cols = lax.fori_loop(kj + 1, nq, functools.partial(step, masked=False), carry)
        row_t = lax.broadcasted_iota(jnp.int32, (LANES, tq), 0)
        for hh in range(2):
            dk_h = jnp.where(row_t == 64, -cols[hh], dkt[hh]) if key_bias else dkt[hh]
            dk_ref[:, LANES * hh:LANES * (hh + 1)] = dk_h.T
        dv_ref[...] = dvt.T

        if key_bias:
            @pl.when(kj == nq - 1)
            def _():
                for hh in range(2):
                    blk = dq_ref[:, LANES * hh:LANES * (hh + 1)]
                    dq_ref[:, LANES * hh:LANES * (hh + 1)] = jnp.where(lane_s == 64, rsum[hh], blk)

        if ex:
            @pl.when((b == nb - 1) & (pr == PAIRS - 1) & (kj == nq - 1))
            def _():
                ex.wait(*ex_refs)

    per_seq = lambda w: pl.BlockSpec((seq, w), lambda b, p, j: (b, p))
    per_blk = lambda w: pl.BlockSpec((tq, w), lambda b, p, j: (b * nq + j, p))
    res = pl.pallas_call(
        body, name=name, grid=(nb, PAIRS, nq),
        out_shape=(jax.ShapeDtypeStruct((T, 1024), F32), jax.ShapeDtypeStruct((T, 1024), F32), jax.ShapeDtypeStruct((T, 512), F32))
        + tuple(ex.outs if ex else ()),
        in_specs=[per_seq(2 * LANES), per_blk(2 * LANES), per_blk(LANES), per_seq(LANES), per_seq(LANES), per_seq(2 * LANES)] + [ANY] * n_in,
        out_specs=[per_seq(2 * LANES), per_blk(2 * LANES), per_blk(LANES)] + [ANY] * n_out,
        scratch_shapes=[pltpu.VMEM((2, seq, LANES), F32), pltpu.VMEM((2, seq, LANES) if key_bias else (2, 8, LANES), F32)]
        + (ex.sems() if ex else []),
        compiler_params=_params(("arbitrary", "arbitrary", "arbitrary")),
    )(q, k, v, o, do, lse, *(ex.ins if ex else ()))
    return res[0], res[1], res[2], list(res[3:])


def _mid(of, om, x, tgt, g_fo, g_mo, g2, g3, w_o, w_g, w_u, w_d, tm):
    T = x.shape[0]

    def body(of_ref, om_ref, x_ref, t_ref, gfo_ref, gmo_ref, g2_ref, g3_ref, wo_ref, wg_ref, wu_ref, wd_ref,
             a_ref, h2_ref, hid_ref, dg_ref, du_ref, dx3_ref, dx2_ref, dof_ref, dom_ref, st_ref):
        i = pl.program_id(0)

        @pl.when(i == 0)
        def _():
            st_ref[...] = jnp.zeros_like(st_ref)

        ofv, omv = of_ref[...], om_ref[...]
        rf, rm = _rms(ofv, FOX_W), _rms(omv, FOX_W)
        fhat, mhat = ofv * rf, omv * rm
        a = jnp.concatenate([fhat * gfo_ref[...], mhat * gmo_ref[...]], axis=1).astype(BF16)
        a_ref[...] = a
        x2 = x_ref[...] + _dot(a, wo_ref[...])
        r2 = _rms(x2, D_MODEL)
        xh2 = x2 * r2
        h2 = (xh2 * g2_ref[...]).astype(BF16)
        h2_ref[...] = h2
        gt = _dot_nt(h2, wg_ref[...])
        up = _dot_nt(h2, wu_ref[...])
        sg = jax.nn.sigmoid(gt)
        sl = gt * sg
        hid = (sl * up).astype(BF16)
        hid_ref[...] = hid
        x3 = x2 + _dot(hid, wd_ref[...])
        r3 = _rms(x3, D_MODEL)
        xh3 = x3 * r3
        diff = xh3 * g3_ref[...] - t_ref[...]
        dy = diff * (1.0 / D_MODEL)
        st_ref[3:4, :] += jnp.sum(diff * diff, axis=0, keepdims=True) * (0.5 / D_MODEL)
        st_ref[0:1, :] += jnp.sum(dy * xh3, axis=0, keepdims=True)
        dx3 = _rms_bwd(dy, xh3, r3, g3_ref[...], D_MODEL)
        dx3b = dx3.astype(BF16)
        dx3_ref[...] = dx3b
        dhid = _dot_nt(dx3b, wd_ref[...])
        dg = (dhid * up * (sg * (1.0 + gt * (1.0 - sg)))).astype(BF16)
        du = (dhid * sl).astype(BF16)
        dg_ref[...] = dg
        du_ref[...] = du
        dh2 = _dot(dg, wg_ref[...]) + _dot(du, wu_ref[...])
        st_ref[1:2, :] += jnp.sum(dh2 * xh2, axis=0, keepdims=True)
        dx2 = dx3 + _rms_bwd(dh2, xh2, r2, g2_ref[...], D_MODEL)
        dx2_ref[...] = dx2
        da = _dot_nt(dx2.astype(BF16), wo_ref[...])
        daf, dam = da[:, 0:FOX_W], da[:, FOX_W:2 * FOX_W]
        st_ref[2:3, 0:FOX_W] += jnp.sum(daf * fhat, axis=0, keepdims=True)
        st_ref[2:3, FOX_W:2 * FOX_W] += jnp.sum(dam * mhat, axis=0, keepdims=True)
        dof_ref[...] = _rms_bwd(daf, fhat, rf, gfo_ref[...], FOX_W).astype(BF16)
        dom_ref[...] = _rms_bwd(dam, mhat, rm, gmo_ref[...], FOX_W).astype(BF16)

    row = lambda w: pl.BlockSpec((tm, w), lambda i: (i, 0))
    ff = jax.ShapeDtypeStruct((T, D_FF), BF16)
    out_shape = (
        jax.ShapeDtypeStruct((T, 1024), BF16), jax.ShapeDtypeStruct((T, 1024), BF16), ff, ff, ff,
        jax.ShapeDtypeStruct((T, 1024), BF16), jax.ShapeDtypeStruct((T, 1024), F32),
        jax.ShapeDtypeStruct((T, 512), BF16), jax.ShapeDtypeStruct((T, 512), BF16), jax.ShapeDtypeStruct((8, 1024), F32),
    )
    return pl.pallas_call(
        body, name="mid", grid=(T // tm,), out_shape=out_shape,
        in_specs=[row(512), row(512), row(1024), row(1024), _full(g_fo.shape), _full(g_mo.shape), _full(g2.shape), _full(g3.shape),
                  _full(w_o.shape), _full(w_g.shape), _full(w_u.shape), _full(w_d.shape)],
        out_specs=[row(1024), row(1024), row(D_FF), row(D_FF), row(D_FF), row(1024), row(1024), row(512), row(512),
                   pl.BlockSpec((8, 1024), lambda i: (0, 0))],
        compiler_params=_params(("arbitrary",)),
    )(of, om, x, tgt, g_fo, g_mo, g2, g3, w_o, w_g, w_u, w_d)


def _in_bwd(dqf, dkf, dvf, dqm, dkm, dvm, lat, x, dx2, g1, gq, gkv, bfg, ct, st, sel_t, w_in, w_q12, w_kv, seq, tm, ex=None):
    T = x.shape[0]
    nblk = T // tm
    nsb = seq // tm
    n_in, n_out = (len(ex.ins), len(ex.outs)) if ex else (0, 0)

    def body(*refs):
        (dqf_ref, dkf_ref, dvf_ref, dqm_ref, dkm_ref, dvm_ref, lat_ref, x_ref, dx2_ref, g1_ref, gq_ref, gkv_ref, b_ref,
         ct_ref, st_ref, selt_ref, win_ref, wq_ref, wkv_ref) = refs[0:19]
        dx_ref, dproj_ref, dq12_ref, dkv_ref, stat_ref = refs[19 + n_in:24 + n_in]
        carry = refs[24 + n_in + n_out]
        i = pl.program_id(0)
        if ex:
            ex_refs = (refs[19:19 + n_in], refs[24 + n_in:24 + n_in + n_out], refs[25 + n_in + n_out:])

            @pl.when(i == 0)
            def _():
                ex.start(*ex_refs)

        @pl.when(i == 0)
        def _():
            stat_ref[...] = jnp.zeros_like(stat_ref)

        @pl.when(i % nsb == 0)
        def _():
            carry[...] = jnp.zeros_like(carry)

        lane = lax.broadcasted_iota(jnp.int32, (tm, LANES), 1)
        low = lane < 64
        ctv, stv = ct_ref[...], st_ref[...]

        for j in range(PAIRS):
            e, o = 2 * LANES * j, 2 * LANES * j + LANES
            dq = jnp.where(low, dqf_ref[:, e:e + LANES], 0.0) + pltpu.roll(jnp.where(low, dqf_ref[:, o:o + LANES], 0.0), 64, 1)
            dk = jnp.where(low, dkf_ref[:, e:e + LANES], 0.0) + pltpu.roll(jnp.where(low, dkf_ref[:, o:o + LANES], 0.0), 64, 1)
            dproj_ref[:, C_FQ + LANES * j:C_FQ + LANES * (j + 1)] = (dq * FOX_SCALE).astype(BF16)
            dproj_ref[:, C_FK + LANES * j:C_FK + LANES * (j + 1)] = dk.astype(BF16)
        dproj_ref[:, C_FV:C_QL] = dvf_ref[...].astype(BF16)
        dcv = dkf_ref[...] + dqf_ref[...]
        k_hi = dcv.astype(BF16)
        k_lo = (dcv - k_hi.astype(F32)).astype(BF16)
        dc = _dot(k_hi, selt_ref[...]) + _dot(k_lo, selt_ref[...])
        rr = lax.broadcasted_iota(jnp.int32, (tm, tm), 0)
        cc = lax.broadcasted_iota(jnp.int32, (tm, tm), 1)
        triu = (cc >= rr).astype(BF16)
        a0, a1, a2 = _split3(dc)
        dlf = _dot(triu, a0) + _dot(triu, a1) + _dot(triu, a2) + carry[0:1, :]
        carry[0:1, :] = dlf[0:1, :]
        misc_a = lat_ref[:, Q_RANK + KV_RANK:Q_RANK + KV_RANK + LANES]
        z = misc_a + b_ref[...]
        dz = jnp.where(lane < HEADS, dlf * jax.nn.sigmoid(-z), 0.0)
        stat_ref[3:4, 0:LANES] += jnp.sum(dz, axis=0, keepdims=True)

        cq = (jnp.where(low, 1.0, 0.0) + ctv) * MLA_SCALE
        sq = stv * MLA_SCALE
        dkpe = jnp.zeros((tm, LANES), F32)
        for hd in range(HEADS):
            s0 = LANES * hd
            dqh = dqm_ref[:, s0:s0 + LANES]
            dq12_ref[:, s0:s0 + LANES] = (dqh * cq).astype(BF16)
            dq12_ref[:, 1024 + s0:1024 + s0 + LANES] = (dqh * sq).astype(BF16)
            dkpe = dkpe + dkm_ref[:, s0:s0 + LANES]
        dkv_ref[:, 0:1024] = dkm_ref[...].astype(BF16)
        dkv_ref[:, 1024:1536] = dvm_ref[...].astype(BF16)
        dproj_ref[:, C_MA:C_MB] = (dz + dkpe * ctv).astype(BF16)
        dproj_ref[:, C_MB:C_END] = (dkpe * stv).astype(BF16)
        dqn = _dot(dq12_ref[...], wq_ref[...])
        dkvn = _dot_nt(dkv_ref[...], wkv_ref[...])
        ql = lat_ref[:, 0:Q_RANK]
        kvl = lat_ref[:, Q_RANK:Q_RANK + KV_RANK]
        rq, rkv = _rms(ql, Q_RANK), _rms(kvl, KV_RANK)
        qhat, kvhat = ql * rq, kvl * rkv
        stat_ref[1:2, 0:Q_RANK] += jnp.sum(dqn * qhat, axis=0, keepdims=True)
        stat_ref[2:3, 0:KV_RANK] += jnp.sum(dkvn * kvhat, axis=0, keepdims=True)
        dproj_ref[:, C_QL:C_KVL] = _rms_bwd(dqn, qhat, rq, gq_ref[...], Q_RANK).astype(BF16)
        dproj_ref[:, C_KVL:C_MA] = _rms_bwd(dkvn, kvhat, rkv, gkv_ref[...], KV_RANK).astype(BF16)

        dh1 = _dot(dproj_ref[...], win_ref[...])
        xv = x_ref[...]
        r1 = _rms(xv, D_MODEL)
        xh = xv * r1
        stat_ref[0:1, :] += jnp.sum(dh1 * xh, axis=0, keepdims=True)
        dx_ref[...] = dx2_ref[...] + _rms_bwd(dh1, xh, r1, g1_ref[...], D_MODEL)

        if ex:
            @pl.when(i == nblk - 1)
            def _():
                ex.wait(*ex_refs)

    rev = lambda w: pl.BlockSpec((tm, w), lambda i: (nblk - 1 - i, 0))
    out_shape = (
        jax.ShapeDtypeStruct((T, 1024), F32), jax.ShapeDtypeStruct((T, C_END), BF16), jax.ShapeDtypeStruct((T, 2048), BF16),
        jax.ShapeDtypeStruct((T, 1536), BF16), jax.ShapeDtypeStruct((8, 1024), F32),
    ) + tuple(ex.outs if ex else ())
    res = pl.pallas_call(
        body, name="in_bwd", grid=(nblk,), out_shape=out_shape,
        in_specs=[rev(1024), rev(1024), rev(512), rev(1024), rev(1024), rev(512), rev(512), rev(1024), rev(1024),
                  _full(g1.shape), _full(gq.shape), _full(gkv.shape), _full(bfg.shape), rev(LANES), rev(LANES), _full(sel_t.shape),
                  _full(w_in.shape), _full(w_q12.shape), _full(w_kv.shape)] + [ANY] * n_in,
        out_specs=[rev(1024), rev(C_END), rev(2048), rev(1536), pl.BlockSpec((8, 1024), lambda i: (0, 0))] + [ANY] * n_out,
        scratch_shapes=[pltpu.VMEM((8, LANES), F32)] + (ex.sems() if ex else []),
        compiler_params=_params(("arbitrary",)),
    )(dqf, dkf, dvf, dqm, dkm, dvm, lat, x, dx2, g1, gq, gkv, bfg, ct, st, sel_t, w_in, w_q12, w_kv, *(ex.ins if ex else ()))
    return res[0], res[1], res[2], res[3], res[4], list(res[5:])


def _wgrad(a, b, tk, tt, name, ex=None):
    T, K = a.shape
    N = b.shape[1]
    n_in, n_out = (len(ex.ins), len(ex.outs)) if ex else (0, 0)
    gk, gt = K // tk, T // tt

    def body(*refs):
        a_ref, b_ref, o_ref = refs[0], refs[1], refs[2 + n_in]
        kb, t = pl.program_id(0), pl.program_id(1)
        if ex:
            ex_refs = (refs[2:2 + n_in], refs[3 + n_in:3 + n_in + n_out], refs[3 + n_in + n_out:])

            @pl.when((kb == 0) & (t == 0))
            def _():
                ex.start(*ex_refs)

        @pl.when(t == 0)
        def _():
            o_ref[...] = jnp.zeros_like(o_ref)

        o_ref[...] += _dot_tn(a_ref[...].astype(BF16), b_ref[...].astype(BF16))

        if ex:
            @pl.when((kb == gk - 1) & (t == gt - 1))
            def _():
                ex.wait(*ex_refs)

    res = pl.pallas_call(
        body, name=name, grid=(gk, gt), out_shape=(jax.ShapeDtypeStruct((K, N), F32),) + tuple(ex.outs if ex else ()),
        in_specs=[pl.BlockSpec((tt, tk), lambda kb, t: (t, kb)), pl.BlockSpec((tt, N), lambda kb, t: (t, 0))] + [ANY] * n_in,
        out_specs=[pl.BlockSpec((tk, N), lambda kb, t: (kb, 0))] + [ANY] * n_out,
        scratch_shapes=ex.sems() if ex else [], input_output_aliases=ex.aliases(2, 1) if ex else {},
        compiler_params=_params(("arbitrary", "arbitrary")),
    )(a, b, *(ex.ins if ex else ()))
    return (res[0], list(res[1:])) if ex else res[0]


def _adamw(w, g, m, v, name):
    R, C = w.shape
    tr = _row_tile(R)

    def body(w_ref, g_ref, m_ref, v_ref, d_ref, nm_ref, nv_ref):
        gv = g_ref[...]
        nm = ADAM_B1 * m_ref[...] + (1.0 - ADAM_B1) * gv
        nv = ADAM_B2 * v_ref[...] + (1.0 - ADAM_B2) * (gv * gv)
        m_hat = nm / (1.0 - ADAM_B1 ** ADAM_STEP)
        v_hat = nv / (1.0 - ADAM_B2 ** ADAM_STEP)
        d_ref[...] = -ADAM_LR * (m_hat / (jnp.sqrt(v_hat) + ADAM_EPS) + ADAM_WD * w_ref[...])
        nm_ref[...] = nm
        nv_ref[...] = nv

    blk = pl.BlockSpec((tr, C), lambda i: (i, 0))
    sh = jax.ShapeDtypeStruct((R, C), F32)
    return pl.pallas_call(
        body, name=name, grid=(R // tr,), out_shape=(sh, sh, sh),
        in_specs=[blk, blk, blk, blk], out_specs=[blk, blk, blk],
        compiler_params=_params(("arbitrary",)),
    )(w, g, m, v)


def _arrange(win_t, wuq_t, wukv):
    dt = win_t.dtype
    z = lambda r: jnp.zeros((r, D_MODEL), dt)
    zh = lambda r: jnp.zeros((HEADS, r, Q_RANK), dt)
    kr1, kr2 = win_t[1928:1944], win_t[1944:1960]
    misc_a = jnp.concatenate([win_t[1536:1544], z(56), kr1, kr2, z(32)], axis=0)
    misc_b = jnp.concatenate([z(64), kr2, kr1, z(32)], axis=0)
    w_in = jnp.concatenate([win_t[0:1536], win_t[1544:1928], misc_a, misc_b], axis=0)
    wq = wuq_t.reshape(HEADS, 96, Q_RANK)
    q1 = jnp.concatenate([wq, zh(32)], axis=1).reshape(1024, Q_RANK)
    q2 = jnp.concatenate([zh(64), wq[:, 80:96], wq[:, 64:80], zh(32)], axis=1).reshape(1024, Q_RANK)
    wkv = wukv.reshape(KV_RANK, HEADS, 128)
    wk = jnp.concatenate([wkv[:, :, 0:64], jnp.zeros((KV_RANK, HEADS, 64), dt)], axis=2).reshape(KV_RANK, 1024)
    wv = wkv[:, :, 64:128].reshape(KV_RANK, 512)
    return dict(w_in=w_in, w_q12=jnp.concatenate([q1, q2], axis=0), w_k=wk, w_v=wv, w_kv=jnp.concatenate([wk, wv], axis=1))


def _unarrange(g_in, g_q12, g_kv):
    kr1 = g_in[C_MA + 64:C_MA + 80] + g_in[C_MB + 80:C_MB + 96]
    kr2 = g_in[C_MA + 80:C_MA + 96] + g_in[C_MB + 64:C_MB + 80]
    win_t = jnp.concatenate([g_in[0:1536], g_in[C_MA:C_MA + 8], g_in[1536:1920], kr1, kr2], axis=0)
    g1 = g_q12[0:1024].reshape(HEADS, 128, Q_RANK)
    g2 = g_q12[1024:2048].reshape(HEADS, 128, Q_RANK)
    wuq_t = jnp.concatenate([g1[:, 0:64], g1[:, 64:80] + g2[:, 80:96], g1[:, 80:96] + g2[:, 64:80]], axis=1).reshape(768, Q_RANK)
    gk = g_kv[:, 0:1024].reshape(KV_RANK, HEADS, 128)
    gv = g_kv[:, 1024:1536].reshape(KV_RANK, HEADS, 64)
    wukv = jnp.concatenate([gk[:, :, 0:64], gv], axis=2).reshape(KV_RANK, 1024)
    return win_t, wuq_t, wukv


def _selectors():
    sel = np.zeros((384, 1024), np.float32)
    sel_t = np.zeros((1024, LANES), np.float32)
    for h in range(HEADS):
        for piece in range(3):
            sel[LANES * piece + h, LANES * h + 64 + piece] = 1.0
        sel_t[LANES * h + 64, h] = 1.0
    return jnp.asarray(sel, BF16), jnp.asarray(sel_t, BF16)


def _rope_tables(positions):
    inv_freq = 10000.0 ** (-jnp.arange(0, ROPE, 2, dtype=F32) / ROPE)
    ang = positions.reshape(-1).astype(F32)[:, None] * inv_freq[None, :]
    cos, sin = jnp.cos(ang), jnp.sin(ang)
    z64, z32 = jnp.zeros((ang.shape[0], 64), F32), jnp.zeros((ang.shape[0], 32), F32)
    return jnp.concatenate([z64, cos, cos, z32], axis=1), jnp.concatenate([z64, -sin, sin, z32], axis=1)


def _reduce_tail(slabs):
    got = _run_exchange(_swap_exchange(slabs), "tail_swap")
    sums = [_add_half(g, s) for g, s in zip(slabs, got)]
    recv = _run_exchange(_scatter_exchange(sums), "tail_scatter")
    return _run_exchange(_join_exchange([_sum_slabs(g, s, r) for g, s, r in zip(slabs, got, recv)]), "tail_join")


def _work(name, t):
    return jnp.swapaxes(t[0], 0, 1) if name in TRANSPOSED else t[0]


def _back(name, t):
    return (jnp.swapaxes(t, 0, 1) if name in TRANSPOSED else t)[None]


SMALL_ROWS = {"norm_mix_g": (0, 1024), "norm_ffn_g": (1, 1024), "final_norm_g": (2, 1024), "q_norm_g": (4, 256),
              "kv_norm_g": (5, 128), "b_fgate": (6, 8)}


def kernel(x, positions, norm_mix_g, w_in, b_fgate, q_norm_g, w_uq, kv_norm_g, w_ukv, fox_out_g, mla_out_g, w_o, norm_ffn_g, w_gate, w_up, w_down, final_norm_g, loss_target, m_norm_mix_g, m_w_in, m_b_fgate, m_q_norm_g, m_w_uq, m_kv_norm_g, m_w_ukv, m_fox_out_g, m_mla_out_g, m_w_o, m_norm_ffn_g, m_w_gate, m_w_up, m_w_down, m_final_norm_g, v_norm_mix_g, v_w_in, v_b_fgate, v_q_norm_g, v_w_uq, v_kv_norm_g, v_w_ukv, v_fox_out_g, v_mla_out_g, v_w_o, v_norm_ffn_g, v_w_gate, v_w_up, v_w_down, v_final_norm_g):
    names = ["norm_mix_g", "w_in", "b_fgate", "q_norm_g", "w_uq", "kv_norm_g", "w_ukv", "fox_out_g", "mla_out_g", "w_o",
             "norm_ffn_g", "w_gate", "w_up", "w_down", "final_norm_g"]
    wts = dict(zip(names, [norm_mix_g, w_in, b_fgate, q_norm_g, w_uq, kv_norm_g, w_ukv, fox_out_g, mla_out_g, w_o, norm_ffn_g,
                           w_gate, w_up, w_down, final_norm_g]))
    mom = dict(zip(names, [m_norm_mix_g, m_w_in, m_b_fgate, m_q_norm_g, m_w_uq, m_kv_norm_g, m_w_ukv, m_fox_out_g, m_mla_out_g,
                           m_w_o, m_norm_ffn_g, m_w_gate, m_w_up, m_w_down, m_final_norm_g]))
    var = dict(zip(names, [v_norm_mix_g, v_w_in, v_b_fgate, v_q_norm_g, v_w_uq, v_kv_norm_g, v_w_ukv, v_fox_out_g, v_mla_out_g,
                           v_w_o, v_norm_ffn_g, v_w_gate, v_w_up, v_w_down, v_final_norm_g]))
    shard = {n: _work(n, wts[n]) for n in HEAD3 + FFN4}
    nb, seq, _ = x.shape
    T = nb * seq
    tm, tq = min(ROW_TILE, seq), min(ATTN_TILE, seq)
    tt = min(WGRAD_TILE, T)
    xf = x.reshape(T, D_MODEL)
    tgt = loss_target.reshape(T, D_MODEL)
    chip = 2 * lax.axis_index("x") + lax.axis_index("y")

    mine = [shard[n].astype(BF16) for n in HEAD3]
    head = _run_exchange(_gather_split_exchange(mine), "gather_head")
    win4, wuq4, wukv4 = [lax.dynamic_update_slice(h, s[None], (chip, 0, 0)) for h, s in zip(head, mine)]
    a = _arrange(win4.reshape(-1, D_MODEL), wuq4.reshape(-1, Q_RANK), wukv4.transpose(1, 0, 2).reshape(KV_RANK, -1))
    sel, sel_t = _selectors()
    ct, st = _rope_tables(positions)
    bfg = jnp.concatenate([b_fgate, jnp.zeros((1, LANES - HEADS), F32)], axis=1)
    g1, gq, gkv = norm_mix_g, q_norm_g, kv_norm_g

    h1, qf, kf, vf, qm, km, vm, lat, qn, kvn = _in_proj(xf, g1, a["w_in"], a["w_q12"], a["w_k"], a["w_v"], gq, gkv, bfg, ct, st, sel, seq, tm)
    of, lse_f, (wo4, wg4) = _attn_fwd(qf, kf, vf, nb, seq, tq, "fox_fwd", _gather_exchange([shard[n].astype(BF16) for n in FFN4[:2]]))
    om, lse_m, (wu4, wd4) = _attn_fwd(qm, km, vm, nb, seq, tq, "mla_fwd", _gather_exchange([shard[n].astype(BF16) for n in FFN4[2:]]))
    a_cat, h2, hid, dg, du, dx3, dx2, dof, dom, st_mid = _mid(
        of, om, xf, tgt, fox_out_g, mla_out_g, norm_ffn_g, final_norm_g.reshape(1, D_MODEL),
        wo4.reshape(D_MODEL, D_MODEL), wg4.reshape(D_FF, D_MODEL), wu4.reshape(D_FF, D_MODEL), wd4.reshape(D_FF, D_MODEL), tm)

    slab = lambda g: g.reshape(N_CHIPS, g.shape[0] // N_CHIPS, g.shape[1])
    big = [slab(_wgrad(a_cat, dx2, D_MODEL, tt, "wgrad_o")), slab(_wgrad(dg, h2, D_FF // 2, tt, "wgrad_gate")),
           slab(_wgrad(du, h2, D_FF // 2, tt, "wgrad_up")), slab(_wgrad(hid, dx3, D_FF // 2, tt, "wgrad_down"))]
    dqf, dkf, dvf, got = _attn_bwd(qf, kf, vf, of, dof, lse_f, nb, seq, tq, "fox_bwd", True, _swap_exchange(big))
    sums = [_add_half(g, s) for g, s in zip(big, got)]
    dqm, dkm, dvm, recv = _attn_bwd(qm, km, vm, om, dom, lse_m, nb, seq, tq, "mla_bwd", False, _scatter_exchange(sums))
    halves = [_sum_slabs(g, s, r) for g, s, r in zip(big, got, recv)]
    dx, dproj, dq12, dkv, st_in, _ = _in_bwd(dqf, dkf, dvf, dqm, dkm, dvm, lat, xf, dx2, g1, gq, gkv, bfg, ct, st, sel_t,
                                             a["w_in"], a["w_q12"], a["w_kv"], seq, tm)
    g_in, joined = _wgrad(dproj, h1, C_END, tt, "wgrad_in", _join_exchange(halves))
    gshard = dict(zip(FFN4, joined))

    loss_row = jnp.concatenate([jnp.sum(st_mid[3:4, :], axis=1, keepdims=True), jnp.zeros((1, D_MODEL - 1), F32)], axis=1)
    stats = jnp.concatenate([st_in[0:1], st_mid[1:2], st_mid[0:1], st_mid[2:3], st_in[1:2], st_in[2:3], st_in[3:4], loss_row], axis=0)
    stats = _allreduce_small(stats)
    gwin_t, gwuq_t, gwukv = _unarrange(g_in, _wgrad(dq12, qn, 2048, tt, "wgrad_uq"), _wgrad(kvn, dkv, KV_RANK, tt, "wgrad_ukv"))
    tail = [slab(gwin_t), slab(gwuq_t), gwukv.reshape(KV_RANK, N_CHIPS, -1).transpose(1, 0, 2)]
    gshard.update(zip(HEAD3, _reduce_tail(tail)))

    grads, delta, new_m, new_v = {}, {}, {}, {}
    for n in HEAD3 + FFN4:
        grads[n] = _back(n, gshard[n])
        d, nm, nv = _adamw(shard[n], gshard[n], _work(n, mom[n]), _work(n, var[n]), "adamw_" + n)
        delta[n], new_m[n], new_v[n] = _back(n, d), _back(n, nm), _back(n, nv)
    sm_g = {n: stats[row:row + 1, 0:width] for n, (row, width) in SMALL_ROWS.items()}
    sm_g["fox_out_g"] = stats[3:4, 0:512]
    sm_g["mla_out_g"] = stats[3:4, 512:1024]
    pad = lambda t: jnp.pad(t.reshape(1, -1), ((0, 0), (0, 1024 - t.size)))
    stack = lambda d: jnp.concatenate([pad(d[n]) for n in SMALL], axis=0)
    sd, sm, sv = _adamw(stack(wts), stack(sm_g), stack(mom), stack(var), "adamw_small")
    for i, n in enumerate(SMALL):
        shp = wts[n].shape
        grads[n] = sm_g[n].reshape(shp)
        delta[n] = sd[i, 0:wts[n].size].reshape(shp)
        new_m[n] = sm[i, 0:wts[n].size].reshape(shp)
        new_v[n] = sv[i, 0:wts[n].size].reshape(shp)
    loss = stats[7, 0]
    return (loss, dx.reshape(x.shape), *[grads[n] for n in names], *[delta[n] for n in names],
            *[new_m[n] for n in names], *[new_v[n] for n in names])
```

```python
import functools

import numpy as np
import jax
import jax.numpy as jnp
from jax import lax
from jax.experimental import pallas as pl
from jax.experimental.pallas import tpu as pltpu

F32 = jnp.float32
BF16 = jnp.bfloat16
MESH = pl.DeviceIdType.MESH

EPS = 1e-6
D_MODEL = 1024
HEADS = 8
PAIRS = HEADS // 2
FOX_W = 512
Q_RANK = 256
KV_RANK = 128
ROPE = 32
D_FF = 2816
N_CHIPS = 4
FOX_SCALE = 64 ** -0.5
MLA_SCALE = 96 ** -0.5
LANES = 128
NEG = -1e30

ADAM_LR, ADAM_B1, ADAM_B2, ADAM_EPS, ADAM_WD, ADAM_STEP = 0.001, 0.9, 0.999, 1e-08, 0.01, 10

C_FQ, C_FK, C_FV, C_QL, C_KVL, C_MA, C_MB, C_END = 0, 512, 1024, 1536, 1792, 1920, 2048, 2176

VMEM_LIMIT = 60 * 1024 * 1024
ROW_TILE = 256
ATTN_TILE = 512
ATTN_FWD_TILE = 1024
WGRAD_TILE = 2048

HEAD3 = ("w_in", "w_uq", "w_ukv")
FFN4 = ("w_o", "w_gate", "w_up", "w_down")
TRANSPOSED = ("w_in", "w_uq", "w_gate", "w_up")
SMALL = ("norm_mix_g", "b_fgate", "q_norm_g", "kv_norm_g", "fox_out_g", "mla_out_g", "norm_ffn_g", "final_norm_g")


def _params(sem=None):
    return pltpu.CompilerParams(dimension_semantics=sem, vmem_limit_bytes=VMEM_LIMIT)


def _full(shape):
    n = len(shape)
    return pl.BlockSpec(shape, lambda *_: (0,) * n, pipeline_mode=pl.Buffered(1))


def _dot(a, b):
    return jnp.dot(a, b, preferred_element_type=F32)


def _dot_nt(a, b):
    return lax.dot_general(a, b, (((1,), (1,)), ((), ())), preferred_element_type=F32)


def _dot_tn(a, b):
    return lax.dot_general(a, b, (((0,), (0,)), ((), ())), preferred_element_type=F32)


def _split3(v):
    hi = v.astype(BF16)
    r1 = v - hi.astype(F32)
    mid = r1.astype(BF16)
    lo = (r1 - mid.astype(F32)).astype(BF16)
    return hi, mid, lo


def _rms(v, width):
    return lax.rsqrt(jnp.sum(v * v, axis=1, keepdims=True) * (1.0 / width) + EPS)


def _rms_bwd(dy, xhat, r, g, width):
    u = dy * g
    return r * (u - xhat * (jnp.sum(u * xhat, axis=1, keepdims=True) * (1.0 / width)))


ANY = pl.BlockSpec(memory_space=pl.ANY)


def _place():
    return lax.axis_index("x"), lax.axis_index("y"), lax.axis_index("c")


def _other_chips(x, y):
    return [(1 - x, y), (x, 1 - y), (1 - x, 1 - y)]


def _remote(src, dst, send, recv, j, dev):
    return pltpu.make_async_remote_copy(src_ref=src, dst_ref=dst, send_sem=send.at[j], recv_sem=recv.at[j], device_id=dev, device_id_type=MESH)


class _Exchange:
    def __init__(self, ins, outs, n_remote, n_local, build, in_place=False):
        self.ins, self.outs, self.n_remote, self.n_local, self.build = list(ins), list(outs), n_remote, max(n_local, 1), build
        self.in_place = in_place
        self.n_aliased = len(self.ins)

    def aliases(self, first_in, first_out):
        return {first_in + i: first_out + i for i in range(self.n_aliased)} if self.in_place else {}

    def sems(self):
        return [pltpu.SemaphoreType.DMA((self.n_remote,)), pltpu.SemaphoreType.DMA((self.n_remote,)), pltpu.SemaphoreType.DMA((self.n_local,))]

    def start(self, in_refs, out_refs, sems):
        for cp in self.build(in_refs, out_refs, *sems)[0]:
            cp.start()

    def wait(self, in_refs, out_refs, sems):
        for w in self.build(in_refs, out_refs, *sems)[1]:
            w()


def _gather_exchange(shards, own=True):
    def build(ins, outs, send, recv, lsem):
        x, y, c = _place()
        starts, waits = [], []
        for i, (s, o) in enumerate(zip(ins, outs)):
            if own:
                mine = pltpu.make_async_copy(s, o.at[2 * x + y], lsem.at[i])
                starts.append(mine)
                waits.append(mine.wait)
            for j, (cx, cy) in enumerate(_other_chips(x, y)):
                out = _remote(s, o.at[2 * x + y], send, recv, 3 * i + j, (cx, cy, c))
                starts.append(out)
                waits.append(_remote(s, o.at[2 * cx + cy], send, recv, 3 * i + j, (cx, cy, c)).wait_recv)
                waits.append(out.wait_send)
        return starts, waits

    outs = [jax.ShapeDtypeStruct((N_CHIPS,) + s.shape, s.dtype) for s in shards]
    return _Exchange(shards, outs, 3 * len(shards), len(shards), build)


def _gather_split_exchange(shards):
    n = len(shards)

    def build(ins, outs, send, recv, lsem):
        x, y, c = _place()
        starts, waits, last = [], [], []
        for i, (s, o) in enumerate(zip(ins, outs)):
            hc = s.shape[1] // 2
            mine, other = pl.ds(c * hc, hc), pl.ds((1 - c) * hc, hc)
            for j, (cx, cy) in enumerate(_other_chips(x, y)):
                out = _remote(s.at[:, mine], o.at[2 * x + y, :, mine], send, recv, 3 * i + j, (cx, cy, c))
                landed = o.at[2 * cx + cy, :, mine]
                arrive = _remote(s.at[:, mine], landed, send, recv, 3 * i + j, (cx, cy, c))
                onward = _remote(landed, landed, send, recv, 3 * n + 3 * i + j, (x, y, 1 - c))
                from_sibling = _remote(landed, o.at[2 * cx + cy, :, other], send, recv, 3 * n + 3 * i + j, (x, y, 1 - c))
                starts.append(out)
                waits.append(lambda arrive=arrive, onward=onward: (arrive.wait_recv(), onward.start()))
                last += [from_sibling.wait_recv, onward.wait_send, out.wait_send]
        return starts, waits + last

    outs = [jax.ShapeDtypeStruct((N_CHIPS,) + s.shape, s.dtype) for s in shards]
    return _Exchange(shards, outs, 6 * n, 0, build)


def _swap_exchange(grads):
    def build(ins, outs, send, recv, lsem):
        x, y, c = _place()
        cps = []
        for i, (g, o) in enumerate(zip(ins, outs)):
            hc = g.shape[2] // 2
            cps.append(_remote(g.at[:, :, pl.ds((1 - c) * hc, hc)], o, send, recv, i, (x, y, 1 - c)))
        return cps, [cp.wait for cp in cps]

    outs = [jax.ShapeDtypeStruct((g.shape[0], g.shape[1], g.shape[2] // 2), g.dtype) for g in grads]
    return _Exchange(grads, outs, len(grads), 0, build)


def _scatter_exchange(sums):
    def build(ins, outs, send, recv, lsem):
        x, y, c = _place()
        cps = []
        for i, (s, o) in enumerate(zip(ins, outs)):
            for j, (cx, cy) in enumerate(_other_chips(x, y)):
                cps.append(_remote(s.at[2 * cx + cy], o.at[j], send, recv, 3 * i + j, (cx, cy, c)))
        return cps, [cp.wait for cp in cps]

    outs = [jax.ShapeDtypeStruct((3,) + s.shape[1:], s.dtype) for s in sums]
    return _Exchange(sums, outs, 3 * len(sums), 0, build)


def _join_exchange(bufs):
    def build(ins, outs, send, recv, lsem):
        x, y, c = _place()
        starts, waits = [], []
        for i, (t, o) in enumerate(zip(ins, outs)):
            hc = t.shape[1] // 2
            out = _remote(t.at[:, pl.ds(c * hc, hc)], o.at[:, pl.ds(c * hc, hc)], send, recv, i, (x, y, 1 - c))
            starts.append(out)
            waits += [_remote(t.at[:, pl.ds(c * hc, hc)], o.at[:, pl.ds((1 - c) * hc, hc)], send, recv, i, (x, y, 1 - c)).wait_recv,
                      out.wait_send]
        return starts, waits

    outs = [jax.ShapeDtypeStruct(t.shape, t.dtype) for t in bufs]
    return _Exchange(bufs, outs, len(bufs), 0, build, in_place=True)


def _everyone_exchange(v):
    def build(ins, outs, send, recv, lsem):
        x, y, c = _place()
        me = 4 * x + 2 * y + c
        mine = pltpu.make_async_copy(ins[0], outs[0].at[me], lsem.at[0])
        starts, waits = [mine], [mine.wait]
        for j in range(7):
            fx, fy, fc = (j + 1) >> 2 & 1, (j + 1) >> 1 & 1, (j + 1) & 1
            peer = (x ^ fx, y ^ fy, c ^ fc)
            out = _remote(ins[0], outs[0].at[me], send, recv, j, peer)
            starts.append(out)
            waits += [_remote(ins[0], outs[0].at[4 * peer[0] + 2 * peer[1] + peer[2]], send, recv, j, peer).wait_recv, out.wait_send]
        return starts, waits

    return _Exchange([v], [jax.ShapeDtypeStruct((8,) + v.shape, v.dtype)], 7, 1, build)


def _both(a, b):
    na_in, na_out = len(a.ins), len(a.outs)

    def build(ins, outs, send, recv, lsem):
        sa, wa = a.build(ins[:na_in], outs[:na_out], send.at[pl.ds(0, a.n_remote)], recv.at[pl.ds(0, a.n_remote)],
                         lsem.at[pl.ds(0, a.n_local)])
        sb, wb = b.build(ins[na_in:], outs[na_out:], send.at[pl.ds(a.n_remote, b.n_remote)], recv.at[pl.ds(a.n_remote, b.n_remote)],
                         lsem.at[pl.ds(a.n_local, b.n_local)])
        return sa + sb, wa + wb

    both = _Exchange(a.ins + b.ins, a.outs + b.outs, a.n_remote + b.n_remote, a.n_local + b.n_local, build, in_place=a.in_place)
    both.n_aliased = na_in
    return both


def _run_exchange(ex, name):
    n_in, n_out = len(ex.ins), len(ex.outs)

    def body(*refs):
        ins, outs, sems = refs[:n_in], refs[n_in:n_in + n_out], refs[n_in + n_out:]
        ex.start(ins, outs, sems)
        ex.wait(ins, outs, sems)

    return pl.pallas_call(
        body, name=name, out_shape=tuple(ex.outs), in_specs=[ANY] * n_in, out_specs=tuple([ANY] * n_out),
        scratch_shapes=ex.sems(), input_output_aliases=ex.aliases(0, 0),
        compiler_params=pltpu.CompilerParams(has_side_effects=True),
    )(*ex.ins)


def _sum_devices(rows):
    def body(r_ref, o_ref):
        acc = r_ref[0]
        for d in range(1, 8):
            acc = acc + r_ref[d]
        o_ref[...] = acc

    vm = pl.BlockSpec(memory_space=pltpu.VMEM)
    return pl.pallas_call(body, name="sum_devices", out_shape=jax.ShapeDtypeStruct(rows.shape[1:], rows.dtype),
                          in_specs=[vm], out_specs=vm)(rows)


def _add_half(g, got):
    n, R, C = g.shape
    hc = C // 2

    def body(c_ref, g_ref, r_ref, o_ref):
        o_ref[...] = (g_ref[...] + r_ref[...]).astype(BF16)

    c = lax.axis_index("c")
    return pl.pallas_call(
        body, name="add_half",
        grid_spec=pltpu.PrefetchScalarGridSpec(
            num_scalar_prefetch=1, grid=(n,),
            in_specs=[pl.BlockSpec((1, R, hc), lambda k, c_ref: (k, 0, c_ref[0])),
                      pl.BlockSpec((1, R, hc), lambda k, c_ref: (k, 0, 0))],
            out_specs=pl.BlockSpec((1, R, hc), lambda k, c_ref: (k, 0, 0))),
        out_shape=jax.ShapeDtypeStruct((n, R, hc), BF16),
        compiler_params=_params(("arbitrary",)),
    )(jnp.reshape(c, (1,)).astype(jnp.int32), g, got)


def _sum_slabs(g, got, recv):
    n, R, C = g.shape
    hc = C // 2

    def body(kc_ref, g_ref, s_ref, r_ref, o_ref):
        o_ref[...] = (((g_ref[0] + s_ref[0]) + r_ref[0].astype(F32)) + r_ref[1].astype(F32)) + r_ref[2].astype(F32)

    kc = jnp.stack([2 * lax.axis_index("x") + lax.axis_index("y"), lax.axis_index("c")]).astype(jnp.int32)
    return pl.pallas_call(
        body, name="sum_slabs",
        grid_spec=pltpu.PrefetchScalarGridSpec(
            num_scalar_prefetch=1, grid=(1,),
            in_specs=[pl.BlockSpec((1, R, hc), lambda i, kc_ref: (kc_ref[0], 0, kc_ref[1])),
                      pl.BlockSpec((1, R, hc), lambda i, kc_ref: (kc_ref[0], 0, 0)),
                      pl.BlockSpec((3, R, hc), lambda i, kc_ref: (0, 0, 0))],
            out_specs=pl.BlockSpec((R, hc), lambda i, kc_ref: (0, kc_ref[1]))),
        out_shape=jax.ShapeDtypeStruct((R, C), F32),
        compiler_params=_params(("arbitrary",)),
    )(kc, g, got, recv)


def _row_tile(rows):
    for cand in (256, 184, 176, 144, 128, 64, 32, 16, 8):
        if rows % cand == 0:
            return cand
    return rows


def _in_proj(x, g1, w_in, w_q12, w_k, w_v, gq, gkv, bfg, ct, st, sel, seq, tm):
    T = x.shape[0]
    nsb = seq // tm

    def body(x_ref, g1_ref, win_ref, wq_ref, wk_ref, wv_ref, gq_ref, gkv_ref, b_ref, ct_ref, st_ref, sel_ref,
             h1_ref, qf_ref, kf_ref, vf_ref, qm_ref, km_ref, vm_ref, lat_ref, qn_ref, kvn_ref, carry):
        i = pl.program_id(0)

        @pl.when(i % nsb == 0)
        def _():
            carry[...] = jnp.zeros_like(carry)

        xv = x_ref[...]
        h = (xv * _rms(xv, D_MODEL) * g1_ref[...]).astype(BF16)
        h1_ref[...] = h
        proj = _dot_nt(h, win_ref[...])
        lane = lax.broadcasted_iota(jnp.int32, (tm, LANES), 1)
        low = lane < 64
        misc_a = proj[:, C_MA:C_MB]
        misc_b = proj[:, C_MB:C_END]

        z = misc_a + b_ref[...]
        lf = jnp.where(lane < HEADS, jnp.minimum(z, 0.0) - jnp.log1p(jnp.exp(-jnp.abs(z))), 0.0)
        rr = lax.broadcasted_iota(jnp.int32, (tm, tm), 0)
        cc = lax.broadcasted_iota(jnp.int32, (tm, tm), 1)
        tri = (rr >= cc).astype(BF16)
        a0, a1, a2 = _split3(lf)
        c = _dot(tri, a0) + _dot(tri, a1) + _dot(tri, a2) + carry[0:1, :]
        carry[0:1, :] = c[tm - 1:tm, :]
        c0, c1, c2 = _split3(c)
        cpl = _dot(jnp.concatenate([c0, c1, c2], axis=1), sel_ref[...])
        qpad = jnp.where((lane >= 64) & (lane < 67), -1.0, 0.0)
        for j in range(PAIRS):
            qc = proj[:, C_FQ + LANES * j:C_FQ + LANES * (j + 1)] * FOX_SCALE
            kc = proj[:, C_FK + LANES * j:C_FK + LANES * (j + 1)]
            e, o = 2 * LANES * j, 2 * LANES * j + LANES
            qf_ref[:, e:e + LANES] = jnp.where(low, qc, qpad).astype(BF16)
            qf_ref[:, o:o + LANES] = jnp.where(low, pltpu.roll(qc, 64, 1), qpad).astype(BF16)
            kf_ref[:, e:e + LANES] = jnp.where(low, kc, cpl[:, e:e + LANES]).astype(BF16)
            kf_ref[:, o:o + LANES] = jnp.where(low, pltpu.roll(kc, 64, 1), cpl[:, o:o + LANES]).astype(BF16)
        vf_ref[...] = proj[:, C_FV:C_QL].astype(BF16)

        ql = proj[:, C_QL:C_KVL]
        kvl = proj[:, C_KVL:C_MA]
        qn = (ql * _rms(ql, Q_RANK) * gq_ref[...]).astype(BF16)
        kvn = (kvl * _rms(kvl, KV_RANK) * gkv_ref[...]).astype(BF16)
        lat_ref[...] = proj[:, C_QL:C_MB]
        qn_ref[...] = qn
        kvn_ref[...] = kvn
        q12 = _dot_nt(qn, wq_ref[...])
        kn = _dot(kvn, wk_ref[...])
        ctv = ct_ref[...]
        stv = st_ref[...]
        cq = (jnp.where(low, 1.0, 0.0) + ctv) * MLA_SCALE
        sq = stv * MLA_SCALE
        kpe = misc_a * ctv + misc_b * stv
        for hd in range(HEADS):
            s0 = LANES * hd
            qm_ref[:, s0:s0 + LANES] = (q12[:, s0:s0 + LANES] * cq + q12[:, 1024 + s0:1024 + s0 + LANES] * sq).astype(BF16)
            km_ref[:, s0:s0 + LANES] = (kn[:, s0:s0 + LANES] + kpe).astype(BF16)
        vm_ref[...] = _dot(kvn, wv_ref[...]).astype(BF16)

    row = lambda w: pl.BlockSpec((tm, w), lambda i: (i, 0))
    out_shape = (
        jax.ShapeDtypeStruct((T, D_MODEL), BF16),
        jax.ShapeDtypeStruct((T, 1024), BF16), jax.ShapeDtypeStruct((T, 1024), BF16), jax.ShapeDtypeStruct((T, 512), BF16),
        jax.ShapeDtypeStruct((T, 1024), BF16), jax.ShapeDtypeStruct((T, 1024), BF16), jax.ShapeDtypeStruct((T, 512), BF16),
        jax.ShapeDtypeStruct((T, 512), F32),
        jax.ShapeDtypeStruct((T, Q_RANK), BF16), jax.ShapeDtypeStruct((T, KV_RANK), BF16),
    )
    return pl.pallas_call(
        body, name="in_proj", grid=(T // tm,), out_shape=out_shape,
        in_specs=[row(D_MODEL), _full(g1.shape), _full(w_in.shape), _full(w_q12.shape), _full(w_k.shape), _full(w_v.shape),
                  _full(gq.shape), _full(gkv.shape), _full(bfg.shape), row(LANES), row(LANES), _full(sel.shape)],
        out_specs=[row(D_MODEL), row(1024), row(1024), row(512), row(1024), row(1024), row(512), row(512), row(Q_RANK), row(KV_RANK)],
        scratch_shapes=[pltpu.VMEM((8, LANES), F32)],
        compiler_params=_params(("arbitrary",)),
    )(x, g1, w_in, w_q12, w_k, w_v, gq, gkv, bfg, ct, st, sel)


def _attn_fwd(q, k, v, nb, seq, tq, name, ex=None):
    T = q.shape[0]
    nq = seq // tq
    n_in, n_out = (len(ex.ins), len(ex.outs)) if ex else (0, 0)

    def body(*refs):
        q_ref, k_ref, v_ref = refs[0:3]
        o_ref, lse_ref = refs[3 + n_in:5 + n_in]
        b, pr, qi = pl.program_id(0), pl.program_id(1), pl.program_id(2)
        if ex:
            ex_refs = (refs[3:3 + n_in], refs[5 + n_in:5 + n_in + n_out], refs[8 + n_in + n_out:])

            @pl.when((b == 0) & (pr == 0) & (qi == 0))
            def _():
                ex.start(*ex_refs)

        s_sc, p_sc, acc_sc = refs[5 + n_in + n_out:8 + n_in + n_out]
        strip = 64
        key_s = lax.broadcasted_iota(jnp.int32, (strip, tq), 0)
        qry_s = lax.broadcasted_iota(jnp.int32, (strip, tq), 1)
        row_t = lax.broadcasted_iota(jnp.int32, (LANES, tq), 0)
        acc_sc[...] = jnp.zeros(acc_sc.shape, F32)

        def fold(x, op):
            out = x[0:8]
            for r in range(8, strip, 8):
                out = op(out, x[r:r + 8])
            return out

        def step(kj, state, masked):
            rows = pl.ds(pl.multiple_of(kj * tq, tq), tq)
            for hh in range(2):
                s_sc[hh] = _dot_nt(k_ref[rows, LANES * hh:LANES * (hh + 1)], q_ref[:, LANES * hh:LANES * (hh + 1)])
            vv = v_ref[rows, :]
            new = []
            for hh in range(2):
                m, l = state[hh]

                def strip_of(r0, hh=hh):
                    s = s_sc[hh, r0:r0 + strip, :]
                    return jnp.where(key_s + r0 <= qry_s, s, NEG) if masked else s

                mx = fold(strip_of(0), jnp.maximum)
                for r0 in range(strip, tq, strip):
                    mx = jnp.maximum(mx, fold(strip_of(r0), jnp.maximum))
                m_new = jnp.maximum(m, jnp.max(mx, axis=0, keepdims=True))
                alpha = jnp.exp(m - m_new)
                sm = jnp.zeros((8, tq), F32)
                for r0 in range(0, tq, strip):
                    p = jnp.exp(strip_of(r0) - m_new)
                    sm = sm + fold(p, jnp.add)
                    p_sc[hh, r0:r0 + strip, :] = p.astype(BF16)
                l = alpha * l + jnp.sum(sm, axis=0, keepdims=True)
                acc_sc[hh] = alpha * acc_sc[hh] + _dot_tn(vv, p_sc[hh])
                new.append((m_new, l))
            return tuple(new)

        one = (jnp.full((1, tq), NEG, F32), jnp.zeros((1, tq), F32))
        state = lax.fori_loop(0, qi, functools.partial(step, masked=False), (one, one))
        (m0, l0), (m1, l1) = step(qi, state, True)
        o_ref[...] = jnp.where(row_t < 64, acc_sc[0] / l0, acc_sc[1] / l1).T
        lse_ref[:, 0:LANES] = jnp.broadcast_to(m0 + jnp.log(l0), (LANES, tq)).T
        lse_ref[:, LANES:2 * LANES] = jnp.broadcast_to(m1 + jnp.log(l1), (LANES, tq)).T

        if ex:
            @pl.when((b == nb - 1) & (pr == PAIRS - 1) & (qi == nq - 1))
            def _():
                ex.wait(*ex_refs)

    res = pl.pallas_call(
        body, name=name, grid=(nb, PAIRS, nq),
        out_shape=(jax.ShapeDtypeStruct((T, 512), F32), jax.ShapeDtypeStruct((T, 1024), F32)) + tuple(ex.outs if ex else ()),
        in_specs=[pl.BlockSpec((tq, 2 * LANES), lambda b, p, i: (b * nq + i, p)),
                  pl.BlockSpec((seq, 2 * LANES), lambda b, p, i: (b, p)),
                  pl.BlockSpec((seq, LANES), lambda b, p, i: (b, p))] + [ANY] * n_in,
        out_specs=[pl.BlockSpec((tq, LANES), lambda b, p, i: (b * nq + i, p)),
                   pl.BlockSpec((tq, 2 * LANES), lambda b, p, i: (b * nq + i, p))] + [ANY] * n_out,
        scratch_shapes=[pltpu.VMEM((2, tq, tq), F32), pltpu.VMEM((2, tq, tq), BF16), pltpu.VMEM((2, LANES, tq), F32)]
        + (ex.sems() if ex else []),
        compiler_params=_params(("arbitrary", "arbitrary", "arbitrary")),
    )(q, k, v, *(ex.ins if ex else ()))
    return res[0], res[1], list(res[2:])


def _attn_bwd(q, k, v, o, do, lse, nb, seq, tq, name, key_bias, ex=None):
    T = q.shape[0]
    nq = seq // tq
    n_in, n_out = (len(ex.ins), len(ex.outs)) if ex else (0, 0)

    def body(*refs):
        q_ref, k_ref, v_ref, o_ref, do_ref, lse_ref = refs[0:6]
        dq_ref, dk_ref, dv_ref = refs[6 + n_in:9 + n_in]
        dsc, rsum = refs[9 + n_in + n_out:11 + n_in + n_out]
        b, pr, kj = pl.program_id(0), pl.program_id(1), pl.program_id(2)
        if ex:
            ex_refs = (refs[6:6 + n_in], refs[9 + n_in:9 + n_in + n_out], refs[11 + n_in + n_out:])

            @pl.when((b == 0) & (pr == 0) & (kj == 0))
            def _():
                ex.start(*ex_refs)

        lane_s = lax.broadcasted_iota(jnp.int32, (seq, LANES), 1)
        lane = lax.broadcasted_iota(jnp.int32, (tq, LANES), 1)
        rr = lax.broadcasted_iota(jnp.int32, (tq, tq), 0)
        cc = lax.broadcasted_iota(jnp.int32, (tq, tq), 1)

        @pl.when(kj == 0)
        def _():
            dq_ref[...] = jnp.zeros_like(dq_ref)
            prod = do_ref[...].astype(F32) * o_ref[...]
            d0 = jnp.sum(jnp.where(lane_s < 64, prod, 0.0), axis=1, keepdims=True)
            d1 = jnp.sum(jnp.where(lane_s < 64, 0.0, prod), axis=1, keepdims=True)
            dsc[0] = jnp.broadcast_to(d0, (seq, LANES))
            dsc[1] = jnp.broadcast_to(d1, (seq, LANES))
            if key_bias:
                rsum[...] = jnp.zeros_like(rsum)

        vv = v_ref[...]

        def step(qi, carry, masked):
            dkt, dvt, cols = carry
            rows = pl.ds(pl.multiple_of(qi * tq, tq), tq)
            dov = do_ref[rows, :]
            new_dkt, new_cols = [], []
            for hh in range(2):
                qv = q_ref[rows, LANES * hh:LANES * (hh + 1)]
                kv = k_ref[:, LANES * hh:LANES * (hh + 1)]
                dom = jnp.where((lane < 64) if hh == 0 else (lane >= 64), dov, jnp.zeros((), BF16))
                s = _dot_nt(qv, kv)
                if masked:
                    s = jnp.where(cc <= rr, s, NEG)
                p = jnp.exp(s - lse_ref[rows, LANES * hh:LANES * hh + 1])
                dp = _dot_nt(dom, vv)
                ds32 = p * (dp - dsc[hh, rows, 0:1])
                col = cols[hh]
                if key_bias:
                    col = col + jnp.sum(ds32, axis=0, keepdims=True)
                    rsum[hh, rows, :] += jnp.broadcast_to(jnp.sum(ds32, axis=1, keepdims=True), (tq, LANES))
                ds = ds32.astype(BF16)
                dvt = dvt + _dot_tn(dom, p.astype(BF16))
                new_dkt.append(dkt[hh] + _dot_tn(qv, ds))
                new_cols.append(col)
                dq_ref[rows, LANES * hh:LANES * (hh + 1)] += _dot(ds, kv)
            return tuple(new_dkt), dvt, tuple(new_cols)

        zt = jnp.zeros((LANES, tq), F32)
        zc = jnp.zeros((1, tq), F32)
        carry = step(kj, ((zt, zt), zt, (zc, zc)), True)
        dkt, dvt, cols = lax.fori_loop(kj + 1, nq, functools.partial(step, masked=False), carry)
        row_t = lax.broadcasted_iota(jnp.int32, (LANES, tq), 0)
        for hh in range(2):
            dk_h = jnp.where(row_t == 64, -cols[hh], dkt[hh]) if key_bias else dkt[hh]
            dk_ref[:, LANES * hh:LANES * (hh + 1)] = dk_h.T
        dv_ref[...] = dvt.T

        if key_bias:
            @pl.when(kj == nq - 1)
            def _():
                for hh in range(2):
                    blk = dq_ref[:, LANES * hh:LANES * (hh + 1)]
                    dq_ref[:, LANES * hh:LANES * (hh + 1)] = jnp.where(lane_s == 64, rsum[hh], blk)

        if ex:
            @pl.when((b == nb - 1) & (pr == PAIRS - 1) & (kj == nq - 1))
            def _():
                ex.wait(*ex_refs)

    per_seq = lambda w: pl.BlockSpec((seq, w), lambda b, p, j: (b, p))
    per_blk = lambda w: pl.BlockSpec((tq, w), lambda b, p, j: (b * nq + j, p))
    res = pl.pallas_call(
        body, name=name, grid=(nb, PAIRS, nq),
        out_shape=(jax.ShapeDtypeStruct((T, 1024), F32), jax.ShapeDtypeStruct((T, 1024), F32), jax.ShapeDtypeStruct((T, 512), F32))
        + tuple(ex.outs if ex else ()),
        in_specs=[per_seq(2 * LANES), per_blk(2 * LANES), per_blk(LANES), per_seq(LANES), per_seq(LANES), per_seq(2 * LANES)] + [ANY] * n_in,
        out_specs=[per_seq(2 * LANES), per_blk(2 * LANES), per_blk(LANES)] + [ANY] * n_out,
        scratch_shapes=[pltpu.VMEM((2, seq, LANES), F32), pltpu.VMEM((2, seq, LANES) if key_bias else (2, 8, LANES), F32)]
        + (ex.sems() if ex else []),
        compiler_params=_params(("arbitrary", "arbitrary", "arbitrary")),
    )(q, k, v, o, do, lse, *(ex.ins if ex else ()))
    return res[0], res[1], res[2], list(res[3:])


def _mid(of, om, x, tgt, g_fo, g_mo, g2, g3, w_o, w_g, w_u, w_d, tm):
    T = x.shape[0]

    def body(of_ref, om_ref, x_ref, t_ref, gfo_ref, gmo_ref, g2_ref, g3_ref, wo_ref, wg_ref, wu_ref, wd_ref,
             a_ref, h2_ref, hid_ref, dg_ref, du_ref, dx3_ref, dx2_ref, dof_ref, dom_ref, st_ref):
        i = pl.program_id(0)

        @pl.when(i == 0)
        def _():
            st_ref[...] = jnp.zeros_like(st_ref)

        ofv, omv = of_ref[...], om_ref[...]
        rf, rm = _rms(ofv, FOX_W), _rms(omv, FOX_W)
        fhat, mhat = ofv * rf, omv * rm
        a = jnp.concatenate([fhat * gfo_ref[...], mhat * gmo_ref[...]], axis=1).astype(BF16)
        a_ref[...] = a
        x2 = x_ref[...] + _dot(a, wo_ref[...])
        r2 = _rms(x2, D_MODEL)
        xh2 = x2 * r2
        h2 = (xh2 * g2_ref[...]).astype(BF16)
        h2_ref[...] = h2
        gt = _dot_nt(h2, wg_ref[...])
        up = _dot_nt(h2, wu_ref[...])
        sg = jax.nn.sigmoid(gt)
        sl = gt * sg
        hid = (sl * up).astype(BF16)
        hid_ref[...] = hid
        x3 = x2 + _dot(hid, wd_ref[...])
        r3 = _rms(x3, D_MODEL)
        xh3 = x3 * r3
        diff = xh3 * g3_ref[...] - t_ref[...]
        dy = diff * (1.0 / D_MODEL)
        st_ref[3:4, :] += jnp.sum(diff * diff, axis=0, keepdims=True) * (0.5 / D_MODEL)
        st_ref[0:1, :] += jnp.sum(dy * xh3, axis=0, keepdims=True)
        dx3 = _rms_bwd(dy, xh3, r3, g3_ref[...], D_MODEL)
        dx3b = dx3.astype(BF16)
        dx3_ref[...] = dx3b
        dhid = _dot_nt(dx3b, wd_ref[...])
        dg = (dhid * up * (sg * (1.0 + gt * (1.0 - sg)))).astype(BF16)
        du = (dhid * sl).astype(BF16)
        dg_ref[...] = dg
        du_ref[...] = du
        dh2 = _dot(dg, wg_ref[...]) + _dot(du, wu_ref[...])
        st_ref[1:2, :] += jnp.sum(dh2 * xh2, axis=0, keepdims=True)
        dx2 = dx3 + _rms_bwd(dh2, xh2, r2, g2_ref[...], D_MODEL)
        dx2_ref[...] = dx2
        da = _dot_nt(dx2.astype(BF16), wo_ref[...])
        daf, dam = da[:, 0:FOX_W], da[:, FOX_W:2 * FOX_W]
        st_ref[2:3, 0:FOX_W] += jnp.sum(daf * fhat, axis=0, keepdims=True)
        st_ref[2:3, FOX_W:2 * FOX_W] += jnp.sum(dam * mhat, axis=0, keepdims=True)
        dof_ref[...] = _rms_bwd(daf, fhat, rf, gfo_ref[...], FOX_W).astype(BF16)
        dom_ref[...] = _rms_bwd(dam, mhat, rm, gmo_ref[...], FOX_W).astype(BF16)

    row = lambda w: pl.BlockSpec((tm, w), lambda i: (i, 0))
    ff = jax.ShapeDtypeStruct((T, D_FF), BF16)
    out_shape = (
        jax.ShapeDtypeStruct((T, 1024), BF16), jax.ShapeDtypeStruct((T, 1024), BF16), ff, ff, ff,
        jax.ShapeDtypeStruct((T, 1024), BF16), jax.ShapeDtypeStruct((T, 1024), F32),
        jax.ShapeDtypeStruct((T, 512), BF16), jax.ShapeDtypeStruct((T, 512), BF16), jax.ShapeDtypeStruct((8, 1024), F32),
    )
    return pl.pallas_call(
        body, name="mid", grid=(T // tm,), out_shape=out_shape,
        in_specs=[row(512), row(512), row(1024), row(1024), _full(g_fo.shape), _full(g_mo.shape), _full(g2.shape), _full(g3.shape),
                  _full(w_o.shape), _full(w_g.shape), _full(w_u.shape), _full(w_d.shape)],
        out_specs=[row(1024), row(1024), row(D_FF), row(D_FF), row(D_FF), row(1024), row(1024), row(512), row(512),
                   pl.BlockSpec((8, 1024), lambda i: (0, 0))],
        compiler_params=_params(("arbitrary",)),
    )(of, om, x, tgt, g_fo, g_mo, g2, g3, w_o, w_g, w_u, w_d)


def _in_bwd(dqf, dkf, dvf, dqm, dkm, dvm, lat, x, dx2, g1, gq, gkv, bfg, ct, st, sel_t, w_in, w_q12, w_kv, seq, tm, ex=None):
    T = x.shape[0]
    nblk = T // tm
    nsb = seq // tm
    n_in, n_out = (len(ex.ins), len(ex.outs)) if ex else (0, 0)

    def body(*refs):
        (dqf_ref, dkf_ref, dvf_ref, dqm_ref, dkm_ref, dvm_ref, lat_ref, x_ref, dx2_ref, g1_ref, gq_ref, gkv_ref, b_ref,
         ct_ref, st_ref, selt_ref, win_ref, wq_ref, wkv_ref) = refs[0:19]
        dx_ref, dproj_ref, dq12_ref, dkv_ref, stat_ref = refs[19 + n_in:24 + n_in]
        carry = refs[24 + n_in + n_out]
        i = pl.program_id(0)
        if ex:
            ex_refs = (refs[19:19 + n_in], refs[24 + n_in:24 + n_in + n_out], refs[25 + n_in + n_out:])

            @pl.when(i == 0)
            def _():
                ex.start(*ex_refs)

        @pl.when(i == 0)
        def _():
            stat_ref[...] = jnp.zeros_like(stat_ref)

        @pl.when(i % nsb == 0)
        def _():
            carry[...] = jnp.zeros_like(carry)

        lane = lax.broadcasted_iota(jnp.int32, (tm, LANES), 1)
        low = lane < 64
        ctv, stv = ct_ref[...], st_ref[...]

        for j in range(PAIRS):
            e, o = 2 * LANES * j, 2 * LANES * j + LANES
            dq = jnp.where(low, dqf_ref[:, e:e + LANES], 0.0) + pltpu.roll(jnp.where(low, dqf_ref[:, o:o + LANES], 0.0), 64, 1)
            dk = jnp.where(low, dkf_ref[:, e:e + LANES], 0.0) + pltpu.roll(jnp.where(low, dkf_ref[:, o:o + LANES], 0.0), 64, 1)
            dproj_ref[:, C_FQ + LANES * j:C_FQ + LANES * (j + 1)] = (dq * FOX_SCALE).astype(BF16)
            dproj_ref[:, C_FK + LANES * j:C_FK + LANES * (j + 1)] = dk.astype(BF16)
        dproj_ref[:, C_FV:C_QL] = dvf_ref[...].astype(BF16)
        dcv = dkf_ref[...] + dqf_ref[...]
        k_hi = dcv.astype(BF16)
        k_lo = (dcv - k_hi.astype(F32)).astype(BF16)
        dc = _dot(k_hi, selt_ref[...]) + _dot(k_lo, selt_ref[...])
        rr = lax.broadcasted_iota(jnp.int32, (tm, tm), 0)
        cc = lax.broadcasted_iota(jnp.int32, (tm, tm), 1)
        triu = (cc >= rr).astype(BF16)
        a0, a1, a2 = _split3(dc)
        dlf = _dot(triu, a0) + _dot(triu, a1) + _dot(triu, a2) + carry[0:1, :]
        carry[0:1, :] = dlf[0:1, :]
        misc_a = lat_ref[:, Q_RANK + KV_RANK:Q_RANK + KV_RANK + LANES]
        z = misc_a + b_ref[...]
        dz = jnp.where(lane < HEADS, dlf * jax.nn.sigmoid(-z), 0.0)
        stat_ref[3:4, 0:LANES] += jnp.sum(dz, axis=0, keepdims=True)

        cq = (jnp.where(low, 1.0, 0.0) + ctv) * MLA_SCALE
        sq = stv * MLA_SCALE
        dkpe = jnp.zeros((tm, LANES), F32)
        for hd in range(HEADS):
            s0 = LANES * hd
            dqh = dqm_ref[:, s0:s0 + LANES]
            dq12_ref[:, s0:s0 + LANES] = (dqh * cq).astype(BF16)
            dq12_ref[:, 1024 + s0:1024 + s0 + LANES] = (dqh * sq).astype(BF16)
            dkpe = dkpe + dkm_ref[:, s0:s0 + LANES]
        dkv_ref[:, 0:1024] = dkm_ref[...].astype(BF16)
        dkv_ref[:, 1024:1536] = dvm_ref[...].astype(BF16)
        dproj_ref[:, C_MA:C_MB] = (dz + dkpe * ctv).astype(BF16)
        dproj_ref[:, C_MB:C_END] = (dkpe * stv).astype(BF16)
        dqn = _dot(dq12_ref[...], wq_ref[...])
        dkvn = _dot_nt(dkv_ref[...], wkv_ref[...])
        ql = lat_ref[:, 0:Q_RANK]
        kvl = lat_ref[:, Q_RANK:Q_RANK + KV_RANK]
        rq, rkv = _rms(ql, Q_RANK), _rms(kvl, KV_RANK)
        qhat, kvhat = ql * rq, kvl * rkv
        stat_ref[1:2, 0:Q_RANK] += jnp.sum(dqn * qhat, axis=0, keepdims=True)
        stat_ref[2:3, 0:KV_RANK] += jnp.sum(dkvn * kvhat, axis=0, keepdims=True)
        dproj_ref[:, C_QL:C_KVL] = _rms_bwd(dqn, qhat, rq, gq_ref[...], Q_RANK).astype(BF16)
        dproj_ref[:, C_KVL:C_MA] = _rms_bwd(dkvn, kvhat, rkv, gkv_ref[...], KV_RANK).astype(BF16)

        dh1 = _dot(dproj_ref[...], win_ref[...])
        xv = x_ref[...]
        r1 = _rms(xv, D_MODEL)
        xh = xv * r1
        stat_ref[0:1, :] += jnp.sum(dh1 * xh, axis=0, keepdims=True)
        dx_ref[...] = dx2_ref[...] + _rms_bwd(dh1, xh, r1, g1_ref[...], D_MODEL)

        if ex:
            @pl.when(i == nblk - 1)
            def _():
                ex.wait(*ex_refs)

    rev = lambda w: pl.BlockSpec((tm, w), lambda i: (nblk - 1 - i, 0))
    out_shape = (
        jax.ShapeDtypeStruct((T, 1024), F32), jax.ShapeDtypeStruct((T, C_END), BF16), jax.ShapeDtypeStruct((T, 2048), BF16),
        jax.ShapeDtypeStruct((T, 1536), BF16), jax.ShapeDtypeStruct((8, 1024), F32),
    ) + tuple(ex.outs if ex else ())
    res = pl.pallas_call(
        body, name="in_bwd", grid=(nblk,), out_shape=out_shape,
        in_specs=[rev(1024), rev(1024), rev(512), rev(1024), rev(1024), rev(512), rev(512), rev(1024), rev(1024),
                  _full(g1.shape), _full(gq.shape), _full(gkv.shape), _full(bfg.shape), rev(LANES), rev(LANES), _full(sel_t.shape),
                  _full(w_in.shape), _full(w_q12.shape), _full(w_kv.shape)] + [ANY] * n_in,
        out_specs=[rev(1024), rev(C_END), rev(2048), rev(1536), pl.BlockSpec((8, 1024), lambda i: (0, 0))] + [ANY] * n_out,
        scratch_shapes=[pltpu.VMEM((8, LANES), F32)] + (ex.sems() if ex else []),
        compiler_params=_params(("arbitrary",)),
    )(dqf, dkf, dvf, dqm, dkm, dvm, lat, x, dx2, g1, gq, gkv, bfg, ct, st, sel_t, w_in, w_q12, w_kv, *(ex.ins if ex else ()))
    return res[0], res[1], res[2], res[3], res[4], list(res[5:])


def _wgrad(a, b, tk, tt, name, ex=None):
    T, K = a.shape
    N = b.shape[1]
    n_in, n_out = (len(ex.ins), len(ex.outs)) if ex else (0, 0)
    gk, gt = K // tk, T // tt

    def body(*refs):
        a_ref, b_ref, o_ref = refs[0], refs[1], refs[2 + n_in]
        kb, t = pl.program_id(0), pl.program_id(1)
        if ex:
            ex_refs = (refs[2:2 + n_in], refs[3 + n_in:3 + n_in + n_out], refs[3 + n_in + n_out:])

            @pl.when((kb == 0) & (t == 0))
            def _():
                ex.start(*ex_refs)

        @pl.when(t == 0)
        def _():
            o_ref[...] = jnp.zeros_like(o_ref)

        o_ref[...] += _dot_tn(a_ref[...].astype(BF16), b_ref[...].astype(BF16))

        if ex:
            @pl.when((kb == gk - 1) & (t == gt - 1))
            def _():
                ex.wait(*ex_refs)

    res = pl.pallas_call(
        body, name=name, grid=(gk, gt), out_shape=(jax.ShapeDtypeStruct((K, N), F32),) + tuple(ex.outs if ex else ()),
        in_specs=[pl.BlockSpec((tt, tk), lambda kb, t: (t, kb)), pl.BlockSpec((tt, N), lambda kb, t: (t, 0))] + [ANY] * n_in,
        out_specs=[pl.BlockSpec((tk, N), lambda kb, t: (kb, 0))] + [ANY] * n_out,
        scratch_shapes=ex.sems() if ex else [], input_output_aliases=ex.aliases(2, 1) if ex else {},
        compiler_params=_params(("arbitrary", "arbitrary")),
    )(a, b, *(ex.ins if ex else ()))
    return (res[0], list(res[1:])) if ex else res[0]


def _adamw(w, g, m, v, name):
    R, C = w.shape
    tr = _row_tile(R)

    def body(w_ref, g_ref, m_ref, v_ref, d_ref, nm_ref, nv_ref):
        gv = g_ref[...]
        nm = ADAM_B1 * m_ref[...] + (1.0 - ADAM_B1) * gv
        nv = ADAM_B2 * v_ref[...] + (1.0 - ADAM_B2) * (gv * gv)
        m_hat = nm / (1.0 - ADAM_B1 ** ADAM_STEP)
        v_hat = nv / (1.0 - ADAM_B2 ** ADAM_STEP)
        d_ref[...] = -ADAM_LR * (m_hat / (jnp.sqrt(v_hat) + ADAM_EPS) + ADAM_WD * w_ref[...])
        nm_ref[...] = nm
        nv_ref[...] = nv

    blk = pl.BlockSpec((tr, C), lambda i: (i, 0))
    sh = jax.ShapeDtypeStruct((R, C), F32)
    return pl.pallas_call(
        body, name=name, grid=(R // tr,), out_shape=(sh, sh, sh),
        in_specs=[blk, blk, blk, blk], out_specs=[blk, blk, blk],
        compiler_params=_params(("arbitrary",)),
    )(w, g, m, v)


def _arrange(win_t, wuq_t, wukv):
    dt = win_t.dtype
    z = lambda r: jnp.zeros((r, D_MODEL), dt)
    zh = lambda r: jnp.zeros((HEADS, r, Q_RANK), dt)
    kr1, kr2 = win_t[1928:1944], win_t[1944:1960]
    misc_a = jnp.concatenate([win_t[1536:1544], z(56), kr1, kr2, z(32)], axis=0)
    misc_b = jnp.concatenate([z(64), kr2, kr1, z(32)], axis=0)
    w_in = jnp.concatenate([win_t[0:1536], win_t[1544:1928], misc_a, misc_b], axis=0)
    wq = wuq_t.reshape(HEADS, 96, Q_RANK)
    q1 = jnp.concatenate([wq, zh(32)], axis=1).reshape(1024, Q_RANK)
    q2 = jnp.concatenate([zh(64), wq[:, 80:96], wq[:, 64:80], zh(32)], axis=1).reshape(1024, Q_RANK)
    wkv = wukv.reshape(KV_RANK, HEADS, 128)
    wk = jnp.concatenate([wkv[:, :, 0:64], jnp.zeros((KV_RANK, HEADS, 64), dt)], axis=2).reshape(KV_RANK, 1024)
    wv = wkv[:, :, 64:128].reshape(KV_RANK, 512)
    return dict(w_in=w_in, w_q12=jnp.concatenate([q1, q2], axis=0), w_k=wk, w_v=wv, w_kv=jnp.concatenate([wk, wv], axis=1))


def _unarrange(g_in, g_q12, g_kv):
    kr1 = g_in[C_MA + 64:C_MA + 80] + g_in[C_MB + 80:C_MB + 96]
    kr2 = g_in[C_MA + 80:C_MA + 96] + g_in[C_MB + 64:C_MB + 80]
    win_t = jnp.concatenate([g_in[0:1536], g_in[C_MA:C_MA + 8], g_in[1536:1920], kr1, kr2], axis=0)
    g1 = g_q12[0:1024].reshape(HEADS, 128, Q_RANK)
    g2 = g_q12[1024:2048].reshape(HEADS, 128, Q_RANK)
    wuq_t = jnp.concatenate([g1[:, 0:64], g1[:, 64:80] + g2[:, 80:96], g1[:, 80:96] + g2[:, 64:80]], axis=1).reshape(768, Q_RANK)
    gk = g_kv[:, 0:1024].reshape(KV_RANK, HEADS, 128)
    gv = g_kv[:, 1024:1536].reshape(KV_RANK, HEADS, 64)
    wukv = jnp.concatenate([gk[:, :, 0:64], gv], axis=2).reshape(KV_RANK, 1024)
    return win_t, wuq_t, wukv


def _selectors():
    sel = np.zeros((384, 1024), np.float32)
    sel_t = np.zeros((1024, LANES), np.float32)
    for h in range(HEADS):
        for piece in range(3):
            sel[LANES * piece + h, LANES * h + 64 + piece] = 1.0
        sel_t[LANES * h + 64, h] = 1.0
    return jnp.asarray(sel, BF16), jnp.asarray(sel_t, BF16)


def _rope_tables(positions):
    inv_freq = 10000.0 ** (-jnp.arange(0, ROPE, 2, dtype=F32) / ROPE)
    n = positions.size
    ang = (positions.reshape(n // 8, 8, 1).astype(F32) * inv_freq[None, None, :]).reshape(n // 8, 8 * (ROPE // 2))
    cos, sin = jnp.cos(ang).reshape(n, ROPE // 2), jnp.sin(ang).reshape(n, ROPE // 2)
    z64, z32 = jnp.zeros((n, 64), F32), jnp.zeros((n, 32), F32)
    return jnp.concatenate([z64, cos, cos, z32], axis=1), jnp.concatenate([z64, -sin, sin, z32], axis=1)


def _reduce_tail(slabs):
    got = _run_exchange(_swap_exchange(slabs), "tail_swap")
    sums = [_add_half(g, s) for g, s in zip(slabs, got)]
    recv = _run_exchange(_scatter_exchange(sums), "tail_scatter")
    return _run_exchange(_join_exchange([_sum_slabs(g, s, r) for g, s, r in zip(slabs, got, recv)]), "tail_join")


def _work(name, t):
    return jnp.swapaxes(t[0], 0, 1) if name in TRANSPOSED else t[0]


def _back(name, t):
    return (jnp.swapaxes(t, 0, 1) if name in TRANSPOSED else t)[None]


SMALL_ROWS = {"norm_mix_g": (0, 1024), "norm_ffn_g": (1, 1024), "final_norm_g": (2, 1024), "q_norm_g": (4, 256),
              "kv_norm_g": (5, 128), "b_fgate": (6, 8)}


def kernel(x, positions, norm_mix_g, w_in, b_fgate, q_norm_g, w_uq, kv_norm_g, w_ukv, fox_out_g, mla_out_g, w_o, norm_ffn_g, w_gate, w_up, w_down, final_norm_g, loss_target, m_norm_mix_g, m_w_in, m_b_fgate, m_q_norm_g, m_w_uq, m_kv_norm_g, m_w_ukv, m_fox_out_g, m_mla_out_g, m_w_o, m_norm_ffn_g, m_w_gate, m_w_up, m_w_down, m_final_norm_g, v_norm_mix_g, v_w_in, v_b_fgate, v_q_norm_g, v_w_uq, v_kv_norm_g, v_w_ukv, v_fox_out_g, v_mla_out_g, v_w_o, v_norm_ffn_g, v_w_gate, v_w_up, v_w_down, v_final_norm_g):
    names = ["norm_mix_g", "w_in", "b_fgate", "q_norm_g", "w_uq", "kv_norm_g", "w_ukv", "fox_out_g", "mla_out_g", "w_o",
             "norm_ffn_g", "w_gate", "w_up", "w_down", "final_norm_g"]
    wts = dict(zip(names, [norm_mix_g, w_in, b_fgate, q_norm_g, w_uq, kv_norm_g, w_ukv, fox_out_g, mla_out_g, w_o, norm_ffn_g,
                           w_gate, w_up, w_down, final_norm_g]))
    mom = dict(zip(names, [m_norm_mix_g, m_w_in, m_b_fgate, m_q_norm_g, m_w_uq, m_kv_norm_g, m_w_ukv, m_fox_out_g, m_mla_out_g,
                           m_w_o, m_norm_ffn_g, m_w_gate, m_w_up, m_w_down, m_final_norm_g]))
    var = dict(zip(names, [v_norm_mix_g, v_w_in, v_b_fgate, v_q_norm_g, v_w_uq, v_kv_norm_g, v_w_ukv, v_fox_out_g, v_mla_out_g,
                           v_w_o, v_norm_ffn_g, v_w_gate, v_w_up, v_w_down, v_final_norm_g]))
    shard = {n: _work(n, wts[n]) for n in HEAD3 + FFN4}
    nb, seq, _ = x.shape
    T = nb * seq
    tm, tq = min(ROW_TILE, seq), min(ATTN_TILE, seq)
    tt = min(WGRAD_TILE, T)
    xf = x.reshape(T, D_MODEL)
    tgt = loss_target.reshape(T, D_MODEL)
    chip = 2 * lax.axis_index("x") + lax.axis_index("y")

    mine = [shard[n].astype(BF16) for n in HEAD3]
    head = _run_exchange(_gather_split_exchange(mine), "gather_head")
    win4, wuq4, wukv4 = [lax.dynamic_update_slice(h, s[None], (chip, 0, 0)) for h, s in zip(head, mine)]
    a = _arrange(win4.reshape(-1, D_MODEL), wuq4.reshape(-1, Q_RANK), wukv4.transpose(1, 0, 2).reshape(KV_RANK, -1))
    sel, sel_t = _selectors()
    ct, st = _rope_tables(positions)
    bfg = jnp.concatenate([b_fgate, jnp.zeros((1, LANES - HEADS), F32)], axis=1)
    g1, gq, gkv = norm_mix_g, q_norm_g, kv_norm_g

    h1, qf, kf, vf, qm, km, vm, lat, qn, kvn = _in_proj(xf, g1, a["w_in"], a["w_q12"], a["w_k"], a["w_v"], gq, gkv, bfg, ct, st, sel, seq, tm)
    tqf = min(ATTN_FWD_TILE, seq)
    of, lse_f, (wo4, wg4) = _attn_fwd(qf, kf, vf, nb, seq, tqf, "fox_fwd", _gather_exchange([shard[n].astype(BF16) for n in FFN4[:2]]))
    om, lse_m, (wu4, wd4) = _attn_fwd(qm, km, vm, nb, seq, tqf, "mla_fwd", _gather_exchange([shard[n].astype(BF16) for n in FFN4[2:]]))
    a_cat, h2, hid, dg, du, dx3, dx2, dof, dom, st_mid = _mid(
        of, om, xf, tgt, fox_out_g, mla_out_g, norm_ffn_g, final_norm_g.reshape(1, D_MODEL),
        wo4.reshape(D_MODEL, D_MODEL), wg4.reshape(D_FF, D_MODEL), wu4.reshape(D_FF, D_MODEL), wd4.reshape(D_FF, D_MODEL), tm)

    slab = lambda g: g.reshape(N_CHIPS, g.shape[0] // N_CHIPS, g.shape[1])
    big = [slab(_wgrad(a_cat, dx2, D_MODEL, tt, "wgrad_o")), slab(_wgrad(dg, h2, D_FF // 2, tt, "wgrad_gate")),
           slab(_wgrad(du, h2, D_FF // 2, tt, "wgrad_up")), slab(_wgrad(hid, dx3, D_FF // 2, tt, "wgrad_down"))]
    dqf, dkf, dvf, got = _attn_bwd(qf, kf, vf, of, dof, lse_f, nb, seq, tq, "fox_bwd", True, _swap_exchange(big))
    sums = [_add_half(g, s) for g, s in zip(big, got)]
    dqm, dkm, dvm, recv = _attn_bwd(qm, km, vm, om, dom, lse_m, nb, seq, tq, "mla_bwd", False, _scatter_exchange(sums))
    halves = [_sum_slabs(g, s, r) for g, s, r in zip(big, got, recv)]
    dx, dproj, dq12, dkv, st_in, _ = _in_bwd(dqf, dkf, dvf, dqm, dkm, dvm, lat, xf, dx2, g1, gq, gkv, bfg, ct, st, sel_t,
                                             a["w_in"], a["w_q12"], a["w_kv"], seq, tm)
    loss_row = jnp.concatenate([jnp.sum(st_mid[3:4, :], axis=1, keepdims=True), jnp.zeros((1, D_MODEL - 1), F32)], axis=1)
    stats = jnp.concatenate([st_in[0:1], st_mid[1:2], st_mid[0:1], st_mid[2:3], st_in[1:2], st_in[2:3], st_in[3:4], loss_row], axis=0)
    g_in, results = _wgrad(dproj, h1, C_END, tt, "wgrad_in", _both(_join_exchange(halves), _everyone_exchange(stats)))
    gshard = dict(zip(FFN4, results[:4]))
    stats = _sum_devices(results[4])

    gwin_t, gwuq_t, gwukv = _unarrange(g_in, _wgrad(dq12, qn, 2048, tt, "wgrad_uq"), _wgrad(kvn, dkv, KV_RANK, tt, "wgrad_ukv"))
    tail = [slab(gwin_t), slab(gwuq_t), gwukv.reshape(KV_RANK, N_CHIPS, -1).transpose(1, 0, 2)]
    gshard.update(zip(HEAD3, _reduce_tail(tail)))

    grads, delta, new_m, new_v = {}, {}, {}, {}
    for n in HEAD3 + FFN4:
        grads[n] = _back(n, gshard[n])
        d, nm, nv = _adamw(shard[n], gshard[n], _work(n, mom[n]), _work(n, var[n]), "adamw_" + n)
        delta[n], new_m[n], new_v[n] = _back(n, d), _back(n, nm), _back(n, nv)
    sm_g = {n: stats[row:row + 1, 0:width] for n, (row, width) in SMALL_ROWS.items()}
    sm_g["fox_out_g"] = stats[3:4, 0:512]
    sm_g["mla_out_g"] = stats[3:4, 512:1024]
    pad = lambda t: jnp.pad(t.reshape(1, -1), ((0, 0), (0, 1024 - t.size)))
    stack = lambda d: jnp.concatenate([pad(d[n]) for n in SMALL], axis=0)
    sd, sm, sv = _adamw(stack(wts), stack(sm_g), stack(mom), stack(var), "adamw_small")
    for i, n in enumerate(SMALL):
        shp = wts[n].shape
        grads[n] = sm_g[n].reshape(shp)
        delta[n] = sd[i, 0:wts[n].size].reshape(shp)
        new_m[n] = sm[i, 0:wts[n].size].reshape(shp)
        new_v[n] = sv[i, 0:wts[n].size].reshape(shp)
    loss = stats[7, 0]
    return (loss, dx.reshape(x.shape), *[grads[n] for n in names], *[delta[n] for n in names],
            *[new_m[n] for n in names], *[new_v[n] for n in names])
```

```python
import functools

import numpy as np
import jax
import jax.numpy as jnp
from jax import lax
from jax.experimental import pallas as pl
from jax.experimental.pallas import tpu as pltpu

F32 = jnp.float32
BF16 = jnp.bfloat16
MESH = pl.DeviceIdType.MESH

EPS = 1e-6
D_MODEL = 1024
HEADS = 8
PAIRS = HEADS // 2
FOX_W = 512
Q_RANK = 256
KV_RANK = 128
ROPE = 32
D_FF = 2816
N_CHIPS = 4
FOX_SCALE = 64 ** -0.5
MLA_SCALE = 96 ** -0.5
LANES = 128
NEG = -1e30

ADAM_LR, ADAM_B1, ADAM_B2, ADAM_EPS, ADAM_WD, ADAM_STEP = 0.001, 0.9, 0.999, 1e-08, 0.01, 10

C_FQ, C_FK, C_FV, C_QL, C_KVL, C_MA, C_END = 0, 512, 1024, 1536, 1792, 1920, 2048
C_MB = C_END

VMEM_LIMIT = 60 * 1024 * 1024
ROW_TILE = 256
ATTN_TILE = 512
ATTN_FWD_TILE = 1024
WGRAD_TILE = 2048

HEAD3 = ("w_in", "w_uq", "w_ukv")
FFN4 = ("w_o", "w_gate", "w_up", "w_down")
TRANSPOSED = ("w_in", "w_uq", "w_gate", "w_up")
SMALL = ("norm_mix_g", "b_fgate", "q_norm_g", "kv_norm_g", "fox_out_g", "mla_out_g", "norm_ffn_g", "final_norm_g")


def _params(sem=None):
    return pltpu.CompilerParams(dimension_semantics=sem, vmem_limit_bytes=VMEM_LIMIT)


def _full(shape):
    n = len(shape)
    return pl.BlockSpec(shape, lambda *_: (0,) * n, pipeline_mode=pl.Buffered(1))


def _dot(a, b):
    return jnp.dot(a, b, preferred_element_type=F32)


def _dot_nt(a, b):
    return lax.dot_general(a, b, (((1,), (1,)), ((), ())), preferred_element_type=F32)


def _dot_tn(a, b):
    return lax.dot_general(a, b, (((0,), (0,)), ((), ())), preferred_element_type=F32)


def _split3(v):
    hi = v.astype(BF16)
    r1 = v - hi.astype(F32)
    mid = r1.astype(BF16)
    lo = (r1 - mid.astype(F32)).astype(BF16)
    return hi, mid, lo


def _rms(v, width):
    return lax.rsqrt(jnp.sum(v * v, axis=1, keepdims=True) * (1.0 / width) + EPS)


def _rms_bwd(dy, xhat, r, g, width):
    u = dy * g
    return r * (u - xhat * (jnp.sum(u * xhat, axis=1, keepdims=True) * (1.0 / width)))


ANY = pl.BlockSpec(memory_space=pl.ANY)


def _place():
    return lax.axis_index("x"), lax.axis_index("y"), lax.axis_index("c")


def _other_chips(x, y):
    return [(1 - x, y), (x, 1 - y), (1 - x, 1 - y)]


def _remote(src, dst, send, recv, j, dev):
    return pltpu.make_async_remote_copy(src_ref=src, dst_ref=dst, send_sem=send.at[j], recv_sem=recv.at[j], device_id=dev, device_id_type=MESH)


class _Exchange:
    def __init__(self, ins, outs, n_remote, n_local, build, in_place=False):
        self.ins, self.outs, self.n_remote, self.n_local, self.build = list(ins), list(outs), n_remote, max(n_local, 1), build
        self.in_place = in_place
        self.n_aliased = len(self.ins)

    def aliases(self, first_in, first_out):
        return {first_in + i: first_out + i for i in range(self.n_aliased)} if self.in_place else {}

    def sems(self):
        return [pltpu.SemaphoreType.DMA((self.n_remote,)), pltpu.SemaphoreType.DMA((self.n_remote,)), pltpu.SemaphoreType.DMA((self.n_local,))]

    def start(self, in_refs, out_refs, sems):
        for cp in self.build(in_refs, out_refs, *sems)[0]:
            cp.start()

    def wait(self, in_refs, out_refs, sems):
        for w in self.build(in_refs, out_refs, *sems)[1]:
            w()


def _gather_exchange(shards, own=True):
    def build(ins, outs, send, recv, lsem):
        x, y, c = _place()
        starts, waits = [], []
        for i, (s, o) in enumerate(zip(ins, outs)):
            if own:
                mine = pltpu.make_async_copy(s, o.at[2 * x + y], lsem.at[i])
                starts.append(mine)
                waits.append(mine.wait)
            for j, (cx, cy) in enumerate(_other_chips(x, y)):
                out = _remote(s, o.at[2 * x + y], send, recv, 3 * i + j, (cx, cy, c))
                starts.append(out)
                waits.append(_remote(s, o.at[2 * cx + cy], send, recv, 3 * i + j, (cx, cy, c)).wait_recv)
                waits.append(out.wait_send)
        return starts, waits

    outs = [jax.ShapeDtypeStruct((N_CHIPS,) + s.shape, s.dtype) for s in shards]
    return _Exchange(shards, outs, 3 * len(shards), len(shards), build)


def _gather_split_exchange(shards):
    n = len(shards)

    def build(ins, outs, send, recv, lsem):
        x, y, c = _place()
        starts, waits, last = [], [], []
        for i, (s, o) in enumerate(zip(ins, outs)):
            hc = s.shape[1] // 2
            mine, other = pl.ds(c * hc, hc), pl.ds((1 - c) * hc, hc)
            for j, (cx, cy) in enumerate(_other_chips(x, y)):
                out = _remote(s.at[:, mine], o.at[2 * x + y, :, mine], send, recv, 3 * i + j, (cx, cy, c))
                landed = o.at[2 * cx + cy, :, mine]
                arrive = _remote(s.at[:, mine], landed, send, recv, 3 * i + j, (cx, cy, c))
                onward = _remote(landed, landed, send, recv, 3 * n + 3 * i + j, (x, y, 1 - c))
                from_sibling = _remote(landed, o.at[2 * cx + cy, :, other], send, recv, 3 * n + 3 * i + j, (x, y, 1 - c))
                starts.append(out)
                waits.append(lambda arrive=arrive, onward=onward: (arrive.wait_recv(), onward.start()))
                last += [from_sibling.wait_recv, onward.wait_send, out.wait_send]
        return starts, waits + last

    outs = [jax.ShapeDtypeStruct((N_CHIPS,) + s.shape, s.dtype) for s in shards]
    return _Exchange(shards, outs, 6 * n, 0, build)


def _swap_exchange(grads):
    def build(ins, outs, send, recv, lsem):
        x, y, c = _place()
        cps = []
        for i, (g, o) in enumerate(zip(ins, outs)):
            hc = g.shape[2] // 2
            cps.append(_remote(g.at[:, :, pl.ds((1 - c) * hc, hc)], o, send, recv, i, (x, y, 1 - c)))
        return cps, [cp.wait for cp in cps]

    outs = [jax.ShapeDtypeStruct((g.shape[0], g.shape[1], g.shape[2] // 2), g.dtype) for g in grads]
    return _Exchange(grads, outs, len(grads), 0, build)


def _scatter_exchange(sums):
    def build(ins, outs, send, recv, lsem):
        x, y, c = _place()
        cps = []
        for i, (s, o) in enumerate(zip(ins, outs)):
            for j, (cx, cy) in enumerate(_other_chips(x, y)):
                cps.append(_remote(s.at[2 * cx + cy], o.at[j], send, recv, 3 * i + j, (cx, cy, c)))
        return cps, [cp.wait for cp in cps]

    outs = [jax.ShapeDtypeStruct((3,) + s.shape[1:], s.dtype) for s in sums]
    return _Exchange(sums, outs, 3 * len(sums), 0, build)


def _join_exchange(bufs):
    def build(ins, outs, send, recv, lsem):
        x, y, c = _place()
        starts, waits = [], []
        for i, (t, o) in enumerate(zip(ins, outs)):
            hc = t.shape[1] // 2
            out = _remote(t.at[:, pl.ds(c * hc, hc)], o.at[:, pl.ds(c * hc, hc)], send, recv, i, (x, y, 1 - c))
            starts.append(out)
            waits += [_remote(t.at[:, pl.ds(c * hc, hc)], o.at[:, pl.ds((1 - c) * hc, hc)], send, recv, i, (x, y, 1 - c)).wait_recv,
                      out.wait_send]
        return starts, waits

    outs = [jax.ShapeDtypeStruct(t.shape, t.dtype) for t in bufs]
    return _Exchange(bufs, outs, len(bufs), 0, build, in_place=True)


def _everyone_exchange(v):
    def build(ins, outs, send, recv, lsem):
        x, y, c = _place()
        me = 4 * x + 2 * y + c
        mine = pltpu.make_async_copy(ins[0], outs[0].at[me], lsem.at[0])
        starts, waits = [mine], [mine.wait]
        for j in range(7):
            fx, fy, fc = (j + 1) >> 2 & 1, (j + 1) >> 1 & 1, (j + 1) & 1
            peer = (x ^ fx, y ^ fy, c ^ fc)
            out = _remote(ins[0], outs[0].at[me], send, recv, j, peer)
            starts.append(out)
            waits += [_remote(ins[0], outs[0].at[4 * peer[0] + 2 * peer[1] + peer[2]], send, recv, j, peer).wait_recv, out.wait_send]
        return starts, waits

    return _Exchange([v], [jax.ShapeDtypeStruct((8,) + v.shape, v.dtype)], 7, 1, build)


def _both(a, b):
    na_in, na_out = len(a.ins), len(a.outs)

    def build(ins, outs, send, recv, lsem):
        sa, wa = a.build(ins[:na_in], outs[:na_out], send.at[pl.ds(0, a.n_remote)], recv.at[pl.ds(0, a.n_remote)],
                         lsem.at[pl.ds(0, a.n_local)])
        sb, wb = b.build(ins[na_in:], outs[na_out:], send.at[pl.ds(a.n_remote, b.n_remote)], recv.at[pl.ds(a.n_remote, b.n_remote)],
                         lsem.at[pl.ds(a.n_local, b.n_local)])
        return sa + sb, wa + wb

    both = _Exchange(a.ins + b.ins, a.outs + b.outs, a.n_remote + b.n_remote, a.n_local + b.n_local, build, in_place=a.in_place)
    both.n_aliased = na_in
    return both


def _run_exchange(ex, name):
    n_in, n_out = len(ex.ins), len(ex.outs)

    def body(*refs):
        ins, outs, sems = refs[:n_in], refs[n_in:n_in + n_out], refs[n_in + n_out:]
        ex.start(ins, outs, sems)
        ex.wait(ins, outs, sems)

    return pl.pallas_call(
        body, name=name, out_shape=tuple(ex.outs), in_specs=[ANY] * n_in, out_specs=tuple([ANY] * n_out),
        scratch_shapes=ex.sems(), input_output_aliases=ex.aliases(0, 0),
        compiler_params=pltpu.CompilerParams(has_side_effects=True),
    )(*ex.ins)


def _sum_devices(rows):
    def body(r_ref, o_ref):
        acc = r_ref[0]
        for d in range(1, 8):
            acc = acc + r_ref[d]
        o_ref[...] = acc

    vm = pl.BlockSpec(memory_space=pltpu.VMEM)
    return pl.pallas_call(body, name="sum_devices", out_shape=jax.ShapeDtypeStruct(rows.shape[1:], rows.dtype),
                          in_specs=[vm], out_specs=vm)(rows)


def _add_half(g, got):
    n, R, C = g.shape
    hc = C // 2

    def body(c_ref, g_ref, r_ref, o_ref):
        o_ref[...] = (g_ref[...] + r_ref[...]).astype(BF16)

    c = lax.axis_index("c")
    return pl.pallas_call(
        body, name="add_half",
        grid_spec=pltpu.PrefetchScalarGridSpec(
            num_scalar_prefetch=1, grid=(n,),
            in_specs=[pl.BlockSpec((1, R, hc), lambda k, c_ref: (k, 0, c_ref[0])),
                      pl.BlockSpec((1, R, hc), lambda k, c_ref: (k, 0, 0))],
            out_specs=pl.BlockSpec((1, R, hc), lambda k, c_ref: (k, 0, 0))),
        out_shape=jax.ShapeDtypeStruct((n, R, hc), BF16),
        compiler_params=_params(("arbitrary",)),
    )(jnp.reshape(c, (1,)).astype(jnp.int32), g, got)


def _sum_slabs(g, got, recv):
    n, R, C = g.shape
    hc = C // 2

    def body(kc_ref, g_ref, s_ref, r_ref, o_ref):
        o_ref[...] = (((g_ref[0] + s_ref[0]) + r_ref[0].astype(F32)) + r_ref[1].astype(F32)) + r_ref[2].astype(F32)

    kc = jnp.stack([2 * lax.axis_index("x") + lax.axis_index("y"), lax.axis_index("c")]).astype(jnp.int32)
    return pl.pallas_call(
        body, name="sum_slabs",
        grid_spec=pltpu.PrefetchScalarGridSpec(
            num_scalar_prefetch=1, grid=(1,),
            in_specs=[pl.BlockSpec((1, R, hc), lambda i, kc_ref: (kc_ref[0], 0, kc_ref[1])),
                      pl.BlockSpec((1, R, hc), lambda i, kc_ref: (kc_ref[0], 0, 0)),
                      pl.BlockSpec((3, R, hc), lambda i, kc_ref: (0, 0, 0))],
            out_specs=pl.BlockSpec((R, hc), lambda i, kc_ref: (0, kc_ref[1]))),
        out_shape=jax.ShapeDtypeStruct((R, C), F32),
        compiler_params=_params(("arbitrary",)),
    )(kc, g, got, recv)


def _row_tile(rows):
    for cand in (256, 184, 176, 144, 128, 64, 32, 16, 8):
        if rows % cand == 0:
            return cand
    return rows


def _in_proj(x, g1, w_in, w_q12, w_k, w_v, gq, gkv, bfg, ct, st, sel, seq, tm):
    T = x.shape[0]
    nsb = seq // tm

    def body(x_ref, g1_ref, win_ref, wq_ref, wk_ref, wv_ref, gq_ref, gkv_ref, b_ref, ct_ref, st_ref, sel_ref,
             h1_ref, qf_ref, kf_ref, vf_ref, qm_ref, km_ref, vm_ref, lat_ref, qn_ref, kvn_ref, carry):
        i = pl.program_id(0)

        @pl.when(i % nsb == 0)
        def _():
            carry[...] = jnp.zeros_like(carry)

        xv = x_ref[...]
        h = (xv * _rms(xv, D_MODEL) * g1_ref[...]).astype(BF16)
        h1_ref[...] = h
        proj = _dot_nt(h, win_ref[...])
        lane = lax.broadcasted_iota(jnp.int32, (tm, LANES), 1)
        low = lane < 64
        misc_a = proj[:, C_MA:C_END]
        misc_b = pltpu.roll(misc_a, 96, 1)

        z = misc_a + b_ref[...]
        lf = jnp.where(lane < HEADS, jnp.minimum(z, 0.0) - jnp.log1p(jnp.exp(-jnp.abs(z))), 0.0)
        rr = lax.broadcasted_iota(jnp.int32, (tm, tm), 0)
        cc = lax.broadcasted_iota(jnp.int32, (tm, tm), 1)
        tri = (rr >= cc).astype(BF16)
        a0, a1, a2 = _split3(lf)
        c = _dot(tri, a0) + _dot(tri, a1) + _dot(tri, a2) + carry[0:1, :]
        carry[0:1, :] = c[tm - 1:tm, :]
        c0, c1, c2 = _split3(c)
        cpl = _dot(jnp.concatenate([c0, c1, c2], axis=1), sel_ref[...])
        qpad = jnp.where((lane >= 64) & (lane < 67), -1.0, 0.0)
        for j in range(PAIRS):
            qc = proj[:, C_FQ + LANES * j:C_FQ + LANES * (j + 1)] * FOX_SCALE
            kc = proj[:, C_FK + LANES * j:C_FK + LANES * (j + 1)]
            e, o = 2 * LANES * j, 2 * LANES * j + LANES
            qf_ref[:, e:e + LANES] = jnp.where(low, qc, qpad).astype(BF16)
            qf_ref[:, o:o + LANES] = jnp.where(low, pltpu.roll(qc, 64, 1), qpad).astype(BF16)
            kf_ref[:, e:e + LANES] = jnp.where(low, kc, cpl[:, e:e + LANES]).astype(BF16)
            kf_ref[:, o:o + LANES] = jnp.where(low, pltpu.roll(kc, 64, 1), cpl[:, o:o + LANES]).astype(BF16)
        vf_ref[...] = proj[:, C_FV:C_QL].astype(BF16)

        ql = proj[:, C_QL:C_KVL]
        kvl = proj[:, C_KVL:C_MA]
        qn = (ql * _rms(ql, Q_RANK) * gq_ref[...]).astype(BF16)
        kvn = (kvl * _rms(kvl, KV_RANK) * gkv_ref[...]).astype(BF16)
        lat_ref[...] = proj[:, C_QL:C_MB]
        qn_ref[...] = qn
        kvn_ref[...] = kvn
        q12 = _dot_nt(qn, wq_ref[...])
        kn = _dot(kvn, wk_ref[...])
        ctv = ct_ref[...]
        stv = st_ref[...]
        cq = (jnp.where(low, 1.0, 0.0) + ctv) * MLA_SCALE
        sq = stv * MLA_SCALE
        kpe = misc_a * ctv + misc_b * stv
        for hd in range(HEADS):
            s0 = LANES * hd
            qm_ref[:, s0:s0 + LANES] = (q12[:, s0:s0 + LANES] * cq + q12[:, 1024 + s0:1024 + s0 + LANES] * sq).astype(BF16)
            km_ref[:, s0:s0 + LANES] = (kn[:, s0:s0 + LANES] + kpe).astype(BF16)
        vm_ref[...] = _dot(kvn, wv_ref[...]).astype(BF16)

    row = lambda w: pl.BlockSpec((tm, w), lambda i: (i, 0))
    out_shape = (
        jax.ShapeDtypeStruct((T, D_MODEL), BF16),
        jax.ShapeDtypeStruct((T, 1024), BF16), jax.ShapeDtypeStruct((T, 1024), BF16), jax.ShapeDtypeStruct((T, 512), BF16),
        jax.ShapeDtypeStruct((T, 1024), BF16), jax.ShapeDtypeStruct((T, 1024), BF16), jax.ShapeDtypeStruct((T, 512), BF16),
        jax.ShapeDtypeStruct((T, 512), F32),
        jax.ShapeDtypeStruct((T, Q_RANK), BF16), jax.ShapeDtypeStruct((T, KV_RANK), BF16),
    )
    return pl.pallas_call(
        body, name="in_proj", grid=(T // tm,), out_shape=out_shape,
        in_specs=[row(D_MODEL), _full(g1.shape), _full(w_in.shape), _full(w_q12.shape), _full(w_k.shape), _full(w_v.shape),
                  _full(gq.shape), _full(gkv.shape), _full(bfg.shape), row(LANES), row(LANES), _full(sel.shape)],
        out_specs=[row(D_MODEL), row(1024), row(1024), row(512), row(1024), row(1024), row(512), row(512), row(Q_RANK), row(KV_RANK)],
        scratch_shapes=[pltpu.VMEM((8, LANES), F32)],
        compiler_params=_params(("arbitrary",)),
    )(x, g1, w_in, w_q12, w_k, w_v, gq, gkv, bfg, ct, st, sel)


def _attn_fwd(q, k, v, nb, seq, tq, name, ex=None):
    T = q.shape[0]
    nq = seq // tq
    n_in, n_out = (len(ex.ins), len(ex.outs)) if ex else (0, 0)

    def body(*refs):
        q_ref, k_ref, v_ref = refs[0:3]
        o_ref, lse_ref = refs[3 + n_in:5 + n_in]
        b, pr, qi = pl.program_id(0), pl.program_id(1), pl.program_id(2)
        if ex:
            ex_refs = (refs[3:3 + n_in], refs[5 + n_in:5 + n_in + n_out], refs[8 + n_in + n_out:])

            @pl.when((b == 0) & (pr == 0) & (qi == 0))
            def _():
                ex.start(*ex_refs)

        s_sc, p_sc, acc_sc = refs[5 + n_in + n_out:8 + n_in + n_out]
        strip = 64
        key_s = lax.broadcasted_iota(jnp.int32, (strip, tq), 0)
        qry_s = lax.broadcasted_iota(jnp.int32, (strip, tq), 1)
        row_t = lax.broadcasted_iota(jnp.int32, (LANES, tq), 0)
        acc_sc[...] = jnp.zeros(acc_sc.shape, F32)

        def fold(x, op):
            out = x[0:8]
            for r in range(8, strip, 8):
                out = op(out, x[r:r + 8])
            return out

        def step(kj, state, masked):
            rows = pl.ds(pl.multiple_of(kj * tq, tq), tq)
            for hh in range(2):
                s_sc[hh] = _dot_nt(k_ref[rows, LANES * hh:LANES * (hh + 1)], q_ref[:, LANES * hh:LANES * (hh + 1)])
            vv = v_ref[rows, :]
            new = []
            for hh in range(2):
                m, l = state[hh]

                def strip_of(r0, hh=hh):
                    s = s_sc[hh, r0:r0 + strip, :]
                    return jnp.where(key_s + r0 <= qry_s, s, NEG) if masked else s

                mx = fold(strip_of(0), jnp.maximum)
                for r0 in range(strip, tq, strip):
                    mx = jnp.maximum(mx, fold(strip_of(r0), jnp.maximum))
                m_new = jnp.maximum(m, jnp.max(mx, axis=0, keepdims=True))
                alpha = jnp.exp(m - m_new)
                sm = jnp.zeros((8, tq), F32)
                for r0 in range(0, tq, strip):
                    p = jnp.exp(strip_of(r0) - m_new)
                    sm = sm + fold(p, jnp.add)
                    p_sc[hh, r0:r0 + strip, :] = p.astype(BF16)
                l = alpha * l + jnp.sum(sm, axis=0, keepdims=True)
                acc_sc[hh] = alpha * acc_sc[hh] + _dot_tn(vv, p_sc[hh])
                new.append((m_new, l))
            return tuple(new)

        one = (jnp.full((1, tq), NEG, F32), jnp.zeros((1, tq), F32))
        state = lax.fori_loop(0, qi, functools.partial(step, masked=False), (one, one))
        (m0, l0), (m1, l1) = step(qi, state, True)
        o_ref[...] = jnp.where(row_t < 64, acc_sc[0] / l0, acc_sc[1] / l1).T
        lse_ref[:, 0:LANES] = jnp.broadcast_to(m0 + jnp.log(l0), (LANES, tq)).T
        lse_ref[:, LANES:2 * LANES] = jnp.broadcast_to(m1 + jnp.log(l1), (LANES, tq)).T

        if ex:
            @pl.when((b == nb - 1) & (pr == PAIRS - 1) & (qi == nq - 1))
            def _():
                ex.wait(*ex_refs)

    res = pl.pallas_call(
        body, name=name, grid=(nb, PAIRS, nq),
        out_shape=(jax.ShapeDtypeStruct((T, 512), F32), jax.ShapeDtypeStruct((T, 1024), F32)) + tuple(ex.outs if ex else ()),
        in_specs=[pl.BlockSpec((tq, 2 * LANES), lambda b, p, i: (b * nq + i, p)),
                  pl.BlockSpec((seq, 2 * LANES), lambda b, p, i: (b, p)),
                  pl.BlockSpec((seq, LANES), lambda b, p, i: (b, p))] + [ANY] * n_in,
        out_specs=[pl.BlockSpec((tq, LANES), lambda b, p, i: (b * nq + i, p)),
                   pl.BlockSpec((tq, 2 * LANES), lambda b, p, i: (b * nq + i, p))] + [ANY] * n_out,
        scratch_shapes=[pltpu.VMEM((2, tq, tq), F32), pltpu.VMEM((2, tq, tq), BF16), pltpu.VMEM((2, LANES, tq), F32)]
        + (ex.sems() if ex else []),
        compiler_params=_params(("arbitrary", "arbitrary", "arbitrary")),
    )(q, k, v, *(ex.ins if ex else ()))
    return res[0], res[1], list(res[2:])


def _attn_bwd(q, k, v, o, do, lse, nb, seq, tq, name, key_bias, ex=None):
    T = q.shape[0]
    nq = seq // tq
    n_in, n_out = (len(ex.ins), len(ex.outs)) if ex else (0, 0)

    def body(*refs):
        q_ref, k_ref, v_ref, o_ref, do_ref, lse_ref = refs[0:6]
        dq_ref, dk_ref, dv_ref = refs[6 + n_in:9 + n_in]
        dsc, rsum = refs[9 + n_in + n_out:11 + n_in + n_out]
        b, pr, kj = pl.program_id(0), pl.program_id(1), pl.program_id(2)
        if ex:
            ex_refs = (refs[6:6 + n_in], refs[9 + n_in:9 + n_in + n_out], refs[11 + n_in + n_out:])

            @pl.when((b == 0) & (pr == 0) & (kj == 0))
            def _():
                ex.start(*ex_refs)

        lane_s = lax.broadcasted_iota(jnp.int32, (seq, LANES), 1)
        lane = lax.broadcasted_iota(jnp.int32, (tq, LANES), 1)
        rr = lax.broadcasted_iota(jnp.int32, (tq, tq), 0)
        cc = lax.broadcasted_iota(jnp.int32, (tq, tq), 1)

        @pl.when(kj == 0)
        def _():
            dq_ref[...] = jnp.zeros_like(dq_ref)
            prod = do_ref[...].astype(F32) * o_ref[...]
            d0 = jnp.sum(jnp.where(lane_s < 64, prod, 0.0), axis=1, keepdims=True)
            d1 = jnp.sum(jnp.where(lane_s < 64, 0.0, prod), axis=1, keepdims=True)
            dsc[0] = jnp.broadcast_to(d0, (seq, LANES))
            dsc[1] = jnp.broadcast_to(d1, (seq, LANES))
            if key_bias:
                rsum[...] = jnp.zeros_like(rsum)

        vv = v_ref[...]

        def step(qi, carry, masked):
            dkt, dvt, cols = carry
            rows = pl.ds(pl.multiple_of(qi * tq, tq), tq)
            dov = do_ref[rows, :]
            new_dkt, new_cols = [], []
            for hh in range(2):
                qv = q_ref[rows, LANES * hh:LANES * (hh + 1)]
                kv = k_ref[:, LANES * hh:LANES * (hh + 1)]
                dom = jnp.where((lane < 64) if hh == 0 else (lane >= 64), dov, jnp.zeros((), BF16))
                s = _dot_nt(qv, kv)
                if masked:
                    s = jnp.where(cc <= rr, s, NEG)
                p = jnp.exp(s - lse_ref[rows, LANES * hh:LANES * hh + 1])
                dp = _dot_nt(dom, vv)
                ds32 = p * (dp - dsc[hh, rows, 0:1])
                col = cols[hh]
                if key_bias:
                    col = col + jnp.sum(ds32, axis=0, keepdims=True)
                    rsum[hh, rows, :] += jnp.broadcast_to(jnp.sum(ds32, axis=1, keepdims=True), (tq, LANES))
                ds = ds32.astype(BF16)
                dvt = dvt + _dot_tn(dom, p.astype(BF16))
                new_dkt.append(dkt[hh] + _dot_tn(qv, ds))
                new_cols.append(col)
                dq_ref[rows, LANES * hh:LANES * (hh + 1)] += _dot(ds, kv)
            return tuple(new_dkt), dvt, tuple(new_cols)

        zt = jnp.zeros((LANES, tq), F32)
        zc = jnp.zeros((1, tq), F32)
        carry = step(kj, ((zt, zt), zt, (zc, zc)), True)
        dkt, dvt, cols = lax.fori_loop(kj + 1, nq, functools.partial(step, masked=False), carry)
        row_t = lax.broadcasted_iota(jnp.int32, (LANES, tq), 0)
        for hh in range(2):
            dk_h = jnp.where(row_t == 64, -cols[hh], dkt[hh]) if key_bias else dkt[hh]
            dk_ref[:, LANES * hh:LANES * (hh + 1)] = dk_h.T
        dv_ref[...] = dvt.T

        if key_bias:
            @pl.when(kj == nq - 1)
            def _():
                for hh in range(2):
                    blk = dq_ref[:, LANES * hh:LANES * (hh + 1)]
                    dq_ref[:, LANES * hh:LANES * (hh + 1)] = jnp.where(lane_s == 64, rsum[hh], blk)

        if ex:
            @pl.when((b == nb - 1) & (pr == PAIRS - 1) & (kj == nq - 1))
            def _():
                ex.wait(*ex_refs)

    per_seq = lambda w: pl.BlockSpec((seq, w), lambda b, p, j: (b, p))
    per_blk = lambda w: pl.BlockSpec((tq, w), lambda b, p, j: (b * nq + j, p))
    res = pl.pallas_call(
        body, name=name, grid=(nb, PAIRS, nq),
        out_shape=(jax.ShapeDtypeStruct((T, 1024), F32), jax.ShapeDtypeStruct((T, 1024), F32), jax.ShapeDtypeStruct((T, 512), F32))
        + tuple(ex.outs if ex else ()),
        in_specs=[per_seq(2 * LANES), per_blk(2 * LANES), per_blk(LANES), per_seq(LANES), per_seq(LANES), per_seq(2 * LANES)] + [ANY] * n_in,
        out_specs=[per_seq(2 * LANES), per_blk(2 * LANES), per_blk(LANES)] + [ANY] * n_out,
        scratch_shapes=[pltpu.VMEM((2, seq, LANES), F32), pltpu.VMEM((2, seq, LANES) if key_bias else (2, 8, LANES), F32)]
        + (ex.sems() if ex else []),
        compiler_params=_params(("arbitrary", "arbitrary", "arbitrary")),
    )(q, k, v, o, do, lse, *(ex.ins if ex else ()))
    return res[0], res[1], res[2], list(res[3:])


def _mid(of, om, x, tgt, g_fo, g_mo, g2, g3, w_o, w_g, w_u, w_d, tm):
    T = x.shape[0]

    def body(of_ref, om_ref, x_ref, t_ref, gfo_ref, gmo_ref, g2_ref, g3_ref, wo_ref, wg_ref, wu_ref, wd_ref,
             a_ref, h2_ref, hid_ref, dg_ref, du_ref, dx3_ref, dx2_ref, dof_ref, dom_ref, st_ref):
        i = pl.program_id(0)

        @pl.when(i == 0)
        def _():
            st_ref[...] = jnp.zeros_like(st_ref)

        ofv, omv = of_ref[...], om_ref[...]
        rf, rm = _rms(ofv, FOX_W), _rms(omv, FOX_W)
        fhat, mhat = ofv * rf, omv * rm
        a = jnp.concatenate([fhat * gfo_ref[...], mhat * gmo_ref[...]], axis=1).astype(BF16)
        a_ref[...] = a
        x2 = x_ref[...] + _dot(a, wo_ref[...])
        r2 = _rms(x2, D_MODEL)
        xh2 = x2 * r2
        h2 = (xh2 * g2_ref[...]).astype(BF16)
        h2_ref[...] = h2
        gt = _dot_nt(h2, wg_ref[...])
        up = _dot_nt(h2, wu_ref[...])
        sg = jax.nn.sigmoid(gt)
        sl = gt * sg
        hid = (sl * up).astype(BF16)
        hid_ref[...] = hid
        x3 = x2 + _dot(hid, wd_ref[...])
        r3 = _rms(x3, D_MODEL)
        xh3 = x3 * r3
        diff = xh3 * g3_ref[...] - t_ref[...]
        dy = diff * (1.0 / D_MODEL)
        st_ref[3:4, :] += jnp.sum(diff * diff, axis=0, keepdims=True) * (0.5 / D_MODEL)
        st_ref[0:1, :] += jnp.sum(dy * xh3, axis=0, keepdims=True)
        dx3 = _rms_bwd(dy, xh3, r3, g3_ref[...], D_MODEL)
        dx3b = dx3.astype(BF16)
        dx3_ref[...] = dx3b
        dhid = _dot_nt(dx3b, wd_ref[...])
        dg = (dhid * up * (sg * (1.0 + gt * (1.0 - sg)))).astype(BF16)
        du = (dhid * sl).astype(BF16)
        dg_ref[...] = dg
        du_ref[...] = du
        dh2 = _dot(dg, wg_ref[...]) + _dot(du, wu_ref[...])
        st_ref[1:2, :] += jnp.sum(dh2 * xh2, axis=0, keepdims=True)
        dx2 = dx3 + _rms_bwd(dh2, xh2, r2, g2_ref[...], D_MODEL)
        dx2_ref[...] = dx2
        da = _dot_nt(dx2.astype(BF16), wo_ref[...])
        daf, dam = da[:, 0:FOX_W], da[:, FOX_W:2 * FOX_W]
        st_ref[2:3, 0:FOX_W] += jnp.sum(daf * fhat, axis=0, keepdims=True)
        st_ref[2:3, FOX_W:2 * FOX_W] += jnp.sum(dam * mhat, axis=0, keepdims=True)
        dof_ref[...] = _rms_bwd(daf, fhat, rf, gfo_ref[...], FOX_W).astype(BF16)
        dom_ref[...] = _rms_bwd(dam, mhat, rm, gmo_ref[...], FOX_W).astype(BF16)

    row = lambda w: pl.BlockSpec((tm, w), lambda i: (i, 0))
    ff = jax.ShapeDtypeStruct((T, D_FF), BF16)
    out_shape = (
        jax.ShapeDtypeStruct((T, 1024), BF16), jax.ShapeDtypeStruct((T, 1024), BF16), ff, ff, ff,
        jax.ShapeDtypeStruct((T, 1024), BF16), jax.ShapeDtypeStruct((T, 1024), F32),
        jax.ShapeDtypeStruct((T, 512), BF16), jax.ShapeDtypeStruct((T, 512), BF16), jax.ShapeDtypeStruct((8, 1024), F32),
    )
    return pl.pallas_call(
        body, name="mid", grid=(T // tm,), out_shape=out_shape,
        in_specs=[row(512), row(512), row(1024), row(1024), _full(g_fo.shape), _full(g_mo.shape), _full(g2.shape), _full(g3.shape),
                  _full(w_o.shape), _full(w_g.shape), _full(w_u.shape), _full(w_d.shape)],
        out_specs=[row(1024), row(1024), row(D_FF), row(D_FF), row(D_FF), row(1024), row(1024), row(512), row(512),
                   pl.BlockSpec((8, 1024), lambda i: (0, 0))],
        compiler_params=_params(("arbitrary",)),
    )(of, om, x, tgt, g_fo, g_mo, g2, g3, w_o, w_g, w_u, w_d)


def _in_bwd(dqf, dkf, dvf, dqm, dkm, dvm, lat, x, dx2, g1, gq, gkv, bfg, ct, st, sel_t, w_in, w_q12, w_kv, seq, tm, ex=None):
    T = x.shape[0]
    nblk = T // tm
    nsb = seq // tm
    n_in, n_out = (len(ex.ins), len(ex.outs)) if ex else (0, 0)

    def body(*refs):
        (dqf_ref, dkf_ref, dvf_ref, dqm_ref, dkm_ref, dvm_ref, lat_ref, x_ref, dx2_ref, g1_ref, gq_ref, gkv_ref, b_ref,
         ct_ref, st_ref, selt_ref, win_ref, wq_ref, wkv_ref) = refs[0:19]
        dx_ref, dproj_ref, dq12_ref, dkv_ref, stat_ref = refs[19 + n_in:24 + n_in]
        carry = refs[24 + n_in + n_out]
        i = pl.program_id(0)
        if ex:
            ex_refs = (refs[19:19 + n_in], refs[24 + n_in:24 + n_in + n_out], refs[25 + n_in + n_out:])

            @pl.when(i == 0)
            def _():
                ex.start(*ex_refs)

        @pl.when(i == 0)
        def _():
            stat_ref[...] = jnp.zeros_like(stat_ref)

        @pl.when(i % nsb == 0)
        def _():
            carry[...] = jnp.zeros_like(carry)

        lane = lax.broadcasted_iota(jnp.int32, (tm, LANES), 1)
        low = lane < 64
        ctv, stv = ct_ref[...], st_ref[...]

        for j in range(PAIRS):
            e, o = 2 * LANES * j, 2 * LANES * j + LANES
            dq = jnp.where(low, dqf_ref[:, e:e + LANES], 0.0) + pltpu.roll(jnp.where(low, dqf_ref[:, o:o + LANES], 0.0), 64, 1)
            dk = jnp.where(low, dkf_ref[:, e:e + LANES], 0.0) + pltpu.roll(jnp.where(low, dkf_ref[:, o:o + LANES], 0.0), 64, 1)
            dproj_ref[:, C_FQ + LANES * j:C_FQ + LANES * (j + 1)] = (dq * FOX_SCALE).astype(BF16)
            dproj_ref[:, C_FK + LANES * j:C_FK + LANES * (j + 1)] = dk.astype(BF16)
        dproj_ref[:, C_FV:C_QL] = dvf_ref[...].astype(BF16)
        dcv = dkf_ref[...] + dqf_ref[...]
        k_hi = dcv.astype(BF16)
        k_lo = (dcv - k_hi.astype(F32)).astype(BF16)
        dc = _dot(k_hi, selt_ref[...]) + _dot(k_lo, selt_ref[...])
        rr = lax.broadcasted_iota(jnp.int32, (tm, tm), 0)
        cc = lax.broadcasted_iota(jnp.int32, (tm, tm), 1)
        triu = (cc >= rr).astype(BF16)
        a0, a1, a2 = _split3(dc)
        dlf = _dot(triu, a0) + _dot(triu, a1) + _dot(triu, a2) + carry[0:1, :]
        carry[0:1, :] = dlf[0:1, :]
        misc_a = lat_ref[:, Q_RANK + KV_RANK:Q_RANK + KV_RANK + LANES]
        z = misc_a + b_ref[...]
        dz = jnp.where(lane < HEADS, dlf * jax.nn.sigmoid(-z), 0.0)
        stat_ref[3:4, 0:LANES] += jnp.sum(dz, axis=0, keepdims=True)

        cq = (jnp.where(low, 1.0, 0.0) + ctv) * MLA_SCALE
        sq = stv * MLA_SCALE
        dkpe = jnp.zeros((tm, LANES), F32)
        for hd in range(HEADS):
            s0 = LANES * hd
            dqh = dqm_ref[:, s0:s0 + LANES]
            dq12_ref[:, s0:s0 + LANES] = (dqh * cq).astype(BF16)
            dq12_ref[:, 1024 + s0:1024 + s0 + LANES] = (dqh * sq).astype(BF16)
            dkpe = dkpe + dkm_ref[:, s0:s0 + LANES]
        dkv_ref[:, 0:1024] = dkm_ref[...].astype(BF16)
        dkv_ref[:, 1024:1536] = dvm_ref[...].astype(BF16)
        dproj_ref[:, C_MA:C_END] = (dz + dkpe * ctv + pltpu.roll(dkpe * stv, 32, 1)).astype(BF16)
        dqn = _dot(dq12_ref[...], wq_ref[...])
        dkvn = _dot_nt(dkv_ref[...], wkv_ref[...])
        ql = lat_ref[:, 0:Q_RANK]
        kvl = lat_ref[:, Q_RANK:Q_RANK + KV_RANK]
        rq, rkv = _rms(ql, Q_RANK), _rms(kvl, KV_RANK)
        qhat, kvhat = ql * rq, kvl * rkv
        stat_ref[1:2, 0:Q_RANK] += jnp.sum(dqn * qhat, axis=0, keepdims=True)
        stat_ref[2:3, 0:KV_RANK] += jnp.sum(dkvn * kvhat, axis=0, keepdims=True)
        dproj_ref[:, C_QL:C_KVL] = _rms_bwd(dqn, qhat, rq, gq_ref[...], Q_RANK).astype(BF16)
        dproj_ref[:, C_KVL:C_MA] = _rms_bwd(dkvn, kvhat, rkv, gkv_ref[...], KV_RANK).astype(BF16)

        dh1 = _dot(dproj_ref[...], win_ref[...])
        xv = x_ref[...]
        r1 = _rms(xv, D_MODEL)
        xh = xv * r1
        stat_ref[0:1, :] += jnp.sum(dh1 * xh, axis=0, keepdims=True)
        dx_ref[...] = dx2_ref[...] + _rms_bwd(dh1, xh, r1, g1_ref[...], D_MODEL)

        if ex:
            @pl.when(i == nblk - 1)
            def _():
                ex.wait(*ex_refs)

    rev = lambda w: pl.BlockSpec((tm, w), lambda i: (nblk - 1 - i, 0))
    out_shape = (
        jax.ShapeDtypeStruct((T, 1024), F32), jax.ShapeDtypeStruct((T, C_END), BF16), jax.ShapeDtypeStruct((T, 2048), BF16),
        jax.ShapeDtypeStruct((T, 1536), BF16), jax.ShapeDtypeStruct((8, 1024), F32),
    ) + tuple(ex.outs if ex else ())
    res = pl.pallas_call(
        body, name="in_bwd", grid=(nblk,), out_shape=out_shape,
        in_specs=[rev(1024), rev(1024), rev(512), rev(1024), rev(1024), rev(512), rev(512), rev(1024), rev(1024),
                  _full(g1.shape), _full(gq.shape), _full(gkv.shape), _full(bfg.shape), rev(LANES), rev(LANES), _full(sel_t.shape),
                  _full(w_in.shape), _full(w_q12.shape), _full(w_kv.shape)] + [ANY] * n_in,
        out_specs=[rev(1024), rev(C_END), rev(2048), rev(1536), pl.BlockSpec((8, 1024), lambda i: (0, 0))] + [ANY] * n_out,
        scratch_shapes=[pltpu.VMEM((8, LANES), F32)] + (ex.sems() if ex else []),
        compiler_params=_params(("arbitrary",)),
    )(dqf, dkf, dvf, dqm, dkm, dvm, lat, x, dx2, g1, gq, gkv, bfg, ct, st, sel_t, w_in, w_q12, w_kv, *(ex.ins if ex else ()))
    return res[0], res[1], res[2], res[3], res[4], list(res[5:])


def _wgrad(a, b, tk, tt, name, ex=None):
    T, K = a.shape
    N = b.shape[1]
    n_in, n_out = (len(ex.ins), len(ex.outs)) if ex else (0, 0)
    gk, gt = K // tk, T // tt

    def body(*refs):
        a_ref, b_ref, o_ref = refs[0], refs[1], refs[2 + n_in]
        kb, t = pl.program_id(0), pl.program_id(1)
        if ex:
            ex_refs = (refs[2:2 + n_in], refs[3 + n_in:3 + n_in + n_out], refs[3 + n_in + n_out:])

            @pl.when((kb == 0) & (t == 0))
            def _():
                ex.start(*ex_refs)

        @pl.when(t == 0)
        def _():
            o_ref[...] = jnp.zeros_like(o_ref)

        o_ref[...] += _dot_tn(a_ref[...].astype(BF16), b_ref[...].astype(BF16))

        if ex:
            @pl.when((kb == gk - 1) & (t == gt - 1))
            def _():
                ex.wait(*ex_refs)

    res = pl.pallas_call(
        body, name=name, grid=(gk, gt), out_shape=(jax.ShapeDtypeStruct((K, N), F32),) + tuple(ex.outs if ex else ()),
        in_specs=[pl.BlockSpec((tt, tk), lambda kb, t: (t, kb)), pl.BlockSpec((tt, N), lambda kb, t: (t, 0))] + [ANY] * n_in,
        out_specs=[pl.BlockSpec((tk, N), lambda kb, t: (kb, 0))] + [ANY] * n_out,
        scratch_shapes=ex.sems() if ex else [], input_output_aliases=ex.aliases(2, 1) if ex else {},
        compiler_params=_params(("arbitrary", "arbitrary")),
    )(a, b, *(ex.ins if ex else ()))
    return (res[0], list(res[1:])) if ex else res[0]


def _adamw(tensors, name, ex=None):
    n = len(tensors)
    R, C = tensors[0][0].shape
    tr = _row_tile(R)
    steps = R // tr
    n_in, n_out = (len(ex.ins), len(ex.outs)) if ex else (0, 0)

    def body(*refs):
        i = pl.program_id(0)
        if ex:
            ex_refs = (refs[4 * n:4 * n + n_in], refs[7 * n + n_in:7 * n + n_in + n_out], refs[7 * n + n_in + n_out:])

            @pl.when(i == 0)
            def _():
                ex.start(*ex_refs)

        for t in range(n):
            w_ref, g_ref, m_ref, v_ref = refs[4 * t:4 * t + 4]
            d_ref, nm_ref, nv_ref = refs[4 * n + n_in + 3 * t:4 * n + n_in + 3 * t + 3]
            gv = g_ref[...]
            nm = ADAM_B1 * m_ref[...] + (1.0 - ADAM_B1) * gv
            nv = ADAM_B2 * v_ref[...] + (1.0 - ADAM_B2) * (gv * gv)
            m_hat = nm / (1.0 - ADAM_B1 ** ADAM_STEP)
            v_hat = nv / (1.0 - ADAM_B2 ** ADAM_STEP)
            d_ref[...] = -ADAM_LR * (m_hat / (jnp.sqrt(v_hat) + ADAM_EPS) + ADAM_WD * w_ref[...])
            nm_ref[...] = nm
            nv_ref[...] = nv

        if ex:
            @pl.when(i == steps - 1)
            def _():
                ex.wait(*ex_refs)

    blk = pl.BlockSpec((tr, C), lambda i: (i, 0))
    sh = jax.ShapeDtypeStruct((R, C), F32)
    res = pl.pallas_call(
        body, name=name, grid=(steps,), out_shape=(sh,) * (3 * n) + tuple(ex.outs if ex else ()),
        in_specs=[blk] * (4 * n) + [ANY] * n_in, out_specs=[blk] * (3 * n) + [ANY] * n_out,
        scratch_shapes=ex.sems() if ex else [],
        compiler_params=_params(("arbitrary",)),
    )(*[a for t in tensors for a in t], *(ex.ins if ex else ()))
    out = [tuple(res[3 * t:3 * t + 3]) for t in range(n)]
    return (out, list(res[3 * n:])) if ex else out


def _arrange(win_t, wuq_t, wukv):
    dt = win_t.dtype
    z = lambda r: jnp.zeros((r, D_MODEL), dt)
    zh = lambda r: jnp.zeros((HEADS, r, Q_RANK), dt)
    kr1, kr2 = win_t[1928:1944], win_t[1944:1960]
    misc = jnp.concatenate([win_t[1536:1544], z(56), kr1, kr2, kr2, kr1], axis=0)
    w_in = jnp.concatenate([win_t[0:1536], win_t[1544:1928], misc], axis=0)
    wq = wuq_t.reshape(HEADS, 96, Q_RANK)
    q1 = jnp.concatenate([wq, zh(32)], axis=1).reshape(1024, Q_RANK)
    q2 = jnp.concatenate([zh(64), wq[:, 80:96], wq[:, 64:80], zh(32)], axis=1).reshape(1024, Q_RANK)
    wkv = wukv.reshape(KV_RANK, HEADS, 128)
    wk = jnp.concatenate([wkv[:, :, 0:64], jnp.zeros((KV_RANK, HEADS, 64), dt)], axis=2).reshape(KV_RANK, 1024)
    wv = wkv[:, :, 64:128].reshape(KV_RANK, 512)
    return dict(w_in=w_in, w_q12=jnp.concatenate([q1, q2], axis=0), w_k=wk, w_v=wv, w_kv=jnp.concatenate([wk, wv], axis=1))


def _unarrange(g_in, g_q12, g_kv):
    kr1 = g_in[C_MA + 64:C_MA + 80] + g_in[C_MA + 112:C_MA + 128]
    kr2 = g_in[C_MA + 80:C_MA + 96] + g_in[C_MA + 96:C_MA + 112]
    win_t = jnp.concatenate([g_in[0:1536], g_in[C_MA:C_MA + 8], g_in[1536:1920], kr1, kr2], axis=0)
    g1 = g_q12[0:1024].reshape(HEADS, 128, Q_RANK)
    g2 = g_q12[1024:2048].reshape(HEADS, 128, Q_RANK)
    wuq_t = jnp.concatenate([g1[:, 0:64], g1[:, 64:80] + g2[:, 80:96], g1[:, 80:96] + g2[:, 64:80]], axis=1).reshape(768, Q_RANK)
    gk = g_kv[:, 0:1024].reshape(KV_RANK, HEADS, 128)
    gv = g_kv[:, 1024:1536].reshape(KV_RANK, HEADS, 64)
    wukv = jnp.concatenate([gk[:, :, 0:64], gv], axis=2).reshape(KV_RANK, 1024)
    return win_t, wuq_t, wukv


def _selectors():
    sel = np.zeros((384, 1024), np.float32)
    sel_t = np.zeros((1024, LANES), np.float32)
    for h in range(HEADS):
        for piece in range(3):
            sel[LANES * piece + h, LANES * h + 64 + piece] = 1.0
        sel_t[LANES * h + 64, h] = 1.0
    return jnp.asarray(sel, BF16), jnp.asarray(sel_t, BF16)


def _rope_tables(positions):
    inv_freq = 10000.0 ** (-jnp.arange(0, ROPE, 2, dtype=F32) / ROPE)
    n = positions.size
    ang = (positions.reshape(n // 8, 8, 1).astype(F32) * inv_freq[None, None, :]).reshape(n // 8, 8 * (ROPE // 2))
    cos, sin = lax.optimization_barrier((jnp.cos(lax.optimization_barrier(ang)), jnp.sin(lax.optimization_barrier(ang))))
    cos, sin = cos.reshape(n, ROPE // 2), sin.reshape(n, ROPE // 2)
    z64, z32 = jnp.zeros((n, 64), F32), jnp.zeros((n, 32), F32)
    return jnp.concatenate([z64, cos, cos, z32], axis=1), jnp.concatenate([z64, -sin, sin, z32], axis=1)


def _work(name, t):
    return jnp.swapaxes(t[0], 0, 1) if name in TRANSPOSED else t[0]


def _back(name, t):
    return (jnp.swapaxes(t, 0, 1) if name in TRANSPOSED else t)[None]


SMALL_ROWS = {"norm_mix_g": (0, 1024), "norm_ffn_g": (1, 1024), "final_norm_g": (2, 1024), "q_norm_g": (4, 256),
              "kv_norm_g": (5, 128), "b_fgate": (6, 8)}


def kernel(x, positions, norm_mix_g, w_in, b_fgate, q_norm_g, w_uq, kv_norm_g, w_ukv, fox_out_g, mla_out_g, w_o, norm_ffn_g, w_gate, w_up, w_down, final_norm_g, loss_target, m_norm_mix_g, m_w_in, m_b_fgate, m_q_norm_g, m_w_uq, m_kv_norm_g, m_w_ukv, m_fox_out_g, m_mla_out_g, m_w_o, m_norm_ffn_g, m_w_gate, m_w_up, m_w_down, m_final_norm_g, v_norm_mix_g, v_w_in, v_b_fgate, v_q_norm_g, v_w_uq, v_kv_norm_g, v_w_ukv, v_fox_out_g, v_mla_out_g, v_w_o, v_norm_ffn_g, v_w_gate, v_w_up, v_w_down, v_final_norm_g):
    names = ["norm_mix_g", "w_in", "b_fgate", "q_norm_g", "w_uq", "kv_norm_g", "w_ukv", "fox_out_g", "mla_out_g", "w_o",
             "norm_ffn_g", "w_gate", "w_up", "w_down", "final_norm_g"]
    wts = dict(zip(names, [norm_mix_g, w_in, b_fgate, q_norm_g, w_uq, kv_norm_g, w_ukv, fox_out_g, mla_out_g, w_o, norm_ffn_g,
                           w_gate, w_up, w_down, final_norm_g]))
    mom = dict(zip(names, [m_norm_mix_g, m_w_in, m_b_fgate, m_q_norm_g, m_w_uq, m_kv_norm_g, m_w_ukv, m_fox_out_g, m_mla_out_g,
                           m_w_o, m_norm_ffn_g, m_w_gate, m_w_up, m_w_down, m_final_norm_g]))
    var = dict(zip(names, [v_norm_mix_g, v_w_in, v_b_fgate, v_q_norm_g, v_w_uq, v_kv_norm_g, v_w_ukv, v_fox_out_g, v_mla_out_g,
                           v_w_o, v_norm_ffn_g, v_w_gate, v_w_up, v_w_down, v_final_norm_g]))
    shard = {n: _work(n, wts[n]) for n in HEAD3 + FFN4}
    nb, seq, _ = x.shape
    T = nb * seq
    tm, tq = min(ROW_TILE, seq), min(ATTN_TILE, seq)
    tt = min(WGRAD_TILE, T)
    xf = x.reshape(T, D_MODEL)
    tgt = loss_target.reshape(T, D_MODEL)
    chip = 2 * lax.axis_index("x") + lax.axis_index("y")

    mine = [shard[n].astype(BF16) for n in HEAD3]
    head = _run_exchange(_gather_split_exchange(mine), "gather_head")
    win4, wuq4, wukv4 = [lax.dynamic_update_slice(h, s[None], (chip, 0, 0)) for h, s in zip(head, mine)]
    a = _arrange(win4.reshape(-1, D_MODEL), wuq4.reshape(-1, Q_RANK), wukv4.transpose(1, 0, 2).reshape(KV_RANK, -1))
    sel, sel_t = _selectors()
    ct, st = _rope_tables(positions)
    bfg = jnp.concatenate([b_fgate, jnp.zeros((1, LANES - HEADS), F32)], axis=1)
    g1, gq, gkv = norm_mix_g, q_norm_g, kv_norm_g

    h1, qf, kf, vf, qm, km, vm, lat, qn, kvn = _in_proj(xf, g1, a["w_in"], a["w_q12"], a["w_k"], a["w_v"], gq, gkv, bfg, ct, st, sel, seq, tm)
    tqf = min(ATTN_FWD_TILE, seq)
    of, lse_f, (wo4, wg4) = _attn_fwd(qf, kf, vf, nb, seq, tqf, "fox_fwd", _gather_exchange([shard[n].astype(BF16) for n in FFN4[:2]]))
    om, lse_m, (wu4, wd4) = _attn_fwd(qm, km, vm, nb, seq, tqf, "mla_fwd", _gather_exchange([shard[n].astype(BF16) for n in FFN4[2:]]))
    a_cat, h2, hid, dg, du, dx3, dx2, dof, dom, st_mid = _mid(
        of, om, xf, tgt, fox_out_g, mla_out_g, norm_ffn_g, final_norm_g.reshape(1, D_MODEL),
        wo4.reshape(D_MODEL, D_MODEL), wg4.reshape(D_FF, D_MODEL), wu4.reshape(D_FF, D_MODEL), wd4.reshape(D_FF, D_MODEL), tm)

    slab = lambda g: g.reshape(N_CHIPS, g.shape[0] // N_CHIPS, g.shape[1])
    big = [slab(_wgrad(a_cat, dx2, D_MODEL, tt, "wgrad_o")), slab(_wgrad(dg, h2, D_FF // 2, tt, "wgrad_gate")),
           slab(_wgrad(du, h2, D_FF // 2, tt, "wgrad_up")), slab(_wgrad(hid, dx3, D_FF // 2, tt, "wgrad_down"))]
    dqf, dkf, dvf, got = _attn_bwd(qf, kf, vf, of, dof, lse_f, nb, seq, tq, "fox_bwd", True, _swap_exchange(big))
    sums = [_add_half(g, s) for g, s in zip(big, got)]
    dqm, dkm, dvm, recv = _attn_bwd(qm, km, vm, om, dom, lse_m, nb, seq, tq, "mla_bwd", False, _scatter_exchange(sums))
    halves = [_sum_slabs(g, s, r) for g, s, r in zip(big, got, recv)]
    dx, dproj, dq12, dkv, st_in, _ = _in_bwd(dqf, dkf, dvf, dqm, dkm, dvm, lat, xf, dx2, g1, gq, gkv, bfg, ct, st, sel_t,
                                             a["w_in"], a["w_q12"], a["w_kv"], seq, tm)
    loss_row = jnp.concatenate([jnp.sum(st_mid[3:4, :], axis=1, keepdims=True), jnp.zeros((1, D_MODEL - 1), F32)], axis=1)
    stats = jnp.concatenate([st_in[0:1], st_mid[1:2], st_mid[0:1], st_mid[2:3], st_in[1:2], st_in[2:3], st_in[3:4], loss_row], axis=0)
    g_in, results = _wgrad(dproj, h1, C_END, tt, "wgrad_in", _both(_join_exchange(halves), _everyone_exchange(stats)))
    gshard = dict(zip(FFN4, results[:4]))
    stats = _sum_devices(results[4])

    gwin_t, gwuq_t, gwukv = _unarrange(g_in, _wgrad(dq12, qn, 2048, tt, "wgrad_uq"), _wgrad(kvn, dkv, KV_RANK, tt, "wgrad_ukv"))
    tail = [slab(gwin_t), slab(gwuq_t), gwukv.reshape(KV_RANK, N_CHIPS, -1).transpose(1, 0, 2)]
    tail_got = _run_exchange(_swap_exchange(tail), "tail_swap")
    tail_sums = [_add_half(g, s) for g, s in zip(tail, tail_got)]
    tail_recv = _run_exchange(_scatter_exchange(tail_sums), "tail_scatter")
    tail_joined = _run_exchange(_join_exchange([_sum_slabs(g, s, r) for g, s, r in zip(tail, tail_got, tail_recv)]), "tail_join")
    gshard.update(zip(HEAD3, tail_joined))
    quad = lambda n: (shard[n], gshard[n], _work(n, mom[n]), _work(n, var[n]))
    updates = dict(zip(FFN4[1:], _adamw([quad(n) for n in FFN4[1:]], "adamw_ffn")))
    for n in HEAD3 + FFN4[:1]:
        updates[n], = _adamw([quad(n)], "adamw_" + n)

    grads, delta, new_m, new_v = {}, {}, {}, {}
    for n in HEAD3 + FFN4:
        grads[n] = _back(n, gshard[n])
        delta[n], new_m[n], new_v[n] = [_back(n, t) for t in updates[n]]
    sm_g = {n: stats[row:row + 1, 0:width] for n, (row, width) in SMALL_ROWS.items()}
    sm_g["fox_out_g"] = stats[3:4, 0:512]
    sm_g["mla_out_g"] = stats[3:4, 512:1024]
    pad = lambda t: jnp.pad(t.reshape(1, -1), ((0, 0), (0, 1024 - t.size)))
    stack = lambda d: jnp.concatenate([pad(d[n]) for n in SMALL], axis=0)
    (sd, sm, sv), = _adamw([(stack(wts), stack(sm_g), stack(mom), stack(var))], "adamw_small")
    for i, n in enumerate(SMALL):
        shp = wts[n].shape
        grads[n] = sm_g[n].reshape(shp)
        delta[n] = sd[i, 0:wts[n].size].reshape(shp)
        new_m[n] = sm[i, 0:wts[n].size].reshape(shp)
        new_v[n] = sv[i, 0:wts[n].size].reshape(shp)
    loss = stats[7, 0]
    return (loss, dx.reshape(x.shape), *[grads[n] for n in names], *[delta[n] for n in names],
            *[new_m[n] for n in names], *[new_v[n] for n in names])
```

```python
import functools

import numpy as np
import jax
import jax.numpy as jnp
from jax import lax
from jax.experimental import pallas as pl
from jax.experimental.pallas import tpu as pltpu

F32 = jnp.float32
BF16 = jnp.bfloat16
MESH = pl.DeviceIdType.MESH

EPS = 1e-6
D_MODEL = 1024
HEADS = 8
PAIRS = HEADS // 2
FOX_W = 512
Q_RANK = 256
KV_RANK = 128
ROPE = 32
D_FF = 2816
N_CHIPS = 4
FOX_SCALE = 64 ** -0.5
MLA_SCALE = 96 ** -0.5
LANES = 128
NEG = -1e30

ADAM_LR, ADAM_B1, ADAM_B2, ADAM_EPS, ADAM_WD, ADAM_STEP = 0.001, 0.9, 0.999, 1e-08, 0.01, 10

C_FQ, C_FK, C_FV, C_QL, C_KVL, C_MA, C_END = 0, 512, 1024, 1536, 1792, 1920, 2048
C_MB = C_END

VMEM_LIMIT = 60 * 1024 * 1024
ROW_TILE = 256
ATTN_TILE = 512
ATTN_FWD_TILE = 1024
WGRAD_TILE = 2048

HEAD3 = ("w_in", "w_uq", "w_ukv")
FFN4 = ("w_o", "w_gate", "w_up", "w_down")
TRANSPOSED = ("w_in", "w_uq", "w_gate", "w_up")
SMALL = ("norm_mix_g", "b_fgate", "q_norm_g", "kv_norm_g", "fox_out_g", "mla_out_g", "norm_ffn_g", "final_norm_g")


def _params(sem=None):
    return pltpu.CompilerParams(dimension_semantics=sem, vmem_limit_bytes=VMEM_LIMIT)


def _full(shape):
    n = len(shape)
    return pl.BlockSpec(shape, lambda *_: (0,) * n, pipeline_mode=pl.Buffered(1))


def _dot(a, b):
    return jnp.dot(a, b, preferred_element_type=F32)


def _dot_nt(a, b):
    return lax.dot_general(a, b, (((1,), (1,)), ((), ())), preferred_element_type=F32)


def _dot_tn(a, b):
    return lax.dot_general(a, b, (((0,), (0,)), ((), ())), preferred_element_type=F32)


def _split3(v):
    hi = v.astype(BF16)
    r1 = v - hi.astype(F32)
    mid = r1.astype(BF16)
    lo = (r1 - mid.astype(F32)).astype(BF16)
    return hi, mid, lo


def _rms(v, width):
    return lax.rsqrt(jnp.sum(v * v, axis=1, keepdims=True) * (1.0 / width) + EPS)


def _rms_bwd(dy, xhat, r, g, width):
    u = dy * g
    return r * (u - xhat * (jnp.sum(u * xhat, axis=1, keepdims=True) * (1.0 / width)))


ANY = pl.BlockSpec(memory_space=pl.ANY)


def _place():
    return lax.axis_index("x"), lax.axis_index("y"), lax.axis_index("c")


def _other_chips(x, y):
    return [(1 - x, y), (x, 1 - y), (1 - x, 1 - y)]


def _remote(src, dst, send, recv, j, dev):
    return pltpu.make_async_remote_copy(src_ref=src, dst_ref=dst, send_sem=send.at[j], recv_sem=recv.at[j], device_id=dev, device_id_type=MESH)


class _Exchange:
    def __init__(self, ins, outs, n_remote, n_local, build, in_place=False):
        self.ins, self.outs, self.n_remote, self.n_local, self.build = list(ins), list(outs), n_remote, max(n_local, 1), build
        self.in_place = in_place
        self.n_aliased = len(self.ins)

    def aliases(self, first_in, first_out):
        return {first_in + i: first_out + i for i in range(self.n_aliased)} if self.in_place else {}

    def sems(self):
        return [pltpu.SemaphoreType.DMA((self.n_remote,)), pltpu.SemaphoreType.DMA((self.n_remote,)), pltpu.SemaphoreType.DMA((self.n_local,))]

    def start(self, in_refs, out_refs, sems):
        for cp in self.build(in_refs, out_refs, *sems)[0]:
            cp.start()

    def wait(self, in_refs, out_refs, sems):
        for w in self.build(in_refs, out_refs, *sems)[1]:
            w()


def _gather_exchange(shards, own=True):
    def build(ins, outs, send, recv, lsem):
        x, y, c = _place()
        starts, waits = [], []
        for i, (s, o) in enumerate(zip(ins, outs)):
            if own:
                mine = pltpu.make_async_copy(s, o.at[2 * x + y], lsem.at[i])
                starts.append(mine)
                waits.append(mine.wait)
            for j, (cx, cy) in enumerate(_other_chips(x, y)):
                out = _remote(s, o.at[2 * x + y], send, recv, 3 * i + j, (cx, cy, c))
                starts.append(out)
                waits.append(_remote(s, o.at[2 * cx + cy], send, recv, 3 * i + j, (cx, cy, c)).wait_recv)
                waits.append(out.wait_send)
        return starts, waits

    outs = [jax.ShapeDtypeStruct((N_CHIPS,) + s.shape, s.dtype) for s in shards]
    return _Exchange(shards, outs, 3 * len(shards), len(shards), build)


def _gather_split_exchange(shards):
    n = len(shards)

    def build(ins, outs, send, recv, lsem):
        x, y, c = _place()
        starts, waits, last = [], [], []
        for i, (s, o) in enumerate(zip(ins, outs)):
            hc = s.shape[1] // 2
            mine, other = pl.ds(c * hc, hc), pl.ds((1 - c) * hc, hc)
            for j, (cx, cy) in enumerate(_other_chips(x, y)):
                out = _remote(s.at[:, mine], o.at[2 * x + y, :, mine], send, recv, 3 * i + j, (cx, cy, c))
                landed = o.at[2 * cx + cy, :, mine]
                arrive = _remote(s.at[:, mine], landed, send, recv, 3 * i + j, (cx, cy, c))
                onward = _remote(landed, landed, send, recv, 3 * n + 3 * i + j, (x, y, 1 - c))
                from_sibling = _remote(landed, o.at[2 * cx + cy, :, other], send, recv, 3 * n + 3 * i + j, (x, y, 1 - c))
                starts.append(out)
                waits.append(lambda arrive=arrive, onward=onward: (arrive.wait_recv(), onward.start()))
                last += [from_sibling.wait_recv, onward.wait_send, out.wait_send]
        return starts, waits + last

    outs = [jax.ShapeDtypeStruct((N_CHIPS,) + s.shape, s.dtype) for s in shards]
    return _Exchange(shards, outs, 6 * n, 0, build)


def _swap_exchange(grads):
    def build(ins, outs, send, recv, lsem):
        x, y, c = _place()
        cps = []
        for i, (g, o) in enumerate(zip(ins, outs)):
            hc = g.shape[2] // 2
            cps.append(_remote(g.at[:, :, pl.ds((1 - c) * hc, hc)], o, send, recv, i, (x, y, 1 - c)))
        return cps, [cp.wait for cp in cps]

    outs = [jax.ShapeDtypeStruct((g.shape[0], g.shape[1], g.shape[2] // 2), g.dtype) for g in grads]
    return _Exchange(grads, outs, len(grads), 0, build)


def _scatter_exchange(sums):
    def build(ins, outs, send, recv, lsem):
        x, y, c = _place()
        cps = []
        for i, (s, o) in enumerate(zip(ins, outs)):
            for j, (cx, cy) in enumerate(_other_chips(x, y)):
                cps.append(_remote(s.at[2 * cx + cy], o.at[j], send, recv, 3 * i + j, (cx, cy, c)))
        return cps, [cp.wait for cp in cps]

    outs = [jax.ShapeDtypeStruct((3,) + s.shape[1:], s.dtype) for s in sums]
    return _Exchange(sums, outs, 3 * len(sums), 0, build)


def _join_exchange(bufs):
    def build(ins, outs, send, recv, lsem):
        x, y, c = _place()
        starts, waits = [], []
        for i, (t, o) in enumerate(zip(ins, outs)):
            hc = t.shape[1] // 2
            out = _remote(t.at[:, pl.ds(c * hc, hc)], o.at[:, pl.ds(c * hc, hc)], send, recv, i, (x, y, 1 - c))
            starts.append(out)
            waits += [_remote(t.at[:, pl.ds(c * hc, hc)], o.at[:, pl.ds((1 - c) * hc, hc)], send, recv, i, (x, y, 1 - c)).wait_recv,
                      out.wait_send]
        return starts, waits

    outs = [jax.ShapeDtypeStruct(t.shape, t.dtype) for t in bufs]
    return _Exchange(bufs, outs, len(bufs), 0, build, in_place=True)


def _everyone_exchange(v):
    def build(ins, outs, send, recv, lsem):
        x, y, c = _place()
        me = 4 * x + 2 * y + c
        mine = pltpu.make_async_copy(ins[0], outs[0].at[me], lsem.at[0])
        starts, waits = [mine], [mine.wait]
        for j in range(7):
            fx, fy, fc = (j + 1) >> 2 & 1, (j + 1) >> 1 & 1, (j + 1) & 1
            peer = (x ^ fx, y ^ fy, c ^ fc)
            out = _remote(ins[0], outs[0].at[me], send, recv, j, peer)
            starts.append(out)
            waits += [_remote(ins[0], outs[0].at[4 * peer[0] + 2 * peer[1] + peer[2]], send, recv, j, peer).wait_recv, out.wait_send]
        return starts, waits

    return _Exchange([v], [jax.ShapeDtypeStruct((8,) + v.shape, v.dtype)], 7, 1, build)


def _both(a, b):
    na_in, na_out = len(a.ins), len(a.outs)

    def build(ins, outs, send, recv, lsem):
        sa, wa = a.build(ins[:na_in], outs[:na_out], send.at[pl.ds(0, a.n_remote)], recv.at[pl.ds(0, a.n_remote)],
                         lsem.at[pl.ds(0, a.n_local)])
        sb, wb = b.build(ins[na_in:], outs[na_out:], send.at[pl.ds(a.n_remote, b.n_remote)], recv.at[pl.ds(a.n_remote, b.n_remote)],
                         lsem.at[pl.ds(a.n_local, b.n_local)])
        return sa + sb, wa + wb

    both = _Exchange(a.ins + b.ins, a.outs + b.outs, a.n_remote + b.n_remote, a.n_local + b.n_local, build, in_place=a.in_place)
    both.n_aliased = na_in
    return both


def _run_exchange(ex, name):
    n_in, n_out = len(ex.ins), len(ex.outs)

    def body(*refs):
        ins, outs, sems = refs[:n_in], refs[n_in:n_in + n_out], refs[n_in + n_out:]
        ex.start(ins, outs, sems)
        ex.wait(ins, outs, sems)

    return pl.pallas_call(
        body, name=name, out_shape=tuple(ex.outs), in_specs=[ANY] * n_in, out_specs=tuple([ANY] * n_out),
        scratch_shapes=ex.sems(), input_output_aliases=ex.aliases(0, 0),
        compiler_params=pltpu.CompilerParams(has_side_effects=True),
    )(*ex.ins)


def _sum_devices(rows):
    def body(r_ref, o_ref):
        acc = r_ref[0]
        for d in range(1, 8):
            acc = acc + r_ref[d]
        o_ref[...] = acc

    vm = pl.BlockSpec(memory_space=pltpu.VMEM)
    return pl.pallas_call(body, name="sum_devices", out_shape=jax.ShapeDtypeStruct(rows.shape[1:], rows.dtype),
                          in_specs=[vm], out_specs=vm)(rows)


def _add_half(g, got):
    n, R, C = g.shape
    hc = C // 2

    def body(c_ref, g_ref, r_ref, o_ref):
        o_ref[...] = (g_ref[...] + r_ref[...]).astype(BF16)

    c = lax.axis_index("c")
    return pl.pallas_call(
        body, name="add_half",
        grid_spec=pltpu.PrefetchScalarGridSpec(
            num_scalar_prefetch=1, grid=(n,),
            in_specs=[pl.BlockSpec((1, R, hc), lambda k, c_ref: (k, 0, c_ref[0])),
                      pl.BlockSpec((1, R, hc), lambda k, c_ref: (k, 0, 0))],
            out_specs=pl.BlockSpec((1, R, hc), lambda k, c_ref: (k, 0, 0))),
        out_shape=jax.ShapeDtypeStruct((n, R, hc), BF16),
        compiler_params=_params(("arbitrary",)),
    )(jnp.reshape(c, (1,)).astype(jnp.int32), g, got)


def _sum_slabs(g, got, recv):
    n, R, C = g.shape
    hc = C // 2

    def body(kc_ref, g_ref, s_ref, r_ref, o_ref):
        o_ref[...] = (((g_ref[0] + s_ref[0]) + r_ref[0].astype(F32)) + r_ref[1].astype(F32)) + r_ref[2].astype(F32)

    kc = jnp.stack([2 * lax.axis_index("x") + lax.axis_index("y"), lax.axis_index("c")]).astype(jnp.int32)
    return pl.pallas_call(
        body, name="sum_slabs",
        grid_spec=pltpu.PrefetchScalarGridSpec(
            num_scalar_prefetch=1, grid=(1,),
            in_specs=[pl.BlockSpec((1, R, hc), lambda i, kc_ref: (kc_ref[0], 0, kc_ref[1])),
                      pl.BlockSpec((1, R, hc), lambda i, kc_ref: (kc_ref[0], 0, 0)),
                      pl.BlockSpec((3, R, hc), lambda i, kc_ref: (0, 0, 0))],
            out_specs=pl.BlockSpec((R, hc), lambda i, kc_ref: (0, kc_ref[1]))),
        out_shape=jax.ShapeDtypeStruct((R, C), F32),
        compiler_params=_params(("arbitrary",)),
    )(kc, g, got, recv)


def _row_tile(rows):
    for cand in (256, 184, 176, 144, 128, 64, 32, 16, 8):
        if rows % cand == 0:
            return cand
    return rows


def _in_proj(x, g1, w_in, w_q12, w_k, w_v, gq, gkv, bfg, ct, st, sel, seq, tm):
    T = x.shape[0]
    nsb = seq // tm

    def body(x_ref, g1_ref, win_ref, wq_ref, wk_ref, wv_ref, gq_ref, gkv_ref, b_ref, ct_ref, st_ref, sel_ref,
             h1_ref, qf_ref, kf_ref, vf_ref, qm_ref, km_ref, vm_ref, lat_ref, qn_ref, kvn_ref, carry):
        i = pl.program_id(0)

        @pl.when(i % nsb == 0)
        def _():
            carry[...] = jnp.zeros_like(carry)

        xv = x_ref[...]
        h = (xv * _rms(xv, D_MODEL) * g1_ref[...]).astype(BF16)
        h1_ref[...] = h
        proj = _dot_nt(h, win_ref[...])
        lane = lax.broadcasted_iota(jnp.int32, (tm, LANES), 1)
        low = lane < 64
        misc_a = proj[:, C_MA:C_END]
        misc_b = pltpu.roll(misc_a, 96, 1)

        z = misc_a + b_ref[...]
        lf = jnp.where(lane < HEADS, jnp.minimum(z, 0.0) - jnp.log1p(jnp.exp(-jnp.abs(z))), 0.0)
        rr = lax.broadcasted_iota(jnp.int32, (tm, tm), 0)
        cc = lax.broadcasted_iota(jnp.int32, (tm, tm), 1)
        tri = (rr >= cc).astype(BF16)
        a0, a1, a2 = _split3(lf)
        c = _dot(tri, a0) + _dot(tri, a1) + _dot(tri, a2) + carry[0:1, :]
        carry[0:1, :] = c[tm - 1:tm, :]
        c0, c1, c2 = _split3(c)
        cpl = _dot(jnp.concatenate([c0, c1, c2], axis=1), sel_ref[...])
        qpad = jnp.where((lane >= 64) & (lane < 67), -1.0, 0.0)
        for j in range(PAIRS):
            qc = proj[:, C_FQ + LANES * j:C_FQ + LANES * (j + 1)] * FOX_SCALE
            kc = proj[:, C_FK + LANES * j:C_FK + LANES * (j + 1)]
            e, o = 2 * LANES * j, 2 * LANES * j + LANES
            qf_ref[:, e:e + LANES] = jnp.where(low, qc, qpad).astype(BF16)
            qf_ref[:, o:o + LANES] = jnp.where(low, pltpu.roll(qc, 64, 1), qpad).astype(BF16)
            kf_ref[:, e:e + LANES] = jnp.where(low, kc, cpl[:, e:e + LANES]).astype(BF16)
            kf_ref[:, o:o + LANES] = jnp.where(low, pltpu.roll(kc, 64, 1), cpl[:, o:o + LANES]).astype(BF16)
        vf_ref[...] = proj[:, C_FV:C_QL].astype(BF16)

        ql = proj[:, C_QL:C_KVL]
        kvl = proj[:, C_KVL:C_MA]
        qn = (ql * _rms(ql, Q_RANK) * gq_ref[...]).astype(BF16)
        kvn = (kvl * _rms(kvl, KV_RANK) * gkv_ref[...]).astype(BF16)
        lat_ref[...] = proj[:, C_QL:C_MB]
        qn_ref[...] = qn
        kvn_ref[...] = kvn
        q12 = _dot_nt(qn, wq_ref[...])
        kn = _dot(kvn, wk_ref[...])
        ctv = ct_ref[...]
        stv = st_ref[...]
        cq = (jnp.where(low, 1.0, 0.0) + ctv) * MLA_SCALE
        sq = stv * MLA_SCALE
        kpe = misc_a * ctv + misc_b * stv
        for hd in range(HEADS):
            s0 = LANES * hd
            qm_ref[:, s0:s0 + LANES] = (q12[:, s0:s0 + LANES] * cq + q12[:, 1024 + s0:1024 + s0 + LANES] * sq).astype(BF16)
            km_ref[:, s0:s0 + LANES] = (kn[:, s0:s0 + LANES] + kpe).astype(BF16)
        vm_ref[...] = _dot(kvn, wv_ref[...]).astype(BF16)

    row = lambda w: pl.BlockSpec((tm, w), lambda i: (i, 0))
    out_shape = (
        jax.ShapeDtypeStruct((T, D_MODEL), BF16),
        jax.ShapeDtypeStruct((T, 1024), BF16), jax.ShapeDtypeStruct((T, 1024), BF16), jax.ShapeDtypeStruct((T, 512), BF16),
        jax.ShapeDtypeStruct((T, 1024), BF16), jax.ShapeDtypeStruct((T, 1024), BF16), jax.ShapeDtypeStruct((T, 512), BF16),
        jax.ShapeDtypeStruct((T, 512), F32),
        jax.ShapeDtypeStruct((T, Q_RANK), BF16), jax.ShapeDtypeStruct((T, KV_RANK), BF16),
    )
    return pl.pallas_call(
        body, name="in_proj", grid=(T // tm,), out_shape=out_shape,
        in_specs=[row(D_MODEL), _full(g1.shape), _full(w_in.shape), _full(w_q12.shape), _full(w_k.shape), _full(w_v.shape),
                  _full(gq.shape), _full(gkv.shape), _full(bfg.shape), row(LANES), row(LANES), _full(sel.shape)],
        out_specs=[row(D_MODEL), row(1024), row(1024), row(512), row(1024), row(1024), row(512), row(512), row(Q_RANK), row(KV_RANK)],
        scratch_shapes=[pltpu.VMEM((8, LANES), F32)],
        compiler_params=_params(("arbitrary",)),
    )(x, g1, w_in, w_q12, w_k, w_v, gq, gkv, bfg, ct, st, sel)


def _attn_fwd(q, k, v, nb, seq, tq, name, ex=None):
    T = q.shape[0]
    nq = seq // tq
    n_in, n_out = (len(ex.ins), len(ex.outs)) if ex else (0, 0)

    def body(*refs):
        q_ref, k_ref, v_ref = refs[0:3]
        o_ref, lse_ref = refs[3 + n_in:5 + n_in]
        b, pr, qi = pl.program_id(0), pl.program_id(1), pl.program_id(2)
        if ex:
            ex_refs = (refs[3:3 + n_in], refs[5 + n_in:5 + n_in + n_out], refs[8 + n_in + n_out:])

            @pl.when((b == 0) & (pr == 0) & (qi == 0))
            def _():
                ex.start(*ex_refs)

        s_sc, p_sc, acc_sc = refs[5 + n_in + n_out:8 + n_in + n_out]
        strip = 64
        key_s = lax.broadcasted_iota(jnp.int32, (strip, tq), 0)
        qry_s = lax.broadcasted_iota(jnp.int32, (strip, tq), 1)
        row_t = lax.broadcasted_iota(jnp.int32, (LANES, tq), 0)
        acc_sc[...] = jnp.zeros(acc_sc.shape, F32)

        def fold(x, op):
            out = x[0:8]
            for r in range(8, strip, 8):
                out = op(out, x[r:r + 8])
            return out

        def step(kj, state, masked):
            rows = pl.ds(pl.multiple_of(kj * tq, tq), tq)
            for hh in range(2):
                s_sc[hh] = _dot_nt(k_ref[rows, LANES * hh:LANES * (hh + 1)], q_ref[:, LANES * hh:LANES * (hh + 1)])
            vv = v_ref[rows, :]
            new = []
            for hh in range(2):
                m, l = state[hh]

                def strip_of(r0, hh=hh):
                    s = s_sc[hh, r0:r0 + strip, :]
                    return jnp.where(key_s + r0 <= qry_s, s, NEG) if masked else s

                mx = fold(strip_of(0), jnp.maximum)
                for r0 in range(strip, tq, strip):
                    mx = jnp.maximum(mx, fold(strip_of(r0), jnp.maximum))
                m_new = jnp.maximum(m, jnp.max(mx, axis=0, keepdims=True))
                alpha = jnp.exp(m - m_new)
                sm = jnp.zeros((8, tq), F32)
                for r0 in range(0, tq, strip):
                    p = jnp.exp(strip_of(r0) - m_new)
                    sm = sm + fold(p, jnp.add)
                    p_sc[hh, r0:r0 + strip, :] = p.astype(BF16)
                l = alpha * l + jnp.sum(sm, axis=0, keepdims=True)
                acc_sc[hh] = alpha * acc_sc[hh] + _dot_tn(vv, p_sc[hh])
                new.append((m_new, l))
            return tuple(new)

        one = (jnp.full((1, tq), NEG, F32), jnp.zeros((1, tq), F32))
        state = lax.fori_loop(0, qi, functools.partial(step, masked=False), (one, one))
        (m0, l0), (m1, l1) = step(qi, state, True)
        o_ref[...] = jnp.where(row_t < 64, acc_sc[0] / l0, acc_sc[1] / l1).T
        lse_ref[:, 0:LANES] = jnp.broadcast_to(m0 + jnp.log(l0), (LANES, tq)).T
        lse_ref[:, LANES:2 * LANES] = jnp.broadcast_to(m1 + jnp.log(l1), (LANES, tq)).T

        if ex:
            @pl.when((b == nb - 1) & (pr == PAIRS - 1) & (qi == nq - 1))
            def _():
                ex.wait(*ex_refs)

    res = pl.pallas_call(
        body, name=name, grid=(nb, PAIRS, nq),
        out_shape=(jax.ShapeDtypeStruct((T, 512), F32), jax.ShapeDtypeStruct((T, 1024), F32)) + tuple(ex.outs if ex else ()),
        in_specs=[pl.BlockSpec((tq, 2 * LANES), lambda b, p, i: (b * nq + i, p)),
                  pl.BlockSpec((seq, 2 * LANES), lambda b, p, i: (b, p)),
                  pl.BlockSpec((seq, LANES), lambda b, p, i: (b, p))] + [ANY] * n_in,
        out_specs=[pl.BlockSpec((tq, LANES), lambda b, p, i: (b * nq + i, p)),
                   pl.BlockSpec((tq, 2 * LANES), lambda b, p, i: (b * nq + i, p))] + [ANY] * n_out,
        scratch_shapes=[pltpu.VMEM((2, tq, tq), F32), pltpu.VMEM((2, tq, tq), BF16), pltpu.VMEM((2, LANES, tq), F32)]
        + (ex.sems() if ex else []),
        compiler_params=_params(("arbitrary", "arbitrary", "arbitrary")),
    )(q, k, v, *(ex.ins if ex else ()))
    return res[0], res[1], list(res[2:])


def _attn_bwd(q, k, v, o, do, lse, nb, seq, tq, name, key_bias, ex=None):
    T = q.shape[0]
    nq = seq // tq
    n_in, n_out = (len(ex.ins), len(ex.outs)) if ex else (0, 0)

    def body(*refs):
        q_ref, k_ref, v_ref, o_ref, do_ref, lse_ref = refs[0:6]
        dq_ref, dk_ref, dv_ref = refs[6 + n_in:9 + n_in]
        dsc, rsum = refs[9 + n_in + n_out:11 + n_in + n_out]
        b, pr, kj = pl.program_id(0), pl.program_id(1), pl.program_id(2)
        if ex:
            ex_refs = (refs[6:6 + n_in], refs[9 + n_in:9 + n_in + n_out], refs[11 + n_in + n_out:])

            @pl.when((b == 0) & (pr == 0) & (kj == 0))
            def _():
                ex.start(*ex_refs)

        lane_s = lax.broadcasted_iota(jnp.int32, (seq, LANES), 1)
        lane = lax.broadcasted_iota(jnp.int32, (tq, LANES), 1)
        rr = lax.broadcasted_iota(jnp.int32, (tq, tq), 0)
        cc = lax.broadcasted_iota(jnp.int32, (tq, tq), 1)

        @pl.when(kj == 0)
        def _():
            dq_ref[...] = jnp.zeros_like(dq_ref)
            prod = do_ref[...].astype(F32) * o_ref[...]
            d0 = jnp.sum(jnp.where(lane_s < 64, prod, 0.0), axis=1, keepdims=True)
            d1 = jnp.sum(jnp.where(lane_s < 64, 0.0, prod), axis=1, keepdims=True)
            dsc[0] = jnp.broadcast_to(d0, (seq, LANES))
            dsc[1] = jnp.broadcast_to(d1, (seq, LANES))
            if key_bias:
                rsum[...] = jnp.zeros_like(rsum)

        vv = v_ref[...]

        def step(qi, carry, masked):
            dkt, dvt, cols = carry
            rows = pl.ds(pl.multiple_of(qi * tq, tq), tq)
            dov = do_ref[rows, :]
            new_dkt, new_cols = [], []
            for hh in range(2):
                qv = q_ref[rows, LANES * hh:LANES * (hh + 1)]
                kv = k_ref[:, LANES * hh:LANES * (hh + 1)]
                dom = jnp.where((lane < 64) if hh == 0 else (lane >= 64), dov, jnp.zeros((), BF16))
                s = _dot_nt(qv, kv)
                if masked:
                    s = jnp.where(cc <= rr, s, NEG)
                p = jnp.exp(s - jnp.tile(lse_ref[rows, LANES * hh:LANES * (hh + 1)], (1, tq // LANES)))
                dp = _dot_nt(dom, vv)
                ds32 = p * (dp - jnp.tile(dsc[hh, rows, :], (1, tq // LANES)))
                col = cols[hh]
                if key_bias:
                    col = col + jnp.sum(ds32, axis=0, keepdims=True)
                    rsum[hh, rows, :] += jnp.broadcast_to(jnp.sum(ds32, axis=1, keepdims=True), (tq, LANES))
                ds = ds32.astype(BF16)
                dvt = dvt + _dot_tn(dom, p.astype(BF16))
                new_dkt.append(dkt[hh] + _dot_tn(qv, ds))
                new_cols.append(col)
                dq_ref[rows, LANES * hh:LANES * (hh + 1)] += _dot(ds, kv)
            return tuple(new_dkt), dvt, tuple(new_cols)

        zt = jnp.zeros((LANES, tq), F32)
        zc = jnp.zeros((1, tq), F32)
        carry = step(kj, ((zt, zt), zt, (zc, zc)), True)
        dkt, dvt, cols = lax.fori_loop(kj + 1, nq, functools.partial(step, masked=False), carry)
        row_t = lax.broadcasted_iota(jnp.int32, (LANES, tq), 0)
        for hh in range(2):
            dk_h = jnp.where(row_t == 64, -cols[hh], dkt[hh]) if key_bias else dkt[hh]
            dk_ref[:, LANES * hh:LANES * (hh + 1)] = dk_h.T
        dv_ref[...] = dvt.T

        if key_bias:
            @pl.when(kj == nq - 1)
            def _():
                for hh in range(2):
                    blk = dq_ref[:, LANES * hh:LANES * (hh + 1)]
                    dq_ref[:, LANES * hh:LANES * (hh + 1)] = jnp.where(lane_s == 64, rsum[hh], blk)

        if ex:
            @pl.when((b == nb - 1) & (pr == PAIRS - 1) & (kj == nq - 1))
            def _():
                ex.wait(*ex_refs)

    per_seq = lambda w: pl.BlockSpec((seq, w), lambda b, p, j: (b, p))
    per_blk = lambda w: pl.BlockSpec((tq, w), lambda b, p, j: (b * nq + j, p))
    res = pl.pallas_call(
        body, name=name, grid=(nb, PAIRS, nq),
        out_shape=(jax.ShapeDtypeStruct((T, 1024), F32), jax.ShapeDtypeStruct((T, 1024), F32), jax.ShapeDtypeStruct((T, 512), F32))
        + tuple(ex.outs if ex else ()),
        in_specs=[per_seq(2 * LANES), per_blk(2 * LANES), per_blk(LANES), per_seq(LANES), per_seq(LANES), per_seq(2 * LANES)] + [ANY] * n_in,
        out_specs=[per_seq(2 * LANES), per_blk(2 * LANES), per_blk(LANES)] + [ANY] * n_out,
        scratch_shapes=[pltpu.VMEM((2, seq, LANES), F32), pltpu.VMEM((2, seq, LANES) if key_bias else (2, 8, LANES), F32)]
        + (ex.sems() if ex else []),
        compiler_params=_params(("arbitrary", "arbitrary", "arbitrary")),
    )(q, k, v, o, do, lse, *(ex.ins if ex else ()))
    return res[0], res[1], res[2], list(res[3:])


def _mid(of, om, x, tgt, g_fo, g_mo, g2, g3, w_o, w_g, w_u, w_d, tm):
    T = x.shape[0]

    def body(of_ref, om_ref, x_ref, t_ref, gfo_ref, gmo_ref, g2_ref, g3_ref, wo_ref, wg_ref, wu_ref, wd_ref,
             a_ref, h2_ref, hid_ref, dg_ref, du_ref, dx3_ref, dx2_ref, dof_ref, dom_ref, st_ref):
        i = pl.program_id(0)

        @pl.when(i == 0)
        def _():
            st_ref[...] = jnp.zeros_like(st_ref)

        ofv, omv = of_ref[...], om_ref[...]
        rf, rm = _rms(ofv, FOX_W), _rms(omv, FOX_W)
        fhat, mhat = ofv * rf, omv * rm
        a = jnp.concatenate([fhat * gfo_ref[...], mhat * gmo_ref[...]], axis=1).astype(BF16)
        a_ref[...] = a
        x2 = x_ref[...] + _dot(a, wo_ref[...])
        r2 = _rms(x2, D_MODEL)
        xh2 = x2 * r2
        h2 = (xh2 * g2_ref[...]).astype(BF16)
        h2_ref[...] = h2
        gt = _dot_nt(h2, wg_ref[...])
        up = _dot_nt(h2, wu_ref[...])
        sg = jax.nn.sigmoid(gt)
        sl = gt * sg
        hid = (sl * up).astype(BF16)
        hid_ref[...] = hid
        x3 = x2 + _dot(hid, wd_ref[...])
        r3 = _rms(x3, D_MODEL)
        xh3 = x3 * r3
        diff = xh3 * g3_ref[...] - t_ref[...]
        dy = diff * (1.0 / D_MODEL)
        st_ref[3:4, :] += jnp.sum(diff * diff, axis=0, keepdims=True) * (0.5 / D_MODEL)
        st_ref[0:1, :] += jnp.sum(dy * xh3, axis=0, keepdims=True)
        dx3 = _rms_bwd(dy, xh3, r3, g3_ref[...], D_MODEL)
        dx3b = dx3.astype(BF16)
        dx3_ref[...] = dx3b
        dhid = _dot_nt(dx3b, wd_ref[...])
        dg = (dhid * up * (sg * (1.0 + gt * (1.0 - sg)))).astype(BF16)
        du = (dhid * sl).astype(BF16)
        dg_ref[...] = dg
        du_ref[...] = du
        dh2 = _dot(dg, wg_ref[...]) + _dot(du, wu_ref[...])
        st_ref[1:2, :] += jnp.sum(dh2 * xh2, axis=0, keepdims=True)
        dx2 = dx3 + _rms_bwd(dh2, xh2, r2, g2_ref[...], D_MODEL)
        dx2_ref[...] = dx2
        da = _dot_nt(dx2.astype(BF16), wo_ref[...])
        daf, dam = da[:, 0:FOX_W], da[:, FOX_W:2 * FOX_W]
        st_ref[2:3, 0:FOX_W] += jnp.sum(daf * fhat, axis=0, keepdims=True)
        st_ref[2:3, FOX_W:2 * FOX_W] += jnp.sum(dam * mhat, axis=0, keepdims=True)
        dof_ref[...] = _rms_bwd(daf, fhat, rf, gfo_ref[...], FOX_W).astype(BF16)
        dom_ref[...] = _rms_bwd(dam, mhat, rm, gmo_ref[...], FOX_W).astype(BF16)

    row = lambda w: pl.BlockSpec((tm, w), lambda i: (i, 0))
    ff = jax.ShapeDtypeStruct((T, D_FF), BF16)
    out_shape = (
        jax.ShapeDtypeStruct((T, 1024), BF16), jax.ShapeDtypeStruct((T, 1024), BF16), ff, ff, ff,
        jax.ShapeDtypeStruct((T, 1024), BF16), jax.ShapeDtypeStruct((T, 1024), F32),
        jax.ShapeDtypeStruct((T, 512), BF16), jax.ShapeDtypeStruct((T, 512), BF16), jax.ShapeDtypeStruct((8, 1024), F32),
    )
    return pl.pallas_call(
        body, name="mid", grid=(T // tm,), out_shape=out_shape,
        in_specs=[row(512), row(512), row(1024), row(1024), _full(g_fo.shape), _full(g_mo.shape), _full(g2.shape), _full(g3.shape),
                  _full(w_o.shape), _full(w_g.shape), _full(w_u.shape), _full(w_d.shape)],
        out_specs=[row(1024), row(1024), row(D_FF), row(D_FF), row(D_FF), row(1024), row(1024), row(512), row(512),
                   pl.BlockSpec((8, 1024), lambda i: (0, 0))],
        compiler_params=_params(("arbitrary",)),
    )(of, om, x, tgt, g_fo, g_mo, g2, g3, w_o, w_g, w_u, w_d)


def _in_bwd(dqf, dkf, dvf, dqm, dkm, dvm, lat, x, dx2, g1, gq, gkv, bfg, ct, st, sel_t, w_in, w_q12, w_kv, seq, tm, ex=None):
    T = x.shape[0]
    nblk = T // tm
    nsb = seq // tm
    n_in, n_out = (len(ex.ins), len(ex.outs)) if ex else (0, 0)

    def body(*refs):
        (dqf_ref, dkf_ref, dvf_ref, dqm_ref, dkm_ref, dvm_ref, lat_ref, x_ref, dx2_ref, g1_ref, gq_ref, gkv_ref, b_ref,
         ct_ref, st_ref, selt_ref, win_ref, wq_ref, wkv_ref) = refs[0:19]
        dx_ref, dproj_ref, dq12_ref, dkv_ref, stat_ref = refs[19 + n_in:24 + n_in]
        carry = refs[24 + n_in + n_out]
        i = pl.program_id(0)
        if ex:
            ex_refs = (refs[19:19 + n_in], refs[24 + n_in:24 + n_in + n_out], refs[25 + n_in + n_out:])

            @pl.when(i == 0)
            def _():
                ex.start(*ex_refs)

        @pl.when(i == 0)
        def _():
            stat_ref[...] = jnp.zeros_like(stat_ref)

        @pl.when(i % nsb == 0)
        def _():
            carry[...] = jnp.zeros_like(carry)

        lane = lax.broadcasted_iota(jnp.int32, (tm, LANES), 1)
        low = lane < 64
        ctv, stv = ct_ref[...], st_ref[...]

        for j in range(PAIRS):
            e, o = 2 * LANES * j, 2 * LANES * j + LANES
            dq = jnp.where(low, dqf_ref[:, e:e + LANES], 0.0) + pltpu.roll(jnp.where(low, dqf_ref[:, o:o + LANES], 0.0), 64, 1)
            dk = jnp.where(low, dkf_ref[:, e:e + LANES], 0.0) + pltpu.roll(jnp.where(low, dkf_ref[:, o:o + LANES], 0.0), 64, 1)
            dproj_ref[:, C_FQ + LANES * j:C_FQ + LANES * (j + 1)] = (dq * FOX_SCALE).astype(BF16)
            dproj_ref[:, C_FK + LANES * j:C_FK + LANES * (j + 1)] = dk.astype(BF16)
        dproj_ref[:, C_FV:C_QL] = dvf_ref[...].astype(BF16)
        dcv = dkf_ref[...] + dqf_ref[...]
        k_hi = dcv.astype(BF16)
        k_lo = (dcv - k_hi.astype(F32)).astype(BF16)
        dc = _dot(k_hi, selt_ref[...]) + _dot(k_lo, selt_ref[...])
        rr = lax.broadcasted_iota(jnp.int32, (tm, tm), 0)
        cc = lax.broadcasted_iota(jnp.int32, (tm, tm), 1)
        triu = (cc >= rr).astype(BF16)
        a0, a1, a2 = _split3(dc)
        dlf = _dot(triu, a0) + _dot(triu, a1) + _dot(triu, a2) + carry[0:1, :]
        carry[0:1, :] = dlf[0:1, :]
        misc_a = lat_ref[:, Q_RANK + KV_RANK:Q_RANK + KV_RANK + LANES]
        z = misc_a + b_ref[...]
        dz = jnp.where(lane < HEADS, dlf * jax.nn.sigmoid(-z), 0.0)
        stat_ref[3:4, 0:LANES] += jnp.sum(dz, axis=0, keepdims=True)

        cq = (jnp.where(low, 1.0, 0.0) + ctv) * MLA_SCALE
        sq = stv * MLA_SCALE
        dkpe = jnp.zeros((tm, LANES), F32)
        for hd in range(HEADS):
            s0 = LANES * hd
            dqh = dqm_ref[:, s0:s0 + LANES]
            dq12_ref[:, s0:s0 + LANES] = (dqh * cq).astype(BF16)
            dq12_ref[:, 1024 + s0:1024 + s0 + LANES] = (dqh * sq).astype(BF16)
            dkpe = dkpe + dkm_ref[:, s0:s0 + LANES]
        dkv_ref[:, 0:1024] = dkm_ref[...].astype(BF16)
        dkv_ref[:, 1024:1536] = dvm_ref[...].astype(BF16)
        dproj_ref[:, C_MA:C_END] = (dz + dkpe * ctv + pltpu.roll(dkpe * stv, 32, 1)).astype(BF16)
        dqn = _dot(dq12_ref[...], wq_ref[...])
        dkvn = _dot_nt(dkv_ref[...], wkv_ref[...])
        ql = lat_ref[:, 0:Q_RANK]
        kvl = lat_ref[:, Q_RANK:Q_RANK + KV_RANK]
        rq, rkv = _rms(ql, Q_RANK), _rms(kvl, KV_RANK)
        qhat, kvhat = ql * rq, kvl * rkv
        stat_ref[1:2, 0:Q_RANK] += jnp.sum(dqn * qhat, axis=0, keepdims=True)
        stat_ref[2:3, 0:KV_RANK] += jnp.sum(dkvn * kvhat, axis=0, keepdims=True)
        dproj_ref[:, C_QL:C_KVL] = _rms_bwd(dqn, qhat, rq, gq_ref[...], Q_RANK).astype(BF16)
        dproj_ref[:, C_KVL:C_MA] = _rms_bwd(dkvn, kvhat, rkv, gkv_ref[...], KV_RANK).astype(BF16)

        dh1 = _dot(dproj_ref[...], win_ref[...])
        xv = x_ref[...]
        r1 = _rms(xv, D_MODEL)
        xh = xv * r1
        stat_ref[0:1, :] += jnp.sum(dh1 * xh, axis=0, keepdims=True)
        dx_ref[...] = dx2_ref[...] + _rms_bwd(dh1, xh, r1, g1_ref[...], D_MODEL)

        if ex:
            @pl.when(i == nblk - 1)
            def _():
                ex.wait(*ex_refs)

    rev = lambda w: pl.BlockSpec((tm, w), lambda i: (nblk - 1 - i, 0))
    out_shape = (
        jax.ShapeDtypeStruct((T, 1024), F32), jax.ShapeDtypeStruct((T, C_END), BF16), jax.ShapeDtypeStruct((T, 2048), BF16),
        jax.ShapeDtypeStruct((T, 1536), BF16), jax.ShapeDtypeStruct((8, 1024), F32),
    ) + tuple(ex.outs if ex else ())
    res = pl.pallas_call(
        body, name="in_bwd", grid=(nblk,), out_shape=out_shape,
        in_specs=[rev(1024), rev(1024), rev(512), rev(1024), rev(1024), rev(512), rev(512), rev(1024), rev(1024),
                  _full(g1.shape), _full(gq.shape), _full(gkv.shape), _full(bfg.shape), rev(LANES), rev(LANES), _full(sel_t.shape),
                  _full(w_in.shape), _full(w_q12.shape), _full(w_kv.shape)] + [ANY] * n_in,
        out_specs=[rev(1024), rev(C_END), rev(2048), rev(1536), pl.BlockSpec((8, 1024), lambda i: (0, 0))] + [ANY] * n_out,
        scratch_shapes=[pltpu.VMEM((8, LANES), F32)] + (ex.sems() if ex else []),
        compiler_params=_params(("arbitrary",)),
    )(dqf, dkf, dvf, dqm, dkm, dvm, lat, x, dx2, g1, gq, gkv, bfg, ct, st, sel_t, w_in, w_q12, w_kv, *(ex.ins if ex else ()))
    return res[0], res[1], res[2], res[3], res[4], list(res[5:])


def _wgrad(a, b, tk, tt, name, ex=None):
    T, K = a.shape
    N = b.shape[1]
    n_in, n_out = (len(ex.ins), len(ex.outs)) if ex else (0, 0)
    gk, gt = K // tk, T // tt

    def body(*refs):
        a_ref, b_ref, o_ref = refs[0], refs[1], refs[2 + n_in]
        kb, t = pl.program_id(0), pl.program_id(1)
        if ex:
            ex_refs = (refs[2:2 + n_in], refs[3 + n_in:3 + n_in + n_out], refs[3 + n_in + n_out:])

            @pl.when((kb == 0) & (t == 0))
            def _():
                ex.start(*ex_refs)

        @pl.when(t == 0)
        def _():
            o_ref[...] = jnp.zeros_like(o_ref)

        o_ref[...] += _dot_tn(a_ref[...].astype(BF16), b_ref[...].astype(BF16))

        if ex:
            @pl.when((kb == gk - 1) & (t == gt - 1))
            def _():
                ex.wait(*ex_refs)

    res = pl.pallas_call(
        body, name=name, grid=(gk, gt), out_shape=(jax.ShapeDtypeStruct((K, N), F32),) + tuple(ex.outs if ex else ()),
        in_specs=[pl.BlockSpec((tt, tk), lambda kb, t: (t, kb)), pl.BlockSpec((tt, N), lambda kb, t: (t, 0))] + [ANY] * n_in,
        out_specs=[pl.BlockSpec((tk, N), lambda kb, t: (kb, 0))] + [ANY] * n_out,
        scratch_shapes=ex.sems() if ex else [], input_output_aliases=ex.aliases(2, 1) if ex else {},
        compiler_params=_params(("arbitrary", "arbitrary")),
    )(a, b, *(ex.ins if ex else ()))
    return (res[0], list(res[1:])) if ex else res[0]


def _adamw(tensors, name, ex=None):
    n = len(tensors)
    R, C = tensors[0][0].shape
    tr = _row_tile(R)
    steps = R // tr
    n_in, n_out = (len(ex.ins), len(ex.outs)) if ex else (0, 0)

    def body(*refs):
        i = pl.program_id(0)
        if ex:
            ex_refs = (refs[4 * n:4 * n + n_in], refs[7 * n + n_in:7 * n + n_in + n_out], refs[7 * n + n_in + n_out:])

            @pl.when(i == 0)
            def _():
                ex.start(*ex_refs)

        for t in range(n):
            w_ref, g_ref, m_ref, v_ref = refs[4 * t:4 * t + 4]
            d_ref, nm_ref, nv_ref = refs[4 * n + n_in + 3 * t:4 * n + n_in + 3 * t + 3]
            gv = g_ref[...]
            nm = ADAM_B1 * m_ref[...] + (1.0 - ADAM_B1) * gv
            nv = ADAM_B2 * v_ref[...] + (1.0 - ADAM_B2) * (gv * gv)
            m_hat = nm / (1.0 - ADAM_B1 ** ADAM_STEP)
            v_hat = nv / (1.0 - ADAM_B2 ** ADAM_STEP)
            d_ref[...] = -ADAM_LR * (m_hat / (jnp.sqrt(v_hat) + ADAM_EPS) + ADAM_WD * w_ref[...])
            nm_ref[...] = nm
            nv_ref[...] = nv

        if ex:
            @pl.when(i == steps - 1)
            def _():
                ex.wait(*ex_refs)

    blk = pl.BlockSpec((tr, C), lambda i: (i, 0))
    sh = jax.ShapeDtypeStruct((R, C), F32)
    res = pl.pallas_call(
        body, name=name, grid=(steps,), out_shape=(sh,) * (3 * n) + tuple(ex.outs if ex else ()),
        in_specs=[blk] * (4 * n) + [ANY] * n_in, out_specs=[blk] * (3 * n) + [ANY] * n_out,
        scratch_shapes=ex.sems() if ex else [],
        compiler_params=_params(("arbitrary",)),
    )(*[a for t in tensors for a in t], *(ex.ins if ex else ()))
    out = [tuple(res[3 * t:3 * t + 3]) for t in range(n)]
    return (out, list(res[3 * n:])) if ex else out


def _arrange(win_t, wuq_t, wukv):
    dt = win_t.dtype
    z = lambda r: jnp.zeros((r, D_MODEL), dt)
    zh = lambda r: jnp.zeros((HEADS, r, Q_RANK), dt)
    kr1, kr2 = win_t[1928:1944], win_t[1944:1960]
    misc = jnp.concatenate([win_t[1536:1544], z(56), kr1, kr2, kr2, kr1], axis=0)
    w_in = jnp.concatenate([win_t[0:1536], win_t[1544:1928], misc], axis=0)
    wq = wuq_t.reshape(HEADS, 96, Q_RANK)
    q1 = jnp.concatenate([wq, zh(32)], axis=1).reshape(1024, Q_RANK)
    q2 = jnp.concatenate([zh(64), wq[:, 80:96], wq[:, 64:80], zh(32)], axis=1).reshape(1024, Q_RANK)
    wkv = wukv.reshape(KV_RANK, HEADS, 128)
    wk = jnp.concatenate([wkv[:, :, 0:64], jnp.zeros((KV_RANK, HEADS, 64), dt)], axis=2).reshape(KV_RANK, 1024)
    wv = wkv[:, :, 64:128].reshape(KV_RANK, 512)
    return dict(w_in=w_in, w_q12=jnp.concatenate([q1, q2], axis=0), w_k=wk, w_v=wv, w_kv=jnp.concatenate([wk, wv], axis=1))


def _unarrange(g_in, g_q12, g_kv):
    kr1 = g_in[C_MA + 64:C_MA + 80] + g_in[C_MA + 112:C_MA + 128]
    kr2 = g_in[C_MA + 80:C_MA + 96] + g_in[C_MA + 96:C_MA + 112]
    win_t = jnp.concatenate([g_in[0:1536], g_in[C_MA:C_MA + 8], g_in[1536:1920], kr1, kr2], axis=0)
    g1 = g_q12[0:1024].reshape(HEADS, 128, Q_RANK)
    g2 = g_q12[1024:2048].reshape(HEADS, 128, Q_RANK)
    wuq_t = jnp.concatenate([g1[:, 0:64], g1[:, 64:80] + g2[:, 80:96], g1[:, 80:96] + g2[:, 64:80]], axis=1).reshape(768, Q_RANK)
    gk = g_kv[:, 0:1024].reshape(KV_RANK, HEADS, 128)
    gv = g_kv[:, 1024:1536].reshape(KV_RANK, HEADS, 64)
    wukv = jnp.concatenate([gk[:, :, 0:64], gv], axis=2).reshape(KV_RANK, 1024)
    return win_t, wuq_t, wukv


def _selectors():
    sel = np.zeros((384, 1024), np.float32)
    sel_t = np.zeros((1024, LANES), np.float32)
    for h in range(HEADS):
        for piece in range(3):
            sel[LANES * piece + h, LANES * h + 64 + piece] = 1.0
        sel_t[LANES * h + 64, h] = 1.0
    return jnp.asarray(sel, BF16), jnp.asarray(sel_t, BF16)


def _rope_tables(positions):
    inv_freq = 10000.0 ** (-jnp.arange(0, ROPE, 2, dtype=F32) / ROPE)
    n = positions.size
    ang = (positions.reshape(n // 8, 8, 1).astype(F32) * inv_freq[None, None, :]).reshape(n // 8, 8 * (ROPE // 2))
    cos, sin = lax.optimization_barrier((jnp.cos(lax.optimization_barrier(ang)), jnp.sin(lax.optimization_barrier(ang))))
    cos, sin = cos.reshape(n, ROPE // 2), sin.reshape(n, ROPE // 2)
    z64, z32 = jnp.zeros((n, 64), F32), jnp.zeros((n, 32), F32)
    return jnp.concatenate([z64, cos, cos, z32], axis=1), jnp.concatenate([z64, -sin, sin, z32], axis=1)


def _work(name, t):
    return jnp.swapaxes(t[0], 0, 1) if name in TRANSPOSED else t[0]


def _back(name, t):
    return (jnp.swapaxes(t, 0, 1) if name in TRANSPOSED else t)[None]


SMALL_ROWS = {"norm_mix_g": (0, 1024), "norm_ffn_g": (1, 1024), "final_norm_g": (2, 1024), "q_norm_g": (4, 256),
              "kv_norm_g": (5, 128), "b_fgate": (6, 8)}


def kernel(x, positions, norm_mix_g, w_in, b_fgate, q_norm_g, w_uq, kv_norm_g, w_ukv, fox_out_g, mla_out_g, w_o, norm_ffn_g, w_gate, w_up, w_down, final_norm_g, loss_target, m_norm_mix_g, m_w_in, m_b_fgate, m_q_norm_g, m_w_uq, m_kv_norm_g, m_w_ukv, m_fox_out_g, m_mla_out_g, m_w_o, m_norm_ffn_g, m_w_gate, m_w_up, m_w_down, m_final_norm_g, v_norm_mix_g, v_w_in, v_b_fgate, v_q_norm_g, v_w_uq, v_kv_norm_g, v_w_ukv, v_fox_out_g, v_mla_out_g, v_w_o, v_norm_ffn_g, v_w_gate, v_w_up, v_w_down, v_final_norm_g):
    names = ["norm_mix_g", "w_in", "b_fgate", "q_norm_g", "w_uq", "kv_norm_g", "w_ukv", "fox_out_g", "mla_out_g", "w_o",
             "norm_ffn_g", "w_gate", "w_up", "w_down", "final_norm_g"]
    wts = dict(zip(names, [norm_mix_g, w_in, b_fgate, q_norm_g, w_uq, kv_norm_g, w_ukv, fox_out_g, mla_out_g, w_o, norm_ffn_g,
                           w_gate, w_up, w_down, final_norm_g]))
    mom = dict(zip(names, [m_norm_mix_g, m_w_in, m_b_fgate, m_q_norm_g, m_w_uq, m_kv_norm_g, m_w_ukv, m_fox_out_g, m_mla_out_g,
                           m_w_o, m_norm_ffn_g, m_w_gate, m_w_up, m_w_down, m_final_norm_g]))
    var = dict(zip(names, [v_norm_mix_g, v_w_in, v_b_fgate, v_q_norm_g, v_w_uq, v_kv_norm_g, v_w_ukv, v_fox_out_g, v_mla_out_g,
                           v_w_o, v_norm_ffn_g, v_w_gate, v_w_up, v_w_down, v_final_norm_g]))
    shard = {n: _work(n, wts[n]) for n in HEAD3 + FFN4}
    nb, seq, _ = x.shape
    T = nb * seq
    tm, tq = min(ROW_TILE, seq), min(ATTN_TILE, seq)
    tt = min(WGRAD_TILE, T)
    xf = x.reshape(T, D_MODEL)
    tgt = loss_target.reshape(T, D_MODEL)
    chip = 2 * lax.axis_index("x") + lax.axis_index("y")

    mine = [shard[n].astype(BF16) for n in HEAD3]
    head = _run_exchange(_gather_split_exchange(mine), "gather_head")
    win4, wuq4, wukv4 = [lax.dynamic_update_slice(h, s[None], (chip, 0, 0)) for h, s in zip(head, mine)]
    a = _arrange(win4.reshape(-1, D_MODEL), wuq4.reshape(-1, Q_RANK), wukv4.transpose(1, 0, 2).reshape(KV_RANK, -1))
    sel, sel_t = _selectors()
    ct, st = _rope_tables(positions)
    bfg = jnp.concatenate([b_fgate, jnp.zeros((1, LANES - HEADS), F32)], axis=1)
    g1, gq, gkv = norm_mix_g, q_norm_g, kv_norm_g

    h1, qf, kf, vf, qm, km, vm, lat, qn, kvn = _in_proj(xf, g1, a["w_in"], a["w_q12"], a["w_k"], a["w_v"], gq, gkv, bfg, ct, st, sel, seq, tm)
    tqf = min(ATTN_FWD_TILE, seq)
    of, lse_f, (wo4, wg4) = _attn_fwd(qf, kf, vf, nb, seq, tqf, "fox_fwd", _gather_exchange([shard[n].astype(BF16) for n in FFN4[:2]]))
    om, lse_m, (wu4, wd4) = _attn_fwd(qm, km, vm, nb, seq, tqf, "mla_fwd", _gather_exchange([shard[n].astype(BF16) for n in FFN4[2:]]))
    a_cat, h2, hid, dg, du, dx3, dx2, dof, dom, st_mid = _mid(
        of, om, xf, tgt, fox_out_g, mla_out_g, norm_ffn_g, final_norm_g.reshape(1, D_MODEL),
        wo4.reshape(D_MODEL, D_MODEL), wg4.reshape(D_FF, D_MODEL), wu4.reshape(D_FF, D_MODEL), wd4.reshape(D_FF, D_MODEL), tm)

    slab = lambda g: g.reshape(N_CHIPS, g.shape[0] // N_CHIPS, g.shape[1])
    big = [slab(_wgrad(a_cat, dx2, D_MODEL, tt, "wgrad_o")), slab(_wgrad(dg, h2, D_FF // 2, tt, "wgrad_gate")),
           slab(_wgrad(du, h2, D_FF // 2, tt, "wgrad_up")), slab(_wgrad(hid, dx3, D_FF // 2, tt, "wgrad_down"))]
    dqf, dkf, dvf, got = _attn_bwd(qf, kf, vf, of, dof, lse_f, nb, seq, tq, "fox_bwd", True, _swap_exchange(big))
    sums = [_add_half(g, s) for g, s in zip(big, got)]
    dqm, dkm, dvm, recv = _attn_bwd(qm, km, vm, om, dom, lse_m, nb, seq, tq, "mla_bwd", False, _scatter_exchange(sums))
    halves = [_sum_slabs(g, s, r) for g, s, r in zip(big, got, recv)]
    dx, dproj, dq12, dkv, st_in, _ = _in_bwd(dqf, dkf, dvf, dqm, dkm, dvm, lat, xf, dx2, g1, gq, gkv, bfg, ct, st, sel_t,
                                             a["w_in"], a["w_q12"], a["w_kv"], seq, tm)
    loss_row = jnp.concatenate([jnp.sum(st_mid[3:4, :], axis=1, keepdims=True), jnp.zeros((1, D_MODEL - 1), F32)], axis=1)
    stats = jnp.concatenate([st_in[0:1], st_mid[1:2], st_mid[0:1], st_mid[2:3], st_in[1:2], st_in[2:3], st_in[3:4], loss_row], axis=0)
    g_in, results = _wgrad(dproj, h1, C_END, tt, "wgrad_in", _both(_join_exchange(halves), _everyone_exchange(stats)))
    gshard = dict(zip(FFN4, results[:4]))
    stats = _sum_devices(results[4])

    gwin_t, gwuq_t, gwukv = _unarrange(g_in, _wgrad(dq12, qn, 2048, tt, "wgrad_uq"), _wgrad(kvn, dkv, KV_RANK, tt, "wgrad_ukv"))
    tail = [slab(gwin_t), slab(gwuq_t), gwukv.reshape(KV_RANK, N_CHIPS, -1).transpose(1, 0, 2)]
    tail_got = _run_exchange(_swap_exchange(tail), "tail_swap")
    tail_sums = [_add_half(g, s) for g, s in zip(tail, tail_got)]
    tail_recv = _run_exchange(_scatter_exchange(tail_sums), "tail_scatter")
    tail_joined = _run_exchange(_join_exchange([_sum_slabs(g, s, r) for g, s, r in zip(tail, tail_got, tail_recv)]), "tail_join")
    gshard.update(zip(HEAD3, tail_joined))
    quad = lambda n: (shard[n], gshard[n], _work(n, mom[n]), _work(n, var[n]))
    updates = dict(zip(FFN4[1:], _adamw([quad(n) for n in FFN4[1:]], "adamw_ffn")))
    for n in HEAD3 + FFN4[:1]:
        updates[n], = _adamw([quad(n)], "adamw_" + n)

    grads, delta, new_m, new_v = {}, {}, {}, {}
    for n in HEAD3 + FFN4:
        grads[n] = _back(n, gshard[n])
        delta[n], new_m[n], new_v[n] = [_back(n, t) for t in updates[n]]
    sm_g = {n: stats[row:row + 1, 0:width] for n, (row, width) in SMALL_ROWS.items()}
    sm_g["fox_out_g"] = stats[3:4, 0:512]
    sm_g["mla_out_g"] = stats[3:4, 512:1024]
    pad = lambda t: jnp.pad(t.reshape(1, -1), ((0, 0), (0, 1024 - t.size)))
    stack = lambda d: jnp.concatenate([pad(d[n]) for n in SMALL], axis=0)
    (sd, sm, sv), = _adamw([(stack(wts), stack(sm_g), stack(mom), stack(var))], "adamw_small")
    for i, n in enumerate(SMALL):
        shp = wts[n].shape
        grads[n] = sm_g[n].reshape(shp)
        delta[n] = sd[i, 0:wts[n].size].reshape(shp)
        new_m[n] = sm[i, 0:wts[n].size].reshape(shp)
        new_v[n] = sv[i, 0:wts[n].size].reshape(shp)
    loss = stats[7, 0]
    return (loss, dx.reshape(x.shape), *[grads[n] for n in names], *[delta[n] for n in names],
            *[new_m[n] for n in names], *[new_v[n] for n in names])
```

```python
import functools

import numpy as np
import jax
import jax.numpy as jnp
from jax import lax
from jax.experimental import pallas as pl
from jax.experimental.pallas import tpu as pltpu

F32 = jnp.float32
BF16 = jnp.bfloat16
MESH = pl.DeviceIdType.MESH

EPS = 1e-6
D_MODEL = 1024
HEADS = 8
PAIRS = HEADS // 2
FOX_W = 512
Q_RANK = 256
KV_RANK = 128
ROPE = 32
D_FF = 2816
N_CHIPS = 4
FOX_SCALE = 64 ** -0.5
MLA_SCALE = 96 ** -0.5
LANES = 128
NEG = -1e30

ADAM_LR, ADAM_B1, ADAM_B2, ADAM_EPS, ADAM_WD, ADAM_STEP = 0.001, 0.9, 0.999, 1e-08, 0.01, 10

C_FQ, C_FK, C_FV, C_QL, C_KVL, C_MA, C_END = 0, 512, 1024, 1536, 1792, 1920, 2048
C_MB = C_END

VMEM_LIMIT = 60 * 1024 * 1024
ROW_TILE = 256
ATTN_TILE = 512
ATTN_FWD_TILE = 1024
WGRAD_TILE = 2048

HEAD3 = ("w_in", "w_uq", "w_ukv")
FFN4 = ("w_o", "w_gate", "w_up", "w_down")
TRANSPOSED = ("w_in", "w_uq", "w_gate", "w_up")
SMALL = ("norm_mix_g", "b_fgate", "q_norm_g", "kv_norm_g", "fox_out_g", "mla_out_g", "norm_ffn_g", "final_norm_g")


def _params(sem=None):
    return pltpu.CompilerParams(dimension_semantics=sem, vmem_limit_bytes=VMEM_LIMIT)


def _full(shape):
    n = len(shape)
    return pl.BlockSpec(shape, lambda *_: (0,) * n, pipeline_mode=pl.Buffered(1))


def _dot(a, b):
    return jnp.dot(a, b, preferred_element_type=F32)


def _dot_nt(a, b):
    return lax.dot_general(a, b, (((1,), (1,)), ((), ())), preferred_element_type=F32)


def _dot_tn(a, b):
    return lax.dot_general(a, b, (((0,), (0,)), ((), ())), preferred_element_type=F32)


def _split3(v):
    hi = v.astype(BF16)
    r1 = v - hi.astype(F32)
    mid = r1.astype(BF16)
    lo = (r1 - mid.astype(F32)).astype(BF16)
    return hi, mid, lo


def _rms(v, width):
    return lax.rsqrt(jnp.sum(v * v, axis=1, keepdims=True) * (1.0 / width) + EPS)


def _rms_bwd(dy, xhat, r, g, width):
    u = dy * g
    return r * (u - xhat * (jnp.sum(u * xhat, axis=1, keepdims=True) * (1.0 / width)))


ANY = pl.BlockSpec(memory_space=pl.ANY)


def _place():
    return lax.axis_index("x"), lax.axis_index("y"), lax.axis_index("c")


def _other_chips(x, y):
    return [(1 - x, y), (x, 1 - y), (1 - x, 1 - y)]


def _remote(src, dst, send, recv, j, dev):
    return pltpu.make_async_remote_copy(src_ref=src, dst_ref=dst, send_sem=send.at[j], recv_sem=recv.at[j], device_id=dev, device_id_type=MESH)


class _Exchange:
    def __init__(self, ins, outs, n_remote, n_local, build, in_place=False):
        self.ins, self.outs, self.n_remote, self.n_local, self.build = list(ins), list(outs), n_remote, max(n_local, 1), build
        self.in_place = in_place
        self.n_aliased = len(self.ins)

    def aliases(self, first_in, first_out):
        return {first_in + i: first_out + i for i in range(self.n_aliased)} if self.in_place else {}

    def sems(self):
        return [pltpu.SemaphoreType.DMA((self.n_remote,)), pltpu.SemaphoreType.DMA((self.n_remote,)), pltpu.SemaphoreType.DMA((self.n_local,))]

    def start(self, in_refs, out_refs, sems):
        for cp in self.build(in_refs, out_refs, *sems)[0]:
            cp.start()

    def wait(self, in_refs, out_refs, sems):
        for w in self.build(in_refs, out_refs, *sems)[1]:
            w()


def _gather_exchange(shards, own=True):
    def build(ins, outs, send, recv, lsem):
        x, y, c = _place()
        starts, waits = [], []
        for i, (s, o) in enumerate(zip(ins, outs)):
            if own:
                mine = pltpu.make_async_copy(s, o.at[2 * x + y], lsem.at[i])
                starts.append(mine)
                waits.append(mine.wait)
            for j, (cx, cy) in enumerate(_other_chips(x, y)):
                out = _remote(s, o.at[2 * x + y], send, recv, 3 * i + j, (cx, cy, c))
                starts.append(out)
                waits.append(_remote(s, o.at[2 * cx + cy], send, recv, 3 * i + j, (cx, cy, c)).wait_recv)
                waits.append(out.wait_send)
        return starts, waits

    outs = [jax.ShapeDtypeStruct((N_CHIPS,) + s.shape, s.dtype) for s in shards]
    return _Exchange(shards, outs, 3 * len(shards), len(shards), build)


def _gather_split_exchange(shards):
    n = len(shards)

    def build(ins, outs, send, recv, lsem):
        x, y, c = _place()
        starts, waits, last = [], [], []
        for i, (s, o) in enumerate(zip(ins, outs)):
            hc = s.shape[1] // 2
            mine, other = pl.ds(c * hc, hc), pl.ds((1 - c) * hc, hc)
            for j, (cx, cy) in enumerate(_other_chips(x, y)):
                out = _remote(s.at[:, mine], o.at[2 * x + y, :, mine], send, recv, 3 * i + j, (cx, cy, c))
                landed = o.at[2 * cx + cy, :, mine]
                arrive = _remote(s.at[:, mine], landed, send, recv, 3 * i + j, (cx, cy, c))
                onward = _remote(landed, landed, send, recv, 3 * n + 3 * i + j, (x, y, 1 - c))
                from_sibling = _remote(landed, o.at[2 * cx + cy, :, other], send, recv, 3 * n + 3 * i + j, (x, y, 1 - c))
                starts.append(out)
                waits.append(lambda arrive=arrive, onward=onward: (arrive.wait_recv(), onward.start()))
                last += [from_sibling.wait_recv, onward.wait_send, out.wait_send]
        return starts, waits + last

    outs = [jax.ShapeDtypeStruct((N_CHIPS,) + s.shape, s.dtype) for s in shards]
    return _Exchange(shards, outs, 6 * n, 0, build)


def _swap_exchange(grads):
    def build(ins, outs, send, recv, lsem):
        x, y, c = _place()
        cps = []
        for i, (g, o) in enumerate(zip(ins, outs)):
            hc = g.shape[2] // 2
            cps.append(_remote(g.at[:, :, pl.ds((1 - c) * hc, hc)], o, send, recv, i, (x, y, 1 - c)))
        return cps, [cp.wait for cp in cps]

    outs = [jax.ShapeDtypeStruct((g.shape[0], g.shape[1], g.shape[2] // 2), g.dtype) for g in grads]
    return _Exchange(grads, outs, len(grads), 0, build)


def _scatter_exchange(sums):
    def build(ins, outs, send, recv, lsem):
        x, y, c = _place()
        cps = []
        for i, (s, o) in enumerate(zip(ins, outs)):
            for j, (cx, cy) in enumerate(_other_chips(x, y)):
                cps.append(_remote(s.at[2 * cx + cy], o.at[j], send, recv, 3 * i + j, (cx, cy, c)))
        return cps, [cp.wait for cp in cps]

    outs = [jax.ShapeDtypeStruct((3,) + s.shape[1:], s.dtype) for s in sums]
    return _Exchange(sums, outs, 3 * len(sums), 0, build)


def _join_exchange(bufs):
    def build(ins, outs, send, recv, lsem):
        x, y, c = _place()
        starts, waits = [], []
        for i, (t, o) in enumerate(zip(ins, outs)):
            hc = t.shape[1] // 2
            out = _remote(t.at[:, pl.ds(c * hc, hc)], o.at[:, pl.ds(c * hc, hc)], send, recv, i, (x, y, 1 - c))
            starts.append(out)
            waits += [_remote(t.at[:, pl.ds(c * hc, hc)], o.at[:, pl.ds((1 - c) * hc, hc)], send, recv, i, (x, y, 1 - c)).wait_recv,
                      out.wait_send]
        return starts, waits

    outs = [jax.ShapeDtypeStruct(t.shape, t.dtype) for t in bufs]
    return _Exchange(bufs, outs, len(bufs), 0, build, in_place=True)


def _everyone_exchange(v):
    def build(ins, outs, send, recv, lsem):
        x, y, c = _place()
        me = 4 * x + 2 * y + c
        mine = pltpu.make_async_copy(ins[0], outs[0].at[me], lsem.at[0])
        starts, waits = [mine], [mine.wait]
        for j in range(7):
            fx, fy, fc = (j + 1) >> 2 & 1, (j + 1) >> 1 & 1, (j + 1) & 1
            peer = (x ^ fx, y ^ fy, c ^ fc)
            out = _remote(ins[0], outs[0].at[me], send, recv, j, peer)
            starts.append(out)
            waits += [_remote(ins[0], outs[0].at[4 * peer[0] + 2 * peer[1] + peer[2]], send, recv, j, peer).wait_recv, out.wait_send]
        return starts, waits

    return _Exchange([v], [jax.ShapeDtypeStruct((8,) + v.shape, v.dtype)], 7, 1, build)


def _both(a, b):
    na_in, na_out = len(a.ins), len(a.outs)

    def build(ins, outs, send, recv, lsem):
        sa, wa = a.build(ins[:na_in], outs[:na_out], send.at[pl.ds(0, a.n_remote)], recv.at[pl.ds(0, a.n_remote)],
                         lsem.at[pl.ds(0, a.n_local)])
        sb, wb = b.build(ins[na_in:], outs[na_out:], send.at[pl.ds(a.n_remote, b.n_remote)], recv.at[pl.ds(a.n_remote, b.n_remote)],
                         lsem.at[pl.ds(a.n_local, b.n_local)])
        return sa + sb, wa + wb

    both = _Exchange(a.ins + b.ins, a.outs + b.outs, a.n_remote + b.n_remote, a.n_local + b.n_local, build, in_place=a.in_place)
    both.n_aliased = na_in
    return both


def _run_exchange(ex, name):
    n_in, n_out = len(ex.ins), len(ex.outs)

    def body(*refs):
        ins, outs, sems = refs[:n_in], refs[n_in:n_in + n_out], refs[n_in + n_out:]
        ex.start(ins, outs, sems)
        ex.wait(ins, outs, sems)

    return pl.pallas_call(
        body, name=name, out_shape=tuple(ex.outs), in_specs=[ANY] * n_in, out_specs=tuple([ANY] * n_out),
        scratch_shapes=ex.sems(), input_output_aliases=ex.aliases(0, 0),
        compiler_params=pltpu.CompilerParams(has_side_effects=True),
    )(*ex.ins)


def _sum_devices(rows):
    def body(r_ref, o_ref):
        acc = r_ref[0]
        for d in range(1, 8):
            acc = acc + r_ref[d]
        o_ref[...] = acc

    vm = pl.BlockSpec(memory_space=pltpu.VMEM)
    return pl.pallas_call(body, name="sum_devices", out_shape=jax.ShapeDtypeStruct(rows.shape[1:], rows.dtype),
                          in_specs=[vm], out_specs=vm)(rows)


def _add_half(g, got):
    n, R, C = g.shape
    hc = C // 2

    def body(c_ref, g_ref, r_ref, o_ref):
        o_ref[...] = (g_ref[...] + r_ref[...]).astype(BF16)

    c = lax.axis_index("c")
    return pl.pallas_call(
        body, name="add_half",
        grid_spec=pltpu.PrefetchScalarGridSpec(
            num_scalar_prefetch=1, grid=(n,),
            in_specs=[pl.BlockSpec((1, R, hc), lambda k, c_ref: (k, 0, c_ref[0])),
                      pl.BlockSpec((1, R, hc), lambda k, c_ref: (k, 0, 0))],
            out_specs=pl.BlockSpec((1, R, hc), lambda k, c_ref: (k, 0, 0))),
        out_shape=jax.ShapeDtypeStruct((n, R, hc), BF16),
        compiler_params=_params(("arbitrary",)),
    )(jnp.reshape(c, (1,)).astype(jnp.int32), g, got)


def _sum_slabs(g, got, recv):
    n, R, C = g.shape
    hc = C // 2

    def body(kc_ref, g_ref, s_ref, r_ref, o_ref):
        o_ref[...] = (((g_ref[0] + s_ref[0]) + r_ref[0].astype(F32)) + r_ref[1].astype(F32)) + r_ref[2].astype(F32)

    kc = jnp.stack([2 * lax.axis_index("x") + lax.axis_index("y"), lax.axis_index("c")]).astype(jnp.int32)
    return pl.pallas_call(
        body, name="sum_slabs",
        grid_spec=pltpu.PrefetchScalarGridSpec(
            num_scalar_prefetch=1, grid=(1,),
            in_specs=[pl.BlockSpec((1, R, hc), lambda i, kc_ref: (kc_ref[0], 0, kc_ref[1])),
                      pl.BlockSpec((1, R, hc), lambda i, kc_ref: (kc_ref[0], 0, 0)),
                      pl.BlockSpec((3, R, hc), lambda i, kc_ref: (0, 0, 0))],
            out_specs=pl.BlockSpec((R, hc), lambda i, kc_ref: (0, kc_ref[1]))),
        out_shape=jax.ShapeDtypeStruct((R, C), F32),
        compiler_params=_params(("arbitrary",)),
    )(kc, g, got, recv)


def _row_tile(rows):
    for cand in (256, 184, 176, 144, 128, 64, 32, 16, 8):
        if rows % cand == 0:
            return cand
    return rows


def _in_proj(x, g1, w_in, w_q12, w_k, w_v, gq, gkv, bfg, ct, st, sel, seq, tm):
    T = x.shape[0]
    nsb = seq // tm

    def body(x_ref, g1_ref, win_ref, wq_ref, wk_ref, wv_ref, gq_ref, gkv_ref, b_ref, ct_ref, st_ref, sel_ref,
             h1_ref, qf_ref, kf_ref, vf_ref, qm_ref, km_ref, vm_ref, lat_ref, qn_ref, kvn_ref, carry):
        i = pl.program_id(0)

        @pl.when(i % nsb == 0)
        def _():
            carry[...] = jnp.zeros_like(carry)

        xv = x_ref[...]
        h = (xv * _rms(xv, D_MODEL) * g1_ref[...]).astype(BF16)
        h1_ref[...] = h
        proj = _dot_nt(h, win_ref[...])
        lane = lax.broadcasted_iota(jnp.int32, (tm, LANES), 1)
        low = lane < 64
        misc_a = proj[:, C_MA:C_END]
        misc_b = pltpu.roll(misc_a, 96, 1)

        z = misc_a + b_ref[...]
        lf = jnp.where(lane < HEADS, jnp.minimum(z, 0.0) - jnp.log1p(jnp.exp(-jnp.abs(z))), 0.0)
        rr = lax.broadcasted_iota(jnp.int32, (tm, tm), 0)
        cc = lax.broadcasted_iota(jnp.int32, (tm, tm), 1)
        tri = (rr >= cc).astype(BF16)
        a0, a1, a2 = _split3(lf)
        c = _dot(tri, a0) + _dot(tri, a1) + _dot(tri, a2) + carry[0:1, :]
        carry[0:1, :] = c[tm - 1:tm, :]
        c0, c1, c2 = _split3(c)
        cpl = _dot(jnp.concatenate([c0, c1, c2], axis=1), sel_ref[...])
        qpad = jnp.where((lane >= 64) & (lane < 67), -1.0, 0.0)
        for j in range(PAIRS):
            qc = proj[:, C_FQ + LANES * j:C_FQ + LANES * (j + 1)] * FOX_SCALE
            kc = proj[:, C_FK + LANES * j:C_FK + LANES * (j + 1)]
            e, o = 2 * LANES * j, 2 * LANES * j + LANES
            qf_ref[:, e:e + LANES] = jnp.where(low, qc, qpad).astype(BF16)
            qf_ref[:, o:o + LANES] = jnp.where(low, pltpu.roll(qc, 64, 1), qpad).astype(BF16)
            kf_ref[:, e:e + LANES] = jnp.where(low, kc, cpl[:, e:e + LANES]).astype(BF16)
            kf_ref[:, o:o + LANES] = jnp.where(low, pltpu.roll(kc, 64, 1), cpl[:, o:o + LANES]).astype(BF16)
        vf_ref[...] = proj[:, C_FV:C_QL].astype(BF16)

        ql = proj[:, C_QL:C_KVL]
        kvl = proj[:, C_KVL:C_MA]
        qn = (ql * _rms(ql, Q_RANK) * gq_ref[...]).astype(BF16)
        kvn = (kvl * _rms(kvl, KV_RANK) * gkv_ref[...]).astype(BF16)
        lat_ref[...] = proj[:, C_QL:C_MB]
        qn_ref[...] = qn
        kvn_ref[...] = kvn
        q12 = _dot_nt(qn, wq_ref[...])
        kn = _dot(kvn, wk_ref[...])
        ctv = ct_ref[...]
        stv = st_ref[...]
        cq = (jnp.where(low, 1.0, 0.0) + ctv) * MLA_SCALE
        sq = stv * MLA_SCALE
        kpe = misc_a * ctv + misc_b * stv
        for hd in range(HEADS):
            s0 = LANES * hd
            qm_ref[:, s0:s0 + LANES] = (q12[:, s0:s0 + LANES] * cq + q12[:, 1024 + s0:1024 + s0 + LANES] * sq).astype(BF16)
            km_ref[:, s0:s0 + LANES] = (kn[:, s0:s0 + LANES] + kpe).astype(BF16)
        vm_ref[...] = _dot(kvn, wv_ref[...]).astype(BF16)

    row = lambda w: pl.BlockSpec((tm, w), lambda i: (i, 0))
    out_shape = (
        jax.ShapeDtypeStruct((T, D_MODEL), BF16),
        jax.ShapeDtypeStruct((T, 1024), BF16), jax.ShapeDtypeStruct((T, 1024), BF16), jax.ShapeDtypeStruct((T, 512), BF16),
        jax.ShapeDtypeStruct((T, 1024), BF16), jax.ShapeDtypeStruct((T, 1024), BF16), jax.ShapeDtypeStruct((T, 512), BF16),
        jax.ShapeDtypeStruct((T, 512), F32),
        jax.ShapeDtypeStruct((T, Q_RANK), BF16), jax.ShapeDtypeStruct((T, KV_RANK), BF16),
    )
    return pl.pallas_call(
        body, name="in_proj", grid=(T // tm,), out_shape=out_shape,
        in_specs=[row(D_MODEL), _full(g1.shape), _full(w_in.shape), _full(w_q12.shape), _full(w_k.shape), _full(w_v.shape),
                  _full(gq.shape), _full(gkv.shape), _full(bfg.shape), row(LANES), row(LANES), _full(sel.shape)],
        out_specs=[row(D_MODEL), row(1024), row(1024), row(512), row(1024), row(1024), row(512), row(512), row(Q_RANK), row(KV_RANK)],
        scratch_shapes=[pltpu.VMEM((8, LANES), F32)],
        compiler_params=_params(("arbitrary",)),
    )(x, g1, w_in, w_q12, w_k, w_v, gq, gkv, bfg, ct, st, sel)


def _attn_fwd(q, k, v, nb, seq, tq, name, ex=None):
    T = q.shape[0]
    nq = seq // tq
    n_in, n_out = (len(ex.ins), len(ex.outs)) if ex else (0, 0)

    def body(*refs):
        q_ref, k_ref, v_ref = refs[0:3]
        o_ref, lse_ref = refs[3 + n_in:5 + n_in]
        b, pr, qi = pl.program_id(0), pl.program_id(1), pl.program_id(2)
        if ex:
            ex_refs = (refs[3:3 + n_in], refs[5 + n_in:5 + n_in + n_out], refs[8 + n_in + n_out:])

            @pl.when((b == 0) & (pr == 0) & (qi == 0))
            def _():
                ex.start(*ex_refs)

        s_sc, p_sc, acc_sc = refs[5 + n_in + n_out:8 + n_in + n_out]
        strip = 64
        key_s = lax.broadcasted_iota(jnp.int32, (strip, tq), 0)
        qry_s = lax.broadcasted_iota(jnp.int32, (strip, tq), 1)
        row_t = lax.broadcasted_iota(jnp.int32, (LANES, tq), 0)
        acc_sc[...] = jnp.zeros(acc_sc.shape, F32)

        def fold(x, op):
            out = x[0:8]
            for r in range(8, strip, 8):
                out = op(out, x[r:r + 8])
            return out

        def step(kj, state, masked):
            rows = pl.ds(pl.multiple_of(kj * tq, tq), tq)
            for hh in range(2):
                s_sc[hh] = _dot_nt(k_ref[rows, LANES * hh:LANES * (hh + 1)], q_ref[:, LANES * hh:LANES * (hh + 1)])
            vv = v_ref[rows, :]
            new = []
            for hh in range(2):
                m, l = state[hh]

                def strip_of(r0, hh=hh):
                    s = s_sc[hh, r0:r0 + strip, :]
                    return jnp.where(key_s + r0 <= qry_s, s, NEG) if masked else s

                mx = fold(strip_of(0), jnp.maximum)
                for r0 in range(strip, tq, strip):
                    mx = jnp.maximum(mx, fold(strip_of(r0), jnp.maximum))
                m_new = jnp.maximum(m, jnp.max(mx, axis=0, keepdims=True))
                alpha = jnp.exp(m - m_new)
                sm = jnp.zeros((8, tq), F32)
                for r0 in range(0, tq, strip):
                    p = jnp.exp(strip_of(r0) - m_new)
                    sm = sm + fold(p, jnp.add)
                    p_sc[hh, r0:r0 + strip, :] = p.astype(BF16)
                l = alpha * l + jnp.sum(sm, axis=0, keepdims=True)
                acc_sc[hh] = alpha * acc_sc[hh] + _dot_tn(vv, p_sc[hh])
                new.append((m_new, l))
            return tuple(new)

        one = (jnp.full((1, tq), NEG, F32), jnp.zeros((1, tq), F32))
        state = lax.fori_loop(0, qi, functools.partial(step, masked=False), (one, one))
        (m0, l0), (m1, l1) = step(qi, state, True)
        o_ref[...] = jnp.where(row_t < 64, acc_sc[0] / l0, acc_sc[1] / l1).T
        lse_ref[:, 0:LANES] = jnp.broadcast_to(m0 + jnp.log(l0), (LANES, tq)).T
        lse_ref[:, LANES:2 * LANES] = jnp.broadcast_to(m1 + jnp.log(l1), (LANES, tq)).T

        if ex:
            @pl.when((b == nb - 1) & (pr == PAIRS - 1) & (qi == nq - 1))
            def _():
                ex.wait(*ex_refs)

    res = pl.pallas_call(
        body, name=name, grid=(nb, PAIRS, nq),
        out_shape=(jax.ShapeDtypeStruct((T, 512), F32), jax.ShapeDtypeStruct((T, 1024), F32)) + tuple(ex.outs if ex else ()),
        in_specs=[pl.BlockSpec((tq, 2 * LANES), lambda b, p, i: (b * nq + i, p)),
                  pl.BlockSpec((seq, 2 * LANES), lambda b, p, i: (b, p)),
                  pl.BlockSpec((seq, LANES), lambda b, p, i: (b, p))] + [ANY] * n_in,
        out_specs=[pl.BlockSpec((tq, LANES), lambda b, p, i: (b * nq + i, p)),
                   pl.BlockSpec((tq, 2 * LANES), lambda b, p, i: (b * nq + i, p))] + [ANY] * n_out,
        scratch_shapes=[pltpu.VMEM((2, tq, tq), F32), pltpu.VMEM((2, tq, tq), BF16), pltpu.VMEM((2, LANES, tq), F32)]
        + (ex.sems() if ex else []),
        compiler_params=_params(("arbitrary", "arbitrary", "arbitrary")),
    )(q, k, v, *(ex.ins if ex else ()))
    return res[0], res[1], list(res[2:])


def _attn_bwd(q, k, v, o, do, lse, nb, seq, tq, name, key_bias, ex=None):
    T = q.shape[0]
    nq = seq // tq
    n_in, n_out = (len(ex.ins), len(ex.outs)) if ex else (0, 0)

    def body(*refs):
        q_ref, k_ref, v_ref, o_ref, do_ref, lse_ref = refs[0:6]
        dq_ref, dk_ref, dv_ref = refs[6 + n_in:9 + n_in]
        dsc, rsum = refs[9 + n_in + n_out:11 + n_in + n_out]
        b, pr, step_no = pl.program_id(0), pl.program_id(1), pl.program_id(2)
        kj = nq - 1 - step_no
        if ex:
            ex_refs = (refs[6:6 + n_in], refs[9 + n_in:9 + n_in + n_out], refs[11 + n_in + n_out:])

            @pl.when((b == 0) & (pr == 0) & (step_no == 0))
            def _():
                ex.start(*ex_refs)

        lane_s = lax.broadcasted_iota(jnp.int32, (seq, LANES), 1)
        lane = lax.broadcasted_iota(jnp.int32, (tq, LANES), 1)
        rr = lax.broadcasted_iota(jnp.int32, (tq, tq), 0)
        cc = lax.broadcasted_iota(jnp.int32, (tq, tq), 1)

        @pl.when(step_no == 0)
        def _():
            dq_ref[...] = jnp.zeros_like(dq_ref)
            prod = do_ref[...].astype(F32) * o_ref[...]
            d0 = jnp.sum(jnp.where(lane_s < 64, prod, 0.0), axis=1, keepdims=True)
            d1 = jnp.sum(jnp.where(lane_s < 64, 0.0, prod), axis=1, keepdims=True)
            dsc[0] = jnp.broadcast_to(d0, (seq, LANES))
            dsc[1] = jnp.broadcast_to(d1, (seq, LANES))
            if key_bias:
                rsum[...] = jnp.zeros_like(rsum)

        vv = v_ref[...]

        def step(qi, carry, masked):
            dkt, dvt, cols = carry
            rows = pl.ds(pl.multiple_of(qi * tq, tq), tq)
            dov = do_ref[rows, :]
            new_dkt, new_cols = [], []
            for hh in range(2):
                qv = q_ref[rows, LANES * hh:LANES * (hh + 1)]
                kv = k_ref[:, LANES * hh:LANES * (hh + 1)]
                dom = jnp.where((lane < 64) if hh == 0 else (lane >= 64), dov, jnp.zeros((), BF16))
                s = _dot_nt(qv, kv)
                if masked:
                    s = jnp.where(cc <= rr, s, NEG)
                p = jnp.exp(s - jnp.tile(lse_ref[rows, LANES * hh:LANES * (hh + 1)], (1, tq // LANES)))
                dp = _dot_nt(dom, vv)
                ds32 = p * (dp - jnp.tile(dsc[hh, rows, :], (1, tq // LANES)))
                col = cols[hh]
                if key_bias:
                    col = col + jnp.sum(ds32, axis=0, keepdims=True)
                    rsum[hh, rows, :] += jnp.broadcast_to(jnp.sum(ds32, axis=1, keepdims=True), (tq, LANES))
                ds = ds32.astype(BF16)
                dvt = dvt + _dot_tn(dom, p.astype(BF16))
                new_dkt.append(dkt[hh] + _dot_tn(qv, ds))
                new_cols.append(col)
                dq_ref[rows, LANES * hh:LANES * (hh + 1)] += _dot(ds, kv)
            return tuple(new_dkt), dvt, tuple(new_cols)

        zt = jnp.zeros((LANES, tq), F32)
        zc = jnp.zeros((1, tq), F32)
        carry = step(kj, ((zt, zt), zt, (zc, zc)), True)
        dkt, dvt, cols = lax.fori_loop(kj + 1, nq, functools.partial(step, masked=False), carry)
        row_t = lax.broadcasted_iota(jnp.int32, (LANES, tq), 0)
        for hh in range(2):
            dk_h = jnp.where(row_t == 64, -cols[hh], dkt[hh]) if key_bias else dkt[hh]
            dk_ref[:, LANES * hh:LANES * (hh + 1)] = dk_h.T
        dv_ref[...] = dvt.T

        if key_bias:
            @pl.when(step_no == nq - 1)
            def _():
                for hh in range(2):
                    blk = dq_ref[:, LANES * hh:LANES * (hh + 1)]
                    dq_ref[:, LANES * hh:LANES * (hh + 1)] = jnp.where(lane_s == 64, rsum[hh], blk)

        if ex:
            @pl.when((b == nb - 1) & (pr == PAIRS - 1) & (step_no == nq - 1))
            def _():
                ex.wait(*ex_refs)

    per_seq = lambda w: pl.BlockSpec((seq, w), lambda b, p, j: (b, p))
    per_blk = lambda w: pl.BlockSpec((tq, w), lambda b, p, j: (b * nq + nq - 1 - j, p))
    res = pl.pallas_call(
        body, name=name, grid=(nb, PAIRS, nq),
        out_shape=(jax.ShapeDtypeStruct((T, 1024), F32), jax.ShapeDtypeStruct((T, 1024), F32), jax.ShapeDtypeStruct((T, 512), F32))
        + tuple(ex.outs if ex else ()),
        in_specs=[per_seq(2 * LANES), per_blk(2 * LANES), per_blk(LANES), per_seq(LANES), per_seq(LANES), per_seq(2 * LANES)] + [ANY] * n_in,
        out_specs=[per_seq(2 * LANES), per_blk(2 * LANES), per_blk(LANES)] + [ANY] * n_out,
        scratch_shapes=[pltpu.VMEM((2, seq, LANES), F32), pltpu.VMEM((2, seq, LANES) if key_bias else (2, 8, LANES), F32)]
        + (ex.sems() if ex else []),
        compiler_params=_params(("arbitrary", "arbitrary", "arbitrary")),
    )(q, k, v, o, do, lse, *(ex.ins if ex else ()))
    return res[0], res[1], res[2], list(res[3:])


def _mid(of, om, x, tgt, g_fo, g_mo, g2, g3, w_o, w_g, w_u, w_d, tm):
    T = x.shape[0]

    def body(of_ref, om_ref, x_ref, t_ref, gfo_ref, gmo_ref, g2_ref, g3_ref, wo_ref, wg_ref, wu_ref, wd_ref,
             a_ref, h2_ref, hid_ref, dg_ref, du_ref, dx3_ref, dx2_ref, dof_ref, dom_ref, st_ref):
        i = pl.program_id(0)

        @pl.when(i == 0)
        def _():
            st_ref[...] = jnp.zeros_like(st_ref)

        ofv, omv = of_ref[...], om_ref[...]
        rf, rm = _rms(ofv, FOX_W), _rms(omv, FOX_W)
        fhat, mhat = ofv * rf, omv * rm
        a = jnp.concatenate([fhat * gfo_ref[...], mhat * gmo_ref[...]], axis=1).astype(BF16)
        a_ref[...] = a
        x2 = x_ref[...] + _dot(a, wo_ref[...])
        r2 = _rms(x2, D_MODEL)
        xh2 = x2 * r2
        h2 = (xh2 * g2_ref[...]).astype(BF16)
        h2_ref[...] = h2
        gt = _dot_nt(h2, wg_ref[...])
        up = _dot_nt(h2, wu_ref[...])
        sg = jax.nn.sigmoid(gt)
        sl = gt * sg
        hid = (sl * up).astype(BF16)
        hid_ref[...] = hid
        x3 = x2 + _dot(hid, wd_ref[...])
        r3 = _rms(x3, D_MODEL)
        xh3 = x3 * r3
        diff = xh3 * g3_ref[...] - t_ref[...]
        dy = diff * (1.0 / D_MODEL)
        st_ref[3:4, :] += jnp.sum(diff * diff, axis=0, keepdims=True) * (0.5 / D_MODEL)
        st_ref[0:1, :] += jnp.sum(dy * xh3, axis=0, keepdims=True)
        dx3 = _rms_bwd(dy, xh3, r3, g3_ref[...], D_MODEL)
        dx3b = dx3.astype(BF16)
        dx3_ref[...] = dx3b
        dhid = _dot_nt(dx3b, wd_ref[...])
        dg = (dhid * up * (sg * (1.0 + gt * (1.0 - sg)))).astype(BF16)
        du = (dhid * sl).astype(BF16)
        dg_ref[...] = dg
        du_ref[...] = du
        dh2 = _dot(dg, wg_ref[...]) + _dot(du, wu_ref[...])
        st_ref[1:2, :] += jnp.sum(dh2 * xh2, axis=0, keepdims=True)
        dx2 = dx3 + _rms_bwd(dh2, xh2, r2, g2_ref[...], D_MODEL)
        dx2_ref[...] = dx2
        da = _dot_nt(dx2.astype(BF16), wo_ref[...])
        daf, dam = da[:, 0:FOX_W], da[:, FOX_W:2 * FOX_W]
        st_ref[2:3, 0:FOX_W] += jnp.sum(daf * fhat, axis=0, keepdims=True)
        st_ref[2:3, FOX_W:2 * FOX_W] += jnp.sum(dam * mhat, axis=0, keepdims=True)
        dof_ref[...] = _rms_bwd(daf, fhat, rf, gfo_ref[...], FOX_W).astype(BF16)
        dom_ref[...] = _rms_bwd(dam, mhat, rm, gmo_ref[...], FOX_W).astype(BF16)

    row = lambda w: pl.BlockSpec((tm, w), lambda i: (i, 0))
    ff = jax.ShapeDtypeStruct((T, D_FF), BF16)
    out_shape = (
        jax.ShapeDtypeStruct((T, 1024), BF16), jax.ShapeDtypeStruct((T, 1024), BF16), ff, ff, ff,
        jax.ShapeDtypeStruct((T, 1024), BF16), jax.ShapeDtypeStruct((T, 1024), F32),
        jax.ShapeDtypeStruct((T, 512), BF16), jax.ShapeDtypeStruct((T, 512), BF16), jax.ShapeDtypeStruct((8, 1024), F32),
    )
    return pl.pallas_call(
        body, name="mid", grid=(T // tm,), out_shape=out_shape,
        in_specs=[row(512), row(512), row(1024), row(1024), _full(g_fo.shape), _full(g_mo.shape), _full(g2.shape), _full(g3.shape),
                  _full(w_o.shape), _full(w_g.shape), _full(w_u.shape), _full(w_d.shape)],
        out_specs=[row(1024), row(1024), row(D_FF), row(D_FF), row(D_FF), row(1024), row(1024), row(512), row(512),
                   pl.BlockSpec((8, 1024), lambda i: (0, 0))],
        compiler_params=_params(("arbitrary",)),
    )(of, om, x, tgt, g_fo, g_mo, g2, g3, w_o, w_g, w_u, w_d)


def _in_bwd(dqf, dkf, dvf, dqm, dkm, dvm, lat, x, dx2, g1, gq, gkv, bfg, ct, st, sel_t, w_in, w_q12, w_kv, seq, tm, ex=None):
    T = x.shape[0]
    nblk = T // tm
    nsb = seq // tm
    n_in, n_out = (len(ex.ins), len(ex.outs)) if ex else (0, 0)

    def body(*refs):
        (dqf_ref, dkf_ref, dvf_ref, dqm_ref, dkm_ref, dvm_ref, lat_ref, x_ref, dx2_ref, g1_ref, gq_ref, gkv_ref, b_ref,
         ct_ref, st_ref, selt_ref, win_ref, wq_ref, wkv_ref) = refs[0:19]
        dx_ref, dproj_ref, dq12_ref, dkv_ref, stat_ref = refs[19 + n_in:24 + n_in]
        carry = refs[24 + n_in + n_out]
        i = pl.program_id(0)
        if ex:
            ex_refs = (refs[19:19 + n_in], refs[24 + n_in:24 + n_in + n_out], refs[25 + n_in + n_out:])

            @pl.when(i == 0)
            def _():
                ex.start(*ex_refs)

        @pl.when(i == 0)
        def _():
            stat_ref[...] = jnp.zeros_like(stat_ref)

        @pl.when(i % nsb == 0)
        def _():
            carry[...] = jnp.zeros_like(carry)

        lane = lax.broadcasted_iota(jnp.int32, (tm, LANES), 1)
        low = lane < 64
        ctv, stv = ct_ref[...], st_ref[...]

        for j in range(PAIRS):
            e, o = 2 * LANES * j, 2 * LANES * j + LANES
            dq = jnp.where(low, dqf_ref[:, e:e + LANES], 0.0) + pltpu.roll(jnp.where(low, dqf_ref[:, o:o + LANES], 0.0), 64, 1)
            dk = jnp.where(low, dkf_ref[:, e:e + LANES], 0.0) + pltpu.roll(jnp.where(low, dkf_ref[:, o:o + LANES], 0.0), 64, 1)
            dproj_ref[:, C_FQ + LANES * j:C_FQ + LANES * (j + 1)] = (dq * FOX_SCALE).astype(BF16)
            dproj_ref[:, C_FK + LANES * j:C_FK + LANES * (j + 1)] = dk.astype(BF16)
        dproj_ref[:, C_FV:C_QL] = dvf_ref[...].astype(BF16)
        dcv = dkf_ref[...] + dqf_ref[...]
        k_hi = dcv.astype(BF16)
        k_lo = (dcv - k_hi.astype(F32)).astype(BF16)
        dc = _dot(k_hi, selt_ref[...]) + _dot(k_lo, selt_ref[...])
        rr = lax.broadcasted_iota(jnp.int32, (tm, tm), 0)
        cc = lax.broadcasted_iota(jnp.int32, (tm, tm), 1)
        triu = (cc >= rr).astype(BF16)
        a0, a1, a2 = _split3(dc)
        dlf = _dot(triu, a0) + _dot(triu, a1) + _dot(triu, a2) + carry[0:1, :]
        carry[0:1, :] = dlf[0:1, :]
        misc_a = lat_ref[:, Q_RANK + KV_RANK:Q_RANK + KV_RANK + LANES]
        z = misc_a + b_ref[...]
        dz = jnp.where(lane < HEADS, dlf * jax.nn.sigmoid(-z), 0.0)
        stat_ref[3:4, 0:LANES] += jnp.sum(dz, axis=0, keepdims=True)

        cq = (jnp.where(low, 1.0, 0.0) + ctv) * MLA_SCALE
        sq = stv * MLA_SCALE
        dkpe = jnp.zeros((tm, LANES), F32)
        for hd in range(HEADS):
            s0 = LANES * hd
            dqh = dqm_ref[:, s0:s0 + LANES]
            dq12_ref[:, s0:s0 + LANES] = (dqh * cq).astype(BF16)
            dq12_ref[:, 1024 + s0:1024 + s0 + LANES] = (dqh * sq).astype(BF16)
            dkpe = dkpe + dkm_ref[:, s0:s0 + LANES]
        dkv_ref[:, 0:1024] = dkm_ref[...].astype(BF16)
        dkv_ref[:, 1024:1536] = dvm_ref[...].astype(BF16)
        dproj_ref[:, C_MA:C_END] = (dz + dkpe * ctv + pltpu.roll(dkpe * stv, 32, 1)).astype(BF16)
        dqn = _dot(dq12_ref[...], wq_ref[...])
        dkvn = _dot_nt(dkv_ref[...], wkv_ref[...])
        ql = lat_ref[:, 0:Q_RANK]
        kvl = lat_ref[:, Q_RANK:Q_RANK + KV_RANK]
        rq, rkv = _rms(ql, Q_RANK), _rms(kvl, KV_RANK)
        qhat, kvhat = ql * rq, kvl * rkv
        stat_ref[1:2, 0:Q_RANK] += jnp.sum(dqn * qhat, axis=0, keepdims=True)
        stat_ref[2:3, 0:KV_RANK] += jnp.sum(dkvn * kvhat, axis=0, keepdims=True)
        dproj_ref[:, C_QL:C_KVL] = _rms_bwd(dqn, qhat, rq, gq_ref[...], Q_RANK).astype(BF16)
        dproj_ref[:, C_KVL:C_MA] = _rms_bwd(dkvn, kvhat, rkv, gkv_ref[...], KV_RANK).astype(BF16)

        dh1 = _dot(dproj_ref[...], win_ref[...])
        xv = x_ref[...]
        r1 = _rms(xv, D_MODEL)
        xh = xv * r1
        stat_ref[0:1, :] += jnp.sum(dh1 * xh, axis=0, keepdims=True)
        dx_ref[...] = dx2_ref[...] + _rms_bwd(dh1, xh, r1, g1_ref[...], D_MODEL)

        if ex:
            @pl.when(i == nblk - 1)
            def _():
                ex.wait(*ex_refs)

    rev = lambda w: pl.BlockSpec((tm, w), lambda i: (nblk - 1 - i, 0))
    out_shape = (
        jax.ShapeDtypeStruct((T, 1024), F32), jax.ShapeDtypeStruct((T, C_END), BF16), jax.ShapeDtypeStruct((T, 2048), BF16),
        jax.ShapeDtypeStruct((T, 1536), BF16), jax.ShapeDtypeStruct((8, 1024), F32),
    ) + tuple(ex.outs if ex else ())
    res = pl.pallas_call(
        body, name="in_bwd", grid=(nblk,), out_shape=out_shape,
        in_specs=[rev(1024), rev(1024), rev(512), rev(1024), rev(1024), rev(512), rev(512), rev(1024), rev(1024),
                  _full(g1.shape), _full(gq.shape), _full(gkv.shape), _full(bfg.shape), rev(LANES), rev(LANES), _full(sel_t.shape),
                  _full(w_in.shape), _full(w_q12.shape), _full(w_kv.shape)] + [ANY] * n_in,
        out_specs=[rev(1024), rev(C_END), rev(2048), rev(1536), pl.BlockSpec((8, 1024), lambda i: (0, 0))] + [ANY] * n_out,
        scratch_shapes=[pltpu.VMEM((8, LANES), F32)] + (ex.sems() if ex else []),
        compiler_params=_params(("arbitrary",)),
    )(dqf, dkf, dvf, dqm, dkm, dvm, lat, x, dx2, g1, gq, gkv, bfg, ct, st, sel_t, w_in, w_q12, w_kv, *(ex.ins if ex else ()))
    return res[0], res[1], res[2], res[3], res[4], list(res[5:])


def _wgrad(a, b, tk, tt, name, ex=None):
    T, K = a.shape
    N = b.shape[1]
    n_in, n_out = (len(ex.ins), len(ex.outs)) if ex else (0, 0)
    gk, gt = K // tk, T // tt

    def body(*refs):
        a_ref, b_ref, o_ref = refs[0], refs[1], refs[2 + n_in]
        kb, t = pl.program_id(0), pl.program_id(1)
        if ex:
            ex_refs = (refs[2:2 + n_in], refs[3 + n_in:3 + n_in + n_out], refs[3 + n_in + n_out:])

            @pl.when((kb == 0) & (t == 0))
            def _():
                ex.start(*ex_refs)

        @pl.when(t == 0)
        def _():
            o_ref[...] = jnp.zeros_like(o_ref)

        o_ref[...] += _dot_tn(a_ref[...].astype(BF16), b_ref[...].astype(BF16))

        if ex:
            @pl.when((kb == gk - 1) & (t == gt - 1))
            def _():
                ex.wait(*ex_refs)

    res = pl.pallas_call(
        body, name=name, grid=(gk, gt), out_shape=(jax.ShapeDtypeStruct((K, N), F32),) + tuple(ex.outs if ex else ()),
        in_specs=[pl.BlockSpec((tt, tk), lambda kb, t: (t, kb)), pl.BlockSpec((tt, N), lambda kb, t: (t, 0))] + [ANY] * n_in,
        out_specs=[pl.BlockSpec((tk, N), lambda kb, t: (kb, 0))] + [ANY] * n_out,
        scratch_shapes=ex.sems() if ex else [], input_output_aliases=ex.aliases(2, 1) if ex else {},
        compiler_params=_params(("arbitrary", "arbitrary")),
    )(a, b, *(ex.ins if ex else ()))
    return (res[0], list(res[1:])) if ex else res[0]


def _adamw(tensors, name, ex=None):
    n = len(tensors)
    R, C = tensors[0][0].shape
    tr = _row_tile(R)
    steps = R // tr
    n_in, n_out = (len(ex.ins), len(ex.outs)) if ex else (0, 0)

    def body(*refs):
        i = pl.program_id(0)
        if ex:
            ex_refs = (refs[4 * n:4 * n + n_in], refs[7 * n + n_in:7 * n + n_in + n_out], refs[7 * n + n_in + n_out:])

            @pl.when(i == 0)
            def _():
                ex.start(*ex_refs)

        for t in range(n):
            w_ref, g_ref, m_ref, v_ref = refs[4 * t:4 * t + 4]
            d_ref, nm_ref, nv_ref = refs[4 * n + n_in + 3 * t:4 * n + n_in + 3 * t + 3]
            gv = g_ref[...]
            nm = ADAM_B1 * m_ref[...] + (1.0 - ADAM_B1) * gv
            nv = ADAM_B2 * v_ref[...] + (1.0 - ADAM_B2) * (gv * gv)
            m_hat = nm / (1.0 - ADAM_B1 ** ADAM_STEP)
            v_hat = nv / (1.0 - ADAM_B2 ** ADAM_STEP)
            d_ref[...] = -ADAM_LR * (m_hat / (jnp.sqrt(v_hat) + ADAM_EPS) + ADAM_WD * w_ref[...])
            nm_ref[...] = nm
            nv_ref[...] = nv

        if ex:
            @pl.when(i == steps - 1)
            def _():
                ex.wait(*ex_refs)

    blk = pl.BlockSpec((tr, C), lambda i: (i, 0))
    sh = jax.ShapeDtypeStruct((R, C), F32)
    res = pl.pallas_call(
        body, name=name, grid=(steps,), out_shape=(sh,) * (3 * n) + tuple(ex.outs if ex else ()),
        in_specs=[blk] * (4 * n) + [ANY] * n_in, out_specs=[blk] * (3 * n) + [ANY] * n_out,
        scratch_shapes=ex.sems() if ex else [],
        compiler_params=_params(("arbitrary",)),
    )(*[a for t in tensors for a in t], *(ex.ins if ex else ()))
    out = [tuple(res[3 * t:3 * t + 3]) for t in range(n)]
    return (out, list(res[3 * n:])) if ex else out


def _arrange(win_t, wuq_t, wukv):
    dt = win_t.dtype
    z = lambda r: jnp.zeros((r, D_MODEL), dt)
    zh = lambda r: jnp.zeros((HEADS, r, Q_RANK), dt)
    kr1, kr2 = win_t[1928:1944], win_t[1944:1960]
    misc = jnp.concatenate([win_t[1536:1544], z(56), kr1, kr2, kr2, kr1], axis=0)
    w_in = jnp.concatenate([win_t[0:1536], win_t[1544:1928], misc], axis=0)
    wq = wuq_t.reshape(HEADS, 96, Q_RANK)
    q1 = jnp.concatenate([wq, zh(32)], axis=1).reshape(1024, Q_RANK)
    q2 = jnp.concatenate([zh(64), wq[:, 80:96], wq[:, 64:80], zh(32)], axis=1).reshape(1024, Q_RANK)
    wkv = wukv.reshape(KV_RANK, HEADS, 128)
    wk = jnp.concatenate([wkv[:, :, 0:64], jnp.zeros((KV_RANK, HEADS, 64), dt)], axis=2).reshape(KV_RANK, 1024)
    wv = wkv[:, :, 64:128].reshape(KV_RANK, 512)
    return dict(w_in=w_in, w_q12=jnp.concatenate([q1, q2], axis=0), w_k=wk, w_v=wv, w_kv=jnp.concatenate([wk, wv], axis=1))


def _unarrange(g_in, g_q12, g_kv):
    kr1 = g_in[C_MA + 64:C_MA + 80] + g_in[C_MA + 112:C_MA + 128]
    kr2 = g_in[C_MA + 80:C_MA + 96] + g_in[C_MA + 96:C_MA + 112]
    win_t = jnp.concatenate([g_in[0:1536], g_in[C_MA:C_MA + 8], g_in[1536:1920], kr1, kr2], axis=0)
    g1 = g_q12[0:1024].reshape(HEADS, 128, Q_RANK)
    g2 = g_q12[1024:2048].reshape(HEADS, 128, Q_RANK)
    wuq_t = jnp.concatenate([g1[:, 0:64], g1[:, 64:80] + g2[:, 80:96], g1[:, 80:96] + g2[:, 64:80]], axis=1).reshape(768, Q_RANK)
    gk = g_kv[:, 0:1024].reshape(KV_RANK, HEADS, 128)
    gv = g_kv[:, 1024:1536].reshape(KV_RANK, HEADS, 64)
    wukv = jnp.concatenate([gk[:, :, 0:64], gv], axis=2).reshape(KV_RANK, 1024)
    return win_t, wuq_t, wukv


def _selectors():
    sel = np.zeros((384, 1024), np.float32)
    sel_t = np.zeros((1024, LANES), np.float32)
    for h in range(HEADS):
        for piece in range(3):
            sel[LANES * piece + h, LANES * h + 64 + piece] = 1.0
        sel_t[LANES * h + 64, h] = 1.0
    return jnp.asarray(sel, BF16), jnp.asarray(sel_t, BF16)


def _rope_tables(positions):
    inv_freq = 10000.0 ** (-jnp.arange(0, ROPE, 2, dtype=F32) / ROPE)
    n = positions.size
    ang = (positions.reshape(n // 8, 8, 1).astype(F32) * inv_freq[None, None, :]).reshape(n // 8, 8 * (ROPE // 2))
    cos, sin = lax.optimization_barrier((jnp.cos(lax.optimization_barrier(ang)), jnp.sin(lax.optimization_barrier(ang))))
    cos, sin = cos.reshape(n, ROPE // 2), sin.reshape(n, ROPE // 2)
    z64, z32 = jnp.zeros((n, 64), F32), jnp.zeros((n, 32), F32)
    return jnp.concatenate([z64, cos, cos, z32], axis=1), jnp.concatenate([z64, -sin, sin, z32], axis=1)


def _work(name, t):
    return jnp.swapaxes(t[0], 0, 1) if name in TRANSPOSED else t[0]


def _back(name, t):
    return (jnp.swapaxes(t, 0, 1) if name in TRANSPOSED else t)[None]


SMALL_ROWS = {"norm_mix_g": (0, 1024), "norm_ffn_g": (1, 1024), "final_norm_g": (2, 1024), "q_norm_g": (4, 256),
              "kv_norm_g": (5, 128), "b_fgate": (6, 8)}


def kernel(x, positions, norm_mix_g, w_in, b_fgate, q_norm_g, w_uq, kv_norm_g, w_ukv, fox_out_g, mla_out_g, w_o, norm_ffn_g, w_gate, w_up, w_down, final_norm_g, loss_target, m_norm_mix_g, m_w_in, m_b_fgate, m_q_norm_g, m_w_uq, m_kv_norm_g, m_w_ukv, m_fox_out_g, m_mla_out_g, m_w_o, m_norm_ffn_g, m_w_gate, m_w_up, m_w_down, m_final_norm_g, v_norm_mix_g, v_w_in, v_b_fgate, v_q_norm_g, v_w_uq, v_kv_norm_g, v_w_ukv, v_fox_out_g, v_mla_out_g, v_w_o, v_norm_ffn_g, v_w_gate, v_w_up, v_w_down, v_final_norm_g):
    names = ["norm_mix_g", "w_in", "b_fgate", "q_norm_g", "w_uq", "kv_norm_g", "w_ukv", "fox_out_g", "mla_out_g", "w_o",
             "norm_ffn_g", "w_gate", "w_up", "w_down", "final_norm_g"]
    wts = dict(zip(names, [norm_mix_g, w_in, b_fgate, q_norm_g, w_uq, kv_norm_g, w_ukv, fox_out_g, mla_out_g, w_o, norm_ffn_g,
                           w_gate, w_up, w_down, final_norm_g]))
    mom = dict(zip(names, [m_norm_mix_g, m_w_in, m_b_fgate, m_q_norm_g, m_w_uq, m_kv_norm_g, m_w_ukv, m_fox_out_g, m_mla_out_g,
                           m_w_o, m_norm_ffn_g, m_w_gate, m_w_up, m_w_down, m_final_norm_g]))
    var = dict(zip(names, [v_norm_mix_g, v_w_in, v_b_fgate, v_q_norm_g, v_w_uq, v_kv_norm_g, v_w_ukv, v_fox_out_g, v_mla_out_g,
                           v_w_o, v_norm_ffn_g, v_w_gate, v_w_up, v_w_down, v_final_norm_g]))
    shard = {n: _work(n, wts[n]) for n in HEAD3 + FFN4}
    nb, seq, _ = x.shape
    T = nb * seq
    tm, tq = min(ROW_TILE, seq), min(ATTN_TILE, seq)
    tt = min(WGRAD_TILE, T)
    xf = x.reshape(T, D_MODEL)
    tgt = loss_target.reshape(T, D_MODEL)
    chip = 2 * lax.axis_index("x") + lax.axis_index("y")

    mine = [shard[n].astype(BF16) for n in HEAD3]
    head = _run_exchange(_gather_split_exchange(mine), "gather_head")
    win4, wuq4, wukv4 = [lax.dynamic_update_slice(h, s[None], (chip, 0, 0)) for h, s in zip(head, mine)]
    a = _arrange(win4.reshape(-1, D_MODEL), wuq4.reshape(-1, Q_RANK), wukv4.transpose(1, 0, 2).reshape(KV_RANK, -1))
    sel, sel_t = _selectors()
    ct, st = _rope_tables(positions)
    bfg = jnp.concatenate([b_fgate, jnp.zeros((1, LANES - HEADS), F32)], axis=1)
    g1, gq, gkv = norm_mix_g, q_norm_g, kv_norm_g

    h1, qf, kf, vf, qm, km, vm, lat, qn, kvn = _in_proj(xf, g1, a["w_in"], a["w_q12"], a["w_k"], a["w_v"], gq, gkv, bfg, ct, st, sel, seq, tm)
    tqf = min(ATTN_FWD_TILE, seq)
    of, lse_f, (wo4, wg4) = _attn_fwd(qf, kf, vf, nb, seq, tqf, "fox_fwd", _gather_exchange([shard[n].astype(BF16) for n in FFN4[:2]]))
    om, lse_m, (wu4, wd4) = _attn_fwd(qm, km, vm, nb, seq, tqf, "mla_fwd", _gather_exchange([shard[n].astype(BF16) for n in FFN4[2:]]))
    a_cat, h2, hid, dg, du, dx3, dx2, dof, dom, st_mid = _mid(
        of, om, xf, tgt, fox_out_g, mla_out_g, norm_ffn_g, final_norm_g.reshape(1, D_MODEL),
        wo4.reshape(D_MODEL, D_MODEL), wg4.reshape(D_FF, D_MODEL), wu4.reshape(D_FF, D_MODEL), wd4.reshape(D_FF, D_MODEL), tm)

    slab = lambda g: g.reshape(N_CHIPS, g.shape[0] // N_CHIPS, g.shape[1])
    big = [slab(_wgrad(a_cat, dx2, D_MODEL, tt, "wgrad_o")), slab(_wgrad(dg, h2, D_FF // 2, tt, "wgrad_gate")),
           slab(_wgrad(du, h2, D_FF // 2, tt, "wgrad_up")), slab(_wgrad(hid, dx3, D_FF // 2, tt, "wgrad_down"))]
    dqf, dkf, dvf, got = _attn_bwd(qf, kf, vf, of, dof, lse_f, nb, seq, tq, "fox_bwd", True, _swap_exchange(big))
    sums = [_add_half(g, s) for g, s in zip(big, got)]
    dqm, dkm, dvm, recv = _attn_bwd(qm, km, vm, om, dom, lse_m, nb, seq, tq, "mla_bwd", False, _scatter_exchange(sums))
    halves = [_sum_slabs(g, s, r) for g, s, r in zip(big, got, recv)]
    dx, dproj, dq12, dkv, st_in, _ = _in_bwd(dqf, dkf, dvf, dqm, dkm, dvm, lat, xf, dx2, g1, gq, gkv, bfg, ct, st, sel_t,
                                             a["w_in"], a["w_q12"], a["w_kv"], seq, tm)
    loss_row = jnp.concatenate([jnp.sum(st_mid[3:4, :], axis=1, keepdims=True), jnp.zeros((1, D_MODEL - 1), F32)], axis=1)
    stats = jnp.concatenate([st_in[0:1], st_mid[1:2], st_mid[0:1], st_mid[2:3], st_in[1:2], st_in[2:3], st_in[3:4], loss_row], axis=0)
    g_in, results = _wgrad(dproj, h1, C_END, tt, "wgrad_in", _both(_join_exchange(halves), _everyone_exchange(stats)))
    gshard = dict(zip(FFN4, results[:4]))
    stats = _sum_devices(results[4])

    gwin_t, gwuq_t, gwukv = _unarrange(g_in, _wgrad(dq12, qn, 2048, tt, "wgrad_uq"), _wgrad(kvn, dkv, KV_RANK, tt, "wgrad_ukv"))
    tail = [slab(gwin_t), slab(gwuq_t), gwukv.reshape(KV_RANK, N_CHIPS, -1).transpose(1, 0, 2)]
    tail_got = _run_exchange(_swap_exchange(tail), "tail_swap")
    tail_sums = [_add_half(g, s) for g, s in zip(tail, tail_got)]
    tail_recv = _run_exchange(_scatter_exchange(tail_sums), "tail_scatter")
    tail_joined = _run_exchange(_join_exchange([_sum_slabs(g, s, r) for g, s, r in zip(tail, tail_got, tail_recv)]), "tail_join")
    gshard.update(zip(HEAD3, tail_joined))
    quad = lambda n: (shard[n], gshard[n], _work(n, mom[n]), _work(n, var[n]))
    updates = dict(zip(FFN4[1:], _adamw([quad(n) for n in FFN4[1:]], "adamw_ffn")))
    for n in HEAD3 + FFN4[:1]:
        updates[n], = _adamw([quad(n)], "adamw_" + n)

    grads, delta, new_m, new_v = {}, {}, {}, {}
    for n in HEAD3 + FFN4:
        grads[n] = _back(n, gshard[n])
        delta[n], new_m[n], new_v[n] = [_back(n, t) for t in updates[n]]
    sm_g = {n: stats[row:row + 1, 0:width] for n, (row, width) in SMALL_ROWS.items()}
    sm_g["fox_out_g"] = stats[3:4, 0:512]
    sm_g["mla_out_g"] = stats[3:4, 512:1024]
    pad = lambda t: jnp.pad(t.reshape(1, -1), ((0, 0), (0, 1024 - t.size)))
    stack = lambda d: jnp.concatenate([pad(d[n]) for n in SMALL], axis=0)
    (sd, sm, sv), = _adamw([(stack(wts), stack(sm_g), stack(mom), stack(var))], "adamw_small")
    for i, n in enumerate(SMALL):
        shp = wts[n].shape
        grads[n] = sm_g[n].reshape(shp)
        delta[n] = sd[i, 0:wts[n].size].reshape(shp)
        new_m[n] = sm[i, 0:wts[n].size].reshape(shp)
        new_v[n] = sv[i, 0:wts[n].size].reshape(shp)
    loss = stats[7, 0]
    return (loss, dx.reshape(x.shape), *[grads[n] for n in names], *[delta[n] for n in names],
            *[new_m[n] for n in names], *[new_v[n] for n in names])
```

```python
import functools

import numpy as np
import jax
import jax.numpy as jnp
from jax import lax
from jax.experimental import pallas as pl
from jax.experimental.pallas import tpu as pltpu

F32 = jnp.float32
BF16 = jnp.bfloat16
MESH = pl.DeviceIdType.MESH

EPS = 1e-6
D_MODEL = 1024
HEADS = 8
PAIRS = HEADS // 2
FOX_W = 512
Q_RANK = 256
KV_RANK = 128
ROPE = 32
D_FF = 2816
N_CHIPS = 4
FOX_SCALE = 64 ** -0.5
MLA_SCALE = 96 ** -0.5
LANES = 128
NEG = -1e30

ADAM_LR, ADAM_B1, ADAM_B2, ADAM_EPS, ADAM_WD, ADAM_STEP = 0.001, 0.9, 0.999, 1e-08, 0.01, 10

C_FQ, C_FK, C_FV, C_QL, C_KVL, C_MA, C_END = 0, 512, 1024, 1536, 1792, 1920, 2048
C_MB = C_END

VMEM_LIMIT = 60 * 1024 * 1024
ROW_TILE = 256
IN_PROJ_TILE = 512
ATTN_TILE = 512
ATTN_FWD_TILE = 1024
WGRAD_TILE = 2048

HEAD3 = ("w_in", "w_uq", "w_ukv")
FFN4 = ("w_o", "w_gate", "w_up", "w_down")
TRANSPOSED = ("w_in", "w_uq", "w_gate", "w_up")
SMALL = ("norm_mix_g", "b_fgate", "q_norm_g", "kv_norm_g", "fox_out_g", "mla_out_g", "norm_ffn_g", "final_norm_g")


def _params(sem=None):
    return pltpu.CompilerParams(dimension_semantics=sem, vmem_limit_bytes=VMEM_LIMIT)


def _full(shape):
    n = len(shape)
    return pl.BlockSpec(shape, lambda *_: (0,) * n, pipeline_mode=pl.Buffered(1))


def _dot(a, b):
    return jnp.dot(a, b, preferred_element_type=F32)


def _dot_nt(a, b):
    return lax.dot_general(a, b, (((1,), (1,)), ((), ())), preferred_element_type=F32)


def _dot_tn(a, b):
    return lax.dot_general(a, b, (((0,), (0,)), ((), ())), preferred_element_type=F32)


def _split3(v):
    hi = v.astype(BF16)
    r1 = v - hi.astype(F32)
    mid = r1.astype(BF16)
    lo = (r1 - mid.astype(F32)).astype(BF16)
    return hi, mid, lo


def _rms(v, width):
    return lax.rsqrt(jnp.sum(v * v, axis=1, keepdims=True) * (1.0 / width) + EPS)


def _rms_bwd(dy, xhat, r, g, width):
    u = dy * g
    return r * (u - xhat * (jnp.sum(u * xhat, axis=1, keepdims=True) * (1.0 / width)))


ANY = pl.BlockSpec(memory_space=pl.ANY)


def _place():
    return lax.axis_index("x"), lax.axis_index("y"), lax.axis_index("c")


def _other_chips(x, y):
    return [(1 - x, y), (x, 1 - y), (1 - x, 1 - y)]


def _remote(src, dst, send, recv, j, dev):
    return pltpu.make_async_remote_copy(src_ref=src, dst_ref=dst, send_sem=send.at[j], recv_sem=recv.at[j], device_id=dev, device_id_type=MESH)


class _Exchange:
    def __init__(self, ins, outs, n_remote, n_local, build, in_place=False):
        self.ins, self.outs, self.n_remote, self.n_local, self.build = list(ins), list(outs), n_remote, max(n_local, 1), build
        self.in_place = in_place
        self.n_aliased = len(self.ins)

    def aliases(self, first_in, first_out):
        return {first_in + i: first_out + i for i in range(self.n_aliased)} if self.in_place else {}

    def sems(self):
        return [pltpu.SemaphoreType.DMA((self.n_remote,)), pltpu.SemaphoreType.DMA((self.n_remote,)), pltpu.SemaphoreType.DMA((self.n_local,))]

    def start(self, in_refs, out_refs, sems):
        for cp in self.build(in_refs, out_refs, *sems)[0]:
            cp.start()

    def wait(self, in_refs, out_refs, sems):
        for w in self.build(in_refs, out_refs, *sems)[1]:
            w()


def _gather_exchange(shards):
    def build(ins, outs, send, recv, lsem):
        x, y, c = _place()
        starts, waits = [], []
        for i, (s, o) in enumerate(zip(ins, outs)):
            mine = pltpu.make_async_copy(s, o.at[2 * x + y], lsem.at[i])
            starts.append(mine)
            waits.append(mine.wait)
            for j, (cx, cy) in enumerate(_other_chips(x, y)):
                out = _remote(s, o.at[2 * x + y], send, recv, 3 * i + j, (cx, cy, c))
                starts.append(out)
                waits.append(_remote(s, o.at[2 * cx + cy], send, recv, 3 * i + j, (cx, cy, c)).wait_recv)
                waits.append(out.wait_send)
        return starts, waits

    outs = [jax.ShapeDtypeStruct((N_CHIPS,) + s.shape, s.dtype) for s in shards]
    return _Exchange(shards, outs, 3 * len(shards), len(shards), build)


def _gather_split_exchange(shards):
    n = len(shards)

    def build(ins, outs, send, recv, lsem):
        x, y, c = _place()
        starts, waits, last = [], [], []
        for i, (s, o) in enumerate(zip(ins, outs)):
            hc = s.shape[1] // 2
            mine, other = pl.ds(c * hc, hc), pl.ds((1 - c) * hc, hc)
            for j, (cx, cy) in enumerate(_other_chips(x, y)):
                out = _remote(s.at[:, mine], o.at[2 * x + y, :, mine], send, recv, 3 * i + j, (cx, cy, c))
                landed = o.at[2 * cx + cy, :, mine]
                arrive = _remote(s.at[:, mine], landed, send, recv, 3 * i + j, (cx, cy, c))
                onward = _remote(landed, landed, send, recv, 3 * n + 3 * i + j, (x, y, 1 - c))
                from_sibling = _remote(landed, o.at[2 * cx + cy, :, other], send, recv, 3 * n + 3 * i + j, (x, y, 1 - c))
                starts.append(out)
                waits.append(lambda arrive=arrive, onward=onward: (arrive.wait_recv(), onward.start()))
                last += [from_sibling.wait_recv, onward.wait_send, out.wait_send]
        return starts, waits + last

    outs = [jax.ShapeDtypeStruct((N_CHIPS,) + s.shape, s.dtype) for s in shards]
    return _Exchange(shards, outs, 6 * n, 0, build)


def _swap_exchange(grads):
    def build(ins, outs, send, recv, lsem):
        x, y, c = _place()
        cps = []
        for i, (g, o) in enumerate(zip(ins, outs)):
            hc = g.shape[2] // 2
            cps.append(_remote(g.at[:, :, pl.ds((1 - c) * hc, hc)], o, send, recv, i, (x, y, 1 - c)))
        return cps, [cp.wait for cp in cps]

    outs = [jax.ShapeDtypeStruct((g.shape[0], g.shape[1], g.shape[2] // 2), g.dtype) for g in grads]
    return _Exchange(grads, outs, len(grads), 0, build)


def _scatter_exchange(sums):
    def build(ins, outs, send, recv, lsem):
        x, y, c = _place()
        cps = []
        for i, (s, o) in enumerate(zip(ins, outs)):
            for j, (cx, cy) in enumerate(_other_chips(x, y)):
                cps.append(_remote(s.at[2 * cx + cy], o.at[j], send, recv, 3 * i + j, (cx, cy, c)))
        return cps, [cp.wait for cp in cps]

    outs = [jax.ShapeDtypeStruct((3,) + s.shape[1:], s.dtype) for s in sums]
    return _Exchange(sums, outs, 3 * len(sums), 0, build)


def _join_exchange(bufs):
    def build(ins, outs, send, recv, lsem):
        x, y, c = _place()
        starts, waits = [], []
        for i, (t, o) in enumerate(zip(ins, outs)):
            hc = t.shape[1] // 2
            out = _remote(t.at[:, pl.ds(c * hc, hc)], o.at[:, pl.ds(c * hc, hc)], send, recv, i, (x, y, 1 - c))
            starts.append(out)
            waits += [_remote(t.at[:, pl.ds(c * hc, hc)], o.at[:, pl.ds((1 - c) * hc, hc)], send, recv, i, (x, y, 1 - c)).wait_recv,
                      out.wait_send]
        return starts, waits

    outs = [jax.ShapeDtypeStruct(t.shape, t.dtype) for t in bufs]
    return _Exchange(bufs, outs, len(bufs), 0, build, in_place=True)


def _everyone_exchange(v):
    def build(ins, outs, send, recv, lsem):
        x, y, c = _place()
        me = 4 * x + 2 * y + c
        mine = pltpu.make_async_copy(ins[0], outs[0].at[me], lsem.at[0])
        starts, waits = [mine], [mine.wait]
        for j in range(7):
            fx, fy, fc = (j + 1) >> 2 & 1, (j + 1) >> 1 & 1, (j + 1) & 1
            peer = (x ^ fx, y ^ fy, c ^ fc)
            out = _remote(ins[0], outs[0].at[me], send, recv, j, peer)
            starts.append(out)
            waits += [_remote(ins[0], outs[0].at[4 * peer[0] + 2 * peer[1] + peer[2]], send, recv, j, peer).wait_recv, out.wait_send]
        return starts, waits

    return _Exchange([v], [jax.ShapeDtypeStruct((8,) + v.shape, v.dtype)], 7, 1, build)


def _both(a, b):
    na_in, na_out = len(a.ins), len(a.outs)

    def build(ins, outs, send, recv, lsem):
        sa, wa = a.build(ins[:na_in], outs[:na_out], send.at[pl.ds(0, a.n_remote)], recv.at[pl.ds(0, a.n_remote)],
                         lsem.at[pl.ds(0, a.n_local)])
        sb, wb = b.build(ins[na_in:], outs[na_out:], send.at[pl.ds(a.n_remote, b.n_remote)], recv.at[pl.ds(a.n_remote, b.n_remote)],
                         lsem.at[pl.ds(a.n_local, b.n_local)])
        return sa + sb, wa + wb

    both = _Exchange(a.ins + b.ins, a.outs + b.outs, a.n_remote + b.n_remote, a.n_local + b.n_local, build, in_place=a.in_place)
    both.n_aliased = na_in
    return both


def _run_exchange(ex, name):
    n_in, n_out = len(ex.ins), len(ex.outs)

    def body(*refs):
        ins, outs, sems = refs[:n_in], refs[n_in:n_in + n_out], refs[n_in + n_out:]
        ex.start(ins, outs, sems)
        ex.wait(ins, outs, sems)

    return pl.pallas_call(
        body, name=name, out_shape=tuple(ex.outs), in_specs=[ANY] * n_in, out_specs=tuple([ANY] * n_out),
        scratch_shapes=ex.sems(), input_output_aliases=ex.aliases(0, 0),
        compiler_params=pltpu.CompilerParams(has_side_effects=True),
    )(*ex.ins)


def _sum_devices(rows):
    def body(r_ref, o_ref):
        acc = r_ref[0]
        for d in range(1, 8):
            acc = acc + r_ref[d]
        o_ref[...] = acc

    vm = pl.BlockSpec(memory_space=pltpu.VMEM)
    return pl.pallas_call(body, name="sum_devices", out_shape=jax.ShapeDtypeStruct(rows.shape[1:], rows.dtype),
                          in_specs=[vm], out_specs=vm)(rows)


def _add_half(g, got):
    n, R, C = g.shape
    hc = C // 2

    def body(c_ref, g_ref, r_ref, o_ref):
        o_ref[...] = (g_ref[...] + r_ref[...]).astype(BF16)

    c = lax.axis_index("c")
    return pl.pallas_call(
        body, name="add_half",
        grid_spec=pltpu.PrefetchScalarGridSpec(
            num_scalar_prefetch=1, grid=(n,),
            in_specs=[pl.BlockSpec((1, R, hc), lambda k, c_ref: (k, 0, c_ref[0])),
                      pl.BlockSpec((1, R, hc), lambda k, c_ref: (k, 0, 0))],
            out_specs=pl.BlockSpec((1, R, hc), lambda k, c_ref: (k, 0, 0))),
        out_shape=jax.ShapeDtypeStruct((n, R, hc), BF16),
        compiler_params=_params(("arbitrary",)),
    )(jnp.reshape(c, (1,)).astype(jnp.int32), g, got)


def _sum_slabs(g, got, recv):
    _, R, C = g.shape
    hc = C // 2

    def body(kc_ref, g_ref, s_ref, r_ref, o_ref):
        o_ref[...] = (((g_ref[0] + s_ref[0]) + r_ref[0].astype(F32)) + r_ref[1].astype(F32)) + r_ref[2].astype(F32)

    kc = jnp.stack([2 * lax.axis_index("x") + lax.axis_index("y"), lax.axis_index("c")]).astype(jnp.int32)
    return pl.pallas_call(
        body, name="sum_slabs",
        grid_spec=pltpu.PrefetchScalarGridSpec(
            num_scalar_prefetch=1, grid=(1,),
            in_specs=[pl.BlockSpec((1, R, hc), lambda i, kc_ref: (kc_ref[0], 0, kc_ref[1])),
                      pl.BlockSpec((1, R, hc), lambda i, kc_ref: (kc_ref[0], 0, 0)),
                      pl.BlockSpec((3, R, hc), lambda i, kc_ref: (0, 0, 0))],
            out_specs=pl.BlockSpec((R, hc), lambda i, kc_ref: (0, kc_ref[1]))),
        out_shape=jax.ShapeDtypeStruct((R, C), F32),
        compiler_params=_params(("arbitrary",)),
    )(kc, g, got, recv)


def _row_tile(rows):
    for cand in (256, 184, 176, 144, 128, 64, 32, 16, 8):
        if rows % cand == 0:
            return cand
    return rows


def _in_proj(x, g1, w_in, w_q12, w_k, w_v, gq, gkv, bfg, ct, st, sel, seq, tm):
    T = x.shape[0]
    nsb = seq // tm

    def body(x_ref, g1_ref, win_ref, wq_ref, wk_ref, wv_ref, gq_ref, gkv_ref, b_ref, ct_ref, st_ref, sel_ref,
             h1_ref, qf_ref, kf_ref, vf_ref, qm_ref, km_ref, vm_ref, lat_ref, qn_ref, kvn_ref, carry):
        i = pl.program_id(0)

        @pl.when(i % nsb == 0)
        def _():
            carry[...] = jnp.zeros_like(carry)

        xv = x_ref[...]
        h = (xv * _rms(xv, D_MODEL) * g1_ref[...]).astype(BF16)
        h1_ref[...] = h
        proj = _dot_nt(h, win_ref[...])
        lane = lax.broadcasted_iota(jnp.int32, (tm, LANES), 1)
        low = lane < 64
        misc_a = proj[:, C_MA:C_END]
        misc_b = pltpu.roll(misc_a, 96, 1)

        z = misc_a + b_ref[...]
        lf = jnp.where(lane < HEADS, jnp.minimum(z, 0.0) - jnp.log1p(jnp.exp(-jnp.abs(z))), 0.0)
        rr = lax.broadcasted_iota(jnp.int32, (tm, tm), 0)
        cc = lax.broadcasted_iota(jnp.int32, (tm, tm), 1)
        tri = (rr >= cc).astype(BF16)
        a0, a1, a2 = _split3(lf)
        c = _dot(tri, a0) + _dot(tri, a1) + _dot(tri, a2) + carry[0:1, :]
        carry[0:1, :] = c[tm - 1:tm, :]
        c0, c1, c2 = _split3(c)
        cpl = _dot(jnp.concatenate([c0, c1, c2], axis=1), sel_ref[...])
        qpad = jnp.where((lane >= 64) & (lane < 67), -1.0, 0.0)
        for j in range(PAIRS):
            qc = proj[:, C_FQ + LANES * j:C_FQ + LANES * (j + 1)] * FOX_SCALE
            kc = proj[:, C_FK + LANES * j:C_FK + LANES * (j + 1)]
            e, o = 2 * LANES * j, 2 * LANES * j + LANES
            qf_ref[:, e:e + LANES] = jnp.where(low, qc, qpad).astype(BF16)
            qf_ref[:, o:o + LANES] = jnp.where(low, pltpu.roll(qc, 64, 1), qpad).astype(BF16)
            kf_ref[:, e:e + LANES] = jnp.where(low, kc, cpl[:, e:e + LANES]).astype(BF16)
            kf_ref[:, o:o + LANES] = jnp.where(low, pltpu.roll(kc, 64, 1), cpl[:, o:o + LANES]).astype(BF16)
        vf_ref[...] = proj[:, C_FV:C_QL].astype(BF16)

        ql = proj[:, C_QL:C_KVL]
        kvl = proj[:, C_KVL:C_MA]
        qn = (ql * _rms(ql, Q_RANK) * gq_ref[...]).astype(BF16)
        kvn = (kvl * _rms(kvl, KV_RANK) * gkv_ref[...]).astype(BF16)
        lat_ref[...] = proj[:, C_QL:C_MB]
        qn_ref[...] = qn
        kvn_ref[...] = kvn
        q12 = _dot_nt(qn, wq_ref[...])
        kn = _dot(kvn, wk_ref[...])
        ctv = ct_ref[...]
        stv = st_ref[...]
        cq = (jnp.where(low, 1.0, 0.0) + ctv) * MLA_SCALE
        sq = stv * MLA_SCALE
        kpe = misc_a * ctv + misc_b * stv
        for hd in range(HEADS):
            s0 = LANES * hd
            qm_ref[:, s0:s0 + LANES] = (q12[:, s0:s0 + LANES] * cq + q12[:, 1024 + s0:1024 + s0 + LANES] * sq).astype(BF16)
            km_ref[:, s0:s0 + LANES] = (kn[:, s0:s0 + LANES] + kpe).astype(BF16)
        vm_ref[...] = _dot(kvn, wv_ref[...]).astype(BF16)

    row = lambda w: pl.BlockSpec((tm, w), lambda i: (i, 0))
    out_shape = (
        jax.ShapeDtypeStruct((T, D_MODEL), BF16),
        jax.ShapeDtypeStruct((T, 1024), BF16), jax.ShapeDtypeStruct((T, 1024), BF16), jax.ShapeDtypeStruct((T, 512), BF16),
        jax.ShapeDtypeStruct((T, 1024), BF16), jax.ShapeDtypeStruct((T, 1024), BF16), jax.ShapeDtypeStruct((T, 512), BF16),
        jax.ShapeDtypeStruct((T, 512), F32),
        jax.ShapeDtypeStruct((T, Q_RANK), BF16), jax.ShapeDtypeStruct((T, KV_RANK), BF16),
    )
    return pl.pallas_call(
        body, name="in_proj", grid=(T // tm,), out_shape=out_shape,
        in_specs=[row(D_MODEL), _full(g1.shape), _full(w_in.shape), _full(w_q12.shape), _full(w_k.shape), _full(w_v.shape),
                  _full(gq.shape), _full(gkv.shape), _full(bfg.shape), row(LANES), row(LANES), _full(sel.shape)],
        out_specs=[row(D_MODEL), row(1024), row(1024), row(512), row(1024), row(1024), row(512), row(512), row(Q_RANK), row(KV_RANK)],
        scratch_shapes=[pltpu.VMEM((8, LANES), F32)],
        compiler_params=_params(("arbitrary",)),
    )(x, g1, w_in, w_q12, w_k, w_v, gq, gkv, bfg, ct, st, sel)


def _attn_fwd(q, k, v, nb, seq, tq, name, ex=None):
    T = q.shape[0]
    nq = seq // tq
    n_in, n_out = (len(ex.ins), len(ex.outs)) if ex else (0, 0)

    def body(*refs):
        q_ref, k_ref, v_ref = refs[0:3]
        o_ref, lse_ref = refs[3 + n_in:5 + n_in]
        b, pr, qi = pl.program_id(0), pl.program_id(1), pl.program_id(2)
        if ex:
            ex_refs = (refs[3:3 + n_in], refs[5 + n_in:5 + n_in + n_out], refs[8 + n_in + n_out:])

            @pl.when((b == 0) & (pr == 0) & (qi == 0))
            def _():
                ex.start(*ex_refs)

        s_sc, p_sc, acc_sc = refs[5 + n_in + n_out:8 + n_in + n_out]
        strip = 64
        key_s = lax.broadcasted_iota(jnp.int32, (strip, tq), 0)
        qry_s = lax.broadcasted_iota(jnp.int32, (strip, tq), 1)
        row_t = lax.broadcasted_iota(jnp.int32, (LANES, tq), 0)
        acc_sc[...] = jnp.zeros(acc_sc.shape, F32)

        def fold(x, op):
            out = x[0:8]
            for r in range(8, strip, 8):
                out = op(out, x[r:r + 8])
            return out

        def step(kj, state, masked):
            rows = pl.ds(pl.multiple_of(kj * tq, tq), tq)
            for hh in range(2):
                s_sc[hh] = _dot_nt(k_ref[rows, LANES * hh:LANES * (hh + 1)], q_ref[:, LANES * hh:LANES * (hh + 1)])
            vv = v_ref[rows, :]
            new = []
            for hh in range(2):
                m, l = state[hh]

                def strip_of(r0, hh=hh):
                    s = s_sc[hh, r0:r0 + strip, :]
                    return jnp.where(key_s + r0 <= qry_s, s, NEG) if masked else s

                mx = fold(strip_of(0), jnp.maximum)
                for r0 in range(strip, tq, strip):
                    mx = jnp.maximum(mx, fold(strip_of(r0), jnp.maximum))
                m_new = jnp.maximum(m, jnp.max(mx, axis=0, keepdims=True))
                alpha = jnp.exp(m - m_new)
                sm = jnp.zeros((8, tq), F32)
                for r0 in range(0, tq, strip):
                    p = jnp.exp(strip_of(r0) - m_new)
                    sm = sm + fold(p, jnp.add)
                    p_sc[hh, r0:r0 + strip, :] = p.astype(BF16)
                l = alpha * l + jnp.sum(sm, axis=0, keepdims=True)
                acc_sc[hh] = alpha * acc_sc[hh] + _dot_tn(vv, p_sc[hh])
                new.append((m_new, l))
            return tuple(new)

        one = (jnp.full((1, tq), NEG, F32), jnp.zeros((1, tq), F32))
        state = lax.fori_loop(0, qi, functools.partial(step, masked=False), (one, one))
        (m0, l0), (m1, l1) = step(qi, state, True)
        o_ref[...] = jnp.where(row_t < 64, acc_sc[0] / l0, acc_sc[1] / l1).T
        lse_ref[:, 0:LANES] = jnp.broadcast_to(m0 + jnp.log(l0), (LANES, tq)).T
        lse_ref[:, LANES:2 * LANES] = jnp.broadcast_to(m1 + jnp.log(l1), (LANES, tq)).T

        if ex:
            @pl.when((b == nb - 1) & (pr == PAIRS - 1) & (qi == nq - 1))
            def _():
                ex.wait(*ex_refs)

    res = pl.pallas_call(
        body, name=name, grid=(nb, PAIRS, nq),
        out_shape=(jax.ShapeDtypeStruct((T, 512), F32), jax.ShapeDtypeStruct((T, 1024), F32)) + tuple(ex.outs if ex else ()),
        in_specs=[pl.BlockSpec((tq, 2 * LANES), lambda b, p, i: (b * nq + i, p)),
                  pl.BlockSpec((seq, 2 * LANES), lambda b, p, i: (b, p)),
                  pl.BlockSpec((seq, LANES), lambda b, p, i: (b, p))] + [ANY] * n_in,
        out_specs=[pl.BlockSpec((tq, LANES), lambda b, p, i: (b * nq + i, p)),
                   pl.BlockSpec((tq, 2 * LANES), lambda b, p, i: (b * nq + i, p))] + [ANY] * n_out,
        scratch_shapes=[pltpu.VMEM((2, tq, tq), F32), pltpu.VMEM((2, tq, tq), BF16), pltpu.VMEM((2, LANES, tq), F32)]
        + (ex.sems() if ex else []),
        compiler_params=_params(("arbitrary", "arbitrary", "arbitrary")),
    )(q, k, v, *(ex.ins if ex else ()))
    return res[0], res[1], list(res[2:])


def _attn_bwd(q, k, v, o, do, lse, nb, seq, tq, name, key_bias, ex=None):
    T = q.shape[0]
    nq = seq // tq
    n_in, n_out = (len(ex.ins), len(ex.outs)) if ex else (0, 0)

    def body(*refs):
        q_ref, k_ref, v_ref, o_ref, do_ref, lse_ref = refs[0:6]
        dq_ref, dk_ref, dv_ref = refs[6 + n_in:9 + n_in]
        dsc, rsum = refs[9 + n_in + n_out:11 + n_in + n_out]
        b, pr, step_no = pl.program_id(0), pl.program_id(1), pl.program_id(2)
        kj = nq - 1 - step_no
        if ex:
            ex_refs = (refs[6:6 + n_in], refs[9 + n_in:9 + n_in + n_out], refs[11 + n_in + n_out:])

            @pl.when((b == 0) & (pr == 0) & (step_no == 0))
            def _():
                ex.start(*ex_refs)

        lane_s = lax.broadcasted_iota(jnp.int32, (seq, LANES), 1)
        lane = lax.broadcasted_iota(jnp.int32, (tq, LANES), 1)
        rr = lax.broadcasted_iota(jnp.int32, (tq, tq), 0)
        cc = lax.broadcasted_iota(jnp.int32, (tq, tq), 1)

        @pl.when(step_no == 0)
        def _():
            dq_ref[...] = jnp.zeros_like(dq_ref)
            prod = do_ref[...].astype(F32) * o_ref[...]
            d0 = jnp.sum(jnp.where(lane_s < 64, prod, 0.0), axis=1, keepdims=True)
            d1 = jnp.sum(jnp.where(lane_s < 64, 0.0, prod), axis=1, keepdims=True)
            dsc[0] = jnp.broadcast_to(d0, (seq, LANES))
            dsc[1] = jnp.broadcast_to(d1, (seq, LANES))
            if key_bias:
                rsum[...] = jnp.zeros_like(rsum)

        vv = v_ref[...]

        def step(qi, carry, masked):
            dkt, dvt, cols = carry
            rows = pl.ds(pl.multiple_of(qi * tq, tq), tq)
            dov = do_ref[rows, :]
            new_dkt, new_cols = [], []
            for hh in range(2):
                qv = q_ref[rows, LANES * hh:LANES * (hh + 1)]
                kv = k_ref[:, LANES * hh:LANES * (hh + 1)]
                dom = jnp.where((lane < 64) if hh == 0 else (lane >= 64), dov, jnp.zeros((), BF16))
                s = _dot_nt(qv, kv)
                if masked:
                    s = jnp.where(cc <= rr, s, NEG)
                p = jnp.exp(s - jnp.tile(lse_ref[rows, LANES * hh:LANES * (hh + 1)], (1, tq // LANES)))
                dp = _dot_nt(dom, vv)
                ds32 = p * (dp - jnp.tile(dsc[hh, rows, :], (1, tq // LANES)))
                col = cols[hh]
                if key_bias:
                    col = col + jnp.sum(ds32, axis=0, keepdims=True)
                    rsum[hh, rows, :] += jnp.broadcast_to(jnp.sum(ds32, axis=1, keepdims=True), (tq, LANES))
                ds = ds32.astype(BF16)
                dvt = dvt + _dot_tn(dom, p.astype(BF16))
                new_dkt.append(dkt[hh] + _dot_tn(qv, ds))
                new_cols.append(col)
                dq_ref[rows, LANES * hh:LANES * (hh + 1)] += _dot(ds, kv)
            return tuple(new_dkt), dvt, tuple(new_cols)

        zt = jnp.zeros((LANES, tq), F32)
        zc = jnp.zeros((1, tq), F32)
        carry = step(kj, ((zt, zt), zt, (zc, zc)), True)
        dkt, dvt, cols = lax.fori_loop(kj + 1, nq, functools.partial(step, masked=False), carry)
        row_t = lax.broadcasted_iota(jnp.int32, (LANES, tq), 0)
        for hh in range(2):
            dk_h = jnp.where(row_t == 64, -cols[hh], dkt[hh]) if key_bias else dkt[hh]
            dk_ref[:, LANES * hh:LANES * (hh + 1)] = dk_h.T
        dv_ref[...] = dvt.T

        if key_bias:
            @pl.when(step_no == nq - 1)
            def _():
                for hh in range(2):
                    blk = dq_ref[:, LANES * hh:LANES * (hh + 1)]
                    dq_ref[:, LANES * hh:LANES * (hh + 1)] = jnp.where(lane_s == 64, rsum[hh], blk)

        if ex:
            @pl.when((b == nb - 1) & (pr == PAIRS - 1) & (step_no == nq - 1))
            def _():
                ex.wait(*ex_refs)

    per_seq = lambda w: pl.BlockSpec((seq, w), lambda b, p, j: (b, p))
    per_blk = lambda w: pl.BlockSpec((tq, w), lambda b, p, j: (b * nq + nq - 1 - j, p))
    res = pl.pallas_call(
        body, name=name, grid=(nb, PAIRS, nq),
        out_shape=(jax.ShapeDtypeStruct((T, 1024), F32), jax.ShapeDtypeStruct((T, 1024), F32), jax.ShapeDtypeStruct((T, 512), F32))
        + tuple(ex.outs if ex else ()),
        in_specs=[per_seq(2 * LANES), per_blk(2 * LANES), per_blk(LANES), per_seq(LANES), per_seq(LANES), per_seq(2 * LANES)] + [ANY] * n_in,
        out_specs=[per_seq(2 * LANES), per_blk(2 * LANES), per_blk(LANES)] + [ANY] * n_out,
        scratch_shapes=[pltpu.VMEM((2, seq, LANES), F32), pltpu.VMEM((2, seq, LANES) if key_bias else (2, 8, LANES), F32)]
        + (ex.sems() if ex else []),
        compiler_params=_params(("arbitrary", "arbitrary", "arbitrary")),
    )(q, k, v, o, do, lse, *(ex.ins if ex else ()))
    return res[0], res[1], res[2], list(res[3:])


def _mid(of, om, x, tgt, g_fo, g_mo, g2, g3, w_o, w_g, w_u, w_d, tm):
    T = x.shape[0]

    def body(of_ref, om_ref, x_ref, t_ref, gfo_ref, gmo_ref, g2_ref, g3_ref, wo_ref, wg_ref, wu_ref, wd_ref,
             a_ref, h2_ref, hid_ref, dg_ref, du_ref, dx3_ref, dx2_ref, dof_ref, dom_ref, st_ref):
        i = pl.program_id(0)

        @pl.when(i == 0)
        def _():
            st_ref[...] = jnp.zeros_like(st_ref)

        ofv, omv = of_ref[...], om_ref[...]
        rf, rm = _rms(ofv, FOX_W), _rms(omv, FOX_W)
        fhat, mhat = ofv * rf, omv * rm
        a = jnp.concatenate([fhat * gfo_ref[...], mhat * gmo_ref[...]], axis=1).astype(BF16)
        a_ref[...] = a
        x2 = x_ref[...] + _dot(a, wo_ref[...])
        r2 = _rms(x2, D_MODEL)
        xh2 = x2 * r2
        h2 = (xh2 * g2_ref[...]).astype(BF16)
        h2_ref[...] = h2
        gt = _dot_nt(h2, wg_ref[...])
        up = _dot_nt(h2, wu_ref[...])
        sg = jax.nn.sigmoid(gt)
        sl = gt * sg
        hid = (sl * up).astype(BF16)
        hid_ref[...] = hid
        x3 = x2 + _dot(hid, wd_ref[...])
        r3 = _rms(x3, D_MODEL)
        xh3 = x3 * r3
        diff = xh3 * g3_ref[...] - t_ref[...]
        dy = diff * (1.0 / D_MODEL)
        st_ref[3:4, :] += jnp.sum(diff * diff, axis=0, keepdims=True) * (0.5 / D_MODEL)
        st_ref[0:1, :] += jnp.sum(dy * xh3, axis=0, keepdims=True)
        dx3 = _rms_bwd(dy, xh3, r3, g3_ref[...], D_MODEL)
        dx3b = dx3.astype(BF16)
        dx3_ref[...] = dx3b
        dhid = _dot_nt(dx3b, wd_ref[...])
        dg = (dhid * up * (sg * (1.0 + gt * (1.0 - sg)))).astype(BF16)
        du = (dhid * sl).astype(BF16)
        dg_ref[...] = dg
        du_ref[...] = du
        dh2 = _dot(dg, wg_ref[...]) + _dot(du, wu_ref[...])
        st_ref[1:2, :] += jnp.sum(dh2 * xh2, axis=0, keepdims=True)
        dx2 = dx3 + _rms_bwd(dh2, xh2, r2, g2_ref[...], D_MODEL)
        dx2_ref[...] = dx2
        da = _dot_nt(dx2.astype(BF16), wo_ref[...])
        daf, dam = da[:, 0:FOX_W], da[:, FOX_W:2 * FOX_W]
        st_ref[2:3, 0:FOX_W] += jnp.sum(daf * fhat, axis=0, keepdims=True)
        st_ref[2:3, FOX_W:2 * FOX_W] += jnp.sum(dam * mhat, axis=0, keepdims=True)
        dof_ref[...] = _rms_bwd(daf, fhat, rf, gfo_ref[...], FOX_W).astype(BF16)
        dom_ref[...] = _rms_bwd(dam, mhat, rm, gmo_ref[...], FOX_W).astype(BF16)

    row = lambda w: pl.BlockSpec((tm, w), lambda i: (i, 0))
    ff = jax.ShapeDtypeStruct((T, D_FF), BF16)
    out_shape = (
        jax.ShapeDtypeStruct((T, 1024), BF16), jax.ShapeDtypeStruct((T, 1024), BF16), ff, ff, ff,
        jax.ShapeDtypeStruct((T, 1024), BF16), jax.ShapeDtypeStruct((T, 1024), F32),
        jax.ShapeDtypeStruct((T, 512), BF16), jax.ShapeDtypeStruct((T, 512), BF16), jax.ShapeDtypeStruct((8, 1024), F32),
    )
    return pl.pallas_call(
        body, name="mid", grid=(T // tm,), out_shape=out_shape,
        in_specs=[row(512), row(512), row(1024), row(1024), _full(g_fo.shape), _full(g_mo.shape), _full(g2.shape), _full(g3.shape),
                  _full(w_o.shape), _full(w_g.shape), _full(w_u.shape), _full(w_d.shape)],
        out_specs=[row(1024), row(1024), row(D_FF), row(D_FF), row(D_FF), row(1024), row(1024), row(512), row(512),
                   pl.BlockSpec((8, 1024), lambda i: (0, 0))],
        compiler_params=_params(("arbitrary",)),
    )(of, om, x, tgt, g_fo, g_mo, g2, g3, w_o, w_g, w_u, w_d)


def _in_bwd(dqf, dkf, dvf, dqm, dkm, dvm, lat, x, dx2, g1, gq, gkv, bfg, ct, st, sel_t, w_in, w_q12, w_kv, seq, tm):
    T = x.shape[0]
    nblk = T // tm
    nsb = seq // tm

    def body(dqf_ref, dkf_ref, dvf_ref, dqm_ref, dkm_ref, dvm_ref, lat_ref, x_ref, dx2_ref, g1_ref, gq_ref, gkv_ref, b_ref,
             ct_ref, st_ref, selt_ref, win_ref, wq_ref, wkv_ref, dx_ref, dproj_ref, dq12_ref, dkv_ref, stat_ref, carry):
        i = pl.program_id(0)

        @pl.when(i == 0)
        def _():
            stat_ref[...] = jnp.zeros_like(stat_ref)

        @pl.when(i % nsb == 0)
        def _():
            carry[...] = jnp.zeros_like(carry)

        lane = lax.broadcasted_iota(jnp.int32, (tm, LANES), 1)
        low = lane < 64
        ctv, stv = ct_ref[...], st_ref[...]

        for j in range(PAIRS):
            e, o = 2 * LANES * j, 2 * LANES * j + LANES
            dq = jnp.where(low, dqf_ref[:, e:e + LANES], 0.0) + pltpu.roll(jnp.where(low, dqf_ref[:, o:o + LANES], 0.0), 64, 1)
            dk = jnp.where(low, dkf_ref[:, e:e + LANES], 0.0) + pltpu.roll(jnp.where(low, dkf_ref[:, o:o + LANES], 0.0), 64, 1)
            dproj_ref[:, C_FQ + LANES * j:C_FQ + LANES * (j + 1)] = (dq * FOX_SCALE).astype(BF16)
            dproj_ref[:, C_FK + LANES * j:C_FK + LANES * (j + 1)] = dk.astype(BF16)
        dproj_ref[:, C_FV:C_QL] = dvf_ref[...].astype(BF16)
        dcv = dkf_ref[...] + dqf_ref[...]
        k_hi = dcv.astype(BF16)
        k_lo = (dcv - k_hi.astype(F32)).astype(BF16)
        dc = _dot(k_hi, selt_ref[...]) + _dot(k_lo, selt_ref[...])
        rr = lax.broadcasted_iota(jnp.int32, (tm, tm), 0)
        cc = lax.broadcasted_iota(jnp.int32, (tm, tm), 1)
        triu = (cc >= rr).astype(BF16)
        a0, a1, a2 = _split3(dc)
        dlf = _dot(triu, a0) + _dot(triu, a1) + _dot(triu, a2) + carry[0:1, :]
        carry[0:1, :] = dlf[0:1, :]
        misc_a = lat_ref[:, Q_RANK + KV_RANK:Q_RANK + KV_RANK + LANES]
        z = misc_a + b_ref[...]
        dz = jnp.where(lane < HEADS, dlf * jax.nn.sigmoid(-z), 0.0)
        stat_ref[3:4, 0:LANES] += jnp.sum(dz, axis=0, keepdims=True)

        cq = (jnp.where(low, 1.0, 0.0) + ctv) * MLA_SCALE
        sq = stv * MLA_SCALE
        dkpe = jnp.zeros((tm, LANES), F32)
        for hd in range(HEADS):
            s0 = LANES * hd
            dqh = dqm_ref[:, s0:s0 + LANES]
            dq12_ref[:, s0:s0 + LANES] = (dqh * cq).astype(BF16)
            dq12_ref[:, 1024 + s0:1024 + s0 + LANES] = (dqh * sq).astype(BF16)
            dkpe = dkpe + dkm_ref[:, s0:s0 + LANES]
        dkv_ref[:, 0:1024] = dkm_ref[...].astype(BF16)
        dkv_ref[:, 1024:1536] = dvm_ref[...].astype(BF16)
        dproj_ref[:, C_MA:C_END] = (dz + dkpe * ctv + pltpu.roll(dkpe * stv, 32, 1)).astype(BF16)
        dqn = _dot(dq12_ref[...], wq_ref[...])
        dkvn = _dot_nt(dkv_ref[...], wkv_ref[...])
        ql = lat_ref[:, 0:Q_RANK]
        kvl = lat_ref[:, Q_RANK:Q_RANK + KV_RANK]
        rq, rkv = _rms(ql, Q_RANK), _rms(kvl, KV_RANK)
        qhat, kvhat = ql * rq, kvl * rkv
        stat_ref[1:2, 0:Q_RANK] += jnp.sum(dqn * qhat, axis=0, keepdims=True)
        stat_ref[2:3, 0:KV_RANK] += jnp.sum(dkvn * kvhat, axis=0, keepdims=True)
        dproj_ref[:, C_QL:C_KVL] = _rms_bwd(dqn, qhat, rq, gq_ref[...], Q_RANK).astype(BF16)
        dproj_ref[:, C_KVL:C_MA] = _rms_bwd(dkvn, kvhat, rkv, gkv_ref[...], KV_RANK).astype(BF16)

        dh1 = _dot(dproj_ref[...], win_ref[...])
        xv = x_ref[...]
        r1 = _rms(xv, D_MODEL)
        xh = xv * r1
        stat_ref[0:1, :] += jnp.sum(dh1 * xh, axis=0, keepdims=True)
        dx_ref[...] = dx2_ref[...] + _rms_bwd(dh1, xh, r1, g1_ref[...], D_MODEL)

    rev = lambda w: pl.BlockSpec((tm, w), lambda i: (nblk - 1 - i, 0))
    out_shape = (
        jax.ShapeDtypeStruct((T, 1024), F32), jax.ShapeDtypeStruct((T, C_END), BF16), jax.ShapeDtypeStruct((T, 2048), BF16),
        jax.ShapeDtypeStruct((T, 1536), BF16), jax.ShapeDtypeStruct((8, 1024), F32),
    )
    return pl.pallas_call(
        body, name="in_bwd", grid=(nblk,), out_shape=out_shape,
        in_specs=[rev(1024), rev(1024), rev(512), rev(1024), rev(1024), rev(512), rev(512), rev(1024), rev(1024),
                  _full(g1.shape), _full(gq.shape), _full(gkv.shape), _full(bfg.shape), rev(LANES), rev(LANES), _full(sel_t.shape),
                  _full(w_in.shape), _full(w_q12.shape), _full(w_kv.shape)],
        out_specs=[rev(1024), rev(C_END), rev(2048), rev(1536), pl.BlockSpec((8, 1024), lambda i: (0, 0))],
        scratch_shapes=[pltpu.VMEM((8, LANES), F32)],
        compiler_params=_params(("arbitrary",)),
    )(dqf, dkf, dvf, dqm, dkm, dvm, lat, x, dx2, g1, gq, gkv, bfg, ct, st, sel_t, w_in, w_q12, w_kv)


def _wgrad(a, b, tk, tt, name, ex=None):
    T, K = a.shape
    N = b.shape[1]
    n_in, n_out = (len(ex.ins), len(ex.outs)) if ex else (0, 0)
    gk, gt = K // tk, T // tt

    def body(*refs):
        a_ref, b_ref, o_ref = refs[0], refs[1], refs[2 + n_in]
        kb, t = pl.program_id(0), pl.program_id(1)
        if ex:
            ex_refs = (refs[2:2 + n_in], refs[3 + n_in:3 + n_in + n_out], refs[3 + n_in + n_out:])

            @pl.when((kb == 0) & (t == 0))
            def _():
                ex.start(*ex_refs)

        @pl.when(t == 0)
        def _():
            o_ref[...] = jnp.zeros_like(o_ref)

        o_ref[...] += _dot_tn(a_ref[...].astype(BF16), b_ref[...].astype(BF16))

        if ex:
            @pl.when((kb == gk - 1) & (t == gt - 1))
            def _():
                ex.wait(*ex_refs)

    res = pl.pallas_call(
        body, name=name, grid=(gk, gt), out_shape=(jax.ShapeDtypeStruct((K, N), F32),) + tuple(ex.outs if ex else ()),
        in_specs=[pl.BlockSpec((tt, tk), lambda kb, t: (t, kb)), pl.BlockSpec((tt, N), lambda kb, t: (t, 0))] + [ANY] * n_in,
        out_specs=[pl.BlockSpec((tk, N), lambda kb, t: (kb, 0))] + [ANY] * n_out,
        scratch_shapes=ex.sems() if ex else [], input_output_aliases=ex.aliases(2, 1) if ex else {},
        compiler_params=_params(("arbitrary", "arbitrary")),
    )(a, b, *(ex.ins if ex else ()))
    return (res[0], list(res[1:])) if ex else res[0]


def _adamw(tensors, name):
    n = len(tensors)
    R, C = tensors[0][0].shape
    tr = _row_tile(R)

    def body(*refs):
        for t in range(n):
            w_ref, g_ref, m_ref, v_ref = refs[4 * t:4 * t + 4]
            d_ref, nm_ref, nv_ref = refs[4 * n + 3 * t:4 * n + 3 * t + 3]
            gv = g_ref[...]
            nm = ADAM_B1 * m_ref[...] + (1.0 - ADAM_B1) * gv
            nv = ADAM_B2 * v_ref[...] + (1.0 - ADAM_B2) * (gv * gv)
            m_hat = nm / (1.0 - ADAM_B1 ** ADAM_STEP)
            v_hat = nv / (1.0 - ADAM_B2 ** ADAM_STEP)
            d_ref[...] = -ADAM_LR * (m_hat / (jnp.sqrt(v_hat) + ADAM_EPS) + ADAM_WD * w_ref[...])
            nm_ref[...] = nm
            nv_ref[...] = nv

    blk = pl.BlockSpec((tr, C), lambda i: (i, 0))
    sh = jax.ShapeDtypeStruct((R, C), F32)
    res = pl.pallas_call(
        body, name=name, grid=(R // tr,), out_shape=(sh,) * (3 * n),
        in_specs=[blk] * (4 * n), out_specs=[blk] * (3 * n),
        compiler_params=_params(("arbitrary",)),
    )(*[a for t in tensors for a in t])
    return [tuple(res[3 * t:3 * t + 3]) for t in range(n)]


def _arrange(win_t, wuq_t, wukv):
    dt = win_t.dtype
    z = lambda r: jnp.zeros((r, D_MODEL), dt)
    zh = lambda r: jnp.zeros((HEADS, r, Q_RANK), dt)
    kr1, kr2 = win_t[1928:1944], win_t[1944:1960]
    misc = jnp.concatenate([win_t[1536:1544], z(56), kr1, kr2, kr2, kr1], axis=0)
    w_in = jnp.concatenate([win_t[0:1536], win_t[1544:1928], misc], axis=0)
    wq = wuq_t.reshape(HEADS, 96, Q_RANK)
    q1 = jnp.concatenate([wq, zh(32)], axis=1).reshape(1024, Q_RANK)
    q2 = jnp.concatenate([zh(64), wq[:, 80:96], wq[:, 64:80], zh(32)], axis=1).reshape(1024, Q_RANK)
    wkv = wukv.reshape(KV_RANK, HEADS, 128)
    wk = jnp.concatenate([wkv[:, :, 0:64], jnp.zeros((KV_RANK, HEADS, 64), dt)], axis=2).reshape(KV_RANK, 1024)
    wv = wkv[:, :, 64:128].reshape(KV_RANK, 512)
    return dict(w_in=w_in, w_q12=jnp.concatenate([q1, q2], axis=0), w_k=wk, w_v=wv, w_kv=jnp.concatenate([wk, wv], axis=1))


def _unarrange(g_in, g_q12, g_kv):
    kr1 = g_in[C_MA + 64:C_MA + 80] + g_in[C_MA + 112:C_MA + 128]
    kr2 = g_in[C_MA + 80:C_MA + 96] + g_in[C_MA + 96:C_MA + 112]
    win_t = jnp.concatenate([g_in[0:1536], g_in[C_MA:C_MA + 8], g_in[1536:1920], kr1, kr2], axis=0)
    g1 = g_q12[0:1024].reshape(HEADS, 128, Q_RANK)
    g2 = g_q12[1024:2048].reshape(HEADS, 128, Q_RANK)
    wuq_t = jnp.concatenate([g1[:, 0:64], g1[:, 64:80] + g2[:, 80:96], g1[:, 80:96] + g2[:, 64:80]], axis=1).reshape(768, Q_RANK)
    gk = g_kv[:, 0:1024].reshape(KV_RANK, HEADS, 128)
    gv = g_kv[:, 1024:1536].reshape(KV_RANK, HEADS, 64)
    wukv = jnp.concatenate([gk[:, :, 0:64], gv], axis=2).reshape(KV_RANK, 1024)
    return win_t, wuq_t, wukv


def _selectors():
    sel = np.zeros((384, 1024), np.float32)
    sel_t = np.zeros((1024, LANES), np.float32)
    for h in range(HEADS):
        for piece in range(3):
            sel[LANES * piece + h, LANES * h + 64 + piece] = 1.0
        sel_t[LANES * h + 64, h] = 1.0
    return jnp.asarray(sel, BF16), jnp.asarray(sel_t, BF16)


def _rope_tables(positions):
    inv_freq = 10000.0 ** (-jnp.arange(0, ROPE, 2, dtype=F32) / ROPE)
    n = positions.size
    ang = (positions.reshape(n // 8, 8, 1).astype(F32) * inv_freq[None, None, :]).reshape(n // 8, 8 * (ROPE // 2))
    cos, sin = lax.optimization_barrier((jnp.cos(lax.optimization_barrier(ang)), jnp.sin(lax.optimization_barrier(ang))))
    cos, sin = cos.reshape(n, ROPE // 2), sin.reshape(n, ROPE // 2)
    z64, z32 = jnp.zeros((n, 64), F32), jnp.zeros((n, 32), F32)
    return jnp.concatenate([z64, cos, cos, z32], axis=1), jnp.concatenate([z64, -sin, sin, z32], axis=1)


def _work(name, t):
    return jnp.swapaxes(t[0], 0, 1) if name in TRANSPOSED else t[0]


def _back(name, t):
    return (jnp.swapaxes(t, 0, 1) if name in TRANSPOSED else t)[None]


SMALL_ROWS = {"norm_mix_g": (0, 1024), "norm_ffn_g": (1, 1024), "final_norm_g": (2, 1024), "q_norm_g": (4, 256),
              "kv_norm_g": (5, 128), "b_fgate": (6, 8)}


def kernel(x, positions, norm_mix_g, w_in, b_fgate, q_norm_g, w_uq, kv_norm_g, w_ukv, fox_out_g, mla_out_g, w_o, norm_ffn_g, w_gate, w_up, w_down, final_norm_g, loss_target, m_norm_mix_g, m_w_in, m_b_fgate, m_q_norm_g, m_w_uq, m_kv_norm_g, m_w_ukv, m_fox_out_g, m_mla_out_g, m_w_o, m_norm_ffn_g, m_w_gate, m_w_up, m_w_down, m_final_norm_g, v_norm_mix_g, v_w_in, v_b_fgate, v_q_norm_g, v_w_uq, v_kv_norm_g, v_w_ukv, v_fox_out_g, v_mla_out_g, v_w_o, v_norm_ffn_g, v_w_gate, v_w_up, v_w_down, v_final_norm_g):
    names = ["norm_mix_g", "w_in", "b_fgate", "q_norm_g", "w_uq", "kv_norm_g", "w_ukv", "fox_out_g", "mla_out_g", "w_o",
             "norm_ffn_g", "w_gate", "w_up", "w_down", "final_norm_g"]
    wts = dict(zip(names, [norm_mix_g, w_in, b_fgate, q_norm_g, w_uq, kv_norm_g, w_ukv, fox_out_g, mla_out_g, w_o, norm_ffn_g,
                           w_gate, w_up, w_down, final_norm_g]))
    mom = dict(zip(names, [m_norm_mix_g, m_w_in, m_b_fgate, m_q_norm_g, m_w_uq, m_kv_norm_g, m_w_ukv, m_fox_out_g, m_mla_out_g,
                           m_w_o, m_norm_ffn_g, m_w_gate, m_w_up, m_w_down, m_final_norm_g]))
    var = dict(zip(names, [v_norm_mix_g, v_w_in, v_b_fgate, v_q_norm_g, v_w_uq, v_kv_norm_g, v_w_ukv, v_fox_out_g, v_mla_out_g,
                           v_w_o, v_norm_ffn_g, v_w_gate, v_w_up, v_w_down, v_final_norm_g]))
    shard = {n: _work(n, wts[n]) for n in HEAD3 + FFN4}
    nb, seq, _ = x.shape
    T = nb * seq
    tm, tq = min(ROW_TILE, seq), min(ATTN_TILE, seq)
    tt = min(WGRAD_TILE, T)
    xf = x.reshape(T, D_MODEL)
    tgt = loss_target.reshape(T, D_MODEL)
    chip = 2 * lax.axis_index("x") + lax.axis_index("y")

    mine = [shard[n].astype(BF16) for n in HEAD3]
    head = _run_exchange(_gather_split_exchange(mine), "gather_head")
    win4, wuq4, wukv4 = [lax.dynamic_update_slice(h, s[None], (chip, 0, 0)) for h, s in zip(head, mine)]
    a = _arrange(win4.reshape(-1, D_MODEL), wuq4.reshape(-1, Q_RANK), wukv4.transpose(1, 0, 2).reshape(KV_RANK, -1))
    sel, sel_t = _selectors()
    ct, st = _rope_tables(positions)
    bfg = jnp.concatenate([b_fgate, jnp.zeros((1, LANES - HEADS), F32)], axis=1)
    g1, gq, gkv = norm_mix_g, q_norm_g, kv_norm_g

    h1, qf, kf, vf, qm, km, vm, lat, qn, kvn = _in_proj(xf, g1, a["w_in"], a["w_q12"], a["w_k"], a["w_v"], gq, gkv, bfg, ct, st, sel, seq,
                                                        min(IN_PROJ_TILE, seq))
    tqf = min(ATTN_FWD_TILE, seq)
    of, lse_f, (wo4, wg4) = _attn_fwd(qf, kf, vf, nb, seq, tqf, "fox_fwd", _gather_exchange([shard[n].astype(BF16) for n in FFN4[:2]]))
    om, lse_m, (wu4, wd4) = _attn_fwd(qm, km, vm, nb, seq, tqf, "mla_fwd", _gather_exchange([shard[n].astype(BF16) for n in FFN4[2:]]))
    a_cat, h2, hid, dg, du, dx3, dx2, dof, dom, st_mid = _mid(
        of, om, xf, tgt, fox_out_g, mla_out_g, norm_ffn_g, final_norm_g.reshape(1, D_MODEL),
        wo4.reshape(D_MODEL, D_MODEL), wg4.reshape(D_FF, D_MODEL), wu4.reshape(D_FF, D_MODEL), wd4.reshape(D_FF, D_MODEL), tm)

    slab = lambda g: g.reshape(N_CHIPS, g.shape[0] // N_CHIPS, g.shape[1])
    big = [slab(_wgrad(a_cat, dx2, D_MODEL, tt, "wgrad_o")), slab(_wgrad(dg, h2, D_FF // 2, tt, "wgrad_gate")),
           slab(_wgrad(du, h2, D_FF // 2, tt, "wgrad_up")), slab(_wgrad(hid, dx3, D_FF // 2, tt, "wgrad_down"))]
    dqf, dkf, dvf, got = _attn_bwd(qf, kf, vf, of, dof, lse_f, nb, seq, tq, "fox_bwd", True, _swap_exchange(big))
    sums = [_add_half(g, s) for g, s in zip(big, got)]
    dqm, dkm, dvm, recv = _attn_bwd(qm, km, vm, om, dom, lse_m, nb, seq, tq, "mla_bwd", False, _scatter_exchange(sums))
    halves = [_sum_slabs(g, s, r) for g, s, r in zip(big, got, recv)]
    dx, dproj, dq12, dkv, st_in = _in_bwd(dqf, dkf, dvf, dqm, dkm, dvm, lat, xf, dx2, g1, gq, gkv, bfg, ct, st, sel_t,
                                             a["w_in"], a["w_q12"], a["w_kv"], seq, tm)
    loss_row = jnp.concatenate([jnp.sum(st_mid[3:4, :], axis=1, keepdims=True), jnp.zeros((1, D_MODEL - 1), F32)], axis=1)
    stats = jnp.concatenate([st_in[0:1], st_mid[1:2], st_mid[0:1], st_mid[2:3], st_in[1:2], st_in[2:3], st_in[3:4], loss_row], axis=0)
    g_in, results = _wgrad(dproj, h1, C_END, tt, "wgrad_in", _both(_join_exchange(halves), _everyone_exchange(stats)))
    gshard = dict(zip(FFN4, results[:4]))
    stats = _sum_devices(results[4])

    gwin_t, gwuq_t, gwukv = _unarrange(g_in, _wgrad(dq12, qn, 2048, tt, "wgrad_uq"), _wgrad(kvn, dkv, KV_RANK, tt, "wgrad_ukv"))
    tail = [slab(gwin_t), slab(gwuq_t), gwukv.reshape(KV_RANK, N_CHIPS, -1).transpose(1, 0, 2)]
    tail_got = _run_exchange(_swap_exchange(tail), "tail_swap")
    tail_sums = [_add_half(g, s) for g, s in zip(tail, tail_got)]
    tail_recv = _run_exchange(_scatter_exchange(tail_sums), "tail_scatter")
    tail_joined = _run_exchange(_join_exchange([_sum_slabs(g, s, r) for g, s, r in zip(tail, tail_got, tail_recv)]), "tail_join")
    gshard.update(zip(HEAD3, tail_joined))
    quad = lambda n: (shard[n], gshard[n], _work(n, mom[n]), _work(n, var[n]))
    updates = dict(zip(FFN4[1:], _adamw([quad(n) for n in FFN4[1:]], "adamw_ffn")))
    for n in HEAD3 + FFN4[:1]:
        updates[n], = _adamw([quad(n)], "adamw_" + n)

    grads, delta, new_m, new_v = {}, {}, {}, {}
    for n in HEAD3 + FFN4:
        grads[n] = _back(n, gshard[n])
        delta[n], new_m[n], new_v[n] = [_back(n, t) for t in updates[n]]
    sm_g = {n: stats[row:row + 1, 0:width] for n, (row, width) in SMALL_ROWS.items()}
    sm_g["fox_out_g"] = stats[3:4, 0:512]
    sm_g["mla_out_g"] = stats[3:4, 512:1024]
    pad = lambda t: jnp.pad(t.reshape(1, -1), ((0, 0), (0, 1024 - t.size)))
    stack = lambda d: jnp.concatenate([pad(d[n]) for n in SMALL], axis=0)
    (sd, sm, sv), = _adamw([(stack(wts), stack(sm_g), stack(mom), stack(var))], "adamw_small")
    for i, n in enumerate(SMALL):
        shp = wts[n].shape
        grads[n] = sm_g[n].reshape(shp)
        delta[n] = sd[i, 0:wts[n].size].reshape(shp)
        new_m[n] = sm[i, 0:wts[n].size].reshape(shp)
        new_v[n] = sv[i, 0:wts[n].size].reshape(shp)
    loss = stats[7, 0]
    return (loss, dx.reshape(x.shape), *[grads[n] for n in names], *[delta[n] for n in names],
            *[new_m[n] for n in names], *[new_v[n] for n in names])
```

```python
import functools

import numpy as np
import jax
import jax.numpy as jnp
from jax import lax
from jax.experimental import pallas as pl
from jax.experimental.pallas import tpu as pltpu

F32 = jnp.float32
BF16 = jnp.bfloat16
MESH = pl.DeviceIdType.MESH

EPS = 1e-6
D_MODEL = 1024
HEADS = 8
PAIRS = HEADS // 2
FOX_W = 512
Q_RANK = 256
KV_RANK = 128
ROPE = 32
D_FF = 2816
N_CHIPS = 4
FOX_SCALE = 64 ** -0.5
MLA_SCALE = 96 ** -0.5
LANES = 128
NEG = -1e30

ADAM_LR, ADAM_B1, ADAM_B2, ADAM_EPS, ADAM_WD, ADAM_STEP = 0.001, 0.9, 0.999, 1e-08, 0.01, 10

C_FQ, C_FK, C_FV, C_QL, C_KVL, C_MA, C_END = 0, 512, 1024, 1536, 1792, 1920, 2048
C_MB = C_END

VMEM_LIMIT = 60 * 1024 * 1024
ROW_TILE = 256
IN_PROJ_TILE = 512
ATTN_TILE = 512
ATTN_FWD_TILE = 1024
WGRAD_TILE = 2048

HEAD3 = ("w_in", "w_uq", "w_ukv")
FFN4 = ("w_o", "w_gate", "w_up", "w_down")
TRANSPOSED = ("w_in", "w_uq", "w_gate", "w_up")
SMALL = ("norm_mix_g", "b_fgate", "q_norm_g", "kv_norm_g", "fox_out_g", "mla_out_g", "norm_ffn_g", "final_norm_g")
ROW_NORM_MIX, ROW_NORM_FFN, ROW_FINAL, ROW_OUT, ROW_Q, ROW_KV, ROW_B, ROW_LOSS = range(8)
SMALL_AT = {"norm_mix_g": (ROW_NORM_MIX, 0, 1024), "norm_ffn_g": (ROW_NORM_FFN, 0, 1024), "final_norm_g": (ROW_FINAL, 0, 1024),
            "fox_out_g": (ROW_OUT, 0, 512), "mla_out_g": (ROW_OUT, 512, 512), "q_norm_g": (ROW_Q, 0, 256),
            "kv_norm_g": (ROW_KV, 0, 128), "b_fgate": (ROW_B, 0, 8)}


def _params(sem=None):
    return pltpu.CompilerParams(dimension_semantics=sem, vmem_limit_bytes=VMEM_LIMIT)


def _full(shape):
    n = len(shape)
    return pl.BlockSpec(shape, lambda *_: (0,) * n, pipeline_mode=pl.Buffered(1))


def _dot(a, b):
    return jnp.dot(a, b, preferred_element_type=F32)


def _dot_nt(a, b):
    return lax.dot_general(a, b, (((1,), (1,)), ((), ())), preferred_element_type=F32)


def _dot_tn(a, b):
    return lax.dot_general(a, b, (((0,), (0,)), ((), ())), preferred_element_type=F32)


def _split3(v):
    hi = v.astype(BF16)
    r1 = v - hi.astype(F32)
    mid = r1.astype(BF16)
    lo = (r1 - mid.astype(F32)).astype(BF16)
    return hi, mid, lo


def _rms(v, width):
    return lax.rsqrt(jnp.sum(v * v, axis=1, keepdims=True) * (1.0 / width) + EPS)


def _rms_bwd(dy, xhat, r, g, width):
    u = dy * g
    return r * (u - xhat * (jnp.sum(u * xhat, axis=1, keepdims=True) * (1.0 / width)))


ANY = pl.BlockSpec(memory_space=pl.ANY)


def _place():
    return lax.axis_index("x"), lax.axis_index("y"), lax.axis_index("c")


def _other_chips(x, y):
    return [(1 - x, y), (x, 1 - y), (1 - x, 1 - y)]


def _remote(src, dst, send, recv, j, dev):
    return pltpu.make_async_remote_copy(src_ref=src, dst_ref=dst, send_sem=send.at[j], recv_sem=recv.at[j], device_id=dev, device_id_type=MESH)


class _Exchange:
    def __init__(self, ins, outs, n_remote, n_local, build, in_place=False):
        self.ins, self.outs, self.n_remote, self.n_local, self.build = list(ins), list(outs), n_remote, max(n_local, 1), build
        self.in_place = in_place
        self.n_aliased = len(self.ins)

    def aliases(self, first_in, first_out):
        return {first_in + i: first_out + i for i in range(self.n_aliased)} if self.in_place else {}

    def sems(self):
        return [pltpu.SemaphoreType.DMA((self.n_remote,)), pltpu.SemaphoreType.DMA((self.n_remote,)), pltpu.SemaphoreType.DMA((self.n_local,))]

    def start(self, in_refs, out_refs, sems):
        for cp in self.build(in_refs, out_refs, *sems)[0]:
            cp.start()

    def wait(self, in_refs, out_refs, sems):
        for w in self.build(in_refs, out_refs, *sems)[1]:
            w()


def _gather_exchange(shards):
    def build(ins, outs, send, recv, lsem):
        x, y, c = _place()
        starts, waits = [], []
        for i, (s, o) in enumerate(zip(ins, outs)):
            mine = pltpu.make_async_copy(s, o.at[2 * x + y], lsem.at[i])
            starts.append(mine)
            waits.append(mine.wait)
            for j, (cx, cy) in enumerate(_other_chips(x, y)):
                out = _remote(s, o.at[2 * x + y], send, recv, 3 * i + j, (cx, cy, c))
                starts.append(out)
                waits.append(_remote(s, o.at[2 * cx + cy], send, recv, 3 * i + j, (cx, cy, c)).wait_recv)
                waits.append(out.wait_send)
        return starts, waits

    outs = [jax.ShapeDtypeStruct((N_CHIPS,) + s.shape, s.dtype) for s in shards]
    return _Exchange(shards, outs, 3 * len(shards), len(shards), build)


def _gather_split_exchange(shards):
    n = len(shards)

    def build(ins, outs, send, recv, lsem):
        x, y, c = _place()
        starts, waits, last = [], [], []
        for i, (s, o) in enumerate(zip(ins, outs)):
            hc = s.shape[1] // 2
            mine, other = pl.ds(c * hc, hc), pl.ds((1 - c) * hc, hc)
            for j, (cx, cy) in enumerate(_other_chips(x, y)):
                out = _remote(s.at[:, mine], o.at[2 * x + y, :, mine], send, recv, 3 * i + j, (cx, cy, c))
                landed = o.at[2 * cx + cy, :, mine]
                arrive = _remote(s.at[:, mine], landed, send, recv, 3 * i + j, (cx, cy, c))
                onward = _remote(landed, landed, send, recv, 3 * n + 3 * i + j, (x, y, 1 - c))
                from_sibling = _remote(landed, o.at[2 * cx + cy, :, other], send, recv, 3 * n + 3 * i + j, (x, y, 1 - c))
                starts.append(out)
                waits.append(lambda arrive=arrive, onward=onward: (arrive.wait_recv(), onward.start()))
                last += [from_sibling.wait_recv, onward.wait_send, out.wait_send]
        return starts, waits + last

    outs = [jax.ShapeDtypeStruct((N_CHIPS,) + s.shape, s.dtype) for s in shards]
    return _Exchange(shards, outs, 6 * n, 0, build)


def _swap_exchange(grads):
    def build(ins, outs, send, recv, lsem):
        x, y, c = _place()
        cps = []
        for i, (g, o) in enumerate(zip(ins, outs)):
            hc = g.shape[2] // 2
            cps.append(_remote(g.at[:, :, pl.ds((1 - c) * hc, hc)], o, send, recv, i, (x, y, 1 - c)))
        return cps, [cp.wait for cp in cps]

    outs = [jax.ShapeDtypeStruct((g.shape[0], g.shape[1], g.shape[2] // 2), g.dtype) for g in grads]
    return _Exchange(grads, outs, len(grads), 0, build)


def _scatter_exchange(sums):
    def build(ins, outs, send, recv, lsem):
        x, y, c = _place()
        cps = []
        for i, (s, o) in enumerate(zip(ins, outs)):
            for j, (cx, cy) in enumerate(_other_chips(x, y)):
                cps.append(_remote(s.at[2 * cx + cy], o.at[j], send, recv, 3 * i + j, (cx, cy, c)))
        return cps, [cp.wait for cp in cps]

    outs = [jax.ShapeDtypeStruct((3,) + s.shape[1:], s.dtype) for s in sums]
    return _Exchange(sums, outs, 3 * len(sums), 0, build)


def _join_exchange(bufs):
    def build(ins, outs, send, recv, lsem):
        x, y, c = _place()
        starts, waits = [], []
        for i, (t, o) in enumerate(zip(ins, outs)):
            hc = t.shape[1] // 2
            out = _remote(t.at[:, pl.ds(c * hc, hc)], o.at[:, pl.ds(c * hc, hc)], send, recv, i, (x, y, 1 - c))
            starts.append(out)
            waits += [_remote(t.at[:, pl.ds(c * hc, hc)], o.at[:, pl.ds((1 - c) * hc, hc)], send, recv, i, (x, y, 1 - c)).wait_recv,
                      out.wait_send]
        return starts, waits

    outs = [jax.ShapeDtypeStruct(t.shape, t.dtype) for t in bufs]
    return _Exchange(bufs, outs, len(bufs), 0, build, in_place=True)


def _everyone_exchange(v):
    def build(ins, outs, send, recv, lsem):
        x, y, c = _place()
        me = 4 * x + 2 * y + c
        mine = pltpu.make_async_copy(ins[0], outs[0].at[me], lsem.at[0])
        starts, waits = [mine], [mine.wait]
        for j in range(7):
            fx, fy, fc = (j + 1) >> 2 & 1, (j + 1) >> 1 & 1, (j + 1) & 1
            peer = (x ^ fx, y ^ fy, c ^ fc)
            out = _remote(ins[0], outs[0].at[me], send, recv, j, peer)
            starts.append(out)
            waits += [_remote(ins[0], outs[0].at[4 * peer[0] + 2 * peer[1] + peer[2]], send, recv, j, peer).wait_recv, out.wait_send]
        return starts, waits

    return _Exchange([v], [jax.ShapeDtypeStruct((8,) + v.shape, v.dtype)], 7, 1, build)


def _both(a, b):
    na_in, na_out = len(a.ins), len(a.outs)

    def build(ins, outs, send, recv, lsem):
        sa, wa = a.build(ins[:na_in], outs[:na_out], send.at[pl.ds(0, a.n_remote)], recv.at[pl.ds(0, a.n_remote)],
                         lsem.at[pl.ds(0, a.n_local)])
        sb, wb = b.build(ins[na_in:], outs[na_out:], send.at[pl.ds(a.n_remote, b.n_remote)], recv.at[pl.ds(a.n_remote, b.n_remote)],
                         lsem.at[pl.ds(a.n_local, b.n_local)])
        return sa + sb, wa + wb

    both = _Exchange(a.ins + b.ins, a.outs + b.outs, a.n_remote + b.n_remote, a.n_local + b.n_local, build, in_place=a.in_place)
    both.n_aliased = na_in
    return both


def _run_exchange(ex, name):
    n_in, n_out = len(ex.ins), len(ex.outs)

    def body(*refs):
        ins, outs, sems = refs[:n_in], refs[n_in:n_in + n_out], refs[n_in + n_out:]
        ex.start(ins, outs, sems)
        ex.wait(ins, outs, sems)

    return pl.pallas_call(
        body, name=name, out_shape=tuple(ex.outs), in_specs=[ANY] * n_in, out_specs=tuple([ANY] * n_out),
        scratch_shapes=ex.sems(), input_output_aliases=ex.aliases(0, 0),
        compiler_params=pltpu.CompilerParams(has_side_effects=True),
    )(*ex.ins)


def _sum_devices(rows):
    def body(r_ref, o_ref):
        acc = r_ref[0]
        for d in range(1, 8):
            acc = acc + r_ref[d]
        o_ref[...] = acc

    vm = pl.BlockSpec(memory_space=pltpu.VMEM)
    return pl.pallas_call(body, name="sum_devices", out_shape=jax.ShapeDtypeStruct(rows.shape[1:], rows.dtype),
                          in_specs=[vm], out_specs=vm)(rows)


def _add_half(g, got):
    n, R, C = g.shape
    hc = C // 2

    def body(c_ref, g_ref, r_ref, o_ref):
        o_ref[...] = (g_ref[...] + r_ref[...]).astype(BF16)

    c = lax.axis_index("c")
    return pl.pallas_call(
        body, name="add_half",
        grid_spec=pltpu.PrefetchScalarGridSpec(
            num_scalar_prefetch=1, grid=(n,),
            in_specs=[pl.BlockSpec((1, R, hc), lambda k, c_ref: (k, 0, c_ref[0])),
                      pl.BlockSpec((1, R, hc), lambda k, c_ref: (k, 0, 0))],
            out_specs=pl.BlockSpec((1, R, hc), lambda k, c_ref: (k, 0, 0))),
        out_shape=jax.ShapeDtypeStruct((n, R, hc), BF16),
        compiler_params=_params(("arbitrary",)),
    )(jnp.reshape(c, (1,)).astype(jnp.int32), g, got)


def _sum_slabs(g, got, recv):
    _, R, C = g.shape
    hc = C // 2

    def body(kc_ref, g_ref, s_ref, r_ref, o_ref):
        o_ref[...] = (((g_ref[0] + s_ref[0]) + r_ref[0].astype(F32)) + r_ref[1].astype(F32)) + r_ref[2].astype(F32)

    kc = jnp.stack([2 * lax.axis_index("x") + lax.axis_index("y"), lax.axis_index("c")]).astype(jnp.int32)
    return pl.pallas_call(
        body, name="sum_slabs",
        grid_spec=pltpu.PrefetchScalarGridSpec(
            num_scalar_prefetch=1, grid=(1,),
            in_specs=[pl.BlockSpec((1, R, hc), lambda i, kc_ref: (kc_ref[0], 0, kc_ref[1])),
                      pl.BlockSpec((1, R, hc), lambda i, kc_ref: (kc_ref[0], 0, 0)),
                      pl.BlockSpec((3, R, hc), lambda i, kc_ref: (0, 0, 0))],
            out_specs=pl.BlockSpec((R, hc), lambda i, kc_ref: (0, kc_ref[1]))),
        out_shape=jax.ShapeDtypeStruct((R, C), F32),
        compiler_params=_params(("arbitrary",)),
    )(kc, g, got, recv)


def _row_tile(rows):
    for cand in (256, 184, 176, 144, 128, 64, 32, 16, 8):
        if rows % cand == 0:
            return cand
    return rows


def _in_proj(x, g1, w_in, w_q12, w_k, w_v, gq, gkv, bfg, ct, st, sel, seq, tm):
    T = x.shape[0]
    nsb = seq // tm

    def body(x_ref, g1_ref, win_ref, wq_ref, wk_ref, wv_ref, gq_ref, gkv_ref, b_ref, ct_ref, st_ref, sel_ref,
             h1_ref, qf_ref, kf_ref, vf_ref, qm_ref, km_ref, vm_ref, lat_ref, qn_ref, kvn_ref, carry):
        i = pl.program_id(0)

        @pl.when(i % nsb == 0)
        def _():
            carry[...] = jnp.zeros_like(carry)

        xv = x_ref[...]
        h = (xv * _rms(xv, D_MODEL) * g1_ref[...]).astype(BF16)
        h1_ref[...] = h
        proj = _dot_nt(h, win_ref[...])
        lane = lax.broadcasted_iota(jnp.int32, (tm, LANES), 1)
        low = lane < 64
        misc_a = proj[:, C_MA:C_END]
        misc_b = pltpu.roll(misc_a, 96, 1)

        z = misc_a + b_ref[...]
        lf = jnp.where(lane < HEADS, jnp.minimum(z, 0.0) - jnp.log1p(jnp.exp(-jnp.abs(z))), 0.0)
        rr = lax.broadcasted_iota(jnp.int32, (tm, tm), 0)
        cc = lax.broadcasted_iota(jnp.int32, (tm, tm), 1)
        tri = (rr >= cc).astype(BF16)
        a0, a1, a2 = _split3(lf)
        c = _dot(tri, a0) + _dot(tri, a1) + _dot(tri, a2) + carry[0:1, :]
        carry[0:1, :] = c[tm - 1:tm, :]
        c0, c1, c2 = _split3(c)
        cpl = _dot(jnp.concatenate([c0, c1, c2], axis=1), sel_ref[...])
        qpad = jnp.where((lane >= 64) & (lane < 67), -1.0, 0.0)
        for j in range(PAIRS):
            qc = proj[:, C_FQ + LANES * j:C_FQ + LANES * (j + 1)] * FOX_SCALE
            kc = proj[:, C_FK + LANES * j:C_FK + LANES * (j + 1)]
            e, o = 2 * LANES * j, 2 * LANES * j + LANES
            qf_ref[:, e:e + LANES] = jnp.where(low, qc, qpad).astype(BF16)
            qf_ref[:, o:o + LANES] = jnp.where(low, pltpu.roll(qc, 64, 1), qpad).astype(BF16)
            kf_ref[:, e:e + LANES] = jnp.where(low, kc, cpl[:, e:e + LANES]).astype(BF16)
            kf_ref[:, o:o + LANES] = jnp.where(low, pltpu.roll(kc, 64, 1), cpl[:, o:o + LANES]).astype(BF16)
        vf_ref[...] = proj[:, C_FV:C_QL].astype(BF16)

        ql = proj[:, C_QL:C_KVL]
        kvl = proj[:, C_KVL:C_MA]
        qn = (ql * _rms(ql, Q_RANK) * gq_ref[...]).astype(BF16)
        kvn = (kvl * _rms(kvl, KV_RANK) * gkv_ref[...]).astype(BF16)
        lat_ref[...] = proj[:, C_QL:C_MB]
        qn_ref[...] = qn
        kvn_ref[...] = kvn
        q12 = _dot_nt(qn, wq_ref[...])
        kn = _dot(kvn, wk_ref[...])
        ctv = ct_ref[...]
        stv = st_ref[...]
        cq = (jnp.where(low, 1.0, 0.0) + ctv) * MLA_SCALE
        sq = stv * MLA_SCALE
        kpe = misc_a * ctv + misc_b * stv
        for hd in range(HEADS):
            s0 = LANES * hd
            qm_ref[:, s0:s0 + LANES] = (q12[:, s0:s0 + LANES] * cq + q12[:, 1024 + s0:1024 + s0 + LANES] * sq).astype(BF16)
            km_ref[:, s0:s0 + LANES] = (kn[:, s0:s0 + LANES] + kpe).astype(BF16)
        vm_ref[...] = _dot(kvn, wv_ref[...]).astype(BF16)

    row = lambda w: pl.BlockSpec((tm, w), lambda i: (i, 0))
    out_shape = (
        jax.ShapeDtypeStruct((T, D_MODEL), BF16),
        jax.ShapeDtypeStruct((T, 1024), BF16), jax.ShapeDtypeStruct((T, 1024), BF16), jax.ShapeDtypeStruct((T, 512), BF16),
        jax.ShapeDtypeStruct((T, 1024), BF16), jax.ShapeDtypeStruct((T, 1024), BF16), jax.ShapeDtypeStruct((T, 512), BF16),
        jax.ShapeDtypeStruct((T, 512), F32),
        jax.ShapeDtypeStruct((T, Q_RANK), BF16), jax.ShapeDtypeStruct((T, KV_RANK), BF16),
    )
    return pl.pallas_call(
        body, name="in_proj", grid=(T // tm,), out_shape=out_shape,
        in_specs=[row(D_MODEL), _full(g1.shape), _full(w_in.shape), _full(w_q12.shape), _full(w_k.shape), _full(w_v.shape),
                  _full(gq.shape), _full(gkv.shape), _full(bfg.shape), row(LANES), row(LANES), _full(sel.shape)],
        out_specs=[row(D_MODEL), row(1024), row(1024), row(512), row(1024), row(1024), row(512), row(512), row(Q_RANK), row(KV_RANK)],
        scratch_shapes=[pltpu.VMEM((8, LANES), F32)],
        compiler_params=_params(("arbitrary",)),
    )(x, g1, w_in, w_q12, w_k, w_v, gq, gkv, bfg, ct, st, sel)


def _attn_fwd(q, k, v, nb, seq, tq, name, ex=None):
    T = q.shape[0]
    nq = seq // tq
    n_in, n_out = (len(ex.ins), len(ex.outs)) if ex else (0, 0)

    def body(*refs):
        q_ref, k_ref, v_ref = refs[0:3]
        o_ref, lse_ref = refs[3 + n_in:5 + n_in]
        b, pr, qi = pl.program_id(0), pl.program_id(1), pl.program_id(2)
        if ex:
            ex_refs = (refs[3:3 + n_in], refs[5 + n_in:5 + n_in + n_out], refs[8 + n_in + n_out:])

            @pl.when((b == 0) & (pr == 0) & (qi == 0))
            def _():
                ex.start(*ex_refs)

        s_sc, p_sc, acc_sc = refs[5 + n_in + n_out:8 + n_in + n_out]
        strip = 64
        key_s = lax.broadcasted_iota(jnp.int32, (strip, tq), 0)
        qry_s = lax.broadcasted_iota(jnp.int32, (strip, tq), 1)
        row_t = lax.broadcasted_iota(jnp.int32, (LANES, tq), 0)
        acc_sc[...] = jnp.zeros(acc_sc.shape, F32)

        def fold(x, op):
            out = x[0:8]
            for r in range(8, strip, 8):
                out = op(out, x[r:r + 8])
            return out

        def step(kj, state, masked):
            rows = pl.ds(pl.multiple_of(kj * tq, tq), tq)
            for hh in range(2):
                s_sc[hh] = _dot_nt(k_ref[rows, LANES * hh:LANES * (hh + 1)], q_ref[:, LANES * hh:LANES * (hh + 1)])
            vv = v_ref[rows, :]
            new = []
            for hh in range(2):
                m, l = state[hh]

                def strip_of(r0, hh=hh):
                    s = s_sc[hh, r0:r0 + strip, :]
                    return jnp.where(key_s + r0 <= qry_s, s, NEG) if masked else s

                mx = fold(strip_of(0), jnp.maximum)
                for r0 in range(strip, tq, strip):
                    mx = jnp.maximum(mx, fold(strip_of(r0), jnp.maximum))
                m_new = jnp.maximum(m, jnp.max(mx, axis=0, keepdims=True))
                alpha = jnp.exp(m - m_new)
                sm = jnp.zeros((8, tq), F32)
                for r0 in range(0, tq, strip):
                    p = jnp.exp(strip_of(r0) - m_new)
                    sm = sm + fold(p, jnp.add)
                    p_sc[hh, r0:r0 + strip, :] = p.astype(BF16)
                l = alpha * l + jnp.sum(sm, axis=0, keepdims=True)
                acc_sc[hh] = alpha * acc_sc[hh] + _dot_tn(vv, p_sc[hh])
                new.append((m_new, l))
            return tuple(new)

        one = (jnp.full((1, tq), NEG, F32), jnp.zeros((1, tq), F32))
        state = lax.fori_loop(0, qi, functools.partial(step, masked=False), (one, one))
        (m0, l0), (m1, l1) = step(qi, state, True)
        o_ref[...] = jnp.where(row_t < 64, acc_sc[0] / l0, acc_sc[1] / l1).T
        lse_ref[:, 0:LANES] = jnp.broadcast_to(m0 + jnp.log(l0), (LANES, tq)).T
        lse_ref[:, LANES:2 * LANES] = jnp.broadcast_to(m1 + jnp.log(l1), (LANES, tq)).T

        if ex:
            @pl.when((b == nb - 1) & (pr == PAIRS - 1) & (qi == nq - 1))
            def _():
                ex.wait(*ex_refs)

    res = pl.pallas_call(
        body, name=name, grid=(nb, PAIRS, nq),
        out_shape=(jax.ShapeDtypeStruct((T, 512), F32), jax.ShapeDtypeStruct((T, 1024), F32)) + tuple(ex.outs if ex else ()),
        in_specs=[pl.BlockSpec((tq, 2 * LANES), lambda b, p, i: (b * nq + i, p)),
                  pl.BlockSpec((seq, 2 * LANES), lambda b, p, i: (b, p)),
                  pl.BlockSpec((seq, LANES), lambda b, p, i: (b, p))] + [ANY] * n_in,
        out_specs=[pl.BlockSpec((tq, LANES), lambda b, p, i: (b * nq + i, p)),
                   pl.BlockSpec((tq, 2 * LANES), lambda b, p, i: (b * nq + i, p))] + [ANY] * n_out,
        scratch_shapes=[pltpu.VMEM((2, tq, tq), F32), pltpu.VMEM((2, tq, tq), BF16), pltpu.VMEM((2, LANES, tq), F32)]
        + (ex.sems() if ex else []),
        compiler_params=_params(("arbitrary", "arbitrary", "arbitrary")),
    )(q, k, v, *(ex.ins if ex else ()))
    return res[0], res[1], list(res[2:])


def _attn_bwd(q, k, v, o, do, lse, nb, seq, tq, name, key_bias, ex=None):
    T = q.shape[0]
    nq = seq // tq
    n_in, n_out = (len(ex.ins), len(ex.outs)) if ex else (0, 0)

    def body(*refs):
        q_ref, k_ref, v_ref, o_ref, do_ref, lse_ref = refs[0:6]
        dq_ref, dk_ref, dv_ref = refs[6 + n_in:9 + n_in]
        dsc, rsum = refs[9 + n_in + n_out:11 + n_in + n_out]
        b, pr, step_no = pl.program_id(0), pl.program_id(1), pl.program_id(2)
        kj = nq - 1 - step_no
        if ex:
            ex_refs = (refs[6:6 + n_in], refs[9 + n_in:9 + n_in + n_out], refs[11 + n_in + n_out:])

            @pl.when((b == 0) & (pr == 0) & (step_no == 0))
            def _():
                ex.start(*ex_refs)

        lane_s = lax.broadcasted_iota(jnp.int32, (seq, LANES), 1)
        lane = lax.broadcasted_iota(jnp.int32, (tq, LANES), 1)
        rr = lax.broadcasted_iota(jnp.int32, (tq, tq), 0)
        cc = lax.broadcasted_iota(jnp.int32, (tq, tq), 1)

        @pl.when(step_no == 0)
        def _():
            dq_ref[...] = jnp.zeros_like(dq_ref)
            prod = do_ref[...].astype(F32) * o_ref[...]
            d0 = jnp.sum(jnp.where(lane_s < 64, prod, 0.0), axis=1, keepdims=True)
            d1 = jnp.sum(jnp.where(lane_s < 64, 0.0, prod), axis=1, keepdims=True)
            dsc[0] = jnp.broadcast_to(d0, (seq, LANES))
            dsc[1] = jnp.broadcast_to(d1, (seq, LANES))
            if key_bias:
                rsum[...] = jnp.zeros_like(rsum)

        vv = v_ref[...]

        def step(qi, carry, masked):
            dkt, dvt, cols = carry
            rows = pl.ds(pl.multiple_of(qi * tq, tq), tq)
            dov = do_ref[rows, :]
            new_dkt, new_cols = [], []
            for hh in range(2):
                qv = q_ref[rows, LANES * hh:LANES * (hh + 1)]
                kv = k_ref[:, LANES * hh:LANES * (hh + 1)]
                dom = jnp.where((lane < 64) if hh == 0 else (lane >= 64), dov, jnp.zeros((), BF16))
                s = _dot_nt(qv, kv)
                if masked:
                    s = jnp.where(cc <= rr, s, NEG)
                p = jnp.exp(s - jnp.tile(lse_ref[rows, LANES * hh:LANES * (hh + 1)], (1, tq // LANES)))
                dp = _dot_nt(dom, vv)
                ds32 = p * (dp - jnp.tile(dsc[hh, rows, :], (1, tq // LANES)))
                col = cols[hh]
                if key_bias:
                    col = col + jnp.sum(ds32, axis=0, keepdims=True)
                    rsum[hh, rows, :] += jnp.broadcast_to(jnp.sum(ds32, axis=1, keepdims=True), (tq, LANES))
                ds = ds32.astype(BF16)
                dvt = dvt + _dot_tn(dom, p.astype(BF16))
                new_dkt.append(dkt[hh] + _dot_tn(qv, ds))
                new_cols.append(col)
                dq_ref[rows, LANES * hh:LANES * (hh + 1)] += _dot(ds, kv)
            return tuple(new_dkt), dvt, tuple(new_cols)

        zt = jnp.zeros((LANES, tq), F32)
        zc = jnp.zeros((1, tq), F32)
        carry = step(kj, ((zt, zt), zt, (zc, zc)), True)
        dkt, dvt, cols = lax.fori_loop(kj + 1, nq, functools.partial(step, masked=False), carry)
        row_t = lax.broadcasted_iota(jnp.int32, (LANES, tq), 0)
        for hh in range(2):
            dk_h = jnp.where(row_t == 64, -cols[hh], dkt[hh]) if key_bias else dkt[hh]
            dk_ref[:, LANES * hh:LANES * (hh + 1)] = dk_h.T
        dv_ref[...] = dvt.T

        if key_bias:
            @pl.when(step_no == nq - 1)
            def _():
                for hh in range(2):
                    blk = dq_ref[:, LANES * hh:LANES * (hh + 1)]
                    dq_ref[:, LANES * hh:LANES * (hh + 1)] = jnp.where(lane_s == 64, rsum[hh], blk)

        if ex:
            @pl.when((b == nb - 1) & (pr == PAIRS - 1) & (step_no == nq - 1))
            def _():
                ex.wait(*ex_refs)

    per_seq = lambda w: pl.BlockSpec((seq, w), lambda b, p, j: (b, p))
    per_blk = lambda w: pl.BlockSpec((tq, w), lambda b, p, j: (b * nq + nq - 1 - j, p))
    res = pl.pallas_call(
        body, name=name, grid=(nb, PAIRS, nq),
        out_shape=(jax.ShapeDtypeStruct((T, 1024), F32), jax.ShapeDtypeStruct((T, 1024), F32), jax.ShapeDtypeStruct((T, 512), F32))
        + tuple(ex.outs if ex else ()),
        in_specs=[per_seq(2 * LANES), per_blk(2 * LANES), per_blk(LANES), per_seq(LANES), per_seq(LANES), per_seq(2 * LANES)] + [ANY] * n_in,
        out_specs=[per_seq(2 * LANES), per_blk(2 * LANES), per_blk(LANES)] + [ANY] * n_out,
        scratch_shapes=[pltpu.VMEM((2, seq, LANES), F32), pltpu.VMEM((2, seq, LANES) if key_bias else (2, 8, LANES), F32)]
        + (ex.sems() if ex else []),
        compiler_params=_params(("arbitrary", "arbitrary", "arbitrary")),
    )(q, k, v, o, do, lse, *(ex.ins if ex else ()))
    return res[0], res[1], res[2], list(res[3:])


def _mid(of, om, x, tgt, g_fo, g_mo, g2, g3, w_o, w_g, w_u, w_d, tm):
    T = x.shape[0]

    def body(of_ref, om_ref, x_ref, t_ref, gfo_ref, gmo_ref, g2_ref, g3_ref, wo_ref, wg_ref, wu_ref, wd_ref,
             a_ref, h2_ref, hid_ref, dg_ref, du_ref, dx3_ref, dx2_ref, dof_ref, dom_ref, st_ref):
        i = pl.program_id(0)

        @pl.when(i == 0)
        def _():
            st_ref[...] = jnp.zeros_like(st_ref)

        ofv, omv = of_ref[...], om_ref[...]
        rf, rm = _rms(ofv, FOX_W), _rms(omv, FOX_W)
        fhat, mhat = ofv * rf, omv * rm
        a = jnp.concatenate([fhat * gfo_ref[...], mhat * gmo_ref[...]], axis=1).astype(BF16)
        a_ref[...] = a
        x2 = x_ref[...] + _dot(a, wo_ref[...])
        r2 = _rms(x2, D_MODEL)
        xh2 = x2 * r2
        h2 = (xh2 * g2_ref[...]).astype(BF16)
        h2_ref[...] = h2
        gt = _dot_nt(h2, wg_ref[...])
        up = _dot_nt(h2, wu_ref[...])
        sg = jax.nn.sigmoid(gt)
        sl = gt * sg
        hid = (sl * up).astype(BF16)
        hid_ref[...] = hid
        x3 = x2 + _dot(hid, wd_ref[...])
        r3 = _rms(x3, D_MODEL)
        xh3 = x3 * r3
        diff = xh3 * g3_ref[...] - t_ref[...]
        dy = diff * (1.0 / D_MODEL)
        st_ref[ROW_LOSS:ROW_LOSS + 1, :] += jnp.sum(diff * diff, axis=0, keepdims=True) * (0.5 / D_MODEL)
        st_ref[ROW_FINAL:ROW_FINAL + 1, :] += jnp.sum(dy * xh3, axis=0, keepdims=True)
        dx3 = _rms_bwd(dy, xh3, r3, g3_ref[...], D_MODEL)
        dx3b = dx3.astype(BF16)
        dx3_ref[...] = dx3b
        dhid = _dot_nt(dx3b, wd_ref[...])
        dg = (dhid * up * (sg * (1.0 + gt * (1.0 - sg)))).astype(BF16)
        du = (dhid * sl).astype(BF16)
        dg_ref[...] = dg
        du_ref[...] = du
        dh2 = _dot(dg, wg_ref[...]) + _dot(du, wu_ref[...])
        st_ref[ROW_NORM_FFN:ROW_NORM_FFN + 1, :] += jnp.sum(dh2 * xh2, axis=0, keepdims=True)
        dx2 = dx3 + _rms_bwd(dh2, xh2, r2, g2_ref[...], D_MODEL)
        dx2_ref[...] = dx2
        da = _dot_nt(dx2.astype(BF16), wo_ref[...])
        daf, dam = da[:, 0:FOX_W], da[:, FOX_W:2 * FOX_W]
        st_ref[ROW_OUT:ROW_OUT + 1, 0:FOX_W] += jnp.sum(daf * fhat, axis=0, keepdims=True)
        st_ref[ROW_OUT:ROW_OUT + 1, FOX_W:2 * FOX_W] += jnp.sum(dam * mhat, axis=0, keepdims=True)
        dof_ref[...] = _rms_bwd(daf, fhat, rf, gfo_ref[...], FOX_W).astype(BF16)
        dom_ref[...] = _rms_bwd(dam, mhat, rm, gmo_ref[...], FOX_W).astype(BF16)

    row = lambda w: pl.BlockSpec((tm, w), lambda i: (i, 0))
    ff = jax.ShapeDtypeStruct((T, D_FF), BF16)
    out_shape = (
        jax.ShapeDtypeStruct((T, 1024), BF16), jax.ShapeDtypeStruct((T, 1024), BF16), ff, ff, ff,
        jax.ShapeDtypeStruct((T, 1024), BF16), jax.ShapeDtypeStruct((T, 1024), F32),
        jax.ShapeDtypeStruct((T, 512), BF16), jax.ShapeDtypeStruct((T, 512), BF16), jax.ShapeDtypeStruct((8, 1024), F32),
    )
    return pl.pallas_call(
        body, name="mid", grid=(T // tm,), out_shape=out_shape,
        in_specs=[row(512), row(512), row(1024), row(1024), _full(g_fo.shape), _full(g_mo.shape), _full(g2.shape), _full(g3.shape),
                  _full(w_o.shape), _full(w_g.shape), _full(w_u.shape), _full(w_d.shape)],
        out_specs=[row(1024), row(1024), row(D_FF), row(D_FF), row(D_FF), row(1024), row(1024), row(512), row(512),
                   pl.BlockSpec((8, 1024), lambda i: (0, 0))],
        compiler_params=_params(("arbitrary",)),
    )(of, om, x, tgt, g_fo, g_mo, g2, g3, w_o, w_g, w_u, w_d)


def _in_bwd(dqf, dkf, dvf, dqm, dkm, dvm, lat, x, dx2, g1, gq, gkv, bfg, ct, st, sel_t, w_in, w_q12, w_kv, seq, tm):
    T = x.shape[0]
    nblk = T // tm
    nsb = seq // tm

    def body(dqf_ref, dkf_ref, dvf_ref, dqm_ref, dkm_ref, dvm_ref, lat_ref, x_ref, dx2_ref, g1_ref, gq_ref, gkv_ref, b_ref,
             ct_ref, st_ref, selt_ref, win_ref, wq_ref, wkv_ref, dx_ref, dproj_ref, dq12_ref, dkv_ref, stat_ref, carry):
        i = pl.program_id(0)

        @pl.when(i == 0)
        def _():
            stat_ref[...] = jnp.zeros_like(stat_ref)

        @pl.when(i % nsb == 0)
        def _():
            carry[...] = jnp.zeros_like(carry)

        lane = lax.broadcasted_iota(jnp.int32, (tm, LANES), 1)
        low = lane < 64
        ctv, stv = ct_ref[...], st_ref[...]

        for j in range(PAIRS):
            e, o = 2 * LANES * j, 2 * LANES * j + LANES
            dq = jnp.where(low, dqf_ref[:, e:e + LANES], 0.0) + pltpu.roll(jnp.where(low, dqf_ref[:, o:o + LANES], 0.0), 64, 1)
            dk = jnp.where(low, dkf_ref[:, e:e + LANES], 0.0) + pltpu.roll(jnp.where(low, dkf_ref[:, o:o + LANES], 0.0), 64, 1)
            dproj_ref[:, C_FQ + LANES * j:C_FQ + LANES * (j + 1)] = (dq * FOX_SCALE).astype(BF16)
            dproj_ref[:, C_FK + LANES * j:C_FK + LANES * (j + 1)] = dk.astype(BF16)
        dproj_ref[:, C_FV:C_QL] = dvf_ref[...].astype(BF16)
        dcv = dkf_ref[...] + dqf_ref[...]
        k_hi = dcv.astype(BF16)
        k_lo = (dcv - k_hi.astype(F32)).astype(BF16)
        dc = _dot(k_hi, selt_ref[...]) + _dot(k_lo, selt_ref[...])
        rr = lax.broadcasted_iota(jnp.int32, (tm, tm), 0)
        cc = lax.broadcasted_iota(jnp.int32, (tm, tm), 1)
        triu = (cc >= rr).astype(BF16)
        a0, a1, a2 = _split3(dc)
        dlf = _dot(triu, a0) + _dot(triu, a1) + _dot(triu, a2) + carry[0:1, :]
        carry[0:1, :] = dlf[0:1, :]
        misc_a = lat_ref[:, Q_RANK + KV_RANK:Q_RANK + KV_RANK + LANES]
        z = misc_a + b_ref[...]
        dz = jnp.where(lane < HEADS, dlf * jax.nn.sigmoid(-z), 0.0)
        stat_ref[ROW_B:ROW_B + 1, 0:LANES] += jnp.sum(dz, axis=0, keepdims=True)

        cq = (jnp.where(low, 1.0, 0.0) + ctv) * MLA_SCALE
        sq = stv * MLA_SCALE
        dkpe = jnp.zeros((tm, LANES), F32)
        for hd in range(HEADS):
            s0 = LANES * hd
            dqh = dqm_ref[:, s0:s0 + LANES]
            dq12_ref[:, s0:s0 + LANES] = (dqh * cq).astype(BF16)
            dq12_ref[:, 1024 + s0:1024 + s0 + LANES] = (dqh * sq).astype(BF16)
            dkpe = dkpe + dkm_ref[:, s0:s0 + LANES]
        dkv_ref[:, 0:1024] = dkm_ref[...].astype(BF16)
        dkv_ref[:, 1024:1536] = dvm_ref[...].astype(BF16)
        dproj_ref[:, C_MA:C_END] = (dz + dkpe * ctv + pltpu.roll(dkpe * stv, 32, 1)).astype(BF16)
        dqn = _dot(dq12_ref[...], wq_ref[...])
        dkvn = _dot_nt(dkv_ref[...], wkv_ref[...])
        ql = lat_ref[:, 0:Q_RANK]
        kvl = lat_ref[:, Q_RANK:Q_RANK + KV_RANK]
        rq, rkv = _rms(ql, Q_RANK), _rms(kvl, KV_RANK)
        qhat, kvhat = ql * rq, kvl * rkv
        stat_ref[ROW_Q:ROW_Q + 1, 0:Q_RANK] += jnp.sum(dqn * qhat, axis=0, keepdims=True)
        stat_ref[ROW_KV:ROW_KV + 1, 0:KV_RANK] += jnp.sum(dkvn * kvhat, axis=0, keepdims=True)
        dproj_ref[:, C_QL:C_KVL] = _rms_bwd(dqn, qhat, rq, gq_ref[...], Q_RANK).astype(BF16)
        dproj_ref[:, C_KVL:C_MA] = _rms_bwd(dkvn, kvhat, rkv, gkv_ref[...], KV_RANK).astype(BF16)

        dh1 = _dot(dproj_ref[...], win_ref[...])
        xv = x_ref[...]
        r1 = _rms(xv, D_MODEL)
        xh = xv * r1
        stat_ref[ROW_NORM_MIX:ROW_NORM_MIX + 1, :] += jnp.sum(dh1 * xh, axis=0, keepdims=True)
        dx_ref[...] = dx2_ref[...] + _rms_bwd(dh1, xh, r1, g1_ref[...], D_MODEL)

    rev = lambda w: pl.BlockSpec((tm, w), lambda i: (nblk - 1 - i, 0))
    out_shape = (
        jax.ShapeDtypeStruct((T, 1024), F32), jax.ShapeDtypeStruct((T, C_END), BF16), jax.ShapeDtypeStruct((T, 2048), BF16),
        jax.ShapeDtypeStruct((T, 1536), BF16), jax.ShapeDtypeStruct((8, 1024), F32),
    )
    return pl.pallas_call(
        body, name="in_bwd", grid=(nblk,), out_shape=out_shape,
        in_specs=[rev(1024), rev(1024), rev(512), rev(1024), rev(1024), rev(512), rev(512), rev(1024), rev(1024),
                  _full(g1.shape), _full(gq.shape), _full(gkv.shape), _full(bfg.shape), rev(LANES), rev(LANES), _full(sel_t.shape),
                  _full(w_in.shape), _full(w_q12.shape), _full(w_kv.shape)],
        out_specs=[rev(1024), rev(C_END), rev(2048), rev(1536), pl.BlockSpec((8, 1024), lambda i: (0, 0))],
        scratch_shapes=[pltpu.VMEM((8, LANES), F32)],
        compiler_params=_params(("arbitrary",)),
    )(dqf, dkf, dvf, dqm, dkm, dvm, lat, x, dx2, g1, gq, gkv, bfg, ct, st, sel_t, w_in, w_q12, w_kv)


def _wgrad(a, b, tk, tt, name, ex=None):
    T, K = a.shape
    N = b.shape[1]
    n_in, n_out = (len(ex.ins), len(ex.outs)) if ex else (0, 0)
    gk, gt = K // tk, T // tt

    def body(*refs):
        a_ref, b_ref, o_ref = refs[0], refs[1], refs[2 + n_in]
        kb, t = pl.program_id(0), pl.program_id(1)
        if ex:
            ex_refs = (refs[2:2 + n_in], refs[3 + n_in:3 + n_in + n_out], refs[3 + n_in + n_out:])

            @pl.when((kb == 0) & (t == 0))
            def _():
                ex.start(*ex_refs)

        @pl.when(t == 0)
        def _():
            o_ref[...] = jnp.zeros_like(o_ref)

        o_ref[...] += _dot_tn(a_ref[...].astype(BF16), b_ref[...].astype(BF16))

        if ex:
            @pl.when((kb == gk - 1) & (t == gt - 1))
            def _():
                ex.wait(*ex_refs)

    res = pl.pallas_call(
        body, name=name, grid=(gk, gt), out_shape=(jax.ShapeDtypeStruct((K, N), F32),) + tuple(ex.outs if ex else ()),
        in_specs=[pl.BlockSpec((tt, tk), lambda kb, t: (t, kb)), pl.BlockSpec((tt, N), lambda kb, t: (t, 0))] + [ANY] * n_in,
        out_specs=[pl.BlockSpec((tk, N), lambda kb, t: (kb, 0))] + [ANY] * n_out,
        scratch_shapes=ex.sems() if ex else [], input_output_aliases=ex.aliases(2, 1) if ex else {},
        compiler_params=_params(("arbitrary", "arbitrary")),
    )(a, b, *(ex.ins if ex else ()))
    return (res[0], list(res[1:])) if ex else res[0]


def _adam_update(w, g, m, v):
    nm = ADAM_B1 * m + (1.0 - ADAM_B1) * g
    nv = ADAM_B2 * v + (1.0 - ADAM_B2) * (g * g)
    m_hat = nm / (1.0 - ADAM_B1 ** ADAM_STEP)
    v_hat = nv / (1.0 - ADAM_B2 ** ADAM_STEP)
    return -ADAM_LR * (m_hat / (jnp.sqrt(v_hat) + ADAM_EPS) + ADAM_WD * w), nm, nv


def _adamw_small(stats, params):
    k = len(SMALL)

    def body(*refs):
        for t, name in enumerate(SMALL):
            row, c0, width = SMALL_AT[name]
            w_ref, m_ref, v_ref = refs[1 + 3 * t:4 + 3 * t]
            g_ref, d_ref, nm_ref, nv_ref = refs[1 + 3 * k + 4 * t:5 + 3 * k + 4 * t]
            g = refs[0][row:row + 1, c0:c0 + width]
            g_ref[...] = g
            d_ref[...], nm_ref[...], nv_ref[...] = _adam_update(w_ref[...], g, m_ref[...], v_ref[...])

    vm = pl.BlockSpec(memory_space=pltpu.VMEM)
    out_shape = tuple(jax.ShapeDtypeStruct((1, SMALL_AT[name][2]), F32) for name in SMALL for _ in range(4))
    res = pl.pallas_call(body, name="adamw_small", out_shape=out_shape, in_specs=[vm] * (1 + 3 * k), out_specs=tuple([vm] * (4 * k)))(
        stats, *[a for name in SMALL for a in params[name]])
    return {name: tuple(res[4 * t:4 * t + 4]) for t, name in enumerate(SMALL)}


def _adamw(tensors, name):
    n = len(tensors)
    R, C = tensors[0][0].shape
    tr = _row_tile(R)

    def body(*refs):
        for t in range(n):
            w_ref, g_ref, m_ref, v_ref = refs[4 * t:4 * t + 4]
            d_ref, nm_ref, nv_ref = refs[4 * n + 3 * t:4 * n + 3 * t + 3]
            d_ref[...], nm_ref[...], nv_ref[...] = _adam_update(w_ref[...], g_ref[...], m_ref[...], v_ref[...])

    blk = pl.BlockSpec((tr, C), lambda i: (i, 0))
    sh = jax.ShapeDtypeStruct((R, C), F32)
    res = pl.pallas_call(
        body, name=name, grid=(R // tr,), out_shape=(sh,) * (3 * n),
        in_specs=[blk] * (4 * n), out_specs=[blk] * (3 * n),
        compiler_params=_params(("arbitrary",)),
    )(*[a for t in tensors for a in t])
    return [tuple(res[3 * t:3 * t + 3]) for t in range(n)]


def _arrange(win_t, wuq_t, wukv):
    dt = win_t.dtype
    z = lambda r: jnp.zeros((r, D_MODEL), dt)
    zh = lambda r: jnp.zeros((HEADS, r, Q_RANK), dt)
    kr1, kr2 = win_t[1928:1944], win_t[1944:1960]
    misc = jnp.concatenate([win_t[1536:1544], z(56), kr1, kr2, kr2, kr1], axis=0)
    w_in = jnp.concatenate([win_t[0:1536], win_t[1544:1928], misc], axis=0)
    wq = wuq_t.reshape(HEADS, 96, Q_RANK)
    q1 = jnp.concatenate([wq, zh(32)], axis=1).reshape(1024, Q_RANK)
    q2 = jnp.concatenate([zh(64), wq[:, 80:96], wq[:, 64:80], zh(32)], axis=1).reshape(1024, Q_RANK)
    wkv = wukv.reshape(KV_RANK, HEADS, 128)
    wk = jnp.concatenate([wkv[:, :, 0:64], jnp.zeros((KV_RANK, HEADS, 64), dt)], axis=2).reshape(KV_RANK, 1024)
    wv = wkv[:, :, 64:128].reshape(KV_RANK, 512)
    return dict(w_in=w_in, w_q12=jnp.concatenate([q1, q2], axis=0), w_k=wk, w_v=wv, w_kv=jnp.concatenate([wk, wv], axis=1))


def _unarrange(g_in, g_q12, g_kv):
    kr1 = g_in[C_MA + 64:C_MA + 80] + g_in[C_MA + 112:C_MA + 128]
    kr2 = g_in[C_MA + 80:C_MA + 96] + g_in[C_MA + 96:C_MA + 112]
    win_t = jnp.concatenate([g_in[0:1536], g_in[C_MA:C_MA + 8], g_in[1536:1920], kr1, kr2], axis=0)
    g1 = g_q12[0:1024].reshape(HEADS, 128, Q_RANK)
    g2 = g_q12[1024:2048].reshape(HEADS, 128, Q_RANK)
    wuq_t = jnp.concatenate([g1[:, 0:64], g1[:, 64:80] + g2[:, 80:96], g1[:, 80:96] + g2[:, 64:80]], axis=1).reshape(768, Q_RANK)
    gk = g_kv[:, 0:1024].reshape(KV_RANK, HEADS, 128)
    gv = g_kv[:, 1024:1536].reshape(KV_RANK, HEADS, 64)
    wukv = jnp.concatenate([gk[:, :, 0:64], gv], axis=2).reshape(KV_RANK, 1024)
    return win_t, wuq_t, wukv


def _selectors():
    sel = np.zeros((384, 1024), np.float32)
    sel_t = np.zeros((1024, LANES), np.float32)
    for h in range(HEADS):
        for piece in range(3):
            sel[LANES * piece + h, LANES * h + 64 + piece] = 1.0
        sel_t[LANES * h + 64, h] = 1.0
    return jnp.asarray(sel, BF16), jnp.asarray(sel_t, BF16)


def _rope_tables(positions):
    inv_freq = 10000.0 ** (-jnp.arange(0, ROPE, 2, dtype=F32) / ROPE)
    n = positions.size
    ang = (positions.reshape(n // 8, 8, 1).astype(F32) * inv_freq[None, None, :]).reshape(n // 8, 8 * (ROPE // 2))
    cos, sin = lax.optimization_barrier((jnp.cos(lax.optimization_barrier(ang)), jnp.sin(lax.optimization_barrier(ang))))
    cos, sin = cos.reshape(n, ROPE // 2), sin.reshape(n, ROPE // 2)
    z64, z32 = jnp.zeros((n, 64), F32), jnp.zeros((n, 32), F32)
    return jnp.concatenate([z64, cos, cos, z32], axis=1), jnp.concatenate([z64, -sin, sin, z32], axis=1)


def _work(name, t):
    return jnp.swapaxes(t[0], 0, 1) if name in TRANSPOSED else t[0]


def _back(name, t):
    return (jnp.swapaxes(t, 0, 1) if name in TRANSPOSED else t)[None]


def kernel(x, positions, norm_mix_g, w_in, b_fgate, q_norm_g, w_uq, kv_norm_g, w_ukv, fox_out_g, mla_out_g, w_o, norm_ffn_g, w_gate, w_up, w_down, final_norm_g, loss_target, m_norm_mix_g, m_w_in, m_b_fgate, m_q_norm_g, m_w_uq, m_kv_norm_g, m_w_ukv, m_fox_out_g, m_mla_out_g, m_w_o, m_norm_ffn_g, m_w_gate, m_w_up, m_w_down, m_final_norm_g, v_norm_mix_g, v_w_in, v_b_fgate, v_q_norm_g, v_w_uq, v_kv_norm_g, v_w_ukv, v_fox_out_g, v_mla_out_g, v_w_o, v_norm_ffn_g, v_w_gate, v_w_up, v_w_down, v_final_norm_g):
    names = ["norm_mix_g", "w_in", "b_fgate", "q_norm_g", "w_uq", "kv_norm_g", "w_ukv", "fox_out_g", "mla_out_g", "w_o",
             "norm_ffn_g", "w_gate", "w_up", "w_down", "final_norm_g"]
    wts = dict(zip(names, [norm_mix_g, w_in, b_fgate, q_norm_g, w_uq, kv_norm_g, w_ukv, fox_out_g, mla_out_g, w_o, norm_ffn_g,
                           w_gate, w_up, w_down, final_norm_g]))
    mom = dict(zip(names, [m_norm_mix_g, m_w_in, m_b_fgate, m_q_norm_g, m_w_uq, m_kv_norm_g, m_w_ukv, m_fox_out_g, m_mla_out_g,
                           m_w_o, m_norm_ffn_g, m_w_gate, m_w_up, m_w_down, m_final_norm_g]))
    var = dict(zip(names, [v_norm_mix_g, v_w_in, v_b_fgate, v_q_norm_g, v_w_uq, v_kv_norm_g, v_w_ukv, v_fox_out_g, v_mla_out_g,
                           v_w_o, v_norm_ffn_g, v_w_gate, v_w_up, v_w_down, v_final_norm_g]))
    shard = {n: _work(n, wts[n]) for n in HEAD3 + FFN4}
    nb, seq, _ = x.shape
    T = nb * seq
    tm, tq = min(ROW_TILE, seq), min(ATTN_TILE, seq)
    tt = min(WGRAD_TILE, T)
    xf = x.reshape(T, D_MODEL)
    tgt = loss_target.reshape(T, D_MODEL)
    chip = 2 * lax.axis_index("x") + lax.axis_index("y")

    mine = [shard[n].astype(BF16) for n in HEAD3]
    head = _run_exchange(_gather_split_exchange(mine), "gather_head")
    win4, wuq4, wukv4 = [lax.dynamic_update_slice(h, s[None], (chip, 0, 0)) for h, s in zip(head, mine)]
    a = _arrange(win4.reshape(-1, D_MODEL), wuq4.reshape(-1, Q_RANK), wukv4.transpose(1, 0, 2).reshape(KV_RANK, -1))
    sel, sel_t = _selectors()
    ct, st = _rope_tables(positions)
    bfg = jnp.concatenate([b_fgate, jnp.zeros((1, LANES - HEADS), F32)], axis=1)
    g1, gq, gkv = norm_mix_g, q_norm_g, kv_norm_g

    h1, qf, kf, vf, qm, km, vm, lat, qn, kvn = _in_proj(xf, g1, a["w_in"], a["w_q12"], a["w_k"], a["w_v"], gq, gkv, bfg, ct, st, sel, seq,
                                                        min(IN_PROJ_TILE, seq))
    tqf = min(ATTN_FWD_TILE, seq)
    of, lse_f, (wo4, wg4) = _attn_fwd(qf, kf, vf, nb, seq, tqf, "fox_fwd", _gather_exchange([shard[n].astype(BF16) for n in FFN4[:2]]))
    om, lse_m, (wu4, wd4) = _attn_fwd(qm, km, vm, nb, seq, tqf, "mla_fwd", _gather_exchange([shard[n].astype(BF16) for n in FFN4[2:]]))
    a_cat, h2, hid, dg, du, dx3, dx2, dof, dom, st_mid = _mid(
        of, om, xf, tgt, fox_out_g, mla_out_g, norm_ffn_g, final_norm_g.reshape(1, D_MODEL),
        wo4.reshape(D_MODEL, D_MODEL), wg4.reshape(D_FF, D_MODEL), wu4.reshape(D_FF, D_MODEL), wd4.reshape(D_FF, D_MODEL), tm)

    slab = lambda g: g.reshape(N_CHIPS, g.shape[0] // N_CHIPS, g.shape[1])
    big = [slab(_wgrad(a_cat, dx2, D_MODEL, tt, "wgrad_o")), slab(_wgrad(dg, h2, D_FF // 2, tt, "wgrad_gate")),
           slab(_wgrad(du, h2, D_FF // 2, tt, "wgrad_up")), slab(_wgrad(hid, dx3, D_FF // 2, tt, "wgrad_down"))]
    dqf, dkf, dvf, got = _attn_bwd(qf, kf, vf, of, dof, lse_f, nb, seq, tq, "fox_bwd", True, _swap_exchange(big))
    sums = [_add_half(g, s) for g, s in zip(big, got)]
    dqm, dkm, dvm, recv = _attn_bwd(qm, km, vm, om, dom, lse_m, nb, seq, tq, "mla_bwd", False, _scatter_exchange(sums))
    halves = [_sum_slabs(g, s, r) for g, s, r in zip(big, got, recv)]
    dx, dproj, dq12, dkv, st_in = _in_bwd(dqf, dkf, dvf, dqm, dkm, dvm, lat, xf, dx2, g1, gq, gkv, bfg, ct, st, sel_t,
                                             a["w_in"], a["w_q12"], a["w_kv"], seq, tm)
    g_in, results = _wgrad(dproj, h1, C_END, tt, "wgrad_in", _both(_join_exchange(halves), _everyone_exchange(st_mid + st_in)))
    gshard = dict(zip(FFN4, results[:4]))
    stats = _sum_devices(results[4])

    gwin_t, gwuq_t, gwukv = _unarrange(g_in, _wgrad(dq12, qn, 2048, tt, "wgrad_uq"), _wgrad(kvn, dkv, KV_RANK, tt, "wgrad_ukv"))
    tail = [slab(gwin_t), slab(gwuq_t), gwukv.reshape(KV_RANK, N_CHIPS, -1).transpose(1, 0, 2)]
    tail_got = _run_exchange(_swap_exchange(tail), "tail_swap")
    tail_sums = [_add_half(g, s) for g, s in zip(tail, tail_got)]
    tail_recv = _run_exchange(_scatter_exchange(tail_sums), "tail_scatter")
    tail_joined = _run_exchange(_join_exchange([_sum_slabs(g, s, r) for g, s, r in zip(tail, tail_got, tail_recv)]), "tail_join")
    gshard.update(zip(HEAD3, tail_joined))
    quad = lambda n: (shard[n], gshard[n], _work(n, mom[n]), _work(n, var[n]))
    updates = dict(zip(FFN4[1:], _adamw([quad(n) for n in FFN4[1:]], "adamw_ffn")))
    for n in HEAD3 + FFN4[:1]:
        updates[n], = _adamw([quad(n)], "adamw_" + n)

    grads, delta, new_m, new_v = {}, {}, {}, {}
    for n in HEAD3 + FFN4:
        grads[n] = _back(n, gshard[n])
        delta[n], new_m[n], new_v[n] = [_back(n, t) for t in updates[n]]
    row = lambda t: t.reshape(1, -1)
    small = _adamw_small(stats, {n: (row(wts[n]), row(mom[n]), row(var[n])) for n in SMALL})
    for n in SMALL:
        grads[n], delta[n], new_m[n], new_v[n] = [t.reshape(wts[n].shape) for t in small[n]]
    loss = jnp.sum(stats[ROW_LOSS])
    return (loss, dx.reshape(x.shape), *[grads[n] for n in names], *[delta[n] for n in names],
            *[new_m[n] for n in names], *[new_v[n] for n in names])
```

```python
import functools

import numpy as np
import jax
import jax.numpy as jnp
from jax import lax
from jax.experimental import pallas as pl
from jax.experimental.pallas import tpu as pltpu

F32 = jnp.float32
BF16 = jnp.bfloat16
MESH = pl.DeviceIdType.MESH

EPS = 1e-6
D_MODEL = 1024
HEADS = 8
PAIRS = HEADS // 2
FOX_W = 512
Q_RANK = 256
KV_RANK = 128
ROPE = 32
D_FF = 2816
N_CHIPS = 4
FOX_SCALE = 64 ** -0.5
MLA_SCALE = 96 ** -0.5
LANES = 128
NEG = -1e30

ADAM_LR, ADAM_B1, ADAM_B2, ADAM_EPS, ADAM_WD, ADAM_STEP = 0.001, 0.9, 0.999, 1e-08, 0.01, 10

C_FQ, C_FK, C_FV, C_QL, C_KVL, C_MA, C_END = 0, 512, 1024, 1536, 1792, 1920, 2048
C_MB = C_END

VMEM_LIMIT = 60 * 1024 * 1024
ROW_TILE = 256
IN_PROJ_TILE = 512
ATTN_TILE = 512
ATTN_FWD_TILE = 1024
WGRAD_TILE = 2048

HEAD3 = ("w_in", "w_uq", "w_ukv")
FFN4 = ("w_o", "w_gate", "w_up", "w_down")
TRANSPOSED = ("w_in", "w_uq", "w_gate", "w_up")
SMALL = ("norm_mix_g", "b_fgate", "q_norm_g", "kv_norm_g", "fox_out_g", "mla_out_g", "norm_ffn_g", "final_norm_g")
ROW_NORM_MIX, ROW_NORM_FFN, ROW_FINAL, ROW_OUT, ROW_Q, ROW_KV, ROW_B, ROW_LOSS = range(8)
SMALL_AT = {"norm_mix_g": (ROW_NORM_MIX, 0, 1024), "norm_ffn_g": (ROW_NORM_FFN, 0, 1024), "final_norm_g": (ROW_FINAL, 0, 1024),
            "fox_out_g": (ROW_OUT, 0, 512), "mla_out_g": (ROW_OUT, 512, 512), "q_norm_g": (ROW_Q, 0, 256),
            "kv_norm_g": (ROW_KV, 0, 128), "b_fgate": (ROW_B, 0, 8)}


def _params(sem=None):
    return pltpu.CompilerParams(dimension_semantics=sem, vmem_limit_bytes=VMEM_LIMIT)


def _full(shape):
    n = len(shape)
    return pl.BlockSpec(shape, lambda *_: (0,) * n, pipeline_mode=pl.Buffered(1))


def _dot(a, b):
    return jnp.dot(a, b, preferred_element_type=F32)


def _dot_nt(a, b):
    return lax.dot_general(a, b, (((1,), (1,)), ((), ())), preferred_element_type=F32)


def _dot_tn(a, b):
    return lax.dot_general(a, b, (((0,), (0,)), ((), ())), preferred_element_type=F32)


def _split3(v):
    hi = v.astype(BF16)
    r1 = v - hi.astype(F32)
    mid = r1.astype(BF16)
    lo = (r1 - mid.astype(F32)).astype(BF16)
    return hi, mid, lo


def _rms(v, width):
    return lax.rsqrt(jnp.sum(v * v, axis=1, keepdims=True) * (1.0 / width) + EPS)


def _rms_bwd(dy, xhat, r, g, width):
    u = dy * g
    return r * (u - xhat * (jnp.sum(u * xhat, axis=1, keepdims=True) * (1.0 / width)))


ANY = pl.BlockSpec(memory_space=pl.ANY)


def _place():
    return lax.axis_index("x"), lax.axis_index("y"), lax.axis_index("c")


def _other_chips(x, y):
    return [(1 - x, y), (x, 1 - y), (1 - x, 1 - y)]


def _remote(src, dst, send, recv, j, dev):
    return pltpu.make_async_remote_copy(src_ref=src, dst_ref=dst, send_sem=send.at[j], recv_sem=recv.at[j], device_id=dev, device_id_type=MESH)


class _Exchange:
    def __init__(self, ins, outs, n_remote, n_local, build, in_place=False):
        self.ins, self.outs, self.n_remote, self.n_local, self.build = list(ins), list(outs), n_remote, max(n_local, 1), build
        self.in_place = in_place
        self.n_aliased = len(self.ins)

    def aliases(self, first_in, first_out):
        return {first_in + i: first_out + i for i in range(self.n_aliased)} if self.in_place else {}

    def sems(self):
        return [pltpu.SemaphoreType.DMA((self.n_remote,)), pltpu.SemaphoreType.DMA((self.n_remote,)), pltpu.SemaphoreType.DMA((self.n_local,))]

    def start(self, in_refs, out_refs, sems):
        for cp in self.build(in_refs, out_refs, *sems)[0]:
            cp.start()

    def wait(self, in_refs, out_refs, sems):
        for w in self.build(in_refs, out_refs, *sems)[1]:
            w()


def _gather_exchange(shards):
    def build(ins, outs, send, recv, lsem):
        x, y, c = _place()
        starts, waits = [], []
        for i, (s, o) in enumerate(zip(ins, outs)):
            mine = pltpu.make_async_copy(s, o.at[2 * x + y], lsem.at[i])
            starts.append(mine)
            waits.append(mine.wait)
            for j, (cx, cy) in enumerate(_other_chips(x, y)):
                out = _remote(s, o.at[2 * x + y], send, recv, 3 * i + j, (cx, cy, c))
                starts.append(out)
                waits.append(_remote(s, o.at[2 * cx + cy], send, recv, 3 * i + j, (cx, cy, c)).wait_recv)
                waits.append(out.wait_send)
        return starts, waits

    outs = [jax.ShapeDtypeStruct((N_CHIPS,) + s.shape, s.dtype) for s in shards]
    return _Exchange(shards, outs, 3 * len(shards), len(shards), build)


def _gather_split_exchange(shards):
    n = len(shards)

    def build(ins, outs, send, recv, lsem):
        x, y, c = _place()
        starts, waits, last = [], [], []
        for i, (s, o) in enumerate(zip(ins, outs)):
            own = pltpu.make_async_copy(s, o.at[2 * x + y], lsem.at[i])
            starts.append(own)
            last.append(own.wait)
            hc = s.shape[1] // 2
            mine, other = pl.ds(c * hc, hc), pl.ds((1 - c) * hc, hc)
            for j, (cx, cy) in enumerate(_other_chips(x, y)):
                out = _remote(s.at[:, mine], o.at[2 * x + y, :, mine], send, recv, 3 * i + j, (cx, cy, c))
                landed = o.at[2 * cx + cy, :, mine]
                arrive = _remote(s.at[:, mine], landed, send, recv, 3 * i + j, (cx, cy, c))
                onward = _remote(landed, landed, send, recv, 3 * n + 3 * i + j, (x, y, 1 - c))
                from_sibling = _remote(landed, o.at[2 * cx + cy, :, other], send, recv, 3 * n + 3 * i + j, (x, y, 1 - c))
                starts.append(out)
                waits.append(lambda arrive=arrive, onward=onward: (arrive.wait_recv(), onward.start()))
                last += [from_sibling.wait_recv, onward.wait_send, out.wait_send]
        return starts, waits + last

    outs = [jax.ShapeDtypeStruct((N_CHIPS,) + s.shape, s.dtype) for s in shards]
    return _Exchange(shards, outs, 6 * n, n, build)


def _swap_exchange(grads):
    def build(ins, outs, send, recv, lsem):
        x, y, c = _place()
        cps = []
        for i, (g, o) in enumerate(zip(ins, outs)):
            hc = g.shape[2] // 2
            cps.append(_remote(g.at[:, :, pl.ds((1 - c) * hc, hc)], o, send, recv, i, (x, y, 1 - c)))
        return cps, [cp.wait for cp in cps]

    outs = [jax.ShapeDtypeStruct((g.shape[0], g.shape[1], g.shape[2] // 2), g.dtype) for g in grads]
    return _Exchange(grads, outs, len(grads), 0, build)


def _scatter_exchange(sums):
    def build(ins, outs, send, recv, lsem):
        x, y, c = _place()
        cps = []
        for i, (s, o) in enumerate(zip(ins, outs)):
            for j, (cx, cy) in enumerate(_other_chips(x, y)):
                cps.append(_remote(s.at[2 * cx + cy], o.at[j], send, recv, 3 * i + j, (cx, cy, c)))
        return cps, [cp.wait for cp in cps]

    outs = [jax.ShapeDtypeStruct((3,) + s.shape[1:], s.dtype) for s in sums]
    return _Exchange(sums, outs, 3 * len(sums), 0, build)


def _join_exchange(bufs):
    def build(ins, outs, send, recv, lsem):
        x, y, c = _place()
        starts, waits = [], []
        for i, (t, o) in enumerate(zip(ins, outs)):
            hc = t.shape[1] // 2
            out = _remote(t.at[:, pl.ds(c * hc, hc)], o.at[:, pl.ds(c * hc, hc)], send, recv, i, (x, y, 1 - c))
            starts.append(out)
            waits += [_remote(t.at[:, pl.ds(c * hc, hc)], o.at[:, pl.ds((1 - c) * hc, hc)], send, recv, i, (x, y, 1 - c)).wait_recv,
                      out.wait_send]
        return starts, waits

    outs = [jax.ShapeDtypeStruct(t.shape, t.dtype) for t in bufs]
    return _Exchange(bufs, outs, len(bufs), 0, build, in_place=True)


def _everyone_exchange(v):
    def build(ins, outs, send, recv, lsem):
        x, y, c = _place()
        me = 4 * x + 2 * y + c
        mine = pltpu.make_async_copy(ins[0], outs[0].at[me], lsem.at[0])
        starts, waits = [mine], [mine.wait]
        for j in range(7):
            fx, fy, fc = (j + 1) >> 2 & 1, (j + 1) >> 1 & 1, (j + 1) & 1
            peer = (x ^ fx, y ^ fy, c ^ fc)
            out = _remote(ins[0], outs[0].at[me], send, recv, j, peer)
            starts.append(out)
            waits += [_remote(ins[0], outs[0].at[4 * peer[0] + 2 * peer[1] + peer[2]], send, recv, j, peer).wait_recv, out.wait_send]
        return starts, waits

    return _Exchange([v], [jax.ShapeDtypeStruct((8,) + v.shape, v.dtype)], 7, 1, build)


def _both(a, b):
    na_in, na_out = len(a.ins), len(a.outs)

    def build(ins, outs, send, recv, lsem):
        sa, wa = a.build(ins[:na_in], outs[:na_out], send.at[pl.ds(0, a.n_remote)], recv.at[pl.ds(0, a.n_remote)],
                         lsem.at[pl.ds(0, a.n_local)])
        sb, wb = b.build(ins[na_in:], outs[na_out:], send.at[pl.ds(a.n_remote, b.n_remote)], recv.at[pl.ds(a.n_remote, b.n_remote)],
                         lsem.at[pl.ds(a.n_local, b.n_local)])
        return sa + sb, wa + wb

    both = _Exchange(a.ins + b.ins, a.outs + b.outs, a.n_remote + b.n_remote, a.n_local + b.n_local, build, in_place=a.in_place)
    both.n_aliased = na_in
    return both


def _run_exchange(ex, name):
    n_in, n_out = len(ex.ins), len(ex.outs)

    def body(*refs):
        ins, outs, sems = refs[:n_in], refs[n_in:n_in + n_out], refs[n_in + n_out:]
        ex.start(ins, outs, sems)
        ex.wait(ins, outs, sems)

    return pl.pallas_call(
        body, name=name, out_shape=tuple(ex.outs), in_specs=[ANY] * n_in, out_specs=tuple([ANY] * n_out),
        scratch_shapes=ex.sems(), input_output_aliases=ex.aliases(0, 0),
        compiler_params=pltpu.CompilerParams(has_side_effects=True),
    )(*ex.ins)


def _sum_devices(rows):
    def body(r_ref, o_ref):
        acc = r_ref[0]
        for d in range(1, 8):
            acc = acc + r_ref[d]
        o_ref[...] = acc

    vm = pl.BlockSpec(memory_space=pltpu.VMEM)
    return pl.pallas_call(body, name="sum_devices", out_shape=jax.ShapeDtypeStruct(rows.shape[1:], rows.dtype),
                          in_specs=[vm], out_specs=vm)(rows)


def _add_half(g, got):
    n, R, C = g.shape
    hc = C // 2

    def body(c_ref, g_ref, r_ref, o_ref):
        o_ref[...] = (g_ref[...] + r_ref[...]).astype(BF16)

    c = lax.axis_index("c")
    return pl.pallas_call(
        body, name="add_half",
        grid_spec=pltpu.PrefetchScalarGridSpec(
            num_scalar_prefetch=1, grid=(n,),
            in_specs=[pl.BlockSpec((1, R, hc), lambda k, c_ref: (k, 0, c_ref[0])),
                      pl.BlockSpec((1, R, hc), lambda k, c_ref: (k, 0, 0))],
            out_specs=pl.BlockSpec((1, R, hc), lambda k, c_ref: (k, 0, 0))),
        out_shape=jax.ShapeDtypeStruct((n, R, hc), BF16),
        compiler_params=_params(("arbitrary",)),
    )(jnp.reshape(c, (1,)).astype(jnp.int32), g, got)


def _sum_slabs(g, got, recv):
    _, R, C = g.shape
    hc = C // 2

    def body(kc_ref, g_ref, s_ref, r_ref, o_ref):
        o_ref[...] = (((g_ref[0] + s_ref[0]) + r_ref[0].astype(F32)) + r_ref[1].astype(F32)) + r_ref[2].astype(F32)

    kc = jnp.stack([2 * lax.axis_index("x") + lax.axis_index("y"), lax.axis_index("c")]).astype(jnp.int32)
    return pl.pallas_call(
        body, name="sum_slabs",
        grid_spec=pltpu.PrefetchScalarGridSpec(
            num_scalar_prefetch=1, grid=(1,),
            in_specs=[pl.BlockSpec((1, R, hc), lambda i, kc_ref: (kc_ref[0], 0, kc_ref[1])),
                      pl.BlockSpec((1, R, hc), lambda i, kc_ref: (kc_ref[0], 0, 0)),
                      pl.BlockSpec((3, R, hc), lambda i, kc_ref: (0, 0, 0))],
            out_specs=pl.BlockSpec((R, hc), lambda i, kc_ref: (0, kc_ref[1]))),
        out_shape=jax.ShapeDtypeStruct((R, C), F32),
        compiler_params=_params(("arbitrary",)),
    )(kc, g, got, recv)


def _row_tile(rows):
    for cand in (256, 184, 176, 144, 128, 64, 32, 16, 8):
        if rows % cand == 0:
            return cand
    return rows


def _in_proj(x, g1, w_in, w_q12, w_k, w_v, gq, gkv, bfg, ct, st, sel, seq, tm):
    T = x.shape[0]
    nsb = seq // tm

    def body(x_ref, g1_ref, win_ref, wq_ref, wk_ref, wv_ref, gq_ref, gkv_ref, b_ref, ct_ref, st_ref, sel_ref,
             h1_ref, qf_ref, kf_ref, vf_ref, qm_ref, km_ref, vm_ref, lat_ref, qn_ref, kvn_ref, carry):
        i = pl.program_id(0)

        @pl.when(i % nsb == 0)
        def _():
            carry[...] = jnp.zeros_like(carry)

        xv = x_ref[...]
        h = (xv * _rms(xv, D_MODEL) * g1_ref[...]).astype(BF16)
        h1_ref[...] = h
        proj = _dot_nt(h, win_ref[...])
        lane = lax.broadcasted_iota(jnp.int32, (tm, LANES), 1)
        low = lane < 64
        misc_a = proj[:, C_MA:C_END]
        misc_b = pltpu.roll(misc_a, 96, 1)

        z = misc_a + b_ref[...]
        lf = jnp.where(lane < HEADS, jnp.minimum(z, 0.0) - jnp.log1p(jnp.exp(-jnp.abs(z))), 0.0)
        rr = lax.broadcasted_iota(jnp.int32, (tm, tm), 0)
        cc = lax.broadcasted_iota(jnp.int32, (tm, tm), 1)
        tri = (rr >= cc).astype(BF16)
        a0, a1, a2 = _split3(lf)
        c = _dot(tri, a0) + _dot(tri, a1) + _dot(tri, a2) + carry[0:1, :]
        carry[0:1, :] = c[tm - 1:tm, :]
        c0, c1, c2 = _split3(c)
        cpl = _dot(jnp.concatenate([c0, c1, c2], axis=1), sel_ref[...])
        qpad = jnp.where((lane >= 64) & (lane < 67), -1.0, 0.0)
        for j in range(PAIRS):
            qc = proj[:, C_FQ + LANES * j:C_FQ + LANES * (j + 1)] * FOX_SCALE
            kc = proj[:, C_FK + LANES * j:C_FK + LANES * (j + 1)]
            e, o = 2 * LANES * j, 2 * LANES * j + LANES
            qf_ref[:, e:e + LANES] = jnp.where(low, qc, qpad).astype(BF16)
            qf_ref[:, o:o + LANES] = jnp.where(low, pltpu.roll(qc, 64, 1), qpad).astype(BF16)
            kf_ref[:, e:e + LANES] = jnp.where(low, kc, cpl[:, e:e + LANES]).astype(BF16)
            kf_ref[:, o:o + LANES] = jnp.where(low, pltpu.roll(kc, 64, 1), cpl[:, o:o + LANES]).astype(BF16)
        vf_ref[...] = proj[:, C_FV:C_QL].astype(BF16)

        ql = proj[:, C_QL:C_KVL]
        kvl = proj[:, C_KVL:C_MA]
        qn = (ql * _rms(ql, Q_RANK) * gq_ref[...]).astype(BF16)
        kvn = (kvl * _rms(kvl, KV_RANK) * gkv_ref[...]).astype(BF16)
        lat_ref[...] = proj[:, C_QL:C_MB]
        qn_ref[...] = qn
        kvn_ref[...] = kvn
        q12 = _dot_nt(qn, wq_ref[...])
        kn = _dot(kvn, wk_ref[...])
        ctv = ct_ref[...]
        stv = st_ref[...]
        cq = (jnp.where(low, 1.0, 0.0) + ctv) * MLA_SCALE
        sq = stv * MLA_SCALE
        kpe = misc_a * ctv + misc_b * stv
        for hd in range(HEADS):
            s0 = LANES * hd
            qm_ref[:, s0:s0 + LANES] = (q12[:, s0:s0 + LANES] * cq + q12[:, 1024 + s0:1024 + s0 + LANES] * sq).astype(BF16)
            km_ref[:, s0:s0 + LANES] = (kn[:, s0:s0 + LANES] + kpe).astype(BF16)
        vm_ref[...] = _dot(kvn, wv_ref[...]).astype(BF16)

    row = lambda w: pl.BlockSpec((tm, w), lambda i: (i, 0))
    out_shape = (
        jax.ShapeDtypeStruct((T, D_MODEL), BF16),
        jax.ShapeDtypeStruct((T, 1024), BF16), jax.ShapeDtypeStruct((T, 1024), BF16), jax.ShapeDtypeStruct((T, 512), BF16),
        jax.ShapeDtypeStruct((T, 1024), BF16), jax.ShapeDtypeStruct((T, 1024), BF16), jax.ShapeDtypeStruct((T, 512), BF16),
        jax.ShapeDtypeStruct((T, 512), F32),
        jax.ShapeDtypeStruct((T, Q_RANK), BF16), jax.ShapeDtypeStruct((T, KV_RANK), BF16),
    )
    return pl.pallas_call(
        body, name="in_proj", grid=(T // tm,), out_shape=out_shape,
        in_specs=[row(D_MODEL), _full(g1.shape), _full(w_in.shape), _full(w_q12.shape), _full(w_k.shape), _full(w_v.shape),
                  _full(gq.shape), _full(gkv.shape), _full(bfg.shape), row(LANES), row(LANES), _full(sel.shape)],
        out_specs=[row(D_MODEL), row(1024), row(1024), row(512), row(1024), row(1024), row(512), row(512), row(Q_RANK), row(KV_RANK)],
        scratch_shapes=[pltpu.VMEM((8, LANES), F32)],
        compiler_params=_params(("arbitrary",)),
    )(x, g1, w_in, w_q12, w_k, w_v, gq, gkv, bfg, ct, st, sel)


def _attn_fwd(q, k, v, nb, seq, tq, name, ex=None):
    T = q.shape[0]
    nq = seq // tq
    n_in, n_out = (len(ex.ins), len(ex.outs)) if ex else (0, 0)

    def body(*refs):
        q_ref, k_ref, v_ref = refs[0:3]
        o_ref, lse_ref = refs[3 + n_in:5 + n_in]
        b, pr, qi = pl.program_id(0), pl.program_id(1), pl.program_id(2)
        if ex:
            ex_refs = (refs[3:3 + n_in], refs[5 + n_in:5 + n_in + n_out], refs[8 + n_in + n_out:])

            @pl.when((b == 0) & (pr == 0) & (qi == 0))
            def _():
                ex.start(*ex_refs)

        s_sc, p_sc, acc_sc = refs[5 + n_in + n_out:8 + n_in + n_out]
        strip = 64
        key_s = lax.broadcasted_iota(jnp.int32, (strip, tq), 0)
        qry_s = lax.broadcasted_iota(jnp.int32, (strip, tq), 1)
        row_t = lax.broadcasted_iota(jnp.int32, (LANES, tq), 0)
        acc_sc[...] = jnp.zeros(acc_sc.shape, F32)

        def fold(x, op):
            out = x[0:8]
            for r in range(8, strip, 8):
                out = op(out, x[r:r + 8])
            return out

        def step(kj, state, masked):
            rows = pl.ds(pl.multiple_of(kj * tq, tq), tq)
            for hh in range(2):
                s_sc[hh] = _dot_nt(k_ref[rows, LANES * hh:LANES * (hh + 1)], q_ref[:, LANES * hh:LANES * (hh + 1)])
            vv = v_ref[rows, :]
            new = []
            for hh in range(2):
                m, l = state[hh]

                def strip_of(r0, hh=hh):
                    s = s_sc[hh, r0:r0 + strip, :]
                    return jnp.where(key_s + r0 <= qry_s, s, NEG) if masked else s

                mx = fold(strip_of(0), jnp.maximum)
                for r0 in range(strip, tq, strip):
                    mx = jnp.maximum(mx, fold(strip_of(r0), jnp.maximum))
                m_new = jnp.maximum(m, jnp.max(mx, axis=0, keepdims=True))
                alpha = jnp.exp(m - m_new)
                sm = jnp.zeros((8, tq), F32)
                for r0 in range(0, tq, strip):
                    p = jnp.exp(strip_of(r0) - m_new)
                    sm = sm + fold(p, jnp.add)
                    p_sc[hh, r0:r0 + strip, :] = p.astype(BF16)
                l = alpha * l + jnp.sum(sm, axis=0, keepdims=True)
                acc_sc[hh] = alpha * acc_sc[hh] + _dot_tn(vv, p_sc[hh])
                new.append((m_new, l))
            return tuple(new)

        one = (jnp.full((1, tq), NEG, F32), jnp.zeros((1, tq), F32))
        state = lax.fori_loop(0, qi, functools.partial(step, masked=False), (one, one))
        (m0, l0), (m1, l1) = step(qi, state, True)
        o_ref[...] = jnp.where(row_t < 64, acc_sc[0] / l0, acc_sc[1] / l1).T
        lse_ref[:, 0:LANES] = jnp.broadcast_to(m0 + jnp.log(l0), (LANES, tq)).T
        lse_ref[:, LANES:2 * LANES] = jnp.broadcast_to(m1 + jnp.log(l1), (LANES, tq)).T

        if ex:
            @pl.when((b == nb - 1) & (pr == PAIRS - 1) & (qi == nq - 1))
            def _():
                ex.wait(*ex_refs)

    res = pl.pallas_call(
        body, name=name, grid=(nb, PAIRS, nq),
        out_shape=(jax.ShapeDtypeStruct((T, 512), F32), jax.ShapeDtypeStruct((T, 1024), F32)) + tuple(ex.outs if ex else ()),
        in_specs=[pl.BlockSpec((tq, 2 * LANES), lambda b, p, i: (b * nq + i, p)),
                  pl.BlockSpec((seq, 2 * LANES), lambda b, p, i: (b, p)),
                  pl.BlockSpec((seq, LANES), lambda b, p, i: (b, p))] + [ANY] * n_in,
        out_specs=[pl.BlockSpec((tq, LANES), lambda b, p, i: (b * nq + i, p)),
                   pl.BlockSpec((tq, 2 * LANES), lambda b, p, i: (b * nq + i, p))] + [ANY] * n_out,
        scratch_shapes=[pltpu.VMEM((2, tq, tq), F32), pltpu.VMEM((2, tq, tq), BF16), pltpu.VMEM((2, LANES, tq), F32)]
        + (ex.sems() if ex else []),
        compiler_params=_params(("arbitrary", "arbitrary", "arbitrary")),
    )(q, k, v, *(ex.ins if ex else ()))
    return res[0], res[1], list(res[2:])


def _attn_bwd(q, k, v, o, do, lse, nb, seq, tq, name, key_bias, ex=None):
    T = q.shape[0]
    nq = seq // tq
    n_in, n_out = (len(ex.ins), len(ex.outs)) if ex else (0, 0)

    def body(*refs):
        q_ref, k_ref, v_ref, o_ref, do_ref, lse_ref = refs[0:6]
        dq_ref, dk_ref, dv_ref = refs[6 + n_in:9 + n_in]
        dsc, rsum = refs[9 + n_in + n_out:11 + n_in + n_out]
        b, pr, step_no = pl.program_id(0), pl.program_id(1), pl.program_id(2)
        kj = nq - 1 - step_no
        if ex:
            ex_refs = (refs[6:6 + n_in], refs[9 + n_in:9 + n_in + n_out], refs[11 + n_in + n_out:])

            @pl.when((b == 0) & (pr == 0) & (step_no == 0))
            def _():
                ex.start(*ex_refs)

        lane_s = lax.broadcasted_iota(jnp.int32, (seq, LANES), 1)
        lane = lax.broadcasted_iota(jnp.int32, (tq, LANES), 1)
        rr = lax.broadcasted_iota(jnp.int32, (tq, tq), 0)
        cc = lax.broadcasted_iota(jnp.int32, (tq, tq), 1)

        @pl.when(step_no == 0)
        def _():
            dq_ref[...] = jnp.zeros_like(dq_ref)
            prod = do_ref[...].astype(F32) * o_ref[...]
            d0 = jnp.sum(jnp.where(lane_s < 64, prod, 0.0), axis=1, keepdims=True)
            d1 = jnp.sum(jnp.where(lane_s < 64, 0.0, prod), axis=1, keepdims=True)
            dsc[0] = jnp.broadcast_to(d0, (seq, LANES))
            dsc[1] = jnp.broadcast_to(d1, (seq, LANES))
            if key_bias:
                rsum[...] = jnp.zeros_like(rsum)

        vv = v_ref[...]

        def step(qi, carry, masked):
            dkt, dvt, cols = carry
            rows = pl.ds(pl.multiple_of(qi * tq, tq), tq)
            dov = do_ref[rows, :]
            new_dkt, new_cols = [], []
            for hh in range(2):
                qv = q_ref[rows, LANES * hh:LANES * (hh + 1)]
                kv = k_ref[:, LANES * hh:LANES * (hh + 1)]
                dom = jnp.where((lane < 64) if hh == 0 else (lane >= 64), dov, jnp.zeros((), BF16))
                s = _dot_nt(qv, kv)
                if masked:
                    s = jnp.where(cc <= rr, s, NEG)
                p = jnp.exp(s - jnp.tile(lse_ref[rows, LANES * hh:LANES * (hh + 1)], (1, tq // LANES)))
                dp = _dot_nt(dom, vv)
                ds32 = p * (dp - jnp.tile(dsc[hh, rows, :], (1, tq // LANES)))
                col = cols[hh]
                if key_bias:
                    col = col + jnp.sum(ds32, axis=0, keepdims=True)
                    rsum[hh, rows, :] += jnp.broadcast_to(jnp.sum(ds32, axis=1, keepdims=True), (tq, LANES))
                ds = ds32.astype(BF16)
                dvt = dvt + _dot_tn(dom, p.astype(BF16))
                new_dkt.append(dkt[hh] + _dot_tn(qv, ds))
                new_cols.append(col)
                dq_ref[rows, LANES * hh:LANES * (hh + 1)] += _dot(ds, kv)
            return tuple(new_dkt), dvt, tuple(new_cols)

        zt = jnp.zeros((LANES, tq), F32)
        zc = jnp.zeros((1, tq), F32)
        carry = step(kj, ((zt, zt), zt, (zc, zc)), True)
        dkt, dvt, cols = lax.fori_loop(kj + 1, nq, functools.partial(step, masked=False), carry)
        row_t = lax.broadcasted_iota(jnp.int32, (LANES, tq), 0)
        for hh in range(2):
            dk_h = jnp.where(row_t == 64, -cols[hh], dkt[hh]) if key_bias else dkt[hh]
            dk_ref[:, LANES * hh:LANES * (hh + 1)] = dk_h.T
        dv_ref[...] = dvt.T

        if key_bias:
            @pl.when(step_no == nq - 1)
            def _():
                for hh in range(2):
                    blk = dq_ref[:, LANES * hh:LANES * (hh + 1)]
                    dq_ref[:, LANES * hh:LANES * (hh + 1)] = jnp.where(lane_s == 64, rsum[hh], blk)

        if ex:
            @pl.when((b == nb - 1) & (pr == PAIRS - 1) & (step_no == nq - 1))
            def _():
                ex.wait(*ex_refs)

    per_seq = lambda w: pl.BlockSpec((seq, w), lambda b, p, j: (b, p))
    per_blk = lambda w: pl.BlockSpec((tq, w), lambda b, p, j: (b * nq + nq - 1 - j, p))
    res = pl.pallas_call(
        body, name=name, grid=(nb, PAIRS, nq),
        out_shape=(jax.ShapeDtypeStruct((T, 1024), F32), jax.ShapeDtypeStruct((T, 1024), F32), jax.ShapeDtypeStruct((T, 512), F32))
        + tuple(ex.outs if ex else ()),
        in_specs=[per_seq(2 * LANES), per_blk(2 * LANES), per_blk(LANES), per_seq(LANES), per_seq(LANES), per_seq(2 * LANES)] + [ANY] * n_in,
        out_specs=[per_seq(2 * LANES), per_blk(2 * LANES), per_blk(LANES)] + [ANY] * n_out,
        scratch_shapes=[pltpu.VMEM((2, seq, LANES), F32), pltpu.VMEM((2, seq, LANES) if key_bias else (2, 8, LANES), F32)]
        + (ex.sems() if ex else []),
        compiler_params=_params(("arbitrary", "arbitrary", "arbitrary")),
    )(q, k, v, o, do, lse, *(ex.ins if ex else ()))
    return res[0], res[1], res[2], list(res[3:])


def _mid(of, om, x, tgt, g_fo, g_mo, g2, g3, w_o, w_g, w_u, w_d, tm):
    T = x.shape[0]

    def body(of_ref, om_ref, x_ref, t_ref, gfo_ref, gmo_ref, g2_ref, g3_ref, wo_ref, wg_ref, wu_ref, wd_ref,
             a_ref, h2_ref, hid_ref, dg_ref, du_ref, dx3_ref, dx2_ref, dof_ref, dom_ref, st_ref):
        i = pl.program_id(0)

        @pl.when(i == 0)
        def _():
            st_ref[...] = jnp.zeros_like(st_ref)

        ofv, omv = of_ref[...], om_ref[...]
        rf, rm = _rms(ofv, FOX_W), _rms(omv, FOX_W)
        fhat, mhat = ofv * rf, omv * rm
        a = jnp.concatenate([fhat * gfo_ref[...], mhat * gmo_ref[...]], axis=1).astype(BF16)
        a_ref[...] = a
        x2 = x_ref[...] + _dot(a, wo_ref[...])
        r2 = _rms(x2, D_MODEL)
        xh2 = x2 * r2
        h2 = (xh2 * g2_ref[...]).astype(BF16)
        h2_ref[...] = h2
        gt = _dot_nt(h2, wg_ref[...])
        up = _dot_nt(h2, wu_ref[...])
        sg = jax.nn.sigmoid(gt)
        sl = gt * sg
        hid = (sl * up).astype(BF16)
        hid_ref[...] = hid
        x3 = x2 + _dot(hid, wd_ref[...])
        r3 = _rms(x3, D_MODEL)
        xh3 = x3 * r3
        diff = xh3 * g3_ref[...] - t_ref[...]
        dy = diff * (1.0 / D_MODEL)
        st_ref[ROW_LOSS:ROW_LOSS + 1, :] += jnp.sum(diff * diff, axis=0, keepdims=True) * (0.5 / D_MODEL)
        st_ref[ROW_FINAL:ROW_FINAL + 1, :] += jnp.sum(dy * xh3, axis=0, keepdims=True)
        dx3 = _rms_bwd(dy, xh3, r3, g3_ref[...], D_MODEL)
        dx3b = dx3.astype(BF16)
        dx3_ref[...] = dx3b
        dhid = _dot_nt(dx3b, wd_ref[...])
        dg = (dhid * up * (sg * (1.0 + gt * (1.0 - sg)))).astype(BF16)
        du = (dhid * sl).astype(BF16)
        dg_ref[...] = dg
        du_ref[...] = du
        dh2 = _dot(dg, wg_ref[...]) + _dot(du, wu_ref[...])
        st_ref[ROW_NORM_FFN:ROW_NORM_FFN + 1, :] += jnp.sum(dh2 * xh2, axis=0, keepdims=True)
        dx2 = dx3 + _rms_bwd(dh2, xh2, r2, g2_ref[...], D_MODEL)
        dx2_ref[...] = dx2
        da = _dot_nt(dx2.astype(BF16), wo_ref[...])
        daf, dam = da[:, 0:FOX_W], da[:, FOX_W:2 * FOX_W]
        st_ref[ROW_OUT:ROW_OUT + 1, 0:FOX_W] += jnp.sum(daf * fhat, axis=0, keepdims=True)
        st_ref[ROW_OUT:ROW_OUT + 1, FOX_W:2 * FOX_W] += jnp.sum(dam * mhat, axis=0, keepdims=True)
        dof_ref[...] = _rms_bwd(daf, fhat, rf, gfo_ref[...], FOX_W).astype(BF16)
        dom_ref[...] = _rms_bwd(dam, mhat, rm, gmo_ref[...], FOX_W).astype(BF16)

    row = lambda w: pl.BlockSpec((tm, w), lambda i: (i, 0))
    ff = jax.ShapeDtypeStruct((T, D_FF), BF16)
    out_shape = (
        jax.ShapeDtypeStruct((T, 1024), BF16), jax.ShapeDtypeStruct((T, 1024), BF16), ff, ff, ff,
        jax.ShapeDtypeStruct((T, 1024), BF16), jax.ShapeDtypeStruct((T, 1024), F32),
        jax.ShapeDtypeStruct((T, 512), BF16), jax.ShapeDtypeStruct((T, 512), BF16), jax.ShapeDtypeStruct((8, 1024), F32),
    )
    return pl.pallas_call(
        body, name="mid", grid=(T // tm,), out_shape=out_shape,
        in_specs=[row(512), row(512), row(1024), row(1024), _full(g_fo.shape), _full(g_mo.shape), _full(g2.shape), _full(g3.shape),
                  _full(w_o.shape), _full(w_g.shape), _full(w_u.shape), _full(w_d.shape)],
        out_specs=[row(1024), row(1024), row(D_FF), row(D_FF), row(D_FF), row(1024), row(1024), row(512), row(512),
                   pl.BlockSpec((8, 1024), lambda i: (0, 0))],
        compiler_params=_params(("arbitrary",)),
    )(of, om, x, tgt, g_fo, g_mo, g2, g3, w_o, w_g, w_u, w_d)


def _in_bwd(dqf, dkf, dvf, dqm, dkm, dvm, lat, x, dx2, g1, gq, gkv, bfg, ct, st, sel_t, w_in, w_q12, w_kv, seq, tm):
    T = x.shape[0]
    nblk = T // tm
    nsb = seq // tm

    def body(dqf_ref, dkf_ref, dvf_ref, dqm_ref, dkm_ref, dvm_ref, lat_ref, x_ref, dx2_ref, g1_ref, gq_ref, gkv_ref, b_ref,
             ct_ref, st_ref, selt_ref, win_ref, wq_ref, wkv_ref, dx_ref, dproj_ref, dq12_ref, dkv_ref, stat_ref, carry):
        i = pl.program_id(0)

        @pl.when(i == 0)
        def _():
            stat_ref[...] = jnp.zeros_like(stat_ref)

        @pl.when(i % nsb == 0)
        def _():
            carry[...] = jnp.zeros_like(carry)

        lane = lax.broadcasted_iota(jnp.int32, (tm, LANES), 1)
        low = lane < 64
        ctv, stv = ct_ref[...], st_ref[...]

        for j in range(PAIRS):
            e, o = 2 * LANES * j, 2 * LANES * j + LANES
            dq = jnp.where(low, dqf_ref[:, e:e + LANES], 0.0) + pltpu.roll(jnp.where(low, dqf_ref[:, o:o + LANES], 0.0), 64, 1)
            dk = jnp.where(low, dkf_ref[:, e:e + LANES], 0.0) + pltpu.roll(jnp.where(low, dkf_ref[:, o:o + LANES], 0.0), 64, 1)
            dproj_ref[:, C_FQ + LANES * j:C_FQ + LANES * (j + 1)] = (dq * FOX_SCALE).astype(BF16)
            dproj_ref[:, C_FK + LANES * j:C_FK + LANES * (j + 1)] = dk.astype(BF16)
        dproj_ref[:, C_FV:C_QL] = dvf_ref[...].astype(BF16)
        dcv = dkf_ref[...] + dqf_ref[...]
        k_hi = dcv.astype(BF16)
        k_lo = (dcv - k_hi.astype(F32)).astype(BF16)
        dc = _dot(k_hi, selt_ref[...]) + _dot(k_lo, selt_ref[...])
        rr = lax.broadcasted_iota(jnp.int32, (tm, tm), 0)
        cc = lax.broadcasted_iota(jnp.int32, (tm, tm), 1)
        triu = (cc >= rr).astype(BF16)
        a0, a1, a2 = _split3(dc)
        dlf = _dot(triu, a0) + _dot(triu, a1) + _dot(triu, a2) + carry[0:1, :]
        carry[0:1, :] = dlf[0:1, :]
        misc_a = lat_ref[:, Q_RANK + KV_RANK:Q_RANK + KV_RANK + LANES]
        z = misc_a + b_ref[...]
        dz = jnp.where(lane < HEADS, dlf * jax.nn.sigmoid(-z), 0.0)
        stat_ref[ROW_B:ROW_B + 1, 0:LANES] += jnp.sum(dz, axis=0, keepdims=True)

        cq = (jnp.where(low, 1.0, 0.0) + ctv) * MLA_SCALE
        sq = stv * MLA_SCALE
        dkpe = jnp.zeros((tm, LANES), F32)
        for hd in range(HEADS):
            s0 = LANES * hd
            dqh = dqm_ref[:, s0:s0 + LANES]
            dq12_ref[:, s0:s0 + LANES] = (dqh * cq).astype(BF16)
            dq12_ref[:, 1024 + s0:1024 + s0 + LANES] = (dqh * sq).astype(BF16)
            dkpe = dkpe + dkm_ref[:, s0:s0 + LANES]
        dkv_ref[:, 0:1024] = dkm_ref[...].astype(BF16)
        dkv_ref[:, 1024:1536] = dvm_ref[...].astype(BF16)
        dproj_ref[:, C_MA:C_END] = (dz + dkpe * ctv + pltpu.roll(dkpe * stv, 32, 1)).astype(BF16)
        dqn = _dot(dq12_ref[...], wq_ref[...])
        dkvn = _dot_nt(dkv_ref[...], wkv_ref[...])
        ql = lat_ref[:, 0:Q_RANK]
        kvl = lat_ref[:, Q_RANK:Q_RANK + KV_RANK]
        rq, rkv = _rms(ql, Q_RANK), _rms(kvl, KV_RANK)
        qhat, kvhat = ql * rq, kvl * rkv
        stat_ref[ROW_Q:ROW_Q + 1, 0:Q_RANK] += jnp.sum(dqn * qhat, axis=0, keepdims=True)
        stat_ref[ROW_KV:ROW_KV + 1, 0:KV_RANK] += jnp.sum(dkvn * kvhat, axis=0, keepdims=True)
        dproj_ref[:, C_QL:C_KVL] = _rms_bwd(dqn, qhat, rq, gq_ref[...], Q_RANK).astype(BF16)
        dproj_ref[:, C_KVL:C_MA] = _rms_bwd(dkvn, kvhat, rkv, gkv_ref[...], KV_RANK).astype(BF16)

        dh1 = _dot(dproj_ref[...], win_ref[...])
        xv = x_ref[...]
        r1 = _rms(xv, D_MODEL)
        xh = xv * r1
        stat_ref[ROW_NORM_MIX:ROW_NORM_MIX + 1, :] += jnp.sum(dh1 * xh, axis=0, keepdims=True)
        dx_ref[...] = dx2_ref[...] + _rms_bwd(dh1, xh, r1, g1_ref[...], D_MODEL)

    rev = lambda w: pl.BlockSpec((tm, w), lambda i: (nblk - 1 - i, 0))
    out_shape = (
        jax.ShapeDtypeStruct((T, 1024), F32), jax.ShapeDtypeStruct((T, C_END), BF16), jax.ShapeDtypeStruct((T, 2048), BF16),
        jax.ShapeDtypeStruct((T, 1536), BF16), jax.ShapeDtypeStruct((8, 1024), F32),
    )
    return pl.pallas_call(
        body, name="in_bwd", grid=(nblk,), out_shape=out_shape,
        in_specs=[rev(1024), rev(1024), rev(512), rev(1024), rev(1024), rev(512), rev(512), rev(1024), rev(1024),
                  _full(g1.shape), _full(gq.shape), _full(gkv.shape), _full(bfg.shape), rev(LANES), rev(LANES), _full(sel_t.shape),
                  _full(w_in.shape), _full(w_q12.shape), _full(w_kv.shape)],
        out_specs=[rev(1024), rev(C_END), rev(2048), rev(1536), pl.BlockSpec((8, 1024), lambda i: (0, 0))],
        scratch_shapes=[pltpu.VMEM((8, LANES), F32)],
        compiler_params=_params(("arbitrary",)),
    )(dqf, dkf, dvf, dqm, dkm, dvm, lat, x, dx2, g1, gq, gkv, bfg, ct, st, sel_t, w_in, w_q12, w_kv)


def _wgrad(a, b, tk, tt, name, ex=None):
    T, K = a.shape
    N = b.shape[1]
    n_in, n_out = (len(ex.ins), len(ex.outs)) if ex else (0, 0)
    gk, gt = K // tk, T // tt

    def body(*refs):
        a_ref, b_ref, o_ref = refs[0], refs[1], refs[2 + n_in]
        kb, t = pl.program_id(0), pl.program_id(1)
        if ex:
            ex_refs = (refs[2:2 + n_in], refs[3 + n_in:3 + n_in + n_out], refs[3 + n_in + n_out:])

            @pl.when((kb == 0) & (t == 0))
            def _():
                ex.start(*ex_refs)

        @pl.when(t == 0)
        def _():
            o_ref[...] = jnp.zeros_like(o_ref)

        o_ref[...] += _dot_tn(a_ref[...].astype(BF16), b_ref[...].astype(BF16))

        if ex:
            @pl.when((kb == gk - 1) & (t == gt - 1))
            def _():
                ex.wait(*ex_refs)

    res = pl.pallas_call(
        body, name=name, grid=(gk, gt), out_shape=(jax.ShapeDtypeStruct((K, N), F32),) + tuple(ex.outs if ex else ()),
        in_specs=[pl.BlockSpec((tt, tk), lambda kb, t: (t, kb)), pl.BlockSpec((tt, N), lambda kb, t: (t, 0))] + [ANY] * n_in,
        out_specs=[pl.BlockSpec((tk, N), lambda kb, t: (kb, 0))] + [ANY] * n_out,
        scratch_shapes=ex.sems() if ex else [], input_output_aliases=ex.aliases(2, 1) if ex else {},
        compiler_params=_params(("arbitrary", "arbitrary")),
    )(a, b, *(ex.ins if ex else ()))
    return (res[0], list(res[1:])) if ex else res[0]


def _adam_update(w, g, m, v):
    nm = ADAM_B1 * m + (1.0 - ADAM_B1) * g
    nv = ADAM_B2 * v + (1.0 - ADAM_B2) * (g * g)
    m_hat = nm / (1.0 - ADAM_B1 ** ADAM_STEP)
    v_hat = nv / (1.0 - ADAM_B2 ** ADAM_STEP)
    return -ADAM_LR * (m_hat / (jnp.sqrt(v_hat) + ADAM_EPS) + ADAM_WD * w), nm, nv


def _adamw_small(stats, params):
    k = len(SMALL)

    def body(*refs):
        for t, name in enumerate(SMALL):
            row, c0, width = SMALL_AT[name]
            w_ref, m_ref, v_ref = refs[1 + 3 * t:4 + 3 * t]
            g_ref, d_ref, nm_ref, nv_ref = refs[1 + 3 * k + 4 * t:5 + 3 * k + 4 * t]
            g = refs[0][row:row + 1, c0:c0 + width]
            g_ref[...] = g
            d_ref[...], nm_ref[...], nv_ref[...] = _adam_update(w_ref[...], g, m_ref[...], v_ref[...])

    vm = pl.BlockSpec(memory_space=pltpu.VMEM)
    out_shape = tuple(jax.ShapeDtypeStruct((1, SMALL_AT[name][2]), F32) for name in SMALL for _ in range(4))
    res = pl.pallas_call(body, name="adamw_small", out_shape=out_shape, in_specs=[vm] * (1 + 3 * k), out_specs=tuple([vm] * (4 * k)))(
        stats, *[a for name in SMALL for a in params[name]])
    return {name: tuple(res[4 * t:4 * t + 4]) for t, name in enumerate(SMALL)}


def _adamw(tensors, name):
    n = len(tensors)
    R, C = tensors[0][0].shape
    tr = _row_tile(R)

    def body(*refs):
        for t in range(n):
            w_ref, g_ref, m_ref, v_ref = refs[4 * t:4 * t + 4]
            d_ref, nm_ref, nv_ref = refs[4 * n + 3 * t:4 * n + 3 * t + 3]
            d_ref[...], nm_ref[...], nv_ref[...] = _adam_update(w_ref[...], g_ref[...], m_ref[...], v_ref[...])

    blk = pl.BlockSpec((tr, C), lambda i: (i, 0))
    sh = jax.ShapeDtypeStruct((R, C), F32)
    res = pl.pallas_call(
        body, name=name, grid=(R // tr,), out_shape=(sh,) * (3 * n),
        in_specs=[blk] * (4 * n), out_specs=[blk] * (3 * n),
        compiler_params=_params(("arbitrary",)),
    )(*[a for t in tensors for a in t])
    return [tuple(res[3 * t:3 * t + 3]) for t in range(n)]


def _arrange(win4, wuq_t, wukv):
    dt = win4.dtype
    per = win4.shape[1]
    zh = lambda r: jnp.zeros((HEADS, r, Q_RANK), dt)

    def rows(a, b):
        out = []
        while a < b:
            k, lo = divmod(a, per)
            hi = min(b - per * k, per)
            out.append(win4[k, lo:hi])
            a = per * k + hi
        return out

    kr1, kr2 = rows(1928, 1944), rows(1944, 1960)
    misc = rows(1536, 1544) + [jnp.zeros((56, D_MODEL), dt)] + kr1 + kr2 + kr2 + kr1
    w_in = jnp.concatenate(rows(0, 1536) + rows(1544, 1928) + misc, axis=0)
    wq = wuq_t.reshape(HEADS, 96, Q_RANK)
    q1 = jnp.concatenate([wq, zh(32)], axis=1).reshape(1024, Q_RANK)
    q2 = jnp.concatenate([zh(64), wq[:, 80:96], wq[:, 64:80], zh(32)], axis=1).reshape(1024, Q_RANK)
    wkv = wukv.reshape(KV_RANK, HEADS, 128)
    wk = jnp.concatenate([wkv[:, :, 0:64], jnp.zeros((KV_RANK, HEADS, 64), dt)], axis=2).reshape(KV_RANK, 1024)
    wv = wkv[:, :, 64:128].reshape(KV_RANK, 512)
    return dict(w_in=w_in, w_q12=jnp.concatenate([q1, q2], axis=0), w_k=wk, w_v=wv, w_kv=jnp.concatenate([wk, wv], axis=1))


def _unarrange(g_in, g_q12, g_kv):
    kr1 = g_in[C_MA + 64:C_MA + 80] + g_in[C_MA + 112:C_MA + 128]
    kr2 = g_in[C_MA + 80:C_MA + 96] + g_in[C_MA + 96:C_MA + 112]
    segments = [(0, 1536, g_in, 0), (1536, 1544, g_in, C_MA), (1544, 1928, g_in, 1536), (1928, 1944, kr1, 0), (1944, 1960, kr2, 0)]
    per = 1960 // N_CHIPS

    def rows(a, b):
        return [src[s0 + max(a, lo) - lo:s0 + min(b, hi) - lo] for lo, hi, src, s0 in segments if max(a, lo) < min(b, hi)]

    win_t = jnp.stack([jnp.concatenate(rows(per * k, per * (k + 1)), axis=0) for k in range(N_CHIPS)])
    g1 = g_q12[0:1024].reshape(HEADS, 128, Q_RANK)
    g2 = g_q12[1024:2048].reshape(HEADS, 128, Q_RANK)
    wuq_t = jnp.concatenate([g1[:, 0:64], g1[:, 64:80] + g2[:, 80:96], g1[:, 80:96] + g2[:, 64:80]], axis=1).reshape(768, Q_RANK)
    gk = g_kv[:, 0:1024].reshape(KV_RANK, HEADS, 128)
    gv = g_kv[:, 1024:1536].reshape(KV_RANK, HEADS, 64)
    wukv = jnp.concatenate([gk[:, :, 0:64], gv], axis=2).reshape(KV_RANK, 1024)
    return win_t, wuq_t, wukv


def _selectors():
    sel = np.zeros((384, 1024), np.float32)
    sel_t = np.zeros((1024, LANES), np.float32)
    for h in range(HEADS):
        for piece in range(3):
            sel[LANES * piece + h, LANES * h + 64 + piece] = 1.0
        sel_t[LANES * h + 64, h] = 1.0
    return jnp.asarray(sel, BF16), jnp.asarray(sel_t, BF16)


def _rope_tables(positions):
    inv_freq = 10000.0 ** (-jnp.arange(0, ROPE, 2, dtype=F32) / ROPE)
    n = positions.size
    ang = (positions.reshape(n // 8, 8, 1).astype(F32) * inv_freq[None, None, :]).reshape(n // 8, 8 * (ROPE // 2))
    cos, sin = lax.optimization_barrier((jnp.cos(lax.optimization_barrier(ang)), jnp.sin(lax.optimization_barrier(ang))))
    cos, sin = cos.reshape(n, ROPE // 2), sin.reshape(n, ROPE // 2)
    z64, z32 = jnp.zeros((n, 64), F32), jnp.zeros((n, 32), F32)
    return jnp.concatenate([z64, cos, cos, z32], axis=1), jnp.concatenate([z64, -sin, sin, z32], axis=1)


def _work(name, t):
    return jnp.swapaxes(t[0], 0, 1) if name in TRANSPOSED else t[0]


def _back(name, t):
    return (jnp.swapaxes(t, 0, 1) if name in TRANSPOSED else t)[None]


def kernel(x, positions, norm_mix_g, w_in, b_fgate, q_norm_g, w_uq, kv_norm_g, w_ukv, fox_out_g, mla_out_g, w_o, norm_ffn_g, w_gate, w_up, w_down, final_norm_g, loss_target, m_norm_mix_g, m_w_in, m_b_fgate, m_q_norm_g, m_w_uq, m_kv_norm_g, m_w_ukv, m_fox_out_g, m_mla_out_g, m_w_o, m_norm_ffn_g, m_w_gate, m_w_up, m_w_down, m_final_norm_g, v_norm_mix_g, v_w_in, v_b_fgate, v_q_norm_g, v_w_uq, v_kv_norm_g, v_w_ukv, v_fox_out_g, v_mla_out_g, v_w_o, v_norm_ffn_g, v_w_gate, v_w_up, v_w_down, v_final_norm_g):
    names = ["norm_mix_g", "w_in", "b_fgate", "q_norm_g", "w_uq", "kv_norm_g", "w_ukv", "fox_out_g", "mla_out_g", "w_o",
             "norm_ffn_g", "w_gate", "w_up", "w_down", "final_norm_g"]
    wts = dict(zip(names, [norm_mix_g, w_in, b_fgate, q_norm_g, w_uq, kv_norm_g, w_ukv, fox_out_g, mla_out_g, w_o, norm_ffn_g,
                           w_gate, w_up, w_down, final_norm_g]))
    mom = dict(zip(names, [m_norm_mix_g, m_w_in, m_b_fgate, m_q_norm_g, m_w_uq, m_kv_norm_g, m_w_ukv, m_fox_out_g, m_mla_out_g,
                           m_w_o, m_norm_ffn_g, m_w_gate, m_w_up, m_w_down, m_final_norm_g]))
    var = dict(zip(names, [v_norm_mix_g, v_w_in, v_b_fgate, v_q_norm_g, v_w_uq, v_kv_norm_g, v_w_ukv, v_fox_out_g, v_mla_out_g,
                           v_w_o, v_norm_ffn_g, v_w_gate, v_w_up, v_w_down, v_final_norm_g]))
    shard = {n: _work(n, wts[n]) for n in HEAD3 + FFN4}
    nb, seq, _ = x.shape
    T = nb * seq
    tm, tq = min(ROW_TILE, seq), min(ATTN_TILE, seq)
    tt = min(WGRAD_TILE, T)
    xf = x.reshape(T, D_MODEL)
    tgt = loss_target.reshape(T, D_MODEL)

    mine = [shard[n].astype(BF16) for n in HEAD3]
    win4, wuq4, wukv4 = _run_exchange(_gather_split_exchange(mine), "gather_head")
    a = _arrange(win4, wuq4.reshape(-1, Q_RANK), wukv4.transpose(1, 0, 2).reshape(KV_RANK, -1))
    sel, sel_t = _selectors()
    ct, st = _rope_tables(positions)
    bfg = jnp.concatenate([b_fgate, jnp.zeros((1, LANES - HEADS), F32)], axis=1)
    g1, gq, gkv = norm_mix_g, q_norm_g, kv_norm_g

    h1, qf, kf, vf, qm, km, vm, lat, qn, kvn = _in_proj(xf, g1, a["w_in"], a["w_q12"], a["w_k"], a["w_v"], gq, gkv, bfg, ct, st, sel, seq,
                                                        min(IN_PROJ_TILE, seq))
    tqf = min(ATTN_FWD_TILE, seq)
    of, lse_f, (wo4, wg4) = _attn_fwd(qf, kf, vf, nb, seq, tqf, "fox_fwd", _gather_exchange([shard[n].astype(BF16) for n in FFN4[:2]]))
    om, lse_m, (wu4, wd4) = _attn_fwd(qm, km, vm, nb, seq, tqf, "mla_fwd", _gather_exchange([shard[n].astype(BF16) for n in FFN4[2:]]))
    a_cat, h2, hid, dg, du, dx3, dx2, dof, dom, st_mid = _mid(
        of, om, xf, tgt, fox_out_g, mla_out_g, norm_ffn_g, final_norm_g.reshape(1, D_MODEL),
        wo4.reshape(D_MODEL, D_MODEL), wg4.reshape(D_FF, D_MODEL), wu4.reshape(D_FF, D_MODEL), wd4.reshape(D_FF, D_MODEL), tm)

    slab = lambda g: g.reshape(N_CHIPS, g.shape[0] // N_CHIPS, g.shape[1])
    big = [slab(_wgrad(a_cat, dx2, D_MODEL, tt, "wgrad_o")), slab(_wgrad(dg, h2, D_FF // 2, tt, "wgrad_gate")),
           slab(_wgrad(du, h2, D_FF // 2, tt, "wgrad_up")), slab(_wgrad(hid, dx3, D_FF // 2, tt, "wgrad_down"))]
    dqf, dkf, dvf, got = _attn_bwd(qf, kf, vf, of, dof, lse_f, nb, seq, tq, "fox_bwd", True, _swap_exchange(big))
    sums = [_add_half(g, s) for g, s in zip(big, got)]
    dqm, dkm, dvm, recv = _attn_bwd(qm, km, vm, om, dom, lse_m, nb, seq, tq, "mla_bwd", False, _scatter_exchange(sums))
    halves = [_sum_slabs(g, s, r) for g, s, r in zip(big, got, recv)]
    dx, dproj, dq12, dkv, st_in = _in_bwd(dqf, dkf, dvf, dqm, dkm, dvm, lat, xf, dx2, g1, gq, gkv, bfg, ct, st, sel_t,
                                             a["w_in"], a["w_q12"], a["w_kv"], seq, tm)
    g_in, results = _wgrad(dproj, h1, C_END, tt, "wgrad_in", _both(_join_exchange(halves), _everyone_exchange(st_mid + st_in)))
    gshard = dict(zip(FFN4, results[:4]))
    stats = _sum_devices(results[4])

    gwin_t, gwuq_t, gwukv = _unarrange(g_in, _wgrad(dq12, qn, 2048, tt, "wgrad_uq"), _wgrad(kvn, dkv, KV_RANK, tt, "wgrad_ukv"))
    tail = [gwin_t, slab(gwuq_t), gwukv.reshape(KV_RANK, N_CHIPS, -1).transpose(1, 0, 2)]
    tail_got = _run_exchange(_swap_exchange(tail), "tail_swap")
    tail_sums = [_add_half(g, s) for g, s in zip(tail, tail_got)]
    tail_recv = _run_exchange(_scatter_exchange(tail_sums), "tail_scatter")
    tail_joined = _run_exchange(_join_exchange([_sum_slabs(g, s, r) for g, s, r in zip(tail, tail_got, tail_recv)]), "tail_join")
    gshard.update(zip(HEAD3, tail_joined))
    quad = lambda n: (shard[n], gshard[n], _work(n, mom[n]), _work(n, var[n]))
    updates = dict(zip(FFN4[1:], _adamw([quad(n) for n in FFN4[1:]], "adamw_ffn")))
    for n in HEAD3 + FFN4[:1]:
        updates[n], = _adamw([quad(n)], "adamw_" + n)

    grads, delta, new_m, new_v = {}, {}, {}, {}
    for n in HEAD3 + FFN4:
        grads[n] = _back(n, gshard[n])
        delta[n], new_m[n], new_v[n] = [_back(n, t) for t in updates[n]]
    row = lambda t: t.reshape(1, -1)
    small = _adamw_small(stats, {n: (row(wts[n]), row(mom[n]), row(var[n])) for n in SMALL})
    for n in SMALL:
        grads[n], delta[n], new_m[n], new_v[n] = [t.reshape(wts[n].shape) for t in small[n]]
    loss = jnp.sum(stats[ROW_LOSS])
    return (loss, dx.reshape(x.shape), *[grads[n] for n in names], *[delta[n] for n in names],
            *[new_m[n] for n in names], *[new_v[n] for n in names])
```

```python
import functools

import numpy as np
import jax
import jax.numpy as jnp
from jax import lax
from jax.experimental import pallas as pl
from jax.experimental.pallas import tpu as pltpu

F32 = jnp.float32
BF16 = jnp.bfloat16
MESH = pl.DeviceIdType.MESH

EPS = 1e-6
D_MODEL = 1024
HEADS = 8
PAIRS = HEADS // 2
FOX_W = 512
Q_RANK = 256
KV_RANK = 128
ROPE = 32
D_FF = 2816
N_CHIPS = 4
FOX_SCALE = 64 ** -0.5
MLA_SCALE = 96 ** -0.5
LANES = 128
NEG = -1e30

ADAM_LR, ADAM_B1, ADAM_B2, ADAM_EPS, ADAM_WD, ADAM_STEP = 0.001, 0.9, 0.999, 1e-08, 0.01, 10

C_FQ, C_FK, C_FV, C_QL, C_KVL, C_MA, C_END = 0, 512, 1024, 1536, 1792, 1920, 2048
C_MB = C_END

VMEM_LIMIT = 60 * 1024 * 1024
ROW_TILE = 256
IN_PROJ_TILE = 512
ATTN_TILE = 512
ATTN_FWD_TILE = 1024
WGRAD_TILE = 2048

HEAD3 = ("w_in", "w_uq", "w_ukv")
FFN4 = ("w_o", "w_gate", "w_up", "w_down")
TRANSPOSED = ("w_in", "w_uq", "w_gate", "w_up")
SMALL = ("norm_mix_g", "b_fgate", "q_norm_g", "kv_norm_g", "fox_out_g", "mla_out_g", "norm_ffn_g", "final_norm_g")
ROW_NORM_MIX, ROW_NORM_FFN, ROW_FINAL, ROW_OUT, ROW_Q, ROW_KV, ROW_B, ROW_LOSS = range(8)
SMALL_AT = {"norm_mix_g": (ROW_NORM_MIX, 0, 1024), "norm_ffn_g": (ROW_NORM_FFN, 0, 1024), "final_norm_g": (ROW_FINAL, 0, 1024),
            "fox_out_g": (ROW_OUT, 0, 512), "mla_out_g": (ROW_OUT, 512, 512), "q_norm_g": (ROW_Q, 0, 256),
            "kv_norm_g": (ROW_KV, 0, 128), "b_fgate": (ROW_B, 0, 8)}


def _params(sem=None):
    return pltpu.CompilerParams(dimension_semantics=sem, vmem_limit_bytes=VMEM_LIMIT)


def _full(shape):
    n = len(shape)
    return pl.BlockSpec(shape, lambda *_: (0,) * n, pipeline_mode=pl.Buffered(1))


def _dot(a, b):
    return jnp.dot(a, b, preferred_element_type=F32)


def _dot_nt(a, b):
    return lax.dot_general(a, b, (((1,), (1,)), ((), ())), preferred_element_type=F32)


def _dot_tn(a, b):
    return lax.dot_general(a, b, (((0,), (0,)), ((), ())), preferred_element_type=F32)


def _split3(v):
    hi = v.astype(BF16)
    r1 = v - hi.astype(F32)
    mid = r1.astype(BF16)
    lo = (r1 - mid.astype(F32)).astype(BF16)
    return hi, mid, lo


def _rms(v, width):
    return lax.rsqrt(jnp.sum(v * v, axis=1, keepdims=True) * (1.0 / width) + EPS)


def _rms_bwd(dy, xhat, r, g, width):
    u = dy * g
    return r * (u - xhat * (jnp.sum(u * xhat, axis=1, keepdims=True) * (1.0 / width)))


ANY = pl.BlockSpec(memory_space=pl.ANY)


def _place():
    return lax.axis_index("x"), lax.axis_index("y"), lax.axis_index("c")


def _other_chips(x, y):
    return [(1 - x, y), (x, 1 - y), (1 - x, 1 - y)]


def _remote(src, dst, send, recv, j, dev):
    return pltpu.make_async_remote_copy(src_ref=src, dst_ref=dst, send_sem=send.at[j], recv_sem=recv.at[j], device_id=dev, device_id_type=MESH)


class _Exchange:
    def __init__(self, ins, outs, n_remote, n_local, build, in_place=False):
        self.ins, self.outs, self.n_remote, self.n_local, self.build = list(ins), list(outs), n_remote, max(n_local, 1), build
        self.in_place = in_place
        self.n_aliased = len(self.ins)

    def aliases(self, first_in, first_out):
        return {first_in + i: first_out + i for i in range(self.n_aliased)} if self.in_place else {}

    def sems(self):
        return [pltpu.SemaphoreType.DMA((self.n_remote,)), pltpu.SemaphoreType.DMA((self.n_remote,)), pltpu.SemaphoreType.DMA((self.n_local,))]

    def start(self, in_refs, out_refs, sems):
        for cp in self.build(in_refs, out_refs, *sems)[0]:
            cp.start()

    def wait(self, in_refs, out_refs, sems):
        for w in self.build(in_refs, out_refs, *sems)[1]:
            w()


def _gather_exchange(shards):
    def build(ins, outs, send, recv, lsem):
        x, y, c = _place()
        starts, waits = [], []
        for i, (s, o) in enumerate(zip(ins, outs)):
            mine = pltpu.make_async_copy(s, o.at[2 * x + y], lsem.at[i])
            starts.append(mine)
            waits.append(mine.wait)
            for j, (cx, cy) in enumerate(_other_chips(x, y)):
                out = _remote(s, o.at[2 * x + y], send, recv, 3 * i + j, (cx, cy, c))
                starts.append(out)
                waits.append(_remote(s, o.at[2 * cx + cy], send, recv, 3 * i + j, (cx, cy, c)).wait_recv)
                waits.append(out.wait_send)
        return starts, waits

    outs = [jax.ShapeDtypeStruct((N_CHIPS,) + s.shape, s.dtype) for s in shards]
    return _Exchange(shards, outs, 3 * len(shards), len(shards), build)


def _gather_split_exchange(shards):
    n = len(shards)

    def build(ins, outs, send, recv, lsem):
        x, y, c = _place()
        starts, waits, last = [], [], []
        for i, (s, o) in enumerate(zip(ins, outs)):
            hc = s.shape[1] // 2
            mine, other = pl.ds(c * hc, hc), pl.ds((1 - c) * hc, hc)
            for j, (cx, cy) in enumerate(_other_chips(x, y)):
                out = _remote(s.at[:, mine], o.at[2 * x + y, :, mine], send, recv, 3 * i + j, (cx, cy, c))
                landed = o.at[2 * cx + cy, :, mine]
                arrive = _remote(s.at[:, mine], landed, send, recv, 3 * i + j, (cx, cy, c))
                onward = _remote(landed, landed, send, recv, 3 * n + 3 * i + j, (x, y, 1 - c))
                from_sibling = _remote(landed, o.at[2 * cx + cy, :, other], send, recv, 3 * n + 3 * i + j, (x, y, 1 - c))
                starts.append(out)
                waits.append(lambda arrive=arrive, onward=onward: (arrive.wait_recv(), onward.start()))
                last += [from_sibling.wait_recv, onward.wait_send, out.wait_send]
        return starts, waits + last

    outs = [jax.ShapeDtypeStruct((N_CHIPS,) + s.shape, s.dtype) for s in shards]
    return _Exchange(shards, outs, 6 * n, 0, build)


def _swap_exchange(grads):
    def build(ins, outs, send, recv, lsem):
        x, y, c = _place()
        cps = []
        for i, (g, o) in enumerate(zip(ins, outs)):
            hc = g.shape[2] // 2
            cps.append(_remote(g.at[:, :, pl.ds((1 - c) * hc, hc)], o, send, recv, i, (x, y, 1 - c)))
        return cps, [cp.wait for cp in cps]

    outs = [jax.ShapeDtypeStruct((g.shape[0], g.shape[1], g.shape[2] // 2), g.dtype) for g in grads]
    return _Exchange(grads, outs, len(grads), 0, build)


def _scatter_exchange(sums):
    def build(ins, outs, send, recv, lsem):
        x, y, c = _place()
        cps = []
        for i, (s, o) in enumerate(zip(ins, outs)):
            for j, (cx, cy) in enumerate(_other_chips(x, y)):
                cps.append(_remote(s.at[2 * cx + cy], o.at[j], send, recv, 3 * i + j, (cx, cy, c)))
        return cps, [cp.wait for cp in cps]

    outs = [jax.ShapeDtypeStruct((3,) + s.shape[1:], s.dtype) for s in sums]
    return _Exchange(sums, outs, 3 * len(sums), 0, build)


def _join_exchange(bufs):
    def build(ins, outs, send, recv, lsem):
        x, y, c = _place()
        starts, waits = [], []
        for i, (t, o) in enumerate(zip(ins, outs)):
            hc = t.shape[1] // 2
            out = _remote(t.at[:, pl.ds(c * hc, hc)], o.at[:, pl.ds(c * hc, hc)], send, recv, i, (x, y, 1 - c))
            starts.append(out)
            waits += [_remote(t.at[:, pl.ds(c * hc, hc)], o.at[:, pl.ds((1 - c) * hc, hc)], send, recv, i, (x, y, 1 - c)).wait_recv,
                      out.wait_send]
        return starts, waits

    outs = [jax.ShapeDtypeStruct(t.shape, t.dtype) for t in bufs]
    return _Exchange(bufs, outs, len(bufs), 0, build, in_place=True)


def _everyone_exchange(v):
    def build(ins, outs, send, recv, lsem):
        x, y, c = _place()
        me = 4 * x + 2 * y + c
        mine = pltpu.make_async_copy(ins[0], outs[0].at[me], lsem.at[0])
        starts, waits = [mine], [mine.wait]
        for j in range(7):
            fx, fy, fc = (j + 1) >> 2 & 1, (j + 1) >> 1 & 1, (j + 1) & 1
            peer = (x ^ fx, y ^ fy, c ^ fc)
            out = _remote(ins[0], outs[0].at[me], send, recv, j, peer)
            starts.append(out)
            waits += [_remote(ins[0], outs[0].at[4 * peer[0] + 2 * peer[1] + peer[2]], send, recv, j, peer).wait_recv, out.wait_send]
        return starts, waits

    return _Exchange([v], [jax.ShapeDtypeStruct((8,) + v.shape, v.dtype)], 7, 1, build)


def _both(a, b):
    na_in, na_out = len(a.ins), len(a.outs)

    def build(ins, outs, send, recv, lsem):
        sa, wa = a.build(ins[:na_in], outs[:na_out], send.at[pl.ds(0, a.n_remote)], recv.at[pl.ds(0, a.n_remote)],
                         lsem.at[pl.ds(0, a.n_local)])
        sb, wb = b.build(ins[na_in:], outs[na_out:], send.at[pl.ds(a.n_remote, b.n_remote)], recv.at[pl.ds(a.n_remote, b.n_remote)],
                         lsem.at[pl.ds(a.n_local, b.n_local)])
        return sa + sb, wa + wb

    both = _Exchange(a.ins + b.ins, a.outs + b.outs, a.n_remote + b.n_remote, a.n_local + b.n_local, build, in_place=a.in_place)
    both.n_aliased = na_in
    return both


def _run_exchange(ex, name):
    n_in, n_out = len(ex.ins), len(ex.outs)

    def body(*refs):
        ins, outs, sems = refs[:n_in], refs[n_in:n_in + n_out], refs[n_in + n_out:]
        ex.start(ins, outs, sems)
        ex.wait(ins, outs, sems)

    return pl.pallas_call(
        body, name=name, out_shape=tuple(ex.outs), in_specs=[ANY] * n_in, out_specs=tuple([ANY] * n_out),
        scratch_shapes=ex.sems(), input_output_aliases=ex.aliases(0, 0),
        compiler_params=pltpu.CompilerParams(has_side_effects=True),
    )(*ex.ins)


def _sum_devices(rows):
    def body(r_ref, o_ref):
        acc = r_ref[0]
        for d in range(1, 8):
            acc = acc + r_ref[d]
        o_ref[...] = acc

    vm = pl.BlockSpec(memory_space=pltpu.VMEM)
    return pl.pallas_call(body, name="sum_devices", out_shape=jax.ShapeDtypeStruct(rows.shape[1:], rows.dtype),
                          in_specs=[vm], out_specs=vm)(rows)


def _add_half(g, got):
    n, R, C = g.shape
    hc = C // 2

    def body(c_ref, g_ref, r_ref, o_ref):
        o_ref[...] = (g_ref[...] + r_ref[...]).astype(BF16)

    c = lax.axis_index("c")
    return pl.pallas_call(
        body, name="add_half",
        grid_spec=pltpu.PrefetchScalarGridSpec(
            num_scalar_prefetch=1, grid=(n,),
            in_specs=[pl.BlockSpec((1, R, hc), lambda k, c_ref: (k, 0, c_ref[0])),
                      pl.BlockSpec((1, R, hc), lambda k, c_ref: (k, 0, 0))],
            out_specs=pl.BlockSpec((1, R, hc), lambda k, c_ref: (k, 0, 0))),
        out_shape=jax.ShapeDtypeStruct((n, R, hc), BF16),
        compiler_params=_params(("arbitrary",)),
    )(jnp.reshape(c, (1,)).astype(jnp.int32), g, got)


def _sum_slabs(g, got, recv):
    _, R, C = g.shape
    hc = C // 2

    def body(kc_ref, g_ref, s_ref, r_ref, o_ref):
        o_ref[...] = (((g_ref[0] + s_ref[0]) + r_ref[0].astype(F32)) + r_ref[1].astype(F32)) + r_ref[2].astype(F32)

    kc = jnp.stack([2 * lax.axis_index("x") + lax.axis_index("y"), lax.axis_index("c")]).astype(jnp.int32)
    return pl.pallas_call(
        body, name="sum_slabs",
        grid_spec=pltpu.PrefetchScalarGridSpec(
            num_scalar_prefetch=1, grid=(1,),
            in_specs=[pl.BlockSpec((1, R, hc), lambda i, kc_ref: (kc_ref[0], 0, kc_ref[1])),
                      pl.BlockSpec((1, R, hc), lambda i, kc_ref: (kc_ref[0], 0, 0)),
                      pl.BlockSpec((3, R, hc), lambda i, kc_ref: (0, 0, 0))],
            out_specs=pl.BlockSpec((R, hc), lambda i, kc_ref: (0, kc_ref[1]))),
        out_shape=jax.ShapeDtypeStruct((R, C), F32),
        compiler_params=_params(("arbitrary",)),
    )(kc, g, got, recv)


def _row_tile(rows):
    for cand in (256, 184, 176, 144, 128, 64, 32, 16, 8):
        if rows % cand == 0:
            return cand
    return rows


def _in_proj(x, g1, w_in, w_q12, w_k, w_v, gq, gkv, bfg, ct, st, sel, seq, tm):
    T = x.shape[0]
    nsb = seq // tm

    def body(x_ref, g1_ref, win_ref, wq_ref, wk_ref, wv_ref, gq_ref, gkv_ref, b_ref, ct_ref, st_ref, sel_ref,
             h1_ref, qf_ref, kf_ref, vf_ref, qm_ref, km_ref, vm_ref, lat_ref, qn_ref, kvn_ref, carry):
        i = pl.program_id(0)

        @pl.when(i % nsb == 0)
        def _():
            carry[...] = jnp.zeros_like(carry)

        xv = x_ref[...]
        h = (xv * _rms(xv, D_MODEL) * g1_ref[...]).astype(BF16)
        h1_ref[...] = h
        proj = _dot_nt(h, win_ref[...])
        lane = lax.broadcasted_iota(jnp.int32, (tm, LANES), 1)
        low = lane < 64
        misc_a = proj[:, C_MA:C_END]
        misc_b = pltpu.roll(misc_a, 96, 1)

        z = misc_a + b_ref[...]
        lf = jnp.where(lane < HEADS, jnp.minimum(z, 0.0) - jnp.log1p(jnp.exp(-jnp.abs(z))), 0.0)
        rr = lax.broadcasted_iota(jnp.int32, (tm, tm), 0)
        cc = lax.broadcasted_iota(jnp.int32, (tm, tm), 1)
        tri = (rr >= cc).astype(BF16)
        a0, a1, a2 = _split3(lf)
        c = _dot(tri, a0) + _dot(tri, a1) + _dot(tri, a2) + carry[0:1, :]
        carry[0:1, :] = c[tm - 1:tm, :]
        c0, c1, c2 = _split3(c)
        cpl = _dot(jnp.concatenate([c0, c1, c2], axis=1), sel_ref[...])
        qpad = jnp.where((lane >= 64) & (lane < 67), -1.0, 0.0)
        for j in range(PAIRS):
            qc = proj[:, C_FQ + LANES * j:C_FQ + LANES * (j + 1)] * FOX_SCALE
            kc = proj[:, C_FK + LANES * j:C_FK + LANES * (j + 1)]
            e, o = 2 * LANES * j, 2 * LANES * j + LANES
            qf_ref[:, e:e + LANES] = jnp.where(low, qc, qpad).astype(BF16)
            qf_ref[:, o:o + LANES] = jnp.where(low, pltpu.roll(qc, 64, 1), qpad).astype(BF16)
            kf_ref[:, e:e + LANES] = jnp.where(low, kc, cpl[:, e:e + LANES]).astype(BF16)
            kf_ref[:, o:o + LANES] = jnp.where(low, pltpu.roll(kc, 64, 1), cpl[:, o:o + LANES]).astype(BF16)
        vf_ref[...] = proj[:, C_FV:C_QL].astype(BF16)

        ql = proj[:, C_QL:C_KVL]
        kvl = proj[:, C_KVL:C_MA]
        qn = (ql * _rms(ql, Q_RANK) * gq_ref[...]).astype(BF16)
        kvn = (kvl * _rms(kvl, KV_RANK) * gkv_ref[...]).astype(BF16)
        lat_ref[...] = proj[:, C_QL:C_MB]
        qn_ref[...] = qn
        kvn_ref[...] = kvn
        q12 = _dot_nt(qn, wq_ref[...])
        kn = _dot(kvn, wk_ref[...])
        ctv = ct_ref[...]
        stv = st_ref[...]
        cq = (jnp.where(low, 1.0, 0.0) + ctv) * MLA_SCALE
        sq = stv * MLA_SCALE
        kpe = misc_a * ctv + misc_b * stv
        for hd in range(HEADS):
            s0 = LANES * hd
            qm_ref[:, s0:s0 + LANES] = (q12[:, s0:s0 + LANES] * cq + q12[:, 1024 + s0:1024 + s0 + LANES] * sq).astype(BF16)
            km_ref[:, s0:s0 + LANES] = (kn[:, s0:s0 + LANES] + kpe).astype(BF16)
        vm_ref[...] = _dot(kvn, wv_ref[...]).astype(BF16)

    row = lambda w: pl.BlockSpec((tm, w), lambda i: (i, 0))
    out_shape = (
        jax.ShapeDtypeStruct((T, D_MODEL), BF16),
        jax.ShapeDtypeStruct((T, 1024), BF16), jax.ShapeDtypeStruct((T, 1024), BF16), jax.ShapeDtypeStruct((T, 512), BF16),
        jax.ShapeDtypeStruct((T, 1024), BF16), jax.ShapeDtypeStruct((T, 1024), BF16), jax.ShapeDtypeStruct((T, 512), BF16),
        jax.ShapeDtypeStruct((T, 512), F32),
        jax.ShapeDtypeStruct((T, Q_RANK), BF16), jax.ShapeDtypeStruct((T, KV_RANK), BF16),
    )
    return pl.pallas_call(
        body, name="in_proj", grid=(T // tm,), out_shape=out_shape,
        in_specs=[row(D_MODEL), _full(g1.shape), _full(w_in.shape), _full(w_q12.shape), _full(w_k.shape), _full(w_v.shape),
                  _full(gq.shape), _full(gkv.shape), _full(bfg.shape), row(LANES), row(LANES), _full(sel.shape)],
        out_specs=[row(D_MODEL), row(1024), row(1024), row(512), row(1024), row(1024), row(512), row(512), row(Q_RANK), row(KV_RANK)],
        scratch_shapes=[pltpu.VMEM((8, LANES), F32)],
        compiler_params=_params(("arbitrary",)),
    )(x, g1, w_in, w_q12, w_k, w_v, gq, gkv, bfg, ct, st, sel)


def _attn_fwd(q, k, v, nb, seq, tq, name, ex=None):
    T = q.shape[0]
    nq = seq // tq
    n_in, n_out = (len(ex.ins), len(ex.outs)) if ex else (0, 0)

    def body(*refs):
        q_ref, k_ref, v_ref = refs[0:3]
        o_ref, lse_ref = refs[3 + n_in:5 + n_in]
        b, pr, qi = pl.program_id(0), pl.program_id(1), pl.program_id(2)
        if ex:
            ex_refs = (refs[3:3 + n_in], refs[5 + n_in:5 + n_in + n_out], refs[8 + n_in + n_out:])

            @pl.when((b == 0) & (pr == 0) & (qi == 0))
            def _():
                ex.start(*ex_refs)

        s_sc, p_sc, acc_sc = refs[5 + n_in + n_out:8 + n_in + n_out]
        strip = 64
        key_s = lax.broadcasted_iota(jnp.int32, (strip, tq), 0)
        qry_s = lax.broadcasted_iota(jnp.int32, (strip, tq), 1)
        row_t = lax.broadcasted_iota(jnp.int32, (LANES, tq), 0)
        acc_sc[...] = jnp.zeros(acc_sc.shape, F32)

        def fold(x, op):
            out = x[0:8]
            for r in range(8, strip, 8):
                out = op(out, x[r:r + 8])
            return out

        def step(kj, state, masked):
            rows = pl.ds(pl.multiple_of(kj * tq, tq), tq)
            for hh in range(2):
                s_sc[hh] = _dot_nt(k_ref[rows, LANES * hh:LANES * (hh + 1)], q_ref[:, LANES * hh:LANES * (hh + 1)])
            vv = v_ref[rows, :]
            new = []
            for hh in range(2):
                m, l = state[hh]

                def strip_of(r0, hh=hh):
                    s = s_sc[hh, r0:r0 + strip, :]
                    return jnp.where(key_s + r0 <= qry_s, s, NEG) if masked else s

                mx = fold(strip_of(0), jnp.maximum)
                for r0 in range(strip, tq, strip):
                    mx = jnp.maximum(mx, fold(strip_of(r0), jnp.maximum))
                m_new = jnp.maximum(m, jnp.max(mx, axis=0, keepdims=True))
                alpha = jnp.exp(m - m_new)
                sm = jnp.zeros((8, tq), F32)
                for r0 in range(0, tq, strip):
                    p = jnp.exp(strip_of(r0) - m_new)
                    sm = sm + fold(p, jnp.add)
                    p_sc[hh, r0:r0 + strip, :] = p.astype(BF16)
                l = alpha * l + jnp.sum(sm, axis=0, keepdims=True)
                acc_sc[hh] = alpha * acc_sc[hh] + _dot_tn(vv, p_sc[hh])
                new.append((m_new, l))
            return tuple(new)

        one = (jnp.full((1, tq), NEG, F32), jnp.zeros((1, tq), F32))
        state = lax.fori_loop(0, qi, functools.partial(step, masked=False), (one, one))
        (m0, l0), (m1, l1) = step(qi, state, True)
        o_ref[...] = jnp.where(row_t < 64, acc_sc[0] / l0, acc_sc[1] / l1).T
        lse_ref[:, 0:LANES] = jnp.broadcast_to(m0 + jnp.log(l0), (LANES, tq)).T
        lse_ref[:, LANES:2 * LANES] = jnp.broadcast_to(m1 + jnp.log(l1), (LANES, tq)).T

        if ex:
            @pl.when((b == nb - 1) & (pr == PAIRS - 1) & (qi == nq - 1))
            def _():
                ex.wait(*ex_refs)

    res = pl.pallas_call(
        body, name=name, grid=(nb, PAIRS, nq),
        out_shape=(jax.ShapeDtypeStruct((T, 512), F32), jax.ShapeDtypeStruct((T, 1024), F32)) + tuple(ex.outs if ex else ()),
        in_specs=[pl.BlockSpec((tq, 2 * LANES), lambda b, p, i: (b * nq + i, p)),
                  pl.BlockSpec((seq, 2 * LANES), lambda b, p, i: (b, p)),
                  pl.BlockSpec((seq, LANES), lambda b, p, i: (b, p))] + [ANY] * n_in,
        out_specs=[pl.BlockSpec((tq, LANES), lambda b, p, i: (b * nq + i, p)),
                   pl.BlockSpec((tq, 2 * LANES), lambda b, p, i: (b * nq + i, p))] + [ANY] * n_out,
        scratch_shapes=[pltpu.VMEM((2, tq, tq), F32), pltpu.VMEM((2, tq, tq), BF16), pltpu.VMEM((2, LANES, tq), F32)]
        + (ex.sems() if ex else []),
        compiler_params=_params(("arbitrary", "arbitrary", "arbitrary")),
    )(q, k, v, *(ex.ins if ex else ()))
    return res[0], res[1], list(res[2:])


def _attn_bwd(q, k, v, o, do, lse, nb, seq, tq, name, key_bias, ex=None):
    T = q.shape[0]
    nq = seq // tq
    n_in, n_out = (len(ex.ins), len(ex.outs)) if ex else (0, 0)
    qk_dt = F32 if key_bias else BF16

    def body(*refs):
        q_ref, k_ref, v_ref, o_ref, do_ref, lse_ref = refs[0:6]
        dq_ref, dk_ref, dv_ref = refs[6 + n_in:9 + n_in]
        dsc, rsum, dq_acc = refs[9 + n_in + n_out:12 + n_in + n_out]
        b, pr, step_no = pl.program_id(0), pl.program_id(1), pl.program_id(2)
        kj = nq - 1 - step_no
        if ex:
            ex_refs = (refs[6:6 + n_in], refs[9 + n_in:9 + n_in + n_out], refs[12 + n_in + n_out:])

            @pl.when((b == 0) & (pr == 0) & (step_no == 0))
            def _():
                ex.start(*ex_refs)

        lane_s = lax.broadcasted_iota(jnp.int32, (seq, LANES), 1)
        lane = lax.broadcasted_iota(jnp.int32, (tq, LANES), 1)
        rr = lax.broadcasted_iota(jnp.int32, (tq, tq), 0)
        cc = lax.broadcasted_iota(jnp.int32, (tq, tq), 1)

        @pl.when(step_no == 0)
        def _():
            dq_acc[...] = jnp.zeros_like(dq_acc)
            prod = do_ref[...].astype(F32) * o_ref[...]
            d0 = jnp.sum(jnp.where(lane_s < 64, prod, 0.0), axis=1, keepdims=True)
            d1 = jnp.sum(jnp.where(lane_s < 64, 0.0, prod), axis=1, keepdims=True)
            dsc[0] = jnp.broadcast_to(d0, (seq, LANES))
            dsc[1] = jnp.broadcast_to(d1, (seq, LANES))
            if key_bias:
                rsum[...] = jnp.zeros_like(rsum)

        vv = v_ref[...]

        def step(qi, carry, masked):
            dkt, dvt, cols = carry
            rows = pl.ds(pl.multiple_of(qi * tq, tq), tq)
            dov = do_ref[rows, :]
            new_dkt, new_cols = [], []
            for hh in range(2):
                qv = q_ref[rows, LANES * hh:LANES * (hh + 1)]
                kv = k_ref[:, LANES * hh:LANES * (hh + 1)]
                dom = jnp.where((lane < 64) if hh == 0 else (lane >= 64), dov, jnp.zeros((), BF16))
                s = _dot_nt(qv, kv)
                if masked:
                    s = jnp.where(cc <= rr, s, NEG)
                p = jnp.exp(s - jnp.tile(lse_ref[rows, LANES * hh:LANES * (hh + 1)], (1, tq // LANES)))
                dp = _dot_nt(dom, vv)
                ds32 = p * (dp - jnp.tile(dsc[hh, rows, :], (1, tq // LANES)))
                col = cols[hh]
                if key_bias:
                    col = col + jnp.sum(ds32, axis=0, keepdims=True)
                    rsum[hh, rows, :] += jnp.broadcast_to(jnp.sum(ds32, axis=1, keepdims=True), (tq, LANES))
                ds = ds32.astype(BF16)
                dvt = dvt + _dot_tn(dom, p.astype(BF16))
                new_dkt.append(dkt[hh] + _dot_tn(qv, ds))
                new_cols.append(col)
                dq_acc[rows, LANES * hh:LANES * (hh + 1)] += _dot(ds, kv)
            return tuple(new_dkt), dvt, tuple(new_cols)

        zt = jnp.zeros((LANES, tq), F32)
        zc = jnp.zeros((1, tq), F32)
        carry = step(kj, ((zt, zt), zt, (zc, zc)), True)
        dkt, dvt, cols = lax.fori_loop(kj + 1, nq, functools.partial(step, masked=False), carry)
        row_t = lax.broadcasted_iota(jnp.int32, (LANES, tq), 0)
        for hh in range(2):
            dk_h = jnp.where(row_t == 64, -cols[hh], dkt[hh]) if key_bias else dkt[hh]
            dk_ref[:, LANES * hh:LANES * (hh + 1)] = dk_h.T.astype(dk_ref.dtype)
        dv_ref[...] = dvt.T.astype(dv_ref.dtype)

        @pl.when(step_no == nq - 1)
        def _():
            for hh in range(2):
                blk = dq_acc[:, LANES * hh:LANES * (hh + 1)]
                if key_bias:
                    blk = jnp.where(lane_s == 64, rsum[hh], blk)
                dq_ref[:, LANES * hh:LANES * (hh + 1)] = blk.astype(dq_ref.dtype)

        if ex:
            @pl.when((b == nb - 1) & (pr == PAIRS - 1) & (step_no == nq - 1))
            def _():
                ex.wait(*ex_refs)

    per_seq = lambda w: pl.BlockSpec((seq, w), lambda b, p, j: (b, p))
    per_blk = lambda w: pl.BlockSpec((tq, w), lambda b, p, j: (b * nq + nq - 1 - j, p))
    res = pl.pallas_call(
        body, name=name, grid=(nb, PAIRS, nq),
        out_shape=(jax.ShapeDtypeStruct((T, 1024), qk_dt), jax.ShapeDtypeStruct((T, 1024), qk_dt), jax.ShapeDtypeStruct((T, 512), BF16))
        + tuple(ex.outs if ex else ()),
        in_specs=[per_seq(2 * LANES), per_blk(2 * LANES), per_blk(LANES), per_seq(LANES), per_seq(LANES), per_seq(2 * LANES)] + [ANY] * n_in,
        out_specs=[per_seq(2 * LANES), per_blk(2 * LANES), per_blk(LANES)] + [ANY] * n_out,
        scratch_shapes=[pltpu.VMEM((2, seq, LANES), F32), pltpu.VMEM((2, seq, LANES) if key_bias else (2, 8, LANES), F32),
                        pltpu.VMEM((seq, 2 * LANES), F32)]
        + (ex.sems() if ex else []),
        compiler_params=_params(("arbitrary", "arbitrary", "arbitrary")),
    )(q, k, v, o, do, lse, *(ex.ins if ex else ()))
    return res[0], res[1], res[2], list(res[3:])


def _mid(of, om, x, tgt, g_fo, g_mo, g2, g3, w_o, w_g, w_u, w_d, tm):
    T = x.shape[0]

    def body(of_ref, om_ref, x_ref, t_ref, gfo_ref, gmo_ref, g2_ref, g3_ref, wo_ref, wg_ref, wu_ref, wd_ref,
             a_ref, h2_ref, hid_ref, dg_ref, du_ref, dx3_ref, dx2_ref, dof_ref, dom_ref, st_ref):
        i = pl.program_id(0)

        @pl.when(i == 0)
        def _():
            st_ref[...] = jnp.zeros_like(st_ref)

        ofv, omv = of_ref[...], om_ref[...]
        rf, rm = _rms(ofv, FOX_W), _rms(omv, FOX_W)
        fhat, mhat = ofv * rf, omv * rm
        a = jnp.concatenate([fhat * gfo_ref[...], mhat * gmo_ref[...]], axis=1).astype(BF16)
        a_ref[...] = a
        x2 = x_ref[...] + _dot(a, wo_ref[...])
        r2 = _rms(x2, D_MODEL)
        xh2 = x2 * r2
        h2 = (xh2 * g2_ref[...]).astype(BF16)
        h2_ref[...] = h2
        gt = _dot_nt(h2, wg_ref[...])
        up = _dot_nt(h2, wu_ref[...])
        sg = jax.nn.sigmoid(gt)
        sl = gt * sg
        hid = (sl * up).astype(BF16)
        hid_ref[...] = hid
        x3 = x2 + _dot(hid, wd_ref[...])
        r3 = _rms(x3, D_MODEL)
        xh3 = x3 * r3
        diff = xh3 * g3_ref[...] - t_ref[...]
        dy = diff * (1.0 / D_MODEL)
        st_ref[ROW_LOSS:ROW_LOSS + 1, :] += jnp.sum(diff * diff, axis=0, keepdims=True) * (0.5 / D_MODEL)
        st_ref[ROW_FINAL:ROW_FINAL + 1, :] += jnp.sum(dy * xh3, axis=0, keepdims=True)
        dx3 = _rms_bwd(dy, xh3, r3, g3_ref[...], D_MODEL)
        dx3b = dx3.astype(BF16)
        dx3_ref[...] = dx3b
        dhid = _dot_nt(dx3b, wd_ref[...])
        dg = (dhid * up * (sg * (1.0 + gt * (1.0 - sg)))).astype(BF16)
        du = (dhid * sl).astype(BF16)
        dg_ref[...] = dg
        du_ref[...] = du
        dh2 = _dot(dg, wg_ref[...]) + _dot(du, wu_ref[...])
        st_ref[ROW_NORM_FFN:ROW_NORM_FFN + 1, :] += jnp.sum(dh2 * xh2, axis=0, keepdims=True)
        dx2 = dx3 + _rms_bwd(dh2, xh2, r2, g2_ref[...], D_MODEL)
        dx2_ref[...] = dx2
        da = _dot_nt(dx2.astype(BF16), wo_ref[...])
        daf, dam = da[:, 0:FOX_W], da[:, FOX_W:2 * FOX_W]
        st_ref[ROW_OUT:ROW_OUT + 1, 0:FOX_W] += jnp.sum(daf * fhat, axis=0, keepdims=True)
        st_ref[ROW_OUT:ROW_OUT + 1, FOX_W:2 * FOX_W] += jnp.sum(dam * mhat, axis=0, keepdims=True)
        dof_ref[...] = _rms_bwd(daf, fhat, rf, gfo_ref[...], FOX_W).astype(BF16)
        dom_ref[...] = _rms_bwd(dam, mhat, rm, gmo_ref[...], FOX_W).astype(BF16)

    row = lambda w: pl.BlockSpec((tm, w), lambda i: (i, 0))
    ff = jax.ShapeDtypeStruct((T, D_FF), BF16)
    out_shape = (
        jax.ShapeDtypeStruct((T, 1024), BF16), jax.ShapeDtypeStruct((T, 1024), BF16), ff, ff, ff,
        jax.ShapeDtypeStruct((T, 1024), BF16), jax.ShapeDtypeStruct((T, 1024), F32),
        jax.ShapeDtypeStruct((T, 512), BF16), jax.ShapeDtypeStruct((T, 512), BF16), jax.ShapeDtypeStruct((8, 1024), F32),
    )
    return pl.pallas_call(
        body, name="mid", grid=(T // tm,), out_shape=out_shape,
        in_specs=[row(512), row(512), row(1024), row(1024), _full(g_fo.shape), _full(g_mo.shape), _full(g2.shape), _full(g3.shape),
                  _full(w_o.shape), _full(w_g.shape), _full(w_u.shape), _full(w_d.shape)],
        out_specs=[row(1024), row(1024), row(D_FF), row(D_FF), row(D_FF), row(1024), row(1024), row(512), row(512),
                   pl.BlockSpec((8, 1024), lambda i: (0, 0))],
        compiler_params=_params(("arbitrary",)),
    )(of, om, x, tgt, g_fo, g_mo, g2, g3, w_o, w_g, w_u, w_d)


def _in_bwd(dqf, dkf, dvf, dqm, dkm, dvm, lat, x, dx2, g1, gq, gkv, bfg, ct, st, sel_t, w_in, w_q12, w_kv, seq, tm):
    T = x.shape[0]
    nblk = T // tm
    nsb = seq // tm

    def body(dqf_ref, dkf_ref, dvf_ref, dqm_ref, dkm_ref, dvm_ref, lat_ref, x_ref, dx2_ref, g1_ref, gq_ref, gkv_ref, b_ref,
             ct_ref, st_ref, selt_ref, win_ref, wq_ref, wkv_ref, dx_ref, dproj_ref, dq12_ref, dkv_ref, stat_ref, carry):
        i = pl.program_id(0)

        @pl.when(i == 0)
        def _():
            stat_ref[...] = jnp.zeros_like(stat_ref)

        @pl.when(i % nsb == 0)
        def _():
            carry[...] = jnp.zeros_like(carry)

        lane = lax.broadcasted_iota(jnp.int32, (tm, LANES), 1)
        low = lane < 64
        ctv, stv = ct_ref[...], st_ref[...]

        for j in range(PAIRS):
            e, o = 2 * LANES * j, 2 * LANES * j + LANES
            dq = jnp.where(low, dqf_ref[:, e:e + LANES], 0.0) + pltpu.roll(jnp.where(low, dqf_ref[:, o:o + LANES], 0.0), 64, 1)
            dk = jnp.where(low, dkf_ref[:, e:e + LANES], 0.0) + pltpu.roll(jnp.where(low, dkf_ref[:, o:o + LANES], 0.0), 64, 1)
            dproj_ref[:, C_FQ + LANES * j:C_FQ + LANES * (j + 1)] = (dq * FOX_SCALE).astype(BF16)
            dproj_ref[:, C_FK + LANES * j:C_FK + LANES * (j + 1)] = dk.astype(BF16)
        dproj_ref[:, C_FV:C_QL] = dvf_ref[...]
        dcv = dkf_ref[...] + dqf_ref[...]
        k_hi = dcv.astype(BF16)
        k_lo = (dcv - k_hi.astype(F32)).astype(BF16)
        dc = _dot(k_hi, selt_ref[...]) + _dot(k_lo, selt_ref[...])
        rr = lax.broadcasted_iota(jnp.int32, (tm, tm), 0)
        cc = lax.broadcasted_iota(jnp.int32, (tm, tm), 1)
        triu = (cc >= rr).astype(BF16)
        a0, a1, a2 = _split3(dc)
        dlf = _dot(triu, a0) + _dot(triu, a1) + _dot(triu, a2) + carry[0:1, :]
        carry[0:1, :] = dlf[0:1, :]
        misc_a = lat_ref[:, Q_RANK + KV_RANK:Q_RANK + KV_RANK + LANES]
        z = misc_a + b_ref[...]
        dz = jnp.where(lane < HEADS, dlf * jax.nn.sigmoid(-z), 0.0)
        stat_ref[ROW_B:ROW_B + 1, 0:LANES] += jnp.sum(dz, axis=0, keepdims=True)

        cq = (jnp.where(low, 1.0, 0.0) + ctv) * MLA_SCALE
        sq = stv * MLA_SCALE
        dkpe = jnp.zeros((tm, LANES), F32)
        for hd in range(HEADS):
            s0 = LANES * hd
            dqh = dqm_ref[:, s0:s0 + LANES].astype(F32)
            dq12_ref[:, s0:s0 + LANES] = (dqh * cq).astype(BF16)
            dq12_ref[:, 1024 + s0:1024 + s0 + LANES] = (dqh * sq).astype(BF16)
            dkpe = dkpe + dkm_ref[:, s0:s0 + LANES].astype(F32)
        dkv_ref[:, 0:1024] = dkm_ref[...]
        dkv_ref[:, 1024:1536] = dvm_ref[...]
        dproj_ref[:, C_MA:C_END] = (dz + dkpe * ctv + pltpu.roll(dkpe * stv, 32, 1)).astype(BF16)
        dqn = _dot(dq12_ref[...], wq_ref[...])
        dkvn = _dot_nt(dkv_ref[...], wkv_ref[...])
        ql = lat_ref[:, 0:Q_RANK]
        kvl = lat_ref[:, Q_RANK:Q_RANK + KV_RANK]
        rq, rkv = _rms(ql, Q_RANK), _rms(kvl, KV_RANK)
        qhat, kvhat = ql * rq, kvl * rkv
        stat_ref[ROW_Q:ROW_Q + 1, 0:Q_RANK] += jnp.sum(dqn * qhat, axis=0, keepdims=True)
        stat_ref[ROW_KV:ROW_KV + 1, 0:KV_RANK] += jnp.sum(dkvn * kvhat, axis=0, keepdims=True)
        dproj_ref[:, C_QL:C_KVL] = _rms_bwd(dqn, qhat, rq, gq_ref[...], Q_RANK).astype(BF16)
        dproj_ref[:, C_KVL:C_MA] = _rms_bwd(dkvn, kvhat, rkv, gkv_ref[...], KV_RANK).astype(BF16)

        dh1 = _dot(dproj_ref[...], win_ref[...])
        xv = x_ref[...]
        r1 = _rms(xv, D_MODEL)
        xh = xv * r1
        stat_ref[ROW_NORM_MIX:ROW_NORM_MIX + 1, :] += jnp.sum(dh1 * xh, axis=0, keepdims=True)
        dx_ref[...] = dx2_ref[...] + _rms_bwd(dh1, xh, r1, g1_ref[...], D_MODEL)

    rev = lambda w: pl.BlockSpec((tm, w), lambda i: (nblk - 1 - i, 0))
    out_shape = (
        jax.ShapeDtypeStruct((T, 1024), F32), jax.ShapeDtypeStruct((T, C_END), BF16), jax.ShapeDtypeStruct((T, 2048), BF16),
        jax.ShapeDtypeStruct((T, 1536), BF16), jax.ShapeDtypeStruct((8, 1024), F32),
    )
    return pl.pallas_call(
        body, name="in_bwd", grid=(nblk,), out_shape=out_shape,
        in_specs=[rev(1024), rev(1024), rev(512), rev(1024), rev(1024), rev(512), rev(512), rev(1024), rev(1024),
                  _full(g1.shape), _full(gq.shape), _full(gkv.shape), _full(bfg.shape), rev(LANES), rev(LANES), _full(sel_t.shape),
                  _full(w_in.shape), _full(w_q12.shape), _full(w_kv.shape)],
        out_specs=[rev(1024), rev(C_END), rev(2048), rev(1536), pl.BlockSpec((8, 1024), lambda i: (0, 0))],
        scratch_shapes=[pltpu.VMEM((8, LANES), F32)],
        compiler_params=_params(("arbitrary",)),
    )(dqf, dkf, dvf, dqm, dkm, dvm, lat, x, dx2, g1, gq, gkv, bfg, ct, st, sel_t, w_in, w_q12, w_kv)


def _wgrad(a, b, tk, tt, name, ex=None):
    T, K = a.shape
    N = b.shape[1]
    n_in, n_out = (len(ex.ins), len(ex.outs)) if ex else (0, 0)
    gk, gt = K // tk, T // tt

    def body(*refs):
        a_ref, b_ref, o_ref = refs[0], refs[1], refs[2 + n_in]
        kb, t = pl.program_id(0), pl.program_id(1)
        if ex:
            ex_refs = (refs[2:2 + n_in], refs[3 + n_in:3 + n_in + n_out], refs[3 + n_in + n_out:])

            @pl.when((kb == 0) & (t == 0))
            def _():
                ex.start(*ex_refs)

        @pl.when(t == 0)
        def _():
            o_ref[...] = jnp.zeros_like(o_ref)

        o_ref[...] += _dot_tn(a_ref[...].astype(BF16), b_ref[...].astype(BF16))

        if ex:
            @pl.when((kb == gk - 1) & (t == gt - 1))
            def _():
                ex.wait(*ex_refs)

    res = pl.pallas_call(
        body, name=name, grid=(gk, gt), out_shape=(jax.ShapeDtypeStruct((K, N), F32),) + tuple(ex.outs if ex else ()),
        in_specs=[pl.BlockSpec((tt, tk), lambda kb, t: (t, kb)), pl.BlockSpec((tt, N), lambda kb, t: (t, 0))] + [ANY] * n_in,
        out_specs=[pl.BlockSpec((tk, N), lambda kb, t: (kb, 0))] + [ANY] * n_out,
        scratch_shapes=ex.sems() if ex else [], input_output_aliases=ex.aliases(2, 1) if ex else {},
        compiler_params=_params(("arbitrary", "arbitrary")),
    )(a, b, *(ex.ins if ex else ()))
    return (res[0], list(res[1:])) if ex else res[0]


def _adam_update(w, g, m, v):
    nm = ADAM_B1 * m + (1.0 - ADAM_B1) * g
    nv = ADAM_B2 * v + (1.0 - ADAM_B2) * (g * g)
    m_hat = nm / (1.0 - ADAM_B1 ** ADAM_STEP)
    v_hat = nv / (1.0 - ADAM_B2 ** ADAM_STEP)
    return -ADAM_LR * (m_hat / (jnp.sqrt(v_hat) + ADAM_EPS) + ADAM_WD * w), nm, nv


def _adamw_small(stats, params):
    k = len(SMALL)

    def body(*refs):
        for t, name in enumerate(SMALL):
            row, c0, width = SMALL_AT[name]
            w_ref, m_ref, v_ref = refs[1 + 3 * t:4 + 3 * t]
            g_ref, d_ref, nm_ref, nv_ref = refs[1 + 3 * k + 4 * t:5 + 3 * k + 4 * t]
            g = refs[0][row:row + 1, c0:c0 + width]
            g_ref[...] = g
            d_ref[...], nm_ref[...], nv_ref[...] = _adam_update(w_ref[...], g, m_ref[...], v_ref[...])

    vm = pl.BlockSpec(memory_space=pltpu.VMEM)
    out_shape = tuple(jax.ShapeDtypeStruct((1, SMALL_AT[name][2]), F32) for name in SMALL for _ in range(4))
    res = pl.pallas_call(body, name="adamw_small", out_shape=out_shape, in_specs=[vm] * (1 + 3 * k), out_specs=tuple([vm] * (4 * k)))(
        stats, *[a for name in SMALL for a in params[name]])
    return {name: tuple(res[4 * t:4 * t + 4]) for t, name in enumerate(SMALL)}


def _adamw(tensors, name):
    n = len(tensors)
    R, C = tensors[0][0].shape
    tr = _row_tile(R)

    def body(*refs):
        for t in range(n):
            w_ref, g_ref, m_ref, v_ref = refs[4 * t:4 * t + 4]
            d_ref, nm_ref, nv_ref = refs[4 * n + 3 * t:4 * n + 3 * t + 3]
            d_ref[...], nm_ref[...], nv_ref[...] = _adam_update(w_ref[...], g_ref[...], m_ref[...], v_ref[...])

    blk = pl.BlockSpec((tr, C), lambda i: (i, 0))
    sh = jax.ShapeDtypeStruct((R, C), F32)
    res = pl.pallas_call(
        body, name=name, grid=(R // tr,), out_shape=(sh,) * (3 * n),
        in_specs=[blk] * (4 * n), out_specs=[blk] * (3 * n),
        compiler_params=_params(("arbitrary",)),
    )(*[a for t in tensors for a in t])
    return [tuple(res[3 * t:3 * t + 3]) for t in range(n)]


def _arrange(win_t, wuq_t, wukv):
    dt = win_t.dtype
    z = lambda r: jnp.zeros((r, D_MODEL), dt)
    zh = lambda r: jnp.zeros((HEADS, r, Q_RANK), dt)
    kr1, kr2 = win_t[1928:1944], win_t[1944:1960]
    misc = jnp.concatenate([win_t[1536:1544], z(56), kr1, kr2, kr2, kr1], axis=0)
    w_in = jnp.concatenate([win_t[0:1536], win_t[1544:1928], misc], axis=0)
    wq = wuq_t.reshape(HEADS, 96, Q_RANK)
    q1 = jnp.concatenate([wq, zh(32)], axis=1).reshape(1024, Q_RANK)
    q2 = jnp.concatenate([zh(64), wq[:, 80:96], wq[:, 64:80], zh(32)], axis=1).reshape(1024, Q_RANK)
    wkv = wukv.reshape(KV_RANK, HEADS, 128)
    wk = jnp.concatenate([wkv[:, :, 0:64], jnp.zeros((KV_RANK, HEADS, 64), dt)], axis=2).reshape(KV_RANK, 1024)
    wv = wkv[:, :, 64:128].reshape(KV_RANK, 512)
    return dict(w_in=w_in, w_q12=jnp.concatenate([q1, q2], axis=0), w_k=wk, w_v=wv, w_kv=jnp.concatenate([wk, wv], axis=1))


def _unarrange(g_in, g_q12, g_kv):
    kr1 = g_in[C_MA + 64:C_MA + 80] + g_in[C_MA + 112:C_MA + 128]
    kr2 = g_in[C_MA + 80:C_MA + 96] + g_in[C_MA + 96:C_MA + 112]
    win_t = jnp.concatenate([g_in[0:1536], g_in[C_MA:C_MA + 8], g_in[1536:1920], kr1, kr2], axis=0)
    g1 = g_q12[0:1024].reshape(HEADS, 128, Q_RANK)
    g2 = g_q12[1024:2048].reshape(HEADS, 128, Q_RANK)
    wuq_t = jnp.concatenate([g1[:, 0:64], g1[:, 64:80] + g2[:, 80:96], g1[:, 80:96] + g2[:, 64:80]], axis=1).reshape(768, Q_RANK)
    gk = g_kv[:, 0:1024].reshape(KV_RANK, HEADS, 128)
    gv = g_kv[:, 1024:1536].reshape(KV_RANK, HEADS, 64)
    wukv = jnp.concatenate([gk[:, :, 0:64], gv], axis=2).reshape(KV_RANK, 1024)
    return win_t, wuq_t, wukv


def _selectors():
    sel = np.zeros((384, 1024), np.float32)
    sel_t = np.zeros((1024, LANES), np.float32)
    for h in range(HEADS):
        for piece in range(3):
            sel[LANES * piece + h, LANES * h + 64 + piece] = 1.0
        sel_t[LANES * h + 64, h] = 1.0
    return jnp.asarray(sel, BF16), jnp.asarray(sel_t, BF16)


def _rope_tables(positions):
    inv_freq = 10000.0 ** (-jnp.arange(0, ROPE, 2, dtype=F32) / ROPE)
    n = positions.size
    ang = (positions.reshape(n // 8, 8, 1).astype(F32) * inv_freq[None, None, :]).reshape(n // 8, 8 * (ROPE // 2))
    cos, sin = lax.optimization_barrier((jnp.cos(lax.optimization_barrier(ang)), jnp.sin(lax.optimization_barrier(ang))))
    cos, sin = cos.reshape(n, ROPE // 2), sin.reshape(n, ROPE // 2)
    z64, z32 = jnp.zeros((n, 64), F32), jnp.zeros((n, 32), F32)
    return jnp.concatenate([z64, cos, cos, z32], axis=1), jnp.concatenate([z64, -sin, sin, z32], axis=1)


def _work(name, t):
    return jnp.swapaxes(t[0], 0, 1) if name in TRANSPOSED else t[0]


def _back(name, t):
    return (jnp.swapaxes(t, 0, 1) if name in TRANSPOSED else t)[None]


def kernel(x, positions, norm_mix_g, w_in, b_fgate, q_norm_g, w_uq, kv_norm_g, w_ukv, fox_out_g, mla_out_g, w_o, norm_ffn_g, w_gate, w_up, w_down, final_norm_g, loss_target, m_norm_mix_g, m_w_in, m_b_fgate, m_q_norm_g, m_w_uq, m_kv_norm_g, m_w_ukv, m_fox_out_g, m_mla_out_g, m_w_o, m_norm_ffn_g, m_w_gate, m_w_up, m_w_down, m_final_norm_g, v_norm_mix_g, v_w_in, v_b_fgate, v_q_norm_g, v_w_uq, v_kv_norm_g, v_w_ukv, v_fox_out_g, v_mla_out_g, v_w_o, v_norm_ffn_g, v_w_gate, v_w_up, v_w_down, v_final_norm_g):
    names = ["norm_mix_g", "w_in", "b_fgate", "q_norm_g", "w_uq", "kv_norm_g", "w_ukv", "fox_out_g", "mla_out_g", "w_o",
             "norm_ffn_g", "w_gate", "w_up", "w_down", "final_norm_g"]
    wts = dict(zip(names, [norm_mix_g, w_in, b_fgate, q_norm_g, w_uq, kv_norm_g, w_ukv, fox_out_g, mla_out_g, w_o, norm_ffn_g,
                           w_gate, w_up, w_down, final_norm_g]))
    mom = dict(zip(names, [m_norm_mix_g, m_w_in, m_b_fgate, m_q_norm_g, m_w_uq, m_kv_norm_g, m_w_ukv, m_fox_out_g, m_mla_out_g,
                           m_w_o, m_norm_ffn_g, m_w_gate, m_w_up, m_w_down, m_final_norm_g]))
    var = dict(zip(names, [v_norm_mix_g, v_w_in, v_b_fgate, v_q_norm_g, v_w_uq, v_kv_norm_g, v_w_ukv, v_fox_out_g, v_mla_out_g,
                           v_w_o, v_norm_ffn_g, v_w_gate, v_w_up, v_w_down, v_final_norm_g]))
    shard = {n: _work(n, wts[n]) for n in HEAD3 + FFN4}
    nb, seq, _ = x.shape
    T = nb * seq
    tm, tq = min(ROW_TILE, seq), min(ATTN_TILE, seq)
    tt = min(WGRAD_TILE, T)
    xf = x.reshape(T, D_MODEL)
    tgt = loss_target.reshape(T, D_MODEL)
    chip = 2 * lax.axis_index("x") + lax.axis_index("y")

    mine = [shard[n].astype(BF16) for n in HEAD3]
    head = _run_exchange(_gather_split_exchange(mine), "gather_head")
    win4, wuq4, wukv4 = [lax.dynamic_update_slice(h, s[None], (chip, 0, 0)) for h, s in zip(head, mine)]
    a = _arrange(win4.reshape(-1, D_MODEL), wuq4.reshape(-1, Q_RANK), wukv4.transpose(1, 0, 2).reshape(KV_RANK, -1))
    sel, sel_t = _selectors()
    ct, st = _rope_tables(positions)
    bfg = jnp.concatenate([b_fgate, jnp.zeros((1, LANES - HEADS), F32)], axis=1)
    g1, gq, gkv = norm_mix_g, q_norm_g, kv_norm_g

    h1, qf, kf, vf, qm, km, vm, lat, qn, kvn = _in_proj(xf, g1, a["w_in"], a["w_q12"], a["w_k"], a["w_v"], gq, gkv, bfg, ct, st, sel, seq,
                                                        min(IN_PROJ_TILE, seq))
    tqf = min(ATTN_FWD_TILE, seq)
    of, lse_f, (wo4, wg4) = _attn_fwd(qf, kf, vf, nb, seq, tqf, "fox_fwd", _gather_exchange([shard[n].astype(BF16) for n in FFN4[:2]]))
    om, lse_m, (wu4, wd4) = _attn_fwd(qm, km, vm, nb, seq, tqf, "mla_fwd", _gather_exchange([shard[n].astype(BF16) for n in FFN4[2:]]))
    a_cat, h2, hid, dg, du, dx3, dx2, dof, dom, st_mid = _mid(
        of, om, xf, tgt, fox_out_g, mla_out_g, norm_ffn_g, final_norm_g.reshape(1, D_MODEL),
        wo4.reshape(D_MODEL, D_MODEL), wg4.reshape(D_FF, D_MODEL), wu4.reshape(D_FF, D_MODEL), wd4.reshape(D_FF, D_MODEL), tm)

    slab = lambda g: g.reshape(N_CHIPS, g.shape[0] // N_CHIPS, g.shape[1])
    big = [slab(_wgrad(a_cat, dx2, D_MODEL, tt, "wgrad_o")), slab(_wgrad(dg, h2, D_FF // 2, tt, "wgrad_gate")),
           slab(_wgrad(du, h2, D_FF // 2, tt, "wgrad_up")), slab(_wgrad(hid, dx3, D_FF // 2, tt, "wgrad_down"))]
    dqf, dkf, dvf, got = _attn_bwd(qf, kf, vf, of, dof, lse_f, nb, seq, tq, "fox_bwd", True, _swap_exchange(big))
    sums = [_add_half(g, s) for g, s in zip(big, got)]
    dqm, dkm, dvm, recv = _attn_bwd(qm, km, vm, om, dom, lse_m, nb, seq, tq, "mla_bwd", False, _scatter_exchange(sums))
    halves = [_sum_slabs(g, s, r) for g, s, r in zip(big, got, recv)]
    dx, dproj, dq12, dkv, st_in = _in_bwd(dqf, dkf, dvf, dqm, dkm, dvm, lat, xf, dx2, g1, gq, gkv, bfg, ct, st, sel_t,
                                             a["w_in"], a["w_q12"], a["w_kv"], seq, tm)
    g_in, results = _wgrad(dproj, h1, C_END, tt, "wgrad_in", _both(_join_exchange(halves), _everyone_exchange(st_mid + st_in)))
    gshard = dict(zip(FFN4, results[:4]))
    stats = _sum_devices(results[4])

    gwin_t, gwuq_t, gwukv = _unarrange(g_in, _wgrad(dq12, qn, 2048, tt, "wgrad_uq"), _wgrad(kvn, dkv, KV_RANK, tt, "wgrad_ukv"))
    tail = [slab(gwin_t), slab(gwuq_t), gwukv.reshape(KV_RANK, N_CHIPS, -1).transpose(1, 0, 2)]
    tail_got = _run_exchange(_swap_exchange(tail), "tail_swap")
    tail_sums = [_add_half(g, s) for g, s in zip(tail, tail_got)]
    tail_recv = _run_exchange(_scatter_exchange(tail_sums), "tail_scatter")
    tail_joined = _run_exchange(_join_exchange([_sum_slabs(g, s, r) for g, s, r in zip(tail, tail_got, tail_recv)]), "tail_join")
    gshard.update(zip(HEAD3, tail_joined))
    quad = lambda n: (shard[n], gshard[n], _work(n, mom[n]), _work(n, var[n]))
    updates = dict(zip(FFN4[1:], _adamw([quad(n) for n in FFN4[1:]], "adamw_ffn")))
    for n in HEAD3 + FFN4[:1]:
        updates[n], = _adamw([quad(n)], "adamw_" + n)

    grads, delta, new_m, new_v = {}, {}, {}, {}
    for n in HEAD3 + FFN4:
        grads[n] = _back(n, gshard[n])
        delta[n], new_m[n], new_v[n] = [_back(n, t) for t in updates[n]]
    row = lambda t: t.reshape(1, -1)
    small = _adamw_small(stats, {n: (row(wts[n]), row(mom[n]), row(var[n])) for n in SMALL})
    for n in SMALL:
        grads[n], delta[n], new_m[n], new_v[n] = [t.reshape(wts[n].shape) for t in small[n]]
    loss = jnp.sum(stats[ROW_LOSS])
    return (loss, dx.reshape(x.shape), *[grads[n] for n in names], *[delta[n] for n in names],
            *[new_m[n] for n in names], *[new_v[n] for n in names])
```

```python
import functools

import numpy as np
import jax
import jax.numpy as jnp
from jax import lax
from jax.experimental import pallas as pl
from jax.experimental.pallas import tpu as pltpu

F32 = jnp.float32
BF16 = jnp.bfloat16
MESH = pl.DeviceIdType.MESH

EPS = 1e-6
D_MODEL = 1024
HEADS = 8
PAIRS = HEADS // 2
FOX_W = 512
Q_RANK = 256
KV_RANK = 128
ROPE = 32
D_FF = 2816
N_CHIPS = 4
FOX_SCALE = 64 ** -0.5
MLA_SCALE = 96 ** -0.5
LANES = 128
NEG = -1e30

ADAM_LR, ADAM_B1, ADAM_B2, ADAM_EPS, ADAM_WD, ADAM_STEP = 0.001, 0.9, 0.999, 1e-08, 0.01, 10

C_FQ, C_FK, C_FV, C_QL, C_KVL, C_MA, C_END = 0, 512, 1024, 1536, 1792, 1920, 2048
C_MB = C_END

VMEM_LIMIT = 60 * 1024 * 1024
ROW_TILE = 256
IN_PROJ_TILE = 512
ATTN_TILE = 512
ATTN_FWD_TILE = 1024
WGRAD_TILE = 2048

HEAD3 = ("w_in", "w_uq", "w_ukv")
FFN4 = ("w_o", "w_gate", "w_up", "w_down")
TRANSPOSED = ("w_in", "w_uq", "w_gate", "w_up")
SMALL = ("norm_mix_g", "b_fgate", "q_norm_g", "kv_norm_g", "fox_out_g", "mla_out_g", "norm_ffn_g", "final_norm_g")
ROW_NORM_MIX, ROW_NORM_FFN, ROW_FINAL, ROW_OUT, ROW_Q, ROW_KV, ROW_B, ROW_LOSS = range(8)
SMALL_AT = {"norm_mix_g": (ROW_NORM_MIX, 0, 1024), "norm_ffn_g": (ROW_NORM_FFN, 0, 1024), "final_norm_g": (ROW_FINAL, 0, 1024),
            "fox_out_g": (ROW_OUT, 0, 512), "mla_out_g": (ROW_OUT, 512, 512), "q_norm_g": (ROW_Q, 0, 256),
            "kv_norm_g": (ROW_KV, 0, 128), "b_fgate": (ROW_B, 0, 8)}


def _params(sem=None):
    return pltpu.CompilerParams(dimension_semantics=sem, vmem_limit_bytes=VMEM_LIMIT)


def _full(shape):
    n = len(shape)
    return pl.BlockSpec(shape, lambda *_: (0,) * n, pipeline_mode=pl.Buffered(1))


def _dot(a, b):
    return jnp.dot(a, b, preferred_element_type=F32)


def _dot_nt(a, b):
    return lax.dot_general(a, b, (((1,), (1,)), ((), ())), preferred_element_type=F32)


def _dot_tn(a, b):
    return lax.dot_general(a, b, (((0,), (0,)), ((), ())), preferred_element_type=F32)


def _split3(v):
    hi = v.astype(BF16)
    r1 = v - hi.astype(F32)
    mid = r1.astype(BF16)
    lo = (r1 - mid.astype(F32)).astype(BF16)
    return hi, mid, lo


def _rms(v, width):
    return lax.rsqrt(jnp.sum(v * v, axis=1, keepdims=True) * (1.0 / width) + EPS)


def _rms_bwd(dy, xhat, r, g, width):
    u = dy * g
    return r * (u - xhat * (jnp.sum(u * xhat, axis=1, keepdims=True) * (1.0 / width)))


ANY = pl.BlockSpec(memory_space=pl.ANY)


def _place():
    return lax.axis_index("x"), lax.axis_index("y"), lax.axis_index("c")


def _other_chips(x, y):
    return [(1 - x, y), (x, 1 - y), (1 - x, 1 - y)]


def _remote(src, dst, send, recv, j, dev):
    return pltpu.make_async_remote_copy(src_ref=src, dst_ref=dst, send_sem=send.at[j], recv_sem=recv.at[j], device_id=dev, device_id_type=MESH)


class _Exchange:
    def __init__(self, ins, outs, n_remote, n_local, build, in_place=False):
        self.ins, self.outs, self.n_remote, self.n_local, self.build = list(ins), list(outs), n_remote, max(n_local, 1), build
        self.in_place = in_place
        self.n_aliased = len(self.ins)

    def aliases(self, first_in, first_out):
        return {first_in + i: first_out + i for i in range(self.n_aliased)} if self.in_place else {}

    def sems(self):
        return [pltpu.SemaphoreType.DMA((self.n_remote,)), pltpu.SemaphoreType.DMA((self.n_remote,)), pltpu.SemaphoreType.DMA((self.n_local,))]

    def start(self, in_refs, out_refs, sems):
        for cp in self.build(in_refs, out_refs, *sems)[0]:
            cp.start()

    def wait(self, in_refs, out_refs, sems):
        for w in self.build(in_refs, out_refs, *sems)[1]:
            w()


def _gather_exchange(shards):
    def build(ins, outs, send, recv, lsem):
        x, y, c = _place()
        starts, waits = [], []
        for i, (s, o) in enumerate(zip(ins, outs)):
            mine = pltpu.make_async_copy(s, o.at[2 * x + y], lsem.at[i])
            starts.append(mine)
            waits.append(mine.wait)
            for j, (cx, cy) in enumerate(_other_chips(x, y)):
                out = _remote(s, o.at[2 * x + y], send, recv, 3 * i + j, (cx, cy, c))
                starts.append(out)
                waits.append(_remote(s, o.at[2 * cx + cy], send, recv, 3 * i + j, (cx, cy, c)).wait_recv)
                waits.append(out.wait_send)
        return starts, waits

    outs = [jax.ShapeDtypeStruct((N_CHIPS,) + s.shape, s.dtype) for s in shards]
    return _Exchange(shards, outs, 3 * len(shards), len(shards), build)


def _gather_split_exchange(shards):
    n = len(shards)

    def build(ins, outs, send, recv, lsem):
        x, y, c = _place()
        starts, waits, last = [], [], []
        for i, (s, o) in enumerate(zip(ins, outs)):
            hc = s.shape[1] // 2
            mine, other = pl.ds(c * hc, hc), pl.ds((1 - c) * hc, hc)
            for j, (cx, cy) in enumerate(_other_chips(x, y)):
                out = _remote(s.at[:, mine], o.at[2 * x + y, :, mine], send, recv, 3 * i + j, (cx, cy, c))
                landed = o.at[2 * cx + cy, :, mine]
                arrive = _remote(s.at[:, mine], landed, send, recv, 3 * i + j, (cx, cy, c))
                onward = _remote(landed, landed, send, recv, 3 * n + 3 * i + j, (x, y, 1 - c))
                from_sibling = _remote(landed, o.at[2 * cx + cy, :, other], send, recv, 3 * n + 3 * i + j, (x, y, 1 - c))
                starts.append(out)
                waits.append(lambda arrive=arrive, onward=onward: (arrive.wait_recv(), onward.start()))
                last += [from_sibling.wait_recv, onward.wait_send, out.wait_send]
        return starts, waits + last

    outs = [jax.ShapeDtypeStruct((N_CHIPS,) + s.shape, s.dtype) for s in shards]
    return _Exchange(shards, outs, 6 * n, 0, build)


def _swap_exchange(grads):
    def build(ins, outs, send, recv, lsem):
        x, y, c = _place()
        cps = []
        for i, (g, o) in enumerate(zip(ins, outs)):
            hc = g.shape[2] // 2
            cps.append(_remote(g.at[:, :, pl.ds((1 - c) * hc, hc)], o, send, recv, i, (x, y, 1 - c)))
        return cps, [cp.wait for cp in cps]

    outs = [jax.ShapeDtypeStruct((g.shape[0], g.shape[1], g.shape[2] // 2), g.dtype) for g in grads]
    return _Exchange(grads, outs, len(grads), 0, build)


def _scatter_exchange(sums):
    def build(ins, outs, send, recv, lsem):
        x, y, c = _place()
        cps = []
        for i, (s, o) in enumerate(zip(ins, outs)):
            for j, (cx, cy) in enumerate(_other_chips(x, y)):
                cps.append(_remote(s.at[2 * cx + cy], o.at[j], send, recv, 3 * i + j, (cx, cy, c)))
        return cps, [cp.wait for cp in cps]

    outs = [jax.ShapeDtypeStruct((3,) + s.shape[1:], s.dtype) for s in sums]
    return _Exchange(sums, outs, 3 * len(sums), 0, build)


def _join_exchange(bufs):
    def build(ins, outs, send, recv, lsem):
        x, y, c = _place()
        starts, waits = [], []
        for i, (t, o) in enumerate(zip(ins, outs)):
            hc = t.shape[1] // 2
            out = _remote(t.at[:, pl.ds(c * hc, hc)], o.at[:, pl.ds(c * hc, hc)], send, recv, i, (x, y, 1 - c))
            starts.append(out)
            waits += [_remote(t.at[:, pl.ds(c * hc, hc)], o.at[:, pl.ds((1 - c) * hc, hc)], send, recv, i, (x, y, 1 - c)).wait_recv,
                      out.wait_send]
        return starts, waits

    outs = [jax.ShapeDtypeStruct(t.shape, t.dtype) for t in bufs]
    return _Exchange(bufs, outs, len(bufs), 0, build, in_place=True)


def _everyone_exchange(v):
    def build(ins, outs, send, recv, lsem):
        x, y, c = _place()
        me = 4 * x + 2 * y + c
        mine = pltpu.make_async_copy(ins[0], outs[0].at[me], lsem.at[0])
        starts, waits = [mine], [mine.wait]
        for j in range(7):
            fx, fy, fc = (j + 1) >> 2 & 1, (j + 1) >> 1 & 1, (j + 1) & 1
            peer = (x ^ fx, y ^ fy, c ^ fc)
            out = _remote(ins[0], outs[0].at[me], send, recv, j, peer)
            starts.append(out)
            waits += [_remote(ins[0], outs[0].at[4 * peer[0] + 2 * peer[1] + peer[2]], send, recv, j, peer).wait_recv, out.wait_send]
        return starts, waits

    return _Exchange([v], [jax.ShapeDtypeStruct((8,) + v.shape, v.dtype)], 7, 1, build)


def _both(a, b):
    na_in, na_out = len(a.ins), len(a.outs)

    def build(ins, outs, send, recv, lsem):
        sa, wa = a.build(ins[:na_in], outs[:na_out], send.at[pl.ds(0, a.n_remote)], recv.at[pl.ds(0, a.n_remote)],
                         lsem.at[pl.ds(0, a.n_local)])
        sb, wb = b.build(ins[na_in:], outs[na_out:], send.at[pl.ds(a.n_remote, b.n_remote)], recv.at[pl.ds(a.n_remote, b.n_remote)],
                         lsem.at[pl.ds(a.n_local, b.n_local)])
        return sa + sb, wa + wb

    both = _Exchange(a.ins + b.ins, a.outs + b.outs, a.n_remote + b.n_remote, a.n_local + b.n_local, build, in_place=a.in_place)
    both.n_aliased = na_in
    return both


def _run_exchange(ex, name):
    n_in, n_out = len(ex.ins), len(ex.outs)

    def body(*refs):
        ins, outs, sems = refs[:n_in], refs[n_in:n_in + n_out], refs[n_in + n_out:]
        ex.start(ins, outs, sems)
        ex.wait(ins, outs, sems)

    return pl.pallas_call(
        body, name=name, out_shape=tuple(ex.outs), in_specs=[ANY] * n_in, out_specs=tuple([ANY] * n_out),
        scratch_shapes=ex.sems(), input_output_aliases=ex.aliases(0, 0),
        compiler_params=pltpu.CompilerParams(has_side_effects=True),
    )(*ex.ins)


def _sum_devices(rows):
    def body(r_ref, o_ref):
        acc = r_ref[0]
        for d in range(1, 8):
            acc = acc + r_ref[d]
        o_ref[...] = acc

    vm = pl.BlockSpec(memory_space=pltpu.VMEM)
    return pl.pallas_call(body, name="sum_devices", out_shape=jax.ShapeDtypeStruct(rows.shape[1:], rows.dtype),
                          in_specs=[vm], out_specs=vm)(rows)


def _add_half(g, got):
    n, R, C = g.shape
    hc = C // 2

    def body(c_ref, g_ref, r_ref, o_ref):
        o_ref[...] = (g_ref[...] + r_ref[...]).astype(BF16)

    c = lax.axis_index("c")
    return pl.pallas_call(
        body, name="add_half",
        grid_spec=pltpu.PrefetchScalarGridSpec(
            num_scalar_prefetch=1, grid=(n,),
            in_specs=[pl.BlockSpec((1, R, hc), lambda k, c_ref: (k, 0, c_ref[0])),
                      pl.BlockSpec((1, R, hc), lambda k, c_ref: (k, 0, 0))],
            out_specs=pl.BlockSpec((1, R, hc), lambda k, c_ref: (k, 0, 0))),
        out_shape=jax.ShapeDtypeStruct((n, R, hc), BF16),
        compiler_params=_params(("arbitrary",)),
    )(jnp.reshape(c, (1,)).astype(jnp.int32), g, got)


def _sum_slabs(g, got, recv):
    _, R, C = g.shape
    hc = C // 2

    def body(kc_ref, g_ref, s_ref, r_ref, o_ref):
        o_ref[...] = (((g_ref[0] + s_ref[0]) + r_ref[0].astype(F32)) + r_ref[1].astype(F32)) + r_ref[2].astype(F32)

    kc = jnp.stack([2 * lax.axis_index("x") + lax.axis_index("y"), lax.axis_index("c")]).astype(jnp.int32)
    return pl.pallas_call(
        body, name="sum_slabs",
        grid_spec=pltpu.PrefetchScalarGridSpec(
            num_scalar_prefetch=1, grid=(1,),
            in_specs=[pl.BlockSpec((1, R, hc), lambda i, kc_ref: (kc_ref[0], 0, kc_ref[1])),
                      pl.BlockSpec((1, R, hc), lambda i, kc_ref: (kc_ref[0], 0, 0)),
                      pl.BlockSpec((3, R, hc), lambda i, kc_ref: (0, 0, 0))],
            out_specs=pl.BlockSpec((R, hc), lambda i, kc_ref: (0, kc_ref[1]))),
        out_shape=jax.ShapeDtypeStruct((R, C), F32),
        compiler_params=_params(("arbitrary",)),
    )(kc, g, got, recv)


def _row_tile(rows):
    for cand in (256, 184, 176, 144, 128, 64, 32, 16, 8):
        if rows % cand == 0:
            return cand
    return rows


def _in_proj(x, g1, w_in, w_q12, w_k, w_v, gq, gkv, bfg, ct, st, sel, seq, tm):
    T = x.shape[0]
    nsb = seq // tm

    def body(x_ref, g1_ref, win_ref, wq_ref, wk_ref, wv_ref, gq_ref, gkv_ref, b_ref, ct_ref, st_ref, sel_ref,
             h1_ref, qf_ref, kf_ref, vf_ref, qm_ref, km_ref, vm_ref, lat_ref, qn_ref, kvn_ref, carry):
        i = pl.program_id(0)

        @pl.when(i % nsb == 0)
        def _():
            carry[...] = jnp.zeros_like(carry)

        xv = x_ref[...]
        h = (xv * _rms(xv, D_MODEL) * g1_ref[...]).astype(BF16)
        h1_ref[...] = h
        proj = _dot_nt(h, win_ref[...])
        lane = lax.broadcasted_iota(jnp.int32, (tm, LANES), 1)
        low = lane < 64
        misc_a = proj[:, C_MA:C_END]
        misc_b = pltpu.roll(misc_a, 96, 1)

        z = misc_a + b_ref[...]
        lf = jnp.where(lane < HEADS, jnp.minimum(z, 0.0) - jnp.log1p(jnp.exp(-jnp.abs(z))), 0.0)
        rr = lax.broadcasted_iota(jnp.int32, (tm, tm), 0)
        cc = lax.broadcasted_iota(jnp.int32, (tm, tm), 1)
        tri = (rr >= cc).astype(BF16)
        a0, a1, a2 = _split3(lf)
        c = _dot(tri, a0) + _dot(tri, a1) + _dot(tri, a2) + carry[0:1, :]
        carry[0:1, :] = c[tm - 1:tm, :]
        c0, c1, c2 = _split3(c)
        cpl = _dot(jnp.concatenate([c0, c1, c2], axis=1), sel_ref[...])
        qpad = jnp.where((lane >= 64) & (lane < 67), -1.0, 0.0)
        for j in range(PAIRS):
            qc = proj[:, C_FQ + LANES * j:C_FQ + LANES * (j + 1)] * FOX_SCALE
            kc = proj[:, C_FK + LANES * j:C_FK + LANES * (j + 1)]
            e, o = 2 * LANES * j, 2 * LANES * j + LANES
            qf_ref[:, e:e + LANES] = jnp.where(low, qc, qpad).astype(BF16)
            qf_ref[:, o:o + LANES] = jnp.where(low, pltpu.roll(qc, 64, 1), qpad).astype(BF16)
            kf_ref[:, e:e + LANES] = jnp.where(low, kc, cpl[:, e:e + LANES]).astype(BF16)
            kf_ref[:, o:o + LANES] = jnp.where(low, pltpu.roll(kc, 64, 1), cpl[:, o:o + LANES]).astype(BF16)
        vf_ref[...] = proj[:, C_FV:C_QL].astype(BF16)

        ql = proj[:, C_QL:C_KVL]
        kvl = proj[:, C_KVL:C_MA]
        qn = (ql * _rms(ql, Q_RANK) * gq_ref[...]).astype(BF16)
        kvn = (kvl * _rms(kvl, KV_RANK) * gkv_ref[...]).astype(BF16)
        lat_ref[...] = proj[:, C_QL:C_MB]
        qn_ref[...] = qn
        kvn_ref[...] = kvn
        q12 = _dot_nt(qn, wq_ref[...])
        kn = _dot(kvn, wk_ref[...])
        ctv = ct_ref[...]
        stv = st_ref[...]
        cq = (jnp.where(low, 1.0, 0.0) + ctv) * MLA_SCALE
        sq = stv * MLA_SCALE
        kpe = misc_a * ctv + misc_b * stv
        for hd in range(HEADS):
            s0 = LANES * hd
            qm_ref[:, s0:s0 + LANES] = (q12[:, s0:s0 + LANES] * cq + q12[:, 1024 + s0:1024 + s0 + LANES] * sq).astype(BF16)
            km_ref[:, s0:s0 + LANES] = (kn[:, s0:s0 + LANES] + kpe).astype(BF16)
        vm_ref[...] = _dot(kvn, wv_ref[...]).astype(BF16)

    row = lambda w: pl.BlockSpec((tm, w), lambda i: (i, 0))
    out_shape = (
        jax.ShapeDtypeStruct((T, D_MODEL), BF16),
        jax.ShapeDtypeStruct((T, 1024), BF16), jax.ShapeDtypeStruct((T, 1024), BF16), jax.ShapeDtypeStruct((T, 512), BF16),
        jax.ShapeDtypeStruct((T, 1024), BF16), jax.ShapeDtypeStruct((T, 1024), BF16), jax.ShapeDtypeStruct((T, 512), BF16),
        jax.ShapeDtypeStruct((T, 512), F32),
        jax.ShapeDtypeStruct((T, Q_RANK), BF16), jax.ShapeDtypeStruct((T, KV_RANK), BF16),
    )
    return pl.pallas_call(
        body, name="in_proj", grid=(T // tm,), out_shape=out_shape,
        in_specs=[row(D_MODEL), _full(g1.shape), _full(w_in.shape), _full(w_q12.shape), _full(w_k.shape), _full(w_v.shape),
                  _full(gq.shape), _full(gkv.shape), _full(bfg.shape), row(LANES), row(LANES), _full(sel.shape)],
        out_specs=[row(D_MODEL), row(1024), row(1024), row(512), row(1024), row(1024), row(512), row(512), row(Q_RANK), row(KV_RANK)],
        scratch_shapes=[pltpu.VMEM((8, LANES), F32)],
        compiler_params=_params(("arbitrary",)),
    )(x, g1, w_in, w_q12, w_k, w_v, gq, gkv, bfg, ct, st, sel)


def _attn_fwd(q, k, v, nb, seq, tq, name, ex=None):
    T = q.shape[0]
    nq = seq // tq
    n_in, n_out = (len(ex.ins), len(ex.outs)) if ex else (0, 0)

    def body(*refs):
        q_ref, k_ref, v_ref = refs[0:3]
        o_ref, lse_ref = refs[3 + n_in:5 + n_in]
        b, pr, qi = pl.program_id(0), pl.program_id(1), pl.program_id(2)
        if ex:
            ex_refs = (refs[3:3 + n_in], refs[5 + n_in:5 + n_in + n_out], refs[8 + n_in + n_out:])

            @pl.when((b == 0) & (pr == 0) & (qi == 0))
            def _():
                ex.start(*ex_refs)

        s_sc, p_sc, acc_sc = refs[5 + n_in + n_out:8 + n_in + n_out]
        strip = 64
        key_s = lax.broadcasted_iota(jnp.int32, (strip, tq), 0)
        qry_s = lax.broadcasted_iota(jnp.int32, (strip, tq), 1)
        row_t = lax.broadcasted_iota(jnp.int32, (LANES, tq), 0)
        acc_sc[...] = jnp.zeros(acc_sc.shape, F32)

        def fold(x, op):
            out = x[0:8]
            for r in range(8, strip, 8):
                out = op(out, x[r:r + 8])
            return out

        def step(kj, state, masked):
            rows = pl.ds(pl.multiple_of(kj * tq, tq), tq)
            for hh in range(2):
                s_sc[hh] = _dot_nt(k_ref[rows, LANES * hh:LANES * (hh + 1)], q_ref[:, LANES * hh:LANES * (hh + 1)])
            vv = v_ref[rows, :]
            new = []
            for hh in range(2):
                m, l = state[hh]

                def strip_of(r0, hh=hh):
                    s = s_sc[hh, r0:r0 + strip, :]
                    return jnp.where(key_s + r0 <= qry_s, s, NEG) if masked else s

                mx = fold(strip_of(0), jnp.maximum)
                for r0 in range(strip, tq, strip):
                    mx = jnp.maximum(mx, fold(strip_of(r0), jnp.maximum))
                m_new = jnp.maximum(m, jnp.max(mx, axis=0, keepdims=True))
                alpha = jnp.exp(m - m_new)
                sm = jnp.zeros((8, tq), F32)
                for r0 in range(0, tq, strip):
                    p = jnp.exp(strip_of(r0) - m_new)
                    sm = sm + fold(p, jnp.add)
                    p_sc[hh, r0:r0 + strip, :] = p.astype(BF16)
                l = alpha * l + jnp.sum(sm, axis=0, keepdims=True)
                acc_sc[hh] = alpha * acc_sc[hh] + _dot_tn(vv, p_sc[hh])
                new.append((m_new, l))
            return tuple(new)

        one = (jnp.full((1, tq), NEG, F32), jnp.zeros((1, tq), F32))
        state = lax.fori_loop(0, qi, functools.partial(step, masked=False), (one, one))
        (m0, l0), (m1, l1) = step(qi, state, True)
        o_ref[...] = jnp.where(row_t < 64, acc_sc[0] / l0, acc_sc[1] / l1).T
        lse_ref[:, 0:LANES] = jnp.broadcast_to(m0 + jnp.log(l0), (LANES, tq)).T
        lse_ref[:, LANES:2 * LANES] = jnp.broadcast_to(m1 + jnp.log(l1), (LANES, tq)).T

        if ex:
            @pl.when((b == nb - 1) & (pr == PAIRS - 1) & (qi == nq - 1))
            def _():
                ex.wait(*ex_refs)

    res = pl.pallas_call(
        body, name=name, grid=(nb, PAIRS, nq),
        out_shape=(jax.ShapeDtypeStruct((T, 512), F32), jax.ShapeDtypeStruct((T, 1024), F32)) + tuple(ex.outs if ex else ()),
        in_specs=[pl.BlockSpec((tq, 2 * LANES), lambda b, p, i: (b * nq + i, p)),
                  pl.BlockSpec((seq, 2 * LANES), lambda b, p, i: (b, p)),
                  pl.BlockSpec((seq, LANES), lambda b, p, i: (b, p))] + [ANY] * n_in,
        out_specs=[pl.BlockSpec((tq, LANES), lambda b, p, i: (b * nq + i, p)),
                   pl.BlockSpec((tq, 2 * LANES), lambda b, p, i: (b * nq + i, p))] + [ANY] * n_out,
        scratch_shapes=[pltpu.VMEM((2, tq, tq), F32), pltpu.VMEM((2, tq, tq), BF16), pltpu.VMEM((2, LANES, tq), F32)]
        + (ex.sems() if ex else []),
        compiler_params=_params(("arbitrary", "arbitrary", "arbitrary")),
    )(q, k, v, *(ex.ins if ex else ()))
    return res[0], res[1], list(res[2:])


def _attn_bwd(q, k, v, o, do, lse, nb, seq, tq, name, key_bias, ex=None):
    T = q.shape[0]
    nq = seq // tq
    n_in, n_out = (len(ex.ins), len(ex.outs)) if ex else (0, 0)
    n_res = 4 if key_bias else 3

    def body(*refs):
        q_ref, k_ref, v_ref, o_ref, do_ref, lse_ref = refs[0:6]
        dq_ref, dk_ref, dv_ref = refs[6 + n_in:9 + n_in]
        dcb_ref = refs[9 + n_in] if key_bias else None
        first_scratch = 6 + n_in + n_res + n_out
        dsc, rsum, dq_acc = refs[first_scratch:first_scratch + 3]
        b, pr, step_no = pl.program_id(0), pl.program_id(1), pl.program_id(2)
        kj = nq - 1 - step_no
        if ex:
            ex_refs = (refs[6:6 + n_in], refs[6 + n_in + n_res:6 + n_in + n_res + n_out], refs[first_scratch + 3:])

            @pl.when((b == 0) & (pr == 0) & (step_no == 0))
            def _():
                ex.start(*ex_refs)

        lane_s = lax.broadcasted_iota(jnp.int32, (seq, LANES), 1)
        lane = lax.broadcasted_iota(jnp.int32, (tq, LANES), 1)
        rr = lax.broadcasted_iota(jnp.int32, (tq, tq), 0)
        cc = lax.broadcasted_iota(jnp.int32, (tq, tq), 1)

        @pl.when(step_no == 0)
        def _():
            dq_acc[...] = jnp.zeros_like(dq_acc)
            prod = do_ref[...].astype(F32) * o_ref[...]
            d0 = jnp.sum(jnp.where(lane_s < 64, prod, 0.0), axis=1, keepdims=True)
            d1 = jnp.sum(jnp.where(lane_s < 64, 0.0, prod), axis=1, keepdims=True)
            dsc[0] = jnp.broadcast_to(d0, (seq, LANES))
            dsc[1] = jnp.broadcast_to(d1, (seq, LANES))
            if key_bias:
                rsum[...] = jnp.zeros_like(rsum)

        if key_bias:
            @pl.when((pr == 0) & (step_no == 0))
            def _():
                dcb_ref[...] = jnp.zeros_like(dcb_ref)

        vv = v_ref[...]

        def step(qi, carry, masked):
            dkt, dvt, cols = carry
            rows = pl.ds(pl.multiple_of(qi * tq, tq), tq)
            dov = do_ref[rows, :]
            new_dkt, new_cols = [], []
            for hh in range(2):
                qv = q_ref[rows, LANES * hh:LANES * (hh + 1)]
                kv = k_ref[:, LANES * hh:LANES * (hh + 1)]
                dom = jnp.where((lane < 64) if hh == 0 else (lane >= 64), dov, jnp.zeros((), BF16))
                s = _dot_nt(qv, kv)
                if masked:
                    s = jnp.where(cc <= rr, s, NEG)
                p = jnp.exp(s - jnp.tile(lse_ref[rows, LANES * hh:LANES * (hh + 1)], (1, tq // LANES)))
                dp = _dot_nt(dom, vv)
                ds32 = p * (dp - jnp.tile(dsc[hh, rows, :], (1, tq // LANES)))
                col = cols[hh]
                if key_bias:
                    col = col + jnp.sum(ds32, axis=0, keepdims=True)
                    rsum[hh, rows, :] += jnp.broadcast_to(jnp.sum(ds32, axis=1, keepdims=True), (tq, LANES))
                ds = ds32.astype(BF16)
                dvt = dvt + _dot_tn(dom, p.astype(BF16))
                new_dkt.append(dkt[hh] + _dot_tn(qv, ds))
                new_cols.append(col)
                dq_acc[rows, LANES * hh:LANES * (hh + 1)] += _dot(ds, kv)
            return tuple(new_dkt), dvt, tuple(new_cols)

        zt = jnp.zeros((LANES, tq), F32)
        zc = jnp.zeros((1, tq), F32)
        carry = step(kj, ((zt, zt), zt, (zc, zc)), True)
        dkt, dvt, cols = lax.fori_loop(kj + 1, nq, functools.partial(step, masked=False), carry)
        for hh in range(2):
            dk_ref[:, LANES * hh:LANES * (hh + 1)] = dkt[hh].T.astype(dk_ref.dtype)
        dv_ref[...] = dvt.T.astype(dv_ref.dtype)
        if key_bias:
            row_t = lax.broadcasted_iota(jnp.int32, (LANES, tq), 0)
            per_key = jnp.where(row_t == 2 * pr, -cols[0], 0.0) + jnp.where(row_t == 2 * pr + 1, -cols[1], 0.0)
            dcb_ref[pl.ds(pl.multiple_of(kj * tq, tq), tq), :] += per_key.T

        @pl.when(step_no == nq - 1)
        def _():
            dq_ref[...] = dq_acc[...].astype(dq_ref.dtype)
            if key_bias:
                dcb_ref[...] += jnp.where(lane_s == 2 * pr, rsum[0], 0.0) + jnp.where(lane_s == 2 * pr + 1, rsum[1], 0.0)

        if ex:
            @pl.when((b == nb - 1) & (pr == PAIRS - 1) & (step_no == nq - 1))
            def _():
                ex.wait(*ex_refs)

    per_seq = lambda w: pl.BlockSpec((seq, w), lambda b, p, j: (b, p))
    per_blk = lambda w: pl.BlockSpec((tq, w), lambda b, p, j: (b * nq + nq - 1 - j, p))
    res = pl.pallas_call(
        body, name=name, grid=(nb, PAIRS, nq),
        out_shape=(jax.ShapeDtypeStruct((T, 1024), BF16), jax.ShapeDtypeStruct((T, 1024), BF16), jax.ShapeDtypeStruct((T, 512), BF16))
        + ((jax.ShapeDtypeStruct((T, LANES), F32),) if key_bias else ()) + tuple(ex.outs if ex else ()),
        in_specs=[per_seq(2 * LANES), per_blk(2 * LANES), per_blk(LANES), per_seq(LANES), per_seq(LANES), per_seq(2 * LANES)] + [ANY] * n_in,
        out_specs=[per_seq(2 * LANES), per_blk(2 * LANES), per_blk(LANES)]
        + ([pl.BlockSpec((seq, LANES), lambda b, p, j: (b, 0))] if key_bias else []) + [ANY] * n_out,
        scratch_shapes=[pltpu.VMEM((2, seq, LANES), F32), pltpu.VMEM((2, seq, LANES) if key_bias else (2, 8, LANES), F32),
                        pltpu.VMEM((seq, 2 * LANES), F32)]
        + (ex.sems() if ex else []),
        compiler_params=_params(("arbitrary", "arbitrary", "arbitrary")),
    )(q, k, v, o, do, lse, *(ex.ins if ex else ()))
    return list(res[:n_res]), list(res[n_res:])


def _mid(of, om, x, tgt, g_fo, g_mo, g2, g3, w_o, w_g, w_u, w_d, tm):
    T = x.shape[0]

    def body(of_ref, om_ref, x_ref, t_ref, gfo_ref, gmo_ref, g2_ref, g3_ref, wo_ref, wg_ref, wu_ref, wd_ref,
             a_ref, h2_ref, hid_ref, dg_ref, du_ref, dx3_ref, dx2_ref, dof_ref, dom_ref, st_ref):
        i = pl.program_id(0)

        @pl.when(i == 0)
        def _():
            st_ref[...] = jnp.zeros_like(st_ref)

        ofv, omv = of_ref[...], om_ref[...]
        rf, rm = _rms(ofv, FOX_W), _rms(omv, FOX_W)
        fhat, mhat = ofv * rf, omv * rm
        a = jnp.concatenate([fhat * gfo_ref[...], mhat * gmo_ref[...]], axis=1).astype(BF16)
        a_ref[...] = a
        x2 = x_ref[...] + _dot(a, wo_ref[...])
        r2 = _rms(x2, D_MODEL)
        xh2 = x2 * r2
        h2 = (xh2 * g2_ref[...]).astype(BF16)
        h2_ref[...] = h2
        gt = _dot_nt(h2, wg_ref[...])
        up = _dot_nt(h2, wu_ref[...])
        sg = jax.nn.sigmoid(gt)
        sl = gt * sg
        hid = (sl * up).astype(BF16)
        hid_ref[...] = hid
        x3 = x2 + _dot(hid, wd_ref[...])
        r3 = _rms(x3, D_MODEL)
        xh3 = x3 * r3
        diff = xh3 * g3_ref[...] - t_ref[...]
        dy = diff * (1.0 / D_MODEL)
        st_ref[ROW_LOSS:ROW_LOSS + 1, :] += jnp.sum(diff * diff, axis=0, keepdims=True) * (0.5 / D_MODEL)
        st_ref[ROW_FINAL:ROW_FINAL + 1, :] += jnp.sum(dy * xh3, axis=0, keepdims=True)
        dx3 = _rms_bwd(dy, xh3, r3, g3_ref[...], D_MODEL)
        dx3b = dx3.astype(BF16)
        dx3_ref[...] = dx3b
        dhid = _dot_nt(dx3b, wd_ref[...])
        dg = (dhid * up * (sg * (1.0 + gt * (1.0 - sg)))).astype(BF16)
        du = (dhid * sl).astype(BF16)
        dg_ref[...] = dg
        du_ref[...] = du
        dh2 = _dot(dg, wg_ref[...]) + _dot(du, wu_ref[...])
        st_ref[ROW_NORM_FFN:ROW_NORM_FFN + 1, :] += jnp.sum(dh2 * xh2, axis=0, keepdims=True)
        dx2 = dx3 + _rms_bwd(dh2, xh2, r2, g2_ref[...], D_MODEL)
        dx2_ref[...] = dx2
        da = _dot_nt(dx2.astype(BF16), wo_ref[...])
        daf, dam = da[:, 0:FOX_W], da[:, FOX_W:2 * FOX_W]
        st_ref[ROW_OUT:ROW_OUT + 1, 0:FOX_W] += jnp.sum(daf * fhat, axis=0, keepdims=True)
        st_ref[ROW_OUT:ROW_OUT + 1, FOX_W:2 * FOX_W] += jnp.sum(dam * mhat, axis=0, keepdims=True)
        dof_ref[...] = _rms_bwd(daf, fhat, rf, gfo_ref[...], FOX_W).astype(BF16)
        dom_ref[...] = _rms_bwd(dam, mhat, rm, gmo_ref[...], FOX_W).astype(BF16)

    row = lambda w: pl.BlockSpec((tm, w), lambda i: (i, 0))
    ff = jax.ShapeDtypeStruct((T, D_FF), BF16)
    out_shape = (
        jax.ShapeDtypeStruct((T, 1024), BF16), jax.ShapeDtypeStruct((T, 1024), BF16), ff, ff, ff,
        jax.ShapeDtypeStruct((T, 1024), BF16), jax.ShapeDtypeStruct((T, 1024), F32),
        jax.ShapeDtypeStruct((T, 512), BF16), jax.ShapeDtypeStruct((T, 512), BF16), jax.ShapeDtypeStruct((8, 1024), F32),
    )
    return pl.pallas_call(
        body, name="mid", grid=(T // tm,), out_shape=out_shape,
        in_specs=[row(512), row(512), row(1024), row(1024), _full(g_fo.shape), _full(g_mo.shape), _full(g2.shape), _full(g3.shape),
                  _full(w_o.shape), _full(w_g.shape), _full(w_u.shape), _full(w_d.shape)],
        out_specs=[row(1024), row(1024), row(D_FF), row(D_FF), row(D_FF), row(1024), row(1024), row(512), row(512),
                   pl.BlockSpec((8, 1024), lambda i: (0, 0))],
        compiler_params=_params(("arbitrary",)),
    )(of, om, x, tgt, g_fo, g_mo, g2, g3, w_o, w_g, w_u, w_d)


def _in_bwd(dqf, dkf, dvf, dcb, dqm, dkm, dvm, lat, x, dx2, g1, gq, gkv, bfg, ct, st, w_in, w_q12, w_kv, seq, tm):
    T = x.shape[0]
    nblk = T // tm
    nsb = seq // tm

    def body(dqf_ref, dkf_ref, dvf_ref, dcb_ref, dqm_ref, dkm_ref, dvm_ref, lat_ref, x_ref, dx2_ref, g1_ref, gq_ref, gkv_ref, b_ref,
             ct_ref, st_ref, win_ref, wq_ref, wkv_ref, dx_ref, dproj_ref, dq12_ref, dkv_ref, stat_ref, carry):
        i = pl.program_id(0)

        @pl.when(i == 0)
        def _():
            stat_ref[...] = jnp.zeros_like(stat_ref)

        @pl.when(i % nsb == 0)
        def _():
            carry[...] = jnp.zeros_like(carry)

        lane = lax.broadcasted_iota(jnp.int32, (tm, LANES), 1)
        low = lane < 64
        ctv, stv = ct_ref[...], st_ref[...]

        for j in range(PAIRS):
            e, o = 2 * LANES * j, 2 * LANES * j + LANES
            half = lambda ref, c0: jnp.where(low, ref[:, c0:c0 + LANES].astype(F32), 0.0)
            dq = half(dqf_ref, e) + pltpu.roll(half(dqf_ref, o), 64, 1)
            dk = half(dkf_ref, e) + pltpu.roll(half(dkf_ref, o), 64, 1)
            dproj_ref[:, C_FQ + LANES * j:C_FQ + LANES * (j + 1)] = (dq * FOX_SCALE).astype(BF16)
            dproj_ref[:, C_FK + LANES * j:C_FK + LANES * (j + 1)] = dk.astype(BF16)
        dproj_ref[:, C_FV:C_QL] = dvf_ref[...]
        dc = dcb_ref[...]
        rr = lax.broadcasted_iota(jnp.int32, (tm, tm), 0)
        cc = lax.broadcasted_iota(jnp.int32, (tm, tm), 1)
        triu = (cc >= rr).astype(BF16)
        a0, a1, a2 = _split3(dc)
        dlf = _dot(triu, a0) + _dot(triu, a1) + _dot(triu, a2) + carry[0:1, :]
        carry[0:1, :] = dlf[0:1, :]
        misc_a = lat_ref[:, Q_RANK + KV_RANK:Q_RANK + KV_RANK + LANES]
        z = misc_a + b_ref[...]
        dz = jnp.where(lane < HEADS, dlf * jax.nn.sigmoid(-z), 0.0)
        stat_ref[ROW_B:ROW_B + 1, 0:LANES] += jnp.sum(dz, axis=0, keepdims=True)

        cq = (jnp.where(low, 1.0, 0.0) + ctv) * MLA_SCALE
        sq = stv * MLA_SCALE
        dkpe = jnp.zeros((tm, LANES), F32)
        for hd in range(HEADS):
            s0 = LANES * hd
            dqh = dqm_ref[:, s0:s0 + LANES].astype(F32)
            dq12_ref[:, s0:s0 + LANES] = (dqh * cq).astype(BF16)
            dq12_ref[:, 1024 + s0:1024 + s0 + LANES] = (dqh * sq).astype(BF16)
            dkpe = dkpe + dkm_ref[:, s0:s0 + LANES].astype(F32)
        dkv_ref[:, 0:1024] = dkm_ref[...]
        dkv_ref[:, 1024:1536] = dvm_ref[...]
        dproj_ref[:, C_MA:C_END] = (dz + dkpe * ctv + pltpu.roll(dkpe * stv, 32, 1)).astype(BF16)
        dqn = _dot(dq12_ref[...], wq_ref[...])
        dkvn = _dot_nt(dkv_ref[...], wkv_ref[...])
        ql = lat_ref[:, 0:Q_RANK]
        kvl = lat_ref[:, Q_RANK:Q_RANK + KV_RANK]
        rq, rkv = _rms(ql, Q_RANK), _rms(kvl, KV_RANK)
        qhat, kvhat = ql * rq, kvl * rkv
        stat_ref[ROW_Q:ROW_Q + 1, 0:Q_RANK] += jnp.sum(dqn * qhat, axis=0, keepdims=True)
        stat_ref[ROW_KV:ROW_KV + 1, 0:KV_RANK] += jnp.sum(dkvn * kvhat, axis=0, keepdims=True)
        dproj_ref[:, C_QL:C_KVL] = _rms_bwd(dqn, qhat, rq, gq_ref[...], Q_RANK).astype(BF16)
        dproj_ref[:, C_KVL:C_MA] = _rms_bwd(dkvn, kvhat, rkv, gkv_ref[...], KV_RANK).astype(BF16)

        dh1 = _dot(dproj_ref[...], win_ref[...])
        xv = x_ref[...]
        r1 = _rms(xv, D_MODEL)
        xh = xv * r1
        stat_ref[ROW_NORM_MIX:ROW_NORM_MIX + 1, :] += jnp.sum(dh1 * xh, axis=0, keepdims=True)
        dx_ref[...] = dx2_ref[...] + _rms_bwd(dh1, xh, r1, g1_ref[...], D_MODEL)

    rev = lambda w: pl.BlockSpec((tm, w), lambda i: (nblk - 1 - i, 0))
    out_shape = (
        jax.ShapeDtypeStruct((T, 1024), F32), jax.ShapeDtypeStruct((T, C_END), BF16), jax.ShapeDtypeStruct((T, 2048), BF16),
        jax.ShapeDtypeStruct((T, 1536), BF16), jax.ShapeDtypeStruct((8, 1024), F32),
    )
    return pl.pallas_call(
        body, name="in_bwd", grid=(nblk,), out_shape=out_shape,
        in_specs=[rev(1024), rev(1024), rev(512), rev(LANES), rev(1024), rev(1024), rev(512), rev(512), rev(1024), rev(1024),
                  _full(g1.shape), _full(gq.shape), _full(gkv.shape), _full(bfg.shape), rev(LANES), rev(LANES),
                  _full(w_in.shape), _full(w_q12.shape), _full(w_kv.shape)],
        out_specs=[rev(1024), rev(C_END), rev(2048), rev(1536), pl.BlockSpec((8, 1024), lambda i: (0, 0))],
        scratch_shapes=[pltpu.VMEM((8, LANES), F32)],
        compiler_params=_params(("arbitrary",)),
    )(dqf, dkf, dvf, dcb, dqm, dkm, dvm, lat, x, dx2, g1, gq, gkv, bfg, ct, st, w_in, w_q12, w_kv)


def _wgrad(a, b, tk, tt, name, ex=None):
    T, K = a.shape
    N = b.shape[1]
    n_in, n_out = (len(ex.ins), len(ex.outs)) if ex else (0, 0)
    gk, gt = K // tk, T // tt

    def body(*refs):
        a_ref, b_ref, o_ref = refs[0], refs[1], refs[2 + n_in]
        kb, t = pl.program_id(0), pl.program_id(1)
        if ex:
            ex_refs = (refs[2:2 + n_in], refs[3 + n_in:3 + n_in + n_out], refs[3 + n_in + n_out:])

            @pl.when((kb == 0) & (t == 0))
            def _():
                ex.start(*ex_refs)

        @pl.when(t == 0)
        def _():
            o_ref[...] = jnp.zeros_like(o_ref)

        o_ref[...] += _dot_tn(a_ref[...].astype(BF16), b_ref[...].astype(BF16))

        if ex:
            @pl.when((kb == gk - 1) & (t == gt - 1))
            def _():
                ex.wait(*ex_refs)

    res = pl.pallas_call(
        body, name=name, grid=(gk, gt), out_shape=(jax.ShapeDtypeStruct((K, N), F32),) + tuple(ex.outs if ex else ()),
        in_specs=[pl.BlockSpec((tt, tk), lambda kb, t: (t, kb)), pl.BlockSpec((tt, N), lambda kb, t: (t, 0))] + [ANY] * n_in,
        out_specs=[pl.BlockSpec((tk, N), lambda kb, t: (kb, 0))] + [ANY] * n_out,
        scratch_shapes=ex.sems() if ex else [], input_output_aliases=ex.aliases(2, 1) if ex else {},
        compiler_params=_params(("arbitrary", "arbitrary")),
    )(a, b, *(ex.ins if ex else ()))
    return (res[0], list(res[1:])) if ex else res[0]


def _adam_update(w, g, m, v):
    nm = ADAM_B1 * m + (1.0 - ADAM_B1) * g
    nv = ADAM_B2 * v + (1.0 - ADAM_B2) * (g * g)
    m_hat = nm / (1.0 - ADAM_B1 ** ADAM_STEP)
    v_hat = nv / (1.0 - ADAM_B2 ** ADAM_STEP)
    return -ADAM_LR * (m_hat / (jnp.sqrt(v_hat) + ADAM_EPS) + ADAM_WD * w), nm, nv


def _adamw_small(stats, params):
    k = len(SMALL)

    def body(*refs):
        for t, name in enumerate(SMALL):
            row, c0, width = SMALL_AT[name]
            w_ref, m_ref, v_ref = refs[1 + 3 * t:4 + 3 * t]
            g_ref, d_ref, nm_ref, nv_ref = refs[1 + 3 * k + 4 * t:5 + 3 * k + 4 * t]
            g = refs[0][row:row + 1, c0:c0 + width]
            g_ref[...] = g
            d_ref[...], nm_ref[...], nv_ref[...] = _adam_update(w_ref[...], g, m_ref[...], v_ref[...])

    vm = pl.BlockSpec(memory_space=pltpu.VMEM)
    out_shape = tuple(jax.ShapeDtypeStruct((1, SMALL_AT[name][2]), F32) for name in SMALL for _ in range(4))
    res = pl.pallas_call(body, name="adamw_small", out_shape=out_shape, in_specs=[vm] * (1 + 3 * k), out_specs=tuple([vm] * (4 * k)))(
        stats, *[a for name in SMALL for a in params[name]])
    return {name: tuple(res[4 * t:4 * t + 4]) for t, name in enumerate(SMALL)}


def _adamw(tensors, name):
    n = len(tensors)
    R, C = tensors[0][0].shape
    tr = _row_tile(R)

    def body(*refs):
        for t in range(n):
            w_ref, g_ref, m_ref, v_ref = refs[4 * t:4 * t + 4]
            d_ref, nm_ref, nv_ref = refs[4 * n + 3 * t:4 * n + 3 * t + 3]
            d_ref[...], nm_ref[...], nv_ref[...] = _adam_update(w_ref[...], g_ref[...], m_ref[...], v_ref[...])

    blk = pl.BlockSpec((tr, C), lambda i: (i, 0))
    sh = jax.ShapeDtypeStruct((R, C), F32)
    res = pl.pallas_call(
        body, name=name, grid=(R // tr,), out_shape=(sh,) * (3 * n),
        in_specs=[blk] * (4 * n), out_specs=[blk] * (3 * n),
        compiler_params=_params(("arbitrary",)),
    )(*[a for t in tensors for a in t])
    return [tuple(res[3 * t:3 * t + 3]) for t in range(n)]


def _arrange(win_t, wuq_t, wukv):
    dt = win_t.dtype
    z = lambda r: jnp.zeros((r, D_MODEL), dt)
    zh = lambda r: jnp.zeros((HEADS, r, Q_RANK), dt)
    kr1, kr2 = win_t[1928:1944], win_t[1944:1960]
    misc = jnp.concatenate([win_t[1536:1544], z(56), kr1, kr2, kr2, kr1], axis=0)
    w_in = jnp.concatenate([win_t[0:1536], win_t[1544:1928], misc], axis=0)
    wq = wuq_t.reshape(HEADS, 96, Q_RANK)
    q1 = jnp.concatenate([wq, zh(32)], axis=1).reshape(1024, Q_RANK)
    q2 = jnp.concatenate([zh(64), wq[:, 80:96], wq[:, 64:80], zh(32)], axis=1).reshape(1024, Q_RANK)
    wkv = wukv.reshape(KV_RANK, HEADS, 128)
    wk = jnp.concatenate([wkv[:, :, 0:64], jnp.zeros((KV_RANK, HEADS, 64), dt)], axis=2).reshape(KV_RANK, 1024)
    wv = wkv[:, :, 64:128].reshape(KV_RANK, 512)
    return dict(w_in=w_in, w_q12=jnp.concatenate([q1, q2], axis=0), w_k=wk, w_v=wv, w_kv=jnp.concatenate([wk, wv], axis=1))


def _unarrange(g_in, g_q12, g_kv):
    kr1 = g_in[C_MA + 64:C_MA + 80] + g_in[C_MA + 112:C_MA + 128]
    kr2 = g_in[C_MA + 80:C_MA + 96] + g_in[C_MA + 96:C_MA + 112]
    win_t = jnp.concatenate([g_in[0:1536], g_in[C_MA:C_MA + 8], g_in[1536:1920], kr1, kr2], axis=0)
    g1 = g_q12[0:1024].reshape(HEADS, 128, Q_RANK)
    g2 = g_q12[1024:2048].reshape(HEADS, 128, Q_RANK)
    wuq_t = jnp.concatenate([g1[:, 0:64], g1[:, 64:80] + g2[:, 80:96], g1[:, 80:96] + g2[:, 64:80]], axis=1).reshape(768, Q_RANK)
    gk = g_kv[:, 0:1024].reshape(KV_RANK, HEADS, 128)
    gv = g_kv[:, 1024:1536].reshape(KV_RANK, HEADS, 64)
    wukv = jnp.concatenate([gk[:, :, 0:64], gv], axis=2).reshape(KV_RANK, 1024)
    return win_t, wuq_t, wukv


def _selector():
    sel = np.zeros((384, 1024), np.float32)
    for h in range(HEADS):
        for piece in range(3):
            sel[LANES * piece + h, LANES * h + 64 + piece] = 1.0
    return jnp.asarray(sel, BF16)


def _rope_tables(positions):
    inv_freq = 10000.0 ** (-jnp.arange(0, ROPE, 2, dtype=F32) / ROPE)
    n = positions.size
    ang = (positions.reshape(n // 8, 8, 1).astype(F32) * inv_freq[None, None, :]).reshape(n // 8, 8 * (ROPE // 2))
    cos, sin = lax.optimization_barrier((jnp.cos(lax.optimization_barrier(ang)), jnp.sin(lax.optimization_barrier(ang))))
    cos, sin = cos.reshape(n, ROPE // 2), sin.reshape(n, ROPE // 2)
    z64, z32 = jnp.zeros((n, 64), F32), jnp.zeros((n, 32), F32)
    return jnp.concatenate([z64, cos, cos, z32], axis=1), jnp.concatenate([z64, -sin, sin, z32], axis=1)


def _work(name, t):
    return jnp.swapaxes(t[0], 0, 1) if name in TRANSPOSED else t[0]


def _back(name, t):
    return (jnp.swapaxes(t, 0, 1) if name in TRANSPOSED else t)[None]


def kernel(x, positions, norm_mix_g, w_in, b_fgate, q_norm_g, w_uq, kv_norm_g, w_ukv, fox_out_g, mla_out_g, w_o, norm_ffn_g, w_gate, w_up, w_down, final_norm_g, loss_target, m_norm_mix_g, m_w_in, m_b_fgate, m_q_norm_g, m_w_uq, m_kv_norm_g, m_w_ukv, m_fox_out_g, m_mla_out_g, m_w_o, m_norm_ffn_g, m_w_gate, m_w_up, m_w_down, m_final_norm_g, v_norm_mix_g, v_w_in, v_b_fgate, v_q_norm_g, v_w_uq, v_kv_norm_g, v_w_ukv, v_fox_out_g, v_mla_out_g, v_w_o, v_norm_ffn_g, v_w_gate, v_w_up, v_w_down, v_final_norm_g):
    names = ["norm_mix_g", "w_in", "b_fgate", "q_norm_g", "w_uq", "kv_norm_g", "w_ukv", "fox_out_g", "mla_out_g", "w_o",
             "norm_ffn_g", "w_gate", "w_up", "w_down", "final_norm_g"]
    wts = dict(zip(names, [norm_mix_g, w_in, b_fgate, q_norm_g, w_uq, kv_norm_g, w_ukv, fox_out_g, mla_out_g, w_o, norm_ffn_g,
                           w_gate, w_up, w_down, final_norm_g]))
    mom = dict(zip(names, [m_norm_mix_g, m_w_in, m_b_fgate, m_q_norm_g, m_w_uq, m_kv_norm_g, m_w_ukv, m_fox_out_g, m_mla_out_g,
                           m_w_o, m_norm_ffn_g, m_w_gate, m_w_up, m_w_down, m_final_norm_g]))
    var = dict(zip(names, [v_norm_mix_g, v_w_in, v_b_fgate, v_q_norm_g, v_w_uq, v_kv_norm_g, v_w_ukv, v_fox_out_g, v_mla_out_g,
                           v_w_o, v_norm_ffn_g, v_w_gate, v_w_up, v_w_down, v_final_norm_g]))
    shard = {n: _work(n, wts[n]) for n in HEAD3 + FFN4}
    nb, seq, _ = x.shape
    T = nb * seq
    tm, tq = min(ROW_TILE, seq), min(ATTN_TILE, seq)
    tt = min(WGRAD_TILE, T)
    xf = x.reshape(T, D_MODEL)
    tgt = loss_target.reshape(T, D_MODEL)
    chip = 2 * lax.axis_index("x") + lax.axis_index("y")

    mine = [shard[n].astype(BF16) for n in HEAD3]
    head = _run_exchange(_gather_split_exchange(mine), "gather_head")
    win4, wuq4, wukv4 = [lax.dynamic_update_slice(h, s[None], (chip, 0, 0)) for h, s in zip(head, mine)]
    a = _arrange(win4.reshape(-1, D_MODEL), wuq4.reshape(-1, Q_RANK), wukv4.transpose(1, 0, 2).reshape(KV_RANK, -1))
    sel = _selector()
    ct, st = _rope_tables(positions)
    bfg = jnp.concatenate([b_fgate, jnp.zeros((1, LANES - HEADS), F32)], axis=1)
    g1, gq, gkv = norm_mix_g, q_norm_g, kv_norm_g

    h1, qf, kf, vf, qm, km, vm, lat, qn, kvn = _in_proj(xf, g1, a["w_in"], a["w_q12"], a["w_k"], a["w_v"], gq, gkv, bfg, ct, st, sel, seq,
                                                        min(IN_PROJ_TILE, seq))
    tqf = min(ATTN_FWD_TILE, seq)
    of, lse_f, (wo4, wg4) = _attn_fwd(qf, kf, vf, nb, seq, tqf, "fox_fwd", _gather_exchange([shard[n].astype(BF16) for n in FFN4[:2]]))
    om, lse_m, (wu4, wd4) = _attn_fwd(qm, km, vm, nb, seq, tqf, "mla_fwd", _gather_exchange([shard[n].astype(BF16) for n in FFN4[2:]]))
    a_cat, h2, hid, dg, du, dx3, dx2, dof, dom, st_mid = _mid(
        of, om, xf, tgt, fox_out_g, mla_out_g, norm_ffn_g, final_norm_g.reshape(1, D_MODEL),
        wo4.reshape(D_MODEL, D_MODEL), wg4.reshape(D_FF, D_MODEL), wu4.reshape(D_FF, D_MODEL), wd4.reshape(D_FF, D_MODEL), tm)

    slab = lambda g: g.reshape(N_CHIPS, g.shape[0] // N_CHIPS, g.shape[1])
    big = [slab(_wgrad(a_cat, dx2, D_MODEL, tt, "wgrad_o")), slab(_wgrad(dg, h2, D_FF // 2, tt, "wgrad_gate")),
           slab(_wgrad(du, h2, D_FF // 2, tt, "wgrad_up")), slab(_wgrad(hid, dx3, D_FF // 2, tt, "wgrad_down"))]
    (dqf, dkf, dvf, dcb), got = _attn_bwd(qf, kf, vf, of, dof, lse_f, nb, seq, tq, "fox_bwd", True, _swap_exchange(big))
    sums = [_add_half(g, s) for g, s in zip(big, got)]
    (dqm, dkm, dvm), recv = _attn_bwd(qm, km, vm, om, dom, lse_m, nb, seq, tq, "mla_bwd", False, _scatter_exchange(sums))
    halves = [_sum_slabs(g, s, r) for g, s, r in zip(big, got, recv)]
    dx, dproj, dq12, dkv, st_in = _in_bwd(dqf, dkf, dvf, dcb, dqm, dkm, dvm, lat, xf, dx2, g1, gq, gkv, bfg, ct, st,
                                          a["w_in"], a["w_q12"], a["w_kv"], seq, tm)
    g_in, results = _wgrad(dproj, h1, C_END, tt, "wgrad_in", _both(_join_exchange(halves), _everyone_exchange(st_mid + st_in)))
    gshard = dict(zip(FFN4, results[:4]))
    stats = _sum_devices(results[4])

    gwin_t, gwuq_t, gwukv = _unarrange(g_in, _wgrad(dq12, qn, 2048, tt, "wgrad_uq"), _wgrad(kvn, dkv, KV_RANK, tt, "wgrad_ukv"))
    tail = [slab(gwin_t), slab(gwuq_t), gwukv.reshape(KV_RANK, N_CHIPS, -1).transpose(1, 0, 2)]
    tail_got = _run_exchange(_swap_exchange(tail), "tail_swap")
    tail_sums = [_add_half(g, s) for g, s in zip(tail, tail_got)]
    tail_recv = _run_exchange(_scatter_exchange(tail_sums), "tail_scatter")
    tail_joined = _run_exchange(_join_exchange([_sum_slabs(g, s, r) for g, s, r in zip(tail, tail_got, tail_recv)]), "tail_join")
    gshard.update(zip(HEAD3, tail_joined))
    quad = lambda n: (shard[n], gshard[n], _work(n, mom[n]), _work(n, var[n]))
    updates = dict(zip(FFN4[1:], _adamw([quad(n) for n in FFN4[1:]], "adamw_ffn")))
    for n in HEAD3 + FFN4[:1]:
        updates[n], = _adamw([quad(n)], "adamw_" + n)

    grads, delta, new_m, new_v = {}, {}, {}, {}
    for n in HEAD3 + FFN4:
        grads[n] = _back(n, gshard[n])
        delta[n], new_m[n], new_v[n] = [_back(n, t) for t in updates[n]]
    row = lambda t: t.reshape(1, -1)
    small = _adamw_small(stats, {n: (row(wts[n]), row(mom[n]), row(var[n])) for n in SMALL})
    for n in SMALL:
        grads[n], delta[n], new_m[n], new_v[n] = [t.reshape(wts[n].shape) for t in small[n]]
    loss = jnp.sum(stats[ROW_LOSS])
    return (loss, dx.reshape(x.shape), *[grads[n] for n in names], *[delta[n] for n in names],
            *[new_m[n] for n in names], *[new_v[n] for n in names])
```

```python
import functools

import numpy as np
import jax
import jax.numpy as jnp
from jax import lax
from jax.experimental import pallas as pl
from jax.experimental.pallas import tpu as pltpu

F32 = jnp.float32
BF16 = jnp.bfloat16
MESH = pl.DeviceIdType.MESH

EPS = 1e-6
D_MODEL = 1024
HEADS = 8
PAIRS = HEADS // 2
FOX_W = 512
Q_RANK = 256
KV_RANK = 128
ROPE = 32
D_FF = 2816
N_CHIPS = 4
FOX_SCALE = 64 ** -0.5
MLA_SCALE = 96 ** -0.5
LANES = 128
NEG = -1e30

ADAM_LR, ADAM_B1, ADAM_B2, ADAM_EPS, ADAM_WD, ADAM_STEP = 0.001, 0.9, 0.999, 1e-08, 0.01, 10

C_FQ, C_FK, C_FV, C_QL, C_KVL, C_MA, C_END = 0, 512, 1024, 1536, 1792, 1920, 2048
C_MB = C_END

VMEM_LIMIT = 60 * 1024 * 1024
ROW_TILE = 256
IN_PROJ_TILE = 512
ATTN_TILE = 512
ATTN_FWD_TILE = 1024
WGRAD_TILE = 2048

HEAD3 = ("w_in", "w_uq", "w_ukv")
FFN4 = ("w_o", "w_gate", "w_up", "w_down")
TRANSPOSED = ("w_in", "w_uq", "w_gate", "w_up")
SMALL = ("norm_mix_g", "b_fgate", "q_norm_g", "kv_norm_g", "fox_out_g", "mla_out_g", "norm_ffn_g", "final_norm_g")
ROW_NORM_MIX, ROW_NORM_FFN, ROW_FINAL, ROW_OUT, ROW_Q, ROW_KV, ROW_B, ROW_LOSS = range(8)
SMALL_AT = {"norm_mix_g": (ROW_NORM_MIX, 0, 1024), "norm_ffn_g": (ROW_NORM_FFN, 0, 1024), "final_norm_g": (ROW_FINAL, 0, 1024),
            "fox_out_g": (ROW_OUT, 0, 512), "mla_out_g": (ROW_OUT, 512, 512), "q_norm_g": (ROW_Q, 0, 256),
            "kv_norm_g": (ROW_KV, 0, 128), "b_fgate": (ROW_B, 0, 8)}


def _params(sem=None):
    return pltpu.CompilerParams(dimension_semantics=sem, vmem_limit_bytes=VMEM_LIMIT)


def _full(shape):
    n = len(shape)
    return pl.BlockSpec(shape, lambda *_: (0,) * n, pipeline_mode=pl.Buffered(1))


def _dot(a, b):
    return jnp.dot(a, b, preferred_element_type=F32)


def _dot_nt(a, b):
    return lax.dot_general(a, b, (((1,), (1,)), ((), ())), preferred_element_type=F32)


def _dot_tn(a, b):
    return lax.dot_general(a, b, (((0,), (0,)), ((), ())), preferred_element_type=F32)


def _split3(v):
    hi = v.astype(BF16)
    r1 = v - hi.astype(F32)
    mid = r1.astype(BF16)
    lo = (r1 - mid.astype(F32)).astype(BF16)
    return hi, mid, lo


def _rms(v, width):
    return lax.rsqrt(jnp.sum(v * v, axis=1, keepdims=True) * (1.0 / width) + EPS)


def _rms_bwd(dy, xhat, r, g, width):
    u = dy * g
    return r * (u - xhat * (jnp.sum(u * xhat, axis=1, keepdims=True) * (1.0 / width)))


ANY = pl.BlockSpec(memory_space=pl.ANY)


def _place():
    return lax.axis_index("x"), lax.axis_index("y"), lax.axis_index("c")


def _other_chips(x, y):
    return [(1 - x, y), (x, 1 - y), (1 - x, 1 - y)]


def _remote(src, dst, send, recv, j, dev):
    return pltpu.make_async_remote_copy(src_ref=src, dst_ref=dst, send_sem=send.at[j], recv_sem=recv.at[j], device_id=dev, device_id_type=MESH)


class _Exchange:
    def __init__(self, ins, outs, n_remote, n_local, build, in_place=False):
        self.ins, self.outs, self.n_remote, self.n_local, self.build = list(ins), list(outs), n_remote, max(n_local, 1), build
        self.in_place = in_place
        self.n_aliased = len(self.ins)

    def aliases(self, first_in, first_out):
        return {first_in + i: first_out + i for i in range(self.n_aliased)} if self.in_place else {}

    def sems(self):
        return [pltpu.SemaphoreType.DMA((self.n_remote,)), pltpu.SemaphoreType.DMA((self.n_remote,)), pltpu.SemaphoreType.DMA((self.n_local,))]

    def start(self, in_refs, out_refs, sems):
        for cp in self.build(in_refs, out_refs, *sems)[0]:
            cp.start()

    def wait(self, in_refs, out_refs, sems):
        for w in self.build(in_refs, out_refs, *sems)[1]:
            w()


def _gather_exchange(shards):
    def build(ins, outs, send, recv, lsem):
        x, y, c = _place()
        starts, waits = [], []
        for i, (s, o) in enumerate(zip(ins, outs)):
            mine = pltpu.make_async_copy(s, o.at[2 * x + y], lsem.at[i])
            starts.append(mine)
            waits.append(mine.wait)
            for j, (cx, cy) in enumerate(_other_chips(x, y)):
                out = _remote(s, o.at[2 * x + y], send, recv, 3 * i + j, (cx, cy, c))
                starts.append(out)
                waits.append(_remote(s, o.at[2 * cx + cy], send, recv, 3 * i + j, (cx, cy, c)).wait_recv)
                waits.append(out.wait_send)
        return starts, waits

    outs = [jax.ShapeDtypeStruct((N_CHIPS,) + s.shape, s.dtype) for s in shards]
    return _Exchange(shards, outs, 3 * len(shards), len(shards), build)


def _gather_split_exchange(shards):
    n = len(shards)

    def build(ins, outs, send, recv, lsem):
        x, y, c = _place()
        starts, waits, last = [], [], []
        for i, (s, o) in enumerate(zip(ins, outs)):
            hc = s.shape[1] // 2
            mine, other = pl.ds(c * hc, hc), pl.ds((1 - c) * hc, hc)
            for j, (cx, cy) in enumerate(_other_chips(x, y)):
                out = _remote(s.at[:, mine], o.at[2 * x + y, :, mine], send, recv, 3 * i + j, (cx, cy, c))
                landed = o.at[2 * cx + cy, :, mine]
                arrive = _remote(s.at[:, mine], landed, send, recv, 3 * i + j, (cx, cy, c))
                onward = _remote(landed, landed, send, recv, 3 * n + 3 * i + j, (x, y, 1 - c))
                from_sibling = _remote(landed, o.at[2 * cx + cy, :, other], send, recv, 3 * n + 3 * i + j, (x, y, 1 - c))
                starts.append(out)
                waits.append(lambda arrive=arrive, onward=onward: (arrive.wait_recv(), onward.start()))
                last += [from_sibling.wait_recv, onward.wait_send, out.wait_send]
        return starts, waits + last

    outs = [jax.ShapeDtypeStruct((N_CHIPS,) + s.shape, s.dtype) for s in shards]
    return _Exchange(shards, outs, 6 * n, 0, build)


def _swap_exchange(grads):
    def build(ins, outs, send, recv, lsem):
        x, y, c = _place()
        cps = []
        for i, (g, o) in enumerate(zip(ins, outs)):
            hc = g.shape[2] // 2
            cps.append(_remote(g.at[:, :, pl.ds((1 - c) * hc, hc)], o, send, recv, i, (x, y, 1 - c)))
        return cps, [cp.wait for cp in cps]

    outs = [jax.ShapeDtypeStruct((g.shape[0], g.shape[1], g.shape[2] // 2), g.dtype) for g in grads]
    return _Exchange(grads, outs, len(grads), 0, build)


def _scatter_exchange(sums):
    def build(ins, outs, send, recv, lsem):
        x, y, c = _place()
        cps = []
        for i, (s, o) in enumerate(zip(ins, outs)):
            for j, (cx, cy) in enumerate(_other_chips(x, y)):
                cps.append(_remote(s.at[2 * cx + cy], o.at[j], send, recv, 3 * i + j, (cx, cy, c)))
        return cps, [cp.wait for cp in cps]

    outs = [jax.ShapeDtypeStruct((3,) + s.shape[1:], s.dtype) for s in sums]
    return _Exchange(sums, outs, 3 * len(sums), 0, build)


def _join_exchange(bufs):
    def build(ins, outs, send, recv, lsem):
        x, y, c = _place()
        starts, waits = [], []
        for i, (t, o) in enumerate(zip(ins, outs)):
            hc = t.shape[1] // 2
            out = _remote(t.at[:, pl.ds(c * hc, hc)], o.at[:, pl.ds(c * hc, hc)], send, recv, i, (x, y, 1 - c))
            starts.append(out)
            waits += [_remote(t.at[:, pl.ds(c * hc, hc)], o.at[:, pl.ds((1 - c) * hc, hc)], send, recv, i, (x, y, 1 - c)).wait_recv,
                      out.wait_send]
        return starts, waits

    outs = [jax.ShapeDtypeStruct(t.shape, t.dtype) for t in bufs]
    return _Exchange(bufs, outs, len(bufs), 0, build, in_place=True)


def _everyone_exchange(v):
    def build(ins, outs, send, recv, lsem):
        x, y, c = _place()
        me = 4 * x + 2 * y + c
        mine = pltpu.make_async_copy(ins[0], outs[0].at[me], lsem.at[0])
        starts, waits = [mine], [mine.wait]
        for j in range(7):
            fx, fy, fc = (j + 1) >> 2 & 1, (j + 1) >> 1 & 1, (j + 1) & 1
            peer = (x ^ fx, y ^ fy, c ^ fc)
            out = _remote(ins[0], outs[0].at[me], send, recv, j, peer)
            starts.append(out)
            waits += [_remote(ins[0], outs[0].at[4 * peer[0] + 2 * peer[1] + peer[2]], send, recv, j, peer).wait_recv, out.wait_send]
        return starts, waits

    return _Exchange([v], [jax.ShapeDtypeStruct((8,) + v.shape, v.dtype)], 7, 1, build)


def _both(a, b):
    na_in, na_out = len(a.ins), len(a.outs)

    def build(ins, outs, send, recv, lsem):
        sa, wa = a.build(ins[:na_in], outs[:na_out], send.at[pl.ds(0, a.n_remote)], recv.at[pl.ds(0, a.n_remote)],
                         lsem.at[pl.ds(0, a.n_local)])
        sb, wb = b.build(ins[na_in:], outs[na_out:], send.at[pl.ds(a.n_remote, b.n_remote)], recv.at[pl.ds(a.n_remote, b.n_remote)],
                         lsem.at[pl.ds(a.n_local, b.n_local)])
        return sa + sb, wa + wb

    both = _Exchange(a.ins + b.ins, a.outs + b.outs, a.n_remote + b.n_remote, a.n_local + b.n_local, build, in_place=a.in_place)
    both.n_aliased = na_in
    return both


def _run_exchange(ex, name):
    n_in, n_out = len(ex.ins), len(ex.outs)

    def body(*refs):
        ins, outs, sems = refs[:n_in], refs[n_in:n_in + n_out], refs[n_in + n_out:]
        ex.start(ins, outs, sems)
        ex.wait(ins, outs, sems)

    return pl.pallas_call(
        body, name=name, out_shape=tuple(ex.outs), in_specs=[ANY] * n_in, out_specs=tuple([ANY] * n_out),
        scratch_shapes=ex.sems(), input_output_aliases=ex.aliases(0, 0),
        compiler_params=pltpu.CompilerParams(has_side_effects=True),
    )(*ex.ins)


def _sum_devices(rows):
    def body(r_ref, o_ref):
        acc = r_ref[0]
        for d in range(1, 8):
            acc = acc + r_ref[d]
        o_ref[...] = acc

    vm = pl.BlockSpec(memory_space=pltpu.VMEM)
    return pl.pallas_call(body, name="sum_devices", out_shape=jax.ShapeDtypeStruct(rows.shape[1:], rows.dtype),
                          in_specs=[vm], out_specs=vm)(rows)


def _add_half(g, got):
    n, R, C = g.shape
    hc = C // 2

    def body(c_ref, g_ref, r_ref, o_ref):
        o_ref[...] = (g_ref[...] + r_ref[...]).astype(BF16)

    c = lax.axis_index("c")
    return pl.pallas_call(
        body, name="add_half",
        grid_spec=pltpu.PrefetchScalarGridSpec(
            num_scalar_prefetch=1, grid=(n,),
            in_specs=[pl.BlockSpec((1, R, hc), lambda k, c_ref: (k, 0, c_ref[0])),
                      pl.BlockSpec((1, R, hc), lambda k, c_ref: (k, 0, 0))],
            out_specs=pl.BlockSpec((1, R, hc), lambda k, c_ref: (k, 0, 0))),
        out_shape=jax.ShapeDtypeStruct((n, R, hc), BF16),
        compiler_params=_params(("arbitrary",)),
    )(jnp.reshape(c, (1,)).astype(jnp.int32), g, got)


def _sum_slabs(g, got, recv):
    _, R, C = g.shape
    hc = C // 2

    def body(kc_ref, g_ref, s_ref, r_ref, o_ref):
        o_ref[...] = (((g_ref[0] + s_ref[0]) + r_ref[0].astype(F32)) + r_ref[1].astype(F32)) + r_ref[2].astype(F32)

    kc = jnp.stack([2 * lax.axis_index("x") + lax.axis_index("y"), lax.axis_index("c")]).astype(jnp.int32)
    return pl.pallas_call(
        body, name="sum_slabs",
        grid_spec=pltpu.PrefetchScalarGridSpec(
            num_scalar_prefetch=1, grid=(1,),
            in_specs=[pl.BlockSpec((1, R, hc), lambda i, kc_ref: (kc_ref[0], 0, kc_ref[1])),
                      pl.BlockSpec((1, R, hc), lambda i, kc_ref: (kc_ref[0], 0, 0)),
                      pl.BlockSpec((3, R, hc), lambda i, kc_ref: (0, 0, 0))],
            out_specs=pl.BlockSpec((R, hc), lambda i, kc_ref: (0, kc_ref[1]))),
        out_shape=jax.ShapeDtypeStruct((R, C), F32),
        compiler_params=_params(("arbitrary",)),
    )(kc, g, got, recv)


def _row_tile(rows):
    for cand in (256, 184, 176, 144, 128, 64, 32, 16, 8):
        if rows % cand == 0:
            return cand
    return rows


def _in_proj(x, g1, w_in, w_q12, w_k, w_v, gq, gkv, bfg, ct, st, sel, seq, tm):
    T = x.shape[0]
    nsb = seq // tm

    def body(x_ref, g1_ref, win_ref, wq_ref, wk_ref, wv_ref, gq_ref, gkv_ref, b_ref, ct_ref, st_ref, sel_ref,
             h1_ref, qf_ref, kf_ref, vf_ref, qm_ref, km_ref, vm_ref, lat_ref, qn_ref, kvn_ref, carry):
        i = pl.program_id(0)

        @pl.when(i % nsb == 0)
        def _():
            carry[...] = jnp.zeros_like(carry)

        xv = x_ref[...]
        h = (xv * _rms(xv, D_MODEL) * g1_ref[...]).astype(BF16)
        h1_ref[...] = h
        proj = _dot_nt(h, win_ref[...])
        lane = lax.broadcasted_iota(jnp.int32, (tm, LANES), 1)
        low = lane < 64
        misc_a = proj[:, C_MA:C_END]
        misc_b = pltpu.roll(misc_a, 96, 1)

        z = misc_a + b_ref[...]
        lf = jnp.where(lane < HEADS, jnp.minimum(z, 0.0) - jnp.log1p(jnp.exp(-jnp.abs(z))), 0.0)
        rr = lax.broadcasted_iota(jnp.int32, (tm, tm), 0)
        cc = lax.broadcasted_iota(jnp.int32, (tm, tm), 1)
        tri = (rr >= cc).astype(BF16)
        a0, a1, a2 = _split3(lf)
        c = _dot(tri, a0) + _dot(tri, a1) + _dot(tri, a2) + carry[0:1, :]
        carry[0:1, :] = c[tm - 1:tm, :]
        c0, c1, c2 = _split3(c)
        cpl = _dot(jnp.concatenate([c0, c1, c2], axis=1), sel_ref[...])
        qpad = jnp.where((lane >= 64) & (lane < 67), -1.0, 0.0)
        for j in range(PAIRS):
            qc = proj[:, C_FQ + LANES * j:C_FQ + LANES * (j + 1)] * FOX_SCALE
            kc = proj[:, C_FK + LANES * j:C_FK + LANES * (j + 1)]
            e, o = 2 * LANES * j, 2 * LANES * j + LANES
            qf_ref[:, e:e + LANES] = jnp.where(low, qc, qpad).astype(BF16)
            qf_ref[:, o:o + LANES] = jnp.where(low, pltpu.roll(qc, 64, 1), qpad).astype(BF16)
            kf_ref[:, e:e + LANES] = jnp.where(low, kc, cpl[:, e:e + LANES]).astype(BF16)
            kf_ref[:, o:o + LANES] = jnp.where(low, pltpu.roll(kc, 64, 1), cpl[:, o:o + LANES]).astype(BF16)
        vf_ref[...] = proj[:, C_FV:C_QL].astype(BF16)

        ql = proj[:, C_QL:C_KVL]
        kvl = proj[:, C_KVL:C_MA]
        qn = (ql * _rms(ql, Q_RANK) * gq_ref[...]).astype(BF16)
        kvn = (kvl * _rms(kvl, KV_RANK) * gkv_ref[...]).astype(BF16)
        lat_ref[...] = proj[:, C_QL:C_MB]
        qn_ref[...] = qn
        kvn_ref[...] = kvn
        q12 = _dot_nt(qn, wq_ref[...])
        kn = _dot(kvn, wk_ref[...])
        ctv = ct_ref[...]
        stv = st_ref[...]
        cq = (jnp.where(low, 1.0, 0.0) + ctv) * MLA_SCALE
        sq = stv * MLA_SCALE
        kpe = misc_a * ctv + misc_b * stv
        for hd in range(HEADS):
            s0 = LANES * hd
            qm_ref[:, s0:s0 + LANES] = (q12[:, s0:s0 + LANES] * cq + q12[:, 1024 + s0:1024 + s0 + LANES] * sq).astype(BF16)
            km_ref[:, s0:s0 + LANES] = (kn[:, s0:s0 + LANES] + kpe).astype(BF16)
        vm_ref[...] = _dot(kvn, wv_ref[...]).astype(BF16)

    row = lambda w: pl.BlockSpec((tm, w), lambda i: (i, 0))
    out_shape = (
        jax.ShapeDtypeStruct((T, D_MODEL), BF16),
        jax.ShapeDtypeStruct((T, 1024), BF16), jax.ShapeDtypeStruct((T, 1024), BF16), jax.ShapeDtypeStruct((T, 512), BF16),
        jax.ShapeDtypeStruct((T, 1024), BF16), jax.ShapeDtypeStruct((T, 1024), BF16), jax.ShapeDtypeStruct((T, 512), BF16),
        jax.ShapeDtypeStruct((T, 512), F32),
        jax.ShapeDtypeStruct((T, Q_RANK), BF16), jax.ShapeDtypeStruct((T, KV_RANK), BF16),
    )
    return pl.pallas_call(
        body, name="in_proj", grid=(T // tm,), out_shape=out_shape,
        in_specs=[row(D_MODEL), _full(g1.shape), _full(w_in.shape), _full(w_q12.shape), _full(w_k.shape), _full(w_v.shape),
                  _full(gq.shape), _full(gkv.shape), _full(bfg.shape), row(LANES), row(LANES), _full(sel.shape)],
        out_specs=[row(D_MODEL), row(1024), row(1024), row(512), row(1024), row(1024), row(512), row(512), row(Q_RANK), row(KV_RANK)],
        scratch_shapes=[pltpu.VMEM((8, LANES), F32)],
        compiler_params=_params(("arbitrary",)),
    )(x, g1, w_in, w_q12, w_k, w_v, gq, gkv, bfg, ct, st, sel)


def _attn_fwd(q, k, v, nb, seq, tq, name, ex=None):
    T = q.shape[0]
    nq = seq // tq
    n_in, n_out = (len(ex.ins), len(ex.outs)) if ex else (0, 0)

    def body(*refs):
        q_ref, k_ref, v_ref = refs[0:3]
        o_ref, lse_ref = refs[3 + n_in:5 + n_in]
        b, pr, qi = pl.program_id(0), pl.program_id(1), pl.program_id(2)
        if ex:
            ex_refs = (refs[3:3 + n_in], refs[5 + n_in:5 + n_in + n_out], refs[8 + n_in + n_out:])

            @pl.when((b == 0) & (pr == 0) & (qi == 0))
            def _():
                ex.start(*ex_refs)

        s_sc, p_sc, acc_sc = refs[5 + n_in + n_out:8 + n_in + n_out]
        strip = 64
        key_s = lax.broadcasted_iota(jnp.int32, (strip, tq), 0)
        qry_s = lax.broadcasted_iota(jnp.int32, (strip, tq), 1)
        row_t = lax.broadcasted_iota(jnp.int32, (LANES, tq), 0)
        acc_sc[...] = jnp.zeros(acc_sc.shape, F32)

        def fold(x, op):
            out = x[0:8]
            for r in range(8, strip, 8):
                out = op(out, x[r:r + 8])
            return out

        def step(kj, state, masked):
            rows = pl.ds(pl.multiple_of(kj * tq, tq), tq)
            for hh in range(2):
                s_sc[hh] = _dot_nt(k_ref[rows, LANES * hh:LANES * (hh + 1)], q_ref[:, LANES * hh:LANES * (hh + 1)])
            vv = v_ref[rows, :]
            new = []
            for hh in range(2):
                m, l = state[hh]

                def strip_of(r0, hh=hh):
                    s = s_sc[hh, r0:r0 + strip, :]
                    return jnp.where(key_s + r0 <= qry_s, s, NEG) if masked else s

                mx = fold(strip_of(0), jnp.maximum)
                for r0 in range(strip, tq, strip):
                    mx = jnp.maximum(mx, fold(strip_of(r0), jnp.maximum))
                m_new = jnp.maximum(m, jnp.max(mx, axis=0, keepdims=True))
                alpha = jnp.exp(m - m_new)
                sm = jnp.zeros((8, tq), F32)
                for r0 in range(0, tq, strip):
                    p = jnp.exp(strip_of(r0) - m_new)
                    sm = sm + fold(p, jnp.add)
                    p_sc[hh, r0:r0 + strip, :] = p.astype(BF16)
                l = alpha * l + jnp.sum(sm, axis=0, keepdims=True)
                acc_sc[hh] = alpha * acc_sc[hh] + _dot_tn(vv, p_sc[hh])
                new.append((m_new, l))
            return tuple(new)

        one = (jnp.full((1, tq), NEG, F32), jnp.zeros((1, tq), F32))
        state = lax.fori_loop(0, qi, functools.partial(step, masked=False), (one, one))
        (m0, l0), (m1, l1) = step(qi, state, True)
        o_ref[...] = jnp.where(row_t < 64, acc_sc[0] / l0, acc_sc[1] / l1).T
        lse_ref[:, 0:LANES] = jnp.broadcast_to(m0 + jnp.log(l0), (LANES, tq)).T
        lse_ref[:, LANES:2 * LANES] = jnp.broadcast_to(m1 + jnp.log(l1), (LANES, tq)).T

        if ex:
            @pl.when((b == nb - 1) & (pr == PAIRS - 1) & (qi == nq - 1))
            def _():
                ex.wait(*ex_refs)

    res = pl.pallas_call(
        body, name=name, grid=(nb, PAIRS, nq),
        out_shape=(jax.ShapeDtypeStruct((T, 512), F32), jax.ShapeDtypeStruct((T, 1024), F32)) + tuple(ex.outs if ex else ()),
        in_specs=[pl.BlockSpec((tq, 2 * LANES), lambda b, p, i: (b * nq + i, p)),
                  pl.BlockSpec((seq, 2 * LANES), lambda b, p, i: (b, p)),
                  pl.BlockSpec((seq, LANES), lambda b, p, i: (b, p))] + [ANY] * n_in,
        out_specs=[pl.BlockSpec((tq, LANES), lambda b, p, i: (b * nq + i, p)),
                   pl.BlockSpec((tq, 2 * LANES), lambda b, p, i: (b * nq + i, p))] + [ANY] * n_out,
        scratch_shapes=[pltpu.VMEM((2, tq, tq), F32), pltpu.VMEM((2, tq, tq), BF16), pltpu.VMEM((2, LANES, tq), F32)]
        + (ex.sems() if ex else []),
        compiler_params=_params(("arbitrary", "arbitrary", "arbitrary")),
    )(q, k, v, *(ex.ins if ex else ()))
    return res[0], res[1], list(res[2:])


def _attn_bwd(q, k, v, o, do, lse, nb, seq, tq, name, key_bias, ex=None):
    T = q.shape[0]
    nq = seq // tq
    n_in, n_out = (len(ex.ins), len(ex.outs)) if ex else (0, 0)
    n_res = 4 if key_bias else 3

    def body(*refs):
        q_ref, k_ref, v_ref, o_ref, do_ref, lse_ref = refs[0:6]
        dq_ref, dk_ref, dv_ref = refs[6 + n_in:9 + n_in]
        dcb_ref = refs[9 + n_in] if key_bias else None
        first_scratch = 6 + n_in + n_res + n_out
        dsc, rsum, dq_acc = refs[first_scratch:first_scratch + 3]
        b, pr, step_no = pl.program_id(0), pl.program_id(1), pl.program_id(2)
        kj = nq - 1 - step_no
        if ex:
            ex_refs = (refs[6:6 + n_in], refs[6 + n_in + n_res:6 + n_in + n_res + n_out], refs[first_scratch + 3:])

            @pl.when((b == 0) & (pr == 0) & (step_no == 0))
            def _():
                ex.start(*ex_refs)

        lane_s = lax.broadcasted_iota(jnp.int32, (seq, LANES), 1)
        lane = lax.broadcasted_iota(jnp.int32, (tq, LANES), 1)
        rr = lax.broadcasted_iota(jnp.int32, (tq, tq), 0)
        cc = lax.broadcasted_iota(jnp.int32, (tq, tq), 1)

        @pl.when(step_no == 0)
        def _():
            dq_acc[...] = jnp.zeros_like(dq_acc)
            prod = do_ref[...].astype(F32) * o_ref[...]
            d0 = jnp.sum(jnp.where(lane_s < 64, prod, 0.0), axis=1, keepdims=True)
            d1 = jnp.sum(jnp.where(lane_s < 64, 0.0, prod), axis=1, keepdims=True)
            dsc[0] = jnp.broadcast_to(d0, (seq, LANES))
            dsc[1] = jnp.broadcast_to(d1, (seq, LANES))
            if key_bias:
                rsum[...] = jnp.zeros_like(rsum)

        if key_bias:
            @pl.when((pr == 0) & (step_no == 0))
            def _():
                dcb_ref[...] = jnp.zeros_like(dcb_ref)

        vv = v_ref[...]

        def step(qi, carry, masked):
            dkt, dvt, cols = carry
            rows = pl.ds(pl.multiple_of(qi * tq, tq), tq)
            dov = do_ref[rows, :]
            new_dkt, new_cols = [], []
            for hh in range(2):
                qv = q_ref[rows, LANES * hh:LANES * (hh + 1)]
                kv = k_ref[:, LANES * hh:LANES * (hh + 1)]
                dom = jnp.where((lane < 64) if hh == 0 else (lane >= 64), dov, jnp.zeros((), BF16))
                s = _dot_nt(qv, kv)
                if masked:
                    s = jnp.where(cc <= rr, s, NEG)
                p = jnp.exp(s - jnp.tile(lse_ref[rows, LANES * hh:LANES * (hh + 1)], (1, tq // LANES)))
                dp = _dot_nt(dom, vv)
                ds32 = p * (dp - jnp.tile(dsc[hh, rows, :], (1, tq // LANES)))
                col = cols[hh]
                if key_bias:
                    col = col + jnp.sum(ds32, axis=0, keepdims=True)
                    rsum[hh, rows, :] += jnp.broadcast_to(jnp.sum(ds32, axis=1, keepdims=True), (tq, LANES))
                ds = ds32.astype(BF16)
                dvt = dvt + _dot_tn(dom, p.astype(BF16))
                new_dkt.append(dkt[hh] + _dot_tn(qv, ds))
                new_cols.append(col)
                dq_acc[rows, LANES * hh:LANES * (hh + 1)] += _dot(ds, kv)
            return tuple(new_dkt), dvt, tuple(new_cols)

        zt = jnp.zeros((LANES, tq), F32)
        zc = jnp.zeros((1, tq), F32)
        carry = step(kj, ((zt, zt), zt, (zc, zc)), True)
        dkt, dvt, cols = lax.fori_loop(kj + 1, nq, functools.partial(step, masked=False), carry)
        for hh in range(2):
            dk_ref[:, LANES * hh:LANES * (hh + 1)] = dkt[hh].T.astype(dk_ref.dtype)
        dv_ref[...] = dvt.T.astype(dv_ref.dtype)
        if key_bias:
            row_t = lax.broadcasted_iota(jnp.int32, (LANES, tq), 0)
            per_key = jnp.where(row_t == 2 * pr, -cols[0], 0.0) + jnp.where(row_t == 2 * pr + 1, -cols[1], 0.0)
            dcb_ref[pl.ds(pl.multiple_of(kj * tq, tq), tq), :] += per_key.T

        @pl.when(step_no == nq - 1)
        def _():
            dq_ref[...] = dq_acc[...].astype(dq_ref.dtype)
            if key_bias:
                dcb_ref[...] += jnp.where(lane_s == 2 * pr, rsum[0], 0.0) + jnp.where(lane_s == 2 * pr + 1, rsum[1], 0.0)

        if ex:
            @pl.when((b == nb - 1) & (pr == PAIRS - 1) & (step_no == nq - 1))
            def _():
                ex.wait(*ex_refs)

    per_seq = lambda w: pl.BlockSpec((seq, w), lambda b, p, j: (b, p))
    per_blk = lambda w: pl.BlockSpec((tq, w), lambda b, p, j: (b * nq + nq - 1 - j, p))
    res = pl.pallas_call(
        body, name=name, grid=(nb, PAIRS, nq),
        out_shape=(jax.ShapeDtypeStruct((T, 1024), BF16), jax.ShapeDtypeStruct((T, 1024), BF16), jax.ShapeDtypeStruct((T, 512), BF16))
        + ((jax.ShapeDtypeStruct((T, LANES), F32),) if key_bias else ()) + tuple(ex.outs if ex else ()),
        in_specs=[per_seq(2 * LANES), per_blk(2 * LANES), per_blk(LANES), per_seq(LANES), per_seq(LANES), per_seq(2 * LANES)] + [ANY] * n_in,
        out_specs=[per_seq(2 * LANES), per_blk(2 * LANES), per_blk(LANES)]
        + ([pl.BlockSpec((seq, LANES), lambda b, p, j: (b, 0))] if key_bias else []) + [ANY] * n_out,
        scratch_shapes=[pltpu.VMEM((2, seq, LANES), F32), pltpu.VMEM((2, seq, LANES) if key_bias else (2, 8, LANES), F32),
                        pltpu.VMEM((seq, 2 * LANES), F32)]
        + (ex.sems() if ex else []),
        compiler_params=_params(("arbitrary", "arbitrary", "arbitrary")),
    )(q, k, v, o, do, lse, *(ex.ins if ex else ()))
    return list(res[:n_res]), list(res[n_res:])


def _mid(of, om, x, tgt, g_fo, g_mo, g2, g3, w_o, w_g, w_u, w_d, tm):
    T = x.shape[0]

    def body(of_ref, om_ref, x_ref, t_ref, gfo_ref, gmo_ref, g2_ref, g3_ref, wo_ref, wg_ref, wu_ref, wd_ref,
             a_ref, h2_ref, hid_ref, dg_ref, du_ref, dx3_ref, dx2_ref, dof_ref, dom_ref, st_ref):
        i = pl.program_id(0)

        @pl.when(i == 0)
        def _():
            st_ref[...] = jnp.zeros_like(st_ref)

        ofv, omv = of_ref[...], om_ref[...]
        rf, rm = _rms(ofv, FOX_W), _rms(omv, FOX_W)
        fhat, mhat = ofv * rf, omv * rm
        a = jnp.concatenate([fhat * gfo_ref[...], mhat * gmo_ref[...]], axis=1).astype(BF16)
        a_ref[...] = a
        x2 = x_ref[...] + _dot(a, wo_ref[...])
        r2 = _rms(x2, D_MODEL)
        xh2 = x2 * r2
        h2 = (xh2 * g2_ref[...]).astype(BF16)
        h2_ref[...] = h2
        gt = _dot_nt(h2, wg_ref[...])
        up = _dot_nt(h2, wu_ref[...])
        sg = jax.nn.sigmoid(gt)
        sl = gt * sg
        hid = (sl * up).astype(BF16)
        hid_ref[...] = hid
        x3 = x2 + _dot(hid, wd_ref[...])
        r3 = _rms(x3, D_MODEL)
        xh3 = x3 * r3
        diff = xh3 * g3_ref[...] - t_ref[...]
        dy = diff * (1.0 / D_MODEL)
        st_ref[ROW_LOSS:ROW_LOSS + 1, :] += jnp.sum(diff * diff, axis=0, keepdims=True) * (0.5 / D_MODEL)
        st_ref[ROW_FINAL:ROW_FINAL + 1, :] += jnp.sum(dy * xh3, axis=0, keepdims=True)
        dx3 = _rms_bwd(dy, xh3, r3, g3_ref[...], D_MODEL)
        dx3b = dx3.astype(BF16)
        dx3_ref[...] = dx3b
        dhid = _dot_nt(dx3b, wd_ref[...])
        dg = (dhid * up * (sg * (1.0 + gt * (1.0 - sg)))).astype(BF16)
        du = (dhid * sl).astype(BF16)
        dg_ref[...] = dg
        du_ref[...] = du
        dh2 = _dot(dg, wg_ref[...]) + _dot(du, wu_ref[...])
        st_ref[ROW_NORM_FFN:ROW_NORM_FFN + 1, :] += jnp.sum(dh2 * xh2, axis=0, keepdims=True)
        dx2 = dx3 + _rms_bwd(dh2, xh2, r2, g2_ref[...], D_MODEL)
        dx2_ref[...] = dx2
        da = _dot_nt(dx2.astype(BF16), wo_ref[...])
        daf, dam = da[:, 0:FOX_W], da[:, FOX_W:2 * FOX_W]
        st_ref[ROW_OUT:ROW_OUT + 1, 0:FOX_W] += jnp.sum(daf * fhat, axis=0, keepdims=True)
        st_ref[ROW_OUT:ROW_OUT + 1, FOX_W:2 * FOX_W] += jnp.sum(dam * mhat, axis=0, keepdims=True)
        dof_ref[...] = _rms_bwd(daf, fhat, rf, gfo_ref[...], FOX_W).astype(BF16)
        dom_ref[...] = _rms_bwd(dam, mhat, rm, gmo_ref[...], FOX_W).astype(BF16)

    row = lambda w: pl.BlockSpec((tm, w), lambda i: (i, 0))
    ff = jax.ShapeDtypeStruct((T, D_FF), BF16)
    out_shape = (
        jax.ShapeDtypeStruct((T, 1024), BF16), jax.ShapeDtypeStruct((T, 1024), BF16), ff, ff, ff,
        jax.ShapeDtypeStruct((T, 1024), BF16), jax.ShapeDtypeStruct((T, 1024), F32),
        jax.ShapeDtypeStruct((T, 512), BF16), jax.ShapeDtypeStruct((T, 512), BF16), jax.ShapeDtypeStruct((8, 1024), F32),
    )
    return pl.pallas_call(
        body, name="mid", grid=(T // tm,), out_shape=out_shape,
        in_specs=[row(512), row(512), row(1024), row(1024), _full(g_fo.shape), _full(g_mo.shape), _full(g2.shape), _full(g3.shape),
                  _full(w_o.shape), _full(w_g.shape), _full(w_u.shape), _full(w_d.shape)],
        out_specs=[row(1024), row(1024), row(D_FF), row(D_FF), row(D_FF), row(1024), row(1024), row(512), row(512),
                   pl.BlockSpec((8, 1024), lambda i: (0, 0))],
        compiler_params=_params(("arbitrary",)),
    )(of, om, x, tgt, g_fo, g_mo, g2, g3, w_o, w_g, w_u, w_d)


def _in_bwd(dqf, dkf, dvf, dcb, dqm, dkm, dvm, lat, qn, kvn, x, dx2, g1, gq, gkv, bfg, ct, st, w_in, w_q12, w_kv, seq, tm):
    T = x.shape[0]
    nblk = T // tm
    nsb = seq // tm

    def body(dqf_ref, dkf_ref, dvf_ref, dcb_ref, dqm_ref, dkm_ref, dvm_ref, lat_ref, qn_ref, kvn_ref, x_ref, dx2_ref, g1_ref, gq_ref,
             gkv_ref, b_ref, ct_ref, st_ref, win_ref, wq_ref, wkv_ref, dx_ref, dproj_ref, gq12_ref, gkv12_ref, stat_ref, carry,
             dq12_ref, dkv_ref):
        i = pl.program_id(0)

        @pl.when(i == 0)
        def _():
            stat_ref[...] = jnp.zeros_like(stat_ref)
            gq12_ref[...] = jnp.zeros_like(gq12_ref)
            gkv12_ref[...] = jnp.zeros_like(gkv12_ref)

        @pl.when(i % nsb == 0)
        def _():
            carry[...] = jnp.zeros_like(carry)

        lane = lax.broadcasted_iota(jnp.int32, (tm, LANES), 1)
        low = lane < 64
        ctv, stv = ct_ref[...], st_ref[...]

        for j in range(PAIRS):
            e, o = 2 * LANES * j, 2 * LANES * j + LANES
            half = lambda ref, c0: jnp.where(low, ref[:, c0:c0 + LANES].astype(F32), 0.0)
            dq = half(dqf_ref, e) + pltpu.roll(half(dqf_ref, o), 64, 1)
            dk = half(dkf_ref, e) + pltpu.roll(half(dkf_ref, o), 64, 1)
            dproj_ref[:, C_FQ + LANES * j:C_FQ + LANES * (j + 1)] = (dq * FOX_SCALE).astype(BF16)
            dproj_ref[:, C_FK + LANES * j:C_FK + LANES * (j + 1)] = dk.astype(BF16)
        dproj_ref[:, C_FV:C_QL] = dvf_ref[...]
        dc = dcb_ref[...]
        rr = lax.broadcasted_iota(jnp.int32, (tm, tm), 0)
        cc = lax.broadcasted_iota(jnp.int32, (tm, tm), 1)
        triu = (cc >= rr).astype(BF16)
        a0, a1, a2 = _split3(dc)
        dlf = _dot(triu, a0) + _dot(triu, a1) + _dot(triu, a2) + carry[0:1, :]
        carry[0:1, :] = dlf[0:1, :]
        misc_a = lat_ref[:, Q_RANK + KV_RANK:Q_RANK + KV_RANK + LANES]
        z = misc_a + b_ref[...]
        dz = jnp.where(lane < HEADS, dlf * jax.nn.sigmoid(-z), 0.0)
        stat_ref[ROW_B:ROW_B + 1, 0:LANES] += jnp.sum(dz, axis=0, keepdims=True)

        cq = (jnp.where(low, 1.0, 0.0) + ctv) * MLA_SCALE
        sq = stv * MLA_SCALE
        dkpe = jnp.zeros((tm, LANES), F32)
        for hd in range(HEADS):
            s0 = LANES * hd
            dqh = dqm_ref[:, s0:s0 + LANES].astype(F32)
            dq12_ref[:, s0:s0 + LANES] = (dqh * cq).astype(BF16)
            dq12_ref[:, 1024 + s0:1024 + s0 + LANES] = (dqh * sq).astype(BF16)
            dkpe = dkpe + dkm_ref[:, s0:s0 + LANES].astype(F32)
        dkv_ref[:, 0:1024] = dkm_ref[...]
        dkv_ref[:, 1024:1536] = dvm_ref[...]
        dproj_ref[:, C_MA:C_END] = (dz + dkpe * ctv + pltpu.roll(dkpe * stv, 32, 1)).astype(BF16)
        dqn = _dot(dq12_ref[...], wq_ref[...])
        dkvn = _dot_nt(dkv_ref[...], wkv_ref[...])
        gq12_ref[...] += _dot_tn(dq12_ref[...], qn_ref[...])
        gkv12_ref[...] += _dot_tn(kvn_ref[...], dkv_ref[...])
        ql = lat_ref[:, 0:Q_RANK]
        kvl = lat_ref[:, Q_RANK:Q_RANK + KV_RANK]
        rq, rkv = _rms(ql, Q_RANK), _rms(kvl, KV_RANK)
        qhat, kvhat = ql * rq, kvl * rkv
        stat_ref[ROW_Q:ROW_Q + 1, 0:Q_RANK] += jnp.sum(dqn * qhat, axis=0, keepdims=True)
        stat_ref[ROW_KV:ROW_KV + 1, 0:KV_RANK] += jnp.sum(dkvn * kvhat, axis=0, keepdims=True)
        dproj_ref[:, C_QL:C_KVL] = _rms_bwd(dqn, qhat, rq, gq_ref[...], Q_RANK).astype(BF16)
        dproj_ref[:, C_KVL:C_MA] = _rms_bwd(dkvn, kvhat, rkv, gkv_ref[...], KV_RANK).astype(BF16)

        dh1 = _dot(dproj_ref[...], win_ref[...])
        xv = x_ref[...]
        r1 = _rms(xv, D_MODEL)
        xh = xv * r1
        stat_ref[ROW_NORM_MIX:ROW_NORM_MIX + 1, :] += jnp.sum(dh1 * xh, axis=0, keepdims=True)
        dx_ref[...] = dx2_ref[...] + _rms_bwd(dh1, xh, r1, g1_ref[...], D_MODEL)

    rev = lambda w: pl.BlockSpec((tm, w), lambda i: (nblk - 1 - i, 0))
    whole = lambda r, c: pl.BlockSpec((r, c), lambda i: (0, 0))
    out_shape = (
        jax.ShapeDtypeStruct((T, 1024), F32), jax.ShapeDtypeStruct((T, C_END), BF16), jax.ShapeDtypeStruct((2048, Q_RANK), F32),
        jax.ShapeDtypeStruct((KV_RANK, 1536), F32), jax.ShapeDtypeStruct((8, 1024), F32),
    )
    return pl.pallas_call(
        body, name="in_bwd", grid=(nblk,), out_shape=out_shape,
        in_specs=[rev(1024), rev(1024), rev(512), rev(LANES), rev(1024), rev(1024), rev(512), rev(512), rev(Q_RANK), rev(KV_RANK),
                  rev(1024), rev(1024), _full(g1.shape), _full(gq.shape), _full(gkv.shape), _full(bfg.shape), rev(LANES), rev(LANES),
                  _full(w_in.shape), _full(w_q12.shape), _full(w_kv.shape)],
        out_specs=[rev(1024), rev(C_END), whole(2048, Q_RANK), whole(KV_RANK, 1536), whole(8, 1024)],
        scratch_shapes=[pltpu.VMEM((8, LANES), F32), pltpu.VMEM((tm, 2048), BF16), pltpu.VMEM((tm, 1536), BF16)],
        compiler_params=_params(("arbitrary",)),
    )(dqf, dkf, dvf, dcb, dqm, dkm, dvm, lat, qn, kvn, x, dx2, g1, gq, gkv, bfg, ct, st, w_in, w_q12, w_kv)


def _wgrad(a, b, tk, tt, name, ex=None):
    T, K = a.shape
    N = b.shape[1]
    n_in, n_out = (len(ex.ins), len(ex.outs)) if ex else (0, 0)
    gk, gt = K // tk, T // tt

    def body(*refs):
        a_ref, b_ref, o_ref = refs[0], refs[1], refs[2 + n_in]
        kb, t = pl.program_id(0), pl.program_id(1)
        if ex:
            ex_refs = (refs[2:2 + n_in], refs[3 + n_in:3 + n_in + n_out], refs[3 + n_in + n_out:])

            @pl.when((kb == 0) & (t == 0))
            def _():
                ex.start(*ex_refs)

        @pl.when(t == 0)
        def _():
            o_ref[...] = jnp.zeros_like(o_ref)

        o_ref[...] += _dot_tn(a_ref[...].astype(BF16), b_ref[...].astype(BF16))

        if ex:
            @pl.when((kb == gk - 1) & (t == gt - 1))
            def _():
                ex.wait(*ex_refs)

    res = pl.pallas_call(
        body, name=name, grid=(gk, gt), out_shape=(jax.ShapeDtypeStruct((K, N), F32),) + tuple(ex.outs if ex else ()),
        in_specs=[pl.BlockSpec((tt, tk), lambda kb, t: (t, kb)), pl.BlockSpec((tt, N), lambda kb, t: (t, 0))] + [ANY] * n_in,
        out_specs=[pl.BlockSpec((tk, N), lambda kb, t: (kb, 0))] + [ANY] * n_out,
        scratch_shapes=ex.sems() if ex else [], input_output_aliases=ex.aliases(2, 1) if ex else {},
        compiler_params=_params(("arbitrary", "arbitrary")),
    )(a, b, *(ex.ins if ex else ()))
    return (res[0], list(res[1:])) if ex else res[0]


def _adam_update(w, g, m, v):
    nm = ADAM_B1 * m + (1.0 - ADAM_B1) * g
    nv = ADAM_B2 * v + (1.0 - ADAM_B2) * (g * g)
    m_hat = nm / (1.0 - ADAM_B1 ** ADAM_STEP)
    v_hat = nv / (1.0 - ADAM_B2 ** ADAM_STEP)
    return -ADAM_LR * (m_hat / (jnp.sqrt(v_hat) + ADAM_EPS) + ADAM_WD * w), nm, nv


def _adamw_small(stats, params):
    k = len(SMALL)

    def body(*refs):
        for t, name in enumerate(SMALL):
            row, c0, width = SMALL_AT[name]
            w_ref, m_ref, v_ref = refs[1 + 3 * t:4 + 3 * t]
            g_ref, d_ref, nm_ref, nv_ref = refs[1 + 3 * k + 4 * t:5 + 3 * k + 4 * t]
            g = refs[0][row:row + 1, c0:c0 + width]
            g_ref[...] = g
            d_ref[...], nm_ref[...], nv_ref[...] = _adam_update(w_ref[...], g, m_ref[...], v_ref[...])

    vm = pl.BlockSpec(memory_space=pltpu.VMEM)
    out_shape = tuple(jax.ShapeDtypeStruct((1, SMALL_AT[name][2]), F32) for name in SMALL for _ in range(4))
    res = pl.pallas_call(body, name="adamw_small", out_shape=out_shape, in_specs=[vm] * (1 + 3 * k), out_specs=tuple([vm] * (4 * k)))(
        stats, *[a for name in SMALL for a in params[name]])
    return {name: tuple(res[4 * t:4 * t + 4]) for t, name in enumerate(SMALL)}


def _adamw(tensors, name):
    n = len(tensors)
    R, C = tensors[0][0].shape
    tr = _row_tile(R)

    def body(*refs):
        for t in range(n):
            w_ref, g_ref, m_ref, v_ref = refs[4 * t:4 * t + 4]
            d_ref, nm_ref, nv_ref = refs[4 * n + 3 * t:4 * n + 3 * t + 3]
            d_ref[...], nm_ref[...], nv_ref[...] = _adam_update(w_ref[...], g_ref[...], m_ref[...], v_ref[...])

    blk = pl.BlockSpec((tr, C), lambda i: (i, 0))
    sh = jax.ShapeDtypeStruct((R, C), F32)
    res = pl.pallas_call(
        body, name=name, grid=(R // tr,), out_shape=(sh,) * (3 * n),
        in_specs=[blk] * (4 * n), out_specs=[blk] * (3 * n),
        compiler_params=_params(("arbitrary",)),
    )(*[a for t in tensors for a in t])
    return [tuple(res[3 * t:3 * t + 3]) for t in range(n)]


def _arrange(win_t, wuq_t, wukv):
    dt = win_t.dtype
    z = lambda r: jnp.zeros((r, D_MODEL), dt)
    zh = lambda r: jnp.zeros((HEADS, r, Q_RANK), dt)
    kr1, kr2 = win_t[1928:1944], win_t[1944:1960]
    misc = jnp.concatenate([win_t[1536:1544], z(56), kr1, kr2, kr2, kr1], axis=0)
    w_in = jnp.concatenate([win_t[0:1536], win_t[1544:1928], misc], axis=0)
    wq = wuq_t.reshape(HEADS, 96, Q_RANK)
    q1 = jnp.concatenate([wq, zh(32)], axis=1).reshape(1024, Q_RANK)
    q2 = jnp.concatenate([zh(64), wq[:, 80:96], wq[:, 64:80], zh(32)], axis=1).reshape(1024, Q_RANK)
    wkv = wukv.reshape(KV_RANK, HEADS, 128)
    wk = jnp.concatenate([wkv[:, :, 0:64], jnp.zeros((KV_RANK, HEADS, 64), dt)], axis=2).reshape(KV_RANK, 1024)
    wv = wkv[:, :, 64:128].reshape(KV_RANK, 512)
    return dict(w_in=w_in, w_q12=jnp.concatenate([q1, q2], axis=0), w_k=wk, w_v=wv, w_kv=jnp.concatenate([wk, wv], axis=1))


def _unarrange(g_in, g_q12, g_kv):
    kr1 = g_in[C_MA + 64:C_MA + 80] + g_in[C_MA + 112:C_MA + 128]
    kr2 = g_in[C_MA + 80:C_MA + 96] + g_in[C_MA + 96:C_MA + 112]
    win_t = jnp.concatenate([g_in[0:1536], g_in[C_MA:C_MA + 8], g_in[1536:1920], kr1, kr2], axis=0)
    g1 = g_q12[0:1024].reshape(HEADS, 128, Q_RANK)
    g2 = g_q12[1024:2048].reshape(HEADS, 128, Q_RANK)
    wuq_t = jnp.concatenate([g1[:, 0:64], g1[:, 64:80] + g2[:, 80:96], g1[:, 80:96] + g2[:, 64:80]], axis=1).reshape(768, Q_RANK)
    gk = g_kv[:, 0:1024].reshape(KV_RANK, HEADS, 128)
    gv = g_kv[:, 1024:1536].reshape(KV_RANK, HEADS, 64)
    wukv = jnp.concatenate([gk[:, :, 0:64], gv], axis=2).reshape(KV_RANK, 1024)
    return win_t, wuq_t, wukv


def _selector():
    sel = np.zeros((384, 1024), np.float32)
    for h in range(HEADS):
        for piece in range(3):
            sel[LANES * piece + h, LANES * h + 64 + piece] = 1.0
    return jnp.asarray(sel, BF16)


def _rope_tables(positions):
    inv_freq = 10000.0 ** (-jnp.arange(0, ROPE, 2, dtype=F32) / ROPE)
    n = positions.size
    ang = (positions.reshape(n // 8, 8, 1).astype(F32) * inv_freq[None, None, :]).reshape(n // 8, 8 * (ROPE // 2))
    cos, sin = lax.optimization_barrier((jnp.cos(lax.optimization_barrier(ang)), jnp.sin(lax.optimization_barrier(ang))))
    cos, sin = cos.reshape(n, ROPE // 2), sin.reshape(n, ROPE // 2)
    z64, z32 = jnp.zeros((n, 64), F32), jnp.zeros((n, 32), F32)
    return jnp.concatenate([z64, cos, cos, z32], axis=1), jnp.concatenate([z64, -sin, sin, z32], axis=1)


def _work(name, t):
    return jnp.swapaxes(t[0], 0, 1) if name in TRANSPOSED else t[0]


def _back(name, t):
    return (jnp.swapaxes(t, 0, 1) if name in TRANSPOSED else t)[None]


def kernel(x, positions, norm_mix_g, w_in, b_fgate, q_norm_g, w_uq, kv_norm_g, w_ukv, fox_out_g, mla_out_g, w_o, norm_ffn_g, w_gate, w_up, w_down, final_norm_g, loss_target, m_norm_mix_g, m_w_in, m_b_fgate, m_q_norm_g, m_w_uq, m_kv_norm_g, m_w_ukv, m_fox_out_g, m_mla_out_g, m_w_o, m_norm_ffn_g, m_w_gate, m_w_up, m_w_down, m_final_norm_g, v_norm_mix_g, v_w_in, v_b_fgate, v_q_norm_g, v_w_uq, v_kv_norm_g, v_w_ukv, v_fox_out_g, v_mla_out_g, v_w_o, v_norm_ffn_g, v_w_gate, v_w_up, v_w_down, v_final_norm_g):
    names = ["norm_mix_g", "w_in", "b_fgate", "q_norm_g", "w_uq", "kv_norm_g", "w_ukv", "fox_out_g", "mla_out_g", "w_o",
             "norm_ffn_g", "w_gate", "w_up", "w_down", "final_norm_g"]
    wts = dict(zip(names, [norm_mix_g, w_in, b_fgate, q_norm_g, w_uq, kv_norm_g, w_ukv, fox_out_g, mla_out_g, w_o, norm_ffn_g,
                           w_gate, w_up, w_down, final_norm_g]))
    mom = dict(zip(names, [m_norm_mix_g, m_w_in, m_b_fgate, m_q_norm_g, m_w_uq, m_kv_norm_g, m_w_ukv, m_fox_out_g, m_mla_out_g,
                           m_w_o, m_norm_ffn_g, m_w_gate, m_w_up, m_w_down, m_final_norm_g]))
    var = dict(zip(names, [v_norm_mix_g, v_w_in, v_b_fgate, v_q_norm_g, v_w_uq, v_kv_norm_g, v_w_ukv, v_fox_out_g, v_mla_out_g,
                           v_w_o, v_norm_ffn_g, v_w_gate, v_w_up, v_w_down, v_final_norm_g]))
    shard = {n: _work(n, wts[n]) for n in HEAD3 + FFN4}
    nb, seq, _ = x.shape
    T = nb * seq
    tm, tq = min(ROW_TILE, seq), min(ATTN_TILE, seq)
    tt = min(WGRAD_TILE, T)
    xf = x.reshape(T, D_MODEL)
    tgt = loss_target.reshape(T, D_MODEL)
    chip = 2 * lax.axis_index("x") + lax.axis_index("y")

    mine = [shard[n].astype(BF16) for n in HEAD3]
    head = _run_exchange(_gather_split_exchange(mine), "gather_head")
    win4, wuq4, wukv4 = [lax.dynamic_update_slice(h, s[None], (chip, 0, 0)) for h, s in zip(head, mine)]
    a = _arrange(win4.reshape(-1, D_MODEL), wuq4.reshape(-1, Q_RANK), wukv4.transpose(1, 0, 2).reshape(KV_RANK, -1))
    sel = _selector()
    ct, st = _rope_tables(positions)
    bfg = jnp.concatenate([b_fgate, jnp.zeros((1, LANES - HEADS), F32)], axis=1)
    g1, gq, gkv = norm_mix_g, q_norm_g, kv_norm_g

    h1, qf, kf, vf, qm, km, vm, lat, qn, kvn = _in_proj(xf, g1, a["w_in"], a["w_q12"], a["w_k"], a["w_v"], gq, gkv, bfg, ct, st, sel, seq,
                                                        min(IN_PROJ_TILE, seq))
    tqf = min(ATTN_FWD_TILE, seq)
    of, lse_f, (wo4, wg4) = _attn_fwd(qf, kf, vf, nb, seq, tqf, "fox_fwd", _gather_exchange([shard[n].astype(BF16) for n in FFN4[:2]]))
    om, lse_m, (wu4, wd4) = _attn_fwd(qm, km, vm, nb, seq, tqf, "mla_fwd", _gather_exchange([shard[n].astype(BF16) for n in FFN4[2:]]))
    a_cat, h2, hid, dg, du, dx3, dx2, dof, dom, st_mid = _mid(
        of, om, xf, tgt, fox_out_g, mla_out_g, norm_ffn_g, final_norm_g.reshape(1, D_MODEL),
        wo4.reshape(D_MODEL, D_MODEL), wg4.reshape(D_FF, D_MODEL), wu4.reshape(D_FF, D_MODEL), wd4.reshape(D_FF, D_MODEL), tm)

    slab = lambda g: g.reshape(N_CHIPS, g.shape[0] // N_CHIPS, g.shape[1])
    big = [slab(_wgrad(a_cat, dx2, D_MODEL, tt, "wgrad_o")), slab(_wgrad(dg, h2, D_FF // 2, tt, "wgrad_gate")),
           slab(_wgrad(du, h2, D_FF // 2, tt, "wgrad_up")), slab(_wgrad(hid, dx3, D_FF // 2, tt, "wgrad_down"))]
    (dqf, dkf, dvf, dcb), got = _attn_bwd(qf, kf, vf, of, dof, lse_f, nb, seq, tq, "fox_bwd", True, _swap_exchange(big))
    sums = [_add_half(g, s) for g, s in zip(big, got)]
    (dqm, dkm, dvm), recv = _attn_bwd(qm, km, vm, om, dom, lse_m, nb, seq, tq, "mla_bwd", False, _scatter_exchange(sums))
    halves = [_sum_slabs(g, s, r) for g, s, r in zip(big, got, recv)]
    dx, dproj, g_q12, g_kv, st_in = _in_bwd(dqf, dkf, dvf, dcb, dqm, dkm, dvm, lat, qn, kvn, xf, dx2, g1, gq, gkv, bfg, ct, st,
                                            a["w_in"], a["w_q12"], a["w_kv"], seq, tm)
    g_in, results = _wgrad(dproj, h1, C_END, tt, "wgrad_in", _both(_join_exchange(halves), _everyone_exchange(st_mid + st_in)))
    gshard = dict(zip(FFN4, results[:4]))
    stats = _sum_devices(results[4])

    gwin_t, gwuq_t, gwukv = _unarrange(g_in, g_q12, g_kv)
    tail = [slab(gwin_t), slab(gwuq_t), gwukv.reshape(KV_RANK, N_CHIPS, -1).transpose(1, 0, 2)]
    tail_got = _run_exchange(_swap_exchange(tail), "tail_swap")
    tail_sums = [_add_half(g, s) for g, s in zip(tail, tail_got)]
    tail_recv = _run_exchange(_scatter_exchange(tail_sums), "tail_scatter")
    tail_joined = _run_exchange(_join_exchange([_sum_slabs(g, s, r) for g, s, r in zip(tail, tail_got, tail_recv)]), "tail_join")
    gshard.update(zip(HEAD3, tail_joined))
    quad = lambda n: (shard[n], gshard[n], _work(n, mom[n]), _work(n, var[n]))
    updates = dict(zip(FFN4[1:], _adamw([quad(n) for n in FFN4[1:]], "adamw_ffn")))
    for n in HEAD3 + FFN4[:1]:
        updates[n], = _adamw([quad(n)], "adamw_" + n)

    grads, delta, new_m, new_v = {}, {}, {}, {}
    for n in HEAD3 + FFN4:
        grads[n] = _back(n, gshard[n])
        delta[n], new_m[n], new_v[n] = [_back(n, t) for t in updates[n]]
    row = lambda t: t.reshape(1, -1)
    small = _adamw_small(stats, {n: (row(wts[n]), row(mom[n]), row(var[n])) for n in SMALL})
    for n in SMALL:
        grads[n], delta[n], new_m[n], new_v[n] = [t.reshape(wts[n].shape) for t in small[n]]
    loss = jnp.sum(stats[ROW_LOSS])
    return (loss, dx.reshape(x.shape), *[grads[n] for n in names], *[delta[n] for n in names],
            *[new_m[n] for n in names], *[new_v[n] for n in names])
```

```python
import functools

import numpy as np
import jax
import jax.numpy as jnp
from jax import lax
from jax.experimental import pallas as pl
from jax.experimental.pallas import tpu as pltpu

F32 = jnp.float32
BF16 = jnp.bfloat16
MESH = pl.DeviceIdType.MESH

EPS = 1e-6
D_MODEL = 1024
HEADS = 8
PAIRS = HEADS // 2
FOX_W = 512
Q_RANK = 256
KV_RANK = 128
ROPE = 32
D_FF = 2816
FF_SPLIT = 1536
N_CHIPS = 4
FOX_SCALE = 64 ** -0.5
MLA_SCALE = 96 ** -0.5
LANES = 128
NEG = -1e30

ADAM_LR, ADAM_B1, ADAM_B2, ADAM_EPS, ADAM_WD, ADAM_STEP = 0.001, 0.9, 0.999, 1e-08, 0.01, 10

C_FQ, C_FK, C_FV, C_QL, C_KVL, C_MA, C_END = 0, 512, 1024, 1536, 1792, 1920, 2048
C_MB = C_END

VMEM_LIMIT = 60 * 1024 * 1024
ROW_TILE = 256
IN_PROJ_TILE = 512
ATTN_TILE = 512
ATTN_FWD_TILE = 1024
WGRAD_TILE = 2048

HEAD3 = ("w_in", "w_uq", "w_ukv")
FFN4 = ("w_o", "w_gate", "w_up", "w_down")
TRANSPOSED = ("w_in", "w_uq", "w_gate", "w_up")
SMALL = ("norm_mix_g", "b_fgate", "q_norm_g", "kv_norm_g", "fox_out_g", "mla_out_g", "norm_ffn_g", "final_norm_g")
ROW_NORM_MIX, ROW_NORM_FFN, ROW_FINAL, ROW_OUT, ROW_Q, ROW_KV, ROW_B, ROW_LOSS = range(8)
SMALL_AT = {"norm_mix_g": (ROW_NORM_MIX, 0, 1024), "norm_ffn_g": (ROW_NORM_FFN, 0, 1024), "final_norm_g": (ROW_FINAL, 0, 1024),
            "fox_out_g": (ROW_OUT, 0, 512), "mla_out_g": (ROW_OUT, 512, 512), "q_norm_g": (ROW_Q, 0, 256),
            "kv_norm_g": (ROW_KV, 0, 128), "b_fgate": (ROW_B, 0, 8)}


def _params(sem=None):
    return pltpu.CompilerParams(dimension_semantics=sem, vmem_limit_bytes=VMEM_LIMIT)


def _full(shape):
    n = len(shape)
    return pl.BlockSpec(shape, lambda *_: (0,) * n, pipeline_mode=pl.Buffered(1))


def _dot(a, b):
    return jnp.dot(a, b, preferred_element_type=F32)


def _dot_nt(a, b):
    return lax.dot_general(a, b, (((1,), (1,)), ((), ())), preferred_element_type=F32)


def _dot_tn(a, b):
    return lax.dot_general(a, b, (((0,), (0,)), ((), ())), preferred_element_type=F32)


def _split3(v):
    hi = v.astype(BF16)
    r1 = v - hi.astype(F32)
    mid = r1.astype(BF16)
    lo = (r1 - mid.astype(F32)).astype(BF16)
    return hi, mid, lo


def _rms(v, width):
    return lax.rsqrt(jnp.sum(v * v, axis=1, keepdims=True) * (1.0 / width) + EPS)


def _rms_bwd(dy, xhat, r, g, width):
    u = dy * g
    return r * (u - xhat * (jnp.sum(u * xhat, axis=1, keepdims=True) * (1.0 / width)))


ANY = pl.BlockSpec(memory_space=pl.ANY)


def _place():
    return lax.axis_index("x"), lax.axis_index("y"), lax.axis_index("c")


def _other_chips(x, y):
    return [(1 - x, y), (x, 1 - y), (1 - x, 1 - y)]


def _remote(src, dst, send, recv, j, dev):
    return pltpu.make_async_remote_copy(src_ref=src, dst_ref=dst, send_sem=send.at[j], recv_sem=recv.at[j], device_id=dev, device_id_type=MESH)


class _Exchange:
    def __init__(self, ins, outs, n_remote, n_local, build, in_place=False):
        self.ins, self.outs, self.n_remote, self.n_local, self.build = list(ins), list(outs), n_remote, max(n_local, 1), build
        self.in_place = in_place
        self.n_aliased = len(self.ins)

    def aliases(self, first_in, first_out):
        return {first_in + i: first_out + i for i in range(self.n_aliased)} if self.in_place else {}

    def sems(self):
        return [pltpu.SemaphoreType.DMA((self.n_remote,)), pltpu.SemaphoreType.DMA((self.n_remote,)), pltpu.SemaphoreType.DMA((self.n_local,))]

    def start(self, in_refs, out_refs, sems):
        for cp in self.build(in_refs, out_refs, *sems)[0]:
            cp.start()

    def wait(self, in_refs, out_refs, sems):
        for w in self.build(in_refs, out_refs, *sems)[1]:
            w()


def _gather_exchange(shards):
    def build(ins, outs, send, recv, lsem):
        x, y, c = _place()
        starts, waits = [], []
        for i, (s, o) in enumerate(zip(ins, outs)):
            mine = pltpu.make_async_copy(s, o.at[2 * x + y], lsem.at[i])
            starts.append(mine)
            waits.append(mine.wait)
            for j, (cx, cy) in enumerate(_other_chips(x, y)):
                out = _remote(s, o.at[2 * x + y], send, recv, 3 * i + j, (cx, cy, c))
                starts.append(out)
                waits.append(_remote(s, o.at[2 * cx + cy], send, recv, 3 * i + j, (cx, cy, c)).wait_recv)
                waits.append(out.wait_send)
        return starts, waits

    outs = [jax.ShapeDtypeStruct((N_CHIPS,) + s.shape, s.dtype) for s in shards]
    return _Exchange(shards, outs, 3 * len(shards), len(shards), build)


def _gather_split_exchange(shards):
    n = len(shards)

    def build(ins, outs, send, recv, lsem):
        x, y, c = _place()
        starts, waits, last = [], [], []
        for i, (s, o) in enumerate(zip(ins, outs)):
            hc = s.shape[1] // 2
            mine, other = pl.ds(c * hc, hc), pl.ds((1 - c) * hc, hc)
            for j, (cx, cy) in enumerate(_other_chips(x, y)):
                out = _remote(s.at[:, mine], o.at[2 * x + y, :, mine], send, recv, 3 * i + j, (cx, cy, c))
                landed = o.at[2 * cx + cy, :, mine]
                arrive = _remote(s.at[:, mine], landed, send, recv, 3 * i + j, (cx, cy, c))
                onward = _remote(landed, landed, send, recv, 3 * n + 3 * i + j, (x, y, 1 - c))
                from_sibling = _remote(landed, o.at[2 * cx + cy, :, other], send, recv, 3 * n + 3 * i + j, (x, y, 1 - c))
                starts.append(out)
                waits.append(lambda arrive=arrive, onward=onward: (arrive.wait_recv(), onward.start()))
                last += [from_sibling.wait_recv, onward.wait_send, out.wait_send]
        return starts, waits + last

    outs = [jax.ShapeDtypeStruct((N_CHIPS,) + s.shape, s.dtype) for s in shards]
    return _Exchange(shards, outs, 6 * n, 0, build)


def _swap_exchange(grads):
    def build(ins, outs, send, recv, lsem):
        x, y, c = _place()
        cps = []
        for i, (g, o) in enumerate(zip(ins, outs)):
            hc = g.shape[2] // 2
            cps.append(_remote(g.at[:, :, pl.ds((1 - c) * hc, hc)], o, send, recv, i, (x, y, 1 - c)))
        return cps, [cp.wait for cp in cps]

    outs = [jax.ShapeDtypeStruct((g.shape[0], g.shape[1], g.shape[2] // 2), g.dtype) for g in grads]
    return _Exchange(grads, outs, len(grads), 0, build)


def _scatter_exchange(sums):
    def build(ins, outs, send, recv, lsem):
        x, y, c = _place()
        cps = []
        for i, (s, o) in enumerate(zip(ins, outs)):
            for j, (cx, cy) in enumerate(_other_chips(x, y)):
                cps.append(_remote(s.at[2 * cx + cy], o.at[j], send, recv, 3 * i + j, (cx, cy, c)))
        return cps, [cp.wait for cp in cps]

    outs = [jax.ShapeDtypeStruct((3,) + s.shape[1:], s.dtype) for s in sums]
    return _Exchange(sums, outs, 3 * len(sums), 0, build)


def _join_exchange(bufs):
    def build(ins, outs, send, recv, lsem):
        x, y, c = _place()
        starts, waits = [], []
        for i, (t, o) in enumerate(zip(ins, outs)):
            hc = t.shape[1] // 2
            out = _remote(t.at[:, pl.ds(c * hc, hc)], o.at[:, pl.ds(c * hc, hc)], send, recv, i, (x, y, 1 - c))
            starts.append(out)
            waits += [_remote(t.at[:, pl.ds(c * hc, hc)], o.at[:, pl.ds((1 - c) * hc, hc)], send, recv, i, (x, y, 1 - c)).wait_recv,
                      out.wait_send]
        return starts, waits

    outs = [jax.ShapeDtypeStruct(t.shape, t.dtype) for t in bufs]
    return _Exchange(bufs, outs, len(bufs), 0, build, in_place=True)


def _everyone_exchange(v):
    def build(ins, outs, send, recv, lsem):
        x, y, c = _place()
        me = 4 * x + 2 * y + c
        mine = pltpu.make_async_copy(ins[0], outs[0].at[me], lsem.at[0])
        starts, waits = [mine], [mine.wait]
        for j in range(7):
            fx, fy, fc = (j + 1) >> 2 & 1, (j + 1) >> 1 & 1, (j + 1) & 1
            peer = (x ^ fx, y ^ fy, c ^ fc)
            out = _remote(ins[0], outs[0].at[me], send, recv, j, peer)
            starts.append(out)
            waits += [_remote(ins[0], outs[0].at[4 * peer[0] + 2 * peer[1] + peer[2]], send, recv, j, peer).wait_recv, out.wait_send]
        return starts, waits

    return _Exchange([v], [jax.ShapeDtypeStruct((8,) + v.shape, v.dtype)], 7, 1, build)


def _both(a, b):
    na_in, na_out = len(a.ins), len(a.outs)

    def build(ins, outs, send, recv, lsem):
        sa, wa = a.build(ins[:na_in], outs[:na_out], send.at[pl.ds(0, a.n_remote)], recv.at[pl.ds(0, a.n_remote)],
                         lsem.at[pl.ds(0, a.n_local)])
        sb, wb = b.build(ins[na_in:], outs[na_out:], send.at[pl.ds(a.n_remote, b.n_remote)], recv.at[pl.ds(a.n_remote, b.n_remote)],
                         lsem.at[pl.ds(a.n_local, b.n_local)])
        return sa + sb, wa + wb

    both = _Exchange(a.ins + b.ins, a.outs + b.outs, a.n_remote + b.n_remote, a.n_local + b.n_local, build, in_place=a.in_place)
    both.n_aliased = na_in
    return both


def _run_exchange(ex, name):
    n_in, n_out = len(ex.ins), len(ex.outs)

    def body(*refs):
        ins, outs, sems = refs[:n_in], refs[n_in:n_in + n_out], refs[n_in + n_out:]
        ex.start(ins, outs, sems)
        ex.wait(ins, outs, sems)

    return pl.pallas_call(
        body, name=name, out_shape=tuple(ex.outs), in_specs=[ANY] * n_in, out_specs=tuple([ANY] * n_out),
        scratch_shapes=ex.sems(), input_output_aliases=ex.aliases(0, 0),
        compiler_params=pltpu.CompilerParams(has_side_effects=True),
    )(*ex.ins)


def _sum_devices(rows):
    def body(r_ref, o_ref):
        acc = r_ref[0]
        for d in range(1, 8):
            acc = acc + r_ref[d]
        o_ref[...] = acc

    vm = pl.BlockSpec(memory_space=pltpu.VMEM)
    return pl.pallas_call(body, name="sum_devices", out_shape=jax.ShapeDtypeStruct(rows.shape[1:], rows.dtype),
                          in_specs=[vm], out_specs=vm)(rows)


def _add_half(g, got):
    n, R, C = g.shape
    hc = C // 2

    def body(c_ref, g_ref, r_ref, o_ref):
        o_ref[...] = (g_ref[...] + r_ref[...]).astype(BF16)

    c = lax.axis_index("c")
    return pl.pallas_call(
        body, name="add_half",
        grid_spec=pltpu.PrefetchScalarGridSpec(
            num_scalar_prefetch=1, grid=(n,),
            in_specs=[pl.BlockSpec((1, R, hc), lambda k, c_ref: (k, 0, c_ref[0])),
                      pl.BlockSpec((1, R, hc), lambda k, c_ref: (k, 0, 0))],
            out_specs=pl.BlockSpec((1, R, hc), lambda k, c_ref: (k, 0, 0))),
        out_shape=jax.ShapeDtypeStruct((n, R, hc), BF16),
        compiler_params=_params(("arbitrary",)),
    )(jnp.reshape(c, (1,)).astype(jnp.int32), g, got)


def _sum_slabs(g, got, recv):
    _, R, C = g.shape
    hc = C // 2

    def body(kc_ref, g_ref, s_ref, r_ref, o_ref):
        o_ref[...] = (((g_ref[0] + s_ref[0]) + r_ref[0].astype(F32)) + r_ref[1].astype(F32)) + r_ref[2].astype(F32)

    kc = jnp.stack([2 * lax.axis_index("x") + lax.axis_index("y"), lax.axis_index("c")]).astype(jnp.int32)
    return pl.pallas_call(
        body, name="sum_slabs",
        grid_spec=pltpu.PrefetchScalarGridSpec(
            num_scalar_prefetch=1, grid=(1,),
            in_specs=[pl.BlockSpec((1, R, hc), lambda i, kc_ref: (kc_ref[0], 0, kc_ref[1])),
                      pl.BlockSpec((1, R, hc), lambda i, kc_ref: (kc_ref[0], 0, 0)),
                      pl.BlockSpec((3, R, hc), lambda i, kc_ref: (0, 0, 0))],
            out_specs=pl.BlockSpec((R, hc), lambda i, kc_ref: (0, kc_ref[1]))),
        out_shape=jax.ShapeDtypeStruct((R, C), F32),
        compiler_params=_params(("arbitrary",)),
    )(kc, g, got, recv)


def _row_tile(rows):
    for cand in (256, 184, 176, 144, 128, 64, 32, 16, 8):
        if rows % cand == 0:
            return cand
    return rows


def _in_proj(x, g1, w_in, w_q12, w_k, w_v, gq, gkv, bfg, ct, st, sel, seq, tm):
    T = x.shape[0]
    nsb = seq // tm

    def body(x_ref, g1_ref, win_ref, wq_ref, wk_ref, wv_ref, gq_ref, gkv_ref, b_ref, ct_ref, st_ref, sel_ref,
             h1_ref, qf_ref, kf_ref, vf_ref, qm_ref, km_ref, vm_ref, lat_ref, qn_ref, kvn_ref, carry):
        i = pl.program_id(0)

        @pl.when(i % nsb == 0)
        def _():
            carry[...] = jnp.zeros_like(carry)

        xv = x_ref[...]
        h = (xv * _rms(xv, D_MODEL) * g1_ref[...]).astype(BF16)
        h1_ref[...] = h
        proj = _dot_nt(h, win_ref[...])
        lane = lax.broadcasted_iota(jnp.int32, (tm, LANES), 1)
        low = lane < 64
        misc_a = proj[:, C_MA:C_END]
        misc_b = pltpu.roll(misc_a, 96, 1)

        z = misc_a + b_ref[...]
        lf = jnp.where(lane < HEADS, jnp.minimum(z, 0.0) - jnp.log1p(jnp.exp(-jnp.abs(z))), 0.0)
        rr = lax.broadcasted_iota(jnp.int32, (tm, tm), 0)
        cc = lax.broadcasted_iota(jnp.int32, (tm, tm), 1)
        tri = (rr >= cc).astype(BF16)
        a0, a1, a2 = _split3(lf)
        c = _dot(tri, a0) + _dot(tri, a1) + _dot(tri, a2) + carry[0:1, :]
        carry[0:1, :] = c[tm - 1:tm, :]
        c0, c1, c2 = _split3(c)
        cpl = _dot(jnp.concatenate([c0, c1, c2], axis=1), sel_ref[...])
        qpad = jnp.where((lane >= 64) & (lane < 67), -1.0, 0.0)
        for j in range(PAIRS):
            qc = proj[:, C_FQ + LANES * j:C_FQ + LANES * (j + 1)] * FOX_SCALE
            kc = proj[:, C_FK + LANES * j:C_FK + LANES * (j + 1)]
            e, o = 2 * LANES * j, 2 * LANES * j + LANES
            qf_ref[:, e:e + LANES] = jnp.where(low, qc, qpad).astype(BF16)
            qf_ref[:, o:o + LANES] = jnp.where(low, pltpu.roll(qc, 64, 1), qpad).astype(BF16)
            kf_ref[:, e:e + LANES] = jnp.where(low, kc, cpl[:, e:e + LANES]).astype(BF16)
            kf_ref[:, o:o + LANES] = jnp.where(low, pltpu.roll(kc, 64, 1), cpl[:, o:o + LANES]).astype(BF16)
        vf_ref[...] = proj[:, C_FV:C_QL].astype(BF16)

        ql = proj[:, C_QL:C_KVL]
        kvl = proj[:, C_KVL:C_MA]
        qn = (ql * _rms(ql, Q_RANK) * gq_ref[...]).astype(BF16)
        kvn = (kvl * _rms(kvl, KV_RANK) * gkv_ref[...]).astype(BF16)
        lat_ref[...] = proj[:, C_QL:C_MB]
        qn_ref[...] = qn
        kvn_ref[...] = kvn
        q12 = _dot_nt(qn, wq_ref[...])
        kn = _dot(kvn, wk_ref[...])
        ctv = ct_ref[...]
        stv = st_ref[...]
        cq = (jnp.where(low, 1.0, 0.0) + ctv) * MLA_SCALE
        sq = stv * MLA_SCALE
        kpe = misc_a * ctv + misc_b * stv
        for hd in range(HEADS):
            s0 = LANES * hd
            qm_ref[:, s0:s0 + LANES] = (q12[:, s0:s0 + LANES] * cq + q12[:, 1024 + s0:1024 + s0 + LANES] * sq).astype(BF16)
            km_ref[:, s0:s0 + LANES] = (kn[:, s0:s0 + LANES] + kpe).astype(BF16)
        vm_ref[...] = _dot(kvn, wv_ref[...]).astype(BF16)

    row = lambda w: pl.BlockSpec((tm, w), lambda i: (i, 0))
    out_shape = (
        jax.ShapeDtypeStruct((T, D_MODEL), BF16),
        jax.ShapeDtypeStruct((T, 1024), BF16), jax.ShapeDtypeStruct((T, 1024), BF16), jax.ShapeDtypeStruct((T, 512), BF16),
        jax.ShapeDtypeStruct((T, 1024), BF16), jax.ShapeDtypeStruct((T, 1024), BF16), jax.ShapeDtypeStruct((T, 512), BF16),
        jax.ShapeDtypeStruct((T, 512), F32),
        jax.ShapeDtypeStruct((T, Q_RANK), BF16), jax.ShapeDtypeStruct((T, KV_RANK), BF16),
    )
    return pl.pallas_call(
        body, name="in_proj", grid=(T // tm,), out_shape=out_shape,
        in_specs=[row(D_MODEL), _full(g1.shape), _full(w_in.shape), _full(w_q12.shape), _full(w_k.shape), _full(w_v.shape),
                  _full(gq.shape), _full(gkv.shape), _full(bfg.shape), row(LANES), row(LANES), _full(sel.shape)],
        out_specs=[row(D_MODEL), row(1024), row(1024), row(512), row(1024), row(1024), row(512), row(512), row(Q_RANK), row(KV_RANK)],
        scratch_shapes=[pltpu.VMEM((8, LANES), F32)],
        compiler_params=_params(("arbitrary",)),
    )(x, g1, w_in, w_q12, w_k, w_v, gq, gkv, bfg, ct, st, sel)


def _attn_fwd(q, k, v, nb, seq, tq, name, ex=None):
    T = q.shape[0]
    nq = seq // tq
    n_in, n_out = (len(ex.ins), len(ex.outs)) if ex else (0, 0)

    def body(*refs):
        q_ref, k_ref, v_ref = refs[0:3]
        o_ref, lse_ref = refs[3 + n_in:5 + n_in]
        b, pr, qi = pl.program_id(0), pl.program_id(1), pl.program_id(2)
        if ex:
            ex_refs = (refs[3:3 + n_in], refs[5 + n_in:5 + n_in + n_out], refs[8 + n_in + n_out:])

            @pl.when((b == 0) & (pr == 0) & (qi == 0))
            def _():
                ex.start(*ex_refs)

        s_sc, p_sc, acc_sc = refs[5 + n_in + n_out:8 + n_in + n_out]
        strip = 64
        key_s = lax.broadcasted_iota(jnp.int32, (strip, tq), 0)
        qry_s = lax.broadcasted_iota(jnp.int32, (strip, tq), 1)
        row_t = lax.broadcasted_iota(jnp.int32, (LANES, tq), 0)
        acc_sc[...] = jnp.zeros(acc_sc.shape, F32)

        def fold(x, op):
            out = x[0:8]
            for r in range(8, strip, 8):
                out = op(out, x[r:r + 8])
            return out

        def step(kj, state, masked):
            rows = pl.ds(pl.multiple_of(kj * tq, tq), tq)
            for hh in range(2):
                s_sc[hh] = _dot_nt(k_ref[rows, LANES * hh:LANES * (hh + 1)], q_ref[:, LANES * hh:LANES * (hh + 1)])
            vv = v_ref[rows, :]
            new = []
            for hh in range(2):
                m, l = state[hh]

                def strip_of(r0, hh=hh):
                    s = s_sc[hh, r0:r0 + strip, :]
                    return jnp.where(key_s + r0 <= qry_s, s, NEG) if masked else s

                mx = fold(strip_of(0), jnp.maximum)
                for r0 in range(strip, tq, strip):
                    mx = jnp.maximum(mx, fold(strip_of(r0), jnp.maximum))
                m_new = jnp.maximum(m, jnp.max(mx, axis=0, keepdims=True))
                alpha = jnp.exp(m - m_new)
                sm = jnp.zeros((8, tq), F32)
                for r0 in range(0, tq, strip):
                    p = jnp.exp(strip_of(r0) - m_new)
                    sm = sm + fold(p, jnp.add)
                    p_sc[hh, r0:r0 + strip, :] = p.astype(BF16)
                l = alpha * l + jnp.sum(sm, axis=0, keepdims=True)
                acc_sc[hh] = alpha * acc_sc[hh] + _dot_tn(vv, p_sc[hh])
                new.append((m_new, l))
            return tuple(new)

        one = (jnp.full((1, tq), NEG, F32), jnp.zeros((1, tq), F32))
        state = lax.fori_loop(0, qi, functools.partial(step, masked=False), (one, one))
        (m0, l0), (m1, l1) = step(qi, state, True)
        o_ref[...] = jnp.where(row_t < 64, acc_sc[0] / l0, acc_sc[1] / l1).T
        lse_ref[:, 0:LANES] = jnp.broadcast_to(m0 + jnp.log(l0), (LANES, tq)).T
        lse_ref[:, LANES:2 * LANES] = jnp.broadcast_to(m1 + jnp.log(l1), (LANES, tq)).T

        if ex:
            @pl.when((b == nb - 1) & (pr == PAIRS - 1) & (qi == nq - 1))
            def _():
                ex.wait(*ex_refs)

    res = pl.pallas_call(
        body, name=name, grid=(nb, PAIRS, nq),
        out_shape=(jax.ShapeDtypeStruct((T, 512), F32), jax.ShapeDtypeStruct((T, 1024), F32)) + tuple(ex.outs if ex else ()),
        in_specs=[pl.BlockSpec((tq, 2 * LANES), lambda b, p, i: (b * nq + i, p)),
                  pl.BlockSpec((seq, 2 * LANES), lambda b, p, i: (b, p)),
                  pl.BlockSpec((seq, LANES), lambda b, p, i: (b, p))] + [ANY] * n_in,
        out_specs=[pl.BlockSpec((tq, LANES), lambda b, p, i: (b * nq + i, p)),
                   pl.BlockSpec((tq, 2 * LANES), lambda b, p, i: (b * nq + i, p))] + [ANY] * n_out,
        scratch_shapes=[pltpu.VMEM((2, tq, tq), F32), pltpu.VMEM((2, tq, tq), BF16), pltpu.VMEM((2, LANES, tq), F32)]
        + (ex.sems() if ex else []),
        compiler_params=_params(("arbitrary", "arbitrary", "arbitrary")),
    )(q, k, v, *(ex.ins if ex else ()))
    return res[0], res[1], list(res[2:])


def _attn_bwd(q, k, v, o, do, lse, nb, seq, tq, name, key_bias, ex=None):
    T = q.shape[0]
    nq = seq // tq
    n_in, n_out = (len(ex.ins), len(ex.outs)) if ex else (0, 0)
    n_res = 4 if key_bias else 3

    def body(*refs):
        q_ref, k_ref, v_ref, o_ref, do_ref, lse_ref = refs[0:6]
        dq_ref, dk_ref, dv_ref = refs[6 + n_in:9 + n_in]
        dcb_ref = refs[9 + n_in] if key_bias else None
        first_scratch = 6 + n_in + n_res + n_out
        dsc, rsum, dq_acc = refs[first_scratch:first_scratch + 3]
        b, pr, step_no = pl.program_id(0), pl.program_id(1), pl.program_id(2)
        kj = nq - 1 - step_no
        if ex:
            ex_refs = (refs[6:6 + n_in], refs[6 + n_in + n_res:6 + n_in + n_res + n_out], refs[first_scratch + 3:])

            @pl.when((b == 0) & (pr == 0) & (step_no == 0))
            def _():
                ex.start(*ex_refs)

        lane_s = lax.broadcasted_iota(jnp.int32, (seq, LANES), 1)
        lane = lax.broadcasted_iota(jnp.int32, (tq, LANES), 1)
        rr = lax.broadcasted_iota(jnp.int32, (tq, tq), 0)
        cc = lax.broadcasted_iota(jnp.int32, (tq, tq), 1)

        @pl.when(step_no == 0)
        def _():
            dq_acc[...] = jnp.zeros_like(dq_acc)
            prod = do_ref[...].astype(F32) * o_ref[...]
            d0 = jnp.sum(jnp.where(lane_s < 64, prod, 0.0), axis=1, keepdims=True)
            d1 = jnp.sum(jnp.where(lane_s < 64, 0.0, prod), axis=1, keepdims=True)
            dsc[0] = jnp.broadcast_to(d0, (seq, LANES))
            dsc[1] = jnp.broadcast_to(d1, (seq, LANES))
            if key_bias:
                rsum[...] = jnp.zeros_like(rsum)

        if key_bias:
            @pl.when((pr == 0) & (step_no == 0))
            def _():
                dcb_ref[...] = jnp.zeros_like(dcb_ref)

        vv = v_ref[...]

        def step(qi, carry, masked):
            dkt, dvt, cols = carry
            rows = pl.ds(pl.multiple_of(qi * tq, tq), tq)
            dov = do_ref[rows, :]
            new_dkt, new_cols = [], []
            for hh in range(2):
                qv = q_ref[rows, LANES * hh:LANES * (hh + 1)]
                kv = k_ref[:, LANES * hh:LANES * (hh + 1)]
                dom = jnp.where((lane < 64) if hh == 0 else (lane >= 64), dov, jnp.zeros((), BF16))
                s = _dot_nt(qv, kv)
                if masked:
                    s = jnp.where(cc <= rr, s, NEG)
                p = jnp.exp(s - jnp.tile(lse_ref[rows, LANES * hh:LANES * (hh + 1)], (1, tq // LANES)))
                dp = _dot_nt(dom, vv)
                ds32 = p * (dp - jnp.tile(dsc[hh, rows, :], (1, tq // LANES)))
                col = cols[hh]
                if key_bias:
                    col = col + jnp.sum(ds32, axis=0, keepdims=True)
                    rsum[hh, rows, :] += jnp.broadcast_to(jnp.sum(ds32, axis=1, keepdims=True), (tq, LANES))
                ds = ds32.astype(BF16)
                dvt = dvt + _dot_tn(dom, p.astype(BF16))
                new_dkt.append(dkt[hh] + _dot_tn(qv, ds))
                new_cols.append(col)
                dq_acc[rows, LANES * hh:LANES * (hh + 1)] += _dot(ds, kv)
            return tuple(new_dkt), dvt, tuple(new_cols)

        zt = jnp.zeros((LANES, tq), F32)
        zc = jnp.zeros((1, tq), F32)
        carry = step(kj, ((zt, zt), zt, (zc, zc)), True)
        dkt, dvt, cols = lax.fori_loop(kj + 1, nq, functools.partial(step, masked=False), carry)
        for hh in range(2):
            dk_ref[:, LANES * hh:LANES * (hh + 1)] = dkt[hh].T.astype(dk_ref.dtype)
        dv_ref[...] = dvt.T.astype(dv_ref.dtype)
        if key_bias:
            row_t = lax.broadcasted_iota(jnp.int32, (LANES, tq), 0)
            per_key = jnp.where(row_t == 2 * pr, -cols[0], 0.0) + jnp.where(row_t == 2 * pr + 1, -cols[1], 0.0)
            dcb_ref[pl.ds(pl.multiple_of(kj * tq, tq), tq), :] += per_key.T

        @pl.when(step_no == nq - 1)
        def _():
            dq_ref[...] = dq_acc[...].astype(dq_ref.dtype)
            if key_bias:
                dcb_ref[...] += jnp.where(lane_s == 2 * pr, rsum[0], 0.0) + jnp.where(lane_s == 2 * pr + 1, rsum[1], 0.0)

        if ex:
            @pl.when((b == nb - 1) & (pr == PAIRS - 1) & (step_no == nq - 1))
            def _():
                ex.wait(*ex_refs)

    per_seq = lambda w: pl.BlockSpec((seq, w), lambda b, p, j: (b, p))
    per_blk = lambda w: pl.BlockSpec((tq, w), lambda b, p, j: (b * nq + nq - 1 - j, p))
    res = pl.pallas_call(
        body, name=name, grid=(nb, PAIRS, nq),
        out_shape=(jax.ShapeDtypeStruct((T, 1024), BF16), jax.ShapeDtypeStruct((T, 1024), BF16), jax.ShapeDtypeStruct((T, 512), BF16))
        + ((jax.ShapeDtypeStruct((T, LANES), F32),) if key_bias else ()) + tuple(ex.outs if ex else ()),
        in_specs=[per_seq(2 * LANES), per_blk(2 * LANES), per_blk(LANES), per_seq(LANES), per_seq(LANES), per_seq(2 * LANES)] + [ANY] * n_in,
        out_specs=[per_seq(2 * LANES), per_blk(2 * LANES), per_blk(LANES)]
        + ([pl.BlockSpec((seq, LANES), lambda b, p, j: (b, 0))] if key_bias else []) + [ANY] * n_out,
        scratch_shapes=[pltpu.VMEM((2, seq, LANES), F32), pltpu.VMEM((2, seq, LANES) if key_bias else (2, 8, LANES), F32),
                        pltpu.VMEM((seq, 2 * LANES), F32)]
        + (ex.sems() if ex else []),
        compiler_params=_params(("arbitrary", "arbitrary", "arbitrary")),
    )(q, k, v, o, do, lse, *(ex.ins if ex else ()))
    return list(res[:n_res]), list(res[n_res:])


def _mid(of, om, x, tgt, g_fo, g_mo, g2, g3, w_o, w_g, w_u, w_d, tm):
    T = x.shape[0]

    def body(of_ref, om_ref, x_ref, t_ref, gfo_ref, gmo_ref, g2_ref, g3_ref, wo_ref, wg_ref, wu_ref, wd_ref,
             a_ref, h2_ref, hid_ref, dg_ref, du_ref, dx3_ref, dx2_ref, dof_ref, dom_ref, st_ref):
        i = pl.program_id(0)

        @pl.when(i == 0)
        def _():
            st_ref[...] = jnp.zeros_like(st_ref)

        ofv, omv = of_ref[...], om_ref[...]
        rf, rm = _rms(ofv, FOX_W), _rms(omv, FOX_W)
        fhat, mhat = ofv * rf, omv * rm
        a = jnp.concatenate([fhat * gfo_ref[...], mhat * gmo_ref[...]], axis=1).astype(BF16)
        a_ref[...] = a
        x2 = x_ref[...] + _dot(a, wo_ref[...])
        r2 = _rms(x2, D_MODEL)
        xh2 = x2 * r2
        h2 = (xh2 * g2_ref[...]).astype(BF16)
        h2_ref[...] = h2
        parts = (slice(0, FF_SPLIT), slice(FF_SPLIT, D_FF))
        x3, kept = x2, []
        for part in parts:
            gt = _dot_nt(h2, wg_ref[part, :])
            up = _dot_nt(h2, wu_ref[part, :])
            sg = jax.nn.sigmoid(gt)
            sl = gt * sg
            hid = (sl * up).astype(BF16)
            hid_ref[:, part] = hid
            x3 = x3 + _dot(hid, wd_ref[part, :])
            kept.append((up * (sg * (1.0 + gt * (1.0 - sg))), sl))
        r3 = _rms(x3, D_MODEL)
        xh3 = x3 * r3
        diff = xh3 * g3_ref[...] - t_ref[...]
        dy = diff * (1.0 / D_MODEL)
        st_ref[ROW_LOSS:ROW_LOSS + 1, :] += jnp.sum(diff * diff, axis=0, keepdims=True) * (0.5 / D_MODEL)
        st_ref[ROW_FINAL:ROW_FINAL + 1, :] += jnp.sum(dy * xh3, axis=0, keepdims=True)
        dx3 = _rms_bwd(dy, xh3, r3, g3_ref[...], D_MODEL)
        dx3b = dx3.astype(BF16)
        dx3_ref[...] = dx3b
        dh2 = jnp.zeros((tm, D_MODEL), F32)
        for part, (dsilu_up, sl) in zip(parts, kept):
            dhid = _dot_nt(dx3b, wd_ref[part, :])
            dg = (dhid * dsilu_up).astype(BF16)
            du = (dhid * sl).astype(BF16)
            dg_ref[:, part] = dg
            du_ref[:, part] = du
            dh2 = dh2 + _dot(dg, wg_ref[part, :]) + _dot(du, wu_ref[part, :])
        st_ref[ROW_NORM_FFN:ROW_NORM_FFN + 1, :] += jnp.sum(dh2 * xh2, axis=0, keepdims=True)
        dx2 = dx3 + _rms_bwd(dh2, xh2, r2, g2_ref[...], D_MODEL)
        dx2_ref[...] = dx2
        da = _dot_nt(dx2.astype(BF16), wo_ref[...])
        daf, dam = da[:, 0:FOX_W], da[:, FOX_W:2 * FOX_W]
        st_ref[ROW_OUT:ROW_OUT + 1, 0:FOX_W] += jnp.sum(daf * fhat, axis=0, keepdims=True)
        st_ref[ROW_OUT:ROW_OUT + 1, FOX_W:2 * FOX_W] += jnp.sum(dam * mhat, axis=0, keepdims=True)
        dof_ref[...] = _rms_bwd(daf, fhat, rf, gfo_ref[...], FOX_W).astype(BF16)
        dom_ref[...] = _rms_bwd(dam, mhat, rm, gmo_ref[...], FOX_W).astype(BF16)

    row = lambda w: pl.BlockSpec((tm, w), lambda i: (i, 0))
    ff = jax.ShapeDtypeStruct((T, D_FF), BF16)
    out_shape = (
        jax.ShapeDtypeStruct((T, 1024), BF16), jax.ShapeDtypeStruct((T, 1024), BF16), ff, ff, ff,
        jax.ShapeDtypeStruct((T, 1024), BF16), jax.ShapeDtypeStruct((T, 1024), F32),
        jax.ShapeDtypeStruct((T, 512), BF16), jax.ShapeDtypeStruct((T, 512), BF16), jax.ShapeDtypeStruct((8, 1024), F32),
    )
    return pl.pallas_call(
        body, name="mid", grid=(T // tm,), out_shape=out_shape,
        in_specs=[row(512), row(512), row(1024), row(1024), _full(g_fo.shape), _full(g_mo.shape), _full(g2.shape), _full(g3.shape),
                  _full(w_o.shape), _full(w_g.shape), _full(w_u.shape), _full(w_d.shape)],
        out_specs=[row(1024), row(1024), row(D_FF), row(D_FF), row(D_FF), row(1024), row(1024), row(512), row(512),
                   pl.BlockSpec((8, 1024), lambda i: (0, 0))],
        compiler_params=_params(("arbitrary",)),
    )(of, om, x, tgt, g_fo, g_mo, g2, g3, w_o, w_g, w_u, w_d)


def _in_bwd(dqf, dkf, dvf, dcb, dqm, dkm, dvm, lat, qn, kvn, x, dx2, g1, gq, gkv, bfg, ct, st, w_in, w_q12, w_kv, seq, tm):
    T = x.shape[0]
    nblk = T // tm
    nsb = seq // tm

    def body(dqf_ref, dkf_ref, dvf_ref, dcb_ref, dqm_ref, dkm_ref, dvm_ref, lat_ref, qn_ref, kvn_ref, x_ref, dx2_ref, g1_ref, gq_ref,
             gkv_ref, b_ref, ct_ref, st_ref, win_ref, wq_ref, wkv_ref, dx_ref, dproj_ref, gq12_ref, gkv12_ref, stat_ref, carry,
             dq12_ref, dkv_ref):
        i = pl.program_id(0)

        @pl.when(i == 0)
        def _():
            stat_ref[...] = jnp.zeros_like(stat_ref)
            gq12_ref[...] = jnp.zeros_like(gq12_ref)
            gkv12_ref[...] = jnp.zeros_like(gkv12_ref)

        @pl.when(i % nsb == 0)
        def _():
            carry[...] = jnp.zeros_like(carry)

        lane = lax.broadcasted_iota(jnp.int32, (tm, LANES), 1)
        low = lane < 64
        ctv, stv = ct_ref[...], st_ref[...]

        for j in range(PAIRS):
            e, o = 2 * LANES * j, 2 * LANES * j + LANES
            half = lambda ref, c0: jnp.where(low, ref[:, c0:c0 + LANES].astype(F32), 0.0)
            dq = half(dqf_ref, e) + pltpu.roll(half(dqf_ref, o), 64, 1)
            dk = half(dkf_ref, e) + pltpu.roll(half(dkf_ref, o), 64, 1)
            dproj_ref[:, C_FQ + LANES * j:C_FQ + LANES * (j + 1)] = (dq * FOX_SCALE).astype(BF16)
            dproj_ref[:, C_FK + LANES * j:C_FK + LANES * (j + 1)] = dk.astype(BF16)
        dproj_ref[:, C_FV:C_QL] = dvf_ref[...]
        dc = dcb_ref[...]
        rr = lax.broadcasted_iota(jnp.int32, (tm, tm), 0)
        cc = lax.broadcasted_iota(jnp.int32, (tm, tm), 1)
        triu = (cc >= rr).astype(BF16)
        a0, a1, a2 = _split3(dc)
        dlf = _dot(triu, a0) + _dot(triu, a1) + _dot(triu, a2) + carry[0:1, :]
        carry[0:1, :] = dlf[0:1, :]
        misc_a = lat_ref[:, Q_RANK + KV_RANK:Q_RANK + KV_RANK + LANES]
        z = misc_a + b_ref[...]
        dz = jnp.where(lane < HEADS, dlf * jax.nn.sigmoid(-z), 0.0)
        stat_ref[ROW_B:ROW_B + 1, 0:LANES] += jnp.sum(dz, axis=0, keepdims=True)

        cq = (jnp.where(low, 1.0, 0.0) + ctv) * MLA_SCALE
        sq = stv * MLA_SCALE
        dkpe = jnp.zeros((tm, LANES), F32)
        for hd in range(HEADS):
            s0 = LANES * hd
            dqh = dqm_ref[:, s0:s0 + LANES].astype(F32)
            dq12_ref[:, s0:s0 + LANES] = (dqh * cq).astype(BF16)
            dq12_ref[:, 1024 + s0:1024 + s0 + LANES] = (dqh * sq).astype(BF16)
            dkpe = dkpe + dkm_ref[:, s0:s0 + LANES].astype(F32)
        dkv_ref[:, 0:1024] = dkm_ref[...]
        dkv_ref[:, 1024:1536] = dvm_ref[...]
        dproj_ref[:, C_MA:C_END] = (dz + dkpe * ctv + pltpu.roll(dkpe * stv, 32, 1)).astype(BF16)
        dqn = _dot(dq12_ref[...], wq_ref[...])
        dkvn = _dot_nt(dkv_ref[...], wkv_ref[...])
        gq12_ref[...] += _dot_tn(dq12_ref[...], qn_ref[...])
        gkv12_ref[...] += _dot_tn(kvn_ref[...], dkv_ref[...])
        ql = lat_ref[:, 0:Q_RANK]
        kvl = lat_ref[:, Q_RANK:Q_RANK + KV_RANK]
        rq, rkv = _rms(ql, Q_RANK), _rms(kvl, KV_RANK)
        qhat, kvhat = ql * rq, kvl * rkv
        stat_ref[ROW_Q:ROW_Q + 1, 0:Q_RANK] += jnp.sum(dqn * qhat, axis=0, keepdims=True)
        stat_ref[ROW_KV:ROW_KV + 1, 0:KV_RANK] += jnp.sum(dkvn * kvhat, axis=0, keepdims=True)
        dproj_ref[:, C_QL:C_KVL] = _rms_bwd(dqn, qhat, rq, gq_ref[...], Q_RANK).astype(BF16)
        dproj_ref[:, C_KVL:C_MA] = _rms_bwd(dkvn, kvhat, rkv, gkv_ref[...], KV_RANK).astype(BF16)

        dh1 = _dot(dproj_ref[...], win_ref[...])
        xv = x_ref[...]
        r1 = _rms(xv, D_MODEL)
        xh = xv * r1
        stat_ref[ROW_NORM_MIX:ROW_NORM_MIX + 1, :] += jnp.sum(dh1 * xh, axis=0, keepdims=True)
        dx_ref[...] = dx2_ref[...] + _rms_bwd(dh1, xh, r1, g1_ref[...], D_MODEL)

    rev = lambda w: pl.BlockSpec((tm, w), lambda i: (nblk - 1 - i, 0))
    whole = lambda r, c: pl.BlockSpec((r, c), lambda i: (0, 0))
    out_shape = (
        jax.ShapeDtypeStruct((T, 1024), F32), jax.ShapeDtypeStruct((T, C_END), BF16), jax.ShapeDtypeStruct((2048, Q_RANK), F32),
        jax.ShapeDtypeStruct((KV_RANK, 1536), F32), jax.ShapeDtypeStruct((8, 1024), F32),
    )
    return pl.pallas_call(
        body, name="in_bwd", grid=(nblk,), out_shape=out_shape,
        in_specs=[rev(1024), rev(1024), rev(512), rev(LANES), rev(1024), rev(1024), rev(512), rev(512), rev(Q_RANK), rev(KV_RANK),
                  rev(1024), rev(1024), _full(g1.shape), _full(gq.shape), _full(gkv.shape), _full(bfg.shape), rev(LANES), rev(LANES),
                  _full(w_in.shape), _full(w_q12.shape), _full(w_kv.shape)],
        out_specs=[rev(1024), rev(C_END), whole(2048, Q_RANK), whole(KV_RANK, 1536), whole(8, 1024)],
        scratch_shapes=[pltpu.VMEM((8, LANES), F32), pltpu.VMEM((tm, 2048), BF16), pltpu.VMEM((tm, 1536), BF16)],
        compiler_params=_params(("arbitrary",)),
    )(dqf, dkf, dvf, dcb, dqm, dkm, dvm, lat, qn, kvn, x, dx2, g1, gq, gkv, bfg, ct, st, w_in, w_q12, w_kv)


def _wgrad(a, b, tk, tt, name, ex=None):
    T, K = a.shape
    N = b.shape[1]
    n_in, n_out = (len(ex.ins), len(ex.outs)) if ex else (0, 0)
    gk, gt = K // tk, T // tt

    def body(*refs):
        a_ref, b_ref, o_ref = refs[0], refs[1], refs[2 + n_in]
        kb, t = pl.program_id(0), pl.program_id(1)
        if ex:
            ex_refs = (refs[2:2 + n_in], refs[3 + n_in:3 + n_in + n_out], refs[3 + n_in + n_out:])

            @pl.when((kb == 0) & (t == 0))
            def _():
                ex.start(*ex_refs)

        @pl.when(t == 0)
        def _():
            o_ref[...] = jnp.zeros_like(o_ref)

        o_ref[...] += _dot_tn(a_ref[...].astype(BF16), b_ref[...].astype(BF16))

        if ex:
            @pl.when((kb == gk - 1) & (t == gt - 1))
            def _():
                ex.wait(*ex_refs)

    res = pl.pallas_call(
        body, name=name, grid=(gk, gt), out_shape=(jax.ShapeDtypeStruct((K, N), F32),) + tuple(ex.outs if ex else ()),
        in_specs=[pl.BlockSpec((tt, tk), lambda kb, t: (t, kb)), pl.BlockSpec((tt, N), lambda kb, t: (t, 0))] + [ANY] * n_in,
        out_specs=[pl.BlockSpec((tk, N), lambda kb, t: (kb, 0))] + [ANY] * n_out,
        scratch_shapes=ex.sems() if ex else [], input_output_aliases=ex.aliases(2, 1) if ex else {},
        compiler_params=_params(("arbitrary", "arbitrary")),
    )(a, b, *(ex.ins if ex else ()))
    return (res[0], list(res[1:])) if ex else res[0]


def _adam_update(w, g, m, v):
    nm = ADAM_B1 * m + (1.0 - ADAM_B1) * g
    nv = ADAM_B2 * v + (1.0 - ADAM_B2) * (g * g)
    m_hat = nm / (1.0 - ADAM_B1 ** ADAM_STEP)
    v_hat = nv / (1.0 - ADAM_B2 ** ADAM_STEP)
    return -ADAM_LR * (m_hat / (jnp.sqrt(v_hat) + ADAM_EPS) + ADAM_WD * w), nm, nv


def _adamw_small(stats, params):
    k = len(SMALL)

    def body(*refs):
        for t, name in enumerate(SMALL):
            row, c0, width = SMALL_AT[name]
            w_ref, m_ref, v_ref = refs[1 + 3 * t:4 + 3 * t]
            g_ref, d_ref, nm_ref, nv_ref = refs[1 + 3 * k + 4 * t:5 + 3 * k + 4 * t]
            g = refs[0][row:row + 1, c0:c0 + width]
            g_ref[...] = g
            d_ref[...], nm_ref[...], nv_ref[...] = _adam_update(w_ref[...], g, m_ref[...], v_ref[...])

    vm = pl.BlockSpec(memory_space=pltpu.VMEM)
    out_shape = tuple(jax.ShapeDtypeStruct((1, SMALL_AT[name][2]), F32) for name in SMALL for _ in range(4))
    res = pl.pallas_call(body, name="adamw_small", out_shape=out_shape, in_specs=[vm] * (1 + 3 * k), out_specs=tuple([vm] * (4 * k)))(
        stats, *[a for name in SMALL for a in params[name]])
    return {name: tuple(res[4 * t:4 * t + 4]) for t, name in enumerate(SMALL)}


def _adamw(tensors, name):
    n = len(tensors)
    R, C = tensors[0][0].shape
    tr = _row_tile(R)

    def body(*refs):
        for t in range(n):
            w_ref, g_ref, m_ref, v_ref = refs[4 * t:4 * t + 4]
            d_ref, nm_ref, nv_ref = refs[4 * n + 3 * t:4 * n + 3 * t + 3]
            d_ref[...], nm_ref[...], nv_ref[...] = _adam_update(w_ref[...], g_ref[...], m_ref[...], v_ref[...])

    blk = pl.BlockSpec((tr, C), lambda i: (i, 0))
    sh = jax.ShapeDtypeStruct((R, C), F32)
    res = pl.pallas_call(
        body, name=name, grid=(R // tr,), out_shape=(sh,) * (3 * n),
        in_specs=[blk] * (4 * n), out_specs=[blk] * (3 * n),
        compiler_params=_params(("arbitrary",)),
    )(*[a for t in tensors for a in t])
    return [tuple(res[3 * t:3 * t + 3]) for t in range(n)]


def _arrange(win_t, wuq_t, wukv):
    dt = win_t.dtype
    z = lambda r: jnp.zeros((r, D_MODEL), dt)
    zh = lambda r: jnp.zeros((HEADS, r, Q_RANK), dt)
    kr1, kr2 = win_t[1928:1944], win_t[1944:1960]
    misc = jnp.concatenate([win_t[1536:1544], z(56), kr1, kr2, kr2, kr1], axis=0)
    w_in = jnp.concatenate([win_t[0:1536], win_t[1544:1928], misc], axis=0)
    wq = wuq_t.reshape(HEADS, 96, Q_RANK)
    q1 = jnp.concatenate([wq, zh(32)], axis=1).reshape(1024, Q_RANK)
    q2 = jnp.concatenate([zh(64), wq[:, 80:96], wq[:, 64:80], zh(32)], axis=1).reshape(1024, Q_RANK)
    wkv = wukv.reshape(KV_RANK, HEADS, 128)
    wk = jnp.concatenate([wkv[:, :, 0:64], jnp.zeros((KV_RANK, HEADS, 64), dt)], axis=2).reshape(KV_RANK, 1024)
    wv = wkv[:, :, 64:128].reshape(KV_RANK, 512)
    return dict(w_in=w_in, w_q12=jnp.concatenate([q1, q2], axis=0), w_k=wk, w_v=wv, w_kv=jnp.concatenate([wk, wv], axis=1))


def _unarrange(g_in, g_q12, g_kv):
    kr1 = g_in[C_MA + 64:C_MA + 80] + g_in[C_MA + 112:C_MA + 128]
    kr2 = g_in[C_MA + 80:C_MA + 96] + g_in[C_MA + 96:C_MA + 112]
    win_t = jnp.concatenate([g_in[0:1536], g_in[C_MA:C_MA + 8], g_in[1536:1920], kr1, kr2], axis=0)
    g1 = g_q12[0:1024].reshape(HEADS, 128, Q_RANK)
    g2 = g_q12[1024:2048].reshape(HEADS, 128, Q_RANK)
    wuq_t = jnp.concatenate([g1[:, 0:64], g1[:, 64:80] + g2[:, 80:96], g1[:, 80:96] + g2[:, 64:80]], axis=1).reshape(768, Q_RANK)
    gk = g_kv[:, 0:1024].reshape(KV_RANK, HEADS, 128)
    gv = g_kv[:, 1024:1536].reshape(KV_RANK, HEADS, 64)
    wukv = jnp.concatenate([gk[:, :, 0:64], gv], axis=2).reshape(KV_RANK, 1024)
    return win_t, wuq_t, wukv


def _selector():
    sel = np.zeros((384, 1024), np.float32)
    for h in range(HEADS):
        for piece in range(3):
            sel[LANES * piece + h, LANES * h + 64 + piece] = 1.0
    return jnp.asarray(sel, BF16)


def _rope_tables(positions):
    inv_freq = 10000.0 ** (-jnp.arange(0, ROPE, 2, dtype=F32) / ROPE)
    n = positions.size
    ang = (positions.reshape(n // 8, 8, 1).astype(F32) * inv_freq[None, None, :]).reshape(n // 8, 8 * (ROPE // 2))
    cos, sin = lax.optimization_barrier((jnp.cos(lax.optimization_barrier(ang)), jnp.sin(lax.optimization_barrier(ang))))
    cos, sin = cos.reshape(n, ROPE // 2), sin.reshape(n, ROPE // 2)
    z64, z32 = jnp.zeros((n, 64), F32), jnp.zeros((n, 32), F32)
    return jnp.concatenate([z64, cos, cos, z32], axis=1), jnp.concatenate([z64, -sin, sin, z32], axis=1)


def _work(name, t):
    return jnp.swapaxes(t[0], 0, 1) if name in TRANSPOSED else t[0]


def _back(name, t):
    return (jnp.swapaxes(t, 0, 1) if name in TRANSPOSED else t)[None]


def kernel(x, positions, norm_mix_g, w_in, b_fgate, q_norm_g, w_uq, kv_norm_g, w_ukv, fox_out_g, mla_out_g, w_o, norm_ffn_g, w_gate, w_up, w_down, final_norm_g, loss_target, m_norm_mix_g, m_w_in, m_b_fgate, m_q_norm_g, m_w_uq, m_kv_norm_g, m_w_ukv, m_fox_out_g, m_mla_out_g, m_w_o, m_norm_ffn_g, m_w_gate, m_w_up, m_w_down, m_final_norm_g, v_norm_mix_g, v_w_in, v_b_fgate, v_q_norm_g, v_w_uq, v_kv_norm_g, v_w_ukv, v_fox_out_g, v_mla_out_g, v_w_o, v_norm_ffn_g, v_w_gate, v_w_up, v_w_down, v_final_norm_g):
    names = ["norm_mix_g", "w_in", "b_fgate", "q_norm_g", "w_uq", "kv_norm_g", "w_ukv", "fox_out_g", "mla_out_g", "w_o",
             "norm_ffn_g", "w_gate", "w_up", "w_down", "final_norm_g"]
    wts = dict(zip(names, [norm_mix_g, w_in, b_fgate, q_norm_g, w_uq, kv_norm_g, w_ukv, fox_out_g, mla_out_g, w_o, norm_ffn_g,
                           w_gate, w_up, w_down, final_norm_g]))
    mom = dict(zip(names, [m_norm_mix_g, m_w_in, m_b_fgate, m_q_norm_g, m_w_uq, m_kv_norm_g, m_w_ukv, m_fox_out_g, m_mla_out_g,
                           m_w_o, m_norm_ffn_g, m_w_gate, m_w_up, m_w_down, m_final_norm_g]))
    var = dict(zip(names, [v_norm_mix_g, v_w_in, v_b_fgate, v_q_norm_g, v_w_uq, v_kv_norm_g, v_w_ukv, v_fox_out_g, v_mla_out_g,
                           v_w_o, v_norm_ffn_g, v_w_gate, v_w_up, v_w_down, v_final_norm_g]))
    shard = {n: _work(n, wts[n]) for n in HEAD3 + FFN4}
    nb, seq, _ = x.shape
    T = nb * seq
    tm, tq = min(ROW_TILE, seq), min(ATTN_TILE, seq)
    tt = min(WGRAD_TILE, T)
    xf = x.reshape(T, D_MODEL)
    tgt = loss_target.reshape(T, D_MODEL)
    chip = 2 * lax.axis_index("x") + lax.axis_index("y")

    mine = [shard[n].astype(BF16) for n in HEAD3]
    head = _run_exchange(_gather_split_exchange(mine), "gather_head")
    win4, wuq4, wukv4 = [lax.dynamic_update_slice(h, s[None], (chip, 0, 0)) for h, s in zip(head, mine)]
    a = _arrange(win4.reshape(-1, D_MODEL), wuq4.reshape(-1, Q_RANK), wukv4.transpose(1, 0, 2).reshape(KV_RANK, -1))
    sel = _selector()
    ct, st = _rope_tables(positions)
    bfg = jnp.concatenate([b_fgate, jnp.zeros((1, LANES - HEADS), F32)], axis=1)
    g1, gq, gkv = norm_mix_g, q_norm_g, kv_norm_g

    h1, qf, kf, vf, qm, km, vm, lat, qn, kvn = _in_proj(xf, g1, a["w_in"], a["w_q12"], a["w_k"], a["w_v"], gq, gkv, bfg, ct, st, sel, seq,
                                                        min(IN_PROJ_TILE, seq))
    tqf = min(ATTN_FWD_TILE, seq)
    of, lse_f, (wo4, wg4) = _attn_fwd(qf, kf, vf, nb, seq, tqf, "fox_fwd", _gather_exchange([shard[n].astype(BF16) for n in FFN4[:2]]))
    om, lse_m, (wu4, wd4) = _attn_fwd(qm, km, vm, nb, seq, tqf, "mla_fwd", _gather_exchange([shard[n].astype(BF16) for n in FFN4[2:]]))
    a_cat, h2, hid, dg, du, dx3, dx2, dof, dom, st_mid = _mid(
        of, om, xf, tgt, fox_out_g, mla_out_g, norm_ffn_g, final_norm_g.reshape(1, D_MODEL),
        wo4.reshape(D_MODEL, D_MODEL), wg4.reshape(D_FF, D_MODEL), wu4.reshape(D_FF, D_MODEL), wd4.reshape(D_FF, D_MODEL), tm)

    slab = lambda g: g.reshape(N_CHIPS, g.shape[0] // N_CHIPS, g.shape[1])
    big = [slab(_wgrad(a_cat, dx2, D_MODEL, tt, "wgrad_o")), slab(_wgrad(dg, h2, D_FF // 2, tt, "wgrad_gate")),
           slab(_wgrad(du, h2, D_FF // 2, tt, "wgrad_up")), slab(_wgrad(hid, dx3, D_FF // 2, tt, "wgrad_down"))]
    (dqf, dkf, dvf, dcb), got = _attn_bwd(qf, kf, vf, of, dof, lse_f, nb, seq, tq, "fox_bwd", True, _swap_exchange(big))
    sums = [_add_half(g, s) for g, s in zip(big, got)]
    (dqm, dkm, dvm), recv = _attn_bwd(qm, km, vm, om, dom, lse_m, nb, seq, tq, "mla_bwd", False, _scatter_exchange(sums))
    halves = [_sum_slabs(g, s, r) for g, s, r in zip(big, got, recv)]
    dx, dproj, g_q12, g_kv, st_in = _in_bwd(dqf, dkf, dvf, dcb, dqm, dkm, dvm, lat, qn, kvn, xf, dx2, g1, gq, gkv, bfg, ct, st,
                                            a["w_in"], a["w_q12"], a["w_kv"], seq, tm)
    g_in, results = _wgrad(dproj, h1, C_END, tt, "wgrad_in", _both(_join_exchange(halves), _everyone_exchange(st_mid + st_in)))
    gshard = dict(zip(FFN4, results[:4]))
    stats = _sum_devices(results[4])

    gwin_t, gwuq_t, gwukv = _unarrange(g_in, g_q12, g_kv)
    tail = [slab(gwin_t), slab(gwuq_t), gwukv.reshape(KV_RANK, N_CHIPS, -1).transpose(1, 0, 2)]
    tail_got = _run_exchange(_swap_exchange(tail), "tail_swap")
    tail_sums = [_add_half(g, s) for g, s in zip(tail, tail_got)]
    tail_recv = _run_exchange(_scatter_exchange(tail_sums), "tail_scatter")
    tail_joined = _run_exchange(_join_exchange([_sum_slabs(g, s, r) for g, s, r in zip(tail, tail_got, tail_recv)]), "tail_join")
    gshard.update(zip(HEAD3, tail_joined))
    quad = lambda n: (shard[n], gshard[n], _work(n, mom[n]), _work(n, var[n]))
    updates = dict(zip(FFN4[1:], _adamw([quad(n) for n in FFN4[1:]], "adamw_ffn")))
    for n in HEAD3 + FFN4[:1]:
        updates[n], = _adamw([quad(n)], "adamw_" + n)

    grads, delta, new_m, new_v = {}, {}, {}, {}
    for n in HEAD3 + FFN4:
        grads[n] = _back(n, gshard[n])
        delta[n], new_m[n], new_v[n] = [_back(n, t) for t in updates[n]]
    row = lambda t: t.reshape(1, -1)
    small = _adamw_small(stats, {n: (row(wts[n]), row(mom[n]), row(var[n])) for n in SMALL})
    for n in SMALL:
        grads[n], delta[n], new_m[n], new_v[n] = [t.reshape(wts[n].shape) for t in small[n]]
    loss = jnp.sum(stats[ROW_LOSS])
    return (loss, dx.reshape(x.shape), *[grads[n] for n in names], *[delta[n] for n in names],
            *[new_m[n] for n in names], *[new_v[n] for n in names])
```

```python
import functools

import numpy as np
import jax
import jax.numpy as jnp
from jax import lax
from jax.experimental import pallas as pl
from jax.experimental.pallas import tpu as pltpu

F32 = jnp.float32
BF16 = jnp.bfloat16
MESH = pl.DeviceIdType.MESH

EPS = 1e-6
D_MODEL = 1024
HEADS = 8
PAIRS = HEADS // 2
FOX_W = 512
Q_RANK = 256
KV_RANK = 128
ROPE = 32
D_FF = 2816
N_CHIPS = 4
FOX_SCALE = 64 ** -0.5
MLA_SCALE = 96 ** -0.5
LANES = 128
NEG = -1e30

ADAM_LR, ADAM_B1, ADAM_B2, ADAM_EPS, ADAM_WD, ADAM_STEP = 0.001, 0.9, 0.999, 1e-08, 0.01, 10

C_FQ, C_FK, C_FV, C_QL, C_KVL, C_MA, C_END = 0, 512, 1024, 1536, 1792, 1920, 2048
C_MB = C_END

VMEM_LIMIT = 60 * 1024 * 1024
ROW_TILE = 256
IN_PROJ_TILE = 512
ATTN_TILE = 512
ATTN_FWD_TILE = 1024
WGRAD_TILE = 2048

HEAD3 = ("w_in", "w_uq", "w_ukv")
FFN4 = ("w_o", "w_gate", "w_up", "w_down")
TRANSPOSED = ("w_in", "w_uq", "w_gate", "w_up")
SMALL = ("norm_mix_g", "b_fgate", "q_norm_g", "kv_norm_g", "fox_out_g", "mla_out_g", "norm_ffn_g", "final_norm_g")
ROW_NORM_MIX, ROW_NORM_FFN, ROW_FINAL, ROW_OUT, ROW_Q, ROW_KV, ROW_B, ROW_LOSS = range(8)
SMALL_AT = {"norm_mix_g": (ROW_NORM_MIX, 0, 1024), "norm_ffn_g": (ROW_NORM_FFN, 0, 1024), "final_norm_g": (ROW_FINAL, 0, 1024),
            "fox_out_g": (ROW_OUT, 0, 512), "mla_out_g": (ROW_OUT, 512, 512), "q_norm_g": (ROW_Q, 0, 256),
            "kv_norm_g": (ROW_KV, 0, 128), "b_fgate": (ROW_B, 0, 8)}


def _params(sem=None):
    return pltpu.CompilerParams(dimension_semantics=sem, vmem_limit_bytes=VMEM_LIMIT)


def _full(shape):
    n = len(shape)
    return pl.BlockSpec(shape, lambda *_: (0,) * n, pipeline_mode=pl.Buffered(1))


def _dot(a, b):
    return jnp.dot(a, b, preferred_element_type=F32)


def _dot_nt(a, b):
    return lax.dot_general(a, b, (((1,), (1,)), ((), ())), preferred_element_type=F32)


def _dot_tn(a, b):
    return lax.dot_general(a, b, (((0,), (0,)), ((), ())), preferred_element_type=F32)


def _split3(v):
    hi = v.astype(BF16)
    r1 = v - hi.astype(F32)
    mid = r1.astype(BF16)
    lo = (r1 - mid.astype(F32)).astype(BF16)
    return hi, mid, lo


def _rms(v, width):
    return lax.rsqrt(jnp.sum(v * v, axis=1, keepdims=True) * (1.0 / width) + EPS)


def _rms_bwd(dy, xhat, r, g, width):
    u = dy * g
    return r * (u - xhat * (jnp.sum(u * xhat, axis=1, keepdims=True) * (1.0 / width)))


ANY = pl.BlockSpec(memory_space=pl.ANY)


def _place():
    return lax.axis_index("x"), lax.axis_index("y"), lax.axis_index("c")


def _other_chips(x, y):
    return [(1 - x, y), (x, 1 - y), (1 - x, 1 - y)]


def _remote(src, dst, send, recv, j, dev):
    return pltpu.make_async_remote_copy(src_ref=src, dst_ref=dst, send_sem=send.at[j], recv_sem=recv.at[j], device_id=dev, device_id_type=MESH)


class _Exchange:
    def __init__(self, ins, outs, n_remote, n_local, build, in_place=False):
        self.ins, self.outs, self.n_remote, self.n_local, self.build = list(ins), list(outs), n_remote, max(n_local, 1), build
        self.in_place = in_place
        self.n_aliased = len(self.ins)

    def aliases(self, first_in, first_out):
        return {first_in + i: first_out + i for i in range(self.n_aliased)} if self.in_place else {}

    def sems(self):
        return [pltpu.SemaphoreType.DMA((self.n_remote,)), pltpu.SemaphoreType.DMA((self.n_remote,)), pltpu.SemaphoreType.DMA((self.n_local,))]

    def start(self, in_refs, out_refs, sems):
        for cp in self.build(in_refs, out_refs, *sems)[0]:
            cp.start()

    def wait(self, in_refs, out_refs, sems):
        for w in self.build(in_refs, out_refs, *sems)[1]:
            w()


def _gather_exchange(shards):
    def build(ins, outs, send, recv, lsem):
        x, y, c = _place()
        starts, waits = [], []
        for i, (s, o) in enumerate(zip(ins, outs)):
            mine = pltpu.make_async_copy(s, o.at[2 * x + y], lsem.at[i])
            starts.append(mine)
            waits.append(mine.wait)
            for j, (cx, cy) in enumerate(_other_chips(x, y)):
                out = _remote(s, o.at[2 * x + y], send, recv, 3 * i + j, (cx, cy, c))
                starts.append(out)
                waits.append(_remote(s, o.at[2 * cx + cy], send, recv, 3 * i + j, (cx, cy, c)).wait_recv)
                waits.append(out.wait_send)
        return starts, waits

    outs = [jax.ShapeDtypeStruct((N_CHIPS,) + s.shape, s.dtype) for s in shards]
    return _Exchange(shards, outs, 3 * len(shards), len(shards), build)


def _gather_split_exchange(shards):
    n = len(shards)

    def build(ins, outs, send, recv, lsem):
        x, y, c = _place()
        starts, waits, last = [], [], []
        for i, (s, o) in enumerate(zip(ins, outs)):
            hc = s.shape[1] // 2
            mine, other = pl.ds(c * hc, hc), pl.ds((1 - c) * hc, hc)
            for j, (cx, cy) in enumerate(_other_chips(x, y)):
                out = _remote(s.at[:, mine], o.at[2 * x + y, :, mine], send, recv, 3 * i + j, (cx, cy, c))
                landed = o.at[2 * cx + cy, :, mine]
                arrive = _remote(s.at[:, mine], landed, send, recv, 3 * i + j, (cx, cy, c))
                onward = _remote(landed, landed, send, recv, 3 * n + 3 * i + j, (x, y, 1 - c))
                from_sibling = _remote(landed, o.at[2 * cx + cy, :, other], send, recv, 3 * n + 3 * i + j, (x, y, 1 - c))
                starts.append(out)
                waits.append(lambda arrive=arrive, onward=onward: (arrive.wait_recv(), onward.start()))
                last += [from_sibling.wait_recv, onward.wait_send, out.wait_send]
        return starts, waits + last

    outs = [jax.ShapeDtypeStruct((N_CHIPS,) + s.shape, s.dtype) for s in shards]
    return _Exchange(shards, outs, 6 * n, 0, build)


def _swap_exchange(grads):
    def build(ins, outs, send, recv, lsem):
        x, y, c = _place()
        cps = []
        for i, (g, o) in enumerate(zip(ins, outs)):
            hc = g.shape[2] // 2
            cps.append(_remote(g.at[:, :, pl.ds((1 - c) * hc, hc)], o, send, recv, i, (x, y, 1 - c)))
        return cps, [cp.wait for cp in cps]

    outs = [jax.ShapeDtypeStruct((g.shape[0], g.shape[1], g.shape[2] // 2), g.dtype) for g in grads]
    return _Exchange(grads, outs, len(grads), 0, build)


def _scatter_exchange(sums):
    def build(ins, outs, send, recv, lsem):
        x, y, c = _place()
        cps = []
        for i, (s, o) in enumerate(zip(ins, outs)):
            for j, (cx, cy) in enumerate(_other_chips(x, y)):
                cps.append(_remote(s.at[2 * cx + cy], o.at[j], send, recv, 3 * i + j, (cx, cy, c)))
        return cps, [cp.wait for cp in cps]

    outs = [jax.ShapeDtypeStruct((3,) + s.shape[1:], s.dtype) for s in sums]
    return _Exchange(sums, outs, 3 * len(sums), 0, build)


def _join_exchange(bufs):
    def build(ins, outs, send, recv, lsem):
        x, y, c = _place()
        starts, waits = [], []
        for i, (t, o) in enumerate(zip(ins, outs)):
            hc = t.shape[1] // 2
            out = _remote(t.at[:, pl.ds(c * hc, hc)], o.at[:, pl.ds(c * hc, hc)], send, recv, i, (x, y, 1 - c))
            starts.append(out)
            waits += [_remote(t.at[:, pl.ds(c * hc, hc)], o.at[:, pl.ds((1 - c) * hc, hc)], send, recv, i, (x, y, 1 - c)).wait_recv,
                      out.wait_send]
        return starts, waits

    outs = [jax.ShapeDtypeStruct(t.shape, t.dtype) for t in bufs]
    return _Exchange(bufs, outs, len(bufs), 0, build, in_place=True)


def _everyone_exchange(v):
    def build(ins, outs, send, recv, lsem):
        x, y, c = _place()
        me = 4 * x + 2 * y + c
        mine = pltpu.make_async_copy(ins[0], outs[0].at[me], lsem.at[0])
        starts, waits = [mine], [mine.wait]
        for j in range(7):
            fx, fy, fc = (j + 1) >> 2 & 1, (j + 1) >> 1 & 1, (j + 1) & 1
            peer = (x ^ fx, y ^ fy, c ^ fc)
            out = _remote(ins[0], outs[0].at[me], send, recv, j, peer)
            starts.append(out)
            waits += [_remote(ins[0], outs[0].at[4 * peer[0] + 2 * peer[1] + peer[2]], send, recv, j, peer).wait_recv, out.wait_send]
        return starts, waits

    return _Exchange([v], [jax.ShapeDtypeStruct((8,) + v.shape, v.dtype)], 7, 1, build)


def _both(a, b):
    na_in, na_out = len(a.ins), len(a.outs)

    def build(ins, outs, send, recv, lsem):
        sa, wa = a.build(ins[:na_in], outs[:na_out], send.at[pl.ds(0, a.n_remote)], recv.at[pl.ds(0, a.n_remote)],
                         lsem.at[pl.ds(0, a.n_local)])
        sb, wb = b.build(ins[na_in:], outs[na_out:], send.at[pl.ds(a.n_remote, b.n_remote)], recv.at[pl.ds(a.n_remote, b.n_remote)],
                         lsem.at[pl.ds(a.n_local, b.n_local)])
        return sa + sb, wa + wb

    both = _Exchange(a.ins + b.ins, a.outs + b.outs, a.n_remote + b.n_remote, a.n_local + b.n_local, build, in_place=a.in_place)
    both.n_aliased = na_in
    return both


def _run_exchange(ex, name):
    n_in, n_out = len(ex.ins), len(ex.outs)

    def body(*refs):
        ins, outs, sems = refs[:n_in], refs[n_in:n_in + n_out], refs[n_in + n_out:]
        ex.start(ins, outs, sems)
        ex.wait(ins, outs, sems)

    return pl.pallas_call(
        body, name=name, out_shape=tuple(ex.outs), in_specs=[ANY] * n_in, out_specs=tuple([ANY] * n_out),
        scratch_shapes=ex.sems(), input_output_aliases=ex.aliases(0, 0),
        compiler_params=pltpu.CompilerParams(has_side_effects=True),
    )(*ex.ins)


def _sum_devices(rows):
    def body(r_ref, o_ref):
        acc = r_ref[0]
        for d in range(1, 8):
            acc = acc + r_ref[d]
        o_ref[...] = acc

    vm = pl.BlockSpec(memory_space=pltpu.VMEM)
    return pl.pallas_call(body, name="sum_devices", out_shape=jax.ShapeDtypeStruct(rows.shape[1:], rows.dtype),
                          in_specs=[vm], out_specs=vm)(rows)


def _add_half(g, got):
    n, R, C = g.shape
    hc = C // 2

    def body(c_ref, g_ref, r_ref, o_ref):
        o_ref[...] = (g_ref[...] + r_ref[...]).astype(BF16)

    c = lax.axis_index("c")
    return pl.pallas_call(
        body, name="add_half",
        grid_spec=pltpu.PrefetchScalarGridSpec(
            num_scalar_prefetch=1, grid=(n,),
            in_specs=[pl.BlockSpec((1, R, hc), lambda k, c_ref: (k, 0, c_ref[0])),
                      pl.BlockSpec((1, R, hc), lambda k, c_ref: (k, 0, 0))],
            out_specs=pl.BlockSpec((1, R, hc), lambda k, c_ref: (k, 0, 0))),
        out_shape=jax.ShapeDtypeStruct((n, R, hc), BF16),
        compiler_params=_params(("arbitrary",)),
    )(jnp.reshape(c, (1,)).astype(jnp.int32), g, got)


def _sum_slabs(g, got, recv):
    _, R, C = g.shape
    hc = C // 2

    def body(kc_ref, g_ref, s_ref, r_ref, o_ref):
        o_ref[...] = (((g_ref[0] + s_ref[0]) + r_ref[0].astype(F32)) + r_ref[1].astype(F32)) + r_ref[2].astype(F32)

    kc = jnp.stack([2 * lax.axis_index("x") + lax.axis_index("y"), lax.axis_index("c")]).astype(jnp.int32)
    return pl.pallas_call(
        body, name="sum_slabs",
        grid_spec=pltpu.PrefetchScalarGridSpec(
            num_scalar_prefetch=1, grid=(1,),
            in_specs=[pl.BlockSpec((1, R, hc), lambda i, kc_ref: (kc_ref[0], 0, kc_ref[1])),
                      pl.BlockSpec((1, R, hc), lambda i, kc_ref: (kc_ref[0], 0, 0)),
                      pl.BlockSpec((3, R, hc), lambda i, kc_ref: (0, 0, 0))],
            out_specs=pl.BlockSpec((R, hc), lambda i, kc_ref: (0, kc_ref[1]))),
        out_shape=jax.ShapeDtypeStruct((R, C), F32),
        compiler_params=_params(("arbitrary",)),
    )(kc, g, got, recv)


def _row_tile(rows):
    for cand in (256, 184, 176, 144, 128, 64, 32, 16, 8):
        if rows % cand == 0:
            return cand
    return rows


def _in_proj(x, g1, w_in, w_q12, w_k, w_v, gq, gkv, bfg, ct, st, sel, seq, tm):
    T = x.shape[0]
    nsb = seq // tm

    def body(x_ref, g1_ref, win_ref, wq_ref, wk_ref, wv_ref, gq_ref, gkv_ref, b_ref, ct_ref, st_ref, sel_ref,
             h1_ref, qf_ref, kf_ref, vf_ref, qm_ref, km_ref, vm_ref, lat_ref, qn_ref, kvn_ref, carry):
        i = pl.program_id(0)

        @pl.when(i % nsb == 0)
        def _():
            carry[...] = jnp.zeros_like(carry)

        xv = x_ref[...]
        h = (xv * _rms(xv, D_MODEL) * g1_ref[...]).astype(BF16)
        h1_ref[...] = h
        proj = _dot_nt(h, win_ref[...])
        lane = lax.broadcasted_iota(jnp.int32, (tm, LANES), 1)
        low = lane < 64
        misc_a = proj[:, C_MA:C_END]
        misc_b = pltpu.roll(misc_a, 96, 1)

        z = misc_a + b_ref[...]
        lf = jnp.where(lane < HEADS, jnp.minimum(z, 0.0) - jnp.log1p(jnp.exp(-jnp.abs(z))), 0.0)
        rr = lax.broadcasted_iota(jnp.int32, (tm, tm), 0)
        cc = lax.broadcasted_iota(jnp.int32, (tm, tm), 1)
        tri = (rr >= cc).astype(BF16)
        a0, a1, a2 = _split3(lf)
        c = _dot(tri, a0) + _dot(tri, a1) + _dot(tri, a2) + carry[0:1, :]
        carry[0:1, :] = c[tm - 1:tm, :]
        c0, c1, c2 = _split3(c)
        cpl = _dot(jnp.concatenate([c0, c1, c2], axis=1), sel_ref[...])
        qpad = jnp.where((lane >= 64) & (lane < 67), -1.0, 0.0)
        for j in range(PAIRS):
            qc = proj[:, C_FQ + LANES * j:C_FQ + LANES * (j + 1)] * FOX_SCALE
            kc = proj[:, C_FK + LANES * j:C_FK + LANES * (j + 1)]
            e, o = 2 * LANES * j, 2 * LANES * j + LANES
            qf_ref[:, e:e + LANES] = jnp.where(low, qc, qpad).astype(BF16)
            qf_ref[:, o:o + LANES] = jnp.where(low, pltpu.roll(qc, 64, 1), qpad).astype(BF16)
            kf_ref[:, e:e + LANES] = jnp.where(low, kc, cpl[:, e:e + LANES]).astype(BF16)
            kf_ref[:, o:o + LANES] = jnp.where(low, pltpu.roll(kc, 64, 1), cpl[:, o:o + LANES]).astype(BF16)
        vf_ref[...] = proj[:, C_FV:C_QL].astype(BF16)

        ql = proj[:, C_QL:C_KVL]
        kvl = proj[:, C_KVL:C_MA]
        qn = (ql * _rms(ql, Q_RANK) * gq_ref[...]).astype(BF16)
        kvn = (kvl * _rms(kvl, KV_RANK) * gkv_ref[...]).astype(BF16)
        lat_ref[...] = proj[:, C_QL:C_MB]
        qn_ref[...] = qn
        kvn_ref[...] = kvn
        q12 = _dot_nt(qn, wq_ref[...])
        kn = _dot(kvn, wk_ref[...])
        ctv = ct_ref[...]
        stv = st_ref[...]
        cq = (jnp.where(low, 1.0, 0.0) + ctv) * MLA_SCALE
        sq = stv * MLA_SCALE
        kpe = misc_a * ctv + misc_b * stv
        for hd in range(HEADS):
            s0 = LANES * hd
            qm_ref[:, s0:s0 + LANES] = (q12[:, s0:s0 + LANES] * cq + q12[:, 1024 + s0:1024 + s0 + LANES] * sq).astype(BF16)
            km_ref[:, s0:s0 + LANES] = (kn[:, s0:s0 + LANES] + kpe).astype(BF16)
        vm_ref[...] = _dot(kvn, wv_ref[...]).astype(BF16)

    row = lambda w: pl.BlockSpec((tm, w), lambda i: (i, 0))
    out_shape = (
        jax.ShapeDtypeStruct((T, D_MODEL), BF16),
        jax.ShapeDtypeStruct((T, 1024), BF16), jax.ShapeDtypeStruct((T, 1024), BF16), jax.ShapeDtypeStruct((T, 512), BF16),
        jax.ShapeDtypeStruct((T, 1024), BF16), jax.ShapeDtypeStruct((T, 1024), BF16), jax.ShapeDtypeStruct((T, 512), BF16),
        jax.ShapeDtypeStruct((T, 512), F32),
        jax.ShapeDtypeStruct((T, Q_RANK), BF16), jax.ShapeDtypeStruct((T, KV_RANK), BF16),
    )
    return pl.pallas_call(
        body, name="in_proj", grid=(T // tm,), out_shape=out_shape,
        in_specs=[row(D_MODEL), _full(g1.shape), _full(w_in.shape), _full(w_q12.shape), _full(w_k.shape), _full(w_v.shape),
                  _full(gq.shape), _full(gkv.shape), _full(bfg.shape), row(LANES), row(LANES), _full(sel.shape)],
        out_specs=[row(D_MODEL), row(1024), row(1024), row(512), row(1024), row(1024), row(512), row(512), row(Q_RANK), row(KV_RANK)],
        scratch_shapes=[pltpu.VMEM((8, LANES), F32)],
        compiler_params=_params(("arbitrary",)),
    )(x, g1, w_in, w_q12, w_k, w_v, gq, gkv, bfg, ct, st, sel)


def _attn_fwd(q, k, v, nb, seq, tq, name, ex=None):
    T = q.shape[0]
    nq = seq // tq
    n_in, n_out = (len(ex.ins), len(ex.outs)) if ex else (0, 0)

    def body(*refs):
        q_ref, k_ref, v_ref = refs[0:3]
        o_ref, lse_ref = refs[3 + n_in:5 + n_in]
        b, pr, qi = pl.program_id(0), pl.program_id(1), pl.program_id(2)
        if ex:
            ex_refs = (refs[3:3 + n_in], refs[5 + n_in:5 + n_in + n_out], refs[8 + n_in + n_out:])

            @pl.when((b == 0) & (pr == 0) & (qi == 0))
            def _():
                ex.start(*ex_refs)

        s_sc, p_sc, acc_sc = refs[5 + n_in + n_out:8 + n_in + n_out]
        strip = 64
        key_s = lax.broadcasted_iota(jnp.int32, (strip, tq), 0)
        qry_s = lax.broadcasted_iota(jnp.int32, (strip, tq), 1)
        row_t = lax.broadcasted_iota(jnp.int32, (LANES, tq), 0)
        acc_sc[...] = jnp.zeros(acc_sc.shape, F32)

        def fold(x, op):
            out = x[0:8]
            for r in range(8, strip, 8):
                out = op(out, x[r:r + 8])
            return out

        def step(kj, state, masked):
            rows = pl.ds(pl.multiple_of(kj * tq, tq), tq)
            for hh in range(2):
                s_sc[hh] = _dot_nt(k_ref[rows, LANES * hh:LANES * (hh + 1)], q_ref[:, LANES * hh:LANES * (hh + 1)])
            vv = v_ref[rows, :]
            new = []
            for hh in range(2):
                m, l = state[hh]

                def strip_of(r0, hh=hh):
                    s = s_sc[hh, r0:r0 + strip, :]
                    return jnp.where(key_s + r0 <= qry_s, s, NEG) if masked else s

                mx = fold(strip_of(0), jnp.maximum)
                for r0 in range(strip, tq, strip):
                    mx = jnp.maximum(mx, fold(strip_of(r0), jnp.maximum))
                m_new = jnp.maximum(m, jnp.max(mx, axis=0, keepdims=True))
                alpha = jnp.exp(m - m_new)
                sm = jnp.zeros((8, tq), F32)
                for r0 in range(0, tq, strip):
                    p = jnp.exp(strip_of(r0) - m_new)
                    sm = sm + fold(p, jnp.add)
                    p_sc[hh, r0:r0 + strip, :] = p.astype(BF16)
                l = alpha * l + jnp.sum(sm, axis=0, keepdims=True)
                acc_sc[hh] = alpha * acc_sc[hh] + _dot_tn(vv, p_sc[hh])
                new.append((m_new, l))
            return tuple(new)

        one = (jnp.full((1, tq), NEG, F32), jnp.zeros((1, tq), F32))
        state = lax.fori_loop(0, qi, functools.partial(step, masked=False), (one, one))
        (m0, l0), (m1, l1) = step(qi, state, True)
        o_ref[...] = jnp.where(row_t < 64, acc_sc[0] / l0, acc_sc[1] / l1).T
        lse_ref[:, 0:LANES] = jnp.broadcast_to(m0 + jnp.log(l0), (LANES, tq)).T
        lse_ref[:, LANES:2 * LANES] = jnp.broadcast_to(m1 + jnp.log(l1), (LANES, tq)).T

        if ex:
            @pl.when((b == nb - 1) & (pr == PAIRS - 1) & (qi == nq - 1))
            def _():
                ex.wait(*ex_refs)

    res = pl.pallas_call(
        body, name=name, grid=(nb, PAIRS, nq),
        out_shape=(jax.ShapeDtypeStruct((T, 512), F32), jax.ShapeDtypeStruct((T, 1024), F32)) + tuple(ex.outs if ex else ()),
        in_specs=[pl.BlockSpec((tq, 2 * LANES), lambda b, p, i: (b * nq + i, p)),
                  pl.BlockSpec((seq, 2 * LANES), lambda b, p, i: (b, p)),
                  pl.BlockSpec((seq, LANES), lambda b, p, i: (b, p))] + [ANY] * n_in,
        out_specs=[pl.BlockSpec((tq, LANES), lambda b, p, i: (b * nq + i, p)),
                   pl.BlockSpec((tq, 2 * LANES), lambda b, p, i: (b * nq + i, p))] + [ANY] * n_out,
        scratch_shapes=[pltpu.VMEM((2, tq, tq), F32), pltpu.VMEM((2, tq, tq), BF16), pltpu.VMEM((2, LANES, tq), F32)]
        + (ex.sems() if ex else []),
        compiler_params=_params(("arbitrary", "arbitrary", "arbitrary")),
    )(q, k, v, *(ex.ins if ex else ()))
    return res[0], res[1], list(res[2:])


def _attn_bwd(q, k, v, o, do, lse, nb, seq, tq, name, key_bias, ex=None):
    T = q.shape[0]
    nq = seq // tq
    n_in, n_out = (len(ex.ins), len(ex.outs)) if ex else (0, 0)
    n_res = 4 if key_bias else 3

    def body(*refs):
        q_ref, k_ref, v_ref, o_ref, do_ref, lse_ref = refs[0:6]
        dq_ref, dk_ref, dv_ref = refs[6 + n_in:9 + n_in]
        dcb_ref = refs[9 + n_in] if key_bias else None
        first_scratch = 6 + n_in + n_res + n_out
        dsc, rsum, dq_acc = refs[first_scratch:first_scratch + 3]
        b, pr, step_no = pl.program_id(0), pl.program_id(1), pl.program_id(2)
        kj = nq - 1 - step_no
        if ex:
            ex_refs = (refs[6:6 + n_in], refs[6 + n_in + n_res:6 + n_in + n_res + n_out], refs[first_scratch + 3:])

            @pl.when((b == 0) & (pr == 0) & (step_no == 0))
            def _():
                ex.start(*ex_refs)

        lane_s = lax.broadcasted_iota(jnp.int32, (seq, LANES), 1)
        lane = lax.broadcasted_iota(jnp.int32, (tq, LANES), 1)
        rr = lax.broadcasted_iota(jnp.int32, (tq, tq), 0)
        cc = lax.broadcasted_iota(jnp.int32, (tq, tq), 1)

        @pl.when(step_no == 0)
        def _():
            dq_acc[...] = jnp.zeros_like(dq_acc)
            prod = do_ref[...].astype(F32) * o_ref[...]
            d0 = jnp.sum(jnp.where(lane_s < 64, prod, 0.0), axis=1, keepdims=True)
            d1 = jnp.sum(jnp.where(lane_s < 64, 0.0, prod), axis=1, keepdims=True)
            dsc[0] = jnp.broadcast_to(d0, (seq, LANES))
            dsc[1] = jnp.broadcast_to(d1, (seq, LANES))
            if key_bias:
                rsum[...] = jnp.zeros_like(rsum)

        if key_bias:
            @pl.when((pr == 0) & (step_no == 0))
            def _():
                dcb_ref[...] = jnp.zeros_like(dcb_ref)

        vv = v_ref[...]

        def step(qi, carry, masked):
            dkt, dvt, cols = carry
            rows = pl.ds(pl.multiple_of(qi * tq, tq), tq)
            dov = do_ref[rows, :]
            new_dkt, new_cols = [], []
            for hh in range(2):
                qv = q_ref[rows, LANES * hh:LANES * (hh + 1)]
                kv = k_ref[:, LANES * hh:LANES * (hh + 1)]
                dom = jnp.where((lane < 64) if hh == 0 else (lane >= 64), dov, jnp.zeros((), BF16))
                s = _dot_nt(qv, kv)
                if masked:
                    s = jnp.where(cc <= rr, s, NEG)
                p = jnp.exp(s - jnp.tile(lse_ref[rows, LANES * hh:LANES * (hh + 1)], (1, tq // LANES)))
                dp = _dot_nt(dom, vv)
                ds32 = p * (dp - jnp.tile(dsc[hh, rows, :], (1, tq // LANES)))
                col = cols[hh]
                if key_bias:
                    col = col + jnp.sum(ds32, axis=0, keepdims=True)
                    rsum[hh, rows, :] += jnp.broadcast_to(jnp.sum(ds32, axis=1, keepdims=True), (tq, LANES))
                ds = ds32.astype(BF16)
                dvt = dvt + _dot_tn(dom, p.astype(BF16))
                new_dkt.append(dkt[hh] + _dot_tn(qv, ds))
                new_cols.append(col)
                dq_acc[rows, LANES * hh:LANES * (hh + 1)] += _dot(ds, kv)
            return tuple(new_dkt), dvt, tuple(new_cols)

        zt = jnp.zeros((LANES, tq), F32)
        zc = jnp.zeros((1, tq), F32)
        carry = step(kj, ((zt, zt), zt, (zc, zc)), True)
        dkt, dvt, cols = lax.fori_loop(kj + 1, nq, functools.partial(step, masked=False), carry)
        for hh in range(2):
            dk_ref[:, LANES * hh:LANES * (hh + 1)] = dkt[hh].T.astype(dk_ref.dtype)
        dv_ref[...] = dvt.T.astype(dv_ref.dtype)
        if key_bias:
            row_t = lax.broadcasted_iota(jnp.int32, (LANES, tq), 0)
            per_key = jnp.where(row_t == 2 * pr, -cols[0], 0.0) + jnp.where(row_t == 2 * pr + 1, -cols[1], 0.0)
            dcb_ref[pl.ds(pl.multiple_of(kj * tq, tq), tq), :] += per_key.T

        @pl.when(step_no == nq - 1)
        def _():
            dq_ref[...] = dq_acc[...].astype(dq_ref.dtype)
            if key_bias:
                dcb_ref[...] += jnp.where(lane_s == 2 * pr, rsum[0], 0.0) + jnp.where(lane_s == 2 * pr + 1, rsum[1], 0.0)

        if ex:
            @pl.when((b == nb - 1) & (pr == PAIRS - 1) & (step_no == nq - 1))
            def _():
                ex.wait(*ex_refs)

    per_seq = lambda w: pl.BlockSpec((seq, w), lambda b, p, j: (b, p))
    per_blk = lambda w: pl.BlockSpec((tq, w), lambda b, p, j: (b * nq + nq - 1 - j, p))
    res = pl.pallas_call(
        body, name=name, grid=(nb, PAIRS, nq),
        out_shape=(jax.ShapeDtypeStruct((T, 1024), BF16), jax.ShapeDtypeStruct((T, 1024), BF16), jax.ShapeDtypeStruct((T, 512), BF16))
        + ((jax.ShapeDtypeStruct((T, LANES), F32),) if key_bias else ()) + tuple(ex.outs if ex else ()),
        in_specs=[per_seq(2 * LANES), per_blk(2 * LANES), per_blk(LANES), per_seq(LANES), per_seq(LANES), per_seq(2 * LANES)] + [ANY] * n_in,
        out_specs=[per_seq(2 * LANES), per_blk(2 * LANES), per_blk(LANES)]
        + ([pl.BlockSpec((seq, LANES), lambda b, p, j: (b, 0))] if key_bias else []) + [ANY] * n_out,
        scratch_shapes=[pltpu.VMEM((2, seq, LANES), F32), pltpu.VMEM((2, seq, LANES) if key_bias else (2, 8, LANES), F32),
                        pltpu.VMEM((seq, 2 * LANES), F32)]
        + (ex.sems() if ex else []),
        compiler_params=_params(("arbitrary", "arbitrary", "arbitrary")),
    )(q, k, v, o, do, lse, *(ex.ins if ex else ()))
    return list(res[:n_res]), list(res[n_res:])


def _mid(of, om, x, tgt, g_fo, g_mo, g2, g3, w_o, w_g, w_u, w_d, tm):
    T = x.shape[0]

    def body(of_ref, om_ref, x_ref, t_ref, gfo_ref, gmo_ref, g2_ref, g3_ref, wo_ref, wg_ref, wu_ref, wd_ref,
             a_ref, h2_ref, hid_ref, dg_ref, du_ref, dx3_ref, dx2_ref, dof_ref, dom_ref, st_ref):
        i = pl.program_id(0)

        @pl.when(i == 0)
        def _():
            st_ref[...] = jnp.zeros_like(st_ref)

        ofv, omv = of_ref[...], om_ref[...]
        rf, rm = _rms(ofv, FOX_W), _rms(omv, FOX_W)
        fhat, mhat = ofv * rf, omv * rm
        a = jnp.concatenate([fhat * gfo_ref[...], mhat * gmo_ref[...]], axis=1).astype(BF16)
        a_ref[...] = a
        x2 = x_ref[...] + _dot(a, wo_ref[...])
        r2 = _rms(x2, D_MODEL)
        xh2 = x2 * r2
        h2 = (xh2 * g2_ref[...]).astype(BF16)
        h2_ref[...] = h2
        gt = _dot_nt(h2, wg_ref[...])
        up = _dot_nt(h2, wu_ref[...])
        sg = jax.nn.sigmoid(gt)
        sl = gt * sg
        hid = (sl * up).astype(BF16)
        hid_ref[...] = hid
        x3 = x2 + _dot(hid, wd_ref[...])
        r3 = _rms(x3, D_MODEL)
        xh3 = x3 * r3
        diff = xh3 * g3_ref[...] - t_ref[...]
        dy = diff * (1.0 / D_MODEL)
        st_ref[ROW_LOSS:ROW_LOSS + 1, :] += jnp.sum(diff * diff, axis=0, keepdims=True) * (0.5 / D_MODEL)
        st_ref[ROW_FINAL:ROW_FINAL + 1, :] += jnp.sum(dy * xh3, axis=0, keepdims=True)
        dx3 = _rms_bwd(dy, xh3, r3, g3_ref[...], D_MODEL)
        dx3b = dx3.astype(BF16)
        dx3_ref[...] = dx3b
        dhid = _dot_nt(dx3b, wd_ref[...])
        dg = (dhid * up * (sg * (1.0 + gt * (1.0 - sg)))).astype(BF16)
        du = (dhid * sl).astype(BF16)
        dg_ref[...] = dg
        du_ref[...] = du
        dh2 = _dot(dg, wg_ref[...]) + _dot(du, wu_ref[...])
        st_ref[ROW_NORM_FFN:ROW_NORM_FFN + 1, :] += jnp.sum(dh2 * xh2, axis=0, keepdims=True)
        dx2 = dx3 + _rms_bwd(dh2, xh2, r2, g2_ref[...], D_MODEL)
        dx2_ref[...] = dx2
        da = _dot_nt(dx2.astype(BF16), wo_ref[...])
        daf, dam = da[:, 0:FOX_W], da[:, FOX_W:2 * FOX_W]
        st_ref[ROW_OUT:ROW_OUT + 1, 0:FOX_W] += jnp.sum(daf * fhat, axis=0, keepdims=True)
        st_ref[ROW_OUT:ROW_OUT + 1, FOX_W:2 * FOX_W] += jnp.sum(dam * mhat, axis=0, keepdims=True)
        dof_ref[...] = _rms_bwd(daf, fhat, rf, gfo_ref[...], FOX_W).astype(BF16)
        dom_ref[...] = _rms_bwd(dam, mhat, rm, gmo_ref[...], FOX_W).astype(BF16)

    row = lambda w: pl.BlockSpec((tm, w), lambda i: (i, 0))
    ff = jax.ShapeDtypeStruct((T, D_FF), BF16)
    out_shape = (
        jax.ShapeDtypeStruct((T, 1024), BF16), jax.ShapeDtypeStruct((T, 1024), BF16), ff, ff, ff,
        jax.ShapeDtypeStruct((T, 1024), BF16), jax.ShapeDtypeStruct((T, 1024), F32),
        jax.ShapeDtypeStruct((T, 512), BF16), jax.ShapeDtypeStruct((T, 512), BF16), jax.ShapeDtypeStruct((8, 1024), F32),
    )
    return pl.pallas_call(
        body, name="mid", grid=(T // tm,), out_shape=out_shape,
        in_specs=[row(512), row(512), row(1024), row(1024), _full(g_fo.shape), _full(g_mo.shape), _full(g2.shape), _full(g3.shape),
                  _full(w_o.shape), _full(w_g.shape), _full(w_u.shape), _full(w_d.shape)],
        out_specs=[row(1024), row(1024), row(D_FF), row(D_FF), row(D_FF), row(1024), row(1024), row(512), row(512),
                   pl.BlockSpec((8, 1024), lambda i: (0, 0))],
        compiler_params=_params(("arbitrary",)),
    )(of, om, x, tgt, g_fo, g_mo, g2, g3, w_o, w_g, w_u, w_d)


def _in_bwd(dqf, dkf, dvf, dcb, dqm, dkm, dvm, lat, qn, kvn, x, dx2, g1, gq, gkv, bfg, ct, st, w_in, w_q12, w_kv, seq, tm):
    T = x.shape[0]
    nblk = T // tm
    nsb = seq // tm

    def body(dqf_ref, dkf_ref, dvf_ref, dcb_ref, dqm_ref, dkm_ref, dvm_ref, lat_ref, qn_ref, kvn_ref, x_ref, dx2_ref, g1_ref, gq_ref,
             gkv_ref, b_ref, ct_ref, st_ref, win_ref, wq_ref, wkv_ref, dx_ref, dproj_ref, gq12_ref, gkv12_ref, stat_ref, carry,
             dq12_ref, dkv_ref):
        i = pl.program_id(0)

        @pl.when(i == 0)
        def _():
            stat_ref[...] = jnp.zeros_like(stat_ref)
            gq12_ref[...] = jnp.zeros_like(gq12_ref)
            gkv12_ref[...] = jnp.zeros_like(gkv12_ref)

        @pl.when(i % nsb == 0)
        def _():
            carry[...] = jnp.zeros_like(carry)

        lane = lax.broadcasted_iota(jnp.int32, (tm, LANES), 1)
        low = lane < 64
        ctv, stv = ct_ref[...], st_ref[...]

        for j in range(PAIRS):
            e, o = 2 * LANES * j, 2 * LANES * j + LANES
            half = lambda ref, c0: jnp.where(low, ref[:, c0:c0 + LANES].astype(F32), 0.0)
            dq = half(dqf_ref, e) + pltpu.roll(half(dqf_ref, o), 64, 1)
            dk = half(dkf_ref, e) + pltpu.roll(half(dkf_ref, o), 64, 1)
            dproj_ref[:, C_FQ + LANES * j:C_FQ + LANES * (j + 1)] = (dq * FOX_SCALE).astype(BF16)
            dproj_ref[:, C_FK + LANES * j:C_FK + LANES * (j + 1)] = dk.astype(BF16)
        dproj_ref[:, C_FV:C_QL] = dvf_ref[...]
        dc = dcb_ref[...]
        rr = lax.broadcasted_iota(jnp.int32, (tm, tm), 0)
        cc = lax.broadcasted_iota(jnp.int32, (tm, tm), 1)
        triu = (cc >= rr).astype(BF16)
        a0, a1, a2 = _split3(dc)
        dlf = _dot(triu, a0) + _dot(triu, a1) + _dot(triu, a2) + carry[0:1, :]
        carry[0:1, :] = dlf[0:1, :]
        misc_a = lat_ref[:, Q_RANK + KV_RANK:Q_RANK + KV_RANK + LANES]
        z = misc_a + b_ref[...]
        dz = jnp.where(lane < HEADS, dlf * jax.nn.sigmoid(-z), 0.0)
        stat_ref[ROW_B:ROW_B + 1, 0:LANES] += jnp.sum(dz, axis=0, keepdims=True)

        cq = (jnp.where(low, 1.0, 0.0) + ctv) * MLA_SCALE
        sq = stv * MLA_SCALE
        dkpe = jnp.zeros((tm, LANES), F32)
        for hd in range(HEADS):
            s0 = LANES * hd
            dqh = dqm_ref[:, s0:s0 + LANES].astype(F32)
            dq12_ref[:, s0:s0 + LANES] = (dqh * cq).astype(BF16)
            dq12_ref[:, 1024 + s0:1024 + s0 + LANES] = (dqh * sq).astype(BF16)
            dkpe = dkpe + dkm_ref[:, s0:s0 + LANES].astype(F32)
        dkv_ref[:, 0:1024] = dkm_ref[...]
        dkv_ref[:, 1024:1536] = dvm_ref[...]
        dproj_ref[:, C_MA:C_END] = (dz + dkpe * ctv + pltpu.roll(dkpe * stv, 32, 1)).astype(BF16)
        dqn = _dot(dq12_ref[...], wq_ref[...])
        dkvn = _dot_nt(dkv_ref[...], wkv_ref[...])
        gq12_ref[...] += _dot_tn(dq12_ref[...], qn_ref[...])
        gkv12_ref[...] += _dot_tn(kvn_ref[...], dkv_ref[...])
        ql = lat_ref[:, 0:Q_RANK]
        kvl = lat_ref[:, Q_RANK:Q_RANK + KV_RANK]
        rq, rkv = _rms(ql, Q_RANK), _rms(kvl, KV_RANK)
        qhat, kvhat = ql * rq, kvl * rkv
        stat_ref[ROW_Q:ROW_Q + 1, 0:Q_RANK] += jnp.sum(dqn * qhat, axis=0, keepdims=True)
        stat_ref[ROW_KV:ROW_KV + 1, 0:KV_RANK] += jnp.sum(dkvn * kvhat, axis=0, keepdims=True)
        dproj_ref[:, C_QL:C_KVL] = _rms_bwd(dqn, qhat, rq, gq_ref[...], Q_RANK).astype(BF16)
        dproj_ref[:, C_KVL:C_MA] = _rms_bwd(dkvn, kvhat, rkv, gkv_ref[...], KV_RANK).astype(BF16)

        dh1 = _dot(dproj_ref[...], win_ref[...])
        xv = x_ref[...]
        r1 = _rms(xv, D_MODEL)
        xh = xv * r1
        stat_ref[ROW_NORM_MIX:ROW_NORM_MIX + 1, :] += jnp.sum(dh1 * xh, axis=0, keepdims=True)
        dx_ref[...] = dx2_ref[...] + _rms_bwd(dh1, xh, r1, g1_ref[...], D_MODEL)

    rev = lambda w: pl.BlockSpec((tm, w), lambda i: (nblk - 1 - i, 0))
    whole = lambda r, c: pl.BlockSpec((r, c), lambda i: (0, 0))
    out_shape = (
        jax.ShapeDtypeStruct((T, 1024), F32), jax.ShapeDtypeStruct((T, C_END), BF16), jax.ShapeDtypeStruct((2048, Q_RANK), F32),
        jax.ShapeDtypeStruct((KV_RANK, 1536), F32), jax.ShapeDtypeStruct((8, 1024), F32),
    )
    return pl.pallas_call(
        body, name="in_bwd", grid=(nblk,), out_shape=out_shape,
        in_specs=[rev(1024), rev(1024), rev(512), rev(LANES), rev(1024), rev(1024), rev(512), rev(512), rev(Q_RANK), rev(KV_RANK),
                  rev(1024), rev(1024), _full(g1.shape), _full(gq.shape), _full(gkv.shape), _full(bfg.shape), rev(LANES), rev(LANES),
                  _full(w_in.shape), _full(w_q12.shape), _full(w_kv.shape)],
        out_specs=[rev(1024), rev(C_END), whole(2048, Q_RANK), whole(KV_RANK, 1536), whole(8, 1024)],
        scratch_shapes=[pltpu.VMEM((8, LANES), F32), pltpu.VMEM((tm, 2048), BF16), pltpu.VMEM((tm, 1536), BF16)],
        compiler_params=_params(("arbitrary",)),
    )(dqf, dkf, dvf, dcb, dqm, dkm, dvm, lat, qn, kvn, x, dx2, g1, gq, gkv, bfg, ct, st, w_in, w_q12, w_kv)


def _wgrad(a, b, tk, tt, name, ex=None):
    T, K = a.shape
    N = b.shape[1]
    n_in, n_out = (len(ex.ins), len(ex.outs)) if ex else (0, 0)
    gk, gt = K // tk, T // tt

    def body(*refs):
        a_ref, b_ref, o_ref = refs[0], refs[1], refs[2 + n_in]
        kb, t = pl.program_id(0), pl.program_id(1)
        if ex:
            ex_refs = (refs[2:2 + n_in], refs[3 + n_in:3 + n_in + n_out], refs[3 + n_in + n_out:])

            @pl.when((kb == 0) & (t == 0))
            def _():
                ex.start(*ex_refs)

        @pl.when(t == 0)
        def _():
            o_ref[...] = jnp.zeros_like(o_ref)

        o_ref[...] += _dot_tn(a_ref[...].astype(BF16), b_ref[...].astype(BF16))

        if ex:
            @pl.when((kb == gk - 1) & (t == gt - 1))
            def _():
                ex.wait(*ex_refs)

    res = pl.pallas_call(
        body, name=name, grid=(gk, gt), out_shape=(jax.ShapeDtypeStruct((K, N), F32),) + tuple(ex.outs if ex else ()),
        in_specs=[pl.BlockSpec((tt, tk), lambda kb, t: (t, kb)), pl.BlockSpec((tt, N), lambda kb, t: (t, 0))] + [ANY] * n_in,
        out_specs=[pl.BlockSpec((tk, N), lambda kb, t: (kb, 0))] + [ANY] * n_out,
        scratch_shapes=ex.sems() if ex else [], input_output_aliases=ex.aliases(2, 1) if ex else {},
        compiler_params=_params(("arbitrary", "arbitrary")),
    )(a, b, *(ex.ins if ex else ()))
    return (res[0], list(res[1:])) if ex else res[0]


def _adam_update(w, g, m, v):
    nm = ADAM_B1 * m + (1.0 - ADAM_B1) * g
    nv = ADAM_B2 * v + (1.0 - ADAM_B2) * (g * g)
    m_hat = nm / (1.0 - ADAM_B1 ** ADAM_STEP)
    v_hat = nv / (1.0 - ADAM_B2 ** ADAM_STEP)
    return -ADAM_LR * (m_hat / (jnp.sqrt(v_hat) + ADAM_EPS) + ADAM_WD * w), nm, nv


def _adamw_small(stats, params):
    k = len(SMALL)

    def body(*refs):
        for t, name in enumerate(SMALL):
            row, c0, width = SMALL_AT[name]
            w_ref, m_ref, v_ref = refs[1 + 3 * t:4 + 3 * t]
            g_ref, d_ref, nm_ref, nv_ref = refs[1 + 3 * k + 4 * t:5 + 3 * k + 4 * t]
            g = refs[0][row:row + 1, c0:c0 + width]
            g_ref[...] = g
            d_ref[...], nm_ref[...], nv_ref[...] = _adam_update(w_ref[...], g, m_ref[...], v_ref[...])

    vm = pl.BlockSpec(memory_space=pltpu.VMEM)
    out_shape = tuple(jax.ShapeDtypeStruct((1, SMALL_AT[name][2]), F32) for name in SMALL for _ in range(4))
    res = pl.pallas_call(body, name="adamw_small", out_shape=out_shape, in_specs=[vm] * (1 + 3 * k), out_specs=tuple([vm] * (4 * k)))(
        stats, *[a for name in SMALL for a in params[name]])
    return {name: tuple(res[4 * t:4 * t + 4]) for t, name in enumerate(SMALL)}


def _adamw(tensors, name):
    n = len(tensors)
    R, C = tensors[0][0].shape
    tr = _row_tile(R)

    def body(*refs):
        for t in range(n):
            w_ref, g_ref, m_ref, v_ref = refs[4 * t:4 * t + 4]
            d_ref, nm_ref, nv_ref = refs[4 * n + 3 * t:4 * n + 3 * t + 3]
            d_ref[...], nm_ref[...], nv_ref[...] = _adam_update(w_ref[...], g_ref[...], m_ref[...], v_ref[...])

    blk = pl.BlockSpec((tr, C), lambda i: (i, 0))
    sh = jax.ShapeDtypeStruct((R, C), F32)
    res = pl.pallas_call(
        body, name=name, grid=(R // tr,), out_shape=(sh,) * (3 * n),
        in_specs=[blk] * (4 * n), out_specs=[blk] * (3 * n),
        compiler_params=_params(("arbitrary",)),
    )(*[a for t in tensors for a in t])
    return [tuple(res[3 * t:3 * t + 3]) for t in range(n)]


def _arrange(win_t, wuq_t, wukv):
    dt = win_t.dtype
    z = lambda r: jnp.zeros((r, D_MODEL), dt)
    zh = lambda r: jnp.zeros((HEADS, r, Q_RANK), dt)
    kr1, kr2 = win_t[1928:1944], win_t[1944:1960]
    misc = jnp.concatenate([win_t[1536:1544], z(56), kr1, kr2, kr2, kr1], axis=0)
    w_in = jnp.concatenate([win_t[0:1536], win_t[1544:1928], misc], axis=0)
    wq = wuq_t.reshape(HEADS, 96, Q_RANK)
    q1 = jnp.concatenate([wq, zh(32)], axis=1).reshape(1024, Q_RANK)
    q2 = jnp.concatenate([zh(64), wq[:, 80:96], wq[:, 64:80], zh(32)], axis=1).reshape(1024, Q_RANK)
    wkv = wukv.reshape(KV_RANK, HEADS, 128)
    wk = jnp.concatenate([wkv[:, :, 0:64], jnp.zeros((KV_RANK, HEADS, 64), dt)], axis=2).reshape(KV_RANK, 1024)
    wv = wkv[:, :, 64:128].reshape(KV_RANK, 512)
    return dict(w_in=w_in, w_q12=jnp.concatenate([q1, q2], axis=0), w_k=wk, w_v=wv, w_kv=jnp.concatenate([wk, wv], axis=1))


def _unarrange(g_in, g_q12, g_kv):
    kr1 = g_in[C_MA + 64:C_MA + 80] + g_in[C_MA + 112:C_MA + 128]
    kr2 = g_in[C_MA + 80:C_MA + 96] + g_in[C_MA + 96:C_MA + 112]
    win_t = jnp.concatenate([g_in[0:1536], g_in[C_MA:C_MA + 8], g_in[1536:1920], kr1, kr2], axis=0)
    g1 = g_q12[0:1024].reshape(HEADS, 128, Q_RANK)
    g2 = g_q12[1024:2048].reshape(HEADS, 128, Q_RANK)
    wuq_t = jnp.concatenate([g1[:, 0:64], g1[:, 64:80] + g2[:, 80:96], g1[:, 80:96] + g2[:, 64:80]], axis=1).reshape(768, Q_RANK)
    gk = g_kv[:, 0:1024].reshape(KV_RANK, HEADS, 128)
    gv = g_kv[:, 1024:1536].reshape(KV_RANK, HEADS, 64)
    wukv = jnp.concatenate([gk[:, :, 0:64], gv], axis=2).reshape(KV_RANK, 1024)
    return win_t, wuq_t, wukv


def _selector():
    sel = np.zeros((384, 1024), np.float32)
    for h in range(HEADS):
        for piece in range(3):
            sel[LANES * piece + h, LANES * h + 64 + piece] = 1.0
    return jnp.asarray(sel, BF16)


def _rope_tables(positions):
    inv_freq = 10000.0 ** (-jnp.arange(0, ROPE, 2, dtype=F32) / ROPE)
    n = positions.size
    ang = (positions.reshape(n // 8, 8, 1).astype(F32) * inv_freq[None, None, :]).reshape(n // 8, 8 * (ROPE // 2))
    cos, sin = lax.optimization_barrier((jnp.cos(lax.optimization_barrier(ang)), jnp.sin(lax.optimization_barrier(ang))))
    cos, sin = cos.reshape(n, ROPE // 2), sin.reshape(n, ROPE // 2)
    z64, z32 = jnp.zeros((n, 64), F32), jnp.zeros((n, 32), F32)
    return jnp.concatenate([z64, cos, cos, z32], axis=1), jnp.concatenate([z64, -sin, sin, z32], axis=1)


def _work(name, t):
    return jnp.swapaxes(t[0], 0, 1) if name in TRANSPOSED else t[0]


def _back(name, t):
    return (jnp.swapaxes(t, 0, 1) if name in TRANSPOSED else t)[None]


def kernel(x, positions, norm_mix_g, w_in, b_fgate, q_norm_g, w_uq, kv_norm_g, w_ukv, fox_out_g, mla_out_g, w_o, norm_ffn_g, w_gate, w_up, w_down, final_norm_g, loss_target, m_norm_mix_g, m_w_in, m_b_fgate, m_q_norm_g, m_w_uq, m_kv_norm_g, m_w_ukv, m_fox_out_g, m_mla_out_g, m_w_o, m_norm_ffn_g, m_w_gate, m_w_up, m_w_down, m_final_norm_g, v_norm_mix_g, v_w_in, v_b_fgate, v_q_norm_g, v_w_uq, v_kv_norm_g, v_w_ukv, v_fox_out_g, v_mla_out_g, v_w_o, v_norm_ffn_g, v_w_gate, v_w_up, v_w_down, v_final_norm_g):
    names = ["norm_mix_g", "w_in", "b_fgate", "q_norm_g", "w_uq", "kv_norm_g", "w_ukv", "fox_out_g", "mla_out_g", "w_o",
             "norm_ffn_g", "w_gate", "w_up", "w_down", "final_norm_g"]
    wts = dict(zip(names, [norm_mix_g, w_in, b_fgate, q_norm_g, w_uq, kv_norm_g, w_ukv, fox_out_g, mla_out_g, w_o, norm_ffn_g,
                           w_gate, w_up, w_down, final_norm_g]))
    mom = dict(zip(names, [m_norm_mix_g, m_w_in, m_b_fgate, m_q_norm_g, m_w_uq, m_kv_norm_g, m_w_ukv, m_fox_out_g, m_mla_out_g,
                           m_w_o, m_norm_ffn_g, m_w_gate, m_w_up, m_w_down, m_final_norm_g]))
    var = dict(zip(names, [v_norm_mix_g, v_w_in, v_b_fgate, v_q_norm_g, v_w_uq, v_kv_norm_g, v_w_ukv, v_fox_out_g, v_mla_out_g,
                           v_w_o, v_norm_ffn_g, v_w_gate, v_w_up, v_w_down, v_final_norm_g]))
    shard = {n: _work(n, wts[n]) for n in HEAD3 + FFN4}
    nb, seq, _ = x.shape
    T = nb * seq
    tm, tq = min(ROW_TILE, seq), min(ATTN_TILE, seq)
    tt = min(WGRAD_TILE, T)
    xf = x.reshape(T, D_MODEL)
    tgt = loss_target.reshape(T, D_MODEL)
    chip = 2 * lax.axis_index("x") + lax.axis_index("y")

    mine = [shard[n].astype(BF16) for n in HEAD3]
    head = _run_exchange(_gather_split_exchange(mine), "gather_head")
    win4, wuq4, wukv4 = [lax.dynamic_update_slice(h, s[None], (chip, 0, 0)) for h, s in zip(head, mine)]
    a = _arrange(win4.reshape(-1, D_MODEL), wuq4.reshape(-1, Q_RANK), wukv4.transpose(1, 0, 2).reshape(KV_RANK, -1))
    sel = _selector()
    ct, st = _rope_tables(positions)
    bfg = jnp.concatenate([b_fgate, jnp.zeros((1, LANES - HEADS), F32)], axis=1)
    g1, gq, gkv = norm_mix_g, q_norm_g, kv_norm_g

    h1, qf, kf, vf, qm, km, vm, lat, qn, kvn = _in_proj(xf, g1, a["w_in"], a["w_q12"], a["w_k"], a["w_v"], gq, gkv, bfg, ct, st, sel, seq,
                                                        min(IN_PROJ_TILE, seq))
    tqf = min(ATTN_FWD_TILE, seq)
    of, lse_f, (wo4, wg4) = _attn_fwd(qf, kf, vf, nb, seq, tqf, "fox_fwd", _gather_exchange([shard[n].astype(BF16) for n in FFN4[:2]]))
    om, lse_m, (wu4, wd4) = _attn_fwd(qm, km, vm, nb, seq, tqf, "mla_fwd", _gather_exchange([shard[n].astype(BF16) for n in FFN4[2:]]))
    a_cat, h2, hid, dg, du, dx3, dx2, dof, dom, st_mid = _mid(
        of, om, xf, tgt, fox_out_g, mla_out_g, norm_ffn_g, final_norm_g.reshape(1, D_MODEL),
        wo4.reshape(D_MODEL, D_MODEL), wg4.reshape(D_FF, D_MODEL), wu4.reshape(D_FF, D_MODEL), wd4.reshape(D_FF, D_MODEL), tm)

    slab = lambda g: g.reshape(N_CHIPS, g.shape[0] // N_CHIPS, g.shape[1])
    big = [slab(_wgrad(a_cat, dx2, D_MODEL, tt, "wgrad_o")), slab(_wgrad(dg, h2, D_FF // 2, tt, "wgrad_gate")),
           slab(_wgrad(du, h2, D_FF // 2, tt, "wgrad_up")), slab(_wgrad(hid, dx3, D_FF // 2, tt, "wgrad_down"))]
    (dqf, dkf, dvf, dcb), got = _attn_bwd(qf, kf, vf, of, dof, lse_f, nb, seq, tq, "fox_bwd", True, _swap_exchange(big))
    sums = [_add_half(g, s) for g, s in zip(big, got)]
    (dqm, dkm, dvm), recv = _attn_bwd(qm, km, vm, om, dom, lse_m, nb, seq, tq, "mla_bwd", False, _scatter_exchange(sums))
    halves = [_sum_slabs(g, s, r) for g, s, r in zip(big, got, recv)]
    dx, dproj, g_q12, g_kv, st_in = _in_bwd(dqf, dkf, dvf, dcb, dqm, dkm, dvm, lat, qn, kvn, xf, dx2, g1, gq, gkv, bfg, ct, st,
                                            a["w_in"], a["w_q12"], a["w_kv"], seq, tm)
    g_in, results = _wgrad(dproj, h1, C_END // 2, tt, "wgrad_in", _both(_join_exchange(halves), _everyone_exchange(st_mid + st_in)))
    gshard = dict(zip(FFN4, results[:4]))
    stats = _sum_devices(results[4])

    gwin_t, gwuq_t, gwukv = _unarrange(g_in, g_q12, g_kv)
    tail = [slab(gwin_t), slab(gwuq_t), gwukv.reshape(KV_RANK, N_CHIPS, -1).transpose(1, 0, 2)]
    tail_got = _run_exchange(_swap_exchange(tail), "tail_swap")
    tail_sums = [_add_half(g, s) for g, s in zip(tail, tail_got)]
    tail_recv = _run_exchange(_scatter_exchange(tail_sums), "tail_scatter")
    tail_joined = _run_exchange(_join_exchange([_sum_slabs(g, s, r) for g, s, r in zip(tail, tail_got, tail_recv)]), "tail_join")
    gshard.update(zip(HEAD3, tail_joined))
    quad = lambda n: (shard[n], gshard[n], _work(n, mom[n]), _work(n, var[n]))
    updates = dict(zip(FFN4[1:], _adamw([quad(n) for n in FFN4[1:]], "adamw_ffn")))
    for n in HEAD3 + FFN4[:1]:
        updates[n], = _adamw([quad(n)], "adamw_" + n)

    grads, delta, new_m, new_v = {}, {}, {}, {}
    for n in HEAD3 + FFN4:
        grads[n] = _back(n, gshard[n])
        delta[n], new_m[n], new_v[n] = [_back(n, t) for t in updates[n]]
    row = lambda t: t.reshape(1, -1)
    small = _adamw_small(stats, {n: (row(wts[n]), row(mom[n]), row(var[n])) for n in SMALL})
    for n in SMALL:
        grads[n], delta[n], new_m[n], new_v[n] = [t.reshape(wts[n].shape) for t in small[n]]
    loss = jnp.sum(stats[ROW_LOSS])
    return (loss, dx.reshape(x.shape), *[grads[n] for n in names], *[delta[n] for n in names],
            *[new_m[n] for n in names], *[new_v[n] for n in names])
```

```python
import functools

import numpy as np
import jax
import jax.numpy as jnp
from jax import lax
from jax.experimental import pallas as pl
from jax.experimental.pallas import tpu as pltpu

F32 = jnp.float32
BF16 = jnp.bfloat16
MESH = pl.DeviceIdType.MESH

EPS = 1e-6
D_MODEL = 1024
HEADS = 8
PAIRS = HEADS // 2
FOX_W = 512
Q_RANK = 256
KV_RANK = 128
ROPE = 32
D_FF = 2816
N_CHIPS = 4
FOX_SCALE = 64 ** -0.5
MLA_SCALE = 96 ** -0.5
LANES = 128
NEG = -1e30

ADAM_LR, ADAM_B1, ADAM_B2, ADAM_EPS, ADAM_WD, ADAM_STEP = 0.001, 0.9, 0.999, 1e-08, 0.01, 10

C_FQ, C_FK, C_FV, C_QL, C_KVL, C_MA, C_END = 0, 512, 1024, 1536, 1792, 1920, 2048
C_MB = C_END

VMEM_LIMIT = 60 * 1024 * 1024
ROW_TILE = 256
IN_PROJ_TILE = 512
ATTN_TILE = 512
ATTN_FWD_TILE = 1024
WGRAD_TILE = 2048

HEAD3 = ("w_in", "w_uq", "w_ukv")
FFN4 = ("w_o", "w_gate", "w_up", "w_down")
TRANSPOSED = ("w_in", "w_uq", "w_gate", "w_up")
SMALL = ("norm_mix_g", "b_fgate", "q_norm_g", "kv_norm_g", "fox_out_g", "mla_out_g", "norm_ffn_g", "final_norm_g")
ROW_NORM_MIX, ROW_NORM_FFN, ROW_FINAL, ROW_OUT, ROW_Q, ROW_KV, ROW_B, ROW_LOSS = range(8)
SMALL_AT = {"norm_mix_g": (ROW_NORM_MIX, 0, 1024), "norm_ffn_g": (ROW_NORM_FFN, 0, 1024), "final_norm_g": (ROW_FINAL, 0, 1024),
            "fox_out_g": (ROW_OUT, 0, 512), "mla_out_g": (ROW_OUT, 512, 512), "q_norm_g": (ROW_Q, 0, 256),
            "kv_norm_g": (ROW_KV, 0, 128), "b_fgate": (ROW_B, 0, 8)}


def _params(sem=None):
    return pltpu.CompilerParams(dimension_semantics=sem, vmem_limit_bytes=VMEM_LIMIT)


def _full(shape):
    n = len(shape)
    return pl.BlockSpec(shape, lambda *_: (0,) * n, pipeline_mode=pl.Buffered(1))


def _dot(a, b):
    return jnp.dot(a, b, preferred_element_type=F32)


def _dot_nt(a, b):
    return lax.dot_general(a, b, (((1,), (1,)), ((), ())), preferred_element_type=F32)


def _dot_tn(a, b):
    return lax.dot_general(a, b, (((0,), (0,)), ((), ())), preferred_element_type=F32)


def _split3(v):
    hi = v.astype(BF16)
    r1 = v - hi.astype(F32)
    mid = r1.astype(BF16)
    lo = (r1 - mid.astype(F32)).astype(BF16)
    return hi, mid, lo


def _rms(v, width):
    return lax.rsqrt(jnp.sum(v * v, axis=1, keepdims=True) * (1.0 / width) + EPS)


def _rms_bwd(dy, xhat, r, g, width):
    u = dy * g
    return r * (u - xhat * (jnp.sum(u * xhat, axis=1, keepdims=True) * (1.0 / width)))


ANY = pl.BlockSpec(memory_space=pl.ANY)


def _place():
    return lax.axis_index("x"), lax.axis_index("y"), lax.axis_index("c")


def _other_chips(x, y):
    return [(1 - x, y), (x, 1 - y), (1 - x, 1 - y)]


def _remote(src, dst, send, recv, j, dev):
    return pltpu.make_async_remote_copy(src_ref=src, dst_ref=dst, send_sem=send.at[j], recv_sem=recv.at[j], device_id=dev, device_id_type=MESH)


class _Exchange:
    def __init__(self, ins, outs, n_remote, n_local, build, in_place=False):
        self.ins, self.outs, self.n_remote, self.n_local, self.build = list(ins), list(outs), n_remote, max(n_local, 1), build
        self.in_place = in_place
        self.n_aliased = len(self.ins)

    def aliases(self, first_in, first_out):
        return {first_in + i: first_out + i for i in range(self.n_aliased)} if self.in_place else {}

    def sems(self):
        return [pltpu.SemaphoreType.DMA((self.n_remote,)), pltpu.SemaphoreType.DMA((self.n_remote,)), pltpu.SemaphoreType.DMA((self.n_local,))]

    def start(self, in_refs, out_refs, sems):
        for cp in self.build(in_refs, out_refs, *sems)[0]:
            cp.start()

    def wait(self, in_refs, out_refs, sems):
        for w in self.build(in_refs, out_refs, *sems)[1]:
            w()


def _gather_exchange(shards):
    def build(ins, outs, send, recv, lsem):
        x, y, c = _place()
        starts, waits = [], []
        for i, (s, o) in enumerate(zip(ins, outs)):
            mine = pltpu.make_async_copy(s, o.at[2 * x + y], lsem.at[i])
            starts.append(mine)
            waits.append(mine.wait)
            for j, (cx, cy) in enumerate(_other_chips(x, y)):
                out = _remote(s, o.at[2 * x + y], send, recv, 3 * i + j, (cx, cy, c))
                starts.append(out)
                waits.append(_remote(s, o.at[2 * cx + cy], send, recv, 3 * i + j, (cx, cy, c)).wait_recv)
                waits.append(out.wait_send)
        return starts, waits

    outs = [jax.ShapeDtypeStruct((N_CHIPS,) + s.shape, s.dtype) for s in shards]
    return _Exchange(shards, outs, 3 * len(shards), len(shards), build)


def _gather_split_exchange(shards):
    n = len(shards)

    def build(ins, outs, send, recv, lsem):
        x, y, c = _place()
        starts, waits, last = [], [], []
        for i, (s, o) in enumerate(zip(ins, outs)):
            hc = s.shape[1] // 2
            mine, other = pl.ds(c * hc, hc), pl.ds((1 - c) * hc, hc)
            for j, (cx, cy) in enumerate(_other_chips(x, y)):
                out = _remote(s.at[:, mine], o.at[2 * x + y, :, mine], send, recv, 3 * i + j, (cx, cy, c))
                landed = o.at[2 * cx + cy, :, mine]
                arrive = _remote(s.at[:, mine], landed, send, recv, 3 * i + j, (cx, cy, c))
                onward = _remote(landed, landed, send, recv, 3 * n + 3 * i + j, (x, y, 1 - c))
                from_sibling = _remote(landed, o.at[2 * cx + cy, :, other], send, recv, 3 * n + 3 * i + j, (x, y, 1 - c))
                starts.append(out)
                waits.append(lambda arrive=arrive, onward=onward: (arrive.wait_recv(), onward.start()))
                last += [from_sibling.wait_recv, onward.wait_send, out.wait_send]
        return starts, waits + last

    outs = [jax.ShapeDtypeStruct((N_CHIPS,) + s.shape, s.dtype) for s in shards]
    return _Exchange(shards, outs, 6 * n, 0, build)


def _swap_exchange(grads):
    def build(ins, outs, send, recv, lsem):
        x, y, c = _place()
        cps = []
        for i, (g, o) in enumerate(zip(ins, outs)):
            hc = g.shape[2] // 2
            cps.append(_remote(g.at[:, :, pl.ds((1 - c) * hc, hc)], o, send, recv, i, (x, y, 1 - c)))
        return cps, [cp.wait for cp in cps]

    outs = [jax.ShapeDtypeStruct((g.shape[0], g.shape[1], g.shape[2] // 2), g.dtype) for g in grads]
    return _Exchange(grads, outs, len(grads), 0, build)


def _scatter_exchange(sums):
    def build(ins, outs, send, recv, lsem):
        x, y, c = _place()
        cps = []
        for i, (s, o) in enumerate(zip(ins, outs)):
            for j, (cx, cy) in enumerate(_other_chips(x, y)):
                cps.append(_remote(s.at[2 * cx + cy], o.at[j], send, recv, 3 * i + j, (cx, cy, c)))
        return cps, [cp.wait for cp in cps]

    outs = [jax.ShapeDtypeStruct((3,) + s.shape[1:], s.dtype) for s in sums]
    return _Exchange(sums, outs, 3 * len(sums), 0, build)


def _join_exchange(bufs):
    def build(ins, outs, send, recv, lsem):
        x, y, c = _place()
        starts, waits = [], []
        for i, (t, o) in enumerate(zip(ins, outs)):
            hc = t.shape[1] // 2
            out = _remote(t.at[:, pl.ds(c * hc, hc)], o.at[:, pl.ds(c * hc, hc)], send, recv, i, (x, y, 1 - c))
            starts.append(out)
            waits += [_remote(t.at[:, pl.ds(c * hc, hc)], o.at[:, pl.ds((1 - c) * hc, hc)], send, recv, i, (x, y, 1 - c)).wait_recv,
                      out.wait_send]
        return starts, waits

    outs = [jax.ShapeDtypeStruct(t.shape, t.dtype) for t in bufs]
    return _Exchange(bufs, outs, len(bufs), 0, build, in_place=True)


def _everyone_exchange(v):
    def build(ins, outs, send, recv, lsem):
        x, y, c = _place()
        me = 4 * x + 2 * y + c
        mine = pltpu.make_async_copy(ins[0], outs[0].at[me], lsem.at[0])
        starts, waits = [mine], [mine.wait]
        for j in range(7):
            fx, fy, fc = (j + 1) >> 2 & 1, (j + 1) >> 1 & 1, (j + 1) & 1
            peer = (x ^ fx, y ^ fy, c ^ fc)
            out = _remote(ins[0], outs[0].at[me], send, recv, j, peer)
            starts.append(out)
            waits += [_remote(ins[0], outs[0].at[4 * peer[0] + 2 * peer[1] + peer[2]], send, recv, j, peer).wait_recv, out.wait_send]
        return starts, waits

    return _Exchange([v], [jax.ShapeDtypeStruct((8,) + v.shape, v.dtype)], 7, 1, build)


def _both(a, b):
    na_in, na_out = len(a.ins), len(a.outs)

    def build(ins, outs, send, recv, lsem):
        sa, wa = a.build(ins[:na_in], outs[:na_out], send.at[pl.ds(0, a.n_remote)], recv.at[pl.ds(0, a.n_remote)],
                         lsem.at[pl.ds(0, a.n_local)])
        sb, wb = b.build(ins[na_in:], outs[na_out:], send.at[pl.ds(a.n_remote, b.n_remote)], recv.at[pl.ds(a.n_remote, b.n_remote)],
                         lsem.at[pl.ds(a.n_local, b.n_local)])
        return sa + sb, wa + wb

    both = _Exchange(a.ins + b.ins, a.outs + b.outs, a.n_remote + b.n_remote, a.n_local + b.n_local, build, in_place=a.in_place)
    both.n_aliased = na_in
    return both


def _run_exchange(ex, name):
    n_in, n_out = len(ex.ins), len(ex.outs)

    def body(*refs):
        ins, outs, sems = refs[:n_in], refs[n_in:n_in + n_out], refs[n_in + n_out:]
        ex.start(ins, outs, sems)
        ex.wait(ins, outs, sems)

    return pl.pallas_call(
        body, name=name, out_shape=tuple(ex.outs), in_specs=[ANY] * n_in, out_specs=tuple([ANY] * n_out),
        scratch_shapes=ex.sems(), input_output_aliases=ex.aliases(0, 0),
        compiler_params=pltpu.CompilerParams(has_side_effects=True),
    )(*ex.ins)


def _sum_devices(rows):
    def body(r_ref, o_ref):
        acc = r_ref[0]
        for d in range(1, 8):
            acc = acc + r_ref[d]
        o_ref[...] = acc

    vm = pl.BlockSpec(memory_space=pltpu.VMEM)
    return pl.pallas_call(body, name="sum_devices", out_shape=jax.ShapeDtypeStruct(rows.shape[1:], rows.dtype),
                          in_specs=[vm], out_specs=vm)(rows)


def _add_half(gs, gots):
    n = len(gs)

    def body(c_ref, *refs):
        for t in range(n):
            refs[2 * n + t][...] = (refs[2 * t][...] + refs[2 * t + 1][...]).astype(BF16)

    blk = lambda g: (1, g.shape[1], g.shape[2] // 2)
    in_specs = [s for g in gs for s in (pl.BlockSpec(blk(g), lambda k, c_ref: (k, 0, c_ref[0])),
                                        pl.BlockSpec(blk(g), lambda k, c_ref: (k, 0, 0)))]
    res = pl.pallas_call(
        body, name="add_half",
        grid_spec=pltpu.PrefetchScalarGridSpec(
            num_scalar_prefetch=1, grid=(N_CHIPS,), in_specs=in_specs,
            out_specs=[pl.BlockSpec(blk(g), lambda k, c_ref: (k, 0, 0)) for g in gs]),
        out_shape=[jax.ShapeDtypeStruct((N_CHIPS,) + blk(g)[1:], BF16) for g in gs],
        compiler_params=_params(("arbitrary",)),
    )(jnp.reshape(lax.axis_index("c"), (1,)).astype(jnp.int32), *[a for pair in zip(gs, gots) for a in pair])
    return list(res)


def _sum_slabs(gs, gots, recvs):
    n = len(gs)

    def body(kc_ref, *refs):
        for t in range(n):
            g_ref, s_ref, r_ref = refs[3 * t:3 * t + 3]
            refs[3 * n + t][...] = (((g_ref[0] + s_ref[0]) + r_ref[0].astype(F32)) + r_ref[1].astype(F32)) + r_ref[2].astype(F32)

    half = lambda g: (g.shape[1], g.shape[2] // 2)
    in_specs = [s for g in gs for s in (pl.BlockSpec((1,) + half(g), lambda i, kc_ref: (kc_ref[0], 0, kc_ref[1])),
                                        pl.BlockSpec((1,) + half(g), lambda i, kc_ref: (kc_ref[0], 0, 0)),
                                        pl.BlockSpec((3,) + half(g), lambda i, kc_ref: (0, 0, 0)))]
    kc = jnp.stack([2 * lax.axis_index("x") + lax.axis_index("y"), lax.axis_index("c")]).astype(jnp.int32)
    res = pl.pallas_call(
        body, name="sum_slabs",
        grid_spec=pltpu.PrefetchScalarGridSpec(
            num_scalar_prefetch=1, grid=(1,), in_specs=in_specs,
            out_specs=[pl.BlockSpec(half(g), lambda i, kc_ref: (0, kc_ref[1])) for g in gs]),
        out_shape=[jax.ShapeDtypeStruct(g.shape[1:], F32) for g in gs],
        compiler_params=_params(("arbitrary",)),
    )(kc, *[a for trio in zip(gs, gots, recvs) for a in trio])
    return list(res)


def _row_tile(rows):
    for cand in (256, 184, 176, 144, 128, 64, 32, 16, 8):
        if rows % cand == 0:
            return cand
    return rows


def _in_proj(x, g1, w_in, w_q12, w_k, w_v, gq, gkv, bfg, ct, st, sel, seq, tm):
    T = x.shape[0]
    nsb = seq // tm

    def body(x_ref, g1_ref, win_ref, wq_ref, wk_ref, wv_ref, gq_ref, gkv_ref, b_ref, ct_ref, st_ref, sel_ref,
             h1_ref, qf_ref, kf_ref, vf_ref, qm_ref, km_ref, vm_ref, lat_ref, qn_ref, kvn_ref, carry):
        i = pl.program_id(0)

        @pl.when(i % nsb == 0)
        def _():
            carry[...] = jnp.zeros_like(carry)

        xv = x_ref[...]
        h = (xv * _rms(xv, D_MODEL) * g1_ref[...]).astype(BF16)
        h1_ref[...] = h
        proj = _dot_nt(h, win_ref[...])
        lane = lax.broadcasted_iota(jnp.int32, (tm, LANES), 1)
        low = lane < 64
        misc_a = proj[:, C_MA:C_END]
        misc_b = pltpu.roll(misc_a, 96, 1)

        z = misc_a + b_ref[...]
        lf = jnp.where(lane < HEADS, jnp.minimum(z, 0.0) - jnp.log1p(jnp.exp(-jnp.abs(z))), 0.0)
        rr = lax.broadcasted_iota(jnp.int32, (tm, tm), 0)
        cc = lax.broadcasted_iota(jnp.int32, (tm, tm), 1)
        tri = (rr >= cc).astype(BF16)
        a0, a1, a2 = _split3(lf)
        c = _dot(tri, a0) + _dot(tri, a1) + _dot(tri, a2) + carry[0:1, :]
        carry[0:1, :] = c[tm - 1:tm, :]
        c0, c1, c2 = _split3(c)
        cpl = _dot(jnp.concatenate([c0, c1, c2], axis=1), sel_ref[...])
        qpad = jnp.where((lane >= 64) & (lane < 67), -1.0, 0.0)
        for j in range(PAIRS):
            qc = proj[:, C_FQ + LANES * j:C_FQ + LANES * (j + 1)] * FOX_SCALE
            kc = proj[:, C_FK + LANES * j:C_FK + LANES * (j + 1)]
            e, o = 2 * LANES * j, 2 * LANES * j + LANES
            qf_ref[:, e:e + LANES] = jnp.where(low, qc, qpad).astype(BF16)
            qf_ref[:, o:o + LANES] = jnp.where(low, pltpu.roll(qc, 64, 1), qpad).astype(BF16)
            kf_ref[:, e:e + LANES] = jnp.where(low, kc, cpl[:, e:e + LANES]).astype(BF16)
            kf_ref[:, o:o + LANES] = jnp.where(low, pltpu.roll(kc, 64, 1), cpl[:, o:o + LANES]).astype(BF16)
        vf_ref[...] = proj[:, C_FV:C_QL].astype(BF16)

        ql = proj[:, C_QL:C_KVL]
        kvl = proj[:, C_KVL:C_MA]
        qn = (ql * _rms(ql, Q_RANK) * gq_ref[...]).astype(BF16)
        kvn = (kvl * _rms(kvl, KV_RANK) * gkv_ref[...]).astype(BF16)
        lat_ref[...] = proj[:, C_QL:C_MB]
        qn_ref[...] = qn
        kvn_ref[...] = kvn
        q12 = _dot_nt(qn, wq_ref[...])
        kn = _dot(kvn, wk_ref[...])
        ctv = ct_ref[...]
        stv = st_ref[...]
        cq = (jnp.where(low, 1.0, 0.0) + ctv) * MLA_SCALE
        sq = stv * MLA_SCALE
        kpe = misc_a * ctv + misc_b * stv
        for hd in range(HEADS):
            s0 = LANES * hd
            qm_ref[:, s0:s0 + LANES] = (q12[:, s0:s0 + LANES] * cq + q12[:, 1024 + s0:1024 + s0 + LANES] * sq).astype(BF16)
            km_ref[:, s0:s0 + LANES] = (kn[:, s0:s0 + LANES] + kpe).astype(BF16)
        vm_ref[...] = _dot(kvn, wv_ref[...]).astype(BF16)

    row = lambda w: pl.BlockSpec((tm, w), lambda i: (i, 0))
    out_shape = (
        jax.ShapeDtypeStruct((T, D_MODEL), BF16),
        jax.ShapeDtypeStruct((T, 1024), BF16), jax.ShapeDtypeStruct((T, 1024), BF16), jax.ShapeDtypeStruct((T, 512), BF16),
        jax.ShapeDtypeStruct((T, 1024), BF16), jax.ShapeDtypeStruct((T, 1024), BF16), jax.ShapeDtypeStruct((T, 512), BF16),
        jax.ShapeDtypeStruct((T, 512), F32),
        jax.ShapeDtypeStruct((T, Q_RANK), BF16), jax.ShapeDtypeStruct((T, KV_RANK), BF16),
    )
    return pl.pallas_call(
        body, name="in_proj", grid=(T // tm,), out_shape=out_shape,
        in_specs=[row(D_MODEL), _full(g1.shape), _full(w_in.shape), _full(w_q12.shape), _full(w_k.shape), _full(w_v.shape),
                  _full(gq.shape), _full(gkv.shape), _full(bfg.shape), row(LANES), row(LANES), _full(sel.shape)],
        out_specs=[row(D_MODEL), row(1024), row(1024), row(512), row(1024), row(1024), row(512), row(512), row(Q_RANK), row(KV_RANK)],
        scratch_shapes=[pltpu.VMEM((8, LANES), F32)],
        compiler_params=_params(("arbitrary",)),
    )(x, g1, w_in, w_q12, w_k, w_v, gq, gkv, bfg, ct, st, sel)


def _attn_fwd(q, k, v, nb, seq, tq, name, ex=None):
    T = q.shape[0]
    nq = seq // tq
    n_in, n_out = (len(ex.ins), len(ex.outs)) if ex else (0, 0)

    def body(*refs):
        q_ref, k_ref, v_ref = refs[0:3]
        o_ref, lse_ref = refs[3 + n_in:5 + n_in]
        b, pr, qi = pl.program_id(0), pl.program_id(1), pl.program_id(2)
        if ex:
            ex_refs = (refs[3:3 + n_in], refs[5 + n_in:5 + n_in + n_out], refs[8 + n_in + n_out:])

            @pl.when((b == 0) & (pr == 0) & (qi == 0))
            def _():
                ex.start(*ex_refs)

        s_sc, p_sc, acc_sc = refs[5 + n_in + n_out:8 + n_in + n_out]
        strip = 64
        key_s = lax.broadcasted_iota(jnp.int32, (strip, tq), 0)
        qry_s = lax.broadcasted_iota(jnp.int32, (strip, tq), 1)
        row_t = lax.broadcasted_iota(jnp.int32, (LANES, tq), 0)
        acc_sc[...] = jnp.zeros(acc_sc.shape, F32)

        def fold(x, op):
            out = x[0:8]
            for r in range(8, strip, 8):
                out = op(out, x[r:r + 8])
            return out

        def step(kj, state, masked):
            rows = pl.ds(pl.multiple_of(kj * tq, tq), tq)
            for hh in range(2):
                s_sc[hh] = _dot_nt(k_ref[rows, LANES * hh:LANES * (hh + 1)], q_ref[:, LANES * hh:LANES * (hh + 1)])
            vv = v_ref[rows, :]
            new = []
            for hh in range(2):
                m, l = state[hh]

                def strip_of(r0, hh=hh):
                    s = s_sc[hh, r0:r0 + strip, :]
                    return jnp.where(key_s + r0 <= qry_s, s, NEG) if masked else s

                mx = fold(strip_of(0), jnp.maximum)
                for r0 in range(strip, tq, strip):
                    mx = jnp.maximum(mx, fold(strip_of(r0), jnp.maximum))
                m_new = jnp.maximum(m, jnp.max(mx, axis=0, keepdims=True))
                alpha = jnp.exp(m - m_new)
                sm = jnp.zeros((8, tq), F32)
                for r0 in range(0, tq, strip):
                    p = jnp.exp(strip_of(r0) - m_new)
                    sm = sm + fold(p, jnp.add)
                    p_sc[hh, r0:r0 + strip, :] = p.astype(BF16)
                l = alpha * l + jnp.sum(sm, axis=0, keepdims=True)
                acc_sc[hh] = alpha * acc_sc[hh] + _dot_tn(vv, p_sc[hh])
                new.append((m_new, l))
            return tuple(new)

        one = (jnp.full((1, tq), NEG, F32), jnp.zeros((1, tq), F32))
        state = lax.fori_loop(0, qi, functools.partial(step, masked=False), (one, one))
        (m0, l0), (m1, l1) = step(qi, state, True)
        o_ref[...] = jnp.where(row_t < 64, acc_sc[0] / l0, acc_sc[1] / l1).T
        lse_ref[:, 0:LANES] = jnp.broadcast_to(m0 + jnp.log(l0), (LANES, tq)).T
        lse_ref[:, LANES:2 * LANES] = jnp.broadcast_to(m1 + jnp.log(l1), (LANES, tq)).T

        if ex:
            @pl.when((b == nb - 1) & (pr == PAIRS - 1) & (qi == nq - 1))
            def _():
                ex.wait(*ex_refs)

    res = pl.pallas_call(
        body, name=name, grid=(nb, PAIRS, nq),
        out_shape=(jax.ShapeDtypeStruct((T, 512), F32), jax.ShapeDtypeStruct((T, 1024), F32)) + tuple(ex.outs if ex else ()),
        in_specs=[pl.BlockSpec((tq, 2 * LANES), lambda b, p, i: (b * nq + i, p)),
                  pl.BlockSpec((seq, 2 * LANES), lambda b, p, i: (b, p)),
                  pl.BlockSpec((seq, LANES), lambda b, p, i: (b, p))] + [ANY] * n_in,
        out_specs=[pl.BlockSpec((tq, LANES), lambda b, p, i: (b * nq + i, p)),
                   pl.BlockSpec((tq, 2 * LANES), lambda b, p, i: (b * nq + i, p))] + [ANY] * n_out,
        scratch_shapes=[pltpu.VMEM((2, tq, tq), F32), pltpu.VMEM((2, tq, tq), BF16), pltpu.VMEM((2, LANES, tq), F32)]
        + (ex.sems() if ex else []),
        compiler_params=_params(("arbitrary", "arbitrary", "arbitrary")),
    )(q, k, v, *(ex.ins if ex else ()))
    return res[0], res[1], list(res[2:])


def _attn_bwd(q, k, v, o, do, lse, nb, seq, tq, name, key_bias, ex=None):
    T = q.shape[0]
    nq = seq // tq
    n_in, n_out = (len(ex.ins), len(ex.outs)) if ex else (0, 0)
    n_res = 4 if key_bias else 3

    def body(*refs):
        q_ref, k_ref, v_ref, o_ref, do_ref, lse_ref = refs[0:6]
        dq_ref, dk_ref, dv_ref = refs[6 + n_in:9 + n_in]
        dcb_ref = refs[9 + n_in] if key_bias else None
        first_scratch = 6 + n_in + n_res + n_out
        dsc, rsum, dq_acc = refs[first_scratch:first_scratch + 3]
        b, pr, step_no = pl.program_id(0), pl.program_id(1), pl.program_id(2)
        kj = nq - 1 - step_no
        if ex:
            ex_refs = (refs[6:6 + n_in], refs[6 + n_in + n_res:6 + n_in + n_res + n_out], refs[first_scratch + 3:])

            @pl.when((b == 0) & (pr == 0) & (step_no == 0))
            def _():
                ex.start(*ex_refs)

        lane_s = lax.broadcasted_iota(jnp.int32, (seq, LANES), 1)
        lane = lax.broadcasted_iota(jnp.int32, (tq, LANES), 1)
        rr = lax.broadcasted_iota(jnp.int32, (tq, tq), 0)
        cc = lax.broadcasted_iota(jnp.int32, (tq, tq), 1)

        @pl.when(step_no == 0)
        def _():
            dq_acc[...] = jnp.zeros_like(dq_acc)
            prod = do_ref[...].astype(F32) * o_ref[...]
            d0 = jnp.sum(jnp.where(lane_s < 64, prod, 0.0), axis=1, keepdims=True)
            d1 = jnp.sum(jnp.where(lane_s < 64, 0.0, prod), axis=1, keepdims=True)
            dsc[0] = jnp.broadcast_to(d0, (seq, LANES))
            dsc[1] = jnp.broadcast_to(d1, (seq, LANES))
            if key_bias:
                rsum[...] = jnp.zeros_like(rsum)

        if key_bias:
            @pl.when((pr == 0) & (step_no == 0))
            def _():
                dcb_ref[...] = jnp.zeros_like(dcb_ref)

        vv = v_ref[...]

        def step(qi, carry, masked):
            dkt, dvt, cols = carry
            rows = pl.ds(pl.multiple_of(qi * tq, tq), tq)
            dov = do_ref[rows, :]
            new_dkt, new_cols = [], []
            for hh in range(2):
                qv = q_ref[rows, LANES * hh:LANES * (hh + 1)]
                kv = k_ref[:, LANES * hh:LANES * (hh + 1)]
                dom = jnp.where((lane < 64) if hh == 0 else (lane >= 64), dov, jnp.zeros((), BF16))
                s = _dot_nt(qv, kv)
                if masked:
                    s = jnp.where(cc <= rr, s, NEG)
                p = jnp.exp(s - jnp.tile(lse_ref[rows, LANES * hh:LANES * (hh + 1)], (1, tq // LANES)))
                dp = _dot_nt(dom, vv)
                ds32 = p * (dp - jnp.tile(dsc[hh, rows, :], (1, tq // LANES)))
                col = cols[hh]
                if key_bias:
                    col = col + jnp.sum(ds32, axis=0, keepdims=True)
                    rsum[hh, rows, :] += jnp.broadcast_to(jnp.sum(ds32, axis=1, keepdims=True), (tq, LANES))
                ds = ds32.astype(BF16)
                dvt = dvt + _dot_tn(dom, p.astype(BF16))
                new_dkt.append(dkt[hh] + _dot_tn(qv, ds))
                new_cols.append(col)
                dq_acc[rows, LANES * hh:LANES * (hh + 1)] += _dot(ds, kv)
            return tuple(new_dkt), dvt, tuple(new_cols)

        zt = jnp.zeros((LANES, tq), F32)
        zc = jnp.zeros((1, tq), F32)
        carry = step(kj, ((zt, zt), zt, (zc, zc)), True)
        dkt, dvt, cols = lax.fori_loop(kj + 1, nq, functools.partial(step, masked=False), carry)
        for hh in range(2):
            dk_ref[:, LANES * hh:LANES * (hh + 1)] = dkt[hh].T.astype(dk_ref.dtype)
        dv_ref[...] = dvt.T.astype(dv_ref.dtype)
        if key_bias:
            row_t = lax.broadcasted_iota(jnp.int32, (LANES, tq), 0)
            per_key = jnp.where(row_t == 2 * pr, -cols[0], 0.0) + jnp.where(row_t == 2 * pr + 1, -cols[1], 0.0)
            dcb_ref[pl.ds(pl.multiple_of(kj * tq, tq), tq), :] += per_key.T

        @pl.when(step_no == nq - 1)
        def _():
            dq_ref[...] = dq_acc[...].astype(dq_ref.dtype)
            if key_bias:
                dcb_ref[...] += jnp.where(lane_s == 2 * pr, rsum[0], 0.0) + jnp.where(lane_s == 2 * pr + 1, rsum[1], 0.0)

        if ex:
            @pl.when((b == nb - 1) & (pr == PAIRS - 1) & (step_no == nq - 1))
            def _():
                ex.wait(*ex_refs)

    per_seq = lambda w: pl.BlockSpec((seq, w), lambda b, p, j: (b, p))
    per_blk = lambda w: pl.BlockSpec((tq, w), lambda b, p, j: (b * nq + nq - 1 - j, p))
    res = pl.pallas_call(
        body, name=name, grid=(nb, PAIRS, nq),
        out_shape=(jax.ShapeDtypeStruct((T, 1024), BF16), jax.ShapeDtypeStruct((T, 1024), BF16), jax.ShapeDtypeStruct((T, 512), BF16))
        + ((jax.ShapeDtypeStruct((T, LANES), F32),) if key_bias else ()) + tuple(ex.outs if ex else ()),
        in_specs=[per_seq(2 * LANES), per_blk(2 * LANES), per_blk(LANES), per_seq(LANES), per_seq(LANES), per_seq(2 * LANES)] + [ANY] * n_in,
        out_specs=[per_seq(2 * LANES), per_blk(2 * LANES), per_blk(LANES)]
        + ([pl.BlockSpec((seq, LANES), lambda b, p, j: (b, 0))] if key_bias else []) + [ANY] * n_out,
        scratch_shapes=[pltpu.VMEM((2, seq, LANES), F32), pltpu.VMEM((2, seq, LANES) if key_bias else (2, 8, LANES), F32),
                        pltpu.VMEM((seq, 2 * LANES), F32)]
        + (ex.sems() if ex else []),
        compiler_params=_params(("arbitrary", "arbitrary", "arbitrary")),
    )(q, k, v, o, do, lse, *(ex.ins if ex else ()))
    return list(res[:n_res]), list(res[n_res:])


def _mid(of, om, x, tgt, g_fo, g_mo, g2, g3, w_o, w_g, w_u, w_d, tm):
    T = x.shape[0]

    def body(of_ref, om_ref, x_ref, t_ref, gfo_ref, gmo_ref, g2_ref, g3_ref, wo_ref, wg_ref, wu_ref, wd_ref,
             a_ref, h2_ref, hid_ref, dg_ref, du_ref, dx3_ref, dx2_ref, dof_ref, dom_ref, st_ref):
        i = pl.program_id(0)

        @pl.when(i == 0)
        def _():
            st_ref[...] = jnp.zeros_like(st_ref)

        ofv, omv = of_ref[...], om_ref[...]
        rf, rm = _rms(ofv, FOX_W), _rms(omv, FOX_W)
        fhat, mhat = ofv * rf, omv * rm
        a = jnp.concatenate([fhat * gfo_ref[...], mhat * gmo_ref[...]], axis=1).astype(BF16)
        a_ref[...] = a
        x2 = x_ref[...] + _dot(a, wo_ref[...])
        r2 = _rms(x2, D_MODEL)
        xh2 = x2 * r2
        h2 = (xh2 * g2_ref[...]).astype(BF16)
        h2_ref[...] = h2
        gt = _dot_nt(h2, wg_ref[...])
        up = _dot_nt(h2, wu_ref[...])
        sg = jax.nn.sigmoid(gt)
        sl = gt * sg
        hid = (sl * up).astype(BF16)
        hid_ref[...] = hid
        x3 = x2 + _dot(hid, wd_ref[...])
        r3 = _rms(x3, D_MODEL)
        xh3 = x3 * r3
        diff = xh3 * g3_ref[...] - t_ref[...]
        dy = diff * (1.0 / D_MODEL)
        st_ref[ROW_LOSS:ROW_LOSS + 1, :] += jnp.sum(diff * diff, axis=0, keepdims=True) * (0.5 / D_MODEL)
        st_ref[ROW_FINAL:ROW_FINAL + 1, :] += jnp.sum(dy * xh3, axis=0, keepdims=True)
        dx3 = _rms_bwd(dy, xh3, r3, g3_ref[...], D_MODEL)
        dx3b = dx3.astype(BF16)
        dx3_ref[...] = dx3b
        dhid = _dot_nt(dx3b, wd_ref[...])
        dg = (dhid * up * (sg * (1.0 + gt * (1.0 - sg)))).astype(BF16)
        du = (dhid * sl).astype(BF16)
        dg_ref[...] = dg
        du_ref[...] = du
        dh2 = _dot(dg, wg_ref[...]) + _dot(du, wu_ref[...])
        st_ref[ROW_NORM_FFN:ROW_NORM_FFN + 1, :] += jnp.sum(dh2 * xh2, axis=0, keepdims=True)
        dx2 = dx3 + _rms_bwd(dh2, xh2, r2, g2_ref[...], D_MODEL)
        dx2_ref[...] = dx2
        da = _dot_nt(dx2.astype(BF16), wo_ref[...])
        daf, dam = da[:, 0:FOX_W], da[:, FOX_W:2 * FOX_W]
        st_ref[ROW_OUT:ROW_OUT + 1, 0:FOX_W] += jnp.sum(daf * fhat, axis=0, keepdims=True)
        st_ref[ROW_OUT:ROW_OUT + 1, FOX_W:2 * FOX_W] += jnp.sum(dam * mhat, axis=0, keepdims=True)
        dof_ref[...] = _rms_bwd(daf, fhat, rf, gfo_ref[...], FOX_W).astype(BF16)
        dom_ref[...] = _rms_bwd(dam, mhat, rm, gmo_ref[...], FOX_W).astype(BF16)

    row = lambda w: pl.BlockSpec((tm, w), lambda i: (i, 0))
    ff = jax.ShapeDtypeStruct((T, D_FF), BF16)
    out_shape = (
        jax.ShapeDtypeStruct((T, 1024), BF16), jax.ShapeDtypeStruct((T, 1024), BF16), ff, ff, ff,
        jax.ShapeDtypeStruct((T, 1024), BF16), jax.ShapeDtypeStruct((T, 1024), F32),
        jax.ShapeDtypeStruct((T, 512), BF16), jax.ShapeDtypeStruct((T, 512), BF16), jax.ShapeDtypeStruct((8, 1024), F32),
    )
    return pl.pallas_call(
        body, name="mid", grid=(T // tm,), out_shape=out_shape,
        in_specs=[row(512), row(512), row(1024), row(1024), _full(g_fo.shape), _full(g_mo.shape), _full(g2.shape), _full(g3.shape),
                  _full(w_o.shape), _full(w_g.shape), _full(w_u.shape), _full(w_d.shape)],
        out_specs=[row(1024), row(1024), row(D_FF), row(D_FF), row(D_FF), row(1024), row(1024), row(512), row(512),
                   pl.BlockSpec((8, 1024), lambda i: (0, 0))],
        compiler_params=_params(("arbitrary",)),
    )(of, om, x, tgt, g_fo, g_mo, g2, g3, w_o, w_g, w_u, w_d)


def _in_bwd(dqf, dkf, dvf, dcb, dqm, dkm, dvm, lat, qn, kvn, x, dx2, g1, gq, gkv, bfg, ct, st, w_in, w_q12, w_kv, seq, tm):
    T = x.shape[0]
    nblk = T // tm
    nsb = seq // tm

    def body(dqf_ref, dkf_ref, dvf_ref, dcb_ref, dqm_ref, dkm_ref, dvm_ref, lat_ref, qn_ref, kvn_ref, x_ref, dx2_ref, g1_ref, gq_ref,
             gkv_ref, b_ref, ct_ref, st_ref, win_ref, wq_ref, wkv_ref, dx_ref, dproj_ref, gq12_ref, gkv12_ref, stat_ref, carry,
             dq12_ref, dkv_ref):
        i = pl.program_id(0)

        @pl.when(i == 0)
        def _():
            stat_ref[...] = jnp.zeros_like(stat_ref)
            gq12_ref[...] = jnp.zeros_like(gq12_ref)
            gkv12_ref[...] = jnp.zeros_like(gkv12_ref)

        @pl.when(i % nsb == 0)
        def _():
            carry[...] = jnp.zeros_like(carry)

        lane = lax.broadcasted_iota(jnp.int32, (tm, LANES), 1)
        low = lane < 64
        ctv, stv = ct_ref[...], st_ref[...]

        for j in range(PAIRS):
            e, o = 2 * LANES * j, 2 * LANES * j + LANES
            half = lambda ref, c0: jnp.where(low, ref[:, c0:c0 + LANES].astype(F32), 0.0)
            dq = half(dqf_ref, e) + pltpu.roll(half(dqf_ref, o), 64, 1)
            dk = half(dkf_ref, e) + pltpu.roll(half(dkf_ref, o), 64, 1)
            dproj_ref[:, C_FQ + LANES * j:C_FQ + LANES * (j + 1)] = (dq * FOX_SCALE).astype(BF16)
            dproj_ref[:, C_FK + LANES * j:C_FK + LANES * (j + 1)] = dk.astype(BF16)
        dproj_ref[:, C_FV:C_QL] = dvf_ref[...]
        dc = dcb_ref[...]
        rr = lax.broadcasted_iota(jnp.int32, (tm, tm), 0)
        cc = lax.broadcasted_iota(jnp.int32, (tm, tm), 1)
        triu = (cc >= rr).astype(BF16)
        a0, a1, a2 = _split3(dc)
        dlf = _dot(triu, a0) + _dot(triu, a1) + _dot(triu, a2) + carry[0:1, :]
        carry[0:1, :] = dlf[0:1, :]
        misc_a = lat_ref[:, Q_RANK + KV_RANK:Q_RANK + KV_RANK + LANES]
        z = misc_a + b_ref[...]
        dz = jnp.where(lane < HEADS, dlf * jax.nn.sigmoid(-z), 0.0)
        stat_ref[ROW_B:ROW_B + 1, 0:LANES] += jnp.sum(dz, axis=0, keepdims=True)

        cq = (jnp.where(low, 1.0, 0.0) + ctv) * MLA_SCALE
        sq = stv * MLA_SCALE
        dkpe = jnp.zeros((tm, LANES), F32)
        for hd in range(HEADS):
            s0 = LANES * hd
            dqh = dqm_ref[:, s0:s0 + LANES].astype(F32)
            dq12_ref[:, s0:s0 + LANES] = (dqh * cq).astype(BF16)
            dq12_ref[:, 1024 + s0:1024 + s0 + LANES] = (dqh * sq).astype(BF16)
            dkpe = dkpe + dkm_ref[:, s0:s0 + LANES].astype(F32)
        dkv_ref[:, 0:1024] = dkm_ref[...]
        dkv_ref[:, 1024:1536] = dvm_ref[...]
        dproj_ref[:, C_MA:C_END] = (dz + dkpe * ctv + pltpu.roll(dkpe * stv, 32, 1)).astype(BF16)
        dqn = _dot(dq12_ref[...], wq_ref[...])
        dkvn = _dot_nt(dkv_ref[...], wkv_ref[...])
        gq12_ref[...] += _dot_tn(dq12_ref[...], qn_ref[...])
        gkv12_ref[...] += _dot_tn(kvn_ref[...], dkv_ref[...])
        ql = lat_ref[:, 0:Q_RANK]
        kvl = lat_ref[:, Q_RANK:Q_RANK + KV_RANK]
        rq, rkv = _rms(ql, Q_RANK), _rms(kvl, KV_RANK)
        qhat, kvhat = ql * rq, kvl * rkv
        stat_ref[ROW_Q:ROW_Q + 1, 0:Q_RANK] += jnp.sum(dqn * qhat, axis=0, keepdims=True)
        stat_ref[ROW_KV:ROW_KV + 1, 0:KV_RANK] += jnp.sum(dkvn * kvhat, axis=0, keepdims=True)
        dproj_ref[:, C_QL:C_KVL] = _rms_bwd(dqn, qhat, rq, gq_ref[...], Q_RANK).astype(BF16)
        dproj_ref[:, C_KVL:C_MA] = _rms_bwd(dkvn, kvhat, rkv, gkv_ref[...], KV_RANK).astype(BF16)

        dh1 = _dot(dproj_ref[...], win_ref[...])
        xv = x_ref[...]
        r1 = _rms(xv, D_MODEL)
        xh = xv * r1
        stat_ref[ROW_NORM_MIX:ROW_NORM_MIX + 1, :] += jnp.sum(dh1 * xh, axis=0, keepdims=True)
        dx_ref[...] = dx2_ref[...] + _rms_bwd(dh1, xh, r1, g1_ref[...], D_MODEL)

    rev = lambda w: pl.BlockSpec((tm, w), lambda i: (nblk - 1 - i, 0))
    whole = lambda r, c: pl.BlockSpec((r, c), lambda i: (0, 0))
    out_shape = (
        jax.ShapeDtypeStruct((T, 1024), F32), jax.ShapeDtypeStruct((T, C_END), BF16), jax.ShapeDtypeStruct((2048, Q_RANK), F32),
        jax.ShapeDtypeStruct((KV_RANK, 1536), F32), jax.ShapeDtypeStruct((8, 1024), F32),
    )
    return pl.pallas_call(
        body, name="in_bwd", grid=(nblk,), out_shape=out_shape,
        in_specs=[rev(1024), rev(1024), rev(512), rev(LANES), rev(1024), rev(1024), rev(512), rev(512), rev(Q_RANK), rev(KV_RANK),
                  rev(1024), rev(1024), _full(g1.shape), _full(gq.shape), _full(gkv.shape), _full(bfg.shape), rev(LANES), rev(LANES),
                  _full(w_in.shape), _full(w_q12.shape), _full(w_kv.shape)],
        out_specs=[rev(1024), rev(C_END), whole(2048, Q_RANK), whole(KV_RANK, 1536), whole(8, 1024)],
        scratch_shapes=[pltpu.VMEM((8, LANES), F32), pltpu.VMEM((tm, 2048), BF16), pltpu.VMEM((tm, 1536), BF16)],
        compiler_params=_params(("arbitrary",)),
    )(dqf, dkf, dvf, dcb, dqm, dkm, dvm, lat, qn, kvn, x, dx2, g1, gq, gkv, bfg, ct, st, w_in, w_q12, w_kv)


def _wgrad(a, b, tk, tt, name, ex=None):
    T, K = a.shape
    N = b.shape[1]
    n_in, n_out = (len(ex.ins), len(ex.outs)) if ex else (0, 0)
    gk, gt = K // tk, T // tt

    def body(*refs):
        a_ref, b_ref, o_ref = refs[0], refs[1], refs[2 + n_in]
        kb, t = pl.program_id(0), pl.program_id(1)
        if ex:
            ex_refs = (refs[2:2 + n_in], refs[3 + n_in:3 + n_in + n_out], refs[3 + n_in + n_out:])

            @pl.when((kb == 0) & (t == 0))
            def _():
                ex.start(*ex_refs)

        @pl.when(t == 0)
        def _():
            o_ref[...] = jnp.zeros_like(o_ref)

        o_ref[...] += _dot_tn(a_ref[...].astype(BF16), b_ref[...].astype(BF16))

        if ex:
            @pl.when((kb == gk - 1) & (t == gt - 1))
            def _():
                ex.wait(*ex_refs)

    res = pl.pallas_call(
        body, name=name, grid=(gk, gt), out_shape=(jax.ShapeDtypeStruct((K, N), F32),) + tuple(ex.outs if ex else ()),
        in_specs=[pl.BlockSpec((tt, tk), lambda kb, t: (t, kb)), pl.BlockSpec((tt, N), lambda kb, t: (t, 0))] + [ANY] * n_in,
        out_specs=[pl.BlockSpec((tk, N), lambda kb, t: (kb, 0))] + [ANY] * n_out,
        scratch_shapes=ex.sems() if ex else [], input_output_aliases=ex.aliases(2, 1) if ex else {},
        compiler_params=_params(("arbitrary", "arbitrary")),
    )(a, b, *(ex.ins if ex else ()))
    return (res[0], list(res[1:])) if ex else res[0]


def _adam_update(w, g, m, v):
    nm = ADAM_B1 * m + (1.0 - ADAM_B1) * g
    nv = ADAM_B2 * v + (1.0 - ADAM_B2) * (g * g)
    m_hat = nm / (1.0 - ADAM_B1 ** ADAM_STEP)
    v_hat = nv / (1.0 - ADAM_B2 ** ADAM_STEP)
    return -ADAM_LR * (m_hat / (jnp.sqrt(v_hat) + ADAM_EPS) + ADAM_WD * w), nm, nv


def _adamw_small(stats, params):
    k = len(SMALL)

    def body(*refs):
        for t, name in enumerate(SMALL):
            row, c0, width = SMALL_AT[name]
            w_ref, m_ref, v_ref = refs[1 + 3 * t:4 + 3 * t]
            g_ref, d_ref, nm_ref, nv_ref = refs[1 + 3 * k + 4 * t:5 + 3 * k + 4 * t]
            g = refs[0][row:row + 1, c0:c0 + width]
            g_ref[...] = g
            d_ref[...], nm_ref[...], nv_ref[...] = _adam_update(w_ref[...], g, m_ref[...], v_ref[...])

    vm = pl.BlockSpec(memory_space=pltpu.VMEM)
    out_shape = tuple(jax.ShapeDtypeStruct((1, SMALL_AT[name][2]), F32) for name in SMALL for _ in range(4))
    res = pl.pallas_call(body, name="adamw_small", out_shape=out_shape, in_specs=[vm] * (1 + 3 * k), out_specs=tuple([vm] * (4 * k)))(
        stats, *[a for name in SMALL for a in params[name]])
    return {name: tuple(res[4 * t:4 * t + 4]) for t, name in enumerate(SMALL)}


def _adamw(tensors, name):
    n = len(tensors)
    R, C = tensors[0][0].shape
    tr = _row_tile(R)

    def body(*refs):
        for t in range(n):
            w_ref, g_ref, m_ref, v_ref = refs[4 * t:4 * t + 4]
            d_ref, nm_ref, nv_ref = refs[4 * n + 3 * t:4 * n + 3 * t + 3]
            d_ref[...], nm_ref[...], nv_ref[...] = _adam_update(w_ref[...], g_ref[...], m_ref[...], v_ref[...])

    blk = pl.BlockSpec((tr, C), lambda i: (i, 0))
    sh = jax.ShapeDtypeStruct((R, C), F32)
    res = pl.pallas_call(
        body, name=name, grid=(R // tr,), out_shape=(sh,) * (3 * n),
        in_specs=[blk] * (4 * n), out_specs=[blk] * (3 * n),
        compiler_params=_params(("arbitrary",)),
    )(*[a for t in tensors for a in t])
    return [tuple(res[3 * t:3 * t + 3]) for t in range(n)]


def _arrange(win_t, wuq_t, wukv):
    dt = win_t.dtype
    z = lambda r: jnp.zeros((r, D_MODEL), dt)
    zh = lambda r: jnp.zeros((HEADS, r, Q_RANK), dt)
    kr1, kr2 = win_t[1928:1944], win_t[1944:1960]
    misc = jnp.concatenate([win_t[1536:1544], z(56), kr1, kr2, kr2, kr1], axis=0)
    w_in = jnp.concatenate([win_t[0:1536], win_t[1544:1928], misc], axis=0)
    wq = wuq_t.reshape(HEADS, 96, Q_RANK)
    q1 = jnp.concatenate([wq, zh(32)], axis=1).reshape(1024, Q_RANK)
    q2 = jnp.concatenate([zh(64), wq[:, 80:96], wq[:, 64:80], zh(32)], axis=1).reshape(1024, Q_RANK)
    wkv = wukv.reshape(KV_RANK, HEADS, 128)
    wk = jnp.concatenate([wkv[:, :, 0:64], jnp.zeros((KV_RANK, HEADS, 64), dt)], axis=2).reshape(KV_RANK, 1024)
    wv = wkv[:, :, 64:128].reshape(KV_RANK, 512)
    return dict(w_in=w_in, w_q12=jnp.concatenate([q1, q2], axis=0), w_k=wk, w_v=wv, w_kv=jnp.concatenate([wk, wv], axis=1))


def _unarrange(g_in, g_q12, g_kv):
    kr1 = g_in[C_MA + 64:C_MA + 80] + g_in[C_MA + 112:C_MA + 128]
    kr2 = g_in[C_MA + 80:C_MA + 96] + g_in[C_MA + 96:C_MA + 112]
    win_t = jnp.concatenate([g_in[0:1536], g_in[C_MA:C_MA + 8], g_in[1536:1920], kr1, kr2], axis=0)
    g1 = g_q12[0:1024].reshape(HEADS, 128, Q_RANK)
    g2 = g_q12[1024:2048].reshape(HEADS, 128, Q_RANK)
    wuq_t = jnp.concatenate([g1[:, 0:64], g1[:, 64:80] + g2[:, 80:96], g1[:, 80:96] + g2[:, 64:80]], axis=1).reshape(768, Q_RANK)
    gk = g_kv[:, 0:1024].reshape(KV_RANK, HEADS, 128)
    gv = g_kv[:, 1024:1536].reshape(KV_RANK, HEADS, 64)
    wukv = jnp.concatenate([gk[:, :, 0:64], gv], axis=2).reshape(KV_RANK, 1024)
    return win_t, wuq_t, wukv


def _selector():
    sel = np.zeros((384, 1024), np.float32)
    for h in range(HEADS):
        for piece in range(3):
            sel[LANES * piece + h, LANES * h + 64 + piece] = 1.0
    return jnp.asarray(sel, BF16)


def _rope_tables(positions):
    inv_freq = 10000.0 ** (-jnp.arange(0, ROPE, 2, dtype=F32) / ROPE)
    n = positions.size
    ang = (positions.reshape(n // 8, 8, 1).astype(F32) * inv_freq[None, None, :]).reshape(n // 8, 8 * (ROPE // 2))
    cos, sin = lax.optimization_barrier((jnp.cos(lax.optimization_barrier(ang)), jnp.sin(lax.optimization_barrier(ang))))
    cos, sin = cos.reshape(n, ROPE // 2), sin.reshape(n, ROPE // 2)
    z64, z32 = jnp.zeros((n, 64), F32), jnp.zeros((n, 32), F32)
    return jnp.concatenate([z64, cos, cos, z32], axis=1), jnp.concatenate([z64, -sin, sin, z32], axis=1)


def _work(name, t):
    return jnp.swapaxes(t[0], 0, 1) if name in TRANSPOSED else t[0]


def _back(name, t):
    return (jnp.swapaxes(t, 0, 1) if name in TRANSPOSED else t)[None]


def kernel(x, positions, norm_mix_g, w_in, b_fgate, q_norm_g, w_uq, kv_norm_g, w_ukv, fox_out_g, mla_out_g, w_o, norm_ffn_g, w_gate, w_up, w_down, final_norm_g, loss_target, m_norm_mix_g, m_w_in, m_b_fgate, m_q_norm_g, m_w_uq, m_kv_norm_g, m_w_ukv, m_fox_out_g, m_mla_out_g, m_w_o, m_norm_ffn_g, m_w_gate, m_w_up, m_w_down, m_final_norm_g, v_norm_mix_g, v_w_in, v_b_fgate, v_q_norm_g, v_w_uq, v_kv_norm_g, v_w_ukv, v_fox_out_g, v_mla_out_g, v_w_o, v_norm_ffn_g, v_w_gate, v_w_up, v_w_down, v_final_norm_g):
    names = ["norm_mix_g", "w_in", "b_fgate", "q_norm_g", "w_uq", "kv_norm_g", "w_ukv", "fox_out_g", "mla_out_g", "w_o",
             "norm_ffn_g", "w_gate", "w_up", "w_down", "final_norm_g"]
    wts = dict(zip(names, [norm_mix_g, w_in, b_fgate, q_norm_g, w_uq, kv_norm_g, w_ukv, fox_out_g, mla_out_g, w_o, norm_ffn_g,
                           w_gate, w_up, w_down, final_norm_g]))
    mom = dict(zip(names, [m_norm_mix_g, m_w_in, m_b_fgate, m_q_norm_g, m_w_uq, m_kv_norm_g, m_w_ukv, m_fox_out_g, m_mla_out_g,
                           m_w_o, m_norm_ffn_g, m_w_gate, m_w_up, m_w_down, m_final_norm_g]))
    var = dict(zip(names, [v_norm_mix_g, v_w_in, v_b_fgate, v_q_norm_g, v_w_uq, v_kv_norm_g, v_w_ukv, v_fox_out_g, v_mla_out_g,
                           v_w_o, v_norm_ffn_g, v_w_gate, v_w_up, v_w_down, v_final_norm_g]))
    shard = {n: _work(n, wts[n]) for n in HEAD3 + FFN4}
    nb, seq, _ = x.shape
    T = nb * seq
    tm, tq = min(ROW_TILE, seq), min(ATTN_TILE, seq)
    tt = min(WGRAD_TILE, T)
    xf = x.reshape(T, D_MODEL)
    tgt = loss_target.reshape(T, D_MODEL)
    chip = 2 * lax.axis_index("x") + lax.axis_index("y")

    mine = [shard[n].astype(BF16) for n in HEAD3]
    head = _run_exchange(_gather_split_exchange(mine), "gather_head")
    win4, wuq4, wukv4 = [lax.dynamic_update_slice(h, s[None], (chip, 0, 0)) for h, s in zip(head, mine)]
    a = _arrange(win4.reshape(-1, D_MODEL), wuq4.reshape(-1, Q_RANK), wukv4.transpose(1, 0, 2).reshape(KV_RANK, -1))
    sel = _selector()
    ct, st = _rope_tables(positions)
    bfg = jnp.concatenate([b_fgate, jnp.zeros((1, LANES - HEADS), F32)], axis=1)
    g1, gq, gkv = norm_mix_g, q_norm_g, kv_norm_g

    h1, qf, kf, vf, qm, km, vm, lat, qn, kvn = _in_proj(xf, g1, a["w_in"], a["w_q12"], a["w_k"], a["w_v"], gq, gkv, bfg, ct, st, sel, seq,
                                                        min(IN_PROJ_TILE, seq))
    tqf = min(ATTN_FWD_TILE, seq)
    of, lse_f, (wo4, wg4) = _attn_fwd(qf, kf, vf, nb, seq, tqf, "fox_fwd", _gather_exchange([shard[n].astype(BF16) for n in FFN4[:2]]))
    om, lse_m, (wu4, wd4) = _attn_fwd(qm, km, vm, nb, seq, tqf, "mla_fwd", _gather_exchange([shard[n].astype(BF16) for n in FFN4[2:]]))
    a_cat, h2, hid, dg, du, dx3, dx2, dof, dom, st_mid = _mid(
        of, om, xf, tgt, fox_out_g, mla_out_g, norm_ffn_g, final_norm_g.reshape(1, D_MODEL),
        wo4.reshape(D_MODEL, D_MODEL), wg4.reshape(D_FF, D_MODEL), wu4.reshape(D_FF, D_MODEL), wd4.reshape(D_FF, D_MODEL), tm)

    slab = lambda g: g.reshape(N_CHIPS, g.shape[0] // N_CHIPS, g.shape[1])
    big = [slab(_wgrad(a_cat, dx2, D_MODEL, tt, "wgrad_o")), slab(_wgrad(dg, h2, D_FF // 2, tt, "wgrad_gate")),
           slab(_wgrad(du, h2, D_FF // 2, tt, "wgrad_up")), slab(_wgrad(hid, dx3, D_FF // 2, tt, "wgrad_down"))]
    (dqf, dkf, dvf, dcb), got = _attn_bwd(qf, kf, vf, of, dof, lse_f, nb, seq, tq, "fox_bwd", True, _swap_exchange(big))
    sums = _add_half(big, got)
    (dqm, dkm, dvm), recv = _attn_bwd(qm, km, vm, om, dom, lse_m, nb, seq, tq, "mla_bwd", False, _scatter_exchange(sums))
    halves = _sum_slabs(big, got, recv)
    dx, dproj, g_q12, g_kv, st_in = _in_bwd(dqf, dkf, dvf, dcb, dqm, dkm, dvm, lat, qn, kvn, xf, dx2, g1, gq, gkv, bfg, ct, st,
                                            a["w_in"], a["w_q12"], a["w_kv"], seq, tm)
    g_in, results = _wgrad(dproj, h1, C_END, tt, "wgrad_in", _both(_join_exchange(halves), _everyone_exchange(st_mid + st_in)))
    gshard = dict(zip(FFN4, results[:4]))
    stats = _sum_devices(results[4])

    gwin_t, gwuq_t, gwukv = _unarrange(g_in, g_q12, g_kv)
    tail = [slab(gwin_t), slab(gwuq_t), gwukv.reshape(KV_RANK, N_CHIPS, -1).transpose(1, 0, 2)]
    tail_got = _run_exchange(_swap_exchange(tail), "tail_swap")
    tail_sums = _add_half(tail, tail_got)
    tail_recv = _run_exchange(_scatter_exchange(tail_sums), "tail_scatter")
    tail_joined = _run_exchange(_join_exchange(_sum_slabs(tail, tail_got, tail_recv)), "tail_join")
    gshard.update(zip(HEAD3, tail_joined))
    quad = lambda n: (shard[n], gshard[n], _work(n, mom[n]), _work(n, var[n]))
    updates = dict(zip(FFN4[1:], _adamw([quad(n) for n in FFN4[1:]], "adamw_ffn")))
    for n in HEAD3 + FFN4[:1]:
        updates[n], = _adamw([quad(n)], "adamw_" + n)

    grads, delta, new_m, new_v = {}, {}, {}, {}
    for n in HEAD3 + FFN4:
        grads[n] = _back(n, gshard[n])
        delta[n], new_m[n], new_v[n] = [_back(n, t) for t in updates[n]]
    row = lambda t: t.reshape(1, -1)
    small = _adamw_small(stats, {n: (row(wts[n]), row(mom[n]), row(var[n])) for n in SMALL})
    for n in SMALL:
        grads[n], delta[n], new_m[n], new_v[n] = [t.reshape(wts[n].shape) for t in small[n]]
    loss = jnp.sum(stats[ROW_LOSS])
    return (loss, dx.reshape(x.shape), *[grads[n] for n in names], *[delta[n] for n in names],
            *[new_m[n] for n in names], *[new_v[n] for n in names])
```

```python
import functools

import numpy as np
import jax
import jax.numpy as jnp
from jax import lax
from jax.experimental import pallas as pl
from jax.experimental.pallas import tpu as pltpu

F32 = jnp.float32
BF16 = jnp.bfloat16
MESH = pl.DeviceIdType.MESH

EPS = 1e-6
D_MODEL = 1024
HEADS = 8
PAIRS = HEADS // 2
FOX_W = 512
Q_RANK = 256
KV_RANK = 128
ROPE = 32
D_FF = 2816
N_CHIPS = 4
FOX_SCALE = 64 ** -0.5
MLA_SCALE = 96 ** -0.5
LANES = 128
NEG = -1e30

ADAM_LR, ADAM_B1, ADAM_B2, ADAM_EPS, ADAM_WD, ADAM_STEP = 0.001, 0.9, 0.999, 1e-08, 0.01, 10

C_FQ, C_FK, C_FV, C_QL, C_KVL, C_MA, C_END = 0, 512, 1024, 1536, 1792, 1920, 2048
C_MB = C_END

VMEM_LIMIT = 60 * 1024 * 1024
ROW_TILE = 256
IN_PROJ_TILE = 512
ATTN_TILE = 512
ATTN_FWD_TILE = 1024
WGRAD_TILE = 2048

HEAD3 = ("w_in", "w_uq", "w_ukv")
FFN4 = ("w_o", "w_gate", "w_up", "w_down")
TRANSPOSED = ("w_in", "w_uq", "w_gate", "w_up")
SMALL = ("norm_mix_g", "b_fgate", "q_norm_g", "kv_norm_g", "fox_out_g", "mla_out_g", "norm_ffn_g", "final_norm_g")
ROW_NORM_MIX, ROW_NORM_FFN, ROW_FINAL, ROW_OUT, ROW_Q, ROW_KV, ROW_B, ROW_LOSS = range(8)
SMALL_AT = {"norm_mix_g": (ROW_NORM_MIX, 0, 1024), "norm_ffn_g": (ROW_NORM_FFN, 0, 1024), "final_norm_g": (ROW_FINAL, 0, 1024),
            "fox_out_g": (ROW_OUT, 0, 512), "mla_out_g": (ROW_OUT, 512, 512), "q_norm_g": (ROW_Q, 0, 256),
            "kv_norm_g": (ROW_KV, 0, 128), "b_fgate": (ROW_B, 0, 8)}


def _params(sem=None):
    return pltpu.CompilerParams(dimension_semantics=sem, vmem_limit_bytes=VMEM_LIMIT)


def _full(shape):
    n = len(shape)
    return pl.BlockSpec(shape, lambda *_: (0,) * n, pipeline_mode=pl.Buffered(1))


def _dot(a, b):
    return jnp.dot(a, b, preferred_element_type=F32)


def _dot_nt(a, b):
    return lax.dot_general(a, b, (((1,), (1,)), ((), ())), preferred_element_type=F32)


def _dot_tn(a, b):
    return lax.dot_general(a, b, (((0,), (0,)), ((), ())), preferred_element_type=F32)


def _split3(v):
    hi = v.astype(BF16)
    r1 = v - hi.astype(F32)
    mid = r1.astype(BF16)
    lo = (r1 - mid.astype(F32)).astype(BF16)
    return hi, mid, lo


def _rms(v, width):
    return lax.rsqrt(jnp.sum(v * v, axis=1, keepdims=True) * (1.0 / width) + EPS)


def _rms_bwd(dy, xhat, r, g, width):
    u = dy * g
    return r * (u - xhat * (jnp.sum(u * xhat, axis=1, keepdims=True) * (1.0 / width)))


ANY = pl.BlockSpec(memory_space=pl.ANY)


def _place():
    return lax.axis_index("x"), lax.axis_index("y"), lax.axis_index("c")


def _other_chips(x, y):
    return [(1 - x, y), (x, 1 - y), (1 - x, 1 - y)]


def _remote(src, dst, send, recv, j, dev):
    return pltpu.make_async_remote_copy(src_ref=src, dst_ref=dst, send_sem=send.at[j], recv_sem=recv.at[j], device_id=dev, device_id_type=MESH)


class _Exchange:
    def __init__(self, ins, outs, n_remote, n_local, build, in_place=False):
        self.ins, self.outs, self.n_remote, self.n_local, self.build = list(ins), list(outs), n_remote, max(n_local, 1), build
        self.in_place = in_place
        self.n_aliased = len(self.ins)

    def aliases(self, first_in, first_out):
        return {first_in + i: first_out + i for i in range(self.n_aliased)} if self.in_place else {}

    def sems(self):
        return [pltpu.SemaphoreType.DMA((self.n_remote,)), pltpu.SemaphoreType.DMA((self.n_remote,)), pltpu.SemaphoreType.DMA((self.n_local,))]

    def start(self, in_refs, out_refs, sems):
        for cp in self.build(in_refs, out_refs, *sems)[0]:
            cp.start()

    def wait(self, in_refs, out_refs, sems):
        for w in self.build(in_refs, out_refs, *sems)[1]:
            w()


def _gather_exchange(shards):
    def build(ins, outs, send, recv, lsem):
        x, y, c = _place()
        starts, waits = [], []
        for i, (s, o) in enumerate(zip(ins, outs)):
            mine = pltpu.make_async_copy(s, o.at[2 * x + y], lsem.at[i])
            starts.append(mine)
            waits.append(mine.wait)
            for j, (cx, cy) in enumerate(_other_chips(x, y)):
                out = _remote(s, o.at[2 * x + y], send, recv, 3 * i + j, (cx, cy, c))
                starts.append(out)
                waits.append(_remote(s, o.at[2 * cx + cy], send, recv, 3 * i + j, (cx, cy, c)).wait_recv)
                waits.append(out.wait_send)
        return starts, waits

    outs = [jax.ShapeDtypeStruct((N_CHIPS,) + s.shape, s.dtype) for s in shards]
    return _Exchange(shards, outs, 3 * len(shards), len(shards), build)


def _gather_split_exchange(shards):
    n = len(shards)

    def build(ins, outs, send, recv, lsem):
        x, y, c = _place()
        starts, waits, last = [], [], []
        for i, (s, o) in enumerate(zip(ins, outs)):
            hc = s.shape[1] // 2
            mine, other = pl.ds(c * hc, hc), pl.ds((1 - c) * hc, hc)
            for j, (cx, cy) in enumerate(_other_chips(x, y)):
                out = _remote(s.at[:, mine], o.at[2 * x + y, :, mine], send, recv, 3 * i + j, (cx, cy, c))
                landed = o.at[2 * cx + cy, :, mine]
                arrive = _remote(s.at[:, mine], landed, send, recv, 3 * i + j, (cx, cy, c))
                onward = _remote(landed, landed, send, recv, 3 * n + 3 * i + j, (x, y, 1 - c))
                from_sibling = _remote(landed, o.at[2 * cx + cy, :, other], send, recv, 3 * n + 3 * i + j, (x, y, 1 - c))
                starts.append(out)
                waits.append(lambda arrive=arrive, onward=onward: (arrive.wait_recv(), onward.start()))
                last += [from_sibling.wait_recv, onward.wait_send, out.wait_send]
        return starts, waits + last

    outs = [jax.ShapeDtypeStruct((N_CHIPS,) + s.shape, s.dtype) for s in shards]
    return _Exchange(shards, outs, 6 * n, 0, build)


def _swap_exchange(grads):
    def build(ins, outs, send, recv, lsem):
        x, y, c = _place()
        cps = []
        for i, (g, o) in enumerate(zip(ins, outs)):
            hc = g.shape[2] // 2
            cps.append(_remote(g.at[:, :, pl.ds((1 - c) * hc, hc)], o, send, recv, i, (x, y, 1 - c)))
        return cps, [cp.wait for cp in cps]

    outs = [jax.ShapeDtypeStruct((g.shape[0], g.shape[1], g.shape[2] // 2), g.dtype) for g in grads]
    return _Exchange(grads, outs, len(grads), 0, build)


def _scatter_exchange(sums):
    def build(ins, outs, send, recv, lsem):
        x, y, c = _place()
        cps = []
        for i, (s, o) in enumerate(zip(ins, outs)):
            for j, (cx, cy) in enumerate(_other_chips(x, y)):
                cps.append(_remote(s.at[2 * cx + cy], o.at[j], send, recv, 3 * i + j, (cx, cy, c)))
        return cps, [cp.wait for cp in cps]

    outs = [jax.ShapeDtypeStruct((3,) + s.shape[1:], s.dtype) for s in sums]
    return _Exchange(sums, outs, 3 * len(sums), 0, build)


def _join_exchange(bufs):
    def build(ins, outs, send, recv, lsem):
        x, y, c = _place()
        starts, waits = [], []
        for i, (t, o) in enumerate(zip(ins, outs)):
            hc = t.shape[1] // 2
            out = _remote(t.at[:, pl.ds(c * hc, hc)], o.at[:, pl.ds(c * hc, hc)], send, recv, i, (x, y, 1 - c))
            starts.append(out)
            waits += [_remote(t.at[:, pl.ds(c * hc, hc)], o.at[:, pl.ds((1 - c) * hc, hc)], send, recv, i, (x, y, 1 - c)).wait_recv,
                      out.wait_send]
        return starts, waits

    outs = [jax.ShapeDtypeStruct(t.shape, t.dtype) for t in bufs]
    return _Exchange(bufs, outs, len(bufs), 0, build, in_place=True)


def _everyone_exchange(v):
    def build(ins, outs, send, recv, lsem):
        x, y, c = _place()
        me = 4 * x + 2 * y + c
        mine = pltpu.make_async_copy(ins[0], outs[0].at[me], lsem.at[0])
        starts, waits = [mine], [mine.wait]
        for j in range(7):
            fx, fy, fc = (j + 1) >> 2 & 1, (j + 1) >> 1 & 1, (j + 1) & 1
            peer = (x ^ fx, y ^ fy, c ^ fc)
            out = _remote(ins[0], outs[0].at[me], send, recv, j, peer)
            starts.append(out)
            waits += [_remote(ins[0], outs[0].at[4 * peer[0] + 2 * peer[1] + peer[2]], send, recv, j, peer).wait_recv, out.wait_send]
        return starts, waits

    return _Exchange([v], [jax.ShapeDtypeStruct((8,) + v.shape, v.dtype)], 7, 1, build)


def _both(a, b):
    na_in, na_out = len(a.ins), len(a.outs)

    def build(ins, outs, send, recv, lsem):
        sa, wa = a.build(ins[:na_in], outs[:na_out], send.at[pl.ds(0, a.n_remote)], recv.at[pl.ds(0, a.n_remote)],
                         lsem.at[pl.ds(0, a.n_local)])
        sb, wb = b.build(ins[na_in:], outs[na_out:], send.at[pl.ds(a.n_remote, b.n_remote)], recv.at[pl.ds(a.n_remote, b.n_remote)],
                         lsem.at[pl.ds(a.n_local, b.n_local)])
        return sa + sb, wa + wb

    both = _Exchange(a.ins + b.ins, a.outs + b.outs, a.n_remote + b.n_remote, a.n_local + b.n_local, build, in_place=a.in_place)
    both.n_aliased = na_in
    return both


def _run_exchange(ex, name):
    n_in, n_out = len(ex.ins), len(ex.outs)

    def body(*refs):
        ins, outs, sems = refs[:n_in], refs[n_in:n_in + n_out], refs[n_in + n_out:]
        ex.start(ins, outs, sems)
        ex.wait(ins, outs, sems)

    return pl.pallas_call(
        body, name=name, out_shape=tuple(ex.outs), in_specs=[ANY] * n_in, out_specs=tuple([ANY] * n_out),
        scratch_shapes=ex.sems(), input_output_aliases=ex.aliases(0, 0),
        compiler_params=pltpu.CompilerParams(has_side_effects=True),
    )(*ex.ins)


def _sum_devices(rows):
    def body(r_ref, o_ref):
        acc = r_ref[0]
        for d in range(1, 8):
            acc = acc + r_ref[d]
        o_ref[...] = acc

    vm = pl.BlockSpec(memory_space=pltpu.VMEM)
    return pl.pallas_call(body, name="sum_devices", out_shape=jax.ShapeDtypeStruct(rows.shape[1:], rows.dtype),
                          in_specs=[vm], out_specs=vm)(rows)


def _add_half(gs, gots):
    n = len(gs)

    def body(c_ref, *refs):
        for t in range(n):
            refs[2 * n + t][...] = (refs[2 * t][...] + refs[2 * t + 1][...]).astype(BF16)

    blk = lambda g: (1, g.shape[1], g.shape[2] // 2)
    in_specs = [s for g in gs for s in (pl.BlockSpec(blk(g), lambda k, c_ref: (k, 0, c_ref[0])),
                                        pl.BlockSpec(blk(g), lambda k, c_ref: (k, 0, 0)))]
    res = pl.pallas_call(
        body, name="add_half",
        grid_spec=pltpu.PrefetchScalarGridSpec(
            num_scalar_prefetch=1, grid=(N_CHIPS,), in_specs=in_specs,
            out_specs=[pl.BlockSpec(blk(g), lambda k, c_ref: (k, 0, 0)) for g in gs]),
        out_shape=[jax.ShapeDtypeStruct((N_CHIPS,) + blk(g)[1:], BF16) for g in gs],
        compiler_params=_params(("arbitrary",)),
    )(jnp.reshape(lax.axis_index("c"), (1,)).astype(jnp.int32), *[a for pair in zip(gs, gots) for a in pair])
    return list(res)


def _sum_slabs(gs, gots, recvs):
    n = len(gs)

    def body(kc_ref, *refs):
        for t in range(n):
            g_ref, s_ref, r_ref = refs[3 * t:3 * t + 3]
            refs[3 * n + t][...] = (((g_ref[0] + s_ref[0]) + r_ref[0].astype(F32)) + r_ref[1].astype(F32)) + r_ref[2].astype(F32)

    half = lambda g: (g.shape[1], g.shape[2] // 2)
    in_specs = [s for g in gs for s in (pl.BlockSpec((1,) + half(g), lambda i, kc_ref: (kc_ref[0], 0, kc_ref[1])),
                                        pl.BlockSpec((1,) + half(g), lambda i, kc_ref: (kc_ref[0], 0, 0)),
                                        pl.BlockSpec((3,) + half(g), lambda i, kc_ref: (0, 0, 0)))]
    kc = jnp.stack([2 * lax.axis_index("x") + lax.axis_index("y"), lax.axis_index("c")]).astype(jnp.int32)
    res = pl.pallas_call(
        body, name="sum_slabs",
        grid_spec=pltpu.PrefetchScalarGridSpec(
            num_scalar_prefetch=1, grid=(1,), in_specs=in_specs,
            out_specs=[pl.BlockSpec(half(g), lambda i, kc_ref: (0, kc_ref[1])) for g in gs]),
        out_shape=[jax.ShapeDtypeStruct(g.shape[1:], F32) for g in gs],
        compiler_params=_params(("arbitrary",)),
    )(kc, *[a for trio in zip(gs, gots, recvs) for a in trio])
    return list(res)


def _row_tile(rows):
    for cand in (256, 184, 176, 144, 128, 64, 32, 16, 8):
        if rows % cand == 0:
            return cand
    return rows


def _in_proj(x, g1, w_in, w_q12, w_k, w_v, gq, gkv, bfg, ct, st, sel, seq, tm):
    T = x.shape[0]
    nsb = seq // tm

    def body(x_ref, g1_ref, win_ref, wq_ref, wk_ref, wv_ref, gq_ref, gkv_ref, b_ref, ct_ref, st_ref, sel_ref,
             h1_ref, qf_ref, kf_ref, vf_ref, qm_ref, km_ref, vm_ref, lat_ref, qn_ref, kvn_ref, carry):
        i = pl.program_id(0)

        @pl.when(i % nsb == 0)
        def _():
            carry[...] = jnp.zeros_like(carry)

        xv = x_ref[...]
        h = (xv * _rms(xv, D_MODEL) * g1_ref[...]).astype(BF16)
        h1_ref[...] = h
        proj = _dot_nt(h, win_ref[...])
        lane = lax.broadcasted_iota(jnp.int32, (tm, LANES), 1)
        low = lane < 64
        misc_a = proj[:, C_MA:C_END]
        misc_b = pltpu.roll(misc_a, 96, 1)

        z = misc_a + b_ref[...]
        lf = jnp.where(lane < HEADS, jnp.minimum(z, 0.0) - jnp.log1p(jnp.exp(-jnp.abs(z))), 0.0)
        rr = lax.broadcasted_iota(jnp.int32, (tm, tm), 0)
        cc = lax.broadcasted_iota(jnp.int32, (tm, tm), 1)
        tri = (rr >= cc).astype(BF16)
        a0, a1, a2 = _split3(lf)
        c = _dot(tri, a0) + _dot(tri, a1) + _dot(tri, a2) + carry[0:1, :]
        carry[0:1, :] = c[tm - 1:tm, :]
        c0, c1, c2 = _split3(c)
        cpl = _dot(jnp.concatenate([c0, c1, c2], axis=1), sel_ref[...])
        qpad = jnp.where((lane >= 64) & (lane < 67), -1.0, 0.0)
        for j in range(PAIRS):
            qc = proj[:, C_FQ + LANES * j:C_FQ + LANES * (j + 1)] * FOX_SCALE
            kc = proj[:, C_FK + LANES * j:C_FK + LANES * (j + 1)]
            e, o = 2 * LANES * j, 2 * LANES * j + LANES
            qf_ref[:, e:e + LANES] = jnp.where(low, qc, qpad).astype(BF16)
            qf_ref[:, o:o + LANES] = jnp.where(low, pltpu.roll(qc, 64, 1), qpad).astype(BF16)
            kf_ref[:, e:e + LANES] = jnp.where(low, kc, cpl[:, e:e + LANES]).astype(BF16)
            kf_ref[:, o:o + LANES] = jnp.where(low, pltpu.roll(kc, 64, 1), cpl[:, o:o + LANES]).astype(BF16)
        vf_ref[...] = proj[:, C_FV:C_QL].astype(BF16)

        ql = proj[:, C_QL:C_KVL]
        kvl = proj[:, C_KVL:C_MA]
        qn = (ql * _rms(ql, Q_RANK) * gq_ref[...]).astype(BF16)
        kvn = (kvl * _rms(kvl, KV_RANK) * gkv_ref[...]).astype(BF16)
        lat_ref[...] = proj[:, C_QL:C_MB]
        qn_ref[...] = qn
        kvn_ref[...] = kvn
        q12 = _dot_nt(qn, wq_ref[...])
        kn = _dot(kvn, wk_ref[...])
        ctv = ct_ref[...]
        stv = st_ref[...]
        cq = (jnp.where(low, 1.0, 0.0) + ctv) * MLA_SCALE
        sq = stv * MLA_SCALE
        kpe = misc_a * ctv + misc_b * stv
        for hd in range(HEADS):
            s0 = LANES * hd
            qm_ref[:, s0:s0 + LANES] = (q12[:, s0:s0 + LANES] * cq + q12[:, 1024 + s0:1024 + s0 + LANES] * sq).astype(BF16)
            km_ref[:, s0:s0 + LANES] = (kn[:, s0:s0 + LANES] + kpe).astype(BF16)
        vm_ref[...] = _dot(kvn, wv_ref[...]).astype(BF16)

    row = lambda w: pl.BlockSpec((tm, w), lambda i: (i, 0))
    out_shape = (
        jax.ShapeDtypeStruct((T, D_MODEL), BF16),
        jax.ShapeDtypeStruct((T, 1024), BF16), jax.ShapeDtypeStruct((T, 1024), BF16), jax.ShapeDtypeStruct((T, 512), BF16),
        jax.ShapeDtypeStruct((T, 1024), BF16), jax.ShapeDtypeStruct((T, 1024), BF16), jax.ShapeDtypeStruct((T, 512), BF16),
        jax.ShapeDtypeStruct((T, 512), F32),
        jax.ShapeDtypeStruct((T, Q_RANK), BF16), jax.ShapeDtypeStruct((T, KV_RANK), BF16),
    )
    return pl.pallas_call(
        body, name="in_proj", grid=(T // tm,), out_shape=out_shape,
        in_specs=[row(D_MODEL), _full(g1.shape), _full(w_in.shape), _full(w_q12.shape), _full(w_k.shape), _full(w_v.shape),
                  _full(gq.shape), _full(gkv.shape), _full(bfg.shape), row(LANES), row(LANES), _full(sel.shape)],
        out_specs=[row(D_MODEL), row(1024), row(1024), row(512), row(1024), row(1024), row(512), row(512), row(Q_RANK), row(KV_RANK)],
        scratch_shapes=[pltpu.VMEM((8, LANES), F32)],
        compiler_params=_params(("arbitrary",)),
    )(x, g1, w_in, w_q12, w_k, w_v, gq, gkv, bfg, ct, st, sel)


def _attn_fwd(q, k, v, nb, seq, tq, name, ex=None):
    T = q.shape[0]
    nq = seq // tq
    n_in, n_out = (len(ex.ins), len(ex.outs)) if ex else (0, 0)

    def body(*refs):
        q_ref, k_ref, v_ref = refs[0:3]
        o_ref, lse_ref = refs[3 + n_in:5 + n_in]
        b, pr, qi = pl.program_id(0), pl.program_id(1), pl.program_id(2)
        if ex:
            ex_refs = (refs[3:3 + n_in], refs[5 + n_in:5 + n_in + n_out], refs[8 + n_in + n_out:])

            @pl.when((b == 0) & (pr == 0) & (qi == 0))
            def _():
                ex.start(*ex_refs)

        s_sc, p_sc, acc_sc = refs[5 + n_in + n_out:8 + n_in + n_out]
        strip = 64
        key_s = lax.broadcasted_iota(jnp.int32, (strip, tq), 0)
        qry_s = lax.broadcasted_iota(jnp.int32, (strip, tq), 1)
        row_t = lax.broadcasted_iota(jnp.int32, (LANES, tq), 0)
        acc_sc[...] = jnp.zeros(acc_sc.shape, F32)

        def fold(x, op):
            out = x[0:8]
            for r in range(8, strip, 8):
                out = op(out, x[r:r + 8])
            return out

        def step(kj, state, masked):
            rows = pl.ds(pl.multiple_of(kj * tq, tq), tq)
            for hh in range(2):
                s_sc[hh] = _dot_nt(k_ref[rows, LANES * hh:LANES * (hh + 1)], q_ref[:, LANES * hh:LANES * (hh + 1)])
            vv = v_ref[rows, :]
            new = []
            for hh in range(2):
                m, l = state[hh]

                def strip_of(r0, hh=hh):
                    s = s_sc[hh, r0:r0 + strip, :]
                    return jnp.where(key_s + r0 <= qry_s, s, NEG) if masked else s

                mx = fold(strip_of(0), jnp.maximum)
                for r0 in range(strip, tq, strip):
                    mx = jnp.maximum(mx, fold(strip_of(r0), jnp.maximum))
                m_new = jnp.maximum(m, jnp.max(mx, axis=0, keepdims=True))
                alpha = jnp.exp(m - m_new)
                sm = jnp.zeros((8, tq), F32)
                for r0 in range(0, tq, strip):
                    p = jnp.exp(strip_of(r0) - m_new)
                    sm = sm + fold(p, jnp.add)
                    p_sc[hh, r0:r0 + strip, :] = p.astype(BF16)
                l = alpha * l + jnp.sum(sm, axis=0, keepdims=True)
                acc_sc[hh] = alpha * acc_sc[hh] + _dot_tn(vv, p_sc[hh])
                new.append((m_new, l))
            return tuple(new)

        one = (jnp.full((1, tq), NEG, F32), jnp.zeros((1, tq), F32))
        state = lax.fori_loop(0, qi, functools.partial(step, masked=False), (one, one))
        (m0, l0), (m1, l1) = step(qi, state, True)
        o_ref[...] = jnp.where(row_t < 64, acc_sc[0] / l0, acc_sc[1] / l1).T
        lse_ref[:, 0:LANES] = jnp.broadcast_to(m0 + jnp.log(l0), (LANES, tq)).T
        lse_ref[:, LANES:2 * LANES] = jnp.broadcast_to(m1 + jnp.log(l1), (LANES, tq)).T

        if ex:
            @pl.when((b == nb - 1) & (pr == PAIRS - 1) & (qi == nq - 1))
            def _():
                ex.wait(*ex_refs)

    res = pl.pallas_call(
        body, name=name, grid=(nb, PAIRS, nq),
        out_shape=(jax.ShapeDtypeStruct((T, 512), F32), jax.ShapeDtypeStruct((T, 1024), F32)) + tuple(ex.outs if ex else ()),
        in_specs=[pl.BlockSpec((tq, 2 * LANES), lambda b, p, i: (b * nq + i, p)),
                  pl.BlockSpec((seq, 2 * LANES), lambda b, p, i: (b, p)),
                  pl.BlockSpec((seq, LANES), lambda b, p, i: (b, p))] + [ANY] * n_in,
        out_specs=[pl.BlockSpec((tq, LANES), lambda b, p, i: (b * nq + i, p)),
                   pl.BlockSpec((tq, 2 * LANES), lambda b, p, i: (b * nq + i, p))] + [ANY] * n_out,
        scratch_shapes=[pltpu.VMEM((2, tq, tq), F32), pltpu.VMEM((2, tq, tq), BF16), pltpu.VMEM((2, LANES, tq), F32)]
        + (ex.sems() if ex else []),
        compiler_params=_params(("arbitrary", "arbitrary", "arbitrary")),
    )(q, k, v, *(ex.ins if ex else ()))
    return res[0], res[1], list(res[2:])


def _attn_bwd(q, k, v, o, do, lse, nb, seq, tq, name, key_bias, ex=None):
    T = q.shape[0]
    nq = seq // tq
    n_in, n_out = (len(ex.ins), len(ex.outs)) if ex else (0, 0)
    n_res = 4 if key_bias else 3

    def body(*refs):
        q_ref, k_ref, v_ref, o_ref, do_ref, lse_ref = refs[0:6]
        dq_ref, dk_ref, dv_ref = refs[6 + n_in:9 + n_in]
        dcb_ref = refs[9 + n_in] if key_bias else None
        first_scratch = 6 + n_in + n_res + n_out
        dsc, rsum, dq_acc = refs[first_scratch:first_scratch + 3]
        b, pr, step_no = pl.program_id(0), pl.program_id(1), pl.program_id(2)
        kj = nq - 1 - step_no
        if ex:
            ex_refs = (refs[6:6 + n_in], refs[6 + n_in + n_res:6 + n_in + n_res + n_out], refs[first_scratch + 3:])

            @pl.when((b == 0) & (pr == 0) & (step_no == 0))
            def _():
                ex.start(*ex_refs)

        lane_s = lax.broadcasted_iota(jnp.int32, (seq, LANES), 1)
        lane = lax.broadcasted_iota(jnp.int32, (tq, LANES), 1)
        rr = lax.broadcasted_iota(jnp.int32, (tq, tq), 0)
        cc = lax.broadcasted_iota(jnp.int32, (tq, tq), 1)

        @pl.when(step_no == 0)
        def _():
            dq_acc[...] = jnp.zeros_like(dq_acc)
            prod = do_ref[...].astype(F32) * o_ref[...]
            d0 = jnp.sum(jnp.where(lane_s < 64, prod, 0.0), axis=1, keepdims=True)
            d1 = jnp.sum(jnp.where(lane_s < 64, 0.0, prod), axis=1, keepdims=True)
            dsc[0] = jnp.broadcast_to(d0, (seq, LANES))
            dsc[1] = jnp.broadcast_to(d1, (seq, LANES))
            if key_bias:
                rsum[...] = jnp.zeros_like(rsum)

        if key_bias:
            @pl.when((pr == 0) & (step_no == 0))
            def _():
                dcb_ref[...] = jnp.zeros_like(dcb_ref)

        vv = v_ref[...]

        def step(qi, carry, masked):
            dkt, dvt, cols = carry
            rows = pl.ds(pl.multiple_of(qi * tq, tq), tq)
            dov = do_ref[rows, :]
            new_dkt, new_cols = [], []
            for hh in range(2):
                qv = q_ref[rows, LANES * hh:LANES * (hh + 1)]
                kv = k_ref[:, LANES * hh:LANES * (hh + 1)]
                dom = jnp.where((lane < 64) if hh == 0 else (lane >= 64), dov, jnp.zeros((), BF16))
                s = _dot_nt(qv, kv)
                if masked:
                    s = jnp.where(cc <= rr, s, NEG)
                p = jnp.exp(s - jnp.tile(lse_ref[rows, LANES * hh:LANES * (hh + 1)], (1, tq // LANES)))
                dp = _dot_nt(dom, vv)
                ds32 = p * (dp - jnp.tile(dsc[hh, rows, :], (1, tq // LANES)))
                col = cols[hh]
                if key_bias:
                    col = col + jnp.sum(ds32, axis=0, keepdims=True)
                    rsum[hh, rows, :] += jnp.broadcast_to(jnp.sum(ds32, axis=1, keepdims=True), (tq, LANES))
                ds = ds32.astype(BF16)
                dvt = dvt + _dot_tn(dom, p.astype(BF16))
                new_dkt.append(dkt[hh] + _dot_tn(qv, ds))
                new_cols.append(col)
                dq_acc[rows, LANES * hh:LANES * (hh + 1)] += _dot(ds, kv)
            return tuple(new_dkt), dvt, tuple(new_cols)

        zt = jnp.zeros((LANES, tq), F32)
        zc = jnp.zeros((1, tq), F32)
        carry = step(kj, ((zt, zt), zt, (zc, zc)), True)
        dkt, dvt, cols = lax.fori_loop(kj + 1, nq, functools.partial(step, masked=False), carry)
        for hh in range(2):
            dk_ref[:, LANES * hh:LANES * (hh + 1)] = dkt[hh].T.astype(dk_ref.dtype)
        dv_ref[...] = dvt.T.astype(dv_ref.dtype)
        if key_bias:
            row_t = lax.broadcasted_iota(jnp.int32, (LANES, tq), 0)
            per_key = jnp.where(row_t == 2 * pr, -cols[0], 0.0) + jnp.where(row_t == 2 * pr + 1, -cols[1], 0.0)
            dcb_ref[pl.ds(pl.multiple_of(kj * tq, tq), tq), :] += per_key.T

        @pl.when(step_no == nq - 1)
        def _():
            dq_ref[...] = dq_acc[...].astype(dq_ref.dtype)
            if key_bias:
                dcb_ref[...] += jnp.where(lane_s == 2 * pr, rsum[0], 0.0) + jnp.where(lane_s == 2 * pr + 1, rsum[1], 0.0)

        if ex:
            @pl.when((b == nb - 1) & (pr == PAIRS - 1) & (step_no == nq - 1))
            def _():
                ex.wait(*ex_refs)

    per_seq = lambda w: pl.BlockSpec((seq, w), lambda b, p, j: (b, p))
    per_blk = lambda w: pl.BlockSpec((tq, w), lambda b, p, j: (b * nq + nq - 1 - j, p))
    res = pl.pallas_call(
        body, name=name, grid=(nb, PAIRS, nq),
        out_shape=(jax.ShapeDtypeStruct((T, 1024), BF16), jax.ShapeDtypeStruct((T, 1024), BF16), jax.ShapeDtypeStruct((T, 512), BF16))
        + ((jax.ShapeDtypeStruct((T, LANES), F32),) if key_bias else ()) + tuple(ex.outs if ex else ()),
        in_specs=[per_seq(2 * LANES), per_blk(2 * LANES), per_blk(LANES), per_seq(LANES), per_seq(LANES), per_seq(2 * LANES)] + [ANY] * n_in,
        out_specs=[per_seq(2 * LANES), per_blk(2 * LANES), per_blk(LANES)]
        + ([pl.BlockSpec((seq, LANES), lambda b, p, j: (b, 0))] if key_bias else []) + [ANY] * n_out,
        scratch_shapes=[pltpu.VMEM((2, seq, LANES), F32), pltpu.VMEM((2, seq, LANES) if key_bias else (2, 8, LANES), F32),
                        pltpu.VMEM((seq, 2 * LANES), F32)]
        + (ex.sems() if ex else []),
        compiler_params=_params(("arbitrary", "arbitrary", "arbitrary")),
    )(q, k, v, o, do, lse, *(ex.ins if ex else ()))
    return list(res[:n_res]), list(res[n_res:])


def _mid(of, om, x, tgt, g_fo, g_mo, g2, g3, w_o, w_g, w_u, w_d, tm):
    T = x.shape[0]

    def body(of_ref, om_ref, x_ref, t_ref, gfo_ref, gmo_ref, g2_ref, g3_ref, wo_ref, wg_ref, wu_ref, wd_ref,
             a_ref, h2_ref, hid_ref, dg_ref, du_ref, dx3_ref, dx2_ref, dof_ref, dom_ref, st_ref):
        i = pl.program_id(0)

        @pl.when(i == 0)
        def _():
            st_ref[...] = jnp.zeros_like(st_ref)

        ofv, omv = of_ref[...], om_ref[...]
        rf, rm = _rms(ofv, FOX_W), _rms(omv, FOX_W)
        fhat, mhat = ofv * rf, omv * rm
        a = jnp.concatenate([fhat * gfo_ref[...], mhat * gmo_ref[...]], axis=1).astype(BF16)
        a_ref[...] = a
        x2 = x_ref[...] + _dot(a, wo_ref[...])
        r2 = _rms(x2, D_MODEL)
        xh2 = x2 * r2
        h2 = (xh2 * g2_ref[...]).astype(BF16)
        h2_ref[...] = h2
        gt = _dot_nt(h2, wg_ref[...])
        up = _dot_nt(h2, wu_ref[...])
        sg = jax.nn.sigmoid(gt)
        sl = gt * sg
        hid = (sl * up).astype(BF16)
        hid_ref[...] = hid
        x3 = x2 + _dot(hid, wd_ref[...])
        r3 = _rms(x3, D_MODEL)
        xh3 = x3 * r3
        diff = xh3 * g3_ref[...] - t_ref[...]
        dy = diff * (1.0 / D_MODEL)
        st_ref[ROW_LOSS:ROW_LOSS + 1, :] += jnp.sum(diff * diff, axis=0, keepdims=True) * (0.5 / D_MODEL)
        st_ref[ROW_FINAL:ROW_FINAL + 1, :] += jnp.sum(dy * xh3, axis=0, keepdims=True)
        dx3 = _rms_bwd(dy, xh3, r3, g3_ref[...], D_MODEL)
        dx3b = dx3.astype(BF16)
        dx3_ref[...] = dx3b
        dhid = _dot_nt(dx3b, wd_ref[...])
        dg = (dhid * up * (sg * (1.0 + gt * (1.0 - sg)))).astype(BF16)
        du = (dhid * sl).astype(BF16)
        dg_ref[...] = dg
        du_ref[...] = du
        dh2 = _dot(dg, wg_ref[...]) + _dot(du, wu_ref[...])
        st_ref[ROW_NORM_FFN:ROW_NORM_FFN + 1, :] += jnp.sum(dh2 * xh2, axis=0, keepdims=True)
        dx2 = dx3 + _rms_bwd(dh2, xh2, r2, g2_ref[...], D_MODEL)
        dx2_ref[...] = dx2
        da = _dot_nt(dx2.astype(BF16), wo_ref[...])
        daf, dam = da[:, 0:FOX_W], da[:, FOX_W:2 * FOX_W]
        st_ref[ROW_OUT:ROW_OUT + 1, 0:FOX_W] += jnp.sum(daf * fhat, axis=0, keepdims=True)
        st_ref[ROW_OUT:ROW_OUT + 1, FOX_W:2 * FOX_W] += jnp.sum(dam * mhat, axis=0, keepdims=True)
        dof_ref[...] = _rms_bwd(daf, fhat, rf, gfo_ref[...], FOX_W).astype(BF16)
        dom_ref[...] = _rms_bwd(dam, mhat, rm, gmo_ref[...], FOX_W).astype(BF16)

    row = lambda w: pl.BlockSpec((tm, w), lambda i: (i, 0))
    ff = jax.ShapeDtypeStruct((T, D_FF), BF16)
    out_shape = (
        jax.ShapeDtypeStruct((T, 1024), BF16), jax.ShapeDtypeStruct((T, 1024), BF16), ff, ff, ff,
        jax.ShapeDtypeStruct((T, 1024), BF16), jax.ShapeDtypeStruct((T, 1024), F32),
        jax.ShapeDtypeStruct((T, 512), BF16), jax.ShapeDtypeStruct((T, 512), BF16), jax.ShapeDtypeStruct((8, 1024), F32),
    )
    return pl.pallas_call(
        body, name="mid", grid=(T // tm,), out_shape=out_shape,
        in_specs=[row(512), row(512), row(1024), row(1024), _full(g_fo.shape), _full(g_mo.shape), _full(g2.shape), _full(g3.shape),
                  _full(w_o.shape), _full(w_g.shape), _full(w_u.shape), _full(w_d.shape)],
        out_specs=[row(1024), row(1024), row(D_FF), row(D_FF), row(D_FF), row(1024), row(1024), row(512), row(512),
                   pl.BlockSpec((8, 1024), lambda i: (0, 0))],
        compiler_params=_params(("arbitrary",)),
    )(of, om, x, tgt, g_fo, g_mo, g2, g3, w_o, w_g, w_u, w_d)


def _in_bwd(dqf, dkf, dvf, dcb, dqm, dkm, dvm, lat, qn, kvn, x, dx2, g1, gq, gkv, bfg, ct, st, w_in, w_q12, w_kv, seq, tm):
    T = x.shape[0]
    nblk = T // tm
    nsb = seq // tm

    def body(dqf_ref, dkf_ref, dvf_ref, dcb_ref, dqm_ref, dkm_ref, dvm_ref, lat_ref, qn_ref, kvn_ref, x_ref, dx2_ref, g1_ref, gq_ref,
             gkv_ref, b_ref, ct_ref, st_ref, win_ref, wq_ref, wkv_ref, dx_ref, dproj_ref, gq12_ref, gkv12_ref, stat_ref, carry,
             dq12_ref, dkv_ref):
        i = pl.program_id(0)

        @pl.when(i == 0)
        def _():
            stat_ref[...] = jnp.zeros_like(stat_ref)
            gq12_ref[...] = jnp.zeros_like(gq12_ref)
            gkv12_ref[...] = jnp.zeros_like(gkv12_ref)

        @pl.when(i % nsb == 0)
        def _():
            carry[...] = jnp.zeros_like(carry)

        lane = lax.broadcasted_iota(jnp.int32, (tm, LANES), 1)
        low = lane < 64
        ctv, stv = ct_ref[...], st_ref[...]

        for j in range(PAIRS):
            e, o = 2 * LANES * j, 2 * LANES * j + LANES
            half = lambda ref, c0: jnp.where(low, ref[:, c0:c0 + LANES].astype(F32), 0.0)
            dq = half(dqf_ref, e) + pltpu.roll(half(dqf_ref, o), 64, 1)
            dk = half(dkf_ref, e) + pltpu.roll(half(dkf_ref, o), 64, 1)
            dproj_ref[:, C_FQ + LANES * j:C_FQ + LANES * (j + 1)] = (dq * FOX_SCALE).astype(BF16)
            dproj_ref[:, C_FK + LANES * j:C_FK + LANES * (j + 1)] = dk.astype(BF16)
        dproj_ref[:, C_FV:C_QL] = dvf_ref[...]
        dc = dcb_ref[...]
        rr = lax.broadcasted_iota(jnp.int32, (tm, tm), 0)
        cc = lax.broadcasted_iota(jnp.int32, (tm, tm), 1)
        triu = (cc >= rr).astype(BF16)
        a0, a1, a2 = _split3(dc)
        dlf = _dot(triu, a0) + _dot(triu, a1) + _dot(triu, a2) + carry[0:1, :]
        carry[0:1, :] = dlf[0:1, :]
        misc_a = lat_ref[:, Q_RANK + KV_RANK:Q_RANK + KV_RANK + LANES]
        z = misc_a + b_ref[...]
        dz = jnp.where(lane < HEADS, dlf * jax.nn.sigmoid(-z), 0.0)
        stat_ref[ROW_B:ROW_B + 1, 0:LANES] += jnp.sum(dz, axis=0, keepdims=True)

        cq = (jnp.where(low, 1.0, 0.0) + ctv) * MLA_SCALE
        sq = stv * MLA_SCALE
        dkpe = jnp.zeros((tm, LANES), F32)
        for hd in range(HEADS):
            s0 = LANES * hd
            dqh = dqm_ref[:, s0:s0 + LANES].astype(F32)
            dq12_ref[:, s0:s0 + LANES] = (dqh * cq).astype(BF16)
            dq12_ref[:, 1024 + s0:1024 + s0 + LANES] = (dqh * sq).astype(BF16)
            dkpe = dkpe + dkm_ref[:, s0:s0 + LANES].astype(F32)
        dkv_ref[:, 0:1024] = dkm_ref[...]
        dkv_ref[:, 1024:1536] = dvm_ref[...]
        dproj_ref[:, C_MA:C_END] = (dz + dkpe * ctv + pltpu.roll(dkpe * stv, 32, 1)).astype(BF16)
        dqn = _dot(dq12_ref[...], wq_ref[...])
        dkvn = _dot_nt(dkv_ref[...], wkv_ref[...])
        gq12_ref[...] += _dot_tn(dq12_ref[...], qn_ref[...])
        gkv12_ref[...] += _dot_tn(kvn_ref[...], dkv_ref[...])
        ql = lat_ref[:, 0:Q_RANK]
        kvl = lat_ref[:, Q_RANK:Q_RANK + KV_RANK]
        rq, rkv = _rms(ql, Q_RANK), _rms(kvl, KV_RANK)
        qhat, kvhat = ql * rq, kvl * rkv
        stat_ref[ROW_Q:ROW_Q + 1, 0:Q_RANK] += jnp.sum(dqn * qhat, axis=0, keepdims=True)
        stat_ref[ROW_KV:ROW_KV + 1, 0:KV_RANK] += jnp.sum(dkvn * kvhat, axis=0, keepdims=True)
        dproj_ref[:, C_QL:C_KVL] = _rms_bwd(dqn, qhat, rq, gq_ref[...], Q_RANK).astype(BF16)
        dproj_ref[:, C_KVL:C_MA] = _rms_bwd(dkvn, kvhat, rkv, gkv_ref[...], KV_RANK).astype(BF16)

        dh1 = _dot(dproj_ref[...], win_ref[...])
        xv = x_ref[...]
        r1 = _rms(xv, D_MODEL)
        xh = xv * r1
        stat_ref[ROW_NORM_MIX:ROW_NORM_MIX + 1, :] += jnp.sum(dh1 * xh, axis=0, keepdims=True)
        dx_ref[...] = dx2_ref[...] + _rms_bwd(dh1, xh, r1, g1_ref[...], D_MODEL)

    rev = lambda w: pl.BlockSpec((tm, w), lambda i: (nblk - 1 - i, 0))
    whole = lambda r, c: pl.BlockSpec((r, c), lambda i: (0, 0))
    out_shape = (
        jax.ShapeDtypeStruct((T, 1024), F32), jax.ShapeDtypeStruct((T, C_END), BF16), jax.ShapeDtypeStruct((2048, Q_RANK), F32),
        jax.ShapeDtypeStruct((KV_RANK, 1536), F32), jax.ShapeDtypeStruct((8, 1024), F32),
    )
    return pl.pallas_call(
        body, name="in_bwd", grid=(nblk,), out_shape=out_shape,
        in_specs=[rev(1024), rev(1024), rev(512), rev(LANES), rev(1024), rev(1024), rev(512), rev(512), rev(Q_RANK), rev(KV_RANK),
                  rev(1024), rev(1024), _full(g1.shape), _full(gq.shape), _full(gkv.shape), _full(bfg.shape), rev(LANES), rev(LANES),
                  _full(w_in.shape), _full(w_q12.shape), _full(w_kv.shape)],
        out_specs=[rev(1024), rev(C_END), whole(2048, Q_RANK), whole(KV_RANK, 1536), whole(8, 1024)],
        scratch_shapes=[pltpu.VMEM((8, LANES), F32), pltpu.VMEM((tm, 2048), BF16), pltpu.VMEM((tm, 1536), BF16)],
        compiler_params=_params(("arbitrary",)),
    )(dqf, dkf, dvf, dcb, dqm, dkm, dvm, lat, qn, kvn, x, dx2, g1, gq, gkv, bfg, ct, st, w_in, w_q12, w_kv)


def _wgrad(a, b, tk, tt, name, ex=None):
    T, K = a.shape
    N = b.shape[1]
    n_in, n_out = (len(ex.ins), len(ex.outs)) if ex else (0, 0)
    gk, gt = K // tk, T // tt

    def body(*refs):
        a_ref, b_ref, o_ref = refs[0], refs[1], refs[2 + n_in]
        kb, t = pl.program_id(0), pl.program_id(1)
        if ex:
            ex_refs = (refs[2:2 + n_in], refs[3 + n_in:3 + n_in + n_out], refs[3 + n_in + n_out:])

            @pl.when((kb == 0) & (t == 0))
            def _():
                ex.start(*ex_refs)

        @pl.when(t == 0)
        def _():
            o_ref[...] = jnp.zeros_like(o_ref)

        o_ref[...] += _dot_tn(a_ref[...].astype(BF16), b_ref[...].astype(BF16))

        if ex:
            @pl.when((kb == gk - 1) & (t == gt - 1))
            def _():
                ex.wait(*ex_refs)

    res = pl.pallas_call(
        body, name=name, grid=(gk, gt), out_shape=(jax.ShapeDtypeStruct((K, N), F32),) + tuple(ex.outs if ex else ()),
        in_specs=[pl.BlockSpec((tt, tk), lambda kb, t: (t, kb)), pl.BlockSpec((tt, N), lambda kb, t: (t, 0))] + [ANY] * n_in,
        out_specs=[pl.BlockSpec((tk, N), lambda kb, t: (kb, 0))] + [ANY] * n_out,
        scratch_shapes=ex.sems() if ex else [], input_output_aliases=ex.aliases(2, 1) if ex else {},
        compiler_params=_params(("arbitrary", "arbitrary")),
    )(a, b, *(ex.ins if ex else ()))
    return (res[0], list(res[1:])) if ex else res[0]


def _adam_update(w, g, m, v):
    nm = ADAM_B1 * m + (1.0 - ADAM_B1) * g
    nv = ADAM_B2 * v + (1.0 - ADAM_B2) * (g * g)
    m_hat = nm / (1.0 - ADAM_B1 ** ADAM_STEP)
    v_hat = nv / (1.0 - ADAM_B2 ** ADAM_STEP)
    return -ADAM_LR * (m_hat / (jnp.sqrt(v_hat) + ADAM_EPS) + ADAM_WD * w), nm, nv


def _adamw_small(stats, params):
    k = len(SMALL)

    def body(*refs):
        for t, name in enumerate(SMALL):
            row, c0, width = SMALL_AT[name]
            w_ref, m_ref, v_ref = refs[1 + 3 * t:4 + 3 * t]
            g_ref, d_ref, nm_ref, nv_ref = refs[1 + 3 * k + 4 * t:5 + 3 * k + 4 * t]
            g = refs[0][row:row + 1, c0:c0 + width]
            g_ref[...] = g
            d_ref[...], nm_ref[...], nv_ref[...] = _adam_update(w_ref[...], g, m_ref[...], v_ref[...])

    vm = pl.BlockSpec(memory_space=pltpu.VMEM)
    out_shape = tuple(jax.ShapeDtypeStruct((1, SMALL_AT[name][2]), F32) for name in SMALL for _ in range(4))
    res = pl.pallas_call(body, name="adamw_small", out_shape=out_shape, in_specs=[vm] * (1 + 3 * k), out_specs=tuple([vm] * (4 * k)))(
        stats, *[a for name in SMALL for a in params[name]])
    return {name: tuple(res[4 * t:4 * t + 4]) for t, name in enumerate(SMALL)}


def _adamw(tensors, name, ex=None):
    n = len(tensors)
    R, C = tensors[0][0].shape
    tr = _row_tile(R)
    steps = R // tr
    n_in, n_out = (len(ex.ins), len(ex.outs)) if ex else (0, 0)
    first_out = 4 * n + n_in

    def body(*refs):
        if ex:
            ex_refs = (refs[4 * n:first_out], refs[first_out + 3 * n:first_out + 3 * n + n_out], refs[first_out + 3 * n + n_out:])

            @pl.when(pl.program_id(0) == 0)
            def _():
                ex.start(*ex_refs)

        for t in range(n):
            w_ref, g_ref, m_ref, v_ref = refs[4 * t:4 * t + 4]
            d_ref, nm_ref, nv_ref = refs[first_out + 3 * t:first_out + 3 * t + 3]
            d_ref[...], nm_ref[...], nv_ref[...] = _adam_update(w_ref[...], g_ref[...], m_ref[...], v_ref[...])

        if ex:
            @pl.when(pl.program_id(0) == steps - 1)
            def _():
                ex.wait(*ex_refs)

    blk = pl.BlockSpec((tr, C), lambda i: (i, 0))
    sh = jax.ShapeDtypeStruct((R, C), F32)
    res = pl.pallas_call(
        body, name=name, grid=(steps,), out_shape=(sh,) * (3 * n) + tuple(ex.outs if ex else ()),
        in_specs=[blk] * (4 * n) + [ANY] * n_in, out_specs=[blk] * (3 * n) + [ANY] * n_out,
        scratch_shapes=ex.sems() if ex else [], input_output_aliases=ex.aliases(4 * n, 3 * n) if ex else {},
        compiler_params=_params(("arbitrary",)),
    )(*[a for t in tensors for a in t], *(ex.ins if ex else ()))
    updates = [tuple(res[3 * t:3 * t + 3]) for t in range(n)]
    return (updates, list(res[3 * n:])) if ex else updates


def _arrange(win_t, wuq_t, wukv):
    dt = win_t.dtype
    z = lambda r: jnp.zeros((r, D_MODEL), dt)
    zh = lambda r: jnp.zeros((HEADS, r, Q_RANK), dt)
    kr1, kr2 = win_t[1928:1944], win_t[1944:1960]
    misc = jnp.concatenate([win_t[1536:1544], z(56), kr1, kr2, kr2, kr1], axis=0)
    w_in = jnp.concatenate([win_t[0:1536], win_t[1544:1928], misc], axis=0)
    wq = wuq_t.reshape(HEADS, 96, Q_RANK)
    q1 = jnp.concatenate([wq, zh(32)], axis=1).reshape(1024, Q_RANK)
    q2 = jnp.concatenate([zh(64), wq[:, 80:96], wq[:, 64:80], zh(32)], axis=1).reshape(1024, Q_RANK)
    wkv = wukv.reshape(KV_RANK, HEADS, 128)
    wk = jnp.concatenate([wkv[:, :, 0:64], jnp.zeros((KV_RANK, HEADS, 64), dt)], axis=2).reshape(KV_RANK, 1024)
    wv = wkv[:, :, 64:128].reshape(KV_RANK, 512)
    return dict(w_in=w_in, w_q12=jnp.concatenate([q1, q2], axis=0), w_k=wk, w_v=wv, w_kv=jnp.concatenate([wk, wv], axis=1))


def _unarrange(g_in, g_q12, g_kv):
    kr1 = g_in[C_MA + 64:C_MA + 80] + g_in[C_MA + 112:C_MA + 128]
    kr2 = g_in[C_MA + 80:C_MA + 96] + g_in[C_MA + 96:C_MA + 112]
    win_t = jnp.concatenate([g_in[0:1536], g_in[C_MA:C_MA + 8], g_in[1536:1920], kr1, kr2], axis=0)
    g1 = g_q12[0:1024].reshape(HEADS, 128, Q_RANK)
    g2 = g_q12[1024:2048].reshape(HEADS, 128, Q_RANK)
    wuq_t = jnp.concatenate([g1[:, 0:64], g1[:, 64:80] + g2[:, 80:96], g1[:, 80:96] + g2[:, 64:80]], axis=1).reshape(768, Q_RANK)
    gk = g_kv[:, 0:1024].reshape(KV_RANK, HEADS, 128)
    gv = g_kv[:, 1024:1536].reshape(KV_RANK, HEADS, 64)
    wukv = jnp.concatenate([gk[:, :, 0:64], gv], axis=2).reshape(KV_RANK, 1024)
    return win_t, wuq_t, wukv


def _selector():
    sel = np.zeros((384, 1024), np.float32)
    for h in range(HEADS):
        for piece in range(3):
            sel[LANES * piece + h, LANES * h + 64 + piece] = 1.0
    return jnp.asarray(sel, BF16)


def _rope_tables(positions):
    inv_freq = 10000.0 ** (-jnp.arange(0, ROPE, 2, dtype=F32) / ROPE)
    n = positions.size
    ang = (positions.reshape(n // 8, 8, 1).astype(F32) * inv_freq[None, None, :]).reshape(n // 8, 8 * (ROPE // 2))
    cos, sin = lax.optimization_barrier((jnp.cos(lax.optimization_barrier(ang)), jnp.sin(lax.optimization_barrier(ang))))
    cos, sin = cos.reshape(n, ROPE // 2), sin.reshape(n, ROPE // 2)
    z64, z32 = jnp.zeros((n, 64), F32), jnp.zeros((n, 32), F32)
    return jnp.concatenate([z64, cos, cos, z32], axis=1), jnp.concatenate([z64, -sin, sin, z32], axis=1)


def _work(name, t):
    return jnp.swapaxes(t[0], 0, 1) if name in TRANSPOSED else t[0]


def _back(name, t):
    return (jnp.swapaxes(t, 0, 1) if name in TRANSPOSED else t)[None]


def kernel(x, positions, norm_mix_g, w_in, b_fgate, q_norm_g, w_uq, kv_norm_g, w_ukv, fox_out_g, mla_out_g, w_o, norm_ffn_g, w_gate, w_up, w_down, final_norm_g, loss_target, m_norm_mix_g, m_w_in, m_b_fgate, m_q_norm_g, m_w_uq, m_kv_norm_g, m_w_ukv, m_fox_out_g, m_mla_out_g, m_w_o, m_norm_ffn_g, m_w_gate, m_w_up, m_w_down, m_final_norm_g, v_norm_mix_g, v_w_in, v_b_fgate, v_q_norm_g, v_w_uq, v_kv_norm_g, v_w_ukv, v_fox_out_g, v_mla_out_g, v_w_o, v_norm_ffn_g, v_w_gate, v_w_up, v_w_down, v_final_norm_g):
    names = ["norm_mix_g", "w_in", "b_fgate", "q_norm_g", "w_uq", "kv_norm_g", "w_ukv", "fox_out_g", "mla_out_g", "w_o",
             "norm_ffn_g", "w_gate", "w_up", "w_down", "final_norm_g"]
    wts = dict(zip(names, [norm_mix_g, w_in, b_fgate, q_norm_g, w_uq, kv_norm_g, w_ukv, fox_out_g, mla_out_g, w_o, norm_ffn_g,
                           w_gate, w_up, w_down, final_norm_g]))
    mom = dict(zip(names, [m_norm_mix_g, m_w_in, m_b_fgate, m_q_norm_g, m_w_uq, m_kv_norm_g, m_w_ukv, m_fox_out_g, m_mla_out_g,
                           m_w_o, m_norm_ffn_g, m_w_gate, m_w_up, m_w_down, m_final_norm_g]))
    var = dict(zip(names, [v_norm_mix_g, v_w_in, v_b_fgate, v_q_norm_g, v_w_uq, v_kv_norm_g, v_w_ukv, v_fox_out_g, v_mla_out_g,
                           v_w_o, v_norm_ffn_g, v_w_gate, v_w_up, v_w_down, v_final_norm_g]))
    shard = {n: _work(n, wts[n]) for n in HEAD3 + FFN4}
    nb, seq, _ = x.shape
    T = nb * seq
    tm, tq = min(ROW_TILE, seq), min(ATTN_TILE, seq)
    tt = min(WGRAD_TILE, T)
    xf = x.reshape(T, D_MODEL)
    tgt = loss_target.reshape(T, D_MODEL)
    chip = 2 * lax.axis_index("x") + lax.axis_index("y")

    mine = [shard[n].astype(BF16) for n in HEAD3]
    head = _run_exchange(_gather_split_exchange(mine), "gather_head")
    win4, wuq4, wukv4 = [lax.dynamic_update_slice(h, s[None], (chip, 0, 0)) for h, s in zip(head, mine)]
    a = _arrange(win4.reshape(-1, D_MODEL), wuq4.reshape(-1, Q_RANK), wukv4.transpose(1, 0, 2).reshape(KV_RANK, -1))
    sel = _selector()
    ct, st = _rope_tables(positions)
    bfg = jnp.concatenate([b_fgate, jnp.zeros((1, LANES - HEADS), F32)], axis=1)
    g1, gq, gkv = norm_mix_g, q_norm_g, kv_norm_g

    h1, qf, kf, vf, qm, km, vm, lat, qn, kvn = _in_proj(xf, g1, a["w_in"], a["w_q12"], a["w_k"], a["w_v"], gq, gkv, bfg, ct, st, sel, seq,
                                                        min(IN_PROJ_TILE, seq))
    tqf = min(ATTN_FWD_TILE, seq)
    of, lse_f, (wo4, wg4) = _attn_fwd(qf, kf, vf, nb, seq, tqf, "fox_fwd", _gather_exchange([shard[n].astype(BF16) for n in FFN4[:2]]))
    om, lse_m, (wu4, wd4) = _attn_fwd(qm, km, vm, nb, seq, tqf, "mla_fwd", _gather_exchange([shard[n].astype(BF16) for n in FFN4[2:]]))
    a_cat, h2, hid, dg, du, dx3, dx2, dof, dom, st_mid = _mid(
        of, om, xf, tgt, fox_out_g, mla_out_g, norm_ffn_g, final_norm_g.reshape(1, D_MODEL),
        wo4.reshape(D_MODEL, D_MODEL), wg4.reshape(D_FF, D_MODEL), wu4.reshape(D_FF, D_MODEL), wd4.reshape(D_FF, D_MODEL), tm)

    slab = lambda g: g.reshape(N_CHIPS, g.shape[0] // N_CHIPS, g.shape[1])
    big = [slab(_wgrad(a_cat, dx2, D_MODEL, tt, "wgrad_o")), slab(_wgrad(dg, h2, D_FF // 2, tt, "wgrad_gate")),
           slab(_wgrad(du, h2, D_FF // 2, tt, "wgrad_up")), slab(_wgrad(hid, dx3, D_FF // 2, tt, "wgrad_down"))]
    (dqf, dkf, dvf, dcb), got = _attn_bwd(qf, kf, vf, of, dof, lse_f, nb, seq, tq, "fox_bwd", True, _swap_exchange(big))
    sums = _add_half(big, got)
    (dqm, dkm, dvm), recv = _attn_bwd(qm, km, vm, om, dom, lse_m, nb, seq, tq, "mla_bwd", False, _scatter_exchange(sums))
    halves = _sum_slabs(big, got, recv)
    dx, dproj, g_q12, g_kv, st_in = _in_bwd(dqf, dkf, dvf, dcb, dqm, dkm, dvm, lat, qn, kvn, xf, dx2, g1, gq, gkv, bfg, ct, st,
                                            a["w_in"], a["w_q12"], a["w_kv"], seq, tm)
    g_in, results = _wgrad(dproj, h1, C_END, tt, "wgrad_in", _both(_join_exchange(halves), _everyone_exchange(st_mid + st_in)))
    gshard = dict(zip(FFN4, results[:4]))
    stats = _sum_devices(results[4])

    gwin_t, gwuq_t, gwukv = _unarrange(g_in, g_q12, g_kv)
    tail = [slab(gwin_t), slab(gwuq_t), gwukv.reshape(KV_RANK, N_CHIPS, -1).transpose(1, 0, 2)]
    quad = lambda n: (shard[n], gshard[n], _work(n, mom[n]), _work(n, var[n]))
    updates = {}
    (updates["w_o"],), tail_got = _adamw([quad("w_o")], "adamw_w_o", _swap_exchange(tail))
    tail_sums = _add_half(tail, tail_got)
    ffn_updates, tail_recv = _adamw([quad(n) for n in FFN4[1:]], "adamw_ffn", _scatter_exchange(tail_sums))
    updates.update(zip(FFN4[1:], ffn_updates))
    tail_joined = _run_exchange(_join_exchange(_sum_slabs(tail, tail_got, tail_recv)), "tail_join")
    gshard.update(zip(HEAD3, tail_joined))
    for n in HEAD3:
        updates[n], = _adamw([quad(n)], "adamw_" + n)

    grads, delta, new_m, new_v = {}, {}, {}, {}
    for n in HEAD3 + FFN4:
        grads[n] = _back(n, gshard[n])
        delta[n], new_m[n], new_v[n] = [_back(n, t) for t in updates[n]]
    row = lambda t: t.reshape(1, -1)
    small = _adamw_small(stats, {n: (row(wts[n]), row(mom[n]), row(var[n])) for n in SMALL})
    for n in SMALL:
        grads[n], delta[n], new_m[n], new_v[n] = [t.reshape(wts[n].shape) for t in small[n]]
    loss = jnp.sum(stats[ROW_LOSS])
    return (loss, dx.reshape(x.shape), *[grads[n] for n in names], *[delta[n] for n in names],
            *[new_m[n] for n in names], *[new_v[n] for n in names])
```

```python
import functools

import numpy as np
import jax
import jax.numpy as jnp
from jax import lax
from jax.experimental import pallas as pl
from jax.experimental.pallas import tpu as pltpu

F32 = jnp.float32
BF16 = jnp.bfloat16
MESH = pl.DeviceIdType.MESH

EPS = 1e-6
D_MODEL = 1024
HEADS = 8
PAIRS = HEADS // 2
FOX_W = 512
Q_RANK = 256
KV_RANK = 128
ROPE = 32
D_FF = 2816
N_CHIPS = 4
FOX_SCALE = 64 ** -0.5
MLA_SCALE = 96 ** -0.5
LANES = 128
NEG = -1e30

ADAM_LR, ADAM_B1, ADAM_B2, ADAM_EPS, ADAM_WD, ADAM_STEP = 0.001, 0.9, 0.999, 1e-08, 0.01, 10

C_FQ, C_FK, C_FV, C_QL, C_KVL, C_MA, C_END = 0, 512, 1024, 1536, 1792, 1920, 2048
C_MB = C_END

VMEM_LIMIT = 60 * 1024 * 1024
ROW_TILE = 256
IN_PROJ_TILE = 512
ATTN_TILE = 512
ATTN_FWD_TILE = 1024
WGRAD_TILE = 2048

HEAD3 = ("w_in", "w_uq", "w_ukv")
FFN4 = ("w_o", "w_gate", "w_up", "w_down")
TRANSPOSED = ("w_in", "w_uq", "w_gate", "w_up")
SMALL = ("norm_mix_g", "b_fgate", "q_norm_g", "kv_norm_g", "fox_out_g", "mla_out_g", "norm_ffn_g", "final_norm_g")
ROW_NORM_MIX, ROW_NORM_FFN, ROW_FINAL, ROW_OUT, ROW_Q, ROW_KV, ROW_B, ROW_LOSS = range(8)
SMALL_AT = {"norm_mix_g": (ROW_NORM_MIX, 0, 1024), "norm_ffn_g": (ROW_NORM_FFN, 0, 1024), "final_norm_g": (ROW_FINAL, 0, 1024),
            "fox_out_g": (ROW_OUT, 0, 512), "mla_out_g": (ROW_OUT, 512, 512), "q_norm_g": (ROW_Q, 0, 256),
            "kv_norm_g": (ROW_KV, 0, 128), "b_fgate": (ROW_B, 0, 8)}


def _params(sem=None):
    return pltpu.CompilerParams(dimension_semantics=sem, vmem_limit_bytes=VMEM_LIMIT)


def _full(shape):
    n = len(shape)
    return pl.BlockSpec(shape, lambda *_: (0,) * n, pipeline_mode=pl.Buffered(1))


def _dot(a, b):
    return jnp.dot(a, b, preferred_element_type=F32)


def _dot_nt(a, b):
    return lax.dot_general(a, b, (((1,), (1,)), ((), ())), preferred_element_type=F32)


def _dot_tn(a, b):
    return lax.dot_general(a, b, (((0,), (0,)), ((), ())), preferred_element_type=F32)


def _split3(v):
    hi = v.astype(BF16)
    r1 = v - hi.astype(F32)
    mid = r1.astype(BF16)
    lo = (r1 - mid.astype(F32)).astype(BF16)
    return hi, mid, lo


def _rms(v, width):
    return lax.rsqrt(jnp.sum(v * v, axis=1, keepdims=True) * (1.0 / width) + EPS)


def _rms_bwd(dy, xhat, r, g, width):
    u = dy * g
    return r * (u - xhat * (jnp.sum(u * xhat, axis=1, keepdims=True) * (1.0 / width)))


ANY = pl.BlockSpec(memory_space=pl.ANY)


def _place():
    return lax.axis_index("x"), lax.axis_index("y"), lax.axis_index("c")


def _other_chips(x, y):
    return [(1 - x, y), (x, 1 - y), (1 - x, 1 - y)]


def _remote(src, dst, send, recv, j, dev):
    return pltpu.make_async_remote_copy(src_ref=src, dst_ref=dst, send_sem=send.at[j], recv_sem=recv.at[j], device_id=dev, device_id_type=MESH)


class _Exchange:
    def __init__(self, ins, outs, n_remote, n_local, build, in_place=False):
        self.ins, self.outs, self.n_remote, self.n_local, self.build = list(ins), list(outs), n_remote, max(n_local, 1), build
        self.in_place = in_place
        self.n_aliased = len(self.ins)

    def aliases(self, first_in, first_out):
        return {first_in + i: first_out + i for i in range(self.n_aliased)} if self.in_place else {}

    def sems(self):
        return [pltpu.SemaphoreType.DMA((self.n_remote,)), pltpu.SemaphoreType.DMA((self.n_remote,)), pltpu.SemaphoreType.DMA((self.n_local,))]

    def start(self, in_refs, out_refs, sems):
        for cp in self.build(in_refs, out_refs, *sems)[0]:
            cp.start()

    def wait(self, in_refs, out_refs, sems):
        for w in self.build(in_refs, out_refs, *sems)[1]:
            w()


def _gather_exchange(shards):
    def build(ins, outs, send, recv, lsem):
        x, y, c = _place()
        starts, waits = [], []
        for i, (s, o) in enumerate(zip(ins, outs)):
            mine = pltpu.make_async_copy(s, o.at[2 * x + y], lsem.at[i])
            starts.append(mine)
            waits.append(mine.wait)
            for j, (cx, cy) in enumerate(_other_chips(x, y)):
                out = _remote(s, o.at[2 * x + y], send, recv, 3 * i + j, (cx, cy, c))
                starts.append(out)
                waits.append(_remote(s, o.at[2 * cx + cy], send, recv, 3 * i + j, (cx, cy, c)).wait_recv)
                waits.append(out.wait_send)
        return starts, waits

    outs = [jax.ShapeDtypeStruct((N_CHIPS,) + s.shape, s.dtype) for s in shards]
    return _Exchange(shards, outs, 3 * len(shards), len(shards), build)


def _gather_split_exchange(shards):
    n = len(shards)

    def build(ins, outs, send, recv, lsem):
        x, y, c = _place()
        starts, waits, last = [], [], []
        for i, (s, o) in enumerate(zip(ins, outs)):
            hc = s.shape[1] // 2
            mine, other = pl.ds(c * hc, hc), pl.ds((1 - c) * hc, hc)
            for j, (cx, cy) in enumerate(_other_chips(x, y)):
                out = _remote(s.at[:, mine], o.at[2 * x + y, :, mine], send, recv, 3 * i + j, (cx, cy, c))
                landed = o.at[2 * cx + cy, :, mine]
                arrive = _remote(s.at[:, mine], landed, send, recv, 3 * i + j, (cx, cy, c))
                onward = _remote(landed, landed, send, recv, 3 * n + 3 * i + j, (x, y, 1 - c))
                from_sibling = _remote(landed, o.at[2 * cx + cy, :, other], send, recv, 3 * n + 3 * i + j, (x, y, 1 - c))
                starts.append(out)
                waits.append(lambda arrive=arrive, onward=onward: (arrive.wait_recv(), onward.start()))
                last += [from_sibling.wait_recv, onward.wait_send, out.wait_send]
        return starts, waits + last

    outs = [jax.ShapeDtypeStruct((N_CHIPS,) + s.shape, s.dtype) for s in shards]
    return _Exchange(shards, outs, 6 * n, 0, build)


def _swap_exchange(grads):
    def build(ins, outs, send, recv, lsem):
        x, y, c = _place()
        cps = []
        for i, (g, o) in enumerate(zip(ins, outs)):
            hc = g.shape[2] // 2
            cps.append(_remote(g.at[:, :, pl.ds((1 - c) * hc, hc)], o, send, recv, i, (x, y, 1 - c)))
        return cps, [cp.wait for cp in cps]

    outs = [jax.ShapeDtypeStruct((g.shape[0], g.shape[1], g.shape[2] // 2), g.dtype) for g in grads]
    return _Exchange(grads, outs, len(grads), 0, build)


def _scatter_exchange(sums):
    def build(ins, outs, send, recv, lsem):
        x, y, c = _place()
        cps = []
        for i, (s, o) in enumerate(zip(ins, outs)):
            for j, (cx, cy) in enumerate(_other_chips(x, y)):
                cps.append(_remote(s.at[2 * cx + cy], o.at[j], send, recv, 3 * i + j, (cx, cy, c)))
        return cps, [cp.wait for cp in cps]

    outs = [jax.ShapeDtypeStruct((3,) + s.shape[1:], s.dtype) for s in sums]
    return _Exchange(sums, outs, 3 * len(sums), 0, build)


def _join_exchange(bufs):
    def build(ins, outs, send, recv, lsem):
        x, y, c = _place()
        starts, waits = [], []
        for i, (t, o) in enumerate(zip(ins, outs)):
            hc = t.shape[1] // 2
            out = _remote(t.at[:, pl.ds(c * hc, hc)], o.at[:, pl.ds(c * hc, hc)], send, recv, i, (x, y, 1 - c))
            starts.append(out)
            waits += [_remote(t.at[:, pl.ds(c * hc, hc)], o.at[:, pl.ds((1 - c) * hc, hc)], send, recv, i, (x, y, 1 - c)).wait_recv,
                      out.wait_send]
        return starts, waits

    outs = [jax.ShapeDtypeStruct(t.shape, t.dtype) for t in bufs]
    return _Exchange(bufs, outs, len(bufs), 0, build, in_place=True)


def _everyone_exchange(v):
    def build(ins, outs, send, recv, lsem):
        x, y, c = _place()
        me = 4 * x + 2 * y + c
        mine = pltpu.make_async_copy(ins[0], outs[0].at[me], lsem.at[0])
        starts, waits = [mine], [mine.wait]
        for j in range(7):
            fx, fy, fc = (j + 1) >> 2 & 1, (j + 1) >> 1 & 1, (j + 1) & 1
            peer = (x ^ fx, y ^ fy, c ^ fc)
            out = _remote(ins[0], outs[0].at[me], send, recv, j, peer)
            starts.append(out)
            waits += [_remote(ins[0], outs[0].at[4 * peer[0] + 2 * peer[1] + peer[2]], send, recv, j, peer).wait_recv, out.wait_send]
        return starts, waits

    return _Exchange([v], [jax.ShapeDtypeStruct((8,) + v.shape, v.dtype)], 7, 1, build)


def _both(a, b):
    na_in, na_out = len(a.ins), len(a.outs)

    def build(ins, outs, send, recv, lsem):
        sa, wa = a.build(ins[:na_in], outs[:na_out], send.at[pl.ds(0, a.n_remote)], recv.at[pl.ds(0, a.n_remote)],
                         lsem.at[pl.ds(0, a.n_local)])
        sb, wb = b.build(ins[na_in:], outs[na_out:], send.at[pl.ds(a.n_remote, b.n_remote)], recv.at[pl.ds(a.n_remote, b.n_remote)],
                         lsem.at[pl.ds(a.n_local, b.n_local)])
        return sa + sb, wa + wb

    both = _Exchange(a.ins + b.ins, a.outs + b.outs, a.n_remote + b.n_remote, a.n_local + b.n_local, build, in_place=a.in_place)
    both.n_aliased = na_in
    return both


def _run_exchange(ex, name):
    n_in, n_out = len(ex.ins), len(ex.outs)

    def body(*refs):
        ins, outs, sems = refs[:n_in], refs[n_in:n_in + n_out], refs[n_in + n_out:]
        ex.start(ins, outs, sems)
        ex.wait(ins, outs, sems)

    return pl.pallas_call(
        body, name=name, out_shape=tuple(ex.outs), in_specs=[ANY] * n_in, out_specs=tuple([ANY] * n_out),
        scratch_shapes=ex.sems(), input_output_aliases=ex.aliases(0, 0),
        compiler_params=pltpu.CompilerParams(has_side_effects=True),
    )(*ex.ins)


def _sum_devices(rows):
    def body(r_ref, o_ref):
        acc = r_ref[0]
        for d in range(1, 8):
            acc = acc + r_ref[d]
        o_ref[...] = acc

    vm = pl.BlockSpec(memory_space=pltpu.VMEM)
    return pl.pallas_call(body, name="sum_devices", out_shape=jax.ShapeDtypeStruct(rows.shape[1:], rows.dtype),
                          in_specs=[vm], out_specs=vm)(rows)


def _add_half(gs, gots):
    n = len(gs)

    def body(c_ref, *refs):
        for t in range(n):
            refs[2 * n + t][...] = (refs[2 * t][...] + refs[2 * t + 1][...]).astype(BF16)

    blk = lambda g: (1, g.shape[1], g.shape[2] // 2)
    in_specs = [s for g in gs for s in (pl.BlockSpec(blk(g), lambda k, c_ref: (k, 0, c_ref[0])),
                                        pl.BlockSpec(blk(g), lambda k, c_ref: (k, 0, 0)))]
    res = pl.pallas_call(
        body, name="add_half",
        grid_spec=pltpu.PrefetchScalarGridSpec(
            num_scalar_prefetch=1, grid=(N_CHIPS,), in_specs=in_specs,
            out_specs=[pl.BlockSpec(blk(g), lambda k, c_ref: (k, 0, 0)) for g in gs]),
        out_shape=[jax.ShapeDtypeStruct((N_CHIPS,) + blk(g)[1:], BF16) for g in gs],
        compiler_params=_params(("arbitrary",)),
    )(jnp.reshape(lax.axis_index("c"), (1,)).astype(jnp.int32), *[a for pair in zip(gs, gots) for a in pair])
    return list(res)


def _sum_slabs(gs, gots, recvs):
    n = len(gs)

    def body(kc_ref, *refs):
        for t in range(n):
            g_ref, s_ref, r_ref = refs[3 * t:3 * t + 3]
            refs[3 * n + t][...] = (((g_ref[0] + s_ref[0]) + r_ref[0].astype(F32)) + r_ref[1].astype(F32)) + r_ref[2].astype(F32)

    half = lambda g: (g.shape[1], g.shape[2] // 2)
    in_specs = [s for g in gs for s in (pl.BlockSpec((1,) + half(g), lambda i, kc_ref: (kc_ref[0], 0, kc_ref[1])),
                                        pl.BlockSpec((1,) + half(g), lambda i, kc_ref: (kc_ref[0], 0, 0)),
                                        pl.BlockSpec((3,) + half(g), lambda i, kc_ref: (0, 0, 0)))]
    kc = jnp.stack([2 * lax.axis_index("x") + lax.axis_index("y"), lax.axis_index("c")]).astype(jnp.int32)
    res = pl.pallas_call(
        body, name="sum_slabs",
        grid_spec=pltpu.PrefetchScalarGridSpec(
            num_scalar_prefetch=1, grid=(1,), in_specs=in_specs,
            out_specs=[pl.BlockSpec(half(g), lambda i, kc_ref: (0, kc_ref[1])) for g in gs]),
        out_shape=[jax.ShapeDtypeStruct(g.shape[1:], F32) for g in gs],
        compiler_params=_params(("arbitrary",)),
    )(kc, *[a for trio in zip(gs, gots, recvs) for a in trio])
    return list(res)


def _row_tile(rows):
    for cand in (256, 184, 176, 144, 128, 64, 32, 16, 8):
        if rows % cand == 0:
            return cand
    return rows


def _prenorm(x, g1, tm, ex):
    T = x.shape[0]
    steps = T // tm
    n_in, n_out = len(ex.ins), len(ex.outs)

    def body(*refs):
        x_ref, g1_ref, h1_ref = refs[0], refs[1], refs[2 + n_in]
        ex_refs = (refs[2:2 + n_in], refs[3 + n_in:3 + n_in + n_out], refs[3 + n_in + n_out:])

        @pl.when(pl.program_id(0) == 0)
        def _():
            ex.start(*ex_refs)

        xv = x_ref[...]
        h1_ref[...] = (xv * _rms(xv, D_MODEL) * g1_ref[...]).astype(BF16)

        @pl.when(pl.program_id(0) == steps - 1)
        def _():
            ex.wait(*ex_refs)

    row = pl.BlockSpec((tm, D_MODEL), lambda i: (i, 0))
    res = pl.pallas_call(
        body, name="prenorm", grid=(steps,), out_shape=(jax.ShapeDtypeStruct((T, D_MODEL), BF16),) + tuple(ex.outs),
        in_specs=[row, _full(g1.shape)] + [ANY] * n_in, out_specs=[row] + [ANY] * n_out,
        scratch_shapes=ex.sems(), input_output_aliases=ex.aliases(2, 1),
        compiler_params=_params(("arbitrary",)),
    )(x, g1, *ex.ins)
    return res[0], list(res[1:])


def _in_proj(h1, w_in, w_q12, w_k, w_v, gq, gkv, bfg, ct, st, sel, seq, tm):
    T = h1.shape[0]
    nsb = seq // tm

    def body(h1_ref, win_ref, wq_ref, wk_ref, wv_ref, gq_ref, gkv_ref, b_ref, ct_ref, st_ref, sel_ref,
             qf_ref, kf_ref, vf_ref, qm_ref, km_ref, vm_ref, lat_ref, qn_ref, kvn_ref, carry):
        i = pl.program_id(0)

        @pl.when(i % nsb == 0)
        def _():
            carry[...] = jnp.zeros_like(carry)

        h = h1_ref[...]
        proj = _dot_nt(h, win_ref[...])
        lane = lax.broadcasted_iota(jnp.int32, (tm, LANES), 1)
        low = lane < 64
        misc_a = proj[:, C_MA:C_END]
        misc_b = pltpu.roll(misc_a, 96, 1)

        z = misc_a + b_ref[...]
        lf = jnp.where(lane < HEADS, jnp.minimum(z, 0.0) - jnp.log1p(jnp.exp(-jnp.abs(z))), 0.0)
        rr = lax.broadcasted_iota(jnp.int32, (tm, tm), 0)
        cc = lax.broadcasted_iota(jnp.int32, (tm, tm), 1)
        tri = (rr >= cc).astype(BF16)
        a0, a1, a2 = _split3(lf)
        c = _dot(tri, a0) + _dot(tri, a1) + _dot(tri, a2) + carry[0:1, :]
        carry[0:1, :] = c[tm - 1:tm, :]
        c0, c1, c2 = _split3(c)
        cpl = _dot(jnp.concatenate([c0, c1, c2], axis=1), sel_ref[...])
        qpad = jnp.where((lane >= 64) & (lane < 67), -1.0, 0.0)
        for j in range(PAIRS):
            qc = proj[:, C_FQ + LANES * j:C_FQ + LANES * (j + 1)] * FOX_SCALE
            kc = proj[:, C_FK + LANES * j:C_FK + LANES * (j + 1)]
            e, o = 2 * LANES * j, 2 * LANES * j + LANES
            qf_ref[:, e:e + LANES] = jnp.where(low, qc, qpad).astype(BF16)
            qf_ref[:, o:o + LANES] = jnp.where(low, pltpu.roll(qc, 64, 1), qpad).astype(BF16)
            kf_ref[:, e:e + LANES] = jnp.where(low, kc, cpl[:, e:e + LANES]).astype(BF16)
            kf_ref[:, o:o + LANES] = jnp.where(low, pltpu.roll(kc, 64, 1), cpl[:, o:o + LANES]).astype(BF16)
        vf_ref[...] = proj[:, C_FV:C_QL].astype(BF16)

        ql = proj[:, C_QL:C_KVL]
        kvl = proj[:, C_KVL:C_MA]
        qn = (ql * _rms(ql, Q_RANK) * gq_ref[...]).astype(BF16)
        kvn = (kvl * _rms(kvl, KV_RANK) * gkv_ref[...]).astype(BF16)
        lat_ref[...] = proj[:, C_QL:C_MB]
        qn_ref[...] = qn
        kvn_ref[...] = kvn
        q12 = _dot_nt(qn, wq_ref[...])
        kn = _dot(kvn, wk_ref[...])
        ctv = ct_ref[...]
        stv = st_ref[...]
        cq = (jnp.where(low, 1.0, 0.0) + ctv) * MLA_SCALE
        sq = stv * MLA_SCALE
        kpe = misc_a * ctv + misc_b * stv
        for hd in range(HEADS):
            s0 = LANES * hd
            qm_ref[:, s0:s0 + LANES] = (q12[:, s0:s0 + LANES] * cq + q12[:, 1024 + s0:1024 + s0 + LANES] * sq).astype(BF16)
            km_ref[:, s0:s0 + LANES] = (kn[:, s0:s0 + LANES] + kpe).astype(BF16)
        vm_ref[...] = _dot(kvn, wv_ref[...]).astype(BF16)

    row = lambda w: pl.BlockSpec((tm, w), lambda i: (i, 0))
    out_shape = (
        jax.ShapeDtypeStruct((T, 1024), BF16), jax.ShapeDtypeStruct((T, 1024), BF16), jax.ShapeDtypeStruct((T, 512), BF16),
        jax.ShapeDtypeStruct((T, 1024), BF16), jax.ShapeDtypeStruct((T, 1024), BF16), jax.ShapeDtypeStruct((T, 512), BF16),
        jax.ShapeDtypeStruct((T, 512), F32),
        jax.ShapeDtypeStruct((T, Q_RANK), BF16), jax.ShapeDtypeStruct((T, KV_RANK), BF16),
    )
    return pl.pallas_call(
        body, name="in_proj", grid=(T // tm,), out_shape=out_shape,
        in_specs=[row(D_MODEL), _full(w_in.shape), _full(w_q12.shape), _full(w_k.shape), _full(w_v.shape),
                  _full(gq.shape), _full(gkv.shape), _full(bfg.shape), row(LANES), row(LANES), _full(sel.shape)],
        out_specs=[row(1024), row(1024), row(512), row(1024), row(1024), row(512), row(512), row(Q_RANK), row(KV_RANK)],
        scratch_shapes=[pltpu.VMEM((8, LANES), F32)],
        compiler_params=_params(("arbitrary",)),
    )(h1, w_in, w_q12, w_k, w_v, gq, gkv, bfg, ct, st, sel)


def _attn_fwd(q, k, v, nb, seq, tq, name, ex=None):
    T = q.shape[0]
    nq = seq // tq
    n_in, n_out = (len(ex.ins), len(ex.outs)) if ex else (0, 0)

    def body(*refs):
        q_ref, k_ref, v_ref = refs[0:3]
        o_ref, lse_ref = refs[3 + n_in:5 + n_in]
        b, pr, qi = pl.program_id(0), pl.program_id(1), pl.program_id(2)
        if ex:
            ex_refs = (refs[3:3 + n_in], refs[5 + n_in:5 + n_in + n_out], refs[8 + n_in + n_out:])

            @pl.when((b == 0) & (pr == 0) & (qi == 0))
            def _():
                ex.start(*ex_refs)

        s_sc, p_sc, acc_sc = refs[5 + n_in + n_out:8 + n_in + n_out]
        strip = 64
        key_s = lax.broadcasted_iota(jnp.int32, (strip, tq), 0)
        qry_s = lax.broadcasted_iota(jnp.int32, (strip, tq), 1)
        row_t = lax.broadcasted_iota(jnp.int32, (LANES, tq), 0)
        acc_sc[...] = jnp.zeros(acc_sc.shape, F32)

        def fold(x, op):
            out = x[0:8]
            for r in range(8, strip, 8):
                out = op(out, x[r:r + 8])
            return out

        def step(kj, state, masked):
            rows = pl.ds(pl.multiple_of(kj * tq, tq), tq)
            for hh in range(2):
                s_sc[hh] = _dot_nt(k_ref[rows, LANES * hh:LANES * (hh + 1)], q_ref[:, LANES * hh:LANES * (hh + 1)])
            vv = v_ref[rows, :]
            new = []
            for hh in range(2):
                m, l = state[hh]

                def strip_of(r0, hh=hh):
                    s = s_sc[hh, r0:r0 + strip, :]
                    return jnp.where(key_s + r0 <= qry_s, s, NEG) if masked else s

                mx = fold(strip_of(0), jnp.maximum)
                for r0 in range(strip, tq, strip):
                    mx = jnp.maximum(mx, fold(strip_of(r0), jnp.maximum))
                m_new = jnp.maximum(m, jnp.max(mx, axis=0, keepdims=True))
                alpha = jnp.exp(m - m_new)
                sm = jnp.zeros((8, tq), F32)
                for r0 in range(0, tq, strip):
                    p = jnp.exp(strip_of(r0) - m_new)
                    sm = sm + fold(p, jnp.add)
                    p_sc[hh, r0:r0 + strip, :] = p.astype(BF16)
                l = alpha * l + jnp.sum(sm, axis=0, keepdims=True)
                acc_sc[hh] = alpha * acc_sc[hh] + _dot_tn(vv, p_sc[hh])
                new.append((m_new, l))
            return tuple(new)

        one = (jnp.full((1, tq), NEG, F32), jnp.zeros((1, tq), F32))
        state = lax.fori_loop(0, qi, functools.partial(step, masked=False), (one, one))
        (m0, l0), (m1, l1) = step(qi, state, True)
        o_ref[...] = jnp.where(row_t < 64, acc_sc[0] / l0, acc_sc[1] / l1).T
        lse_ref[:, 0:LANES] = jnp.broadcast_to(m0 + jnp.log(l0), (LANES, tq)).T
        lse_ref[:, LANES:2 * LANES] = jnp.broadcast_to(m1 + jnp.log(l1), (LANES, tq)).T

        if ex:
            @pl.when((b == nb - 1) & (pr == PAIRS - 1) & (qi == nq - 1))
            def _():
                ex.wait(*ex_refs)

    res = pl.pallas_call(
        body, name=name, grid=(nb, PAIRS, nq),
        out_shape=(jax.ShapeDtypeStruct((T, 512), F32), jax.ShapeDtypeStruct((T, 1024), F32)) + tuple(ex.outs if ex else ()),
        in_specs=[pl.BlockSpec((tq, 2 * LANES), lambda b, p, i: (b * nq + i, p)),
                  pl.BlockSpec((seq, 2 * LANES), lambda b, p, i: (b, p)),
                  pl.BlockSpec((seq, LANES), lambda b, p, i: (b, p))] + [ANY] * n_in,
        out_specs=[pl.BlockSpec((tq, LANES), lambda b, p, i: (b * nq + i, p)),
                   pl.BlockSpec((tq, 2 * LANES), lambda b, p, i: (b * nq + i, p))] + [ANY] * n_out,
        scratch_shapes=[pltpu.VMEM((2, tq, tq), F32), pltpu.VMEM((2, tq, tq), BF16), pltpu.VMEM((2, LANES, tq), F32)]
        + (ex.sems() if ex else []),
        compiler_params=_params(("arbitrary", "arbitrary", "arbitrary")),
    )(q, k, v, *(ex.ins if ex else ()))
    return res[0], res[1], list(res[2:])


def _attn_bwd(q, k, v, o, do, lse, nb, seq, tq, name, key_bias, ex=None):
    T = q.shape[0]
    nq = seq // tq
    n_in, n_out = (len(ex.ins), len(ex.outs)) if ex else (0, 0)
    n_res = 4 if key_bias else 3

    def body(*refs):
        q_ref, k_ref, v_ref, o_ref, do_ref, lse_ref = refs[0:6]
        dq_ref, dk_ref, dv_ref = refs[6 + n_in:9 + n_in]
        dcb_ref = refs[9 + n_in] if key_bias else None
        first_scratch = 6 + n_in + n_res + n_out
        dsc, rsum, dq_acc = refs[first_scratch:first_scratch + 3]
        b, pr, step_no = pl.program_id(0), pl.program_id(1), pl.program_id(2)
        kj = nq - 1 - step_no
        if ex:
            ex_refs = (refs[6:6 + n_in], refs[6 + n_in + n_res:6 + n_in + n_res + n_out], refs[first_scratch + 3:])

            @pl.when((b == 0) & (pr == 0) & (step_no == 0))
            def _():
                ex.start(*ex_refs)

        lane_s = lax.broadcasted_iota(jnp.int32, (seq, LANES), 1)
        lane = lax.broadcasted_iota(jnp.int32, (tq, LANES), 1)
        rr = lax.broadcasted_iota(jnp.int32, (tq, tq), 0)
        cc = lax.broadcasted_iota(jnp.int32, (tq, tq), 1)

        @pl.when(step_no == 0)
        def _():
            dq_acc[...] = jnp.zeros_like(dq_acc)
            prod = do_ref[...].astype(F32) * o_ref[...]
            d0 = jnp.sum(jnp.where(lane_s < 64, prod, 0.0), axis=1, keepdims=True)
            d1 = jnp.sum(jnp.where(lane_s < 64, 0.0, prod), axis=1, keepdims=True)
            dsc[0] = jnp.broadcast_to(d0, (seq, LANES))
            dsc[1] = jnp.broadcast_to(d1, (seq, LANES))
            if key_bias:
                rsum[...] = jnp.zeros_like(rsum)

        if key_bias:
            @pl.when((pr == 0) & (step_no == 0))
            def _():
                dcb_ref[...] = jnp.zeros_like(dcb_ref)

        vv = v_ref[...]

        def step(qi, carry, masked):
            dkt, dvt, cols = carry
            rows = pl.ds(pl.multiple_of(qi * tq, tq), tq)
            dov = do_ref[rows, :]
            new_dkt, new_cols = [], []
            for hh in range(2):
                qv = q_ref[rows, LANES * hh:LANES * (hh + 1)]
                kv = k_ref[:, LANES * hh:LANES * (hh + 1)]
                dom = jnp.where((lane < 64) if hh == 0 else (lane >= 64), dov, jnp.zeros((), BF16))
                s = _dot_nt(qv, kv)
                if masked:
                    s = jnp.where(cc <= rr, s, NEG)
                p = jnp.exp(s - jnp.tile(lse_ref[rows, LANES * hh:LANES * (hh + 1)], (1, tq // LANES)))
                dp = _dot_nt(dom, vv)
                ds32 = p * (dp - jnp.tile(dsc[hh, rows, :], (1, tq // LANES)))
                col = cols[hh]
                if key_bias:
                    col = col + jnp.sum(ds32, axis=0, keepdims=True)
                    rsum[hh, rows, :] += jnp.broadcast_to(jnp.sum(ds32, axis=1, keepdims=True), (tq, LANES))
                ds = ds32.astype(BF16)
                dvt = dvt + _dot_tn(dom, p.astype(BF16))
                new_dkt.append(dkt[hh] + _dot_tn(qv, ds))
                new_cols.append(col)
                dq_acc[rows, LANES * hh:LANES * (hh + 1)] += _dot(ds, kv)
            return tuple(new_dkt), dvt, tuple(new_cols)

        zt = jnp.zeros((LANES, tq), F32)
        zc = jnp.zeros((1, tq), F32)
        carry = step(kj, ((zt, zt), zt, (zc, zc)), True)
        dkt, dvt, cols = lax.fori_loop(kj + 1, nq, functools.partial(step, masked=False), carry)
        for hh in range(2):
            dk_ref[:, LANES * hh:LANES * (hh + 1)] = dkt[hh].T.astype(dk_ref.dtype)
        dv_ref[...] = dvt.T.astype(dv_ref.dtype)
        if key_bias:
            row_t = lax.broadcasted_iota(jnp.int32, (LANES, tq), 0)
            per_key = jnp.where(row_t == 2 * pr, -cols[0], 0.0) + jnp.where(row_t == 2 * pr + 1, -cols[1], 0.0)
            dcb_ref[pl.ds(pl.multiple_of(kj * tq, tq), tq), :] += per_key.T

        @pl.when(step_no == nq - 1)
        def _():
            dq_ref[...] = dq_acc[...].astype(dq_ref.dtype)
            if key_bias:
                dcb_ref[...] += jnp.where(lane_s == 2 * pr, rsum[0], 0.0) + jnp.where(lane_s == 2 * pr + 1, rsum[1], 0.0)

        if ex:
            @pl.when((b == nb - 1) & (pr == PAIRS - 1) & (step_no == nq - 1))
            def _():
                ex.wait(*ex_refs)

    per_seq = lambda w: pl.BlockSpec((seq, w), lambda b, p, j: (b, p))
    per_blk = lambda w: pl.BlockSpec((tq, w), lambda b, p, j: (b * nq + nq - 1 - j, p))
    res = pl.pallas_call(
        body, name=name, grid=(nb, PAIRS, nq),
        out_shape=(jax.ShapeDtypeStruct((T, 1024), BF16), jax.ShapeDtypeStruct((T, 1024), BF16), jax.ShapeDtypeStruct((T, 512), BF16))
        + ((jax.ShapeDtypeStruct((T, LANES), F32),) if key_bias else ()) + tuple(ex.outs if ex else ()),
        in_specs=[per_seq(2 * LANES), per_blk(2 * LANES), per_blk(LANES), per_seq(LANES), per_seq(LANES), per_seq(2 * LANES)] + [ANY] * n_in,
        out_specs=[per_seq(2 * LANES), per_blk(2 * LANES), per_blk(LANES)]
        + ([pl.BlockSpec((seq, LANES), lambda b, p, j: (b, 0))] if key_bias else []) + [ANY] * n_out,
        scratch_shapes=[pltpu.VMEM((2, seq, LANES), F32), pltpu.VMEM((2, seq, LANES) if key_bias else (2, 8, LANES), F32),
                        pltpu.VMEM((seq, 2 * LANES), F32)]
        + (ex.sems() if ex else []),
        compiler_params=_params(("arbitrary", "arbitrary", "arbitrary")),
    )(q, k, v, o, do, lse, *(ex.ins if ex else ()))
    return list(res[:n_res]), list(res[n_res:])


def _mid(of, om, x, tgt, g_fo, g_mo, g2, g3, w_o, w_g, w_u, w_d, tm):
    T = x.shape[0]

    def body(of_ref, om_ref, x_ref, t_ref, gfo_ref, gmo_ref, g2_ref, g3_ref, wo_ref, wg_ref, wu_ref, wd_ref,
             a_ref, h2_ref, hid_ref, dg_ref, du_ref, dx3_ref, dx2_ref, dof_ref, dom_ref, st_ref):
        i = pl.program_id(0)

        @pl.when(i == 0)
        def _():
            st_ref[...] = jnp.zeros_like(st_ref)

        ofv, omv = of_ref[...], om_ref[...]
        rf, rm = _rms(ofv, FOX_W), _rms(omv, FOX_W)
        fhat, mhat = ofv * rf, omv * rm
        a = jnp.concatenate([fhat * gfo_ref[...], mhat * gmo_ref[...]], axis=1).astype(BF16)
        a_ref[...] = a
        x2 = x_ref[...] + _dot(a, wo_ref[...])
        r2 = _rms(x2, D_MODEL)
        xh2 = x2 * r2
        h2 = (xh2 * g2_ref[...]).astype(BF16)
        h2_ref[...] = h2
        gt = _dot_nt(h2, wg_ref[...])
        up = _dot_nt(h2, wu_ref[...])
        sg = jax.nn.sigmoid(gt)
        sl = gt * sg
        hid = (sl * up).astype(BF16)
        hid_ref[...] = hid
        x3 = x2 + _dot(hid, wd_ref[...])
        r3 = _rms(x3, D_MODEL)
        xh3 = x3 * r3
        diff = xh3 * g3_ref[...] - t_ref[...]
        dy = diff * (1.0 / D_MODEL)
        st_ref[ROW_LOSS:ROW_LOSS + 1, :] += jnp.sum(diff * diff, axis=0, keepdims=True) * (0.5 / D_MODEL)
        st_ref[ROW_FINAL:ROW_FINAL + 1, :] += jnp.sum(dy * xh3, axis=0, keepdims=True)
        dx3 = _rms_bwd(dy, xh3, r3, g3_ref[...], D_MODEL)
        dx3b = dx3.astype(BF16)
        dx3_ref[...] = dx3b
        dhid = _dot_nt(dx3b, wd_ref[...])
        dg = (dhid * up * (sg * (1.0 + gt * (1.0 - sg)))).astype(BF16)
        du = (dhid * sl).astype(BF16)
        dg_ref[...] = dg
        du_ref[...] = du
        dh2 = _dot(dg, wg_ref[...]) + _dot(du, wu_ref[...])
        st_ref[ROW_NORM_FFN:ROW_NORM_FFN + 1, :] += jnp.sum(dh2 * xh2, axis=0, keepdims=True)
        dx2 = dx3 + _rms_bwd(dh2, xh2, r2, g2_ref[...], D_MODEL)
        dx2_ref[...] = dx2
        da = _dot_nt(dx2.astype(BF16), wo_ref[...])
        daf, dam = da[:, 0:FOX_W], da[:, FOX_W:2 * FOX_W]
        st_ref[ROW_OUT:ROW_OUT + 1, 0:FOX_W] += jnp.sum(daf * fhat, axis=0, keepdims=True)
        st_ref[ROW_OUT:ROW_OUT + 1, FOX_W:2 * FOX_W] += jnp.sum(dam * mhat, axis=0, keepdims=True)
        dof_ref[...] = _rms_bwd(daf, fhat, rf, gfo_ref[...], FOX_W).astype(BF16)
        dom_ref[...] = _rms_bwd(dam, mhat, rm, gmo_ref[...], FOX_W).astype(BF16)

    row = lambda w: pl.BlockSpec((tm, w), lambda i: (i, 0))
    ff = jax.ShapeDtypeStruct((T, D_FF), BF16)
    out_shape = (
        jax.ShapeDtypeStruct((T, 1024), BF16), jax.ShapeDtypeStruct((T, 1024), BF16), ff, ff, ff,
        jax.ShapeDtypeStruct((T, 1024), BF16), jax.ShapeDtypeStruct((T, 1024), F32),
        jax.ShapeDtypeStruct((T, 512), BF16), jax.ShapeDtypeStruct((T, 512), BF16), jax.ShapeDtypeStruct((8, 1024), F32),
    )
    return pl.pallas_call(
        body, name="mid", grid=(T // tm,), out_shape=out_shape,
        in_specs=[row(512), row(512), row(1024), row(1024), _full(g_fo.shape), _full(g_mo.shape), _full(g2.shape), _full(g3.shape),
                  _full(w_o.shape), _full(w_g.shape), _full(w_u.shape), _full(w_d.shape)],
        out_specs=[row(1024), row(1024), row(D_FF), row(D_FF), row(D_FF), row(1024), row(1024), row(512), row(512),
                   pl.BlockSpec((8, 1024), lambda i: (0, 0))],
        compiler_params=_params(("arbitrary",)),
    )(of, om, x, tgt, g_fo, g_mo, g2, g3, w_o, w_g, w_u, w_d)


def _in_bwd(dqf, dkf, dvf, dcb, dqm, dkm, dvm, lat, qn, kvn, x, dx2, g1, gq, gkv, bfg, ct, st, w_in, w_q12, w_kv, seq, tm):
    T = x.shape[0]
    nblk = T // tm
    nsb = seq // tm

    def body(dqf_ref, dkf_ref, dvf_ref, dcb_ref, dqm_ref, dkm_ref, dvm_ref, lat_ref, qn_ref, kvn_ref, x_ref, dx2_ref, g1_ref, gq_ref,
             gkv_ref, b_ref, ct_ref, st_ref, win_ref, wq_ref, wkv_ref, dx_ref, dproj_ref, gq12_ref, gkv12_ref, stat_ref, carry,
             dq12_ref, dkv_ref):
        i = pl.program_id(0)

        @pl.when(i == 0)
        def _():
            stat_ref[...] = jnp.zeros_like(stat_ref)
            gq12_ref[...] = jnp.zeros_like(gq12_ref)
            gkv12_ref[...] = jnp.zeros_like(gkv12_ref)

        @pl.when(i % nsb == 0)
        def _():
            carry[...] = jnp.zeros_like(carry)

        lane = lax.broadcasted_iota(jnp.int32, (tm, LANES), 1)
        low = lane < 64
        ctv, stv = ct_ref[...], st_ref[...]

        for j in range(PAIRS):
            e, o = 2 * LANES * j, 2 * LANES * j + LANES
            half = lambda ref, c0: jnp.where(low, ref[:, c0:c0 + LANES].astype(F32), 0.0)
            dq = half(dqf_ref, e) + pltpu.roll(half(dqf_ref, o), 64, 1)
            dk = half(dkf_ref, e) + pltpu.roll(half(dkf_ref, o), 64, 1)
            dproj_ref[:, C_FQ + LANES * j:C_FQ + LANES * (j + 1)] = (dq * FOX_SCALE).astype(BF16)
            dproj_ref[:, C_FK + LANES * j:C_FK + LANES * (j + 1)] = dk.astype(BF16)
        dproj_ref[:, C_FV:C_QL] = dvf_ref[...]
        dc = dcb_ref[...]
        rr = lax.broadcasted_iota(jnp.int32, (tm, tm), 0)
        cc = lax.broadcasted_iota(jnp.int32, (tm, tm), 1)
        triu = (cc >= rr).astype(BF16)
        a0, a1, a2 = _split3(dc)
        dlf = _dot(triu, a0) + _dot(triu, a1) + _dot(triu, a2) + carry[0:1, :]
        carry[0:1, :] = dlf[0:1, :]
        misc_a = lat_ref[:, Q_RANK + KV_RANK:Q_RANK + KV_RANK + LANES]
        z = misc_a + b_ref[...]
        dz = jnp.where(lane < HEADS, dlf * jax.nn.sigmoid(-z), 0.0)
        stat_ref[ROW_B:ROW_B + 1, 0:LANES] += jnp.sum(dz, axis=0, keepdims=True)

        cq = (jnp.where(low, 1.0, 0.0) + ctv) * MLA_SCALE
        sq = stv * MLA_SCALE
        dkpe = jnp.zeros((tm, LANES), F32)
        for hd in range(HEADS):
            s0 = LANES * hd
            dqh = dqm_ref[:, s0:s0 + LANES].astype(F32)
            dq12_ref[:, s0:s0 + LANES] = (dqh * cq).astype(BF16)
            dq12_ref[:, 1024 + s0:1024 + s0 + LANES] = (dqh * sq).astype(BF16)
            dkpe = dkpe + dkm_ref[:, s0:s0 + LANES].astype(F32)
        dkv_ref[:, 0:1024] = dkm_ref[...]
        dkv_ref[:, 1024:1536] = dvm_ref[...]
        dproj_ref[:, C_MA:C_END] = (dz + dkpe * ctv + pltpu.roll(dkpe * stv, 32, 1)).astype(BF16)
        dqn = _dot(dq12_ref[...], wq_ref[...])
        dkvn = _dot_nt(dkv_ref[...], wkv_ref[...])
        gq12_ref[...] += _dot_tn(dq12_ref[...], qn_ref[...])
        gkv12_ref[...] += _dot_tn(kvn_ref[...], dkv_ref[...])
        ql = lat_ref[:, 0:Q_RANK]
        kvl = lat_ref[:, Q_RANK:Q_RANK + KV_RANK]
        rq, rkv = _rms(ql, Q_RANK), _rms(kvl, KV_RANK)
        qhat, kvhat = ql * rq, kvl * rkv
        stat_ref[ROW_Q:ROW_Q + 1, 0:Q_RANK] += jnp.sum(dqn * qhat, axis=0, keepdims=True)
        stat_ref[ROW_KV:ROW_KV + 1, 0:KV_RANK] += jnp.sum(dkvn * kvhat, axis=0, keepdims=True)
        dproj_ref[:, C_QL:C_KVL] = _rms_bwd(dqn, qhat, rq, gq_ref[...], Q_RANK).astype(BF16)
        dproj_ref[:, C_KVL:C_MA] = _rms_bwd(dkvn, kvhat, rkv, gkv_ref[...], KV_RANK).astype(BF16)

        dh1 = _dot(dproj_ref[...], win_ref[...])
        xv = x_ref[...]
        r1 = _rms(xv, D_MODEL)
        xh = xv * r1
        stat_ref[ROW_NORM_MIX:ROW_NORM_MIX + 1, :] += jnp.sum(dh1 * xh, axis=0, keepdims=True)
        dx_ref[...] = dx2_ref[...] + _rms_bwd(dh1, xh, r1, g1_ref[...], D_MODEL)

    rev = lambda w: pl.BlockSpec((tm, w), lambda i: (nblk - 1 - i, 0))
    whole = lambda r, c: pl.BlockSpec((r, c), lambda i: (0, 0))
    out_shape = (
        jax.ShapeDtypeStruct((T, 1024), F32), jax.ShapeDtypeStruct((T, C_END), BF16), jax.ShapeDtypeStruct((2048, Q_RANK), F32),
        jax.ShapeDtypeStruct((KV_RANK, 1536), F32), jax.ShapeDtypeStruct((8, 1024), F32),
    )
    return pl.pallas_call(
        body, name="in_bwd", grid=(nblk,), out_shape=out_shape,
        in_specs=[rev(1024), rev(1024), rev(512), rev(LANES), rev(1024), rev(1024), rev(512), rev(512), rev(Q_RANK), rev(KV_RANK),
                  rev(1024), rev(1024), _full(g1.shape), _full(gq.shape), _full(gkv.shape), _full(bfg.shape), rev(LANES), rev(LANES),
                  _full(w_in.shape), _full(w_q12.shape), _full(w_kv.shape)],
        out_specs=[rev(1024), rev(C_END), whole(2048, Q_RANK), whole(KV_RANK, 1536), whole(8, 1024)],
        scratch_shapes=[pltpu.VMEM((8, LANES), F32), pltpu.VMEM((tm, 2048), BF16), pltpu.VMEM((tm, 1536), BF16)],
        compiler_params=_params(("arbitrary",)),
    )(dqf, dkf, dvf, dcb, dqm, dkm, dvm, lat, qn, kvn, x, dx2, g1, gq, gkv, bfg, ct, st, w_in, w_q12, w_kv)


def _wgrad(a, b, tk, tt, name, ex=None):
    T, K = a.shape
    N = b.shape[1]
    n_in, n_out = (len(ex.ins), len(ex.outs)) if ex else (0, 0)
    gk, gt = K // tk, T // tt

    def body(*refs):
        a_ref, b_ref, o_ref = refs[0], refs[1], refs[2 + n_in]
        kb, t = pl.program_id(0), pl.program_id(1)
        if ex:
            ex_refs = (refs[2:2 + n_in], refs[3 + n_in:3 + n_in + n_out], refs[3 + n_in + n_out:])

            @pl.when((kb == 0) & (t == 0))
            def _():
                ex.start(*ex_refs)

        @pl.when(t == 0)
        def _():
            o_ref[...] = jnp.zeros_like(o_ref)

        o_ref[...] += _dot_tn(a_ref[...].astype(BF16), b_ref[...].astype(BF16))

        if ex:
            @pl.when((kb == gk - 1) & (t == gt - 1))
            def _():
                ex.wait(*ex_refs)

    res = pl.pallas_call(
        body, name=name, grid=(gk, gt), out_shape=(jax.ShapeDtypeStruct((K, N), F32),) + tuple(ex.outs if ex else ()),
        in_specs=[pl.BlockSpec((tt, tk), lambda kb, t: (t, kb)), pl.BlockSpec((tt, N), lambda kb, t: (t, 0))] + [ANY] * n_in,
        out_specs=[pl.BlockSpec((tk, N), lambda kb, t: (kb, 0))] + [ANY] * n_out,
        scratch_shapes=ex.sems() if ex else [], input_output_aliases=ex.aliases(2, 1) if ex else {},
        compiler_params=_params(("arbitrary", "arbitrary")),
    )(a, b, *(ex.ins if ex else ()))
    return (res[0], list(res[1:])) if ex else res[0]


def _adam_update(w, g, m, v):
    nm = ADAM_B1 * m + (1.0 - ADAM_B1) * g
    nv = ADAM_B2 * v + (1.0 - ADAM_B2) * (g * g)
    m_hat = nm / (1.0 - ADAM_B1 ** ADAM_STEP)
    v_hat = nv / (1.0 - ADAM_B2 ** ADAM_STEP)
    return -ADAM_LR * (m_hat / (jnp.sqrt(v_hat) + ADAM_EPS) + ADAM_WD * w), nm, nv


def _adamw_small(stats, params):
    k = len(SMALL)

    def body(*refs):
        for t, name in enumerate(SMALL):
            row, c0, width = SMALL_AT[name]
            w_ref, m_ref, v_ref = refs[1 + 3 * t:4 + 3 * t]
            g_ref, d_ref, nm_ref, nv_ref = refs[1 + 3 * k + 4 * t:5 + 3 * k + 4 * t]
            g = refs[0][row:row + 1, c0:c0 + width]
            g_ref[...] = g
            d_ref[...], nm_ref[...], nv_ref[...] = _adam_update(w_ref[...], g, m_ref[...], v_ref[...])

    vm = pl.BlockSpec(memory_space=pltpu.VMEM)
    out_shape = tuple(jax.ShapeDtypeStruct((1, SMALL_AT[name][2]), F32) for name in SMALL for _ in range(4))
    res = pl.pallas_call(body, name="adamw_small", out_shape=out_shape, in_specs=[vm] * (1 + 3 * k), out_specs=tuple([vm] * (4 * k)))(
        stats, *[a for name in SMALL for a in params[name]])
    return {name: tuple(res[4 * t:4 * t + 4]) for t, name in enumerate(SMALL)}


def _adamw(tensors, name):
    n = len(tensors)
    R, C = tensors[0][0].shape
    tr = _row_tile(R)

    def body(*refs):
        for t in range(n):
            w_ref, g_ref, m_ref, v_ref = refs[4 * t:4 * t + 4]
            d_ref, nm_ref, nv_ref = refs[4 * n + 3 * t:4 * n + 3 * t + 3]
            d_ref[...], nm_ref[...], nv_ref[...] = _adam_update(w_ref[...], g_ref[...], m_ref[...], v_ref[...])

    blk = pl.BlockSpec((tr, C), lambda i: (i, 0))
    sh = jax.ShapeDtypeStruct((R, C), F32)
    res = pl.pallas_call(
        body, name=name, grid=(R // tr,), out_shape=(sh,) * (3 * n),
        in_specs=[blk] * (4 * n), out_specs=[blk] * (3 * n),
        compiler_params=_params(("arbitrary",)),
    )(*[a for t in tensors for a in t])
    return [tuple(res[3 * t:3 * t + 3]) for t in range(n)]


def _arrange(win_t, wuq_t, wukv):
    dt = win_t.dtype
    z = lambda r: jnp.zeros((r, D_MODEL), dt)
    zh = lambda r: jnp.zeros((HEADS, r, Q_RANK), dt)
    kr1, kr2 = win_t[1928:1944], win_t[1944:1960]
    misc = jnp.concatenate([win_t[1536:1544], z(56), kr1, kr2, kr2, kr1], axis=0)
    w_in = jnp.concatenate([win_t[0:1536], win_t[1544:1928], misc], axis=0)
    wq = wuq_t.reshape(HEADS, 96, Q_RANK)
    q1 = jnp.concatenate([wq, zh(32)], axis=1).reshape(1024, Q_RANK)
    q2 = jnp.concatenate([zh(64), wq[:, 80:96], wq[:, 64:80], zh(32)], axis=1).reshape(1024, Q_RANK)
    wkv = wukv.reshape(KV_RANK, HEADS, 128)
    wk = jnp.concatenate([wkv[:, :, 0:64], jnp.zeros((KV_RANK, HEADS, 64), dt)], axis=2).reshape(KV_RANK, 1024)
    wv = wkv[:, :, 64:128].reshape(KV_RANK, 512)
    return dict(w_in=w_in, w_q12=jnp.concatenate([q1, q2], axis=0), w_k=wk, w_v=wv, w_kv=jnp.concatenate([wk, wv], axis=1))


def _unarrange(g_in, g_q12, g_kv):
    kr1 = g_in[C_MA + 64:C_MA + 80] + g_in[C_MA + 112:C_MA + 128]
    kr2 = g_in[C_MA + 80:C_MA + 96] + g_in[C_MA + 96:C_MA + 112]
    win_t = jnp.concatenate([g_in[0:1536], g_in[C_MA:C_MA + 8], g_in[1536:1920], kr1, kr2], axis=0)
    g1 = g_q12[0:1024].reshape(HEADS, 128, Q_RANK)
    g2 = g_q12[1024:2048].reshape(HEADS, 128, Q_RANK)
    wuq_t = jnp.concatenate([g1[:, 0:64], g1[:, 64:80] + g2[:, 80:96], g1[:, 80:96] + g2[:, 64:80]], axis=1).reshape(768, Q_RANK)
    gk = g_kv[:, 0:1024].reshape(KV_RANK, HEADS, 128)
    gv = g_kv[:, 1024:1536].reshape(KV_RANK, HEADS, 64)
    wukv = jnp.concatenate([gk[:, :, 0:64], gv], axis=2).reshape(KV_RANK, 1024)
    return win_t, wuq_t, wukv


def _selector():
    sel = np.zeros((384, 1024), np.float32)
    for h in range(HEADS):
        for piece in range(3):
            sel[LANES * piece + h, LANES * h + 64 + piece] = 1.0
    return jnp.asarray(sel, BF16)


def _rope_tables(positions):
    inv_freq = 10000.0 ** (-jnp.arange(0, ROPE, 2, dtype=F32) / ROPE)
    n = positions.size
    ang = (positions.reshape(n // 8, 8, 1).astype(F32) * inv_freq[None, None, :]).reshape(n // 8, 8 * (ROPE // 2))
    cos, sin = lax.optimization_barrier((jnp.cos(lax.optimization_barrier(ang)), jnp.sin(lax.optimization_barrier(ang))))
    cos, sin = cos.reshape(n, ROPE // 2), sin.reshape(n, ROPE // 2)
    z64, z32 = jnp.zeros((n, 64), F32), jnp.zeros((n, 32), F32)
    return jnp.concatenate([z64, cos, cos, z32], axis=1), jnp.concatenate([z64, -sin, sin, z32], axis=1)


def _work(name, t):
    return jnp.swapaxes(t[0], 0, 1) if name in TRANSPOSED else t[0]


def _back(name, t):
    return (jnp.swapaxes(t, 0, 1) if name in TRANSPOSED else t)[None]


def kernel(x, positions, norm_mix_g, w_in, b_fgate, q_norm_g, w_uq, kv_norm_g, w_ukv, fox_out_g, mla_out_g, w_o, norm_ffn_g, w_gate, w_up, w_down, final_norm_g, loss_target, m_norm_mix_g, m_w_in, m_b_fgate, m_q_norm_g, m_w_uq, m_kv_norm_g, m_w_ukv, m_fox_out_g, m_mla_out_g, m_w_o, m_norm_ffn_g, m_w_gate, m_w_up, m_w_down, m_final_norm_g, v_norm_mix_g, v_w_in, v_b_fgate, v_q_norm_g, v_w_uq, v_kv_norm_g, v_w_ukv, v_fox_out_g, v_mla_out_g, v_w_o, v_norm_ffn_g, v_w_gate, v_w_up, v_w_down, v_final_norm_g):
    names = ["norm_mix_g", "w_in", "b_fgate", "q_norm_g", "w_uq", "kv_norm_g", "w_ukv", "fox_out_g", "mla_out_g", "w_o",
             "norm_ffn_g", "w_gate", "w_up", "w_down", "final_norm_g"]
    wts = dict(zip(names, [norm_mix_g, w_in, b_fgate, q_norm_g, w_uq, kv_norm_g, w_ukv, fox_out_g, mla_out_g, w_o, norm_ffn_g,
                           w_gate, w_up, w_down, final_norm_g]))
    mom = dict(zip(names, [m_norm_mix_g, m_w_in, m_b_fgate, m_q_norm_g, m_w_uq, m_kv_norm_g, m_w_ukv, m_fox_out_g, m_mla_out_g,
                           m_w_o, m_norm_ffn_g, m_w_gate, m_w_up, m_w_down, m_final_norm_g]))
    var = dict(zip(names, [v_norm_mix_g, v_w_in, v_b_fgate, v_q_norm_g, v_w_uq, v_kv_norm_g, v_w_ukv, v_fox_out_g, v_mla_out_g,
                           v_w_o, v_norm_ffn_g, v_w_gate, v_w_up, v_w_down, v_final_norm_g]))
    shard = {n: _work(n, wts[n]) for n in HEAD3 + FFN4}
    nb, seq, _ = x.shape
    T = nb * seq
    tm, tq = min(ROW_TILE, seq), min(ATTN_TILE, seq)
    tt = min(WGRAD_TILE, T)
    xf = x.reshape(T, D_MODEL)
    tgt = loss_target.reshape(T, D_MODEL)
    chip = 2 * lax.axis_index("x") + lax.axis_index("y")

    g1, gq, gkv = norm_mix_g, q_norm_g, kv_norm_g
    mine = [shard[n].astype(BF16) for n in HEAD3]
    h1, head = _prenorm(xf, g1, min(WGRAD_TILE, T), _gather_split_exchange(mine))
    win4, wuq4, wukv4 = [lax.dynamic_update_slice(h, s[None], (chip, 0, 0)) for h, s in zip(head, mine)]
    a = _arrange(win4.reshape(-1, D_MODEL), wuq4.reshape(-1, Q_RANK), wukv4.transpose(1, 0, 2).reshape(KV_RANK, -1))
    sel = _selector()
    ct, st = _rope_tables(positions)
    bfg = jnp.concatenate([b_fgate, jnp.zeros((1, LANES - HEADS), F32)], axis=1)

    qf, kf, vf, qm, km, vm, lat, qn, kvn = _in_proj(h1, a["w_in"], a["w_q12"], a["w_k"], a["w_v"], gq, gkv, bfg, ct, st, sel, seq,
                                                    min(IN_PROJ_TILE, seq))
    tqf = min(ATTN_FWD_TILE, seq)
    of, lse_f, (wo4, wg4) = _attn_fwd(qf, kf, vf, nb, seq, tqf, "fox_fwd", _gather_exchange([shard[n].astype(BF16) for n in FFN4[:2]]))
    om, lse_m, (wu4, wd4) = _attn_fwd(qm, km, vm, nb, seq, tqf, "mla_fwd", _gather_exchange([shard[n].astype(BF16) for n in FFN4[2:]]))
    a_cat, h2, hid, dg, du, dx3, dx2, dof, dom, st_mid = _mid(
        of, om, xf, tgt, fox_out_g, mla_out_g, norm_ffn_g, final_norm_g.reshape(1, D_MODEL),
        wo4.reshape(D_MODEL, D_MODEL), wg4.reshape(D_FF, D_MODEL), wu4.reshape(D_FF, D_MODEL), wd4.reshape(D_FF, D_MODEL), tm)

    slab = lambda g: g.reshape(N_CHIPS, g.shape[0] // N_CHIPS, g.shape[1])
    big = [slab(_wgrad(a_cat, dx2, D_MODEL, tt, "wgrad_o")), slab(_wgrad(dg, h2, D_FF // 2, tt, "wgrad_gate")),
           slab(_wgrad(du, h2, D_FF // 2, tt, "wgrad_up")), slab(_wgrad(hid, dx3, D_FF // 2, tt, "wgrad_down"))]
    (dqf, dkf, dvf, dcb), got = _attn_bwd(qf, kf, vf, of, dof, lse_f, nb, seq, tq, "fox_bwd", True, _swap_exchange(big))
    sums = _add_half(big, got)
    (dqm, dkm, dvm), recv = _attn_bwd(qm, km, vm, om, dom, lse_m, nb, seq, tq, "mla_bwd", False, _scatter_exchange(sums))
    halves = _sum_slabs(big, got, recv)
    dx, dproj, g_q12, g_kv, st_in = _in_bwd(dqf, dkf, dvf, dcb, dqm, dkm, dvm, lat, qn, kvn, xf, dx2, g1, gq, gkv, bfg, ct, st,
                                            a["w_in"], a["w_q12"], a["w_kv"], seq, tm)
    g_in, results = _wgrad(dproj, h1, C_END, tt, "wgrad_in", _both(_join_exchange(halves), _everyone_exchange(st_mid + st_in)))
    gshard = dict(zip(FFN4, results[:4]))
    stats = _sum_devices(results[4])

    gwin_t, gwuq_t, gwukv = _unarrange(g_in, g_q12, g_kv)
    tail = [slab(gwin_t), slab(gwuq_t), gwukv.reshape(KV_RANK, N_CHIPS, -1).transpose(1, 0, 2)]
    tail_got = _run_exchange(_swap_exchange(tail), "tail_swap")
    tail_sums = _add_half(tail, tail_got)
    tail_recv = _run_exchange(_scatter_exchange(tail_sums), "tail_scatter")
    tail_joined = _run_exchange(_join_exchange(_sum_slabs(tail, tail_got, tail_recv)), "tail_join")
    gshard.update(zip(HEAD3, tail_joined))
    quad = lambda n: (shard[n], gshard[n], _work(n, mom[n]), _work(n, var[n]))
    updates = dict(zip(FFN4[1:], _adamw([quad(n) for n in FFN4[1:]], "adamw_ffn")))
    for n in HEAD3 + FFN4[:1]:
        updates[n], = _adamw([quad(n)], "adamw_" + n)

    grads, delta, new_m, new_v = {}, {}, {}, {}
    for n in HEAD3 + FFN4:
        grads[n] = _back(n, gshard[n])
        delta[n], new_m[n], new_v[n] = [_back(n, t) for t in updates[n]]
    row = lambda t: t.reshape(1, -1)
    small = _adamw_small(stats, {n: (row(wts[n]), row(mom[n]), row(var[n])) for n in SMALL})
    for n in SMALL:
        grads[n], delta[n], new_m[n], new_v[n] = [t.reshape(wts[n].shape) for t in small[n]]
    loss = jnp.sum(stats[ROW_LOSS])
    return (loss, dx.reshape(x.shape), *[grads[n] for n in names], *[delta[n] for n in names],
            *[new_m[n] for n in names], *[new_v[n] for n in names])
```

```python
import functools

import numpy as np
import jax
import jax.numpy as jnp
from jax import lax
from jax.experimental import pallas as pl
from jax.experimental.pallas import tpu as pltpu

F32 = jnp.float32
BF16 = jnp.bfloat16
MESH = pl.DeviceIdType.MESH

EPS = 1e-6
D_MODEL = 1024
HEADS = 8
PAIRS = HEADS // 2
FOX_W = 512
Q_RANK = 256
KV_RANK = 128
ROPE = 32
D_FF = 2816
N_CHIPS = 4
FOX_SCALE = 64 ** -0.5
MLA_SCALE = 96 ** -0.5
LANES = 128
NEG = -1e30

ADAM_LR, ADAM_B1, ADAM_B2, ADAM_EPS, ADAM_WD, ADAM_STEP = 0.001, 0.9, 0.999, 1e-08, 0.01, 10

C_FQ, C_FK, C_FV, C_QL, C_KVL, C_MA, C_END = 0, 512, 1024, 1536, 1792, 1920, 2048
C_MB = C_END

VMEM_LIMIT = 60 * 1024 * 1024
ROW_TILE = 256
IN_PROJ_TILE = 512
ATTN_TILE = 512
ATTN_FWD_TILE = 1024
WGRAD_TILE = 2048

HEAD3 = ("w_in", "w_uq", "w_ukv")
FFN4 = ("w_o", "w_gate", "w_up", "w_down")
TRANSPOSED = ("w_in", "w_uq", "w_gate", "w_up")
SMALL = ("norm_mix_g", "b_fgate", "q_norm_g", "kv_norm_g", "fox_out_g", "mla_out_g", "norm_ffn_g", "final_norm_g")
ROW_NORM_MIX, ROW_NORM_FFN, ROW_FINAL, ROW_OUT, ROW_Q, ROW_KV, ROW_B, ROW_LOSS = range(8)
SMALL_AT = {"norm_mix_g": (ROW_NORM_MIX, 0, 1024), "norm_ffn_g": (ROW_NORM_FFN, 0, 1024), "final_norm_g": (ROW_FINAL, 0, 1024),
            "fox_out_g": (ROW_OUT, 0, 512), "mla_out_g": (ROW_OUT, 512, 512), "q_norm_g": (ROW_Q, 0, 256),
            "kv_norm_g": (ROW_KV, 0, 128), "b_fgate": (ROW_B, 0, 8)}


def _params(sem=None):
    return pltpu.CompilerParams(dimension_semantics=sem, vmem_limit_bytes=VMEM_LIMIT)


def _full(shape):
    n = len(shape)
    return pl.BlockSpec(shape, lambda *_: (0,) * n, pipeline_mode=pl.Buffered(1))


def _dot(a, b):
    return jnp.dot(a, b, preferred_element_type=F32)


def _dot_nt(a, b):
    return lax.dot_general(a, b, (((1,), (1,)), ((), ())), preferred_element_type=F32)


def _dot_tn(a, b):
    return lax.dot_general(a, b, (((0,), (0,)), ((), ())), preferred_element_type=F32)


def _split3(v):
    hi = v.astype(BF16)
    r1 = v - hi.astype(F32)
    mid = r1.astype(BF16)
    lo = (r1 - mid.astype(F32)).astype(BF16)
    return hi, mid, lo


def _rms(v, width):
    return lax.rsqrt(jnp.sum(v * v, axis=1, keepdims=True) * (1.0 / width) + EPS)


def _rms_bwd(dy, xhat, r, g, width):
    u = dy * g
    return r * (u - xhat * (jnp.sum(u * xhat, axis=1, keepdims=True) * (1.0 / width)))


ANY = pl.BlockSpec(memory_space=pl.ANY)


def _place():
    return lax.axis_index("x"), lax.axis_index("y"), lax.axis_index("c")


def _other_chips(x, y):
    return [(1 - x, y), (x, 1 - y), (1 - x, 1 - y)]


def _remote(src, dst, send, recv, j, dev):
    return pltpu.make_async_remote_copy(src_ref=src, dst_ref=dst, send_sem=send.at[j], recv_sem=recv.at[j], device_id=dev, device_id_type=MESH)


class _Exchange:
    def __init__(self, ins, outs, n_remote, n_local, build, in_place=False):
        self.ins, self.outs, self.n_remote, self.n_local, self.build = list(ins), list(outs), n_remote, max(n_local, 1), build
        self.in_place = in_place
        self.n_aliased = len(self.ins)

    def aliases(self, first_in, first_out):
        return {first_in + i: first_out + i for i in range(self.n_aliased)} if self.in_place else {}

    def sems(self):
        return [pltpu.SemaphoreType.DMA((self.n_remote,)), pltpu.SemaphoreType.DMA((self.n_remote,)), pltpu.SemaphoreType.DMA((self.n_local,))]

    def start(self, in_refs, out_refs, sems):
        for cp in self.build(in_refs, out_refs, *sems)[0]:
            cp.start()

    def wait(self, in_refs, out_refs, sems):
        for w in self.build(in_refs, out_refs, *sems)[1]:
            w()


def _gather_exchange(shards):
    def build(ins, outs, send, recv, lsem):
        x, y, c = _place()
        starts, waits = [], []
        for i, (s, o) in enumerate(zip(ins, outs)):
            mine = pltpu.make_async_copy(s, o.at[2 * x + y], lsem.at[i])
            starts.append(mine)
            waits.append(mine.wait)
            for j, (cx, cy) in enumerate(_other_chips(x, y)):
                out = _remote(s, o.at[2 * x + y], send, recv, 3 * i + j, (cx, cy, c))
                starts.append(out)
                waits.append(_remote(s, o.at[2 * cx + cy], send, recv, 3 * i + j, (cx, cy, c)).wait_recv)
                waits.append(out.wait_send)
        return starts, waits

    outs = [jax.ShapeDtypeStruct((N_CHIPS,) + s.shape, s.dtype) for s in shards]
    return _Exchange(shards, outs, 3 * len(shards), len(shards), build)


def _gather_split_exchange(shards):
    n = len(shards)

    def build(ins, outs, send, recv, lsem):
        x, y, c = _place()
        starts, waits, last = [], [], []
        for i, (s, o) in enumerate(zip(ins, outs)):
            own = pltpu.make_async_copy(s, o.at[2 * x + y], lsem.at[i])
            starts.append(own)
            last.append(own.wait)
            hc = s.shape[1] // 2
            mine, other = pl.ds(c * hc, hc), pl.ds((1 - c) * hc, hc)
            for j, (cx, cy) in enumerate(_other_chips(x, y)):
                out = _remote(s.at[:, mine], o.at[2 * x + y, :, mine], send, recv, 3 * i + j, (cx, cy, c))
                landed = o.at[2 * cx + cy, :, mine]
                arrive = _remote(s.at[:, mine], landed, send, recv, 3 * i + j, (cx, cy, c))
                onward = _remote(landed, landed, send, recv, 3 * n + 3 * i + j, (x, y, 1 - c))
                from_sibling = _remote(landed, o.at[2 * cx + cy, :, other], send, recv, 3 * n + 3 * i + j, (x, y, 1 - c))
                starts.append(out)
                waits.append(lambda arrive=arrive, onward=onward: (arrive.wait_recv(), onward.start()))
                last += [from_sibling.wait_recv, onward.wait_send, out.wait_send]
        return starts, waits + last

    outs = [jax.ShapeDtypeStruct((N_CHIPS,) + s.shape, s.dtype) for s in shards]
    return _Exchange(shards, outs, 6 * n, n, build)


def _swap_exchange(grads):
    def build(ins, outs, send, recv, lsem):
        x, y, c = _place()
        cps = []
        for i, (g, o) in enumerate(zip(ins, outs)):
            hc = g.shape[2] // 2
            cps.append(_remote(g.at[:, :, pl.ds((1 - c) * hc, hc)], o, send, recv, i, (x, y, 1 - c)))
        return cps, [cp.wait for cp in cps]

    outs = [jax.ShapeDtypeStruct((g.shape[0], g.shape[1], g.shape[2] // 2), g.dtype) for g in grads]
    return _Exchange(grads, outs, len(grads), 0, build)


def _scatter_exchange(sums):
    def build(ins, outs, send, recv, lsem):
        x, y, c = _place()
        cps = []
        for i, (s, o) in enumerate(zip(ins, outs)):
            for j, (cx, cy) in enumerate(_other_chips(x, y)):
                cps.append(_remote(s.at[2 * cx + cy], o.at[j], send, recv, 3 * i + j, (cx, cy, c)))
        return cps, [cp.wait for cp in cps]

    outs = [jax.ShapeDtypeStruct((3,) + s.shape[1:], s.dtype) for s in sums]
    return _Exchange(sums, outs, 3 * len(sums), 0, build)


def _join_exchange(bufs):
    def build(ins, outs, send, recv, lsem):
        x, y, c = _place()
        starts, waits = [], []
        for i, (t, o) in enumerate(zip(ins, outs)):
            hc = t.shape[1] // 2
            out = _remote(t.at[:, pl.ds(c * hc, hc)], o.at[:, pl.ds(c * hc, hc)], send, recv, i, (x, y, 1 - c))
            starts.append(out)
            waits += [_remote(t.at[:, pl.ds(c * hc, hc)], o.at[:, pl.ds((1 - c) * hc, hc)], send, recv, i, (x, y, 1 - c)).wait_recv,
                      out.wait_send]
        return starts, waits

    outs = [jax.ShapeDtypeStruct(t.shape, t.dtype) for t in bufs]
    return _Exchange(bufs, outs, len(bufs), 0, build, in_place=True)


def _everyone_exchange(v):
    def build(ins, outs, send, recv, lsem):
        x, y, c = _place()
        me = 4 * x + 2 * y + c
        mine = pltpu.make_async_copy(ins[0], outs[0].at[me], lsem.at[0])
        starts, waits = [mine], [mine.wait]
        for j in range(7):
            fx, fy, fc = (j + 1) >> 2 & 1, (j + 1) >> 1 & 1, (j + 1) & 1
            peer = (x ^ fx, y ^ fy, c ^ fc)
            out = _remote(ins[0], outs[0].at[me], send, recv, j, peer)
            starts.append(out)
            waits += [_remote(ins[0], outs[0].at[4 * peer[0] + 2 * peer[1] + peer[2]], send, recv, j, peer).wait_recv, out.wait_send]
        return starts, waits

    return _Exchange([v], [jax.ShapeDtypeStruct((8,) + v.shape, v.dtype)], 7, 1, build)


def _both(a, b):
    na_in, na_out = len(a.ins), len(a.outs)

    def build(ins, outs, send, recv, lsem):
        sa, wa = a.build(ins[:na_in], outs[:na_out], send.at[pl.ds(0, a.n_remote)], recv.at[pl.ds(0, a.n_remote)],
                         lsem.at[pl.ds(0, a.n_local)])
        sb, wb = b.build(ins[na_in:], outs[na_out:], send.at[pl.ds(a.n_remote, b.n_remote)], recv.at[pl.ds(a.n_remote, b.n_remote)],
                         lsem.at[pl.ds(a.n_local, b.n_local)])
        return sa + sb, wa + wb

    both = _Exchange(a.ins + b.ins, a.outs + b.outs, a.n_remote + b.n_remote, a.n_local + b.n_local, build, in_place=a.in_place)
    both.n_aliased = na_in
    return both


def _run_exchange(ex, name):
    n_in, n_out = len(ex.ins), len(ex.outs)

    def body(*refs):
        ins, outs, sems = refs[:n_in], refs[n_in:n_in + n_out], refs[n_in + n_out:]
        ex.start(ins, outs, sems)
        ex.wait(ins, outs, sems)

    return pl.pallas_call(
        body, name=name, out_shape=tuple(ex.outs), in_specs=[ANY] * n_in, out_specs=tuple([ANY] * n_out),
        scratch_shapes=ex.sems(), input_output_aliases=ex.aliases(0, 0),
        compiler_params=pltpu.CompilerParams(has_side_effects=True),
    )(*ex.ins)


def _sum_devices(rows):
    def body(r_ref, o_ref):
        acc = r_ref[0]
        for d in range(1, 8):
            acc = acc + r_ref[d]
        o_ref[...] = acc

    vm = pl.BlockSpec(memory_space=pltpu.VMEM)
    return pl.pallas_call(body, name="sum_devices", out_shape=jax.ShapeDtypeStruct(rows.shape[1:], rows.dtype),
                          in_specs=[vm], out_specs=vm)(rows)


def _add_half(gs, gots):
    n = len(gs)

    def body(c_ref, *refs):
        for t in range(n):
            refs[2 * n + t][...] = (refs[2 * t][...] + refs[2 * t + 1][...]).astype(BF16)

    blk = lambda g: (1, g.shape[1], g.shape[2] // 2)
    in_specs = [s for g in gs for s in (pl.BlockSpec(blk(g), lambda k, c_ref: (k, 0, c_ref[0])),
                                        pl.BlockSpec(blk(g), lambda k, c_ref: (k, 0, 0)))]
    res = pl.pallas_call(
        body, name="add_half",
        grid_spec=pltpu.PrefetchScalarGridSpec(
            num_scalar_prefetch=1, grid=(N_CHIPS,), in_specs=in_specs,
            out_specs=[pl.BlockSpec(blk(g), lambda k, c_ref: (k, 0, 0)) for g in gs]),
        out_shape=[jax.ShapeDtypeStruct((N_CHIPS,) + blk(g)[1:], BF16) for g in gs],
        compiler_params=_params(("arbitrary",)),
    )(jnp.reshape(lax.axis_index("c"), (1,)).astype(jnp.int32), *[a for pair in zip(gs, gots) for a in pair])
    return list(res)


def _sum_slabs(gs, gots, recvs):
    n = len(gs)

    def body(kc_ref, *refs):
        for t in range(n):
            g_ref, s_ref, r_ref = refs[3 * t:3 * t + 3]
            refs[3 * n + t][...] = (((g_ref[0] + s_ref[0]) + r_ref[0].astype(F32)) + r_ref[1].astype(F32)) + r_ref[2].astype(F32)

    half = lambda g: (g.shape[1], g.shape[2] // 2)
    in_specs = [s for g in gs for s in (pl.BlockSpec((1,) + half(g), lambda i, kc_ref: (kc_ref[0], 0, kc_ref[1])),
                                        pl.BlockSpec((1,) + half(g), lambda i, kc_ref: (kc_ref[0], 0, 0)),
                                        pl.BlockSpec((3,) + half(g), lambda i, kc_ref: (0, 0, 0)))]
    kc = jnp.stack([2 * lax.axis_index("x") + lax.axis_index("y"), lax.axis_index("c")]).astype(jnp.int32)
    res = pl.pallas_call(
        body, name="sum_slabs",
        grid_spec=pltpu.PrefetchScalarGridSpec(
            num_scalar_prefetch=1, grid=(1,), in_specs=in_specs,
            out_specs=[pl.BlockSpec(half(g), lambda i, kc_ref: (0, kc_ref[1])) for g in gs]),
        out_shape=[jax.ShapeDtypeStruct(g.shape[1:], F32) for g in gs],
        compiler_params=_params(("arbitrary",)),
    )(kc, *[a for trio in zip(gs, gots, recvs) for a in trio])
    return list(res)


def _row_tile(rows):
    for cand in (256, 184, 176, 144, 128, 64, 32, 16, 8):
        if rows % cand == 0:
            return cand
    return rows


def _in_proj(x, g1, w_in, w_q12, w_k, w_v, gq, gkv, bfg, ct, st, sel, seq, tm):
    T = x.shape[0]
    nsb = seq // tm

    def body(x_ref, g1_ref, win_ref, wq_ref, wk_ref, wv_ref, gq_ref, gkv_ref, b_ref, ct_ref, st_ref, sel_ref,
             h1_ref, qf_ref, kf_ref, vf_ref, qm_ref, km_ref, vm_ref, lat_ref, qn_ref, kvn_ref, carry):
        i = pl.program_id(0)

        @pl.when(i % nsb == 0)
        def _():
            carry[...] = jnp.zeros_like(carry)

        xv = x_ref[...]
        h = (xv * _rms(xv, D_MODEL) * g1_ref[...]).astype(BF16)
        h1_ref[...] = h
        proj = _dot_nt(h, win_ref[...])
        lane = lax.broadcasted_iota(jnp.int32, (tm, LANES), 1)
        low = lane < 64
        misc_a = proj[:, C_MA:C_END]
        misc_b = pltpu.roll(misc_a, 96, 1)

        z = misc_a + b_ref[...]
        lf = jnp.where(lane < HEADS, jnp.minimum(z, 0.0) - jnp.log1p(jnp.exp(-jnp.abs(z))), 0.0)
        rr = lax.broadcasted_iota(jnp.int32, (tm, tm), 0)
        cc = lax.broadcasted_iota(jnp.int32, (tm, tm), 1)
        tri = (rr >= cc).astype(BF16)
        a0, a1, a2 = _split3(lf)
        c = _dot(tri, a0) + _dot(tri, a1) + _dot(tri, a2) + carry[0:1, :]
        carry[0:1, :] = c[tm - 1:tm, :]
        c0, c1, c2 = _split3(c)
        cpl = _dot(jnp.concatenate([c0, c1, c2], axis=1), sel_ref[...])
        qpad = jnp.where((lane >= 64) & (lane < 67), -1.0, 0.0)
        for j in range(PAIRS):
            qc = proj[:, C_FQ + LANES * j:C_FQ + LANES * (j + 1)] * FOX_SCALE
            kc = proj[:, C_FK + LANES * j:C_FK + LANES * (j + 1)]
            e, o = 2 * LANES * j, 2 * LANES * j + LANES
            qf_ref[:, e:e + LANES] = jnp.where(low, qc, qpad).astype(BF16)
            qf_ref[:, o:o + LANES] = jnp.where(low, pltpu.roll(qc, 64, 1), qpad).astype(BF16)
            kf_ref[:, e:e + LANES] = jnp.where(low, kc, cpl[:, e:e + LANES]).astype(BF16)
            kf_ref[:, o:o + LANES] = jnp.where(low, pltpu.roll(kc, 64, 1), cpl[:, o:o + LANES]).astype(BF16)
        vf_ref[...] = proj[:, C_FV:C_QL].astype(BF16)

        ql = proj[:, C_QL:C_KVL]
        kvl = proj[:, C_KVL:C_MA]
        qn = (ql * _rms(ql, Q_RANK) * gq_ref[...]).astype(BF16)
        kvn = (kvl * _rms(kvl, KV_RANK) * gkv_ref[...]).astype(BF16)
        lat_ref[...] = proj[:, C_QL:C_MB]
        qn_ref[...] = qn
        kvn_ref[...] = kvn
        q12 = _dot_nt(qn, wq_ref[...])
        kn = _dot(kvn, wk_ref[...])
        ctv = ct_ref[...]
        stv = st_ref[...]
        cq = (jnp.where(low, 1.0, 0.0) + ctv) * MLA_SCALE
        sq = stv * MLA_SCALE
        kpe = misc_a * ctv + misc_b * stv
        for hd in range(HEADS):
            s0 = LANES * hd
            qm_ref[:, s0:s0 + LANES] = (q12[:, s0:s0 + LANES] * cq + q12[:, 1024 + s0:1024 + s0 + LANES] * sq).astype(BF16)
            km_ref[:, s0:s0 + LANES] = (kn[:, s0:s0 + LANES] + kpe).astype(BF16)
        vm_ref[...] = _dot(kvn, wv_ref[...]).astype(BF16)

    row = lambda w: pl.BlockSpec((tm, w), lambda i: (i, 0))
    out_shape = (
        jax.ShapeDtypeStruct((T, D_MODEL), BF16),
        jax.ShapeDtypeStruct((T, 1024), BF16), jax.ShapeDtypeStruct((T, 1024), BF16), jax.ShapeDtypeStruct((T, 512), BF16),
        jax.ShapeDtypeStruct((T, 1024), BF16), jax.ShapeDtypeStruct((T, 1024), BF16), jax.ShapeDtypeStruct((T, 512), BF16),
        jax.ShapeDtypeStruct((T, 512), F32),
        jax.ShapeDtypeStruct((T, Q_RANK), BF16), jax.ShapeDtypeStruct((T, KV_RANK), BF16),
    )
    return pl.pallas_call(
        body, name="in_proj", grid=(T // tm,), out_shape=out_shape,
        in_specs=[row(D_MODEL), _full(g1.shape), _full(w_in.shape), _full(w_q12.shape), _full(w_k.shape), _full(w_v.shape),
                  _full(gq.shape), _full(gkv.shape), _full(bfg.shape), row(LANES), row(LANES), _full(sel.shape)],
        out_specs=[row(D_MODEL), row(1024), row(1024), row(512), row(1024), row(1024), row(512), row(512), row(Q_RANK), row(KV_RANK)],
        scratch_shapes=[pltpu.VMEM((8, LANES), F32)],
        compiler_params=_params(("arbitrary",)),
    )(x, g1, w_in, w_q12, w_k, w_v, gq, gkv, bfg, ct, st, sel)


def _attn_fwd(q, k, v, nb, seq, tq, name, ex=None):
    T = q.shape[0]
    nq = seq // tq
    n_in, n_out = (len(ex.ins), len(ex.outs)) if ex else (0, 0)

    def body(*refs):
        q_ref, k_ref, v_ref = refs[0:3]
        o_ref, lse_ref = refs[3 + n_in:5 + n_in]
        b, pr, qi = pl.program_id(0), pl.program_id(1), pl.program_id(2)
        if ex:
            ex_refs = (refs[3:3 + n_in], refs[5 + n_in:5 + n_in + n_out], refs[8 + n_in + n_out:])

            @pl.when((b == 0) & (pr == 0) & (qi == 0))
            def _():
                ex.start(*ex_refs)

        s_sc, p_sc, acc_sc = refs[5 + n_in + n_out:8 + n_in + n_out]
        strip = 64
        key_s = lax.broadcasted_iota(jnp.int32, (strip, tq), 0)
        qry_s = lax.broadcasted_iota(jnp.int32, (strip, tq), 1)
        row_t = lax.broadcasted_iota(jnp.int32, (LANES, tq), 0)
        acc_sc[...] = jnp.zeros(acc_sc.shape, F32)

        def fold(x, op):
            out = x[0:8]
            for r in range(8, strip, 8):
                out = op(out, x[r:r + 8])
            return out

        def step(kj, state, masked):
            rows = pl.ds(pl.multiple_of(kj * tq, tq), tq)
            for hh in range(2):
                s_sc[hh] = _dot_nt(k_ref[rows, LANES * hh:LANES * (hh + 1)], q_ref[:, LANES * hh:LANES * (hh + 1)])
            vv = v_ref[rows, :]
            new = []
            for hh in range(2):
                m, l = state[hh]

                def strip_of(r0, hh=hh):
                    s = s_sc[hh, r0:r0 + strip, :]
                    return jnp.where(key_s + r0 <= qry_s, s, NEG) if masked else s

                mx = fold(strip_of(0), jnp.maximum)
                for r0 in range(strip, tq, strip):
                    mx = jnp.maximum(mx, fold(strip_of(r0), jnp.maximum))
                m_new = jnp.maximum(m, jnp.max(mx, axis=0, keepdims=True))
                alpha = jnp.exp(m - m_new)
                sm = jnp.zeros((8, tq), F32)
                for r0 in range(0, tq, strip):
                    p = jnp.exp(strip_of(r0) - m_new)
                    sm = sm + fold(p, jnp.add)
                    p_sc[hh, r0:r0 + strip, :] = p.astype(BF16)
                l = alpha * l + jnp.sum(sm, axis=0, keepdims=True)
                acc_sc[hh] = alpha * acc_sc[hh] + _dot_tn(vv, p_sc[hh])
                new.append((m_new, l))
            return tuple(new)

        one = (jnp.full((1, tq), NEG, F32), jnp.zeros((1, tq), F32))
        state = lax.fori_loop(0, qi, functools.partial(step, masked=False), (one, one))
        (m0, l0), (m1, l1) = step(qi, state, True)
        o_ref[...] = jnp.where(row_t < 64, acc_sc[0] / l0, acc_sc[1] / l1).T
        lse_ref[:, 0:LANES] = jnp.broadcast_to(m0 + jnp.log(l0), (LANES, tq)).T
        lse_ref[:, LANES:2 * LANES] = jnp.broadcast_to(m1 + jnp.log(l1), (LANES, tq)).T

        if ex:
            @pl.when((b == nb - 1) & (pr == PAIRS - 1) & (qi == nq - 1))
            def _():
                ex.wait(*ex_refs)

    res = pl.pallas_call(
        body, name=name, grid=(nb, PAIRS, nq),
        out_shape=(jax.ShapeDtypeStruct((T, 512), F32), jax.ShapeDtypeStruct((T, 1024), F32)) + tuple(ex.outs if ex else ()),
        in_specs=[pl.BlockSpec((tq, 2 * LANES), lambda b, p, i: (b * nq + i, p)),
                  pl.BlockSpec((seq, 2 * LANES), lambda b, p, i: (b, p)),
                  pl.BlockSpec((seq, LANES), lambda b, p, i: (b, p))] + [ANY] * n_in,
        out_specs=[pl.BlockSpec((tq, LANES), lambda b, p, i: (b * nq + i, p)),
                   pl.BlockSpec((tq, 2 * LANES), lambda b, p, i: (b * nq + i, p))] + [ANY] * n_out,
        scratch_shapes=[pltpu.VMEM((2, tq, tq), F32), pltpu.VMEM((2, tq, tq), BF16), pltpu.VMEM((2, LANES, tq), F32)]
        + (ex.sems() if ex else []),
        compiler_params=_params(("arbitrary", "arbitrary", "arbitrary")),
    )(q, k, v, *(ex.ins if ex else ()))
    return res[0], res[1], list(res[2:])


def _attn_bwd(q, k, v, o, do, lse, nb, seq, tq, name, key_bias, ex=None):
    T = q.shape[0]
    nq = seq // tq
    n_in, n_out = (len(ex.ins), len(ex.outs)) if ex else (0, 0)
    n_res = 4 if key_bias else 3

    def body(*refs):
        q_ref, k_ref, v_ref, o_ref, do_ref, lse_ref = refs[0:6]
        dq_ref, dk_ref, dv_ref = refs[6 + n_in:9 + n_in]
        dcb_ref = refs[9 + n_in] if key_bias else None
        first_scratch = 6 + n_in + n_res + n_out
        dsc, rsum, dq_acc = refs[first_scratch:first_scratch + 3]
        b, pr, step_no = pl.program_id(0), pl.program_id(1), pl.program_id(2)
        kj = nq - 1 - step_no
        if ex:
            ex_refs = (refs[6:6 + n_in], refs[6 + n_in + n_res:6 + n_in + n_res + n_out], refs[first_scratch + 3:])

            @pl.when((b == 0) & (pr == 0) & (step_no == 0))
            def _():
                ex.start(*ex_refs)

        lane_s = lax.broadcasted_iota(jnp.int32, (seq, LANES), 1)
        lane = lax.broadcasted_iota(jnp.int32, (tq, LANES), 1)
        rr = lax.broadcasted_iota(jnp.int32, (tq, tq), 0)
        cc = lax.broadcasted_iota(jnp.int32, (tq, tq), 1)

        @pl.when(step_no == 0)
        def _():
            dq_acc[...] = jnp.zeros_like(dq_acc)
            prod = do_ref[...].astype(F32) * o_ref[...]
            d0 = jnp.sum(jnp.where(lane_s < 64, prod, 0.0), axis=1, keepdims=True)
            d1 = jnp.sum(jnp.where(lane_s < 64, 0.0, prod), axis=1, keepdims=True)
            dsc[0] = jnp.broadcast_to(d0, (seq, LANES))
            dsc[1] = jnp.broadcast_to(d1, (seq, LANES))
            if key_bias:
                rsum[...] = jnp.zeros_like(rsum)

        if key_bias:
            @pl.when((pr == 0) & (step_no == 0))
            def _():
                dcb_ref[...] = jnp.zeros_like(dcb_ref)

        vv = v_ref[...]

        def step(qi, carry, masked):
            dkt, dvt, cols = carry
            rows = pl.ds(pl.multiple_of(qi * tq, tq), tq)
            dov = do_ref[rows, :]
            new_dkt, new_cols = [], []
            for hh in range(2):
                qv = q_ref[rows, LANES * hh:LANES * (hh + 1)]
                kv = k_ref[:, LANES * hh:LANES * (hh + 1)]
                dom = jnp.where((lane < 64) if hh == 0 else (lane >= 64), dov, jnp.zeros((), BF16))
                s = _dot_nt(qv, kv)
                if masked:
                    s = jnp.where(cc <= rr, s, NEG)
                p = jnp.exp(s - jnp.tile(lse_ref[rows, LANES * hh:LANES * (hh + 1)], (1, tq // LANES)))
                dp = _dot_nt(dom, vv)
                ds32 = p * (dp - jnp.tile(dsc[hh, rows, :], (1, tq // LANES)))
                col = cols[hh]
                if key_bias:
                    col = col + jnp.sum(ds32, axis=0, keepdims=True)
                    rsum[hh, rows, :] += jnp.broadcast_to(jnp.sum(ds32, axis=1, keepdims=True), (tq, LANES))
                ds = ds32.astype(BF16)
                dvt = dvt + _dot_tn(dom, p.astype(BF16))
                new_dkt.append(dkt[hh] + _dot_tn(qv, ds))
                new_cols.append(col)
                dq_acc[rows, LANES * hh:LANES * (hh + 1)] += _dot(ds, kv)
            return tuple(new_dkt), dvt, tuple(new_cols)

        zt = jnp.zeros((LANES, tq), F32)
        zc = jnp.zeros((1, tq), F32)
        carry = step(kj, ((zt, zt), zt, (zc, zc)), True)
        dkt, dvt, cols = lax.fori_loop(kj + 1, nq, functools.partial(step, masked=False), carry)
        for hh in range(2):
            dk_ref[:, LANES * hh:LANES * (hh + 1)] = dkt[hh].T.astype(dk_ref.dtype)
        dv_ref[...] = dvt.T.astype(dv_ref.dtype)
        if key_bias:
            row_t = lax.broadcasted_iota(jnp.int32, (LANES, tq), 0)
            per_key = jnp.where(row_t == 2 * pr, -cols[0], 0.0) + jnp.where(row_t == 2 * pr + 1, -cols[1], 0.0)
            dcb_ref[pl.ds(pl.multiple_of(kj * tq, tq), tq), :] += per_key.T

        @pl.when(step_no == nq - 1)
        def _():
            dq_ref[...] = dq_acc[...].astype(dq_ref.dtype)
            if key_bias:
                dcb_ref[...] += jnp.where(lane_s == 2 * pr, rsum[0], 0.0) + jnp.where(lane_s == 2 * pr + 1, rsum[1], 0.0)

        if ex:
            @pl.when((b == nb - 1) & (pr == PAIRS - 1) & (step_no == nq - 1))
            def _():
                ex.wait(*ex_refs)

    per_seq = lambda w: pl.BlockSpec((seq, w), lambda b, p, j: (b, p))
    per_blk = lambda w: pl.BlockSpec((tq, w), lambda b, p, j: (b * nq + nq - 1 - j, p))
    res = pl.pallas_call(
        body, name=name, grid=(nb, PAIRS, nq),
        out_shape=(jax.ShapeDtypeStruct((T, 1024), BF16), jax.ShapeDtypeStruct((T, 1024), BF16), jax.ShapeDtypeStruct((T, 512), BF16))
        + ((jax.ShapeDtypeStruct((T, LANES), F32),) if key_bias else ()) + tuple(ex.outs if ex else ()),
        in_specs=[per_seq(2 * LANES), per_blk(2 * LANES), per_blk(LANES), per_seq(LANES), per_seq(LANES), per_seq(2 * LANES)] + [ANY] * n_in,
        out_specs=[per_seq(2 * LANES), per_blk(2 * LANES), per_blk(LANES)]
        + ([pl.BlockSpec((seq, LANES), lambda b, p, j: (b, 0))] if key_bias else []) + [ANY] * n_out,
        scratch_shapes=[pltpu.VMEM((2, seq, LANES), F32), pltpu.VMEM((2, seq, LANES) if key_bias else (2, 8, LANES), F32),
                        pltpu.VMEM((seq, 2 * LANES), F32)]
        + (ex.sems() if ex else []),
        compiler_params=_params(("arbitrary", "arbitrary", "arbitrary")),
    )(q, k, v, o, do, lse, *(ex.ins if ex else ()))
    return list(res[:n_res]), list(res[n_res:])


def _mid(of, om, x, tgt, g_fo, g_mo, g2, g3, w_o, w_g, w_u, w_d, tm):
    T = x.shape[0]

    def body(of_ref, om_ref, x_ref, t_ref, gfo_ref, gmo_ref, g2_ref, g3_ref, wo_ref, wg_ref, wu_ref, wd_ref,
             a_ref, h2_ref, hid_ref, dg_ref, du_ref, dx3_ref, dx2_ref, dof_ref, dom_ref, st_ref):
        i = pl.program_id(0)

        @pl.when(i == 0)
        def _():
            st_ref[...] = jnp.zeros_like(st_ref)

        ofv, omv = of_ref[...], om_ref[...]
        rf, rm = _rms(ofv, FOX_W), _rms(omv, FOX_W)
        fhat, mhat = ofv * rf, omv * rm
        a = jnp.concatenate([fhat * gfo_ref[...], mhat * gmo_ref[...]], axis=1).astype(BF16)
        a_ref[...] = a
        x2 = x_ref[...] + _dot(a, wo_ref[...])
        r2 = _rms(x2, D_MODEL)
        xh2 = x2 * r2
        h2 = (xh2 * g2_ref[...]).astype(BF16)
        h2_ref[...] = h2
        gt = _dot_nt(h2, wg_ref[...])
        up = _dot_nt(h2, wu_ref[...])
        sg = jax.nn.sigmoid(gt)
        sl = gt * sg
        hid = (sl * up).astype(BF16)
        hid_ref[...] = hid
        x3 = x2 + _dot(hid, wd_ref[...])
        r3 = _rms(x3, D_MODEL)
        xh3 = x3 * r3
        diff = xh3 * g3_ref[...] - t_ref[...]
        dy = diff * (1.0 / D_MODEL)
        st_ref[ROW_LOSS:ROW_LOSS + 1, :] += jnp.sum(diff * diff, axis=0, keepdims=True) * (0.5 / D_MODEL)
        st_ref[ROW_FINAL:ROW_FINAL + 1, :] += jnp.sum(dy * xh3, axis=0, keepdims=True)
        dx3 = _rms_bwd(dy, xh3, r3, g3_ref[...], D_MODEL)
        dx3b = dx3.astype(BF16)
        dx3_ref[...] = dx3b
        dhid = _dot_nt(dx3b, wd_ref[...])
        dg = (dhid * up * (sg * (1.0 + gt * (1.0 - sg)))).astype(BF16)
        du = (dhid * sl).astype(BF16)
        dg_ref[...] = dg
        du_ref[...] = du
        dh2 = _dot(dg, wg_ref[...]) + _dot(du, wu_ref[...])
        st_ref[ROW_NORM_FFN:ROW_NORM_FFN + 1, :] += jnp.sum(dh2 * xh2, axis=0, keepdims=True)
        dx2 = dx3 + _rms_bwd(dh2, xh2, r2, g2_ref[...], D_MODEL)
        dx2_ref[...] = dx2
        da = _dot_nt(dx2.astype(BF16), wo_ref[...])
        daf, dam = da[:, 0:FOX_W], da[:, FOX_W:2 * FOX_W]
        st_ref[ROW_OUT:ROW_OUT + 1, 0:FOX_W] += jnp.sum(daf * fhat, axis=0, keepdims=True)
        st_ref[ROW_OUT:ROW_OUT + 1, FOX_W:2 * FOX_W] += jnp.sum(dam * mhat, axis=0, keepdims=True)
        dof_ref[...] = _rms_bwd(daf, fhat, rf, gfo_ref[...], FOX_W).astype(BF16)
        dom_ref[...] = _rms_bwd(dam, mhat, rm, gmo_ref[...], FOX_W).astype(BF16)

    row = lambda w: pl.BlockSpec((tm, w), lambda i: (i, 0))
    ff = jax.ShapeDtypeStruct((T, D_FF), BF16)
    out_shape = (
        jax.ShapeDtypeStruct((T, 1024), BF16), jax.ShapeDtypeStruct((T, 1024), BF16), ff, ff, ff,
        jax.ShapeDtypeStruct((T, 1024), BF16), jax.ShapeDtypeStruct((T, 1024), F32),
        jax.ShapeDtypeStruct((T, 512), BF16), jax.ShapeDtypeStruct((T, 512), BF16), jax.ShapeDtypeStruct((8, 1024), F32),
    )
    return pl.pallas_call(
        body, name="mid", grid=(T // tm,), out_shape=out_shape,
        in_specs=[row(512), row(512), row(1024), row(1024), _full(g_fo.shape), _full(g_mo.shape), _full(g2.shape), _full(g3.shape),
                  _full(w_o.shape), _full(w_g.shape), _full(w_u.shape), _full(w_d.shape)],
        out_specs=[row(1024), row(1024), row(D_FF), row(D_FF), row(D_FF), row(1024), row(1024), row(512), row(512),
                   pl.BlockSpec((8, 1024), lambda i: (0, 0))],
        compiler_params=_params(("arbitrary",)),
    )(of, om, x, tgt, g_fo, g_mo, g2, g3, w_o, w_g, w_u, w_d)


def _in_bwd(dqf, dkf, dvf, dcb, dqm, dkm, dvm, lat, qn, kvn, x, dx2, g1, gq, gkv, bfg, ct, st, w_in, w_q12, w_kv, seq, tm):
    T = x.shape[0]
    nblk = T // tm
    nsb = seq // tm

    def body(dqf_ref, dkf_ref, dvf_ref, dcb_ref, dqm_ref, dkm_ref, dvm_ref, lat_ref, qn_ref, kvn_ref, x_ref, dx2_ref, g1_ref, gq_ref,
             gkv_ref, b_ref, ct_ref, st_ref, win_ref, wq_ref, wkv_ref, dx_ref, dproj_ref, gq12_ref, gkv12_ref, stat_ref, carry,
             dq12_ref, dkv_ref):
        i = pl.program_id(0)

        @pl.when(i == 0)
        def _():
            stat_ref[...] = jnp.zeros_like(stat_ref)
            gq12_ref[...] = jnp.zeros_like(gq12_ref)
            gkv12_ref[...] = jnp.zeros_like(gkv12_ref)

        @pl.when(i % nsb == 0)
        def _():
            carry[...] = jnp.zeros_like(carry)

        lane = lax.broadcasted_iota(jnp.int32, (tm, LANES), 1)
        low = lane < 64
        ctv, stv = ct_ref[...], st_ref[...]

        for j in range(PAIRS):
            e, o = 2 * LANES * j, 2 * LANES * j + LANES
            half = lambda ref, c0: jnp.where(low, ref[:, c0:c0 + LANES].astype(F32), 0.0)
            dq = half(dqf_ref, e) + pltpu.roll(half(dqf_ref, o), 64, 1)
            dk = half(dkf_ref, e) + pltpu.roll(half(dkf_ref, o), 64, 1)
            dproj_ref[:, C_FQ + LANES * j:C_FQ + LANES * (j + 1)] = (dq * FOX_SCALE).astype(BF16)
            dproj_ref[:, C_FK + LANES * j:C_FK + LANES * (j + 1)] = dk.astype(BF16)
        dproj_ref[:, C_FV:C_QL] = dvf_ref[...]
        dc = dcb_ref[...]
        rr = lax.broadcasted_iota(jnp.int32, (tm, tm), 0)
        cc = lax.broadcasted_iota(jnp.int32, (tm, tm), 1)
        triu = (cc >= rr).astype(BF16)
        a0, a1, a2 = _split3(dc)
        dlf = _dot(triu, a0) + _dot(triu, a1) + _dot(triu, a2) + carry[0:1, :]
        carry[0:1, :] = dlf[0:1, :]
        misc_a = lat_ref[:, Q_RANK + KV_RANK:Q_RANK + KV_RANK + LANES]
        z = misc_a + b_ref[...]
        dz = jnp.where(lane < HEADS, dlf * jax.nn.sigmoid(-z), 0.0)
        stat_ref[ROW_B:ROW_B + 1, 0:LANES] += jnp.sum(dz, axis=0, keepdims=True)

        cq = (jnp.where(low, 1.0, 0.0) + ctv) * MLA_SCALE
        sq = stv * MLA_SCALE
        dkpe = jnp.zeros((tm, LANES), F32)
        for hd in range(HEADS):
            s0 = LANES * hd
            dqh = dqm_ref[:, s0:s0 + LANES].astype(F32)
            dq12_ref[:, s0:s0 + LANES] = (dqh * cq).astype(BF16)
            dq12_ref[:, 1024 + s0:1024 + s0 + LANES] = (dqh * sq).astype(BF16)
            dkpe = dkpe + dkm_ref[:, s0:s0 + LANES].astype(F32)
        dkv_ref[:, 0:1024] = dkm_ref[...]
        dkv_ref[:, 1024:1536] = dvm_ref[...]
        dproj_ref[:, C_MA:C_END] = (dz + dkpe * ctv + pltpu.roll(dkpe * stv, 32, 1)).astype(BF16)
        dqn = _dot(dq12_ref[...], wq_ref[...])
        dkvn = _dot_nt(dkv_ref[...], wkv_ref[...])
        gq12_ref[...] += _dot_tn(dq12_ref[...], qn_ref[...])
        gkv12_ref[...] += _dot_tn(kvn_ref[...], dkv_ref[...])
        ql = lat_ref[:, 0:Q_RANK]
        kvl = lat_ref[:, Q_RANK:Q_RANK + KV_RANK]
        rq, rkv = _rms(ql, Q_RANK), _rms(kvl, KV_RANK)
        qhat, kvhat = ql * rq, kvl * rkv
        stat_ref[ROW_Q:ROW_Q + 1, 0:Q_RANK] += jnp.sum(dqn * qhat, axis=0, keepdims=True)
        stat_ref[ROW_KV:ROW_KV + 1, 0:KV_RANK] += jnp.sum(dkvn * kvhat, axis=0, keepdims=True)
        dproj_ref[:, C_QL:C_KVL] = _rms_bwd(dqn, qhat, rq, gq_ref[...], Q_RANK).astype(BF16)
        dproj_ref[:, C_KVL:C_MA] = _rms_bwd(dkvn, kvhat, rkv, gkv_ref[...], KV_RANK).astype(BF16)

        dh1 = _dot(dproj_ref[...], win_ref[...])
        xv = x_ref[...]
        r1 = _rms(xv, D_MODEL)
        xh = xv * r1
        stat_ref[ROW_NORM_MIX:ROW_NORM_MIX + 1, :] += jnp.sum(dh1 * xh, axis=0, keepdims=True)
        dx_ref[...] = dx2_ref[...] + _rms_bwd(dh1, xh, r1, g1_ref[...], D_MODEL)

    rev = lambda w: pl.BlockSpec((tm, w), lambda i: (nblk - 1 - i, 0))
    whole = lambda r, c: pl.BlockSpec((r, c), lambda i: (0, 0))
    out_shape = (
        jax.ShapeDtypeStruct((T, 1024), F32), jax.ShapeDtypeStruct((T, C_END), BF16), jax.ShapeDtypeStruct((2048, Q_RANK), F32),
        jax.ShapeDtypeStruct((KV_RANK, 1536), F32), jax.ShapeDtypeStruct((8, 1024), F32),
    )
    return pl.pallas_call(
        body, name="in_bwd", grid=(nblk,), out_shape=out_shape,
        in_specs=[rev(1024), rev(1024), rev(512), rev(LANES), rev(1024), rev(1024), rev(512), rev(512), rev(Q_RANK), rev(KV_RANK),
                  rev(1024), rev(1024), _full(g1.shape), _full(gq.shape), _full(gkv.shape), _full(bfg.shape), rev(LANES), rev(LANES),
                  _full(w_in.shape), _full(w_q12.shape), _full(w_kv.shape)],
        out_specs=[rev(1024), rev(C_END), whole(2048, Q_RANK), whole(KV_RANK, 1536), whole(8, 1024)],
        scratch_shapes=[pltpu.VMEM((8, LANES), F32), pltpu.VMEM((tm, 2048), BF16), pltpu.VMEM((tm, 1536), BF16)],
        compiler_params=_params(("arbitrary",)),
    )(dqf, dkf, dvf, dcb, dqm, dkm, dvm, lat, qn, kvn, x, dx2, g1, gq, gkv, bfg, ct, st, w_in, w_q12, w_kv)


def _wgrad(a, b, tk, tt, name, ex=None):
    T, K = a.shape
    N = b.shape[1]
    n_in, n_out = (len(ex.ins), len(ex.outs)) if ex else (0, 0)
    gk, gt = K // tk, T // tt

    def body(*refs):
        a_ref, b_ref, o_ref = refs[0], refs[1], refs[2 + n_in]
        kb, t = pl.program_id(0), pl.program_id(1)
        if ex:
            ex_refs = (refs[2:2 + n_in], refs[3 + n_in:3 + n_in + n_out], refs[3 + n_in + n_out:])

            @pl.when((kb == 0) & (t == 0))
            def _():
                ex.start(*ex_refs)

        @pl.when(t == 0)
        def _():
            o_ref[...] = jnp.zeros_like(o_ref)

        o_ref[...] += _dot_tn(a_ref[...].astype(BF16), b_ref[...].astype(BF16))

        if ex:
            @pl.when((kb == gk - 1) & (t == gt - 1))
            def _():
                ex.wait(*ex_refs)

    res = pl.pallas_call(
        body, name=name, grid=(gk, gt), out_shape=(jax.ShapeDtypeStruct((K, N), F32),) + tuple(ex.outs if ex else ()),
        in_specs=[pl.BlockSpec((tt, tk), lambda kb, t: (t, kb)), pl.BlockSpec((tt, N), lambda kb, t: (t, 0))] + [ANY] * n_in,
        out_specs=[pl.BlockSpec((tk, N), lambda kb, t: (kb, 0))] + [ANY] * n_out,
        scratch_shapes=ex.sems() if ex else [], input_output_aliases=ex.aliases(2, 1) if ex else {},
        compiler_params=_params(("arbitrary", "arbitrary")),
    )(a, b, *(ex.ins if ex else ()))
    return (res[0], list(res[1:])) if ex else res[0]


def _adam_update(w, g, m, v):
    nm = ADAM_B1 * m + (1.0 - ADAM_B1) * g
    nv = ADAM_B2 * v + (1.0 - ADAM_B2) * (g * g)
    m_hat = nm / (1.0 - ADAM_B1 ** ADAM_STEP)
    v_hat = nv / (1.0 - ADAM_B2 ** ADAM_STEP)
    return -ADAM_LR * (m_hat / (jnp.sqrt(v_hat) + ADAM_EPS) + ADAM_WD * w), nm, nv


def _adamw_small(stats, params):
    k = len(SMALL)

    def body(*refs):
        for t, name in enumerate(SMALL):
            row, c0, width = SMALL_AT[name]
            w_ref, m_ref, v_ref = refs[1 + 3 * t:4 + 3 * t]
            g_ref, d_ref, nm_ref, nv_ref = refs[1 + 3 * k + 4 * t:5 + 3 * k + 4 * t]
            g = refs[0][row:row + 1, c0:c0 + width]
            g_ref[...] = g
            d_ref[...], nm_ref[...], nv_ref[...] = _adam_update(w_ref[...], g, m_ref[...], v_ref[...])

    vm = pl.BlockSpec(memory_space=pltpu.VMEM)
    out_shape = tuple(jax.ShapeDtypeStruct((1, SMALL_AT[name][2]), F32) for name in SMALL for _ in range(4))
    res = pl.pallas_call(body, name="adamw_small", out_shape=out_shape, in_specs=[vm] * (1 + 3 * k), out_specs=tuple([vm] * (4 * k)))(
        stats, *[a for name in SMALL for a in params[name]])
    return {name: tuple(res[4 * t:4 * t + 4]) for t, name in enumerate(SMALL)}


def _adamw(tensors, name):
    n = len(tensors)
    R, C = tensors[0][0].shape
    tr = _row_tile(R)

    def body(*refs):
        for t in range(n):
            w_ref, g_ref, m_ref, v_ref = refs[4 * t:4 * t + 4]
            d_ref, nm_ref, nv_ref = refs[4 * n + 3 * t:4 * n + 3 * t + 3]
            d_ref[...], nm_ref[...], nv_ref[...] = _adam_update(w_ref[...], g_ref[...], m_ref[...], v_ref[...])

    blk = pl.BlockSpec((tr, C), lambda i: (i, 0))
    sh = jax.ShapeDtypeStruct((R, C), F32)
    res = pl.pallas_call(
        body, name=name, grid=(R // tr,), out_shape=(sh,) * (3 * n),
        in_specs=[blk] * (4 * n), out_specs=[blk] * (3 * n),
        compiler_params=_params(("arbitrary",)),
    )(*[a for t in tensors for a in t])
    return [tuple(res[3 * t:3 * t + 3]) for t in range(n)]


def _arrange(win_t, wuq_t, wukv):
    dt = win_t.dtype
    z = lambda r: jnp.zeros((r, D_MODEL), dt)
    zh = lambda r: jnp.zeros((HEADS, r, Q_RANK), dt)
    kr1, kr2 = win_t[1928:1944], win_t[1944:1960]
    misc = jnp.concatenate([win_t[1536:1544], z(56), kr1, kr2, kr2, kr1], axis=0)
    w_in = jnp.concatenate([win_t[0:1536], win_t[1544:1928], misc], axis=0)
    wq = wuq_t.reshape(HEADS, 96, Q_RANK)
    q1 = jnp.concatenate([wq, zh(32)], axis=1).reshape(1024, Q_RANK)
    q2 = jnp.concatenate([zh(64), wq[:, 80:96], wq[:, 64:80], zh(32)], axis=1).reshape(1024, Q_RANK)
    wkv = wukv.reshape(KV_RANK, HEADS, 128)
    wk = jnp.concatenate([wkv[:, :, 0:64], jnp.zeros((KV_RANK, HEADS, 64), dt)], axis=2).reshape(KV_RANK, 1024)
    wv = wkv[:, :, 64:128].reshape(KV_RANK, 512)
    return dict(w_in=w_in, w_q12=jnp.concatenate([q1, q2], axis=0), w_k=wk, w_v=wv, w_kv=jnp.concatenate([wk, wv], axis=1))


def _unarrange(g_in, g_q12, g_kv):
    kr1 = g_in[C_MA + 64:C_MA + 80] + g_in[C_MA + 112:C_MA + 128]
    kr2 = g_in[C_MA + 80:C_MA + 96] + g_in[C_MA + 96:C_MA + 112]
    win_t = jnp.concatenate([g_in[0:1536], g_in[C_MA:C_MA + 8], g_in[1536:1920], kr1, kr2], axis=0)
    g1 = g_q12[0:1024].reshape(HEADS, 128, Q_RANK)
    g2 = g_q12[1024:2048].reshape(HEADS, 128, Q_RANK)
    wuq_t = jnp.concatenate([g1[:, 0:64], g1[:, 64:80] + g2[:, 80:96], g1[:, 80:96] + g2[:, 64:80]], axis=1).reshape(768, Q_RANK)
    gk = g_kv[:, 0:1024].reshape(KV_RANK, HEADS, 128)
    gv = g_kv[:, 1024:1536].reshape(KV_RANK, HEADS, 64)
    wukv = jnp.concatenate([gk[:, :, 0:64], gv], axis=2).reshape(KV_RANK, 1024)
    return win_t, wuq_t, wukv


def _selector():
    sel = np.zeros((384, 1024), np.float32)
    for h in range(HEADS):
        for piece in range(3):
            sel[LANES * piece + h, LANES * h + 64 + piece] = 1.0
    return jnp.asarray(sel, BF16)


def _rope_tables(positions):
    inv_freq = 10000.0 ** (-jnp.arange(0, ROPE, 2, dtype=F32) / ROPE)
    n = positions.size
    ang = (positions.reshape(n // 8, 8, 1).astype(F32) * inv_freq[None, None, :]).reshape(n // 8, 8 * (ROPE // 2))
    cos, sin = lax.optimization_barrier((jnp.cos(lax.optimization_barrier(ang)), jnp.sin(lax.optimization_barrier(ang))))
    cos, sin = cos.reshape(n, ROPE // 2), sin.reshape(n, ROPE // 2)
    z64, z32 = jnp.zeros((n, 64), F32), jnp.zeros((n, 32), F32)
    return jnp.concatenate([z64, cos, cos, z32], axis=1), jnp.concatenate([z64, -sin, sin, z32], axis=1)


def _work(name, t):
    return jnp.swapaxes(t[0], 0, 1) if name in TRANSPOSED else t[0]


def _back(name, t):
    return (jnp.swapaxes(t, 0, 1) if name in TRANSPOSED else t)[None]


def kernel(x, positions, norm_mix_g, w_in, b_fgate, q_norm_g, w_uq, kv_norm_g, w_ukv, fox_out_g, mla_out_g, w_o, norm_ffn_g, w_gate, w_up, w_down, final_norm_g, loss_target, m_norm_mix_g, m_w_in, m_b_fgate, m_q_norm_g, m_w_uq, m_kv_norm_g, m_w_ukv, m_fox_out_g, m_mla_out_g, m_w_o, m_norm_ffn_g, m_w_gate, m_w_up, m_w_down, m_final_norm_g, v_norm_mix_g, v_w_in, v_b_fgate, v_q_norm_g, v_w_uq, v_kv_norm_g, v_w_ukv, v_fox_out_g, v_mla_out_g, v_w_o, v_norm_ffn_g, v_w_gate, v_w_up, v_w_down, v_final_norm_g):
    names = ["norm_mix_g", "w_in", "b_fgate", "q_norm_g", "w_uq", "kv_norm_g", "w_ukv", "fox_out_g", "mla_out_g", "w_o",
             "norm_ffn_g", "w_gate", "w_up", "w_down", "final_norm_g"]
    wts = dict(zip(names, [norm_mix_g, w_in, b_fgate, q_norm_g, w_uq, kv_norm_g, w_ukv, fox_out_g, mla_out_g, w_o, norm_ffn_g,
                           w_gate, w_up, w_down, final_norm_g]))
    mom = dict(zip(names, [m_norm_mix_g, m_w_in, m_b_fgate, m_q_norm_g, m_w_uq, m_kv_norm_g, m_w_ukv, m_fox_out_g, m_mla_out_g,
                           m_w_o, m_norm_ffn_g, m_w_gate, m_w_up, m_w_down, m_final_norm_g]))
    var = dict(zip(names, [v_norm_mix_g, v_w_in, v_b_fgate, v_q_norm_g, v_w_uq, v_kv_norm_g, v_w_ukv, v_fox_out_g, v_mla_out_g,
                           v_w_o, v_norm_ffn_g, v_w_gate, v_w_up, v_w_down, v_final_norm_g]))
    shard = {n: _work(n, wts[n]) for n in HEAD3 + FFN4}
    nb, seq, _ = x.shape
    T = nb * seq
    tm, tq = min(ROW_TILE, seq), min(ATTN_TILE, seq)
    tt = min(WGRAD_TILE, T)
    xf = x.reshape(T, D_MODEL)
    tgt = loss_target.reshape(T, D_MODEL)

    mine = [shard[n].astype(BF16) for n in HEAD3]
    win4, wuq4, wukv4 = _run_exchange(_gather_split_exchange(mine), "gather_head")
    a = _arrange(win4.reshape(-1, D_MODEL), wuq4.reshape(-1, Q_RANK), wukv4.transpose(1, 0, 2).reshape(KV_RANK, -1))
    sel = _selector()
    ct, st = _rope_tables(positions)
    bfg = jnp.concatenate([b_fgate, jnp.zeros((1, LANES - HEADS), F32)], axis=1)
    g1, gq, gkv = norm_mix_g, q_norm_g, kv_norm_g

    h1, qf, kf, vf, qm, km, vm, lat, qn, kvn = _in_proj(xf, g1, a["w_in"], a["w_q12"], a["w_k"], a["w_v"], gq, gkv, bfg, ct, st, sel, seq,
                                                        min(IN_PROJ_TILE, seq))
    tqf = min(ATTN_FWD_TILE, seq)
    of, lse_f, (wo4, wg4) = _attn_fwd(qf, kf, vf, nb, seq, tqf, "fox_fwd", _gather_exchange([shard[n].astype(BF16) for n in FFN4[:2]]))
    om, lse_m, (wu4, wd4) = _attn_fwd(qm, km, vm, nb, seq, tqf, "mla_fwd", _gather_exchange([shard[n].astype(BF16) for n in FFN4[2:]]))
    a_cat, h2, hid, dg, du, dx3, dx2, dof, dom, st_mid = _mid(
        of, om, xf, tgt, fox_out_g, mla_out_g, norm_ffn_g, final_norm_g.reshape(1, D_MODEL),
        wo4.reshape(D_MODEL, D_MODEL), wg4.reshape(D_FF, D_MODEL), wu4.reshape(D_FF, D_MODEL), wd4.reshape(D_FF, D_MODEL), tm)

    slab = lambda g: g.reshape(N_CHIPS, g.shape[0] // N_CHIPS, g.shape[1])
    big = [slab(_wgrad(a_cat, dx2, D_MODEL, tt, "wgrad_o")), slab(_wgrad(dg, h2, D_FF // 2, tt, "wgrad_gate")),
           slab(_wgrad(du, h2, D_FF // 2, tt, "wgrad_up")), slab(_wgrad(hid, dx3, D_FF // 2, tt, "wgrad_down"))]
    (dqf, dkf, dvf, dcb), got = _attn_bwd(qf, kf, vf, of, dof, lse_f, nb, seq, tq, "fox_bwd", True, _swap_exchange(big))
    sums = _add_half(big, got)
    (dqm, dkm, dvm), recv = _attn_bwd(qm, km, vm, om, dom, lse_m, nb, seq, tq, "mla_bwd", False, _scatter_exchange(sums))
    halves = _sum_slabs(big, got, recv)
    dx, dproj, g_q12, g_kv, st_in = _in_bwd(dqf, dkf, dvf, dcb, dqm, dkm, dvm, lat, qn, kvn, xf, dx2, g1, gq, gkv, bfg, ct, st,
                                            a["w_in"], a["w_q12"], a["w_kv"], seq, tm)
    g_in, results = _wgrad(dproj, h1, C_END, tt, "wgrad_in", _both(_join_exchange(halves), _everyone_exchange(st_mid + st_in)))
    gshard = dict(zip(FFN4, results[:4]))
    stats = _sum_devices(results[4])

    gwin_t, gwuq_t, gwukv = _unarrange(g_in, g_q12, g_kv)
    tail = [slab(gwin_t), slab(gwuq_t), gwukv.reshape(KV_RANK, N_CHIPS, -1).transpose(1, 0, 2)]
    tail_got = _run_exchange(_swap_exchange(tail), "tail_swap")
    tail_sums = _add_half(tail, tail_got)
    tail_recv = _run_exchange(_scatter_exchange(tail_sums), "tail_scatter")
    tail_joined = _run_exchange(_join_exchange(_sum_slabs(tail, tail_got, tail_recv)), "tail_join")
    gshard.update(zip(HEAD3, tail_joined))
    quad = lambda n: (shard[n], gshard[n], _work(n, mom[n]), _work(n, var[n]))
    updates = dict(zip(FFN4[1:], _adamw([quad(n) for n in FFN4[1:]], "adamw_ffn")))
    for n in HEAD3 + FFN4[:1]:
        updates[n], = _adamw([quad(n)], "adamw_" + n)

    grads, delta, new_m, new_v = {}, {}, {}, {}
    for n in HEAD3 + FFN4:
        grads[n] = _back(n, gshard[n])
        delta[n], new_m[n], new_v[n] = [_back(n, t) for t in updates[n]]
    row = lambda t: t.reshape(1, -1)
    small = _adamw_small(stats, {n: (row(wts[n]), row(mom[n]), row(var[n])) for n in SMALL})
    for n in SMALL:
        grads[n], delta[n], new_m[n], new_v[n] = [t.reshape(wts[n].shape) for t in small[n]]
    loss = jnp.sum(stats[ROW_LOSS])
    return (loss, dx.reshape(x.shape), *[grads[n] for n in names], *[delta[n] for n in names],
            *[new_m[n] for n in names], *[new_v[n] for n in names])
```

```python
import functools

import numpy as np
import jax
import jax.numpy as jnp
from jax import lax
from jax.experimental import pallas as pl
from jax.experimental.pallas import tpu as pltpu

F32 = jnp.float32
BF16 = jnp.bfloat16
MESH = pl.DeviceIdType.MESH

EPS = 1e-6
D_MODEL = 1024
HEADS = 8
PAIRS = HEADS // 2
FOX_W = 512
Q_RANK = 256
KV_RANK = 128
ROPE = 32
D_FF = 2816
N_CHIPS = 4
FOX_SCALE = 64 ** -0.5
MLA_SCALE = 96 ** -0.5
LANES = 128
NEG = -1e30

ADAM_LR, ADAM_B1, ADAM_B2, ADAM_EPS, ADAM_WD, ADAM_STEP = 0.001, 0.9, 0.999, 1e-08, 0.01, 10

C_FQ, C_FK, C_FV, C_QL, C_KVL, C_MA, C_END = 0, 512, 1024, 1536, 1792, 1920, 2048
C_MB = C_END

VMEM_LIMIT = 60 * 1024 * 1024
ROW_TILE = 256
IN_PROJ_TILE = 512
ATTN_TILE = 512
ATTN_FWD_TILE = 1024
WGRAD_TILE = 2048

HEAD3 = ("w_in", "w_uq", "w_ukv")
FFN4 = ("w_o", "w_gate", "w_up", "w_down")
TRANSPOSED = ("w_in", "w_uq", "w_gate", "w_up")
SMALL = ("norm_mix_g", "b_fgate", "q_norm_g", "kv_norm_g", "fox_out_g", "mla_out_g", "norm_ffn_g", "final_norm_g")
ROW_NORM_MIX, ROW_NORM_FFN, ROW_FINAL, ROW_OUT, ROW_Q, ROW_KV, ROW_B, ROW_LOSS = range(8)
SMALL_AT = {"norm_mix_g": (ROW_NORM_MIX, 0, 1024), "norm_ffn_g": (ROW_NORM_FFN, 0, 1024), "final_norm_g": (ROW_FINAL, 0, 1024),
            "fox_out_g": (ROW_OUT, 0, 512), "mla_out_g": (ROW_OUT, 512, 512), "q_norm_g": (ROW_Q, 0, 256),
            "kv_norm_g": (ROW_KV, 0, 128), "b_fgate": (ROW_B, 0, 8)}


def _params(sem=None):
    return pltpu.CompilerParams(dimension_semantics=sem, vmem_limit_bytes=VMEM_LIMIT)


def _full(shape):
    n = len(shape)
    return pl.BlockSpec(shape, lambda *_: (0,) * n, pipeline_mode=pl.Buffered(1))


def _dot(a, b):
    return jnp.dot(a, b, preferred_element_type=F32)


def _dot_nt(a, b):
    return lax.dot_general(a, b, (((1,), (1,)), ((), ())), preferred_element_type=F32)


def _dot_tn(a, b):
    return lax.dot_general(a, b, (((0,), (0,)), ((), ())), preferred_element_type=F32)


def _split3(v):
    hi = v.astype(BF16)
    r1 = v - hi.astype(F32)
    mid = r1.astype(BF16)
    lo = (r1 - mid.astype(F32)).astype(BF16)
    return hi, mid, lo


def _rms(v, width):
    return lax.rsqrt(jnp.sum(v * v, axis=1, keepdims=True) * (1.0 / width) + EPS)


def _rms_bwd(dy, xhat, r, g, width):
    u = dy * g
    return r * (u - xhat * (jnp.sum(u * xhat, axis=1, keepdims=True) * (1.0 / width)))


ANY = pl.BlockSpec(memory_space=pl.ANY)


def _place():
    return lax.axis_index("x"), lax.axis_index("y"), lax.axis_index("c")


def _other_chips(x, y):
    return [(1 - x, y), (x, 1 - y), (1 - x, 1 - y)]


def _remote(src, dst, send, recv, j, dev):
    return pltpu.make_async_remote_copy(src_ref=src, dst_ref=dst, send_sem=send.at[j], recv_sem=recv.at[j], device_id=dev, device_id_type=MESH)


class _Exchange:
    def __init__(self, ins, outs, n_remote, n_local, build, in_place=False):
        self.ins, self.outs, self.n_remote, self.n_local, self.build = list(ins), list(outs), n_remote, max(n_local, 1), build
        self.in_place = in_place
        self.n_aliased = len(self.ins)

    def aliases(self, first_in, first_out):
        return {first_in + i: first_out + i for i in range(self.n_aliased)} if self.in_place else {}

    def sems(self):
        return [pltpu.SemaphoreType.DMA((self.n_remote,)), pltpu.SemaphoreType.DMA((self.n_remote,)), pltpu.SemaphoreType.DMA((self.n_local,))]

    def start(self, in_refs, out_refs, sems):
        for cp in self.build(in_refs, out_refs, *sems)[0]:
            cp.start()

    def wait(self, in_refs, out_refs, sems):
        for w in self.build(in_refs, out_refs, *sems)[1]:
            w()


def _gather_exchange(shards):
    def build(ins, outs, send, recv, lsem):
        x, y, c = _place()
        starts, waits = [], []
        for i, (s, o) in enumerate(zip(ins, outs)):
            mine = pltpu.make_async_copy(s, o.at[2 * x + y], lsem.at[i])
            starts.append(mine)
            waits.append(mine.wait)
            for j, (cx, cy) in enumerate(_other_chips(x, y)):
                out = _remote(s, o.at[2 * x + y], send, recv, 3 * i + j, (cx, cy, c))
                starts.append(out)
                waits.append(_remote(s, o.at[2 * cx + cy], send, recv, 3 * i + j, (cx, cy, c)).wait_recv)
                waits.append(out.wait_send)
        return starts, waits

    outs = [jax.ShapeDtypeStruct((N_CHIPS,) + s.shape, s.dtype) for s in shards]
    return _Exchange(shards, outs, 3 * len(shards), len(shards), build)


def _gather_split_exchange(shards):
    n = len(shards)

    def build(ins, outs, send, recv, lsem):
        x, y, c = _place()
        starts, waits, last = [], [], []
        for i, (s, o) in enumerate(zip(ins, outs)):
            hc = s.shape[1] // 2
            mine, other = pl.ds(c * hc, hc), pl.ds((1 - c) * hc, hc)
            for j, (cx, cy) in enumerate(_other_chips(x, y)):
                out = _remote(s.at[:, mine], o.at[2 * x + y, :, mine], send, recv, 3 * i + j, (cx, cy, c))
                landed = o.at[2 * cx + cy, :, mine]
                arrive = _remote(s.at[:, mine], landed, send, recv, 3 * i + j, (cx, cy, c))
                onward = _remote(landed, landed, send, recv, 3 * n + 3 * i + j, (x, y, 1 - c))
                from_sibling = _remote(landed, o.at[2 * cx + cy, :, other], send, recv, 3 * n + 3 * i + j, (x, y, 1 - c))
                starts.append(out)
                waits.append(lambda arrive=arrive, onward=onward: (arrive.wait_recv(), onward.start()))
                last += [from_sibling.wait_recv, onward.wait_send, out.wait_send]
        return starts, waits + last

    outs = [jax.ShapeDtypeStruct((N_CHIPS,) + s.shape, s.dtype) for s in shards]
    return _Exchange(shards, outs, 6 * n, 0, build)


def _swap_exchange(grads):
    def build(ins, outs, send, recv, lsem):
        x, y, c = _place()
        cps = []
        for i, (g, o) in enumerate(zip(ins, outs)):
            hc = g.shape[2] // 2
            cps.append(_remote(g.at[:, :, pl.ds((1 - c) * hc, hc)], o, send, recv, i, (x, y, 1 - c)))
        return cps, [cp.wait for cp in cps]

    outs = [jax.ShapeDtypeStruct((g.shape[0], g.shape[1], g.shape[2] // 2), g.dtype) for g in grads]
    return _Exchange(grads, outs, len(grads), 0, build)


def _scatter_exchange(sums):
    def build(ins, outs, send, recv, lsem):
        x, y, c = _place()
        cps = []
        for i, (s, o) in enumerate(zip(ins, outs)):
            for j, (cx, cy) in enumerate(_other_chips(x, y)):
                cps.append(_remote(s.at[2 * cx + cy], o.at[j], send, recv, 3 * i + j, (cx, cy, c)))
        return cps, [cp.wait for cp in cps]

    outs = [jax.ShapeDtypeStruct((3,) + s.shape[1:], s.dtype) for s in sums]
    return _Exchange(sums, outs, 3 * len(sums), 0, build)


def _join_exchange(bufs):
    def build(ins, outs, send, recv, lsem):
        x, y, c = _place()
        starts, waits = [], []
        for i, (t, o) in enumerate(zip(ins, outs)):
            hc = t.shape[1] // 2
            out = _remote(t.at[:, pl.ds(c * hc, hc)], o.at[:, pl.ds(c * hc, hc)], send, recv, i, (x, y, 1 - c))
            starts.append(out)
            waits += [_remote(t.at[:, pl.ds(c * hc, hc)], o.at[:, pl.ds((1 - c) * hc, hc)], send, recv, i, (x, y, 1 - c)).wait_recv,
                      out.wait_send]
        return starts, waits

    outs = [jax.ShapeDtypeStruct(t.shape, t.dtype) for t in bufs]
    return _Exchange(bufs, outs, len(bufs), 0, build, in_place=True)


def _everyone_exchange(v):
    def build(ins, outs, send, recv, lsem):
        x, y, c = _place()
        me = 4 * x + 2 * y + c
        mine = pltpu.make_async_copy(ins[0], outs[0].at[me], lsem.at[0])
        starts, waits = [mine], [mine.wait]
        for j in range(7):
            fx, fy, fc = (j + 1) >> 2 & 1, (j + 1) >> 1 & 1, (j + 1) & 1
            peer = (x ^ fx, y ^ fy, c ^ fc)
            out = _remote(ins[0], outs[0].at[me], send, recv, j, peer)
            starts.append(out)
            waits += [_remote(ins[0], outs[0].at[4 * peer[0] + 2 * peer[1] + peer[2]], send, recv, j, peer).wait_recv, out.wait_send]
        return starts, waits

    return _Exchange([v], [jax.ShapeDtypeStruct((8,) + v.shape, v.dtype)], 7, 1, build)


def _both(a, b):
    na_in, na_out = len(a.ins), len(a.outs)

    def build(ins, outs, send, recv, lsem):
        sa, wa = a.build(ins[:na_in], outs[:na_out], send.at[pl.ds(0, a.n_remote)], recv.at[pl.ds(0, a.n_remote)],
                         lsem.at[pl.ds(0, a.n_local)])
        sb, wb = b.build(ins[na_in:], outs[na_out:], send.at[pl.ds(a.n_remote, b.n_remote)], recv.at[pl.ds(a.n_remote, b.n_remote)],
                         lsem.at[pl.ds(a.n_local, b.n_local)])
        return sa + sb, wa + wb

    both = _Exchange(a.ins + b.ins, a.outs + b.outs, a.n_remote + b.n_remote, a.n_local + b.n_local, build, in_place=a.in_place)
    both.n_aliased = na_in
    return both


def _run_exchange(ex, name):
    n_in, n_out = len(ex.ins), len(ex.outs)

    def body(*refs):
        ins, outs, sems = refs[:n_in], refs[n_in:n_in + n_out], refs[n_in + n_out:]
        ex.start(ins, outs, sems)
        ex.wait(ins, outs, sems)

    return pl.pallas_call(
        body, name=name, out_shape=tuple(ex.outs), in_specs=[ANY] * n_in, out_specs=tuple([ANY] * n_out),
        scratch_shapes=ex.sems(), input_output_aliases=ex.aliases(0, 0),
        compiler_params=pltpu.CompilerParams(has_side_effects=True),
    )(*ex.ins)


def _sum_devices(rows):
    def body(r_ref, o_ref):
        acc = r_ref[0]
        for d in range(1, 8):
            acc = acc + r_ref[d]
        o_ref[...] = acc

    vm = pl.BlockSpec(memory_space=pltpu.VMEM)
    return pl.pallas_call(body, name="sum_devices", out_shape=jax.ShapeDtypeStruct(rows.shape[1:], rows.dtype),
                          in_specs=[vm], out_specs=vm)(rows)


def _add_half(gs, gots):
    n = len(gs)

    def body(c_ref, *refs):
        for t in range(n):
            refs[2 * n + t][...] = (refs[2 * t][...] + refs[2 * t + 1][...]).astype(BF16)

    blk = lambda g: (1, g.shape[1], g.shape[2] // 2)
    in_specs = [s for g in gs for s in (pl.BlockSpec(blk(g), lambda k, c_ref: (k, 0, c_ref[0])),
                                        pl.BlockSpec(blk(g), lambda k, c_ref: (k, 0, 0)))]
    res = pl.pallas_call(
        body, name="add_half",
        grid_spec=pltpu.PrefetchScalarGridSpec(
            num_scalar_prefetch=1, grid=(N_CHIPS,), in_specs=in_specs,
            out_specs=[pl.BlockSpec(blk(g), lambda k, c_ref: (k, 0, 0)) for g in gs]),
        out_shape=[jax.ShapeDtypeStruct((N_CHIPS,) + blk(g)[1:], BF16) for g in gs],
        compiler_params=_params(("arbitrary",)),
    )(jnp.reshape(lax.axis_index("c"), (1,)).astype(jnp.int32), *[a for pair in zip(gs, gots) for a in pair])
    return list(res)


def _sum_slabs(gs, gots, recvs):
    n = len(gs)

    def body(kc_ref, *refs):
        for t in range(n):
            g_ref, s_ref, r_ref = refs[3 * t:3 * t + 3]
            refs[3 * n + t][...] = (((g_ref[0] + s_ref[0]) + r_ref[0].astype(F32)) + r_ref[1].astype(F32)) + r_ref[2].astype(F32)

    half = lambda g: (g.shape[1], g.shape[2] // 2)
    in_specs = [s for g in gs for s in (pl.BlockSpec((1,) + half(g), lambda i, kc_ref: (kc_ref[0], 0, kc_ref[1])),
                                        pl.BlockSpec((1,) + half(g), lambda i, kc_ref: (kc_ref[0], 0, 0)),
                                        pl.BlockSpec((3,) + half(g), lambda i, kc_ref: (0, 0, 0)))]
    kc = jnp.stack([2 * lax.axis_index("x") + lax.axis_index("y"), lax.axis_index("c")]).astype(jnp.int32)
    res = pl.pallas_call(
        body, name="sum_slabs",
        grid_spec=pltpu.PrefetchScalarGridSpec(
            num_scalar_prefetch=1, grid=(1,), in_specs=in_specs,
            out_specs=[pl.BlockSpec(half(g), lambda i, kc_ref: (0, kc_ref[1])) for g in gs]),
        out_shape=[jax.ShapeDtypeStruct(g.shape[1:], F32) for g in gs],
        compiler_params=_params(("arbitrary",)),
    )(kc, *[a for trio in zip(gs, gots, recvs) for a in trio])
    return list(res)


def _row_tile(rows):
    for cand in (256, 184, 176, 144, 128, 64, 32, 16, 8):
        if rows % cand == 0:
            return cand
    return rows


def _in_proj(x, g1, w_in, w_q12, w_k, w_v, gq, gkv, bfg, ct, st, sel, seq, tm):
    T = x.shape[0]
    nsb = seq // tm

    def body(x_ref, g1_ref, win_ref, wq_ref, wk_ref, wv_ref, gq_ref, gkv_ref, b_ref, ct_ref, st_ref, sel_ref,
             h1_ref, qf_ref, kf_ref, vf_ref, qm_ref, km_ref, vm_ref, lat_ref, qn_ref, kvn_ref, carry):
        i = pl.program_id(0)

        @pl.when(i % nsb == 0)
        def _():
            carry[...] = jnp.zeros_like(carry)

        xv = x_ref[...]
        h = (xv * _rms(xv, D_MODEL) * g1_ref[...]).astype(BF16)
        h1_ref[...] = h
        proj = _dot_nt(h, win_ref[...])
        lane = lax.broadcasted_iota(jnp.int32, (tm, LANES), 1)
        low = lane < 64
        misc_a = proj[:, C_MA:C_END]
        misc_b = pltpu.roll(misc_a, 96, 1)

        z = misc_a + b_ref[...]
        lf = jnp.where(lane < HEADS, jnp.minimum(z, 0.0) - jnp.log1p(jnp.exp(-jnp.abs(z))), 0.0)
        rr = lax.broadcasted_iota(jnp.int32, (tm, tm), 0)
        cc = lax.broadcasted_iota(jnp.int32, (tm, tm), 1)
        tri = (rr >= cc).astype(BF16)
        a0, a1, a2 = _split3(lf)
        c = _dot(tri, a0) + _dot(tri, a1) + _dot(tri, a2) + carry[0:1, :]
        carry[0:1, :] = c[tm - 1:tm, :]
        c0, c1, c2 = _split3(c)
        cpl = _dot(jnp.concatenate([c0, c1, c2], axis=1), sel_ref[...])
        qpad = jnp.where((lane >= 64) & (lane < 67), -1.0, 0.0)
        for j in range(PAIRS):
            qc = proj[:, C_FQ + LANES * j:C_FQ + LANES * (j + 1)] * FOX_SCALE
            kc = proj[:, C_FK + LANES * j:C_FK + LANES * (j + 1)]
            e, o = 2 * LANES * j, 2 * LANES * j + LANES
            qf_ref[:, e:e + LANES] = jnp.where(low, qc, qpad).astype(BF16)
            qf_ref[:, o:o + LANES] = jnp.where(low, pltpu.roll(qc, 64, 1), qpad).astype(BF16)
            kf_ref[:, e:e + LANES] = jnp.where(low, kc, cpl[:, e:e + LANES]).astype(BF16)
            kf_ref[:, o:o + LANES] = jnp.where(low, pltpu.roll(kc, 64, 1), cpl[:, o:o + LANES]).astype(BF16)
        vf_ref[...] = proj[:, C_FV:C_QL].astype(BF16)

        ql = proj[:, C_QL:C_KVL]
        kvl = proj[:, C_KVL:C_MA]
        qn = (ql * _rms(ql, Q_RANK) * gq_ref[...]).astype(BF16)
        kvn = (kvl * _rms(kvl, KV_RANK) * gkv_ref[...]).astype(BF16)
        lat_ref[...] = proj[:, C_QL:C_MB]
        qn_ref[...] = qn
        kvn_ref[...] = kvn
        q12 = _dot_nt(qn, wq_ref[...])
        kn = _dot(kvn, wk_ref[...])
        ctv = ct_ref[...]
        stv = st_ref[...]
        cq = (jnp.where(low, 1.0, 0.0) + ctv) * MLA_SCALE
        sq = stv * MLA_SCALE
        kpe = misc_a * ctv + misc_b * stv
        for hd in range(HEADS):
            s0 = LANES * hd
            qm_ref[:, s0:s0 + LANES] = (q12[:, s0:s0 + LANES] * cq + q12[:, 1024 + s0:1024 + s0 + LANES] * sq).astype(BF16)
            km_ref[:, s0:s0 + LANES] = (kn[:, s0:s0 + LANES] + kpe).astype(BF16)
        vm_ref[...] = _dot(kvn, wv_ref[...]).astype(BF16)

    row = lambda w: pl.BlockSpec((tm, w), lambda i: (i, 0))
    out_shape = (
        jax.ShapeDtypeStruct((T, D_MODEL), BF16),
        jax.ShapeDtypeStruct((T, 1024), BF16), jax.ShapeDtypeStruct((T, 1024), BF16), jax.ShapeDtypeStruct((T, 512), BF16),
        jax.ShapeDtypeStruct((T, 1024), BF16), jax.ShapeDtypeStruct((T, 1024), BF16), jax.ShapeDtypeStruct((T, 512), BF16),
        jax.ShapeDtypeStruct((T, 512), F32),
        jax.ShapeDtypeStruct((T, Q_RANK), BF16), jax.ShapeDtypeStruct((T, KV_RANK), BF16),
    )
    return pl.pallas_call(
        body, name="in_proj", grid=(T // tm,), out_shape=out_shape,
        in_specs=[row(D_MODEL), _full(g1.shape), _full(w_in.shape), _full(w_q12.shape), _full(w_k.shape), _full(w_v.shape),
                  _full(gq.shape), _full(gkv.shape), _full(bfg.shape), row(LANES), row(LANES), _full(sel.shape)],
        out_specs=[row(D_MODEL), row(1024), row(1024), row(512), row(1024), row(1024), row(512), row(512), row(Q_RANK), row(KV_RANK)],
        scratch_shapes=[pltpu.VMEM((8, LANES), F32)],
        compiler_params=_params(("arbitrary",)),
    )(x, g1, w_in, w_q12, w_k, w_v, gq, gkv, bfg, ct, st, sel)


def _attn_fwd(q, k, v, nb, seq, tq, name, ex=None):
    T = q.shape[0]
    nq = seq // tq
    n_in, n_out = (len(ex.ins), len(ex.outs)) if ex else (0, 0)

    def body(*refs):
        q_ref, k_ref, v_ref = refs[0:3]
        o_ref, lse_ref = refs[3 + n_in:5 + n_in]
        b, pr, qi = pl.program_id(0), pl.program_id(1), pl.program_id(2)
        if ex:
            ex_refs = (refs[3:3 + n_in], refs[5 + n_in:5 + n_in + n_out], refs[8 + n_in + n_out:])

            @pl.when((b == 0) & (pr == 0) & (qi == 0))
            def _():
                ex.start(*ex_refs)

        s_sc, p_sc, acc_sc = refs[5 + n_in + n_out:8 + n_in + n_out]
        strip = 64
        key_s = lax.broadcasted_iota(jnp.int32, (strip, tq), 0)
        qry_s = lax.broadcasted_iota(jnp.int32, (strip, tq), 1)
        row_t = lax.broadcasted_iota(jnp.int32, (LANES, tq), 0)
        acc_sc[...] = jnp.zeros(acc_sc.shape, F32)

        def fold(x, op):
            out = x[0:8]
            for r in range(8, strip, 8):
                out = op(out, x[r:r + 8])
            return out

        def step(kj, state, masked):
            rows = pl.ds(pl.multiple_of(kj * tq, tq), tq)
            for hh in range(2):
                s_sc[hh] = _dot_nt(k_ref[rows, LANES * hh:LANES * (hh + 1)], q_ref[:, LANES * hh:LANES * (hh + 1)])
            vv = v_ref[rows, :]
            new = []
            for hh in range(2):
                m, l = state[hh]

                def strip_of(r0, hh=hh):
                    s = s_sc[hh, r0:r0 + strip, :]
                    return jnp.where(key_s + r0 <= qry_s, s, NEG) if masked else s

                mx = fold(strip_of(0), jnp.maximum)
                for r0 in range(strip, tq, strip):
                    mx = jnp.maximum(mx, fold(strip_of(r0), jnp.maximum))
                m_new = jnp.maximum(m, jnp.max(mx, axis=0, keepdims=True))
                alpha = jnp.exp(m - m_new)
                sm = jnp.zeros((8, tq), F32)
                for r0 in range(0, tq, strip):
                    p = jnp.exp(strip_of(r0) - m_new)
                    sm = sm + fold(p, jnp.add)
                    p_sc[hh, r0:r0 + strip, :] = p.astype(BF16)
                l = alpha * l + jnp.sum(sm, axis=0, keepdims=True)
                acc_sc[hh] = alpha * acc_sc[hh] + _dot_tn(vv, p_sc[hh])
                new.append((m_new, l))
            return tuple(new)

        one = (jnp.full((1, tq), NEG, F32), jnp.zeros((1, tq), F32))
        state = lax.fori_loop(0, qi, functools.partial(step, masked=False), (one, one))
        (m0, l0), (m1, l1) = step(qi, state, True)
        o_ref[...] = jnp.where(row_t < 64, acc_sc[0] / l0, acc_sc[1] / l1).T
        lse_ref[:, 0:LANES] = jnp.broadcast_to(m0 + jnp.log(l0), (LANES, tq)).T
        lse_ref[:, LANES:2 * LANES] = jnp.broadcast_to(m1 + jnp.log(l1), (LANES, tq)).T

        if ex:
            @pl.when((b == nb - 1) & (pr == PAIRS - 1) & (qi == nq - 1))
            def _():
                ex.wait(*ex_refs)

    res = pl.pallas_call(
        body, name=name, grid=(nb, PAIRS, nq),
        out_shape=(jax.ShapeDtypeStruct((T, 512), F32), jax.ShapeDtypeStruct((T, 1024), F32)) + tuple(ex.outs if ex else ()),
        in_specs=[pl.BlockSpec((tq, 2 * LANES), lambda b, p, i: (b * nq + i, p)),
                  pl.BlockSpec((seq, 2 * LANES), lambda b, p, i: (b, p)),
                  pl.BlockSpec((seq, LANES), lambda b, p, i: (b, p))] + [ANY] * n_in,
        out_specs=[pl.BlockSpec((tq, LANES), lambda b, p, i: (b * nq + i, p)),
                   pl.BlockSpec((tq, 2 * LANES), lambda b, p, i: (b * nq + i, p))] + [ANY] * n_out,
        scratch_shapes=[pltpu.VMEM((2, tq, tq), F32), pltpu.VMEM((2, tq, tq), BF16), pltpu.VMEM((2, LANES, tq), F32)]
        + (ex.sems() if ex else []),
        compiler_params=_params(("arbitrary", "arbitrary", "arbitrary")),
    )(q, k, v, *(ex.ins if ex else ()))
    return res[0], res[1], list(res[2:])


def _attn_bwd(q, k, v, o, do, lse, nb, seq, tq, name, key_bias, ex=None):
    T = q.shape[0]
    nq = seq // tq
    n_in, n_out = (len(ex.ins), len(ex.outs)) if ex else (0, 0)
    n_res = 4 if key_bias else 3

    def body(*refs):
        q_ref, k_ref, v_ref, o_ref, do_ref, lse_ref = refs[0:6]
        dq_ref, dk_ref, dv_ref = refs[6 + n_in:9 + n_in]
        dcb_ref = refs[9 + n_in] if key_bias else None
        first_scratch = 6 + n_in + n_res + n_out
        dsc, rsum, dq_acc = refs[first_scratch:first_scratch + 3]
        b, pr, step_no = pl.program_id(0), pl.program_id(1), pl.program_id(2)
        kj = nq - 1 - step_no
        if ex:
            ex_refs = (refs[6:6 + n_in], refs[6 + n_in + n_res:6 + n_in + n_res + n_out], refs[first_scratch + 3:])

            @pl.when((b == 0) & (pr == 0) & (step_no == 0))
            def _():
                ex.start(*ex_refs)

        lane_s = lax.broadcasted_iota(jnp.int32, (seq, LANES), 1)
        lane = lax.broadcasted_iota(jnp.int32, (tq, LANES), 1)
        rr = lax.broadcasted_iota(jnp.int32, (tq, tq), 0)
        cc = lax.broadcasted_iota(jnp.int32, (tq, tq), 1)

        @pl.when(step_no == 0)
        def _():
            dq_acc[...] = jnp.zeros_like(dq_acc)
            prod = do_ref[...].astype(F32) * o_ref[...]
            d0 = jnp.sum(jnp.where(lane_s < 64, prod, 0.0), axis=1, keepdims=True)
            d1 = jnp.sum(jnp.where(lane_s < 64, 0.0, prod), axis=1, keepdims=True)
            dsc[0] = jnp.broadcast_to(d0, (seq, LANES))
            dsc[1] = jnp.broadcast_to(d1, (seq, LANES))
            if key_bias:
                rsum[...] = jnp.zeros_like(rsum)

        if key_bias:
            @pl.when((pr == 0) & (step_no == 0))
            def _():
                dcb_ref[...] = jnp.zeros_like(dcb_ref)

        vv = v_ref[...]

        def step(qi, carry, masked):
            dkt, dvt, cols = carry
            rows = pl.ds(pl.multiple_of(qi * tq, tq), tq)
            dov = do_ref[rows, :]
            new_dkt, new_cols = [], []
            for hh in range(2):
                qv = q_ref[rows, LANES * hh:LANES * (hh + 1)]
                kv = k_ref[:, LANES * hh:LANES * (hh + 1)]
                dom = jnp.where((lane < 64) if hh == 0 else (lane >= 64), dov, jnp.zeros((), BF16))
                s = _dot_nt(qv, kv)
                if masked:
                    s = jnp.where(cc <= rr, s, NEG)
                p = jnp.exp(s - jnp.tile(lse_ref[rows, LANES * hh:LANES * (hh + 1)], (1, tq // LANES)))
                dp = _dot_nt(dom, vv)
                ds32 = p * (dp - jnp.tile(dsc[hh, rows, :], (1, tq // LANES)))
                col = cols[hh]
                if key_bias:
                    col = col + jnp.sum(ds32, axis=0, keepdims=True)
                    rsum[hh, rows, :] += jnp.broadcast_to(jnp.sum(ds32, axis=1, keepdims=True), (tq, LANES))
                ds = ds32.astype(BF16)
                dvt = dvt + _dot_tn(dom, p.astype(BF16))
                new_dkt.append(dkt[hh] + _dot_tn(qv, ds))
                new_cols.append(col)
                dq_acc[rows, LANES * hh:LANES * (hh + 1)] += _dot(ds, kv)
            return tuple(new_dkt), dvt, tuple(new_cols)

        zt = jnp.zeros((LANES, tq), F32)
        zc = jnp.zeros((1, tq), F32)
        carry = step(kj, ((zt, zt), zt, (zc, zc)), True)
        dkt, dvt, cols = lax.fori_loop(kj + 1, nq, functools.partial(step, masked=False), carry)
        for hh in range(2):
            dk_ref[:, LANES * hh:LANES * (hh + 1)] = dkt[hh].T.astype(dk_ref.dtype)
        dv_ref[...] = dvt.T.astype(dv_ref.dtype)
        if key_bias:
            row_t = lax.broadcasted_iota(jnp.int32, (LANES, tq), 0)
            per_key = jnp.where(row_t == 2 * pr, -cols[0], 0.0) + jnp.where(row_t == 2 * pr + 1, -cols[1], 0.0)
            dcb_ref[pl.ds(pl.multiple_of(kj * tq, tq), tq), :] += per_key.T

        @pl.when(step_no == nq - 1)
        def _():
            dq_ref[...] = dq_acc[...].astype(dq_ref.dtype)
            if key_bias:
                dcb_ref[...] += jnp.where(lane_s == 2 * pr, rsum[0], 0.0) + jnp.where(lane_s == 2 * pr + 1, rsum[1], 0.0)

        if ex:
            @pl.when((b == nb - 1) & (pr == PAIRS - 1) & (step_no == nq - 1))
            def _():
                ex.wait(*ex_refs)

    per_seq = lambda w: pl.BlockSpec((seq, w), lambda b, p, j: (b, p))
    per_blk = lambda w: pl.BlockSpec((tq, w), lambda b, p, j: (b * nq + nq - 1 - j, p))
    res = pl.pallas_call(
        body, name=name, grid=(nb, PAIRS, nq),
        out_shape=(jax.ShapeDtypeStruct((T, 1024), BF16), jax.ShapeDtypeStruct((T, 1024), BF16), jax.ShapeDtypeStruct((T, 512), BF16))
        + ((jax.ShapeDtypeStruct((T, LANES), F32),) if key_bias else ()) + tuple(ex.outs if ex else ()),
        in_specs=[per_seq(2 * LANES), per_blk(2 * LANES), per_blk(LANES), per_seq(LANES), per_seq(LANES), per_seq(2 * LANES)] + [ANY] * n_in,
        out_specs=[per_seq(2 * LANES), per_blk(2 * LANES), per_blk(LANES)]
        + ([pl.BlockSpec((seq, LANES), lambda b, p, j: (b, 0))] if key_bias else []) + [ANY] * n_out,
        scratch_shapes=[pltpu.VMEM((2, seq, LANES), F32), pltpu.VMEM((2, seq, LANES) if key_bias else (2, 8, LANES), F32),
                        pltpu.VMEM((seq, 2 * LANES), F32)]
        + (ex.sems() if ex else []),
        compiler_params=_params(("arbitrary", "arbitrary", "arbitrary")),
    )(q, k, v, o, do, lse, *(ex.ins if ex else ()))
    return list(res[:n_res]), list(res[n_res:])


def _mid(of, om, x, tgt, g_fo, g_mo, g2, g3, w_o, w_g, w_u, w_d, tm):
    T = x.shape[0]

    def body(of_ref, om_ref, x_ref, t_ref, gfo_ref, gmo_ref, g2_ref, g3_ref, wo_ref, wg_ref, wu_ref, wd_ref,
             a_ref, h2_ref, hid_ref, dg_ref, du_ref, dx3_ref, dx2_ref, dof_ref, dom_ref, st_ref):
        i = pl.program_id(0)

        @pl.when(i == 0)
        def _():
            st_ref[...] = jnp.zeros_like(st_ref)

        ofv, omv = of_ref[...], om_ref[...]
        rf, rm = _rms(ofv, FOX_W), _rms(omv, FOX_W)
        fhat, mhat = ofv * rf, omv * rm
        a = jnp.concatenate([fhat * gfo_ref[...], mhat * gmo_ref[...]], axis=1).astype(BF16)
        a_ref[...] = a
        x2 = x_ref[...] + _dot(a, wo_ref[...])
        r2 = _rms(x2, D_MODEL)
        xh2 = x2 * r2
        h2 = (xh2 * g2_ref[...]).astype(BF16)
        h2_ref[...] = h2
        gt = _dot_nt(h2, wg_ref[...])
        up = _dot_nt(h2, wu_ref[...])
        sg = jax.nn.sigmoid(gt)
        sl = gt * sg
        hid = (sl * up).astype(BF16)
        hid_ref[...] = hid
        x3 = x2 + _dot(hid, wd_ref[...])
        r3 = _rms(x3, D_MODEL)
        xh3 = x3 * r3
        diff = xh3 * g3_ref[...] - t_ref[...]
        dy = diff * (1.0 / D_MODEL)
        st_ref[ROW_LOSS:ROW_LOSS + 1, :] += jnp.sum(diff * diff, axis=0, keepdims=True) * (0.5 / D_MODEL)
        st_ref[ROW_FINAL:ROW_FINAL + 1, :] += jnp.sum(dy * xh3, axis=0, keepdims=True)
        dx3 = _rms_bwd(dy, xh3, r3, g3_ref[...], D_MODEL)
        dx3b = dx3.astype(BF16)
        dx3_ref[...] = dx3b
        dhid = _dot_nt(dx3b, wd_ref[...])
        dg = (dhid * up * (sg * (1.0 + gt * (1.0 - sg)))).astype(BF16)
        du = (dhid * sl).astype(BF16)
        dg_ref[...] = dg
        du_ref[...] = du
        dh2 = _dot(dg, wg_ref[...]) + _dot(du, wu_ref[...])
        st_ref[ROW_NORM_FFN:ROW_NORM_FFN + 1, :] += jnp.sum(dh2 * xh2, axis=0, keepdims=True)
        dx2 = dx3 + _rms_bwd(dh2, xh2, r2, g2_ref[...], D_MODEL)
        dx2_ref[...] = dx2
        da = _dot_nt(dx2.astype(BF16), wo_ref[...])
        daf, dam = da[:, 0:FOX_W], da[:, FOX_W:2 * FOX_W]
        st_ref[ROW_OUT:ROW_OUT + 1, 0:FOX_W] += jnp.sum(daf * fhat, axis=0, keepdims=True)
        st_ref[ROW_OUT:ROW_OUT + 1, FOX_W:2 * FOX_W] += jnp.sum(dam * mhat, axis=0, keepdims=True)
        dof_ref[...] = _rms_bwd(daf, fhat, rf, gfo_ref[...], FOX_W).astype(BF16)
        dom_ref[...] = _rms_bwd(dam, mhat, rm, gmo_ref[...], FOX_W).astype(BF16)

    row = lambda w: pl.BlockSpec((tm, w), lambda i: (i, 0))
    ff = jax.ShapeDtypeStruct((T, D_FF), BF16)
    out_shape = (
        jax.ShapeDtypeStruct((T, 1024), BF16), jax.ShapeDtypeStruct((T, 1024), BF16), ff, ff, ff,
        jax.ShapeDtypeStruct((T, 1024), BF16), jax.ShapeDtypeStruct((T, 1024), F32),
        jax.ShapeDtypeStruct((T, 512), BF16), jax.ShapeDtypeStruct((T, 512), BF16), jax.ShapeDtypeStruct((8, 1024), F32),
    )
    return pl.pallas_call(
        body, name="mid", grid=(T // tm,), out_shape=out_shape,
        in_specs=[row(512), row(512), row(1024), row(1024), _full(g_fo.shape), _full(g_mo.shape), _full(g2.shape), _full(g3.shape),
                  _full(w_o.shape), _full(w_g.shape), _full(w_u.shape), _full(w_d.shape)],
        out_specs=[row(1024), row(1024), row(D_FF), row(D_FF), row(D_FF), row(1024), row(1024), row(512), row(512),
                   pl.BlockSpec((8, 1024), lambda i: (0, 0))],
        compiler_params=_params(("arbitrary",)),
    )(of, om, x, tgt, g_fo, g_mo, g2, g3, w_o, w_g, w_u, w_d)


def _in_bwd(dqf, dkf, dvf, dcb, dqm, dkm, dvm, lat, qn, kvn, x, dx2, g1, gq, gkv, bfg, ct, st, w_in, w_q12, w_kv, seq, tm):
    T = x.shape[0]
    nblk = T // tm
    nsb = seq // tm

    def body(dqf_ref, dkf_ref, dvf_ref, dcb_ref, dqm_ref, dkm_ref, dvm_ref, lat_ref, qn_ref, kvn_ref, x_ref, dx2_ref, g1_ref, gq_ref,
             gkv_ref, b_ref, ct_ref, st_ref, win_ref, wq_ref, wkv_ref, dx_ref, dproj_ref, gq12_ref, gkv12_ref, stat_ref, carry,
             dq12_ref, dkv_ref):
        i = pl.program_id(0)

        @pl.when(i == 0)
        def _():
            stat_ref[...] = jnp.zeros_like(stat_ref)
            gq12_ref[...] = jnp.zeros_like(gq12_ref)
            gkv12_ref[...] = jnp.zeros_like(gkv12_ref)

        @pl.when(i % nsb == 0)
        def _():
            carry[...] = jnp.zeros_like(carry)

        lane = lax.broadcasted_iota(jnp.int32, (tm, LANES), 1)
        low = lane < 64
        ctv, stv = ct_ref[...], st_ref[...]

        for j in range(PAIRS):
            e, o = 2 * LANES * j, 2 * LANES * j + LANES
            half = lambda ref, c0: jnp.where(low, ref[:, c0:c0 + LANES].astype(F32), 0.0)
            dq = half(dqf_ref, e) + pltpu.roll(half(dqf_ref, o), 64, 1)
            dk = half(dkf_ref, e) + pltpu.roll(half(dkf_ref, o), 64, 1)
            dproj_ref[:, C_FQ + LANES * j:C_FQ + LANES * (j + 1)] = (dq * FOX_SCALE).astype(BF16)
            dproj_ref[:, C_FK + LANES * j:C_FK + LANES * (j + 1)] = dk.astype(BF16)
        dproj_ref[:, C_FV:C_QL] = dvf_ref[...]
        dc = dcb_ref[...]
        rr = lax.broadcasted_iota(jnp.int32, (tm, tm), 0)
        cc = lax.broadcasted_iota(jnp.int32, (tm, tm), 1)
        triu = (cc >= rr).astype(BF16)
        a0, a1, a2 = _split3(dc)
        dlf = _dot(triu, a0) + _dot(triu, a1) + _dot(triu, a2) + carry[0:1, :]
        carry[0:1, :] = dlf[0:1, :]
        misc_a = lat_ref[:, Q_RANK + KV_RANK:Q_RANK + KV_RANK + LANES]
        z = misc_a + b_ref[...]
        dz = jnp.where(lane < HEADS, dlf * jax.nn.sigmoid(-z), 0.0)
        stat_ref[ROW_B:ROW_B + 1, 0:LANES] += jnp.sum(dz, axis=0, keepdims=True)

        cq = (jnp.where(low, 1.0, 0.0) + ctv) * MLA_SCALE
        sq = stv * MLA_SCALE
        dkpe = jnp.zeros((tm, LANES), F32)
        for hd in range(HEADS):
            s0 = LANES * hd
            dqh = dqm_ref[:, s0:s0 + LANES].astype(F32)
            dq12_ref[:, s0:s0 + LANES] = (dqh * cq).astype(BF16)
            dq12_ref[:, 1024 + s0:1024 + s0 + LANES] = (dqh * sq).astype(BF16)
            dkpe = dkpe + dkm_ref[:, s0:s0 + LANES].astype(F32)
        dkv_ref[:, 0:1024] = dkm_ref[...]
        dkv_ref[:, 1024:1536] = dvm_ref[...]
        dproj_ref[:, C_MA:C_END] = (dz + dkpe * ctv + pltpu.roll(dkpe * stv, 32, 1)).astype(BF16)
        dqn = _dot(dq12_ref[...], wq_ref[...])
        dkvn = _dot_nt(dkv_ref[...], wkv_ref[...])
        gq12_ref[...] += _dot_tn(dq12_ref[...], qn_ref[...])
        gkv12_ref[...] += _dot_tn(kvn_ref[...], dkv_ref[...])
        ql = lat_ref[:, 0:Q_RANK]
        kvl = lat_ref[:, Q_RANK:Q_RANK + KV_RANK]
        rq, rkv = _rms(ql, Q_RANK), _rms(kvl, KV_RANK)
        qhat, kvhat = ql * rq, kvl * rkv
        stat_ref[ROW_Q:ROW_Q + 1, 0:Q_RANK] += jnp.sum(dqn * qhat, axis=0, keepdims=True)
        stat_ref[ROW_KV:ROW_KV + 1, 0:KV_RANK] += jnp.sum(dkvn * kvhat, axis=0, keepdims=True)
        dproj_ref[:, C_QL:C_KVL] = _rms_bwd(dqn, qhat, rq, gq_ref[...], Q_RANK).astype(BF16)
        dproj_ref[:, C_KVL:C_MA] = _rms_bwd(dkvn, kvhat, rkv, gkv_ref[...], KV_RANK).astype(BF16)

        dh1 = _dot(dproj_ref[...], win_ref[...])
        xv = x_ref[...]
        r1 = _rms(xv, D_MODEL)
        xh = xv * r1
        stat_ref[ROW_NORM_MIX:ROW_NORM_MIX + 1, :] += jnp.sum(dh1 * xh, axis=0, keepdims=True)
        dx_ref[...] = dx2_ref[...] + _rms_bwd(dh1, xh, r1, g1_ref[...], D_MODEL)

    rev = lambda w: pl.BlockSpec((tm, w), lambda i: (nblk - 1 - i, 0))
    whole = lambda r, c: pl.BlockSpec((r, c), lambda i: (0, 0))
    out_shape = (
        jax.ShapeDtypeStruct((T, 1024), F32), jax.ShapeDtypeStruct((T, C_END), BF16), jax.ShapeDtypeStruct((2048, Q_RANK), F32),
        jax.ShapeDtypeStruct((KV_RANK, 1536), F32), jax.ShapeDtypeStruct((8, 1024), F32),
    )
    return pl.pallas_call(
        body, name="in_bwd", grid=(nblk,), out_shape=out_shape,
        in_specs=[rev(1024), rev(1024), rev(512), rev(LANES), rev(1024), rev(1024), rev(512), rev(512), rev(Q_RANK), rev(KV_RANK),
                  rev(1024), rev(1024), _full(g1.shape), _full(gq.shape), _full(gkv.shape), _full(bfg.shape), rev(LANES), rev(LANES),
                  _full(w_in.shape), _full(w_q12.shape), _full(w_kv.shape)],
        out_specs=[rev(1024), rev(C_END), whole(2048, Q_RANK), whole(KV_RANK, 1536), whole(8, 1024)],
        scratch_shapes=[pltpu.VMEM((8, LANES), F32), pltpu.VMEM((tm, 2048), BF16), pltpu.VMEM((tm, 1536), BF16)],
        compiler_params=_params(("arbitrary",)),
    )(dqf, dkf, dvf, dcb, dqm, dkm, dvm, lat, qn, kvn, x, dx2, g1, gq, gkv, bfg, ct, st, w_in, w_q12, w_kv)


def _wgrad(a, b, tk, tt, name, ex=None):
    T, K = a.shape
    N = b.shape[1]
    n_in, n_out = (len(ex.ins), len(ex.outs)) if ex else (0, 0)
    gk, gt = K // tk, T // tt

    def body(*refs):
        a_ref, b_ref, o_ref = refs[0], refs[1], refs[2 + n_in]
        kb, t = pl.program_id(0), pl.program_id(1)
        if ex:
            ex_refs = (refs[2:2 + n_in], refs[3 + n_in:3 + n_in + n_out], refs[3 + n_in + n_out:])

            @pl.when((kb == 0) & (t == 0))
            def _():
                ex.start(*ex_refs)

        @pl.when(t == 0)
        def _():
            o_ref[...] = jnp.zeros_like(o_ref)

        o_ref[...] += _dot_tn(a_ref[...].astype(BF16), b_ref[...].astype(BF16))

        if ex:
            @pl.when((kb == gk - 1) & (t == gt - 1))
            def _():
                ex.wait(*ex_refs)

    res = pl.pallas_call(
        body, name=name, grid=(gk, gt), out_shape=(jax.ShapeDtypeStruct((K, N), F32),) + tuple(ex.outs if ex else ()),
        in_specs=[pl.BlockSpec((tt, tk), lambda kb, t: (t, kb)), pl.BlockSpec((tt, N), lambda kb, t: (t, 0))] + [ANY] * n_in,
        out_specs=[pl.BlockSpec((tk, N), lambda kb, t: (kb, 0))] + [ANY] * n_out,
        scratch_shapes=ex.sems() if ex else [], input_output_aliases=ex.aliases(2, 1) if ex else {},
        compiler_params=_params(("arbitrary", "arbitrary")),
    )(a, b, *(ex.ins if ex else ()))
    return (res[0], list(res[1:])) if ex else res[0]


STREAM_STEPS = 4


def _adam_update(w, g, m, v):
    nm = ADAM_B1 * m + (1.0 - ADAM_B1) * g
    nv = ADAM_B2 * v + (1.0 - ADAM_B2) * (g * g)
    m_hat = nm / (1.0 - ADAM_B1 ** ADAM_STEP)
    v_hat = nv / (1.0 - ADAM_B2 ** ADAM_STEP)
    return -ADAM_LR * (m_hat / (jnp.sqrt(v_hat) + ADAM_EPS) + ADAM_WD * w), nm, nv


def _adamw_small(stats, params):
    k = len(SMALL)

    def body(*refs):
        for t, name in enumerate(SMALL):
            row, c0, width = SMALL_AT[name]
            w_ref, m_ref, v_ref = refs[1 + 3 * t:4 + 3 * t]
            g_ref, d_ref, nm_ref, nv_ref = refs[1 + 3 * k + 4 * t:5 + 3 * k + 4 * t]
            g = refs[0][row:row + 1, c0:c0 + width]
            g_ref[...] = g
            d_ref[...], nm_ref[...], nv_ref[...] = _adam_update(w_ref[...], g, m_ref[...], v_ref[...])

    vm = pl.BlockSpec(memory_space=pltpu.VMEM)
    out_shape = tuple(jax.ShapeDtypeStruct((1, SMALL_AT[name][2]), F32) for name in SMALL for _ in range(4))
    res = pl.pallas_call(body, name="adamw_small", out_shape=out_shape, in_specs=[vm] * (1 + 3 * k), out_specs=tuple([vm] * (4 * k)))(
        stats, *[a for name in SMALL for a in params[name]])
    return {name: tuple(res[4 * t:4 * t + 4]) for t, name in enumerate(SMALL)}


def _adamw(tensors, name):
    n = len(tensors)
    R, C = tensors[0][0].shape
    tr = max([d for d in range(8, R // STREAM_STEPS + 1, 8) if R % d == 0], default=R)

    def body(*refs):
        for t in range(n):
            w_ref, g_ref, m_ref, v_ref = refs[4 * t:4 * t + 4]
            d_ref, nm_ref, nv_ref = refs[4 * n + 3 * t:4 * n + 3 * t + 3]
            d_ref[...], nm_ref[...], nv_ref[...] = _adam_update(w_ref[...], g_ref[...], m_ref[...], v_ref[...])

    blk = pl.BlockSpec((tr, C), lambda i: (i, 0))
    sh = jax.ShapeDtypeStruct((R, C), F32)
    res = pl.pallas_call(
        body, name=name, grid=(R // tr,), out_shape=(sh,) * (3 * n),
        in_specs=[blk] * (4 * n), out_specs=[blk] * (3 * n),
        compiler_params=_params(("arbitrary",)),
    )(*[a for t in tensors for a in t])
    return [tuple(res[3 * t:3 * t + 3]) for t in range(n)]


def _arrange(win_t, wuq_t, wukv):
    dt = win_t.dtype
    z = lambda r: jnp.zeros((r, D_MODEL), dt)
    zh = lambda r: jnp.zeros((HEADS, r, Q_RANK), dt)
    kr1, kr2 = win_t[1928:1944], win_t[1944:1960]
    misc = jnp.concatenate([win_t[1536:1544], z(56), kr1, kr2, kr2, kr1], axis=0)
    w_in = jnp.concatenate([win_t[0:1536], win_t[1544:1928], misc], axis=0)
    wq = wuq_t.reshape(HEADS, 96, Q_RANK)
    q1 = jnp.concatenate([wq, zh(32)], axis=1).reshape(1024, Q_RANK)
    q2 = jnp.concatenate([zh(64), wq[:, 80:96], wq[:, 64:80], zh(32)], axis=1).reshape(1024, Q_RANK)
    wkv = wukv.reshape(KV_RANK, HEADS, 128)
    wk = jnp.concatenate([wkv[:, :, 0:64], jnp.zeros((KV_RANK, HEADS, 64), dt)], axis=2).reshape(KV_RANK, 1024)
    wv = wkv[:, :, 64:128].reshape(KV_RANK, 512)
    return dict(w_in=w_in, w_q12=jnp.concatenate([q1, q2], axis=0), w_k=wk, w_v=wv, w_kv=jnp.concatenate([wk, wv], axis=1))


def _unarrange(g_in, g_q12, g_kv):
    kr1 = g_in[C_MA + 64:C_MA + 80] + g_in[C_MA + 112:C_MA + 128]
    kr2 = g_in[C_MA + 80:C_MA + 96] + g_in[C_MA + 96:C_MA + 112]
    win_t = jnp.concatenate([g_in[0:1536], g_in[C_MA:C_MA + 8], g_in[1536:1920], kr1, kr2], axis=0)
    g1 = g_q12[0:1024].reshape(HEADS, 128, Q_RANK)
    g2 = g_q12[1024:2048].reshape(HEADS, 128, Q_RANK)
    wuq_t = jnp.concatenate([g1[:, 0:64], g1[:, 64:80] + g2[:, 80:96], g1[:, 80:96] + g2[:, 64:80]], axis=1).reshape(768, Q_RANK)
    gk = g_kv[:, 0:1024].reshape(KV_RANK, HEADS, 128)
    gv = g_kv[:, 1024:1536].reshape(KV_RANK, HEADS, 64)
    wukv = jnp.concatenate([gk[:, :, 0:64], gv], axis=2).reshape(KV_RANK, 1024)
    return win_t, wuq_t, wukv


def _selector():
    sel = np.zeros((384, 1024), np.float32)
    for h in range(HEADS):
        for piece in range(3):
            sel[LANES * piece + h, LANES * h + 64 + piece] = 1.0
    return jnp.asarray(sel, BF16)


def _rope_tables(positions):
    inv_freq = 10000.0 ** (-jnp.arange(0, ROPE, 2, dtype=F32) / ROPE)
    n = positions.size
    ang = (positions.reshape(n // 8, 8, 1).astype(F32) * inv_freq[None, None, :]).reshape(n // 8, 8 * (ROPE // 2))
    cos, sin = lax.optimization_barrier((jnp.cos(lax.optimization_barrier(ang)), jnp.sin(lax.optimization_barrier(ang))))
    cos, sin = cos.reshape(n, ROPE // 2), sin.reshape(n, ROPE // 2)
    z64, z32 = jnp.zeros((n, 64), F32), jnp.zeros((n, 32), F32)
    return jnp.concatenate([z64, cos, cos, z32], axis=1), jnp.concatenate([z64, -sin, sin, z32], axis=1)


def _work(name, t):
    return jnp.swapaxes(t[0], 0, 1) if name in TRANSPOSED else t[0]


def _back(name, t):
    return (jnp.swapaxes(t, 0, 1) if name in TRANSPOSED else t)[None]


def kernel(x, positions, norm_mix_g, w_in, b_fgate, q_norm_g, w_uq, kv_norm_g, w_ukv, fox_out_g, mla_out_g, w_o, norm_ffn_g, w_gate, w_up, w_down, final_norm_g, loss_target, m_norm_mix_g, m_w_in, m_b_fgate, m_q_norm_g, m_w_uq, m_kv_norm_g, m_w_ukv, m_fox_out_g, m_mla_out_g, m_w_o, m_norm_ffn_g, m_w_gate, m_w_up, m_w_down, m_final_norm_g, v_norm_mix_g, v_w_in, v_b_fgate, v_q_norm_g, v_w_uq, v_kv_norm_g, v_w_ukv, v_fox_out_g, v_mla_out_g, v_w_o, v_norm_ffn_g, v_w_gate, v_w_up, v_w_down, v_final_norm_g):
    names = ["norm_mix_g", "w_in", "b_fgate", "q_norm_g", "w_uq", "kv_norm_g", "w_ukv", "fox_out_g", "mla_out_g", "w_o",
             "norm_ffn_g", "w_gate", "w_up", "w_down", "final_norm_g"]
    wts = dict(zip(names, [norm_mix_g, w_in, b_fgate, q_norm_g, w_uq, kv_norm_g, w_ukv, fox_out_g, mla_out_g, w_o, norm_ffn_g,
                           w_gate, w_up, w_down, final_norm_g]))
    mom = dict(zip(names, [m_norm_mix_g, m_w_in, m_b_fgate, m_q_norm_g, m_w_uq, m_kv_norm_g, m_w_ukv, m_fox_out_g, m_mla_out_g,
                           m_w_o, m_norm_ffn_g, m_w_gate, m_w_up, m_w_down, m_final_norm_g]))
    var = dict(zip(names, [v_norm_mix_g, v_w_in, v_b_fgate, v_q_norm_g, v_w_uq, v_kv_norm_g, v_w_ukv, v_fox_out_g, v_mla_out_g,
                           v_w_o, v_norm_ffn_g, v_w_gate, v_w_up, v_w_down, v_final_norm_g]))
    shard = {n: _work(n, wts[n]) for n in HEAD3 + FFN4}
    nb, seq, _ = x.shape
    T = nb * seq
    tm, tq = min(ROW_TILE, seq), min(ATTN_TILE, seq)
    tt = min(WGRAD_TILE, T)
    xf = x.reshape(T, D_MODEL)
    tgt = loss_target.reshape(T, D_MODEL)
    chip = 2 * lax.axis_index("x") + lax.axis_index("y")

    mine = [shard[n].astype(BF16) for n in HEAD3]
    head = _run_exchange(_gather_split_exchange(mine), "gather_head")
    win4, wuq4, wukv4 = [lax.dynamic_update_slice(h, s[None], (chip, 0, 0)) for h, s in zip(head, mine)]
    a = _arrange(win4.reshape(-1, D_MODEL), wuq4.reshape(-1, Q_RANK), wukv4.transpose(1, 0, 2).reshape(KV_RANK, -1))
    sel = _selector()
    ct, st = _rope_tables(positions)
    bfg = jnp.concatenate([b_fgate, jnp.zeros((1, LANES - HEADS), F32)], axis=1)
    g1, gq, gkv = norm_mix_g, q_norm_g, kv_norm_g

    h1, qf, kf, vf, qm, km, vm, lat, qn, kvn = _in_proj(xf, g1, a["w_in"], a["w_q12"], a["w_k"], a["w_v"], gq, gkv, bfg, ct, st, sel, seq,
                                                        min(IN_PROJ_TILE, seq))
    tqf = min(ATTN_FWD_TILE, seq)
    of, lse_f, (wo4, wg4) = _attn_fwd(qf, kf, vf, nb, seq, tqf, "fox_fwd", _gather_exchange([shard[n].astype(BF16) for n in FFN4[:2]]))
    om, lse_m, (wu4, wd4) = _attn_fwd(qm, km, vm, nb, seq, tqf, "mla_fwd", _gather_exchange([shard[n].astype(BF16) for n in FFN4[2:]]))
    a_cat, h2, hid, dg, du, dx3, dx2, dof, dom, st_mid = _mid(
        of, om, xf, tgt, fox_out_g, mla_out_g, norm_ffn_g, final_norm_g.reshape(1, D_MODEL),
        wo4.reshape(D_MODEL, D_MODEL), wg4.reshape(D_FF, D_MODEL), wu4.reshape(D_FF, D_MODEL), wd4.reshape(D_FF, D_MODEL), tm)

    slab = lambda g: g.reshape(N_CHIPS, g.shape[0] // N_CHIPS, g.shape[1])
    big = [slab(_wgrad(a_cat, dx2, D_MODEL, tt, "wgrad_o")), slab(_wgrad(dg, h2, D_FF // 2, tt, "wgrad_gate")),
           slab(_wgrad(du, h2, D_FF // 2, tt, "wgrad_up")), slab(_wgrad(hid, dx3, D_FF // 2, tt, "wgrad_down"))]
    (dqf, dkf, dvf, dcb), got = _attn_bwd(qf, kf, vf, of, dof, lse_f, nb, seq, tq, "fox_bwd", True, _swap_exchange(big))
    sums = _add_half(big, got)
    (dqm, dkm, dvm), recv = _attn_bwd(qm, km, vm, om, dom, lse_m, nb, seq, tq, "mla_bwd", False, _scatter_exchange(sums))
    halves = _sum_slabs(big, got, recv)
    dx, dproj, g_q12, g_kv, st_in = _in_bwd(dqf, dkf, dvf, dcb, dqm, dkm, dvm, lat, qn, kvn, xf, dx2, g1, gq, gkv, bfg, ct, st,
                                            a["w_in"], a["w_q12"], a["w_kv"], seq, tm)
    g_in, results = _wgrad(dproj, h1, C_END, tt, "wgrad_in", _both(_join_exchange(halves), _everyone_exchange(st_mid + st_in)))
    gshard = dict(zip(FFN4, results[:4]))
    stats = _sum_devices(results[4])

    gwin_t, gwuq_t, gwukv = _unarrange(g_in, g_q12, g_kv)
    tail = [slab(gwin_t), slab(gwuq_t), gwukv.reshape(KV_RANK, N_CHIPS, -1).transpose(1, 0, 2)]
    tail_got = _run_exchange(_swap_exchange(tail), "tail_swap")
    tail_sums = _add_half(tail, tail_got)
    tail_recv = _run_exchange(_scatter_exchange(tail_sums), "tail_scatter")
    tail_joined = _run_exchange(_join_exchange(_sum_slabs(tail, tail_got, tail_recv)), "tail_join")
    gshard.update(zip(HEAD3, tail_joined))
    quad = lambda n: (shard[n], gshard[n], _work(n, mom[n]), _work(n, var[n]))
    updates = dict(zip(FFN4[1:], _adamw([quad(n) for n in FFN4[1:]], "adamw_ffn")))
    for n in HEAD3 + FFN4[:1]:
        updates[n], = _adamw([quad(n)], "adamw_" + n)

    grads, delta, new_m, new_v = {}, {}, {}, {}
    for n in HEAD3 + FFN4:
        grads[n] = _back(n, gshard[n])
        delta[n], new_m[n], new_v[n] = [_back(n, t) for t in updates[n]]
    row = lambda t: t.reshape(1, -1)
    small = _adamw_small(stats, {n: (row(wts[n]), row(mom[n]), row(var[n])) for n in SMALL})
    for n in SMALL:
        grads[n], delta[n], new_m[n], new_v[n] = [t.reshape(wts[n].shape) for t in small[n]]
    loss = jnp.sum(stats[ROW_LOSS])
    return (loss, dx.reshape(x.shape), *[grads[n] for n in names], *[delta[n] for n in names],
            *[new_m[n] for n in names], *[new_v[n] for n in names])
```
